```python
import jax, jax.numpy as jnp
from jax import lax
import numpy as np

D_MODEL = 1024
BATCH = 16
SEQ = 4096
DEPTH = 2

HEAD_DIM = 64
ATTN_WIDTH = D_MODEL // 2
N_Q_HEADS = ATTN_WIDTH // HEAD_DIM
N_KV_HEADS = max(N_Q_HEADS // 4, 1)
Q_PER_KV = N_Q_HEADS // N_KV_HEADS
KV_WIDTH = N_KV_HEADS * HEAD_DIM
WINDOW = 128
BLOCK = 128

CHUNK = 128
SGU_GROUP_DIM = 128
SGU_WIDTH = D_MODEL // 2
SGU_GROUPS = SGU_WIDTH // SGU_GROUP_DIM

ALPHA = (2.0 * DEPTH) ** 0.25
BETA = (8.0 * DEPTH) ** -0.25
LN_EPS = 1e-5

SPLITS = (ATTN_WIDTH, KV_WIDTH, KV_WIDTH, ATTN_WIDTH,
          SGU_WIDTH, SGU_WIDTH, SGU_WIDTH, D_MODEL, D_MODEL)
IN_COLS = sum(SPLITS)

kernel_name = "hybrid_swa_sink_sgu_gated_deepnorm"


def _layernorm(x, g, b):
    xf = x.astype(jnp.float32)
    mu = jnp.mean(xf, axis=-1, keepdims=True)
    var = jnp.mean(jnp.square(xf - mu), axis=-1, keepdims=True)
    y = (xf - mu) * lax.rsqrt(var + LN_EPS) * g.astype(jnp.float32) + b.astype(jnp.float32)
    return y.astype(x.dtype)


def _split_cols(h):
    parts, start = [], 0
    for w in SPLITS:
        parts.append(h[..., start:start + w])
        start += w
    return parts


def _swa_sinks(q, k, v, sinks):
    B, S = q.shape[0], q.shape[1]
    nb = S // BLOCK
    qb = q.reshape(B, nb, BLOCK, N_KV_HEADS, Q_PER_KV, HEAD_DIM)
    kb = k.reshape(B, nb, BLOCK, N_KV_HEADS, HEAD_DIM)
    vb = v.reshape(B, nb, BLOCK, N_KV_HEADS, HEAD_DIM)
    zpad = jnp.zeros_like(kb[:, :1])
    kk = jnp.concatenate([jnp.concatenate([zpad, kb[:, :-1]], axis=1), kb], axis=2)
    vv = jnp.concatenate([jnp.concatenate([zpad, vb[:, :-1]], axis=1), vb], axis=2)
    scale = HEAD_DIM ** -0.5
    scores = jnp.einsum('bnqhgd,bnkhd->bnhgqk', qb, kk).astype(jnp.float32) * scale
    qpos = jnp.arange(BLOCK)[:, None] + BLOCK
    kpos = jnp.arange(2 * BLOCK)[None, :]
    band = (kpos <= qpos) & (kpos > qpos - WINDOW)
    blk = jnp.arange(nb)[:, None, None]
    valid = band[None] & ((blk > 0) | (kpos[None] >= BLOCK))
    neg = jnp.finfo(jnp.float32).min
    scores = jnp.where(valid[None, :, None, None], scores, neg)
    sink = sinks.astype(jnp.float32).reshape(N_KV_HEADS, Q_PER_KV)[None, None, :, :, None, None]
    m = jnp.maximum(jnp.max(scores, axis=-1, keepdims=True), sink)
    p = jnp.exp(scores - m)
    denom = jnp.sum(p, axis=-1, keepdims=True) + jnp.exp(sink - m)
    probs = (p / denom).astype(vv.dtype)
    out = jnp.einsum('bnhgqk,bnkhd->bnqhgd', probs, vv)
    return out.reshape(B, S, ATTN_WIDTH)


def _chunked_sgu(u, v, vn_g, vn_b, w_s, b_s):
    B, S = v.shape[0], v.shape[1]
    nc = S // CHUNK
    v = _layernorm(v, vn_g, vn_b)
    vc = v.reshape(B, nc, CHUNK, SGU_GROUPS, SGU_GROUP_DIM)
    tril = jnp.tril(jnp.ones((CHUNK, CHUNK), dtype=w_s.dtype))
    w = w_s * tril[None]
    mixed = jnp.einsum('gts,bcsgd->bctgd', w, vc) + b_s.T[None, None, :, :, None]
    return u * mixed.reshape(B, S, SGU_WIDTH)


def _fwd_setup_inputs(seed: int = 0) -> dict:
    key = jax.random.key(seed)
    ks = jax.random.split(key, 16)
    L, D = DEPTH, D_MODEL
    x = jax.random.normal(ks[0], (BATCH, SEQ, D), jnp.float32)
    ln_in_g = 1.0 + 0.05 * jax.random.normal(ks[1], (D,), jnp.float32)
    ln_in_b = 0.02 * jax.random.normal(ks[2], (D,), jnp.float32)
    col_scale = jnp.concatenate([
        jnp.ones((ATTN_WIDTH + KV_WIDTH,), jnp.float32),
        jnp.full((KV_WIDTH,), BETA, jnp.float32),
        jnp.ones((ATTN_WIDTH,), jnp.float32),
        jnp.full((SGU_WIDTH,), BETA, jnp.float32),
        jnp.ones((2 * SGU_WIDTH + 2 * D,), jnp.float32)])
    w_in = jax.random.normal(ks[3], (L, D, IN_COLS), jnp.float32) * (D ** -0.5) * col_scale
    b_in = 0.02 * jax.random.normal(ks[4], (L, IN_COLS), jnp.float32)
    sinks = 0.5 * jax.random.normal(ks[5], (L, N_Q_HEADS), jnp.float32)
    vn_g = 1.0 + 0.05 * jax.random.normal(ks[6], (L, SGU_WIDTH), jnp.float32)
    vn_b = 0.02 * jax.random.normal(ks[7], (L, SGU_WIDTH), jnp.float32)
    w_s = jax.random.normal(ks[8], (L, SGU_GROUPS, CHUNK, CHUNK), jnp.float32) * (CHUNK ** -0.5)
    b_s = 1.0 + 0.1 * jax.random.normal(ks[9], (L, SGU_GROUPS, CHUNK), jnp.float32)
    p_a = jax.random.normal(ks[10], (L, ATTN_WIDTH, D), jnp.float32) * (ATTN_WIDTH ** -0.5) * BETA
    p_b = jax.random.normal(ks[11], (L, SGU_WIDTH, D), jnp.float32) * (SGU_WIDTH ** -0.5) * BETA
    w_out = jax.random.normal(ks[12], (L, D, D), jnp.float32) * (D ** -0.5) * BETA
    b_out = 0.02 * jax.random.normal(ks[13], (L, D), jnp.float32)
    ln_g = 1.0 + 0.05 * jax.random.normal(ks[14], (L, D), jnp.float32)
    ln_b = 0.02 * jax.random.normal(ks[15], (L, D), jnp.float32)
    return {"x": x, "ln_in_g": ln_in_g, "ln_in_b": ln_in_b, "w_in": w_in, "b_in": b_in,
            "sinks": sinks, "vn_g": vn_g, "vn_b": vn_b, "w_s": w_s, "b_s": b_s,
            "p_a": p_a, "p_b": p_b, "w_out": w_out, "b_out": b_out,
            "ln_g": ln_g, "ln_b": ln_b}


def _fwd_reference(x, ln_in_g, ln_in_b, w_in, b_in, sinks, vn_g, vn_b, w_s, b_s,
              p_a, p_b, w_out, b_out, ln_g, ln_b):
    x = _layernorm(x, ln_in_g, ln_in_b)
    for l in range(DEPTH):
        h = jnp.einsum('bsd,dc->bsc', x, w_in[l]) + b_in[l]
        q, k, v, g_a, u_b, v_b, g_b, r_a, r_b = _split_cols(h)
        y_a = _swa_sinks(q, k, v, sinks[l]) * jax.nn.silu(g_a)
        y_b = _chunked_sgu(jax.nn.gelu(u_b), jax.nn.gelu(v_b), vn_g[l], vn_b[l],
                           w_s[l], b_s[l]) * jax.nn.silu(g_b)
        merged = (jax.nn.sigmoid(r_a) * jnp.einsum('bsc,cd->bsd', y_a, p_a[l])
                  + jax.nn.sigmoid(r_b) * jnp.einsum('bsc,cd->bsd', y_b, p_b[l]))
        out = jnp.einsum('bsd,de->bse', merged, w_out[l]) + b_out[l]
        x = _layernorm(ALPHA * x + out, ln_g[l], ln_b[l])
    return x


import jax as _jax
import jax.numpy as _jnp

TWIN_FORMAT = 'train_step'
FWD_PARAMS = ['x', 'ln_in_g', 'ln_in_b', 'w_in', 'b_in', 'sinks', 'vn_g', 'vn_b', 'w_s', 'b_s', 'p_a', 'p_b', 'w_out', 'b_out', 'ln_g', 'ln_b']
TWIN_WEIGHTS = ['ln_in_g', 'ln_in_b', 'w_in', 'b_in', 'sinks', 'vn_g', 'vn_b', 'w_s', 'b_s', 'p_a', 'p_b', 'w_out', 'b_out', 'ln_g', 'ln_b']
TWIN_DIFF_INPUT = 'x'
TWIN_INPUTS = ['x', 'ln_in_g', 'ln_in_b', 'w_in', 'b_in', 'sinks', 'vn_g', 'vn_b', 'w_s', 'b_s', 'p_a', 'p_b', 'w_out', 'b_out', 'ln_g', 'ln_b', 'loss_target', 'm_ln_in_g', 'm_ln_in_b', 'm_w_in', 'm_b_in', 'm_sinks', 'm_vn_g', 'm_vn_b', 'm_w_s', 'm_b_s', 'm_p_a', 'm_p_b', 'm_w_out', 'm_b_out', 'm_ln_g', 'm_ln_b', 'v_ln_in_g', 'v_ln_in_b', 'v_w_in', 'v_b_in', 'v_sinks', 'v_vn_g', 'v_vn_b', 'v_w_s', 'v_b_s', 'v_p_a', 'v_p_b', 'v_w_out', 'v_b_out', 'v_ln_g', 'v_ln_b']
TWIN_OUTPUTS = ['loss', 'grad_x', 'grad_ln_in_g', 'grad_ln_in_b', 'grad_w_in', 'grad_b_in', 'grad_sinks', 'grad_vn_g', 'grad_vn_b', 'grad_w_s', 'grad_b_s', 'grad_p_a', 'grad_p_b', 'grad_w_out', 'grad_b_out', 'grad_ln_g', 'grad_ln_b', 'delta_ln_in_g', 'delta_ln_in_b', 'delta_w_in', 'delta_b_in', 'delta_sinks', 'delta_vn_g', 'delta_vn_b', 'delta_w_s', 'delta_b_s', 'delta_p_a', 'delta_p_b', 'delta_w_out', 'delta_b_out', 'delta_ln_g', 'delta_ln_b', 'new_m_ln_in_g', 'new_m_ln_in_b', 'new_m_w_in', 'new_m_b_in', 'new_m_sinks', 'new_m_vn_g', 'new_m_vn_b', 'new_m_w_s', 'new_m_b_s', 'new_m_p_a', 'new_m_p_b', 'new_m_w_out', 'new_m_b_out', 'new_m_ln_g', 'new_m_ln_b', 'new_v_ln_in_g', 'new_v_ln_in_b', 'new_v_w_in', 'new_v_b_in', 'new_v_sinks', 'new_v_vn_g', 'new_v_vn_b', 'new_v_w_s', 'new_v_b_s', 'new_v_p_a', 'new_v_p_b', 'new_v_w_out', 'new_v_b_out', 'new_v_ln_g', 'new_v_ln_b']
TWIN_LEAF_KINDS = {'loss': 'loss', 'grad_x': 'grad_x', 'grad_ln_in_g': 'grad_w', 'grad_ln_in_b': 'grad_w', 'grad_w_in': 'grad_w', 'grad_b_in': 'grad_w', 'grad_sinks': 'grad_w', 'grad_vn_g': 'grad_w', 'grad_vn_b': 'grad_w', 'grad_w_s': 'grad_w', 'grad_b_s': 'grad_w', 'grad_p_a': 'grad_w', 'grad_p_b': 'grad_w', 'grad_w_out': 'grad_w', 'grad_b_out': 'grad_w', 'grad_ln_g': 'grad_w', 'grad_ln_b': 'grad_w', 'delta_ln_in_g': 'delta_w', 'delta_ln_in_b': 'delta_w', 'delta_w_in': 'delta_w', 'delta_b_in': 'delta_w', 'delta_sinks': 'delta_w', 'delta_vn_g': 'delta_w', 'delta_vn_b': 'delta_w', 'delta_w_s': 'delta_w', 'delta_b_s': 'delta_w', 'delta_p_a': 'delta_w', 'delta_p_b': 'delta_w', 'delta_w_out': 'delta_w', 'delta_b_out': 'delta_w', 'delta_ln_g': 'delta_w', 'delta_ln_b': 'delta_w', 'new_m_ln_in_g': 'new_m', 'new_m_ln_in_b': 'new_m', 'new_m_w_in': 'new_m', 'new_m_b_in': 'new_m', 'new_m_sinks': 'new_m', 'new_m_vn_g': 'new_m', 'new_m_vn_b': 'new_m', 'new_m_w_s': 'new_m', 'new_m_b_s': 'new_m', 'new_m_p_a': 'new_m', 'new_m_p_b': 'new_m', 'new_m_w_out': 'new_m', 'new_m_b_out': 'new_m', 'new_m_ln_g': 'new_m', 'new_m_ln_b': 'new_m', 'new_v_ln_in_g': 'new_v', 'new_v_ln_in_b': 'new_v', 'new_v_w_in': 'new_v', 'new_v_b_in': 'new_v', 'new_v_sinks': 'new_v', 'new_v_vn_g': 'new_v', 'new_v_vn_b': 'new_v', 'new_v_w_s': 'new_v', 'new_v_b_s': 'new_v', 'new_v_p_a': 'new_v', 'new_v_p_b': 'new_v', 'new_v_w_out': 'new_v', 'new_v_b_out': 'new_v', 'new_v_ln_g': 'new_v', 'new_v_ln_b': 'new_v'}


def _forward(args):
    return _fwd_reference(*[args[k] for k in FWD_PARAMS])


def _output_shape():
    out = _jax.eval_shape(lambda: _forward(_fwd_setup_inputs(0)))
    return out.shape, out.dtype

N_MICROBATCH = 1
ADAM_LR = 0.001
ADAM_B1 = 0.9
ADAM_B2 = 0.999
ADAM_EPS = 1e-08
ADAM_WD = 0.01
ADAM_STEP = 10
PER_EXAMPLE_BATCH_AXIS = {'x': 0, 'loss_target': 0}
SHARED_INPUTS = []
_WEIGHT_DTYPES = {'ln_in_g': _jnp.float32, 'ln_in_b': _jnp.float32, 'w_in': _jnp.float32, 'b_in': _jnp.float32, 'sinks': _jnp.float32, 'vn_g': _jnp.float32, 'vn_b': _jnp.float32, 'w_s': _jnp.float32, 'b_s': _jnp.float32, 'p_a': _jnp.float32, 'p_b': _jnp.float32, 'w_out': _jnp.float32, 'b_out': _jnp.float32, 'ln_g': _jnp.float32, 'ln_b': _jnp.float32}
MOMENT_SCALE = {'ln_in_g': 6.132961e+00, 'ln_in_b': 7.901164e-01, 'w_in': 5.969611e-03, 'b_in': 1.022836e-02, 'sinks': 1.338524e-03, 'vn_g': 4.212450e-03, 'vn_b': 4.264948e-03, 'w_s': 4.198313e-03, 'b_s': 6.047195e-03, 'p_a': 3.011254e-03, 'p_b': 1.045257e-02, 'w_out': 1.070285e-02, 'b_out': 5.610093e-01, 'ln_g': 4.573311e+01, 'ln_b': 1.364259e+00}


def _to_microbatches(a, axis):
    t = _jnp.moveaxis(a, axis, 0)
    t = t.reshape((N_MICROBATCH, t.shape[0] // N_MICROBATCH) + t.shape[1:])
    return _jnp.moveaxis(t, 1, axis + 1)


def setup_inputs(seed: int = 0) -> dict:
    inp = _fwd_setup_inputs(seed)
    key = _jax.random.fold_in(_jax.random.key(seed), 7919)
    shape, _ = _output_shape()
    out = dict(inp)
    out["loss_target"] = _jax.random.normal(_jax.random.fold_in(key, 0), shape, _jnp.float32)
    for i, name in enumerate(TWIN_WEIGHTS):
        w = inp[name].astype(_jnp.float32)
        if MOMENT_SCALE is None:
            s = _jnp.sqrt(_jnp.mean(_jnp.square(w)) + 1e-30)
        else:
            s = MOMENT_SCALE[name]
        km, kv = _jax.random.split(_jax.random.fold_in(key, i + 1))
        out[name] = w
        out["m_" + name] = s * _jax.random.normal(km, w.shape, _jnp.float32)
        out["v_" + name] = (s * s) * _jax.random.uniform(kv, w.shape, _jnp.float32, 0.5, 1.5)
    if N_MICROBATCH > 1:
        for name, axis in PER_EXAMPLE_BATCH_AXIS.items():
            out[name] = _to_microbatches(out[name], axis)
    return {'x': out['x'], 'ln_in_g': out['ln_in_g'], 'ln_in_b': out['ln_in_b'], 'w_in': out['w_in'], 'b_in': out['b_in'], 'sinks': out['sinks'], 'vn_g': out['vn_g'], 'vn_b': out['vn_b'], 'w_s': out['w_s'], 'b_s': out['b_s'], 'p_a': out['p_a'], 'p_b': out['p_b'], 'w_out': out['w_out'], 'b_out': out['b_out'], 'ln_g': out['ln_g'], 'ln_b': out['ln_b'], 'loss_target': out['loss_target'], 'm_ln_in_g': out['m_ln_in_g'], 'm_ln_in_b': out['m_ln_in_b'], 'm_w_in': out['m_w_in'], 'm_b_in': out['m_b_in'], 'm_sinks': out['m_sinks'], 'm_vn_g': out['m_vn_g'], 'm_vn_b': out['m_vn_b'], 'm_w_s': out['m_w_s'], 'm_b_s': out['m_b_s'], 'm_p_a': out['m_p_a'], 'm_p_b': out['m_p_b'], 'm_w_out': out['m_w_out'], 'm_b_out': out['m_b_out'], 'm_ln_g': out['m_ln_g'], 'm_ln_b': out['m_ln_b'], 'v_ln_in_g': out['v_ln_in_g'], 'v_ln_in_b': out['v_ln_in_b'], 'v_w_in': out['v_w_in'], 'v_b_in': out['v_b_in'], 'v_sinks': out['v_sinks'], 'v_vn_g': out['v_vn_g'], 'v_vn_b': out['v_vn_b'], 'v_w_s': out['v_w_s'], 'v_b_s': out['v_b_s'], 'v_p_a': out['v_p_a'], 'v_p_b': out['v_p_b'], 'v_w_out': out['v_w_out'], 'v_b_out': out['v_b_out'], 'v_ln_g': out['v_ln_g'], 'v_ln_b': out['v_ln_b']}


def _loss(weights, diff, rest, loss_target):
    with _jax.named_scope("forward"):
        args = {**rest, TWIN_DIFF_INPUT: diff, **{k: w.astype(_WEIGHT_DTYPES[k]) for k, w in weights.items()}}
        y = _forward(args)
    with _jax.named_scope("loss_head"):
        err = _jnp.square(y.astype(_jnp.float32) - loss_target)
        return 0.5 * _jnp.sum(_jnp.mean(err, axis=-1)) if err.ndim else 0.5 * err


def _adamw(w, g, m, v):
    m = ADAM_B1 * m + (1.0 - ADAM_B1) * g
    v = ADAM_B2 * v + (1.0 - ADAM_B2) * _jnp.square(g)
    m_hat = m / (1.0 - ADAM_B1 ** ADAM_STEP)
    v_hat = v / (1.0 - ADAM_B2 ** ADAM_STEP)
    delta = -ADAM_LR * (m_hat / (_jnp.sqrt(v_hat) + ADAM_EPS) + ADAM_WD * w)
    return delta, m, v


def reference(x, ln_in_g, ln_in_b, w_in, b_in, sinks, vn_g, vn_b, w_s, b_s, p_a, p_b, w_out, b_out, ln_g, ln_b, loss_target, m_ln_in_g, m_ln_in_b, m_w_in, m_b_in, m_sinks, m_vn_g, m_vn_b, m_w_s, m_b_s, m_p_a, m_p_b, m_w_out, m_b_out, m_ln_g, m_ln_b, v_ln_in_g, v_ln_in_b, v_w_in, v_b_in, v_sinks, v_vn_g, v_vn_b, v_w_s, v_b_s, v_p_a, v_p_b, v_w_out, v_b_out, v_ln_g, v_ln_b):
    given = dict(x=x, ln_in_g=ln_in_g, ln_in_b=ln_in_b, w_in=w_in, b_in=b_in, sinks=sinks, vn_g=vn_g, vn_b=vn_b, w_s=w_s, b_s=b_s, p_a=p_a, p_b=p_b, w_out=w_out, b_out=b_out, ln_g=ln_g, ln_b=ln_b, loss_target=loss_target, m_ln_in_g=m_ln_in_g, m_ln_in_b=m_ln_in_b, m_w_in=m_w_in, m_b_in=m_b_in, m_sinks=m_sinks, m_vn_g=m_vn_g, m_vn_b=m_vn_b, m_w_s=m_w_s, m_b_s=m_b_s, m_p_a=m_p_a, m_p_b=m_p_b, m_w_out=m_w_out, m_b_out=m_b_out, m_ln_g=m_ln_g, m_ln_b=m_ln_b, v_ln_in_g=v_ln_in_g, v_ln_in_b=v_ln_in_b, v_w_in=v_w_in, v_b_in=v_b_in, v_sinks=v_sinks, v_vn_g=v_vn_g, v_vn_b=v_vn_b, v_w_s=v_w_s, v_b_s=v_b_s, v_p_a=v_p_a, v_p_b=v_p_b, v_w_out=v_w_out, v_b_out=v_b_out, v_ln_g=v_ln_g, v_ln_b=v_ln_b)
    weights = {n: given[n] for n in TWIN_WEIGHTS}
    shared = {n: given[n] for n in SHARED_INPUTS}
    per_example = {n: given[n] for n in ['x']}
    grad_fn = _jax.value_and_grad(_loss, argnums=(0, 1))

    def one_microbatch(ex, loss_target):
        ex = dict(ex)
        diff = ex.pop(TWIN_DIFF_INPUT)
        return grad_fn(weights, diff, {**shared, **ex}, loss_target)

    if N_MICROBATCH == 1:
        loss, (grad_w, grad_x) = one_microbatch(per_example, given["loss_target"])
    else:
        def body(carry, xs):
            loss_sum, grad_sum = carry
            l_k, (gw_k, gx_k) = one_microbatch(xs[0], xs[1])
            with _jax.named_scope("update"):
                return (loss_sum + l_k, _jax.tree.map(_jnp.add, grad_sum, gw_k)), gx_k

        init = (_jnp.zeros((), _jnp.float32), _jax.tree.map(_jnp.zeros_like, weights))
        (loss, grad_w), grad_x = _jax.lax.scan(body, init, (per_example, given["loss_target"]))
    with _jax.named_scope("update"):
        delta_w, new_m, new_v = {}, {}, {}
        for n in TWIN_WEIGHTS:
            delta_w[n], new_m[n], new_v[n] = _adamw(weights[n], grad_w[n], given["m_" + n], given["v_" + n])
    return (loss, grad_x, *[grad_w[n] for n in TWIN_WEIGHTS], *[delta_w[n] for n in TWIN_WEIGHTS],
            *[new_m[n] for n in TWIN_WEIGHTS], *[new_v[n] for n in TWIN_WEIGHTS])
```

```python
import jax
import jax.numpy as jnp
from jax import lax
from jax.experimental import pallas as pl
from jax.experimental.pallas import tpu as pltpu

F32 = jnp.float32
MXU_DTYPE = jnp.bfloat16

D_MODEL = 1024
DEPTH = 2
HEAD_DIM = 64
ATTN_W = 512
KV_W = 128
BLK = 128
SGU_W = 512
SGU_G = 4
N_COLS = 4864
C_Q, C_K, C_V, C_GA, C_UB, C_VB, C_GB, C_RA, C_RB = 0, 512, 640, 768, 1280, 1792, 2304, 2816, 3840
ALPHA = (2.0 * DEPTH) ** 0.25
LN_EPS = 1e-5
SCALE = HEAD_DIM ** -0.5
NEG = float(jnp.finfo(jnp.float32).min)
GELU_C = 0.7978845608028654
GELU_A = 0.044715

ADAM_LR, ADAM_B1, ADAM_B2, ADAM_EPS, ADAM_WD, ADAM_STEP = 0.001, 0.9, 0.999, 1e-08, 0.01, 10

V7X_VMEM_BYTES = 64 * 1024 * 1024
V7X_VMEM_RESERVE = 3 * 1024 * 1024
ROWS = 256
ROWS_BWD = 128

MESH = pl.DeviceIdType.MESH
N_CHIPS = 4


def _mm(a, b):
    return jnp.dot(a.astype(MXU_DTYPE), b.astype(MXU_DTYPE), preferred_element_type=F32)


def _mm_nt(a, b):
    return lax.dot_general(a.astype(MXU_DTYPE), b.astype(MXU_DTYPE), (((1,), (1,)), ((), ())),
                           preferred_element_type=F32)


def _mm_tn(a, b):
    return lax.dot_general(a.astype(MXU_DTYPE), b.astype(MXU_DTYPE), (((0,), (0,)), ((), ())),
                           preferred_element_type=F32)


def _sigmoid(x):
    return 1.0 / (1.0 + jnp.exp(-x))


def _gelu_parts(x):
    x2 = x * x
    t = jnp.tanh(GELU_C * (x + GELU_A * x2 * x))
    g = 0.5 * x * (1.0 + t)
    dg = 0.5 * (1.0 + t) + 0.5 * x * (1.0 - t * t) * (GELU_C * (1.0 + 3.0 * GELU_A * x2))
    return g, dg


def _ln_fwd(z, g, b):
    mu = jnp.mean(z, axis=-1, keepdims=True)
    zc = z - mu
    var = jnp.mean(zc * zc, axis=-1, keepdims=True)
    rstd = lax.rsqrt(var + LN_EPS)
    zhat = zc * rstd
    return zhat * g + b, zhat, rstd


def _ln_bwd(dy, zhat, rstd, g):
    dzh = dy * g
    m1 = jnp.mean(dzh, axis=-1, keepdims=True)
    m2 = jnp.mean(dzh * zhat, axis=-1, keepdims=True)
    return rstd * (dzh - m1 - zhat * m2)


def _colsum(v):
    return jnp.sum(v, axis=0, keepdims=True)


def _lane_lo(rows):
    return lax.broadcasted_iota(jnp.int32, (rows, 128), 1) < HEAD_DIM


def _dup_head(x2, hk, lo):
    xr = pltpu.roll(x2, HEAD_DIM, 1)
    return jnp.where(lo, x2, xr) if hk == 0 else jnp.where(lo, xr, x2)


def _fold_head(y0, y1, lo):
    f0 = y0 + pltpu.roll(y0, HEAD_DIM, 1)
    f1 = y1 + pltpu.roll(y1, HEAD_DIM, 1)
    return jnp.where(lo, f0, f1)


def _stack_heads(t0, t1, lo):
    z = jnp.zeros_like(t0)
    return jnp.concatenate([jnp.where(lo, t0, z), jnp.where(lo, z, t0),
                            jnp.where(lo, t1, z), jnp.where(lo, z, t1)], axis=0)


def _unstack_heads(o, lo):
    return (jnp.where(lo, o[0:128], o[128:256]), jnp.where(lo, o[256:384], o[384:512]))


def _attn_probs(qs, kd, sink4, start):
    s = _mm_nt(qs, kd) * SCALE
    row = lax.broadcasted_iota(jnp.int32, (4 * BLK, 2 * BLK), 0) & (BLK - 1)
    kpos = lax.broadcasted_iota(jnp.int32, (4 * BLK, 2 * BLK), 1)
    valid = (kpos > row) & (kpos <= row + BLK)
    if start is not None:
        valid = valid & (jnp.logical_not(start) | (kpos >= BLK))
    s = jnp.where(valid, s, NEG)
    r1 = lax.broadcasted_iota(jnp.int32, (4 * BLK, 1), 0)
    sk = jnp.where(r1 < BLK, sink4[0], jnp.where(r1 < 2 * BLK, sink4[1], jnp.where(r1 < 3 * BLK, sink4[2], sink4[3])))
    m = jnp.maximum(jnp.max(s, axis=-1, keepdims=True), sk)
    p = jnp.exp(s - m)
    es = jnp.exp(sk - m)
    denom = jnp.sum(p, axis=-1, keepdims=True) + es
    return p / denom, es / denom


_PROJ_CHUNK = 512


def _project(x, w_ref, b_ref, h_s):
    xb = x.astype(MXU_DTYPE)
    n = w_ref.shape[1]
    for c0 in range(0, n, _PROJ_CHUNK):
        c1 = min(c0 + _PROJ_CHUNK, n)
        h_s[:, c0:c1] = jnp.dot(xb, w_ref[:, c0:c1], preferred_element_type=F32) + b_ref[:, c0:c1]


def _kv_blocks(xh, w, h_s, nb):
    kvh = _mm(xh, w["win"][:, C_K:C_K + 2 * KV_W]) + w["bin"][:, C_K:C_K + 2 * KV_W]
    ks = [kvh[:, 0:KV_W]] + [h_s[n * BLK:(n + 1) * BLK, C_K:C_K + KV_W] for n in range(nb)]
    vs = [kvh[:, KV_W:2 * KV_W]] + [h_s[n * BLK:(n + 1) * BLK, C_V:C_V + KV_W] for n in range(nb)]
    return ks, vs


def _attn_operands(h_s, ks, vs, n, hk, lo, lo2):
    rows = slice(n * BLK, (n + 1) * BLK)
    kd = _dup_head(jnp.concatenate([ks[n], ks[n + 1]], axis=0), hk, lo2)
    vd = _dup_head(jnp.concatenate([vs[n], vs[n + 1]], axis=0), hk, lo2)
    c0 = C_Q + 2 * hk * 128
    qs = _stack_heads(h_s[rows, c0:c0 + 128], h_s[rows, c0 + 128:c0 + 256], lo)
    return qs, kd, vd


def _sgu_weights(w):
    tri = (lax.broadcasted_iota(jnp.int32, (BLK, BLK), 0) >= lax.broadcasted_iota(jnp.int32, (BLK, BLK), 1))
    return tri, [jnp.where(tri, w["ws"][g], 0.0) for g in range(SGU_G)]


def _layer_forward(x, xh, start, w, h_s, attn_s, mix_s, keep):
    R = x.shape[0]
    nb = R // BLK
    lo = _lane_lo(BLK)
    lo2 = _lane_lo(2 * BLK)
    _project(x, w["win"], w["bin"], h_s)
    ks, vs = _kv_blocks(xh, w, h_s, nb)
    sinks = [w["sinks"][j] for j in range(8)]
    probs_l = []
    for n in range(nb):
        rows = slice(n * BLK, (n + 1) * BLK)
        for hk in range(2):
            qs, kd, vd = _attn_operands(h_s, ks, vs, n, hk, lo, lo2)
            probs, ps = _attn_probs(qs, kd, sinks[4 * hk:4 * hk + 4], start if n == 0 else None)
            t0, t1 = _unstack_heads(_mm(probs, vd), lo)
            attn_s[rows, 2 * hk * 128:(2 * hk + 1) * 128] = t0
            attn_s[rows, (2 * hk + 1) * 128:(2 * hk + 2) * 128] = t1
            if keep:
                probs_l.append((probs, ps))
    ga = h_s[:, C_GA:C_GA + ATTN_W]
    ya = attn_s[...] * (ga * _sigmoid(ga))
    gu, _ = _gelu_parts(h_s[:, C_UB:C_UB + SGU_W])
    gv, _ = _gelu_parts(h_s[:, C_VB:C_VB + SGU_W])
    vn, _, _ = _ln_fwd(gv, w["vng"][...], w["vnb"][...])
    _, wms = _sgu_weights(w)
    for n in range(nb):
        rows = slice(n * BLK, (n + 1) * BLK)
        for g in range(SGU_G):
            cols = slice(g * 128, (g + 1) * 128)
            mix_s[rows, cols] = _mm(wms[g], vn[rows, cols]) + w["bs"][:, cols]
    gb = h_s[:, C_GB:C_GB + SGU_W]
    yb = gu * mix_s[...] * (gb * _sigmoid(gb))
    a = _mm(ya, w["pa"][...])
    b = _mm(yb, w["pb"][...])
    merged = _sigmoid(h_s[:, C_RA:C_RA + D_MODEL]) * a + _sigmoid(h_s[:, C_RB:C_RB + D_MODEL]) * b
    out = _mm(merged, w["wout"][...]) + w["bout"][...]
    y, zhat, rstd = _ln_fwd(ALPHA * x + out, w["lng"][...], w["lnb"][...])
    if not keep:
        return y, None
    return y, dict(ks=ks, vs=vs, probs=probs_l, ya=ya, yb=yb, a=a, b=b, merged=merged, zhat=zhat, rstd=rstd)


def _dsilu(g, sg):
    return sg * (1.0 + g * (1.0 - sg))


_W_NAMES = ("win", "bin", "sinks", "vng", "vnb", "ws", "bs", "pa", "pb", "wout", "bout", "lng", "lnb")
_WT_NAMES = ("paT", "pbT", "woutT")
_ACC_NAMES = ("dwout", "dpa", "dpb", "dbin", "dbout", "dlng", "dlnb", "dvng", "dvnb", "dws", "dbs", "dsink", "loss")


def _make_fwd_body(R, S):
    def body(x_ref, xh_ref, *rest):
        w = dict(zip(_W_NAMES, rest[:len(_W_NAMES)]))
        y_ref, h_s, attn_s, mix_s = rest[len(_W_NAMES):]
        start = (pl.program_id(0) % (S // R)) == 0
        y, _ = _layer_forward(x_ref[...], xh_ref[...], start, w, h_s, attn_s, mix_s, keep=False)
        y_ref[...] = y
    return body


def _make_bwd_body(R, S, nT, is_last):
    nb = R // BLK
    names = _W_NAMES + _WT_NAMES
    nw = len(names)

    def body(x_ref, xh_ref, aux_ref, *rest):
        w = dict(zip(names, rest[:nw]))
        dres_ref, dh_ref = rest[nw:nw + 2]
        acc = dict(zip(_ACC_NAMES, rest[nw + 2:nw + 2 + len(_ACC_NAMES)]))
        h_s, attn_s, mix_s, ckv_s = rest[nw + 2 + len(_ACC_NAMES):]
        i = pl.program_id(0)
        start = ((nT - 1 - i) % (S // R)) == 0
        lo = _lane_lo(BLK)
        lo2 = _lane_lo(2 * BLK)

        @pl.when(i == 0)
        def _():
            for name in _ACC_NAMES:
                acc[name][...] = jnp.zeros(acc[name].shape, F32)
            ckv_s[...] = jnp.zeros(ckv_s.shape, F32)

        x = x_ref[...]
        y, sv = _layer_forward(x, xh_ref[...], start, w, h_s, attn_s, mix_s, keep=True)
        if is_last:
            diff = y - aux_ref[...]
            part = 0.5 * jnp.sum(jnp.mean(diff * diff, axis=-1, keepdims=True), axis=0, keepdims=True)
            acc["loss"][...] += jnp.broadcast_to(part, acc["loss"].shape)
            dy = diff * (1.0 / D_MODEL)
        else:
            dy = aux_ref[...]

        def put(c0, val, rows=slice(None)):
            width = val.shape[1]
            dh_ref[rows, c0:c0 + width] = val.astype(dh_ref.dtype)
            acc["dbin"][:, c0:c0 + width] += _colsum(val)

        acc["dlng"][...] += _colsum(dy * sv["zhat"])
        acc["dlnb"][...] += _colsum(dy)
        dz = _ln_bwd(dy, sv["zhat"], sv["rstd"], w["lng"][...])
        dres_ref[...] = ALPHA * dz
        acc["dbout"][...] += _colsum(dz)
        acc["dwout"][...] += _mm_tn(sv["merged"], dz)
        dmerged = _mm(dz, w["woutT"][...])
        sa = _sigmoid(h_s[:, C_RA:C_RA + D_MODEL])
        da = dmerged * sa
        put(C_RA, da * sv["a"] * (1.0 - sa))
        sb = _sigmoid(h_s[:, C_RB:C_RB + D_MODEL])
        db = dmerged * sb
        put(C_RB, db * sv["b"] * (1.0 - sb))
        acc["dpa"][...] += _mm_tn(sv["ya"], da)
        acc["dpb"][...] += _mm_tn(sv["yb"], db)
        dya = _mm(da, w["paT"][...])
        dyb = _mm(db, w["pbT"][...])
        ga = h_s[:, C_GA:C_GA + ATTN_W]
        sga = _sigmoid(ga)
        put(C_GA, dya * attn_s[...] * _dsilu(ga, sga))
        attn_s[...] = dya * (ga * sga)
        ks, vs = sv["ks"], sv["vs"]
        dks = [jnp.zeros((BLK, KV_W), F32) for _ in range(nb + 1)]
        dvs = [jnp.zeros((BLK, KV_W), F32) for _ in range(nb + 1)]
        for n in range(nb):
            rows = slice(n * BLK, (n + 1) * BLK)
            ydk, ydv = [], []
            for hk in range(2):
                qs, kd, vd = _attn_operands(h_s, ks, vs, n, hk, lo, lo2)
                probs, ps = sv["probs"][2 * n + hk]
                dos = _stack_heads(attn_s[rows, 2 * hk * 128:(2 * hk + 1) * 128],
                                   attn_s[rows, (2 * hk + 1) * 128:(2 * hk + 2) * 128], lo)
                dp = _mm_nt(dos, vd)
                delta = jnp.sum(probs * dp, axis=-1, keepdims=True)
                dsk = -(ps * delta)
                for g in range(4):
                    j = 4 * hk + g
                    acc["dsink"][j:j + 1, :] += jnp.broadcast_to(_colsum(dsk[g * BLK:(g + 1) * BLK]), (1, 128))
                dss = probs * (dp - delta) * SCALE
                q0, q1 = _unstack_heads(_mm(dss, kd), lo)
                put(C_Q + 2 * hk * 128, q0, rows)
                put(C_Q + (2 * hk + 1) * 128, q1, rows)
                ydk.append(_mm_tn(dss, qs))
                ydv.append(_mm_tn(probs, dos))
            dk2 = _fold_head(ydk[0], ydk[1], lo2)
            dv2 = _fold_head(ydv[0], ydv[1], lo2)
            dks[n] = dks[n] + dk2[0:BLK]
            dks[n + 1] = dks[n + 1] + dk2[BLK:2 * BLK]
            dvs[n] = dvs[n] + dv2[0:BLK]
            dvs[n + 1] = dvs[n + 1] + dv2[BLK:2 * BLK]
        dks[nb] = dks[nb] + ckv_s[:, 0:KV_W]
        dvs[nb] = dvs[nb] + ckv_s[:, KV_W:2 * KV_W]
        ckv_s[:, 0:KV_W] = dks[0]
        ckv_s[:, KV_W:2 * KV_W] = dvs[0]
        put(C_K, jnp.concatenate(dks[1:], axis=0))
        put(C_V, jnp.concatenate(dvs[1:], axis=0))
        gb = h_s[:, C_GB:C_GB + SGU_W]
        sgb = _sigmoid(gb)
        gu, dgu = _gelu_parts(h_s[:, C_UB:C_UB + SGU_W])
        mixed = mix_s[...]
        put(C_GB, dyb * (gu * mixed) * _dsilu(gb, sgb))
        dsgu = dyb * (gb * sgb)
        put(C_UB, dsgu * mixed * dgu)
        dmixed = dsgu * gu
        gv, dgv = _gelu_parts(h_s[:, C_VB:C_VB + SGU_W])
        vn, vhat, vrstd = _ln_fwd(gv, w["vng"][...], w["vnb"][...])
        tri, wms = _sgu_weights(w)
        dbs_part = jnp.zeros((BLK, SGU_W), F32)
        for n in range(nb):
            rows = slice(n * BLK, (n + 1) * BLK)
            dbs_part = dbs_part + dmixed[rows]
            for g in range(SGU_G):
                cols = slice(g * 128, (g + 1) * 128)
                acc["dws"][g] += jnp.where(tri, _mm_nt(dmixed[rows, cols], vn[rows, cols]), 0.0)
                mix_s[rows, cols] = _mm_tn(wms[g], dmixed[rows, cols])
        acc["dbs"][...] += dbs_part
        dvn = mix_s[...]
        acc["dvng"][...] += _colsum(dvn * vhat)
        acc["dvnb"][...] += _colsum(dvn)
        put(C_VB, _ln_bwd(dvn, vhat, vrstd, w["vng"][...]) * dgv)

        @pl.when(i == nT - 1)
        def _():
            for g in range(SGU_G):
                cols = slice(g * 128, (g + 1) * 128)
                tot = jnp.sum(acc["dbs"][:, cols], axis=1, keepdims=True)
                acc["dbs"][:, cols] = jnp.broadcast_to(tot, (BLK, 128))

    return body


def _resident():
    return pl.BlockSpec(memory_space=pltpu.VMEM)


def _weight_specs(names):
    return [pl.BlockSpec(memory_space=pltpu.SMEM) if n == "sinks" else _resident() for n in names]


def _nbytes(a):
    n = jnp.dtype(a.dtype).itemsize
    for d in a.shape:
        n *= d
    return n


def _layer_params(resident=()):
    scoped = V7X_VMEM_BYTES - V7X_VMEM_RESERVE - sum(_nbytes(a) for a in resident)
    return pltpu.CompilerParams(dimension_semantics=("arbitrary",), vmem_limit_bytes=scoped)


def layer_forward(x, wts, S, name):
    T = x.shape[0]
    R = ROWS
    npt = R // BLK
    return pl.pallas_call(
        _make_fwd_body(R, S),
        name=name,
        grid=(T // R,),
        in_specs=[pl.BlockSpec((R, D_MODEL), lambda t: (t, 0)),
                  pl.BlockSpec((BLK, D_MODEL), lambda t: (jnp.maximum(t * npt - 1, 0), 0))] + _weight_specs(_W_NAMES),
        out_specs=pl.BlockSpec((R, D_MODEL), lambda t: (t, 0)),
        out_shape=jax.ShapeDtypeStruct((T, D_MODEL), F32),
        scratch_shapes=[pltpu.VMEM((R, N_COLS), F32), pltpu.VMEM((R, ATTN_W), F32), pltpu.VMEM((R, SGU_W), F32)],
        compiler_params=_layer_params([wts[n] for n in _W_NAMES]),
    )(x, x, *[wts[n] for n in _W_NAMES])


_ACC_SHAPES = dict(dwout=(D_MODEL, D_MODEL), dpa=(ATTN_W, D_MODEL), dpb=(SGU_W, D_MODEL), dbin=(1, N_COLS),
                   dbout=(1, D_MODEL), dlng=(1, D_MODEL), dlnb=(1, D_MODEL), dvng=(1, SGU_W), dvnb=(1, SGU_W),
                   dws=(SGU_G, BLK, BLK), dbs=(BLK, SGU_W), dsink=(8, 128), loss=(8, 128))


def layer_backward(x, aux, wts, S, is_last, name):
    T = x.shape[0]
    R = ROWS_BWD
    nT = T // R
    npt = R // BLK
    rev = lambda i: (nT - 1 - i, 0)
    halo = lambda i: (jnp.maximum((nT - 1 - i) * npt - 1, 0), 0)
    names = _W_NAMES + _WT_NAMES
    out_shape = ([jax.ShapeDtypeStruct((T, D_MODEL), F32), jax.ShapeDtypeStruct((T, N_COLS), MXU_DTYPE)]
                 + [jax.ShapeDtypeStruct(_ACC_SHAPES[n], F32) for n in _ACC_NAMES])
    outs = pl.pallas_call(
        _make_bwd_body(R, S, nT, is_last),
        name=name,
        grid=(nT,),
        in_specs=[pl.BlockSpec((R, D_MODEL), rev), pl.BlockSpec((BLK, D_MODEL), halo),
                  pl.BlockSpec((R, D_MODEL), rev)] + _weight_specs(names),
        out_specs=[pl.BlockSpec((R, D_MODEL), rev), pl.BlockSpec((R, N_COLS), rev)] + [_resident() for _ in _ACC_NAMES],
        out_shape=out_shape,
        scratch_shapes=[pltpu.VMEM((R, N_COLS), F32), pltpu.VMEM((R, ATTN_W), F32), pltpu.VMEM((R, SGU_W), F32),
                        pltpu.VMEM((BLK, 2 * KV_W), F32)],
        compiler_params=_layer_params([wts[n] for n in names] + out_shape[2:]),
    )(x, x, aux, *[wts[n] for n in names])
    return outs[0], outs[1], dict(zip(_ACC_NAMES, outs[2:]))


def input_grads(dres, dh, x, winT, name):
    T = x.shape[0]
    R = ROWS

    def body(dres_ref, dh_ref, x_ref, winT_ref, dx_ref, dwin_ref):
        @pl.when(pl.program_id(0) == 0)
        def _():
            dwin_ref[...] = jnp.zeros(dwin_ref.shape, F32)

        dh = dh_ref[...]
        dx_ref[...] = dres_ref[...] + jnp.dot(dh, winT_ref[...], preferred_element_type=F32)
        dwin_ref[...] += _mm_tn(x_ref[...], dh)

    row = lambda t: (t, 0)
    return pl.pallas_call(
        body,
        name=name,
        grid=(T // R,),
        in_specs=[pl.BlockSpec((R, D_MODEL), row), pl.BlockSpec((R, N_COLS), row), pl.BlockSpec((R, D_MODEL), row),
                  _resident()],
        out_specs=[pl.BlockSpec((R, D_MODEL), row), _resident()],
        out_shape=[jax.ShapeDtypeStruct((T, D_MODEL), F32), jax.ShapeDtypeStruct((D_MODEL, N_COLS), F32)],
        compiler_params=_layer_params([winT, jax.ShapeDtypeStruct((D_MODEL, N_COLS), F32)]),
    )(dres, dh, x, winT)


def input_norm_forward(x, g, b, name):
    T = x.shape[0]
    R = ROWS

    def body(x_ref, g_ref, b_ref, y_ref):
        y_ref[...] = _ln_fwd(x_ref[...], g_ref[...], b_ref[...])[0]

    row = lambda t: (t, 0)
    return pl.pallas_call(
        body, name=name, grid=(T // R,),
        in_specs=[pl.BlockSpec((R, D_MODEL), row), _resident(), _resident()],
        out_specs=pl.BlockSpec((R, D_MODEL), row),
        out_shape=jax.ShapeDtypeStruct((T, D_MODEL), F32),
        compiler_params=_layer_params(),
    )(x, g, b)


def input_norm_backward(x, dy, g, name):
    T = x.shape[0]
    R = ROWS

    def body(x_ref, dy_ref, g_ref, dx_ref, dg_ref, db_ref):
        @pl.when(pl.program_id(0) == 0)
        def _():
            dg_ref[...] = jnp.zeros(dg_ref.shape, F32)
            db_ref[...] = jnp.zeros(db_ref.shape, F32)

        dy = dy_ref[...]
        _, xhat, rstd = _ln_fwd(x_ref[...], g_ref[...], g_ref[...])
        dx_ref[...] = _ln_bwd(dy, xhat, rstd, g_ref[...])
        dg_ref[...] += _colsum(dy * xhat)
        db_ref[...] += _colsum(dy)

    row = lambda t: (t, 0)
    vec = jax.ShapeDtypeStruct((1, D_MODEL), F32)
    return pl.pallas_call(
        body, name=name, grid=(T // R,),
        in_specs=[pl.BlockSpec((R, D_MODEL), row), pl.BlockSpec((R, D_MODEL), row), _resident()],
        out_specs=[pl.BlockSpec((R, D_MODEL), row), _resident(), _resident()],
        out_shape=[jax.ShapeDtypeStruct((T, D_MODEL), F32), vec, vec],
        compiler_params=_layer_params(),
    )(x, dy, g)


def local_step(x, target, ln_in_g, ln_in_b, layers, S):
    x0 = input_norm_forward(x, ln_in_g, ln_in_b, "ln_in_fwd")
    x1 = layer_forward(x0, layers[0], S, "layer0_fwd")
    dres1, dh1, g1 = layer_backward(x1, target, layers[1], S, True, "layer1_bwd")
    dx1, g1["dwin"] = input_grads(dres1, dh1, x1, layers[1]["winT"], "layer1_dx_dwin")
    dres0, dh0, g0 = layer_backward(x0, dx1, layers[0], S, False, "layer0_bwd")
    dx0, g0["dwin"] = input_grads(dres0, dh0, x0, layers[0]["winT"], "layer0_dx_dwin")
    gx, dg_in, db_in = input_norm_backward(x, dx0, ln_in_g, "ln_in_bwd")
    return g1["loss"][0, 0], gx, dg_in, db_in, [g0, g1]


_ANY = pl.BlockSpec(memory_space=pl.ANY)


def _place():
    x, y, c = lax.axis_index("x"), lax.axis_index("y"), lax.axis_index("c")
    others = [(1 - x, y), (x, 1 - y), (1 - x, 1 - y)]
    return x, y, c, 2 * x + y, others


def _remote(src, dst, send_sem, recv_sem, to):
    return pltpu.make_async_remote_copy(src_ref=src, dst_ref=dst, send_sem=send_sem, recv_sem=recv_sem,
                                        device_id=to, device_id_type=MESH)


def gather_weights(shards):
    n = len(shards)

    def body(*refs):
        src, dst = refs[:n], refs[n:2 * n]
        send1, recv1, send2, recv2, lsem = refs[2 * n:]
        x, y, c, j, others = _place()
        sib = (x, y, 1 - c)
        own = [pltpu.make_async_copy(src[w], dst[w].at[j], lsem.at[w]) for w in range(n)]
        for cp in own:
            cp.start()
        out = []
        for w in range(n):
            for r, (px, py) in enumerate(others):
                out.append(_remote(src[w].at[c], dst[w].at[j, c], send1.at[3 * w + r], recv1.at[3 * w + r], (px, py, c)))
        for cp in out:
            cp.start()
        passed = []
        for w in range(n):
            for r, (px, py) in enumerate(others):
                k = 2 * px + py
                _remote(src[w].at[c], dst[w].at[k, c], send1.at[3 * w + r], recv1.at[3 * w + r], (px, py, c)).wait_recv()
                cp = _remote(dst[w].at[k, c], dst[w].at[k, c], send2.at[3 * w + r], recv2.at[3 * w + r], sib)
                cp.start()
                passed.append(cp)
        for w in range(n):
            for r, (px, py) in enumerate(others):
                k = 2 * px + py
                _remote(dst[w].at[k, 1 - c], dst[w].at[k, 1 - c], send2.at[3 * w + r], recv2.at[3 * w + r], sib).wait_recv()
        for cp in out + passed:
            cp.wait_send()
        for cp in own:
            cp.wait()

    dma = pltpu.SemaphoreType.DMA
    return pl.pallas_call(
        body, name="gather_weights", in_specs=[_ANY] * n, out_specs=[_ANY] * n,
        out_shape=[jax.ShapeDtypeStruct((N_CHIPS,) + s.shape, s.dtype) for s in shards],
        scratch_shapes=[dma((3 * n,)), dma((3 * n,)), dma((3 * n,)), dma((3 * n,)), dma((n,))],
    )(*shards)


def pair_send_other_half(arrs):
    n = len(arrs)

    def body(*refs):
        src, dst = refs[:n], refs[n:2 * n]
        send, recv = refs[2 * n:]
        x, y, c, _, _ = _place()
        cps = [_remote(src[i].at[1 - c], dst[i], send.at[i], recv.at[i], (x, y, 1 - c)) for i in range(n)]
        for cp in cps:
            cp.start()
        for cp in cps:
            cp.wait()

    dma = pltpu.SemaphoreType.DMA
    return pl.pallas_call(
        body, name="pair_send_other_half", in_specs=[_ANY] * n, out_specs=[_ANY] * n,
        out_shape=[jax.ShapeDtypeStruct(a.shape[1:], a.dtype) for a in arrs],
        scratch_shapes=[dma((n,)), dma((n,))],
    )(*arrs)


def chip_exchange(arrs, scatter):
    n = len(arrs)

    def body(*refs):
        src, dst = refs[:n], refs[n:2 * n]
        send, recv, lsem = refs[2 * n:]
        x, y, c, j, others = _place()
        part = lambda i, k: src[i].at[k] if scatter[i] else src[i]
        own = [pltpu.make_async_copy(part(i, j), dst[i].at[j], lsem.at[i]) for i in range(n)]
        out = []
        for i in range(n):
            for r, (px, py) in enumerate(others):
                out.append(_remote(part(i, 2 * px + py), dst[i].at[j], send.at[3 * i + r], recv.at[3 * i + r], (px, py, c)))
        for cp in own + out:
            cp.start()
        for i in range(n):
            for r, (px, py) in enumerate(others):
                k = 2 * px + py
                _remote(part(i, k), dst[i].at[k], send.at[3 * i + r], recv.at[3 * i + r], (px, py, c)).wait_recv()
        for cp in out:
            cp.wait_send()
        for cp in own:
            cp.wait()

    dma = pltpu.SemaphoreType.DMA
    shapes = [a.shape if scatter[i] else (N_CHIPS,) + a.shape for i, a in enumerate(arrs)]
    return pl.pallas_call(
        body, name="chip_exchange", in_specs=[_ANY] * n, out_specs=[_ANY] * n,
        out_shape=[jax.ShapeDtypeStruct(s, a.dtype) for s, a in zip(shapes, arrs)],
        scratch_shapes=[dma((3 * n,)), dma((3 * n,)), dma((n,))],
    )(*arrs)


def pair_complete(arrs):
    n = len(arrs)

    def body(*refs):
        src, dst = refs[:n], refs[n:2 * n]
        send, recv, lsem = refs[2 * n:]
        x, y, c, _, _ = _place()
        own = [pltpu.make_async_copy(src[i], dst[i].at[c], lsem.at[i]) for i in range(n)]
        out = [_remote(src[i], dst[i].at[c], send.at[i], recv.at[i], (x, y, 1 - c)) for i in range(n)]
        for cp in own + out:
            cp.start()
        for i in range(n):
            _remote(src[i], dst[i].at[1 - c], send.at[i], recv.at[i], (x, y, 1 - c)).wait_recv()
        for cp in out:
            cp.wait_send()
        for cp in own:
            cp.wait()

    dma = pltpu.SemaphoreType.DMA
    return pl.pallas_call(
        body, name="pair_complete", in_specs=[_ANY] * n, out_specs=[_ANY] * n,
        out_shape=[jax.ShapeDtypeStruct((2,) + a.shape, a.dtype) for a in arrs],
        scratch_shapes=[dma((n,)), dma((n,)), dma((n,))],
    )(*arrs)


N_STEPS = 8


def _rows2d(a):
    return a.reshape((-1, a.shape[-1]))


def add_own_half(parts, landed, core, out_dtypes):
    n = len(parts)

    def body(core_ref, *refs):
        for i in range(n):
            refs[2 * n + i][...] = (refs[i][0] + refs[n + i][...]).astype(refs[2 * n + i].dtype)

    in_specs, out_specs = [], []
    for a in parts:
        in_specs.append(pl.BlockSpec((1, a.shape[1] // N_STEPS, a.shape[2]), lambda s, core_ref: (core_ref[0], s, 0)))
    for a in landed:
        in_specs.append(pl.BlockSpec((a.shape[0] // N_STEPS, a.shape[1]), lambda s, core_ref: (s, 0)))
        out_specs.append(pl.BlockSpec((a.shape[0] // N_STEPS, a.shape[1]), lambda s, core_ref: (s, 0)))
    return pl.pallas_call(
        body, name="add_own_half",
        grid_spec=pltpu.PrefetchScalarGridSpec(num_scalar_prefetch=1, grid=(N_STEPS,), in_specs=in_specs, out_specs=out_specs),
        out_shape=[jax.ShapeDtypeStruct(a.shape, dt) for a, dt in zip(landed, out_dtypes)],
        compiler_params=_layer_params(),
    )(core, *parts, *landed)


def sum_chips(arrs):
    n = len(arrs)

    def body(*refs):
        for i in range(n):
            src = refs[i]
            tot = src[0].astype(F32)
            for k in range(1, N_CHIPS):
                tot = tot + src[k].astype(F32)
            refs[n + i][...] = tot

    in_specs = [pl.BlockSpec((N_CHIPS, a.shape[1] // N_STEPS, a.shape[2]), lambda s: (0, s, 0)) for a in arrs]
    out_specs = [pl.BlockSpec((a.shape[1] // N_STEPS, a.shape[2]), lambda s: (s, 0)) for a in arrs]
    return pl.pallas_call(
        body, name="sum_chips", grid=(N_STEPS,), in_specs=in_specs, out_specs=out_specs,
        out_shape=[jax.ShapeDtypeStruct(a.shape[1:], F32) for a in arrs],
        compiler_params=_layer_params(),
    )(*arrs)


def adamw(ws, gs, ms, vs):
    n = len(ws)
    c1 = 1.0 / (1.0 - ADAM_B1 ** ADAM_STEP)
    c2 = 1.0 / (1.0 - ADAM_B2 ** ADAM_STEP)

    def body(*refs):
        for i in range(n):
            w, g, m, v = (refs[k * n + i][...] for k in range(4))
            m = ADAM_B1 * m + (1.0 - ADAM_B1) * g
            v = ADAM_B2 * v + (1.0 - ADAM_B2) * (g * g)
            refs[4 * n + i][...] = -ADAM_LR * ((m * c1) / (jnp.sqrt(v * c2) + ADAM_EPS) + ADAM_WD * w)
            refs[5 * n + i][...] = m
            refs[6 * n + i][...] = v

    spec = lambda a: pl.BlockSpec((a.shape[0] // N_STEPS, a.shape[1]), lambda s: (s, 0))
    outs = pl.pallas_call(
        body, name="adamw", grid=(N_STEPS,),
        in_specs=[spec(a) for a in ws] * 4, out_specs=[spec(a) for a in ws] * 3,
        out_shape=[jax.ShapeDtypeStruct(a.shape, F32) for a in ws] * 3,
        compiler_params=_layer_params(),
    )(*ws, *gs, *ms, *vs)
    return outs[:n], outs[n:2 * n], outs[2 * n:]


_SMALL = (("ln_in_g", (D_MODEL,)), ("ln_in_b", (D_MODEL,)), ("b_in", (DEPTH, N_COLS)), ("sinks", (DEPTH, 8)),
          ("vn_g", (DEPTH, SGU_W)), ("vn_b", (DEPTH, SGU_W)), ("w_s", (DEPTH, SGU_G, BLK, BLK)),
          ("b_s", (DEPTH, SGU_G, BLK)), ("b_out", (DEPTH, D_MODEL)), ("ln_g", (DEPTH, D_MODEL)), ("ln_b", (DEPTH, D_MODEL)))
_BIG = ("w_in", "p_a", "p_b", "w_out")
SMALL_ROWS = 1280


def _pack_small(vals):
    flat = []
    for name, shape in _SMALL:
        v = vals[name].reshape((-1,))
        flat.append(jnp.pad(v, (0, (-v.shape[0]) % 128)))
    flat = jnp.concatenate(flat)
    return jnp.pad(flat, (0, SMALL_ROWS * 128 - flat.shape[0])).reshape((SMALL_ROWS, 128))


def _unpack_small(packed):
    flat = packed.reshape((-1,))
    out, pos = {}, 0
    for name, shape in _SMALL:
        size = 1
        for d in shape:
            size *= d
        out[name] = flat[pos:pos + size].reshape(shape)
        pos += size + (-size) % 128
    return out


def _by_chip(a, axis):
    shp = a.shape
    a = a.reshape(shp[:axis] + (N_CHIPS, shp[axis] // N_CHIPS) + shp[axis + 1:])
    return jnp.moveaxis(a, axis, 1)


def kernel(x, ln_in_g, ln_in_b, w_in, b_in, sinks, vn_g, vn_b, w_s, b_s, p_a, p_b, w_out, b_out, ln_g, ln_b, loss_target, m_ln_in_g, m_ln_in_b, m_w_in, m_b_in, m_sinks, m_vn_g, m_vn_b, m_w_s, m_b_s, m_p_a, m_p_b, m_w_out, m_b_out, m_ln_g, m_ln_b, v_ln_in_g, v_ln_in_b, v_w_in, v_b_in, v_sinks, v_vn_g, v_vn_b, v_w_s, v_b_s, v_p_a, v_p_b, v_w_out, v_b_out, v_ln_g, v_ln_b):
    weights = dict(ln_in_g=ln_in_g, ln_in_b=ln_in_b, w_in=w_in, b_in=b_in, sinks=sinks, vn_g=vn_g, vn_b=vn_b, w_s=w_s,
                   b_s=b_s, p_a=p_a, p_b=p_b, w_out=w_out, b_out=b_out, ln_g=ln_g, ln_b=ln_b)
    mom1 = dict(ln_in_g=m_ln_in_g, ln_in_b=m_ln_in_b, w_in=m_w_in, b_in=m_b_in, sinks=m_sinks, vn_g=m_vn_g, vn_b=m_vn_b,
                w_s=m_w_s, b_s=m_b_s, p_a=m_p_a, p_b=m_p_b, w_out=m_w_out, b_out=m_b_out, ln_g=m_ln_g, ln_b=m_ln_b)
    mom2 = dict(ln_in_g=v_ln_in_g, ln_in_b=v_ln_in_b, w_in=v_w_in, b_in=v_b_in, sinks=v_sinks, vn_g=v_vn_g, vn_b=v_vn_b,
                w_s=v_w_s, b_s=v_b_s, p_a=v_p_a, p_b=v_p_b, w_out=v_w_out, b_out=v_b_out, ln_g=v_ln_g, ln_b=v_ln_b)
    n_seq, S, _ = x.shape
    T = n_seq * S
    core = lax.axis_index("c").astype(jnp.int32).reshape((1,))

    g_win, g_pa, g_pb, g_wout = gather_weights([weights[n].astype(MXU_DTYPE) for n in _BIG])
    layers = []
    for l in range(DEPTH):
        win = jnp.concatenate([g_win[k, l] for k in range(N_CHIPS)], axis=1)
        pa = jnp.concatenate([g_pa[k, l] for k in range(N_CHIPS)], axis=1)
        pb = jnp.concatenate([g_pb[k, l] for k in range(N_CHIPS)], axis=1)
        wout = jnp.concatenate([g_wout[k, l] for k in range(N_CHIPS)], axis=0)
        layers.append(dict(win=win, winT=win.T, pa=pa, paT=pa.T, pb=pb, pbT=pb.T, wout=wout, woutT=wout.T,
                           bin=b_in[l][None], sinks=sinks[l], vng=vn_g[l][None], vnb=vn_b[l][None], ws=w_s[l],
                           bs=jnp.repeat(b_s[l].T, 128, axis=1), bout=b_out[l][None], lng=ln_g[l][None], lnb=ln_b[l][None]))

    loss_part, grad_x, d_ln_in_g, d_ln_in_b, gl = local_step(
        x.reshape((T, D_MODEL)), loss_target.reshape((T, D_MODEL)), ln_in_g[None], ln_in_b[None], layers, S)
    loss = lax.psum(loss_part, ("x", "y", "c"))

    per_layer = lambda key, pick: jnp.stack([pick(gl[l][key]) for l in range(DEPTH)])
    whole = lambda key: per_layer(key, lambda a: a)
    row0 = lambda key: per_layer(key, lambda a: a[0])
    small_part = _pack_small(dict(
        ln_in_g=d_ln_in_g[0], ln_in_b=d_ln_in_b[0], b_in=row0("dbin"), sinks=per_layer("dsink", lambda a: a[:, 0]),
        vn_g=row0("dvng"), vn_b=row0("dvnb"), w_s=whole("dws"), b_s=per_layer("dbs", lambda a: a[:, ::128].T),
        b_out=row0("dbout"), ln_g=row0("dlng"), ln_b=row0("dlnb")))
    big_part = [_by_chip(whole("dwin"), 2), _by_chip(whole("dpa"), 2), _by_chip(whole("dpb"), 2), _by_chip(whole("dwout"), 1)]
    shard_shapes = [p.shape[2:] for p in big_part]
    parts = [p.reshape((2, -1, p.shape[-1])) for p in big_part] + [small_part.reshape((2, SMALL_ROWS // 2, 128))]
    landed = pair_send_other_half(parts)
    chip_sums = add_own_half(parts, landed, core, [MXU_DTYPE] * len(_BIG) + [F32])
    chip_sums = [a.reshape((N_CHIPS,) + s) for a, s in zip(chip_sums, shard_shapes)] + [chip_sums[-1]]
    from_chips = chip_exchange(chip_sums, [True] * len(_BIG) + [False])
    totals = pair_complete(sum_chips(from_chips))
    g_big = dict(zip(_BIG, totals[:len(_BIG)]))
    g_small = totals[-1].reshape((SMALL_ROWS, 128))

    ws = [_rows2d(weights[n]) for n in _BIG] + [_pack_small(weights)]
    gs = [_rows2d(g_big[n]) for n in _BIG] + [g_small]
    ms = [_rows2d(mom1[n]) for n in _BIG] + [_pack_small(mom1)]
    vs = [_rows2d(mom2[n]) for n in _BIG] + [_pack_small(mom2)]
    deltas, new_ms, new_vs = adamw(ws, gs, ms, vs)

    def named(outs):
        d = _unpack_small(outs[-1])
        d.update({n: o.reshape(weights[n].shape) for n, o in zip(_BIG, outs[:-1])})
        return d

    grads = named([g_big[n] for n in _BIG] + [g_small])
    order = ("ln_in_g", "ln_in_b", "w_in", "b_in", "sinks", "vn_g", "vn_b", "w_s", "b_s", "p_a", "p_b", "w_out", "b_out",
             "ln_g", "ln_b")
    res = [loss, grad_x.reshape(x.shape)]
    for group in (grads, named(deltas), named(new_ms), named(new_vs)):
        res.extend(group[n] for n in order)
    return tuple(res)
```

```python
import functools

import jax
import jax.numpy as jnp
from jax import lax
from jax.experimental import pallas as pl
from jax.experimental.pallas import tpu as pltpu

F32 = jnp.float32
MXU_DTYPE = jnp.bfloat16

D_MODEL = 1024
DEPTH = 2
HEAD_DIM = 64
ATTN_W = 512
KV_W = 128
BLK = 128
SGU_W = 512
SGU_G = 4
N_COLS = 4864
C_Q, C_K, C_V, C_GA, C_UB, C_VB, C_GB, C_RA, C_RB = 0, 512, 640, 768, 1280, 1792, 2304, 2816, 3840
ALPHA = (2.0 * DEPTH) ** 0.25
LN_EPS = 1e-5
SCALE = HEAD_DIM ** -0.5
NEG = float(jnp.finfo(jnp.float32).min)
GELU_C = 0.7978845608028654
GELU_A = 0.044715

ADAM_LR, ADAM_B1, ADAM_B2, ADAM_EPS, ADAM_WD, ADAM_STEP = 0.001, 0.9, 0.999, 1e-08, 0.01, 10

V7X_VMEM_BYTES = 64 * 1024 * 1024
V7X_VMEM_RESERVE = 3 * 1024 * 1024
ROWS = 256
ROWS_BWD = 128

MESH = pl.DeviceIdType.MESH
N_CHIPS = 4


def _mm(a, b):
    return jnp.dot(a.astype(MXU_DTYPE), b.astype(MXU_DTYPE), preferred_element_type=F32)


def _mm_nt(a, b):
    return lax.dot_general(a.astype(MXU_DTYPE), b.astype(MXU_DTYPE), (((1,), (1,)), ((), ())),
                           preferred_element_type=F32)


def _mm_tn(a, b):
    return lax.dot_general(a.astype(MXU_DTYPE), b.astype(MXU_DTYPE), (((0,), (0,)), ((), ())),
                           preferred_element_type=F32)


def _sigmoid(x):
    return 1.0 / (1.0 + jnp.exp(-x))


def _gelu_parts(x):
    x2 = x * x
    t = jnp.tanh(GELU_C * (x + GELU_A * x2 * x))
    g = 0.5 * x * (1.0 + t)
    dg = 0.5 * (1.0 + t) + 0.5 * x * (1.0 - t * t) * (GELU_C * (1.0 + 3.0 * GELU_A * x2))
    return g, dg


def _ln_fwd(z, g, b):
    mu = jnp.mean(z, axis=-1, keepdims=True)
    zc = z - mu
    var = jnp.mean(zc * zc, axis=-1, keepdims=True)
    rstd = lax.rsqrt(var + LN_EPS)
    zhat = zc * rstd
    return zhat * g + b, zhat, rstd


def _ln_bwd(dy, zhat, rstd, g):
    dzh = dy * g
    m1 = jnp.mean(dzh, axis=-1, keepdims=True)
    m2 = jnp.mean(dzh * zhat, axis=-1, keepdims=True)
    return rstd * (dzh - m1 - zhat * m2)


def _colsum(v):
    return jnp.sum(v, axis=0, keepdims=True)


def _lane_lo(rows):
    return lax.broadcasted_iota(jnp.int32, (rows, 128), 1) < HEAD_DIM


def _dup_head(x2, hk, lo):
    xr = pltpu.roll(x2, HEAD_DIM, 1)
    return jnp.where(lo, x2, xr) if hk == 0 else jnp.where(lo, xr, x2)


def _fold_head(y0, y1, lo):
    f0 = y0 + pltpu.roll(y0, HEAD_DIM, 1)
    f1 = y1 + pltpu.roll(y1, HEAD_DIM, 1)
    return jnp.where(lo, f0, f1)


def _stack_heads(t0, t1, lo):
    z = jnp.zeros_like(t0)
    return jnp.concatenate([jnp.where(lo, t0, z), jnp.where(lo, z, t0),
                            jnp.where(lo, t1, z), jnp.where(lo, z, t1)], axis=0)


def _unstack_heads(o, lo):
    return (jnp.where(lo, o[0:128], o[128:256]), jnp.where(lo, o[256:384], o[384:512]))


def _attn_probs(qs, kd, sink4, start):
    s = _mm_nt(qs, kd) * SCALE
    row = lax.broadcasted_iota(jnp.int32, (4 * BLK, 2 * BLK), 0) & (BLK - 1)
    kpos = lax.broadcasted_iota(jnp.int32, (4 * BLK, 2 * BLK), 1)
    valid = (kpos > row) & (kpos <= row + BLK)
    if start is not None:
        valid = valid & (jnp.logical_not(start) | (kpos >= BLK))
    s = jnp.where(valid, s, NEG)
    r1 = lax.broadcasted_iota(jnp.int32, (4 * BLK, 1), 0)
    sk = jnp.where(r1 < BLK, sink4[0], jnp.where(r1 < 2 * BLK, sink4[1], jnp.where(r1 < 3 * BLK, sink4[2], sink4[3])))
    m = jnp.maximum(jnp.max(s, axis=-1, keepdims=True), sk)
    p = jnp.exp(s - m)
    es = jnp.exp(sk - m)
    denom = jnp.sum(p, axis=-1, keepdims=True) + es
    return p / denom, es / denom


_PROJ_CHUNK = 512


def _project(x, wT_ref, b_ref, h_s):
    xb = x.astype(MXU_DTYPE)
    n = wT_ref.shape[0]
    for c0 in range(0, n, _PROJ_CHUNK):
        c1 = min(c0 + _PROJ_CHUNK, n)
        h_s[:, c0:c1] = _mm_nt(xb, wT_ref[c0:c1, :]) + b_ref[:, c0:c1]


def _kv_blocks(xh, w, h_s, nb):
    kvh = _mm_nt(xh, w["winT"][C_K:C_K + 2 * KV_W, :]) + w["bin"][:, C_K:C_K + 2 * KV_W]
    ks = [kvh[:, 0:KV_W]] + [h_s[n * BLK:(n + 1) * BLK, C_K:C_K + KV_W] for n in range(nb)]
    vs = [kvh[:, KV_W:2 * KV_W]] + [h_s[n * BLK:(n + 1) * BLK, C_V:C_V + KV_W] for n in range(nb)]
    return ks, vs


def _attn_operands(h_s, ks, vs, n, hk, lo, lo2):
    rows = slice(n * BLK, (n + 1) * BLK)
    kd = _dup_head(jnp.concatenate([ks[n], ks[n + 1]], axis=0), hk, lo2)
    vd = _dup_head(jnp.concatenate([vs[n], vs[n + 1]], axis=0), hk, lo2)
    c0 = C_Q + 2 * hk * 128
    qs = _stack_heads(h_s[rows, c0:c0 + 128], h_s[rows, c0 + 128:c0 + 256], lo)
    return qs, kd, vd


def _sgu_weights(w):
    tri = (lax.broadcasted_iota(jnp.int32, (BLK, BLK), 0) >= lax.broadcasted_iota(jnp.int32, (BLK, BLK), 1))
    return tri, [jnp.where(tri, w["ws"][g], 0.0) for g in range(SGU_G)]


def _layer_forward(x, xh, start, w, h_s, attn_s, mix_s, keep):
    R = x.shape[0]
    nb = R // BLK
    lo = _lane_lo(BLK)
    lo2 = _lane_lo(2 * BLK)
    _project(x, w["winT"], w["bin"], h_s)
    ks, vs = _kv_blocks(xh, w, h_s, nb)
    sinks = [w["sinks"][j] for j in range(8)]
    probs_l = []
    for n in range(nb):
        rows = slice(n * BLK, (n + 1) * BLK)
        for hk in range(2):
            qs, kd, vd = _attn_operands(h_s, ks, vs, n, hk, lo, lo2)
            probs, ps = _attn_probs(qs, kd, sinks[4 * hk:4 * hk + 4], start if n == 0 else None)
            t0, t1 = _unstack_heads(_mm(probs, vd), lo)
            attn_s[rows, 2 * hk * 128:(2 * hk + 1) * 128] = t0
            attn_s[rows, (2 * hk + 1) * 128:(2 * hk + 2) * 128] = t1
            if keep:
                probs_l.append((probs, ps))
    ga = h_s[:, C_GA:C_GA + ATTN_W]
    ya = attn_s[...] * (ga * _sigmoid(ga))
    gu, _ = _gelu_parts(h_s[:, C_UB:C_UB + SGU_W])
    gv, _ = _gelu_parts(h_s[:, C_VB:C_VB + SGU_W])
    vn, _, _ = _ln_fwd(gv, w["vng"][...], w["vnb"][...])
    _, wms = _sgu_weights(w)
    for n in range(nb):
        rows = slice(n * BLK, (n + 1) * BLK)
        for g in range(SGU_G):
            cols = slice(g * 128, (g + 1) * 128)
            mix_s[rows, cols] = _mm(wms[g], vn[rows, cols]) + w["bs"][:, cols]
    gb = h_s[:, C_GB:C_GB + SGU_W]
    yb = gu * mix_s[...] * (gb * _sigmoid(gb))
    a = _mm(ya, w["pa"][...])
    b = _mm(yb, w["pb"][...])
    merged = _sigmoid(h_s[:, C_RA:C_RA + D_MODEL]) * a + _sigmoid(h_s[:, C_RB:C_RB + D_MODEL]) * b
    out = _mm(merged, w["wout"][...]) + w["bout"][...]
    y, zhat, rstd = _ln_fwd(ALPHA * x + out, w["lng"][...], w["lnb"][...])
    if not keep:
        return y, None
    return y, dict(ks=ks, vs=vs, probs=probs_l, ya=ya, yb=yb, a=a, b=b, merged=merged, zhat=zhat, rstd=rstd)


def _dsilu(g, sg):
    return sg * (1.0 + g * (1.0 - sg))


_W_NAMES = ("winT", "bin", "sinks", "vng", "vnb", "ws", "bs", "pa", "pb", "wout", "bout", "lng", "lnb")
_ACC_NAMES = ("dwout", "dpa", "dpb", "dbin", "dbout", "dlng", "dlnb", "dvng", "dvnb", "dws", "dbs", "dsink", "loss")


def _make_fwd_body(R, S):
    def body(x_ref, xh_ref, *rest):
        w = dict(zip(_W_NAMES, rest[:len(_W_NAMES)]))
        y_ref, h_s, attn_s, mix_s = rest[len(_W_NAMES):]
        start = (pl.program_id(0) % (S // R)) == 0
        y, _ = _layer_forward(x_ref[...], xh_ref[...], start, w, h_s, attn_s, mix_s, keep=False)
        y_ref[...] = y
    return body


def _make_bwd_body(R, S, nT, is_last):
    nb = R // BLK
    names = _W_NAMES
    nw = len(names)

    def body(x_ref, xh_ref, aux_ref, *rest):
        w = dict(zip(names, rest[:nw]))
        dres_ref, dh_ref = rest[nw:nw + 2]
        acc = dict(zip(_ACC_NAMES, rest[nw + 2:nw + 2 + len(_ACC_NAMES)]))
        h_s, attn_s, mix_s, ckv_s = rest[nw + 2 + len(_ACC_NAMES):]
        i = pl.program_id(0)
        start = ((nT - 1 - i) % (S // R)) == 0
        lo = _lane_lo(BLK)
        lo2 = _lane_lo(2 * BLK)

        @pl.when(i == 0)
        def _():
            for name in _ACC_NAMES:
                acc[name][...] = jnp.zeros(acc[name].shape, F32)
            ckv_s[...] = jnp.zeros(ckv_s.shape, F32)

        x = x_ref[...]
        y, sv = _layer_forward(x, xh_ref[...], start, w, h_s, attn_s, mix_s, keep=True)
        if is_last:
            diff = y - aux_ref[...]
            part = 0.5 * jnp.sum(jnp.mean(diff * diff, axis=-1, keepdims=True), axis=0, keepdims=True)
            acc["loss"][...] += jnp.broadcast_to(part, acc["loss"].shape)
            dy = diff * (1.0 / D_MODEL)
        else:
            dy = aux_ref[...]

        def put(c0, val, rows=slice(None)):
            width = val.shape[1]
            dh_ref[rows, c0:c0 + width] = val.astype(dh_ref.dtype)
            acc["dbin"][:, c0:c0 + width] += _colsum(val)

        acc["dlng"][...] += _colsum(dy * sv["zhat"])
        acc["dlnb"][...] += _colsum(dy)
        dz = _ln_bwd(dy, sv["zhat"], sv["rstd"], w["lng"][...])
        dres_ref[...] = ALPHA * dz
        acc["dbout"][...] += _colsum(dz)
        acc["dwout"][...] += _mm_tn(sv["merged"], dz)
        dmerged = _mm_nt(dz, w["wout"][...])
        sa = _sigmoid(h_s[:, C_RA:C_RA + D_MODEL])
        da = dmerged * sa
        put(C_RA, da * sv["a"] * (1.0 - sa))
        sb = _sigmoid(h_s[:, C_RB:C_RB + D_MODEL])
        db = dmerged * sb
        put(C_RB, db * sv["b"] * (1.0 - sb))
        acc["dpa"][...] += _mm_tn(sv["ya"], da)
        acc["dpb"][...] += _mm_tn(sv["yb"], db)
        dya = _mm_nt(da, w["pa"][...])
        dyb = _mm_nt(db, w["pb"][...])
        ga = h_s[:, C_GA:C_GA + ATTN_W]
        sga = _sigmoid(ga)
        put(C_GA, dya * attn_s[...] * _dsilu(ga, sga))
        attn_s[...] = dya * (ga * sga)
        ks, vs = sv["ks"], sv["vs"]
        dks = [jnp.zeros((BLK, KV_W), F32) for _ in range(nb + 1)]
        dvs = [jnp.zeros((BLK, KV_W), F32) for _ in range(nb + 1)]
        for n in range(nb):
            rows = slice(n * BLK, (n + 1) * BLK)
            ydk, ydv = [], []
            for hk in range(2):
                qs, kd, vd = _attn_operands(h_s, ks, vs, n, hk, lo, lo2)
                probs, ps = sv["probs"][2 * n + hk]
                dos = _stack_heads(attn_s[rows, 2 * hk * 128:(2 * hk + 1) * 128],
                                   attn_s[rows, (2 * hk + 1) * 128:(2 * hk + 2) * 128], lo)
                dp = _mm_nt(dos, vd)
                delta = jnp.sum(probs * dp, axis=-1, keepdims=True)
                dsk = -(ps * delta)
                for g in range(4):
                    j = 4 * hk + g
                    acc["dsink"][j:j + 1, :] += jnp.broadcast_to(_colsum(dsk[g * BLK:(g + 1) * BLK]), (1, 128))
                dss = probs * (dp - delta) * SCALE
                q0, q1 = _unstack_heads(_mm(dss, kd), lo)
                put(C_Q + 2 * hk * 128, q0, rows)
                put(C_Q + (2 * hk + 1) * 128, q1, rows)
                ydk.append(_mm_tn(dss, qs))
                ydv.append(_mm_tn(probs, dos))
            dk2 = _fold_head(ydk[0], ydk[1], lo2)
            dv2 = _fold_head(ydv[0], ydv[1], lo2)
            dks[n] = dks[n] + dk2[0:BLK]
            dks[n + 1] = dks[n + 1] + dk2[BLK:2 * BLK]
            dvs[n] = dvs[n] + dv2[0:BLK]
            dvs[n + 1] = dvs[n + 1] + dv2[BLK:2 * BLK]
        dks[nb] = dks[nb] + ckv_s[:, 0:KV_W]
        dvs[nb] = dvs[nb] + ckv_s[:, KV_W:2 * KV_W]
        ckv_s[:, 0:KV_W] = dks[0]
        ckv_s[:, KV_W:2 * KV_W] = dvs[0]
        put(C_K, jnp.concatenate(dks[1:], axis=0))
        put(C_V, jnp.concatenate(dvs[1:], axis=0))
        gb = h_s[:, C_GB:C_GB + SGU_W]
        sgb = _sigmoid(gb)
        gu, dgu = _gelu_parts(h_s[:, C_UB:C_UB + SGU_W])
        mixed = mix_s[...]
        put(C_GB, dyb * (gu * mixed) * _dsilu(gb, sgb))
        dsgu = dyb * (gb * sgb)
        put(C_UB, dsgu * mixed * dgu)
        dmixed = dsgu * gu
        gv, dgv = _gelu_parts(h_s[:, C_VB:C_VB + SGU_W])
        vn, vhat, vrstd = _ln_fwd(gv, w["vng"][...], w["vnb"][...])
        tri, wms = _sgu_weights(w)
        dbs_part = jnp.zeros((BLK, SGU_W), F32)
        for n in range(nb):
            rows = slice(n * BLK, (n + 1) * BLK)
            dbs_part = dbs_part + dmixed[rows]
            for g in range(SGU_G):
                cols = slice(g * 128, (g + 1) * 128)
                acc["dws"][g] += jnp.where(tri, _mm_nt(dmixed[rows, cols], vn[rows, cols]), 0.0)
                mix_s[rows, cols] = _mm_tn(wms[g], dmixed[rows, cols])
        acc["dbs"][...] += dbs_part
        dvn = mix_s[...]
        acc["dvng"][...] += _colsum(dvn * vhat)
        acc["dvnb"][...] += _colsum(dvn)
        put(C_VB, _ln_bwd(dvn, vhat, vrstd, w["vng"][...]) * dgv)

        @pl.when(i == nT - 1)
        def _():
            for g in range(SGU_G):
                cols = slice(g * 128, (g + 1) * 128)
                tot = jnp.sum(acc["dbs"][:, cols], axis=1, keepdims=True)
                acc["dbs"][:, cols] = jnp.broadcast_to(tot, (BLK, 128))

    return body


def _resident():
    return pl.BlockSpec(memory_space=pltpu.VMEM)


def _weight_specs(names):
    return [pl.BlockSpec(memory_space=pltpu.SMEM) if n == "sinks" else _resident() for n in names]


def _nbytes(a):
    n = jnp.dtype(a.dtype).itemsize
    for d in a.shape:
        n *= d
    return n


def _layer_params(resident=()):
    scoped = V7X_VMEM_BYTES - V7X_VMEM_RESERVE - sum(_nbytes(a) for a in resident)
    return pltpu.CompilerParams(dimension_semantics=("arbitrary",), vmem_limit_bytes=scoped)


def layer_forward(x, wts, S, name):
    T = x.shape[0]
    R = ROWS
    npt = R // BLK
    return pl.pallas_call(
        _make_fwd_body(R, S),
        name=name,
        grid=(T // R,),
        in_specs=[pl.BlockSpec((R, D_MODEL), lambda t: (t, 0)),
                  pl.BlockSpec((BLK, D_MODEL), lambda t: (jnp.maximum(t * npt - 1, 0), 0))] + _weight_specs(_W_NAMES),
        out_specs=pl.BlockSpec((R, D_MODEL), lambda t: (t, 0)),
        out_shape=jax.ShapeDtypeStruct((T, D_MODEL), F32),
        scratch_shapes=[pltpu.VMEM((R, N_COLS), F32), pltpu.VMEM((R, ATTN_W), F32), pltpu.VMEM((R, SGU_W), F32)],
        compiler_params=_layer_params([wts[n] for n in _W_NAMES]),
    )(x, x, *[wts[n] for n in _W_NAMES])


_ACC_SHAPES = dict(dwout=(D_MODEL, D_MODEL), dpa=(ATTN_W, D_MODEL), dpb=(SGU_W, D_MODEL), dbin=(1, N_COLS),
                   dbout=(1, D_MODEL), dlng=(1, D_MODEL), dlnb=(1, D_MODEL), dvng=(1, SGU_W), dvnb=(1, SGU_W),
                   dws=(SGU_G, BLK, BLK), dbs=(BLK, SGU_W), dsink=(8, 128), loss=(8, 128))


def layer_backward(x, aux, wts, S, is_last, name):
    T = x.shape[0]
    R = ROWS_BWD
    nT = T // R
    npt = R // BLK
    rev = lambda i: (nT - 1 - i, 0)
    halo = lambda i: (jnp.maximum((nT - 1 - i) * npt - 1, 0), 0)
    names = _W_NAMES
    out_shape = ([jax.ShapeDtypeStruct((T, D_MODEL), F32), jax.ShapeDtypeStruct((T, N_COLS), MXU_DTYPE)]
                 + [jax.ShapeDtypeStruct(_ACC_SHAPES[n], F32) for n in _ACC_NAMES])
    outs = pl.pallas_call(
        _make_bwd_body(R, S, nT, is_last),
        name=name,
        grid=(nT,),
        in_specs=[pl.BlockSpec((R, D_MODEL), rev), pl.BlockSpec((BLK, D_MODEL), halo),
                  pl.BlockSpec((R, D_MODEL), rev)] + _weight_specs(names),
        out_specs=[pl.BlockSpec((R, D_MODEL), rev), pl.BlockSpec((R, N_COLS), rev)] + [_resident() for _ in _ACC_NAMES],
        out_shape=out_shape,
        scratch_shapes=[pltpu.VMEM((R, N_COLS), F32), pltpu.VMEM((R, ATTN_W), F32), pltpu.VMEM((R, SGU_W), F32),
                        pltpu.VMEM((BLK, 2 * KV_W), F32)],
        compiler_params=_layer_params([wts[n] for n in names] + out_shape[2:]),
    )(x, x, aux, *[wts[n] for n in names])
    return outs[0], outs[1], dict(zip(_ACC_NAMES, outs[2:]))


def input_grads(dres, dh, x, winT, name):
    T = x.shape[0]
    R = ROWS

    def body(dres_ref, dh_ref, x_ref, winT_ref, dx_ref, dwin_ref):
        @pl.when(pl.program_id(0) == 0)
        def _():
            dwin_ref[...] = jnp.zeros(dwin_ref.shape, F32)

        dh = dh_ref[...]
        dx_ref[...] = dres_ref[...] + jnp.dot(dh, winT_ref[...], preferred_element_type=F32)
        dwin_ref[...] += _mm_tn(dh, x_ref[...])

    row = lambda t: (t, 0)
    return pl.pallas_call(
        body,
        name=name,
        grid=(T // R,),
        in_specs=[pl.BlockSpec((R, D_MODEL), row), pl.BlockSpec((R, N_COLS), row), pl.BlockSpec((R, D_MODEL), row),
                  _resident()],
        out_specs=[pl.BlockSpec((R, D_MODEL), row), _resident()],
        out_shape=[jax.ShapeDtypeStruct((T, D_MODEL), F32), jax.ShapeDtypeStruct((N_COLS, D_MODEL), F32)],
        compiler_params=_layer_params([winT, jax.ShapeDtypeStruct((N_COLS, D_MODEL), F32)]),
    )(dres, dh, x, winT)


def input_norm_forward(x, g, b, name):
    T = x.shape[0]
    R = ROWS

    def body(x_ref, g_ref, b_ref, y_ref):
        y_ref[...] = _ln_fwd(x_ref[...], g_ref[...], b_ref[...])[0]

    row = lambda t: (t, 0)
    return pl.pallas_call(
        body, name=name, grid=(T // R,),
        in_specs=[pl.BlockSpec((R, D_MODEL), row), _resident(), _resident()],
        out_specs=pl.BlockSpec((R, D_MODEL), row),
        out_shape=jax.ShapeDtypeStruct((T, D_MODEL), F32),
        compiler_params=_layer_params(),
    )(x, g, b)


def input_norm_backward(x, dy, g, name):
    T = x.shape[0]
    R = ROWS

    def body(x_ref, dy_ref, g_ref, dx_ref, dg_ref, db_ref):
        @pl.when(pl.program_id(0) == 0)
        def _():
            dg_ref[...] = jnp.zeros(dg_ref.shape, F32)
            db_ref[...] = jnp.zeros(db_ref.shape, F32)

        dy = dy_ref[...]
        _, xhat, rstd = _ln_fwd(x_ref[...], g_ref[...], g_ref[...])
        dx_ref[...] = _ln_bwd(dy, xhat, rstd, g_ref[...])
        dg_ref[...] += _colsum(dy * xhat)
        db_ref[...] += _colsum(dy)

    row = lambda t: (t, 0)
    vec = jax.ShapeDtypeStruct((1, D_MODEL), F32)
    return pl.pallas_call(
        body, name=name, grid=(T // R,),
        in_specs=[pl.BlockSpec((R, D_MODEL), row), pl.BlockSpec((R, D_MODEL), row), _resident()],
        out_specs=[pl.BlockSpec((R, D_MODEL), row), _resident(), _resident()],
        out_shape=[jax.ShapeDtypeStruct((T, D_MODEL), F32), vec, vec],
        compiler_params=_layer_params(),
    )(x, dy, g)


def local_step(x, target, ln_in_g, ln_in_b, layers, S):
    x0 = input_norm_forward(x, ln_in_g, ln_in_b, "ln_in_fwd")
    x1 = layer_forward(x0, layers[0], S, "layer0_fwd")
    dres1, dh1, g1 = layer_backward(x1, target, layers[1], S, True, "layer1_bwd")
    dx1, g1["dwinT"] = input_grads(dres1, dh1, x1, layers[1]["winT"], "layer1_dx_dwin")
    dres0, dh0, g0 = layer_backward(x0, dx1, layers[0], S, False, "layer0_bwd")
    dx0, g0["dwinT"] = input_grads(dres0, dh0, x0, layers[0]["winT"], "layer0_dx_dwin")
    gx, dg_in, db_in = input_norm_backward(x, dx0, ln_in_g, "ln_in_bwd")
    return g1["loss"][0, 0], gx, dg_in, db_in, [g0, g1]


_ANY = pl.BlockSpec(memory_space=pl.ANY)
_DMA = pltpu.SemaphoreType.DMA

_PIECES = ("winT", "pa", "pb", "wout")
_WHOLE = dict(winT=(N_COLS, D_MODEL), pa=(ATTN_W, D_MODEL), pb=(SGU_W, D_MODEL), wout=(D_MODEL, D_MODEL))
_HALF = dict(winT=(N_COLS // 8, D_MODEL), pa=(ATTN_W // 2, D_MODEL // 4), pb=(SGU_W // 2, D_MODEL // 4),
             wout=(D_MODEL // 8, D_MODEL))
SMALL_ROWS = 1280


def _region(name, ref, k, h):
    hr, hc = _HALF[name]
    if name in ("pa", "pb"):
        return ref.at[pl.ds(h * hr, hr), pl.ds(k * hc, hc)]
    return ref.at[pl.ds(k * 2 * hr + h * hr, hr), :]


def _place():
    x, y, c = lax.axis_index("x"), lax.axis_index("y"), lax.axis_index("c")
    return x, y, c, 2 * x + y


def _as_chip(j, fn):
    for jj in range(N_CHIPS):
        pl.when(j == jj)(functools.partial(fn, jj))


def _remote(src, dst, send_sem, recv_sem, to):
    return pltpu.make_async_remote_copy(src_ref=src, dst_ref=dst, send_sem=send_sem, recv_sem=recv_sem,
                                        device_id=to, device_id_type=MESH)


def _core_of(k, c):
    return (k // 2, k % 2, c)


def gather_weights(names, shards):
    n = len(names)

    def body(*refs):
        src, dst = refs[:n], refs[n:2 * n]
        send1, recv1, send2, recv2 = refs[2 * n:]
        x, y, c, j = _place()
        sib = (x, y, 1 - c)

        def run(jj):
            others = [k for k in range(N_CHIPS) if k != jj]
            first = []
            for i, name in enumerate(names):
                hr = _HALF[name][0]
                for k in others:
                    first.append(_remote(src[i].at[pl.ds(c * hr, hr)], _region(name, dst[i], jj, c),
                                         send1.at[4 * i + k], recv1.at[4 * i + jj], _core_of(k, c)))
            for cp in first:
                cp.start()
            passed = []
            for i, name in enumerate(names):
                for k in others:
                    land = _region(name, dst[i], k, c)
                    _remote(land, land, send1.at[4 * i + k], recv1.at[4 * i + k], _core_of(k, c)).wait_recv()
                    cp = _remote(land, land, send2.at[4 * i + k], recv2.at[4 * i + k], sib)
                    cp.start()
                    passed.append(cp)
            for i, name in enumerate(names):
                for k in others:
                    land = _region(name, dst[i], k, 1 - c)
                    _remote(land, land, send2.at[4 * i + k], recv2.at[4 * i + k], sib).wait_recv()
            for cp in first + passed:
                cp.wait_send()

        _as_chip(j, run)

    return pl.pallas_call(
        body, name="gather_weights", in_specs=[_ANY] * n, out_specs=[_ANY] * n,
        out_shape=[jax.ShapeDtypeStruct(_WHOLE[nm], s.dtype) for nm, s in zip(names, shards)],
        scratch_shapes=[_DMA((4 * n,)), _DMA((4 * n,)), _DMA((4 * n,)), _DMA((4 * n,))],
    )(*shards)


def pair_send_halves(names, parts, small):
    n = len(names)

    def body(*refs):
        src, small_src = refs[:n], refs[n]
        dst, small_dst = refs[n + 1:2 * n + 1], refs[2 * n + 1]
        send, recv = refs[2 * n + 2:]
        x, y, c, _ = _place()
        sib = (x, y, 1 - c)
        cps = []
        for i, name in enumerate(names):
            for k in range(N_CHIPS):
                cps.append(_remote(_region(name, src[i], k, 1 - c), dst[i].at[k], send.at[4 * i + k], recv.at[4 * i + k], sib))
        hs = SMALL_ROWS // 2
        cps.append(_remote(small_src.at[pl.ds((1 - c) * hs, hs)], small_dst, send.at[4 * n], recv.at[4 * n], sib))
        for cp in cps:
            cp.start()
        for cp in cps:
            cp.wait()

    return pl.pallas_call(
        body, name="pair_send_halves", in_specs=[_ANY] * (n + 1), out_specs=[_ANY] * (n + 1),
        out_shape=[jax.ShapeDtypeStruct((N_CHIPS,) + _HALF[nm], F32) for nm in names]
        + [jax.ShapeDtypeStruct((SMALL_ROWS // 2, 128), F32)],
        scratch_shapes=[_DMA((4 * n + 1,)), _DMA((4 * n + 1,))],
    )(*parts, small)


def chip_exchange(sums, small):
    n = len(sums)

    def body(*refs):
        src, small_src = refs[:n], refs[n]
        dst, small_dst = refs[n + 1:2 * n + 1], refs[2 * n + 1]
        send, recv, lsem = refs[2 * n + 2:]
        x, y, c, j = _place()

        def run(jj):
            others = [k for k in range(N_CHIPS) if k != jj]
            own = pltpu.make_async_copy(small_src, small_dst.at[jj], lsem)
            out = []
            for k in others:
                for i in range(n):
                    out.append(_remote(src[i].at[k], dst[i].at[jj], send.at[4 * i + k], recv.at[4 * i + jj], _core_of(k, c)))
                out.append(_remote(small_src, small_dst.at[jj], send.at[4 * n + k], recv.at[4 * n + jj], _core_of(k, c)))
            own.start()
            for cp in out:
                cp.start()
            for k in others:
                for i in range(n):
                    _remote(src[i].at[k], dst[i].at[k], send.at[4 * i + k], recv.at[4 * i + k], _core_of(k, c)).wait_recv()
                _remote(small_src, small_dst.at[k], send.at[4 * n + k], recv.at[4 * n + k], _core_of(k, c)).wait_recv()
            for cp in out:
                cp.wait_send()
            own.wait()

        _as_chip(j, run)

    return pl.pallas_call(
        body, name="chip_exchange", in_specs=[_ANY] * (n + 1), out_specs=[_ANY] * (n + 1),
        out_shape=[jax.ShapeDtypeStruct(a.shape, a.dtype) for a in sums]
        + [jax.ShapeDtypeStruct((N_CHIPS,) + small.shape, small.dtype)],
        scratch_shapes=[_DMA((4 * n + 4,)), _DMA((4 * n + 4,)), _DMA(())],
    )(*sums, small)


def pair_send_totals(arrs):
    n = len(arrs)

    def body(*refs):
        src, dst = refs[:n], refs[n:2 * n]
        send, recv = refs[2 * n:]
        x, y, c, _ = _place()
        cps = [_remote(src[i], dst[i], send.at[i], recv.at[i], (x, y, 1 - c)) for i in range(n)]
        for cp in cps:
            cp.start()
        for cp in cps:
            cp.wait()

    return pl.pallas_call(
        body, name="pair_send_totals", in_specs=[_ANY] * n, out_specs=[_ANY] * n,
        out_shape=[jax.ShapeDtypeStruct(a.shape, a.dtype) for a in arrs],
        scratch_shapes=[_DMA((n,)), _DMA((n,))],
    )(*arrs)


def _plain_params():
    return pltpu.CompilerParams(dimension_semantics=("arbitrary",) * 2, vmem_limit_bytes=V7X_VMEM_BYTES // 2)


def add_own_halves(names, parts, landed, core):
    n = len(names)

    def body(core_ref, *refs):
        for i in range(n):
            refs[2 * n + i][0] = (refs[i][...] + refs[n + i][0]).astype(MXU_DTYPE)

    in_specs, out_specs = [], []
    for name in names:
        hr, hc = _HALF[name]
        if name in ("pa", "pb"):
            in_specs.append(pl.BlockSpec((hr // 2, hc), lambda k, s, cr: (cr[0] * 2 + s, k)))
        else:
            in_specs.append(pl.BlockSpec((hr // 2, hc), lambda k, s, cr: (k * 4 + cr[0] * 2 + s, 0)))
    for name in names:
        hr, hc = _HALF[name]
        in_specs.append(pl.BlockSpec((1, hr // 2, hc), lambda k, s, cr: (k, s, 0)))
        out_specs.append(pl.BlockSpec((1, hr // 2, hc), lambda k, s, cr: (k, s, 0)))
    return pl.pallas_call(
        body, name="add_own_halves",
        grid_spec=pltpu.PrefetchScalarGridSpec(num_scalar_prefetch=1, grid=(N_CHIPS, 2), in_specs=in_specs, out_specs=out_specs),
        out_shape=[jax.ShapeDtypeStruct((N_CHIPS,) + _HALF[nm], MXU_DTYPE) for nm in names],
        compiler_params=_plain_params(),
    )(core, *parts, *landed)


def add_small_half(small, landed, core):
    hs = SMALL_ROWS // 2

    def body(core_ref, a_ref, b_ref, o_ref):
        o_ref[...] = a_ref[...] + b_ref[...]

    return pl.pallas_call(
        body, name="add_small_half",
        grid_spec=pltpu.PrefetchScalarGridSpec(
            num_scalar_prefetch=1, grid=(1,),
            in_specs=[pl.BlockSpec((hs, 128), lambda s, cr: (cr[0], 0)), pl.BlockSpec((hs, 128), lambda s, cr: (0, 0))],
            out_specs=pl.BlockSpec((hs, 128), lambda s, cr: (0, 0))),
        out_shape=jax.ShapeDtypeStruct((hs, 128), F32),
    )(core, small, landed)


def sum_chips(sums, landed, small_landed, chips):
    n = len(sums)

    def body(chips_ref, *refs):
        for i in range(n):
            tot = refs[4 * i][0].astype(F32)
            for r in range(1, N_CHIPS):
                tot = tot + refs[4 * i + r][0].astype(F32)
            refs[4 * n + 1 + i][...] = tot

        @pl.when(pl.program_id(0) == 0)
        def _():
            sm = refs[4 * n]
            refs[5 * n + 1][...] = ((sm[0] + sm[1]) + sm[2]) + sm[3]

    in_specs, out_specs, operands = [], [], []
    for a, l in zip(sums, landed):
        _, hr, hc = a.shape
        for r in range(N_CHIPS):
            in_specs.append(pl.BlockSpec((1, hr // 2, hc), functools.partial(lambda s, ch, r: (ch[r], s, 0), r=r)))
            operands.append(a if r == 0 else l)
        out_specs.append(pl.BlockSpec((hr // 2, hc), lambda s, ch: (s, 0)))
    in_specs.append(pl.BlockSpec(small_landed.shape, lambda s, ch: (0, 0, 0)))
    out_specs.append(pl.BlockSpec(small_landed.shape[1:], lambda s, ch: (0, 0)))
    return pl.pallas_call(
        body, name="sum_chips",
        grid_spec=pltpu.PrefetchScalarGridSpec(num_scalar_prefetch=1, grid=(2,), in_specs=in_specs, out_specs=out_specs),
        out_shape=[jax.ShapeDtypeStruct(a.shape[1:], F32) for a in sums] + [jax.ShapeDtypeStruct(small_landed.shape[1:], F32)],
        compiler_params=pltpu.CompilerParams(dimension_semantics=("arbitrary",), vmem_limit_bytes=V7X_VMEM_BYTES // 2),
    )(chips, *operands, small_landed)


N_STEPS = 8


def _rows2d(a):
    return a.reshape((-1, a.shape[-1]))


def adamw(ws, gs, ms, vs):
    n = len(ws)
    c1 = 1.0 / (1.0 - ADAM_B1 ** ADAM_STEP)
    c2 = 1.0 / (1.0 - ADAM_B2 ** ADAM_STEP)

    def body(*refs):
        for i in range(n):
            w, g, m, v = (refs[k * n + i][...] for k in range(4))
            m = ADAM_B1 * m + (1.0 - ADAM_B1) * g
            v = ADAM_B2 * v + (1.0 - ADAM_B2) * (g * g)
            refs[4 * n + i][...] = -ADAM_LR * ((m * c1) / (jnp.sqrt(v * c2) + ADAM_EPS) + ADAM_WD * w)
            refs[5 * n + i][...] = m
            refs[6 * n + i][...] = v

    spec = lambda a: pl.BlockSpec((a.shape[0] // N_STEPS, a.shape[1]), lambda s: (s, 0))
    outs = pl.pallas_call(
        body, name="adamw", grid=(N_STEPS,),
        in_specs=[spec(a) for a in ws] * 4, out_specs=[spec(a) for a in ws] * 3,
        out_shape=[jax.ShapeDtypeStruct(a.shape, F32) for a in ws] * 3,
        compiler_params=pltpu.CompilerParams(dimension_semantics=("arbitrary",), vmem_limit_bytes=V7X_VMEM_BYTES // 2),
    )(*ws, *gs, *ms, *vs)
    return outs[:n], outs[n:2 * n], outs[2 * n:]


_SMALL = (("ln_in_g", (D_MODEL,)), ("ln_in_b", (D_MODEL,)), ("b_in", (DEPTH, N_COLS)), ("sinks", (DEPTH, 8)),
          ("vn_g", (DEPTH, SGU_W)), ("vn_b", (DEPTH, SGU_W)), ("w_s", (DEPTH, SGU_G, BLK, BLK)),
          ("b_s", (DEPTH, SGU_G, BLK)), ("b_out", (DEPTH, D_MODEL)), ("ln_g", (DEPTH, D_MODEL)), ("ln_b", (DEPTH, D_MODEL)))
_BIG = ("w_in", "p_a", "p_b", "w_out")


def _pack_small(vals):
    flat = []
    for name, shape in _SMALL:
        v = vals[name].reshape((-1,))
        flat.append(jnp.pad(v, (0, (-v.shape[0]) % 128)))
    flat = jnp.concatenate(flat)
    return jnp.pad(flat, (0, SMALL_ROWS * 128 - flat.shape[0])).reshape((SMALL_ROWS, 128))


def _unpack_small(packed):
    flat = packed.reshape((-1,))
    out, pos = {}, 0
    for name, shape in _SMALL:
        size = 1
        for d in shape:
            size *= d
        out[name] = flat[pos:pos + size].reshape(shape)
        pos += size + (-size) % 128
    return out


def _shard_of(name, whole, j):
    hr, hc = _HALF[name]
    if name in ("pa", "pb"):
        return (0, j * hc), (2 * hr, hc)
    return (j * 2 * hr, 0), (2 * hr, hc)


def kernel(x, ln_in_g, ln_in_b, w_in, b_in, sinks, vn_g, vn_b, w_s, b_s, p_a, p_b, w_out, b_out, ln_g, ln_b, loss_target, m_ln_in_g, m_ln_in_b, m_w_in, m_b_in, m_sinks, m_vn_g, m_vn_b, m_w_s, m_b_s, m_p_a, m_p_b, m_w_out, m_b_out, m_ln_g, m_ln_b, v_ln_in_g, v_ln_in_b, v_w_in, v_b_in, v_sinks, v_vn_g, v_vn_b, v_w_s, v_b_s, v_p_a, v_p_b, v_w_out, v_b_out, v_ln_g, v_ln_b):
    weights = dict(ln_in_g=ln_in_g, ln_in_b=ln_in_b, w_in=w_in, b_in=b_in, sinks=sinks, vn_g=vn_g, vn_b=vn_b, w_s=w_s,
                   b_s=b_s, p_a=p_a, p_b=p_b, w_out=w_out, b_out=b_out, ln_g=ln_g, ln_b=ln_b)
    mom1 = dict(ln_in_g=m_ln_in_g, ln_in_b=m_ln_in_b, w_in=m_w_in, b_in=m_b_in, sinks=m_sinks, vn_g=m_vn_g, vn_b=m_vn_b,
                w_s=m_w_s, b_s=m_b_s, p_a=m_p_a, p_b=m_p_b, w_out=m_w_out, b_out=m_b_out, ln_g=m_ln_g, ln_b=m_ln_b)
    mom2 = dict(ln_in_g=v_ln_in_g, ln_in_b=v_ln_in_b, w_in=v_w_in, b_in=v_b_in, sinks=v_sinks, vn_g=v_vn_g, vn_b=v_vn_b,
                w_s=v_w_s, b_s=v_b_s, p_a=v_p_a, p_b=v_p_b, w_out=v_w_out, b_out=v_b_out, ln_g=v_ln_g, ln_b=v_ln_b)
    n_seq, S, _ = x.shape
    T = n_seq * S
    c = lax.axis_index("c")
    j = 2 * lax.axis_index("x") + lax.axis_index("y")
    core = c.astype(jnp.int32).reshape((1,))
    chips = jnp.stack([j] + [r + (r >= j) for r in range(N_CHIPS - 1)]).astype(jnp.int32)
    names = [nm for _ in range(DEPTH) for nm in _PIECES]

    own = []
    for l in range(DEPTH):
        own += [w_in[l].T.astype(MXU_DTYPE), p_a[l].astype(MXU_DTYPE), p_b[l].astype(MXU_DTYPE), w_out[l].astype(MXU_DTYPE)]
    gathered = gather_weights(names, own)
    gathered = [lax.dynamic_update_slice(g, s, _shard_of(nm, g, j)[0]) for nm, g, s in zip(names, gathered, own)]
    layers = []
    for l in range(DEPTH):
        d = dict(zip(_PIECES, gathered[4 * l:4 * l + 4]))
        d.update(bin=b_in[l][None], sinks=sinks[l], vng=vn_g[l][None], vnb=vn_b[l][None], ws=w_s[l],
                 bs=jnp.repeat(b_s[l].T, 128, axis=1), bout=b_out[l][None], lng=ln_g[l][None], lnb=ln_b[l][None])
        layers.append(d)

    loss_part, grad_x, d_ln_in_g, d_ln_in_b, gl = local_step(
        x.reshape((T, D_MODEL)), loss_target.reshape((T, D_MODEL)), ln_in_g[None], ln_in_b[None], layers, S)
    loss = lax.psum(loss_part, ("x", "y", "c"))

    per_layer = lambda key, pick: jnp.stack([pick(gl[l][key]) for l in range(DEPTH)])
    row0 = lambda key: per_layer(key, lambda a: a[0])
    small_part = _pack_small(dict(
        ln_in_g=d_ln_in_g[0], ln_in_b=d_ln_in_b[0], b_in=row0("dbin"), sinks=per_layer("dsink", lambda a: a[:, 0]),
        vn_g=row0("dvng"), vn_b=row0("dvnb"), w_s=per_layer("dws", lambda a: a),
        b_s=per_layer("dbs", lambda a: a[:, ::128].T), b_out=row0("dbout"), ln_g=row0("dlng"), ln_b=row0("dlnb")))
    parts = []
    for l in range(DEPTH):
        parts += [gl[l]["dwinT"], gl[l]["dpa"], gl[l]["dpb"], gl[l]["dwout"]]
    *landed, small_landed = pair_send_halves(names, parts, small_part)
    chip_sums = add_own_halves(names, parts, landed, core)
    small_sum = add_small_half(small_part, small_landed, core)
    *from_chips, small_from_chips = chip_exchange(chip_sums, small_sum)
    *totals, small_total = sum_chips(chip_sums, from_chips, small_from_chips, chips)
    *sib_totals, sib_small = pair_send_totals(totals + [small_total])

    def both_halves(mine, theirs):
        return jnp.where(c == 0, jnp.concatenate([mine, theirs], axis=0), jnp.concatenate([theirs, mine], axis=0))

    g_big = {}
    for pi, pname in enumerate(_BIG):
        per = [both_halves(totals[4 * l + pi], sib_totals[4 * l + pi]) for l in range(DEPTH)]
        g_big[pname] = jnp.stack([p.T for p in per] if pname == "w_in" else per)
    g_small = both_halves(small_total, sib_small)

    ws = [_rows2d(weights[n]) for n in _BIG] + [_pack_small(weights)]
    gs = [_rows2d(g_big[n]) for n in _BIG] + [g_small]
    ms = [_rows2d(mom1[n]) for n in _BIG] + [_pack_small(mom1)]
    vs = [_rows2d(mom2[n]) for n in _BIG] + [_pack_small(mom2)]
    deltas, new_ms, new_vs = adamw(ws, gs, ms, vs)

    def named(outs):
        d = _unpack_small(outs[-1])
        d.update({n: o.reshape(weights[n].shape) for n, o in zip(_BIG, outs[:-1])})
        return d

    grads = named([g_big[n] for n in _BIG] + [g_small])
    order = ("ln_in_g", "ln_in_b", "w_in", "b_in", "sinks", "vn_g", "vn_b", "w_s", "b_s", "p_a", "p_b", "w_out", "b_out",
             "ln_g", "ln_b")
    res = [loss, grad_x.reshape(x.shape)]
    for group in (grads, named(deltas), named(new_ms), named(new_vs)):
        res.extend(group[n] for n in order)
    return tuple(res)
```

```python
import functools

import jax
import jax.numpy as jnp
from jax import lax
from jax.experimental import pallas as pl
from jax.experimental.pallas import tpu as pltpu

F32 = jnp.float32
MXU_DTYPE = jnp.bfloat16

D_MODEL = 1024
DEPTH = 2
HEAD_DIM = 64
ATTN_W = 512
KV_W = 128
BLK = 128
SGU_W = 512
SGU_G = 4
N_COLS = 4864
C_Q, C_K, C_V, C_GA, C_UB, C_VB, C_GB, C_RA, C_RB = 0, 512, 640, 768, 1280, 1792, 2304, 2816, 3840
ALPHA = (2.0 * DEPTH) ** 0.25
LN_EPS = 1e-5
SCALE = HEAD_DIM ** -0.5
NEG = float(jnp.finfo(jnp.float32).min)
GELU_C = 0.7978845608028654
GELU_A = 0.044715

ADAM_LR, ADAM_B1, ADAM_B2, ADAM_EPS, ADAM_WD, ADAM_STEP = 0.001, 0.9, 0.999, 1e-08, 0.01, 10

V7X_VMEM_BYTES = 64 * 1024 * 1024
V7X_VMEM_RESERVE = 3 * 1024 * 1024
ROWS = 256
ROWS_BWD = 128

MESH = pl.DeviceIdType.MESH
N_CHIPS = 4


def _mm(a, b):
    return jnp.dot(a.astype(MXU_DTYPE), b.astype(MXU_DTYPE), preferred_element_type=F32)


def _mm_nt(a, b):
    return lax.dot_general(a.astype(MXU_DTYPE), b.astype(MXU_DTYPE), (((1,), (1,)), ((), ())),
                           preferred_element_type=F32)


def _mm_tn(a, b):
    return lax.dot_general(a.astype(MXU_DTYPE), b.astype(MXU_DTYPE), (((0,), (0,)), ((), ())),
                           preferred_element_type=F32)


def _sigmoid(x):
    return 1.0 / (1.0 + jnp.exp(-x))


def _gelu_parts(x):
    x2 = x * x
    t = jnp.tanh(GELU_C * (x + GELU_A * x2 * x))
    g = 0.5 * x * (1.0 + t)
    dg = 0.5 * (1.0 + t) + 0.5 * x * (1.0 - t * t) * (GELU_C * (1.0 + 3.0 * GELU_A * x2))
    return g, dg


def _ln_fwd(z, g, b):
    mu = jnp.mean(z, axis=-1, keepdims=True)
    zc = z - mu
    var = jnp.mean(zc * zc, axis=-1, keepdims=True)
    rstd = lax.rsqrt(var + LN_EPS)
    zhat = zc * rstd
    return zhat * g + b, zhat, rstd


def _ln_bwd(dy, zhat, rstd, g):
    dzh = dy * g
    m1 = jnp.mean(dzh, axis=-1, keepdims=True)
    m2 = jnp.mean(dzh * zhat, axis=-1, keepdims=True)
    return rstd * (dzh - m1 - zhat * m2)


def _colsum(v):
    return jnp.sum(v, axis=0, keepdims=True)


def _lane_lo(rows):
    return lax.broadcasted_iota(jnp.int32, (rows, 128), 1) < HEAD_DIM


def _dup_head(x2, hk, lo):
    xr = pltpu.roll(x2, HEAD_DIM, 1)
    return jnp.where(lo, x2, xr) if hk == 0 else jnp.where(lo, xr, x2)


def _fold_head(y0, y1, lo):
    f0 = y0 + pltpu.roll(y0, HEAD_DIM, 1)
    f1 = y1 + pltpu.roll(y1, HEAD_DIM, 1)
    return jnp.where(lo, f0, f1)


def _stack_heads(t0, t1, lo):
    z = jnp.zeros_like(t0)
    return jnp.concatenate([jnp.where(lo, t0, z), jnp.where(lo, z, t0),
                            jnp.where(lo, t1, z), jnp.where(lo, z, t1)], axis=0)


def _unstack_heads(o, lo):
    return (jnp.where(lo, o[0:128], o[128:256]), jnp.where(lo, o[256:384], o[384:512]))


def _attn_probs(qs, kd, sink4, start):
    s = _mm_nt(qs, kd) * SCALE
    row = lax.broadcasted_iota(jnp.int32, (4 * BLK, 2 * BLK), 0) & (BLK - 1)
    kpos = lax.broadcasted_iota(jnp.int32, (4 * BLK, 2 * BLK), 1)
    valid = (kpos > row) & (kpos <= row + BLK)
    if start is not None:
        valid = valid & (jnp.logical_not(start) | (kpos >= BLK))
    s = jnp.where(valid, s, NEG)
    r1 = lax.broadcasted_iota(jnp.int32, (4 * BLK, 1), 0)
    sk = jnp.where(r1 < BLK, sink4[0], jnp.where(r1 < 2 * BLK, sink4[1], jnp.where(r1 < 3 * BLK, sink4[2], sink4[3])))
    m = jnp.maximum(jnp.max(s, axis=-1, keepdims=True), sk)
    p = jnp.exp(s - m)
    es = jnp.exp(sk - m)
    denom = jnp.sum(p, axis=-1, keepdims=True) + es
    return p / denom, es / denom


_PROJ_CHUNK = 512


def _project(x, w_ref, b_ref, h_s):
    xb = x.astype(MXU_DTYPE)
    n = w_ref.shape[1]
    for c0 in range(0, n, _PROJ_CHUNK):
        c1 = min(c0 + _PROJ_CHUNK, n)
        h_s[:, c0:c1] = jnp.dot(xb, w_ref[:, c0:c1], preferred_element_type=F32) + b_ref[:, c0:c1]


def _kv_blocks(xh, w, h_s, nb):
    kvh = _mm(xh, w["win"][:, C_K:C_K + 2 * KV_W]) + w["bin"][:, C_K:C_K + 2 * KV_W]
    ks = [kvh[:, 0:KV_W]] + [h_s[n * BLK:(n + 1) * BLK, C_K:C_K + KV_W] for n in range(nb)]
    vs = [kvh[:, KV_W:2 * KV_W]] + [h_s[n * BLK:(n + 1) * BLK, C_V:C_V + KV_W] for n in range(nb)]
    return ks, vs


def _attn_operands(h_s, ks, vs, n, hk, lo, lo2):
    rows = slice(n * BLK, (n + 1) * BLK)
    kd = _dup_head(jnp.concatenate([ks[n], ks[n + 1]], axis=0), hk, lo2)
    vd = _dup_head(jnp.concatenate([vs[n], vs[n + 1]], axis=0), hk, lo2)
    c0 = C_Q + 2 * hk * 128
    qs = _stack_heads(h_s[rows, c0:c0 + 128], h_s[rows, c0 + 128:c0 + 256], lo)
    return qs, kd, vd


def _sgu_weights(w):
    tri = (lax.broadcasted_iota(jnp.int32, (BLK, BLK), 0) >= lax.broadcasted_iota(jnp.int32, (BLK, BLK), 1))
    return tri, [jnp.where(tri, w["ws"][g], 0.0) for g in range(SGU_G)]


def _layer_forward(x, xh, start, w, h_s, attn_s, mix_s, keep):
    R = x.shape[0]
    nb = R // BLK
    lo = _lane_lo(BLK)
    lo2 = _lane_lo(2 * BLK)
    _project(x, w["win"], w["bin"], h_s)
    ks, vs = _kv_blocks(xh, w, h_s, nb)
    sinks = [w["sinks"][j] for j in range(8)]
    probs_l = []
    for n in range(nb):
        rows = slice(n * BLK, (n + 1) * BLK)
        for hk in range(2):
            qs, kd, vd = _attn_operands(h_s, ks, vs, n, hk, lo, lo2)
            probs, ps = _attn_probs(qs, kd, sinks[4 * hk:4 * hk + 4], start if n == 0 else None)
            t0, t1 = _unstack_heads(_mm(probs, vd), lo)
            attn_s[rows, 2 * hk * 128:(2 * hk + 1) * 128] = t0
            attn_s[rows, (2 * hk + 1) * 128:(2 * hk + 2) * 128] = t1
            if keep:
                probs_l.append((probs, ps))
    ga = h_s[:, C_GA:C_GA + ATTN_W]
    ya = attn_s[...] * (ga * _sigmoid(ga))
    gu, _ = _gelu_parts(h_s[:, C_UB:C_UB + SGU_W])
    gv, _ = _gelu_parts(h_s[:, C_VB:C_VB + SGU_W])
    vn, _, _ = _ln_fwd(gv, w["vng"][...], w["vnb"][...])
    _, wms = _sgu_weights(w)
    for n in range(nb):
        rows = slice(n * BLK, (n + 1) * BLK)
        for g in range(SGU_G):
            cols = slice(g * 128, (g + 1) * 128)
            mix_s[rows, cols] = _mm(wms[g], vn[rows, cols]) + w["bs"][:, cols]
    gb = h_s[:, C_GB:C_GB + SGU_W]
    yb = gu * mix_s[...] * (gb * _sigmoid(gb))
    a = _mm(ya, w["pa"][...])
    b = _mm(yb, w["pb"][...])
    merged = _sigmoid(h_s[:, C_RA:C_RA + D_MODEL]) * a + _sigmoid(h_s[:, C_RB:C_RB + D_MODEL]) * b
    out = _mm(merged, w["wout"][...]) + w["bout"][...]
    y, zhat, rstd = _ln_fwd(ALPHA * x + out, w["lng"][...], w["lnb"][...])
    if not keep:
        return y, None
    return y, dict(ks=ks, vs=vs, probs=probs_l, ya=ya, yb=yb, a=a, b=b, merged=merged, zhat=zhat, rstd=rstd)


def _dsilu(g, sg):
    return sg * (1.0 + g * (1.0 - sg))


_W_NAMES = ("win", "bin", "sinks", "vng", "vnb", "ws", "bs", "pa", "pb", "wout", "bout", "lng", "lnb")
_WT_NAMES = ("paT", "pbT", "woutT")
_ACC_NAMES = ("dwout", "dpa", "dpb", "dbin", "dbout", "dlng", "dlnb", "dvng", "dvnb", "dws", "dbs", "dsink", "loss")


def _make_fwd_body(R, S):
    def body(x_ref, xh_ref, *rest):
        w = dict(zip(_W_NAMES, rest[:len(_W_NAMES)]))
        y_ref, h_s, attn_s, mix_s = rest[len(_W_NAMES):]
        start = (pl.program_id(0) % (S // R)) == 0
        y, _ = _layer_forward(x_ref[...], xh_ref[...], start, w, h_s, attn_s, mix_s, keep=False)
        y_ref[...] = y
    return body


def _make_bwd_body(R, S, nT, is_last):
    nb = R // BLK
    names = _W_NAMES + _WT_NAMES
    nw = len(names)

    def body(x_ref, xh_ref, aux_ref, *rest):
        w = dict(zip(names, rest[:nw]))
        dres_ref, dh_ref = rest[nw:nw + 2]
        acc = dict(zip(_ACC_NAMES, rest[nw + 2:nw + 2 + len(_ACC_NAMES)]))
        h_s, attn_s, mix_s, ckv_s = rest[nw + 2 + len(_ACC_NAMES):]
        i = pl.program_id(0)
        start = ((nT - 1 - i) % (S // R)) == 0
        lo = _lane_lo(BLK)
        lo2 = _lane_lo(2 * BLK)

        @pl.when(i == 0)
        def _():
            for name in _ACC_NAMES:
                acc[name][...] = jnp.zeros(acc[name].shape, F32)
            ckv_s[...] = jnp.zeros(ckv_s.shape, F32)

        x = x_ref[...]
        y, sv = _layer_forward(x, xh_ref[...], start, w, h_s, attn_s, mix_s, keep=True)
        if is_last:
            diff = y - aux_ref[...]
            part = 0.5 * jnp.sum(jnp.mean(diff * diff, axis=-1, keepdims=True), axis=0, keepdims=True)
            acc["loss"][...] += jnp.broadcast_to(part, acc["loss"].shape)
            dy = diff * (1.0 / D_MODEL)
        else:
            dy = aux_ref[...]

        def put(c0, val, rows=slice(None)):
            width = val.shape[1]
            dh_ref[rows, c0:c0 + width] = val.astype(dh_ref.dtype)
            acc["dbin"][:, c0:c0 + width] += _colsum(val)

        acc["dlng"][...] += _colsum(dy * sv["zhat"])
        acc["dlnb"][...] += _colsum(dy)
        dz = _ln_bwd(dy, sv["zhat"], sv["rstd"], w["lng"][...])
        dres_ref[...] = ALPHA * dz
        acc["dbout"][...] += _colsum(dz)
        acc["dwout"][...] += _mm_tn(sv["merged"], dz)
        dmerged = _mm(dz, w["woutT"][...])
        sa = _sigmoid(h_s[:, C_RA:C_RA + D_MODEL])
        da = dmerged * sa
        put(C_RA, da * sv["a"] * (1.0 - sa))
        sb = _sigmoid(h_s[:, C_RB:C_RB + D_MODEL])
        db = dmerged * sb
        put(C_RB, db * sv["b"] * (1.0 - sb))
        acc["dpa"][...] += _mm_tn(sv["ya"], da)
        acc["dpb"][...] += _mm_tn(sv["yb"], db)
        dya = _mm(da, w["paT"][...])
        dyb = _mm(db, w["pbT"][...])
        ga = h_s[:, C_GA:C_GA + ATTN_W]
        sga = _sigmoid(ga)
        put(C_GA, dya * attn_s[...] * _dsilu(ga, sga))
        attn_s[...] = dya * (ga * sga)
        ks, vs = sv["ks"], sv["vs"]
        dks = [jnp.zeros((BLK, KV_W), F32) for _ in range(nb + 1)]
        dvs = [jnp.zeros((BLK, KV_W), F32) for _ in range(nb + 1)]
        for n in range(nb):
            rows = slice(n * BLK, (n + 1) * BLK)
            ydk, ydv = [], []
            for hk in range(2):
                qs, kd, vd = _attn_operands(h_s, ks, vs, n, hk, lo, lo2)
                probs, ps = sv["probs"][2 * n + hk]
                dos = _stack_heads(attn_s[rows, 2 * hk * 128:(2 * hk + 1) * 128],
                                   attn_s[rows, (2 * hk + 1) * 128:(2 * hk + 2) * 128], lo)
                dp = _mm_nt(dos, vd)
                delta = jnp.sum(probs * dp, axis=-1, keepdims=True)
                dsk = -(ps * delta)
                for g in range(4):
                    j = 4 * hk + g
                    acc["dsink"][j:j + 1, :] += jnp.broadcast_to(_colsum(dsk[g * BLK:(g + 1) * BLK]), (1, 128))
                dss = probs * (dp - delta) * SCALE
                q0, q1 = _unstack_heads(_mm(dss, kd), lo)
                put(C_Q + 2 * hk * 128, q0, rows)
                put(C_Q + (2 * hk + 1) * 128, q1, rows)
                ydk.append(_mm_tn(dss, qs))
                ydv.append(_mm_tn(probs, dos))
            dk2 = _fold_head(ydk[0], ydk[1], lo2)
            dv2 = _fold_head(ydv[0], ydv[1], lo2)
            dks[n] = dks[n] + dk2[0:BLK]
            dks[n + 1] = dks[n + 1] + dk2[BLK:2 * BLK]
            dvs[n] = dvs[n] + dv2[0:BLK]
            dvs[n + 1] = dvs[n + 1] + dv2[BLK:2 * BLK]
        dks[nb] = dks[nb] + ckv_s[:, 0:KV_W]
        dvs[nb] = dvs[nb] + ckv_s[:, KV_W:2 * KV_W]
        ckv_s[:, 0:KV_W] = dks[0]
        ckv_s[:, KV_W:2 * KV_W] = dvs[0]
        put(C_K, jnp.concatenate(dks[1:], axis=0))
        put(C_V, jnp.concatenate(dvs[1:], axis=0))
        gb = h_s[:, C_GB:C_GB + SGU_W]
        sgb = _sigmoid(gb)
        gu, dgu = _gelu_parts(h_s[:, C_UB:C_UB + SGU_W])
        mixed = mix_s[...]
        put(C_GB, dyb * (gu * mixed) * _dsilu(gb, sgb))
        dsgu = dyb * (gb * sgb)
        put(C_UB, dsgu * mixed * dgu)
        dmixed = dsgu * gu
        gv, dgv = _gelu_parts(h_s[:, C_VB:C_VB + SGU_W])
        vn, vhat, vrstd = _ln_fwd(gv, w["vng"][...], w["vnb"][...])
        tri, wms = _sgu_weights(w)
        dbs_part = jnp.zeros((BLK, SGU_W), F32)
        for n in range(nb):
            rows = slice(n * BLK, (n + 1) * BLK)
            dbs_part = dbs_part + dmixed[rows]
            for g in range(SGU_G):
                cols = slice(g * 128, (g + 1) * 128)
                acc["dws"][g] += jnp.where(tri, _mm_nt(dmixed[rows, cols], vn[rows, cols]), 0.0)
                mix_s[rows, cols] = _mm_tn(wms[g], dmixed[rows, cols])
        acc["dbs"][...] += dbs_part
        dvn = mix_s[...]
        acc["dvng"][...] += _colsum(dvn * vhat)
        acc["dvnb"][...] += _colsum(dvn)
        put(C_VB, _ln_bwd(dvn, vhat, vrstd, w["vng"][...]) * dgv)

        @pl.when(i == nT - 1)
        def _():
            for g in range(SGU_G):
                cols = slice(g * 128, (g + 1) * 128)
                tot = jnp.sum(acc["dbs"][:, cols], axis=1, keepdims=True)
                acc["dbs"][:, cols] = jnp.broadcast_to(tot, (BLK, 128))

    return body


def _resident():
    return pl.BlockSpec(memory_space=pltpu.VMEM)


def _weight_specs(names):
    return [pl.BlockSpec(memory_space=pltpu.SMEM) if n == "sinks" else _resident() for n in names]


def _nbytes(a):
    n = jnp.dtype(a.dtype).itemsize
    for d in a.shape:
        n *= d
    return n


def _layer_params(resident=()):
    scoped = V7X_VMEM_BYTES - V7X_VMEM_RESERVE - sum(_nbytes(a) for a in resident)
    return pltpu.CompilerParams(dimension_semantics=("arbitrary",), vmem_limit_bytes=scoped)


def layer_forward(x, wts, S, name):
    T = x.shape[0]
    R = ROWS
    npt = R // BLK
    return pl.pallas_call(
        _make_fwd_body(R, S),
        name=name,
        grid=(T // R,),
        in_specs=[pl.BlockSpec((R, D_MODEL), lambda t: (t, 0)),
                  pl.BlockSpec((BLK, D_MODEL), lambda t: (jnp.maximum(t * npt - 1, 0), 0))] + _weight_specs(_W_NAMES),
        out_specs=pl.BlockSpec((R, D_MODEL), lambda t: (t, 0)),
        out_shape=jax.ShapeDtypeStruct((T, D_MODEL), F32),
        scratch_shapes=[pltpu.VMEM((R, N_COLS), F32), pltpu.VMEM((R, ATTN_W), F32), pltpu.VMEM((R, SGU_W), F32)],
        compiler_params=_layer_params([wts[n] for n in _W_NAMES]),
    )(x, x, *[wts[n] for n in _W_NAMES])


_ACC_SHAPES = dict(dwout=(D_MODEL, D_MODEL), dpa=(ATTN_W, D_MODEL), dpb=(SGU_W, D_MODEL), dbin=(1, N_COLS),
                   dbout=(1, D_MODEL), dlng=(1, D_MODEL), dlnb=(1, D_MODEL), dvng=(1, SGU_W), dvnb=(1, SGU_W),
                   dws=(SGU_G, BLK, BLK), dbs=(BLK, SGU_W), dsink=(8, 128), loss=(8, 128))


def layer_backward(x, aux, wts, S, is_last, name):
    T = x.shape[0]
    R = ROWS_BWD
    nT = T // R
    npt = R // BLK
    rev = lambda i: (nT - 1 - i, 0)
    halo = lambda i: (jnp.maximum((nT - 1 - i) * npt - 1, 0), 0)
    names = _W_NAMES + _WT_NAMES
    out_shape = ([jax.ShapeDtypeStruct((T, D_MODEL), F32), jax.ShapeDtypeStruct((T, N_COLS), MXU_DTYPE)]
                 + [jax.ShapeDtypeStruct(_ACC_SHAPES[n], F32) for n in _ACC_NAMES])
    outs = pl.pallas_call(
        _make_bwd_body(R, S, nT, is_last),
        name=name,
        grid=(nT,),
        in_specs=[pl.BlockSpec((R, D_MODEL), rev), pl.BlockSpec((BLK, D_MODEL), halo),
                  pl.BlockSpec((R, D_MODEL), rev)] + _weight_specs(names),
        out_specs=[pl.BlockSpec((R, D_MODEL), rev), pl.BlockSpec((R, N_COLS), rev)] + [_resident() for _ in _ACC_NAMES],
        out_shape=out_shape,
        scratch_shapes=[pltpu.VMEM((R, N_COLS), F32), pltpu.VMEM((R, ATTN_W), F32), pltpu.VMEM((R, SGU_W), F32),
                        pltpu.VMEM((BLK, 2 * KV_W), F32)],
        compiler_params=_layer_params([wts[n] for n in names] + out_shape[2:]),
    )(x, x, aux, *[wts[n] for n in names])
    return outs[0], outs[1], dict(zip(_ACC_NAMES, outs[2:]))


def input_grads(dres, dh, x, winT, name):
    T = x.shape[0]
    R = ROWS

    def body(dres_ref, dh_ref, x_ref, winT_ref, dx_ref, dwin_ref):
        @pl.when(pl.program_id(0) == 0)
        def _():
            dwin_ref[...] = jnp.zeros(dwin_ref.shape, F32)

        dh = dh_ref[...]
        dx_ref[...] = dres_ref[...] + jnp.dot(dh, winT_ref[...], preferred_element_type=F32)
        dwin_ref[...] += _mm_tn(dh, x_ref[...])

    row = lambda t: (t, 0)
    return pl.pallas_call(
        body,
        name=name,
        grid=(T // R,),
        in_specs=[pl.BlockSpec((R, D_MODEL), row), pl.BlockSpec((R, N_COLS), row), pl.BlockSpec((R, D_MODEL), row),
                  _resident()],
        out_specs=[pl.BlockSpec((R, D_MODEL), row), _resident()],
        out_shape=[jax.ShapeDtypeStruct((T, D_MODEL), F32), jax.ShapeDtypeStruct((N_COLS, D_MODEL), F32)],
        compiler_params=_layer_params([winT, jax.ShapeDtypeStruct((N_COLS, D_MODEL), F32)]),
    )(dres, dh, x, winT)


def input_norm_forward(x, g, b, name):
    T = x.shape[0]
    R = ROWS

    def body(x_ref, g_ref, b_ref, y_ref):
        y_ref[...] = _ln_fwd(x_ref[...], g_ref[...], b_ref[...])[0]

    row = lambda t: (t, 0)
    return pl.pallas_call(
        body, name=name, grid=(T // R,),
        in_specs=[pl.BlockSpec((R, D_MODEL), row), _resident(), _resident()],
        out_specs=pl.BlockSpec((R, D_MODEL), row),
        out_shape=jax.ShapeDtypeStruct((T, D_MODEL), F32),
        compiler_params=_layer_params(),
    )(x, g, b)


def input_norm_backward(x, dy, g, name):
    T = x.shape[0]
    R = ROWS

    def body(x_ref, dy_ref, g_ref, dx_ref, dg_ref, db_ref):
        @pl.when(pl.program_id(0) == 0)
        def _():
            dg_ref[...] = jnp.zeros(dg_ref.shape, F32)
            db_ref[...] = jnp.zeros(db_ref.shape, F32)

        dy = dy_ref[...]
        _, xhat, rstd = _ln_fwd(x_ref[...], g_ref[...], g_ref[...])
        dx_ref[...] = _ln_bwd(dy, xhat, rstd, g_ref[...])
        dg_ref[...] += _colsum(dy * xhat)
        db_ref[...] += _colsum(dy)

    row = lambda t: (t, 0)
    vec = jax.ShapeDtypeStruct((1, D_MODEL), F32)
    return pl.pallas_call(
        body, name=name, grid=(T // R,),
        in_specs=[pl.BlockSpec((R, D_MODEL), row), pl.BlockSpec((R, D_MODEL), row), _resident()],
        out_specs=[pl.BlockSpec((R, D_MODEL), row), _resident(), _resident()],
        out_shape=[jax.ShapeDtypeStruct((T, D_MODEL), F32), vec, vec],
        compiler_params=_layer_params(),
    )(x, dy, g)


def local_step(x, target, ln_in_g, ln_in_b, layers, S):
    x0 = input_norm_forward(x, ln_in_g, ln_in_b, "ln_in_fwd")
    x1 = layer_forward(x0, layers[0], S, "layer0_fwd")
    dres1, dh1, g1 = layer_backward(x1, target, layers[1], S, True, "layer1_bwd")
    dx1, g1["dwinT"] = input_grads(dres1, dh1, x1, layers[1]["winT"], "layer1_dx_dwin")
    dres0, dh0, g0 = layer_backward(x0, dx1, layers[0], S, False, "layer0_bwd")
    dx0, g0["dwinT"] = input_grads(dres0, dh0, x0, layers[0]["winT"], "layer0_dx_dwin")
    gx, dg_in, db_in = input_norm_backward(x, dx0, ln_in_g, "ln_in_bwd")
    return g1["loss"][0, 0], gx, dg_in, db_in, [g0, g1]


_ANY = pl.BlockSpec(memory_space=pl.ANY)
_DMA = pltpu.SemaphoreType.DMA

_PIECES = ("winT", "pa", "pb", "wout")
_WHOLE = dict(winT=(N_COLS, D_MODEL), pa=(ATTN_W, D_MODEL), pb=(SGU_W, D_MODEL), wout=(D_MODEL, D_MODEL))
_HALF = dict(winT=(N_COLS // 8, D_MODEL), pa=(ATTN_W // 2, D_MODEL // 4), pb=(SGU_W // 2, D_MODEL // 4),
             wout=(D_MODEL // 8, D_MODEL))
SMALL_ROWS = 1280


def _region(name, ref, k, h):
    hr, hc = _HALF[name]
    if name in ("pa", "pb"):
        return ref.at[pl.ds(h * hr, hr), pl.ds(k * hc, hc)]
    return ref.at[pl.ds(k * 2 * hr + h * hr, hr), :]


def _place():
    x, y, c = lax.axis_index("x"), lax.axis_index("y"), lax.axis_index("c")
    return x, y, c, 2 * x + y


def _as_chip(j, fn):
    for jj in range(N_CHIPS):
        pl.when(j == jj)(functools.partial(fn, jj))


def _remote(src, dst, send_sem, recv_sem, to):
    return pltpu.make_async_remote_copy(src_ref=src, dst_ref=dst, send_sem=send_sem, recv_sem=recv_sem,
                                        device_id=to, device_id_type=MESH)


def _core_of(k, c):
    return (k // 2, k % 2, c)


def gather_weights(names, shards):
    n = len(names)

    def body(*refs):
        src, dst = refs[:n], refs[n:2 * n]
        send1, recv1, send2, recv2 = refs[2 * n:]
        x, y, c, j = _place()
        sib = (x, y, 1 - c)

        def run(jj):
            others = [k for k in range(N_CHIPS) if k != jj]
            first = []
            for i, name in enumerate(names):
                hr = _HALF[name][0]
                for k in others:
                    first.append(_remote(src[i].at[pl.ds(c * hr, hr)], _region(name, dst[i], jj, c),
                                         send1.at[4 * i + k], recv1.at[4 * i + jj], _core_of(k, c)))
            for cp in first:
                cp.start()
            passed = []
            for i, name in enumerate(names):
                for k in others:
                    land = _region(name, dst[i], k, c)
                    _remote(land, land, send1.at[4 * i + k], recv1.at[4 * i + k], _core_of(k, c)).wait_recv()
                    cp = _remote(land, land, send2.at[4 * i + k], recv2.at[4 * i + k], sib)
                    cp.start()
                    passed.append(cp)
            for i, name in enumerate(names):
                for k in others:
                    land = _region(name, dst[i], k, 1 - c)
                    _remote(land, land, send2.at[4 * i + k], recv2.at[4 * i + k], sib).wait_recv()
            for cp in first + passed:
                cp.wait_send()

        _as_chip(j, run)

    return pl.pallas_call(
        body, name="gather_weights", in_specs=[_ANY] * n, out_specs=[_ANY] * n,
        out_shape=[jax.ShapeDtypeStruct(_WHOLE[nm], s.dtype) for nm, s in zip(names, shards)],
        scratch_shapes=[_DMA((4 * n,)), _DMA((4 * n,)), _DMA((4 * n,)), _DMA((4 * n,))],
    )(*shards)


def pair_send_halves(names, parts, small):
    n = len(names)

    def body(*refs):
        src, small_src = refs[:n], refs[n]
        dst, small_dst = refs[n + 1:2 * n + 1], refs[2 * n + 1]
        send, recv = refs[2 * n + 2:]
        x, y, c, _ = _place()
        sib = (x, y, 1 - c)
        cps = []
        for i, name in enumerate(names):
            for k in range(N_CHIPS):
                cps.append(_remote(_region(name, src[i], k, 1 - c), dst[i].at[k], send.at[4 * i + k], recv.at[4 * i + k], sib))
        hs = SMALL_ROWS // 2
        cps.append(_remote(small_src.at[pl.ds((1 - c) * hs, hs)], small_dst, send.at[4 * n], recv.at[4 * n], sib))
        for cp in cps:
            cp.start()
        for cp in cps:
            cp.wait()

    return pl.pallas_call(
        body, name="pair_send_halves", in_specs=[_ANY] * (n + 1), out_specs=[_ANY] * (n + 1),
        out_shape=[jax.ShapeDtypeStruct((N_CHIPS,) + _HALF[nm], F32) for nm in names]
        + [jax.ShapeDtypeStruct((SMALL_ROWS // 2, 128), F32)],
        scratch_shapes=[_DMA((4 * n + 1,)), _DMA((4 * n + 1,))],
    )(*parts, small)


def chip_exchange(sums, small):
    n = len(sums)

    def body(*refs):
        src, small_src = refs[:n], refs[n]
        dst, small_dst = refs[n + 1:2 * n + 1], refs[2 * n + 1]
        send, recv, lsem = refs[2 * n + 2:]
        x, y, c, j = _place()

        def run(jj):
            others = [k for k in range(N_CHIPS) if k != jj]
            own = pltpu.make_async_copy(small_src, small_dst.at[jj], lsem)
            out = []
            for k in others:
                for i in range(n):
                    out.append(_remote(src[i].at[k], dst[i].at[jj], send.at[4 * i + k], recv.at[4 * i + jj], _core_of(k, c)))
                out.append(_remote(small_src, small_dst.at[jj], send.at[4 * n + k], recv.at[4 * n + jj], _core_of(k, c)))
            own.start()
            for cp in out:
                cp.start()
            for k in others:
                for i in range(n):
                    _remote(src[i].at[k], dst[i].at[k], send.at[4 * i + k], recv.at[4 * i + k], _core_of(k, c)).wait_recv()
                _remote(small_src, small_dst.at[k], send.at[4 * n + k], recv.at[4 * n + k], _core_of(k, c)).wait_recv()
            for cp in out:
                cp.wait_send()
            own.wait()

        _as_chip(j, run)

    return pl.pallas_call(
        body, name="chip_exchange", in_specs=[_ANY] * (n + 1), out_specs=[_ANY] * (n + 1),
        out_shape=[jax.ShapeDtypeStruct(a.shape, a.dtype) for a in sums]
        + [jax.ShapeDtypeStruct((N_CHIPS,) + small.shape, small.dtype)],
        scratch_shapes=[_DMA((4 * n + 4,)), _DMA((4 * n + 4,)), _DMA(())],
    )(*sums, small)


def pair_send_totals(arrs):
    n = len(arrs)

    def body(*refs):
        src, dst = refs[:n], refs[n:2 * n]
        send, recv = refs[2 * n:]
        x, y, c, _ = _place()
        cps = [_remote(src[i], dst[i], send.at[i], recv.at[i], (x, y, 1 - c)) for i in range(n)]
        for cp in cps:
            cp.start()
        for cp in cps:
            cp.wait()

    return pl.pallas_call(
        body, name="pair_send_totals", in_specs=[_ANY] * n, out_specs=[_ANY] * n,
        out_shape=[jax.ShapeDtypeStruct(a.shape, a.dtype) for a in arrs],
        scratch_shapes=[_DMA((n,)), _DMA((n,))],
    )(*arrs)


def _plain_params():
    return pltpu.CompilerParams(dimension_semantics=("arbitrary",) * 2, vmem_limit_bytes=V7X_VMEM_BYTES // 2)


def add_own_halves(names, parts, landed, core):
    n = len(names)

    def body(core_ref, *refs):
        for i in range(n):
            refs[2 * n + i][0] = (refs[i][...] + refs[n + i][0]).astype(MXU_DTYPE)

    in_specs, out_specs = [], []
    for name in names:
        hr, hc = _HALF[name]
        if name in ("pa", "pb"):
            in_specs.append(pl.BlockSpec((hr // 2, hc), lambda k, s, cr: (cr[0] * 2 + s, k)))
        else:
            in_specs.append(pl.BlockSpec((hr // 2, hc), lambda k, s, cr: (k * 4 + cr[0] * 2 + s, 0)))
    for name in names:
        hr, hc = _HALF[name]
        in_specs.append(pl.BlockSpec((1, hr // 2, hc), lambda k, s, cr: (k, s, 0)))
        out_specs.append(pl.BlockSpec((1, hr // 2, hc), lambda k, s, cr: (k, s, 0)))
    return pl.pallas_call(
        body, name="add_own_halves",
        grid_spec=pltpu.PrefetchScalarGridSpec(num_scalar_prefetch=1, grid=(N_CHIPS, 2), in_specs=in_specs, out_specs=out_specs),
        out_shape=[jax.ShapeDtypeStruct((N_CHIPS,) + _HALF[nm], MXU_DTYPE) for nm in names],
        compiler_params=_plain_params(),
    )(core, *parts, *landed)


def add_small_half(small, landed, core):
    hs = SMALL_ROWS // 2

    def body(core_ref, a_ref, b_ref, o_ref):
        o_ref[...] = a_ref[...] + b_ref[...]

    return pl.pallas_call(
        body, name="add_small_half",
        grid_spec=pltpu.PrefetchScalarGridSpec(
            num_scalar_prefetch=1, grid=(1,),
            in_specs=[pl.BlockSpec((hs, 128), lambda s, cr: (cr[0], 0)), pl.BlockSpec((hs, 128), lambda s, cr: (0, 0))],
            out_specs=pl.BlockSpec((hs, 128), lambda s, cr: (0, 0))),
        out_shape=jax.ShapeDtypeStruct((hs, 128), F32),
    )(core, small, landed)


def sum_chips(sums, landed, small_landed, chips):
    n = len(sums)

    def body(chips_ref, *refs):
        for i in range(n):
            tot = refs[4 * i][0].astype(F32)
            for r in range(1, N_CHIPS):
                tot = tot + refs[4 * i + r][0].astype(F32)
            refs[4 * n + 1 + i][...] = tot

        @pl.when(pl.program_id(0) == 0)
        def _():
            sm = refs[4 * n]
            refs[5 * n + 1][...] = ((sm[0] + sm[1]) + sm[2]) + sm[3]

    in_specs, out_specs, operands = [], [], []
    for a, l in zip(sums, landed):
        _, hr, hc = a.shape
        for r in range(N_CHIPS):
            in_specs.append(pl.BlockSpec((1, hr // 2, hc), functools.partial(lambda s, ch, r: (ch[r], s, 0), r=r)))
            operands.append(a if r == 0 else l)
        out_specs.append(pl.BlockSpec((hr // 2, hc), lambda s, ch: (s, 0)))
    in_specs.append(pl.BlockSpec(small_landed.shape, lambda s, ch: (0, 0, 0)))
    out_specs.append(pl.BlockSpec(small_landed.shape[1:], lambda s, ch: (0, 0)))
    return pl.pallas_call(
        body, name="sum_chips",
        grid_spec=pltpu.PrefetchScalarGridSpec(num_scalar_prefetch=1, grid=(2,), in_specs=in_specs, out_specs=out_specs),
        out_shape=[jax.ShapeDtypeStruct(a.shape[1:], F32) for a in sums] + [jax.ShapeDtypeStruct(small_landed.shape[1:], F32)],
        compiler_params=pltpu.CompilerParams(dimension_semantics=("arbitrary",), vmem_limit_bytes=V7X_VMEM_BYTES // 2),
    )(chips, *operands, small_landed)


N_STEPS = 8


def _rows2d(a):
    return a.reshape((-1, a.shape[-1]))


def adamw(ws, gs, ms, vs):
    n = len(ws)
    c1 = 1.0 / (1.0 - ADAM_B1 ** ADAM_STEP)
    c2 = 1.0 / (1.0 - ADAM_B2 ** ADAM_STEP)

    def body(*refs):
        for i in range(n):
            w, g, m, v = (refs[k * n + i][...] for k in range(4))
            m = ADAM_B1 * m + (1.0 - ADAM_B1) * g
            v = ADAM_B2 * v + (1.0 - ADAM_B2) * (g * g)
            refs[4 * n + i][...] = -ADAM_LR * ((m * c1) / (jnp.sqrt(v * c2) + ADAM_EPS) + ADAM_WD * w)
            refs[5 * n + i][...] = m
            refs[6 * n + i][...] = v

    spec = lambda a: pl.BlockSpec((a.shape[0] // N_STEPS, a.shape[1]), lambda s: (s, 0))
    outs = pl.pallas_call(
        body, name="adamw", grid=(N_STEPS,),
        in_specs=[spec(a) for a in ws] * 4, out_specs=[spec(a) for a in ws] * 3,
        out_shape=[jax.ShapeDtypeStruct(a.shape, F32) for a in ws] * 3,
        compiler_params=pltpu.CompilerParams(dimension_semantics=("arbitrary",), vmem_limit_bytes=V7X_VMEM_BYTES // 2),
    )(*ws, *gs, *ms, *vs)
    return outs[:n], outs[n:2 * n], outs[2 * n:]


_SMALL = (("ln_in_g", (D_MODEL,)), ("ln_in_b", (D_MODEL,)), ("b_in", (DEPTH, N_COLS)), ("sinks", (DEPTH, 8)),
          ("vn_g", (DEPTH, SGU_W)), ("vn_b", (DEPTH, SGU_W)), ("w_s", (DEPTH, SGU_G, BLK, BLK)),
          ("b_s", (DEPTH, SGU_G, BLK)), ("b_out", (DEPTH, D_MODEL)), ("ln_g", (DEPTH, D_MODEL)), ("ln_b", (DEPTH, D_MODEL)))
_BIG = ("w_in", "p_a", "p_b", "w_out")


def _pack_small(vals):
    flat = []
    for name, shape in _SMALL:
        v = vals[name].reshape((-1,))
        flat.append(jnp.pad(v, (0, (-v.shape[0]) % 128)))
    flat = jnp.concatenate(flat)
    return jnp.pad(flat, (0, SMALL_ROWS * 128 - flat.shape[0])).reshape((SMALL_ROWS, 128))


def _unpack_small(packed):
    flat = packed.reshape((-1,))
    out, pos = {}, 0
    for name, shape in _SMALL:
        size = 1
        for d in shape:
            size *= d
        out[name] = flat[pos:pos + size].reshape(shape)
        pos += size + (-size) % 128
    return out


def _shard_of(name, whole, j):
    hr, hc = _HALF[name]
    if name in ("pa", "pb"):
        return (0, j * hc), (2 * hr, hc)
    return (j * 2 * hr, 0), (2 * hr, hc)


def kernel(x, ln_in_g, ln_in_b, w_in, b_in, sinks, vn_g, vn_b, w_s, b_s, p_a, p_b, w_out, b_out, ln_g, ln_b, loss_target, m_ln_in_g, m_ln_in_b, m_w_in, m_b_in, m_sinks, m_vn_g, m_vn_b, m_w_s, m_b_s, m_p_a, m_p_b, m_w_out, m_b_out, m_ln_g, m_ln_b, v_ln_in_g, v_ln_in_b, v_w_in, v_b_in, v_sinks, v_vn_g, v_vn_b, v_w_s, v_b_s, v_p_a, v_p_b, v_w_out, v_b_out, v_ln_g, v_ln_b):
    weights = dict(ln_in_g=ln_in_g, ln_in_b=ln_in_b, w_in=w_in, b_in=b_in, sinks=sinks, vn_g=vn_g, vn_b=vn_b, w_s=w_s,
                   b_s=b_s, p_a=p_a, p_b=p_b, w_out=w_out, b_out=b_out, ln_g=ln_g, ln_b=ln_b)
    mom1 = dict(ln_in_g=m_ln_in_g, ln_in_b=m_ln_in_b, w_in=m_w_in, b_in=m_b_in, sinks=m_sinks, vn_g=m_vn_g, vn_b=m_vn_b,
                w_s=m_w_s, b_s=m_b_s, p_a=m_p_a, p_b=m_p_b, w_out=m_w_out, b_out=m_b_out, ln_g=m_ln_g, ln_b=m_ln_b)
    mom2 = dict(ln_in_g=v_ln_in_g, ln_in_b=v_ln_in_b, w_in=v_w_in, b_in=v_b_in, sinks=v_sinks, vn_g=v_vn_g, vn_b=v_vn_b,
                w_s=v_w_s, b_s=v_b_s, p_a=v_p_a, p_b=v_p_b, w_out=v_w_out, b_out=v_b_out, ln_g=v_ln_g, ln_b=v_ln_b)
    n_seq, S, _ = x.shape
    T = n_seq * S
    c = lax.axis_index("c")
    j = 2 * lax.axis_index("x") + lax.axis_index("y")
    core = c.astype(jnp.int32).reshape((1,))
    chips = jnp.stack([j] + [r + (r >= j) for r in range(N_CHIPS - 1)]).astype(jnp.int32)
    names = [nm for _ in range(DEPTH) for nm in _PIECES]

    own = []
    for l in range(DEPTH):
        own += [w_in[l].T.astype(MXU_DTYPE), p_a[l].astype(MXU_DTYPE), p_b[l].astype(MXU_DTYPE), w_out[l].astype(MXU_DTYPE)]
    gathered = gather_weights(names, own)
    gathered = [lax.dynamic_update_slice(g, s, _shard_of(nm, g, j)[0]) for nm, g, s in zip(names, gathered, own)]
    layers = []
    for l in range(DEPTH):
        d = dict(zip(_PIECES, gathered[4 * l:4 * l + 4]))
        d.update(win=d["winT"].T, paT=d["pa"].T, pbT=d["pb"].T, woutT=d["wout"].T)
        d.update(bin=b_in[l][None], sinks=sinks[l], vng=vn_g[l][None], vnb=vn_b[l][None], ws=w_s[l],
                 bs=jnp.repeat(b_s[l].T, 128, axis=1), bout=b_out[l][None], lng=ln_g[l][None], lnb=ln_b[l][None])
        layers.append(d)

    loss_part, grad_x, d_ln_in_g, d_ln_in_b, gl = local_step(
        x.reshape((T, D_MODEL)), loss_target.reshape((T, D_MODEL)), ln_in_g[None], ln_in_b[None], layers, S)
    loss = lax.psum(loss_part, ("x", "y", "c"))

    per_layer = lambda key, pick: jnp.stack([pick(gl[l][key]) for l in range(DEPTH)])
    row0 = lambda key: per_layer(key, lambda a: a[0])
    small_part = _pack_small(dict(
        ln_in_g=d_ln_in_g[0], ln_in_b=d_ln_in_b[0], b_in=row0("dbin"), sinks=per_layer("dsink", lambda a: a[:, 0]),
        vn_g=row0("dvng"), vn_b=row0("dvnb"), w_s=per_layer("dws", lambda a: a),
        b_s=per_layer("dbs", lambda a: a[:, ::128].T), b_out=row0("dbout"), ln_g=row0("dlng"), ln_b=row0("dlnb")))
    parts = []
    for l in range(DEPTH):
        parts += [gl[l]["dwinT"], gl[l]["dpa"], gl[l]["dpb"], gl[l]["dwout"]]
    *landed, small_landed = pair_send_halves(names, parts, small_part)
    chip_sums = add_own_halves(names, parts, landed, core)
    small_sum = add_small_half(small_part, small_landed, core)
    *from_chips, small_from_chips = chip_exchange(chip_sums, small_sum)
    *totals, small_total = sum_chips(chip_sums, from_chips, small_from_chips, chips)
    *sib_totals, sib_small = pair_send_totals(totals + [small_total])

    def both_halves(mine, theirs):
        return jnp.where(c == 0, jnp.concatenate([mine, theirs], axis=0), jnp.concatenate([theirs, mine], axis=0))

    g_big = {}
    for pi, pname in enumerate(_BIG):
        per = [both_halves(totals[4 * l + pi], sib_totals[4 * l + pi]) for l in range(DEPTH)]
        g_big[pname] = jnp.stack([p.T for p in per] if pname == "w_in" else per)
    g_small = both_halves(small_total, sib_small)

    ws = [_rows2d(weights[n]) for n in _BIG] + [_pack_small(weights)]
    gs = [_rows2d(g_big[n]) for n in _BIG] + [g_small]
    ms = [_rows2d(mom1[n]) for n in _BIG] + [_pack_small(mom1)]
    vs = [_rows2d(mom2[n]) for n in _BIG] + [_pack_small(mom2)]
    deltas, new_ms, new_vs = adamw(ws, gs, ms, vs)

    def named(outs):
        d = _unpack_small(outs[-1])
        d.update({n: o.reshape(weights[n].shape) for n, o in zip(_BIG, outs[:-1])})
        return d

    grads = named([g_big[n] for n in _BIG] + [g_small])
    order = ("ln_in_g", "ln_in_b", "w_in", "b_in", "sinks", "vn_g", "vn_b", "w_s", "b_s", "p_a", "p_b", "w_out", "b_out",
             "ln_g", "ln_b")
    res = [loss, grad_x.reshape(x.shape)]
    for group in (grads, named(deltas), named(new_ms), named(new_vs)):
        res.extend(group[n] for n in order)
    return tuple(res)
```

```python
import functools

import jax
import jax.numpy as jnp
from jax import lax
from jax.experimental import pallas as pl
from jax.experimental.pallas import tpu as pltpu

F32 = jnp.float32
MXU_DTYPE = jnp.bfloat16

D_MODEL = 1024
DEPTH = 2
HEAD_DIM = 64
ATTN_W = 512
KV_W = 128
BLK = 128
SGU_W = 512
SGU_G = 4
N_COLS = 4864
C_Q, C_K, C_V, C_GA, C_UB, C_VB, C_GB, C_RA, C_RB = 0, 512, 640, 768, 1280, 1792, 2304, 2816, 3840
ALPHA = (2.0 * DEPTH) ** 0.25
LN_EPS = 1e-5
SCALE = HEAD_DIM ** -0.5
NEG = float(jnp.finfo(jnp.float32).min)
GELU_C = 0.7978845608028654
GELU_A = 0.044715

ADAM_LR, ADAM_B1, ADAM_B2, ADAM_EPS, ADAM_WD, ADAM_STEP = 0.001, 0.9, 0.999, 1e-08, 0.01, 10

V7X_VMEM_BYTES = 64 * 1024 * 1024
V7X_VMEM_RESERVE = 3 * 1024 * 1024
ROWS = 256
ROWS_BWD = 256
ROWS_PROJ = 512

MESH = pl.DeviceIdType.MESH
N_CHIPS = 4


def _mm(a, b):
    return jnp.dot(a.astype(MXU_DTYPE), b.astype(MXU_DTYPE), preferred_element_type=F32)


def _mm_nt(a, b):
    return lax.dot_general(a.astype(MXU_DTYPE), b.astype(MXU_DTYPE), (((1,), (1,)), ((), ())),
                           preferred_element_type=F32)


def _mm_tn(a, b):
    return lax.dot_general(a.astype(MXU_DTYPE), b.astype(MXU_DTYPE), (((0,), (0,)), ((), ())),
                           preferred_element_type=F32)


def _sigmoid(x):
    return 1.0 / (1.0 + jnp.exp(-x))


def _gelu_parts(x):
    x2 = x * x
    t = jnp.tanh(GELU_C * (x + GELU_A * x2 * x))
    g = 0.5 * x * (1.0 + t)
    dg = 0.5 * (1.0 + t) + 0.5 * x * (1.0 - t * t) * (GELU_C * (1.0 + 3.0 * GELU_A * x2))
    return g, dg


def _ln_fwd(z, g, b):
    mu = jnp.mean(z, axis=-1, keepdims=True)
    zc = z - mu
    var = jnp.mean(zc * zc, axis=-1, keepdims=True)
    rstd = lax.rsqrt(var + LN_EPS)
    zhat = zc * rstd
    return zhat * g + b, zhat, rstd


def _ln_bwd(dy, zhat, rstd, g):
    dzh = dy * g
    m1 = jnp.mean(dzh, axis=-1, keepdims=True)
    m2 = jnp.mean(dzh * zhat, axis=-1, keepdims=True)
    return rstd * (dzh - m1 - zhat * m2)


def _colsum(v):
    return jnp.sum(v, axis=0, keepdims=True)


def _lane_lo(rows):
    return lax.broadcasted_iota(jnp.int32, (rows, 128), 1) < HEAD_DIM


def _dup_head(x2, hk, lo):
    xr = pltpu.roll(x2, HEAD_DIM, 1)
    return jnp.where(lo, x2, xr) if hk == 0 else jnp.where(lo, xr, x2)


def _fold_head(y0, y1, lo):
    f0 = y0 + pltpu.roll(y0, HEAD_DIM, 1)
    f1 = y1 + pltpu.roll(y1, HEAD_DIM, 1)
    return jnp.where(lo, f0, f1)


def _stack_heads(t0, t1, lo):
    z = jnp.zeros_like(t0)
    return jnp.concatenate([jnp.where(lo, t0, z), jnp.where(lo, z, t0),
                            jnp.where(lo, t1, z), jnp.where(lo, z, t1)], axis=0)


def _unstack_heads(o, lo):
    return (jnp.where(lo, o[0:128], o[128:256]), jnp.where(lo, o[256:384], o[384:512]))


def _attn_probs(qs, kd, sink4, start):
    s = _mm_nt(qs, kd) * SCALE
    row = lax.broadcasted_iota(jnp.int32, (4 * BLK, 2 * BLK), 0) & (BLK - 1)
    kpos = lax.broadcasted_iota(jnp.int32, (4 * BLK, 2 * BLK), 1)
    valid = (kpos > row) & (kpos <= row + BLK)
    if start is not None:
        valid = valid & (jnp.logical_not(start) | (kpos >= BLK))
    s = jnp.where(valid, s, NEG)
    r1 = lax.broadcasted_iota(jnp.int32, (4 * BLK, 1), 0)
    sk = jnp.where(r1 < BLK, sink4[0], jnp.where(r1 < 2 * BLK, sink4[1], jnp.where(r1 < 3 * BLK, sink4[2], sink4[3])))
    m = jnp.maximum(jnp.max(s, axis=-1, keepdims=True), sk)
    p = jnp.exp(s - m)
    es = jnp.exp(sk - m)
    denom = jnp.sum(p, axis=-1, keepdims=True) + es
    return p / denom, es / denom


_PROJ_CHUNK = 512


def _kv_blocks(kvh, h_s, nb):
    ks = [kvh[:, 0:KV_W]] + [h_s[n * BLK:(n + 1) * BLK, C_K:C_K + KV_W] for n in range(nb)]
    vs = [kvh[:, KV_W:2 * KV_W]] + [h_s[n * BLK:(n + 1) * BLK, C_V:C_V + KV_W] for n in range(nb)]
    return ks, vs


def _attn_operands(h_s, ks, vs, n, hk, lo, lo2):
    rows = slice(n * BLK, (n + 1) * BLK)
    kd = _dup_head(jnp.concatenate([ks[n], ks[n + 1]], axis=0), hk, lo2)
    vd = _dup_head(jnp.concatenate([vs[n], vs[n + 1]], axis=0), hk, lo2)
    c0 = C_Q + 2 * hk * 128
    qs = _stack_heads(h_s[rows, c0:c0 + 128], h_s[rows, c0 + 128:c0 + 256], lo)
    return qs, kd, vd


def _sgu_weights(w):
    tri = (lax.broadcasted_iota(jnp.int32, (BLK, BLK), 0) >= lax.broadcasted_iota(jnp.int32, (BLK, BLK), 1))
    return tri, [jnp.where(tri, w["ws"][g], 0.0) for g in range(SGU_G)]


def _layer_forward(x, kvh, start, w, h_s, attn_s, mix_s, keep):
    R = x.shape[0]
    nb = R // BLK
    lo = _lane_lo(BLK)
    lo2 = _lane_lo(2 * BLK)
    ks, vs = _kv_blocks(kvh, h_s, nb)
    sinks = [w["sinks"][j] for j in range(8)]
    probs_l = []
    for n in range(nb):
        rows = slice(n * BLK, (n + 1) * BLK)
        for hk in range(2):
            qs, kd, vd = _attn_operands(h_s, ks, vs, n, hk, lo, lo2)
            probs, ps = _attn_probs(qs, kd, sinks[4 * hk:4 * hk + 4], start if n == 0 else None)
            t0, t1 = _unstack_heads(_mm(probs, vd), lo)
            attn_s[rows, 2 * hk * 128:(2 * hk + 1) * 128] = t0
            attn_s[rows, (2 * hk + 1) * 128:(2 * hk + 2) * 128] = t1
            if keep:
                probs_l.append((probs, ps))
    ga = h_s[:, C_GA:C_GA + ATTN_W]
    ya = attn_s[...] * (ga * _sigmoid(ga))
    gu, _ = _gelu_parts(h_s[:, C_UB:C_UB + SGU_W])
    gv, _ = _gelu_parts(h_s[:, C_VB:C_VB + SGU_W])
    vn, _, _ = _ln_fwd(gv, w["vng"][...], w["vnb"][...])
    _, wms = _sgu_weights(w)
    for n in range(nb):
        rows = slice(n * BLK, (n + 1) * BLK)
        for g in range(SGU_G):
            cols = slice(g * 128, (g + 1) * 128)
            mix_s[rows, cols] = _mm(wms[g], vn[rows, cols]) + w["bs"][:, cols]
    gb = h_s[:, C_GB:C_GB + SGU_W]
    yb = gu * mix_s[...] * (gb * _sigmoid(gb))
    a = _mm(ya, w["pa"][...])
    b = _mm(yb, w["pb"][...])
    merged = _sigmoid(h_s[:, C_RA:C_RA + D_MODEL]) * a + _sigmoid(h_s[:, C_RB:C_RB + D_MODEL]) * b
    out = _mm(merged, w["wout"][...]) + w["bout"][...]
    y, zhat, rstd = _ln_fwd(ALPHA * x + out, w["lng"][...], w["lnb"][...])
    if not keep:
        return y, None
    return y, dict(ks=ks, vs=vs, probs=probs_l, ya=ya, yb=yb, a=a, b=b, merged=merged, zhat=zhat, rstd=rstd)


def _dsilu(g, sg):
    return sg * (1.0 + g * (1.0 - sg))


_W_NAMES = ("sinks", "vng", "vnb", "ws", "bs", "pa", "pb", "wout", "bout", "lng", "lnb")
_WT_NAMES = ("paT", "pbT", "woutT")
_ACC_NAMES = ("dwout", "dpa", "dpb", "dbin", "dbout", "dlng", "dlnb", "dvng", "dvnb", "dws", "dbs", "dsink", "loss")


def _make_fwd_body(R, S):
    def body(x_ref, h_ref, kvh_ref, *rest):
        w = dict(zip(_W_NAMES, rest[:len(_W_NAMES)]))
        y_ref, attn_s, mix_s = rest[len(_W_NAMES):]
        start = (pl.program_id(0) % (S // R)) == 0
        y, _ = _layer_forward(x_ref[...], kvh_ref[...], start, w, h_ref, attn_s, mix_s, keep=False)
        y_ref[...] = y
    return body


def _make_bwd_body(R, S, nT, is_last):
    nb = R // BLK
    names = _W_NAMES + _WT_NAMES
    nw = len(names)

    def body(x_ref, h_s, kvh_ref, aux_ref, *rest):
        w = dict(zip(names, rest[:nw]))
        dres_ref, dh_ref = rest[nw:nw + 2]
        acc = dict(zip(_ACC_NAMES, rest[nw + 2:nw + 2 + len(_ACC_NAMES)]))
        attn_s, mix_s, ckv_s = rest[nw + 2 + len(_ACC_NAMES):]
        i = pl.program_id(0)
        start = ((nT - 1 - i) % (S // R)) == 0
        lo = _lane_lo(BLK)
        lo2 = _lane_lo(2 * BLK)

        @pl.when(i == 0)
        def _():
            for name in _ACC_NAMES:
                acc[name][...] = jnp.zeros(acc[name].shape, F32)
            ckv_s[...] = jnp.zeros(ckv_s.shape, F32)

        x = x_ref[...]
        y, sv = _layer_forward(x, kvh_ref[...], start, w, h_s, attn_s, mix_s, keep=True)
        if is_last:
            diff = y - aux_ref[...]
            part = 0.5 * jnp.sum(jnp.mean(diff * diff, axis=-1, keepdims=True), axis=0, keepdims=True)
            acc["loss"][...] += jnp.broadcast_to(part, acc["loss"].shape)
            dy = diff * (1.0 / D_MODEL)
        else:
            dy = aux_ref[...]

        def put(c0, val, rows=slice(None)):
            width = val.shape[1]
            dh_ref[rows, c0:c0 + width] = val.astype(dh_ref.dtype)
            acc["dbin"][:, c0:c0 + width] += _colsum(val)

        acc["dlng"][...] += _colsum(dy * sv["zhat"])
        acc["dlnb"][...] += _colsum(dy)
        dz = _ln_bwd(dy, sv["zhat"], sv["rstd"], w["lng"][...])
        dres_ref[...] = ALPHA * dz
        acc["dbout"][...] += _colsum(dz)
        acc["dwout"][...] += _mm_tn(sv["merged"], dz)
        dmerged = _mm(dz, w["woutT"][...])
        sa = _sigmoid(h_s[:, C_RA:C_RA + D_MODEL])
        da = dmerged * sa
        put(C_RA, da * sv["a"] * (1.0 - sa))
        sb = _sigmoid(h_s[:, C_RB:C_RB + D_MODEL])
        db = dmerged * sb
        put(C_RB, db * sv["b"] * (1.0 - sb))
        acc["dpa"][...] += _mm_tn(sv["ya"], da)
        acc["dpb"][...] += _mm_tn(sv["yb"], db)
        dya = _mm(da, w["paT"][...])
        dyb = _mm(db, w["pbT"][...])
        ga = h_s[:, C_GA:C_GA + ATTN_W]
        sga = _sigmoid(ga)
        put(C_GA, dya * attn_s[...] * _dsilu(ga, sga))
        attn_s[...] = dya * (ga * sga)
        ks, vs = sv["ks"], sv["vs"]
        dks = [jnp.zeros((BLK, KV_W), F32) for _ in range(nb + 1)]
        dvs = [jnp.zeros((BLK, KV_W), F32) for _ in range(nb + 1)]
        for n in range(nb):
            rows = slice(n * BLK, (n + 1) * BLK)
            ydk, ydv = [], []
            for hk in range(2):
                qs, kd, vd = _attn_operands(h_s, ks, vs, n, hk, lo, lo2)
                probs, ps = sv["probs"][2 * n + hk]
                dos = _stack_heads(attn_s[rows, 2 * hk * 128:(2 * hk + 1) * 128],
                                   attn_s[rows, (2 * hk + 1) * 128:(2 * hk + 2) * 128], lo)
                dp = _mm_nt(dos, vd)
                delta = jnp.sum(probs * dp, axis=-1, keepdims=True)
                dsk = -(ps * delta)
                for g in range(4):
                    j = 4 * hk + g
                    acc["dsink"][j:j + 1, :] += jnp.broadcast_to(_colsum(dsk[g * BLK:(g + 1) * BLK]), (1, 128))
                dss = probs * (dp - delta) * SCALE
                q0, q1 = _unstack_heads(_mm(dss, kd), lo)
                put(C_Q + 2 * hk * 128, q0, rows)
                put(C_Q + (2 * hk + 1) * 128, q1, rows)
                ydk.append(_mm_tn(dss, qs))
                ydv.append(_mm_tn(probs, dos))
            dk2 = _fold_head(ydk[0], ydk[1], lo2)
            dv2 = _fold_head(ydv[0], ydv[1], lo2)
            dks[n] = dks[n] + dk2[0:BLK]
            dks[n + 1] = dks[n + 1] + dk2[BLK:2 * BLK]
            dvs[n] = dvs[n] + dv2[0:BLK]
            dvs[n + 1] = dvs[n + 1] + dv2[BLK:2 * BLK]
        dks[nb] = dks[nb] + ckv_s[:, 0:KV_W]
        dvs[nb] = dvs[nb] + ckv_s[:, KV_W:2 * KV_W]
        ckv_s[:, 0:KV_W] = dks[0]
        ckv_s[:, KV_W:2 * KV_W] = dvs[0]
        put(C_K, jnp.concatenate(dks[1:], axis=0))
        put(C_V, jnp.concatenate(dvs[1:], axis=0))
        gb = h_s[:, C_GB:C_GB + SGU_W]
        sgb = _sigmoid(gb)
        gu, dgu = _gelu_parts(h_s[:, C_UB:C_UB + SGU_W])
        mixed = mix_s[...]
        put(C_GB, dyb * (gu * mixed) * _dsilu(gb, sgb))
        dsgu = dyb * (gb * sgb)
        put(C_UB, dsgu * mixed * dgu)
        dmixed = dsgu * gu
        gv, dgv = _gelu_parts(h_s[:, C_VB:C_VB + SGU_W])
        vn, vhat, vrstd = _ln_fwd(gv, w["vng"][...], w["vnb"][...])
        tri, wms = _sgu_weights(w)
        dbs_part = jnp.zeros((BLK, SGU_W), F32)
        for n in range(nb):
            rows = slice(n * BLK, (n + 1) * BLK)
            dbs_part = dbs_part + dmixed[rows]
            for g in range(SGU_G):
                cols = slice(g * 128, (g + 1) * 128)
                acc["dws"][g] += jnp.where(tri, _mm_nt(dmixed[rows, cols], vn[rows, cols]), 0.0)
                mix_s[rows, cols] = _mm_tn(wms[g], dmixed[rows, cols])
        acc["dbs"][...] += dbs_part
        dvn = mix_s[...]
        acc["dvng"][...] += _colsum(dvn * vhat)
        acc["dvnb"][...] += _colsum(dvn)
        put(C_VB, _ln_bwd(dvn, vhat, vrstd, w["vng"][...]) * dgv)

        @pl.when(i == nT - 1)
        def _():
            for g in range(SGU_G):
                cols = slice(g * 128, (g + 1) * 128)
                tot = jnp.sum(acc["dbs"][:, cols], axis=1, keepdims=True)
                acc["dbs"][:, cols] = jnp.broadcast_to(tot, (BLK, 128))

    return body


def _resident():
    return pl.BlockSpec(memory_space=pltpu.VMEM)


def _weight_specs(names):
    return [pl.BlockSpec(memory_space=pltpu.SMEM) if n == "sinks" else _resident() for n in names]


def _nbytes(a):
    n = jnp.dtype(a.dtype).itemsize
    for d in a.shape:
        n *= d
    return n


def _layer_params(resident=()):
    scoped = V7X_VMEM_BYTES - V7X_VMEM_RESERVE - sum(_nbytes(a) for a in resident)
    return pltpu.CompilerParams(dimension_semantics=("arbitrary",), vmem_limit_bytes=scoped)


def project(x, win, b, name):
    T = x.shape[0]
    R = ROWS_PROJ

    def body(x_ref, w_ref, b_ref, h_ref):
        xb = x_ref[...].astype(MXU_DTYPE)
        for c0 in range(0, N_COLS, _PROJ_CHUNK):
            c1 = min(c0 + _PROJ_CHUNK, N_COLS)
            h_ref[:, c0:c1] = jnp.dot(xb, w_ref[:, c0:c1], preferred_element_type=F32) + b_ref[:, c0:c1]

    return pl.pallas_call(
        body, name=name, grid=(T // R,),
        in_specs=[pl.BlockSpec((R, D_MODEL), lambda t: (t, 0)), _resident(), _resident()],
        out_specs=pl.BlockSpec((R, N_COLS), lambda t: (t, 0)),
        out_shape=jax.ShapeDtypeStruct((T, N_COLS), F32),
        compiler_params=_layer_params([win, b]),
    )(x, win, b)


_KV_BLOCK = C_K // (2 * KV_W)


def layer_forward(x, h, wts, S, name):
    T = x.shape[0]
    R = ROWS
    npt = R // BLK
    return pl.pallas_call(
        _make_fwd_body(R, S),
        name=name,
        grid=(T // R,),
        in_specs=[pl.BlockSpec((R, D_MODEL), lambda t: (t, 0)), pl.BlockSpec((R, N_COLS), lambda t: (t, 0)),
                  pl.BlockSpec((BLK, 2 * KV_W), lambda t: (jnp.maximum(t * npt - 1, 0), _KV_BLOCK))]
        + _weight_specs(_W_NAMES),
        out_specs=pl.BlockSpec((R, D_MODEL), lambda t: (t, 0)),
        out_shape=jax.ShapeDtypeStruct((T, D_MODEL), F32),
        scratch_shapes=[pltpu.VMEM((R, ATTN_W), F32), pltpu.VMEM((R, SGU_W), F32)],
        compiler_params=_layer_params([wts[n] for n in _W_NAMES]),
    )(x, h, h, *[wts[n] for n in _W_NAMES])


_ACC_SHAPES = dict(dwout=(D_MODEL, D_MODEL), dpa=(ATTN_W, D_MODEL), dpb=(SGU_W, D_MODEL), dbin=(1, N_COLS),
                   dbout=(1, D_MODEL), dlng=(1, D_MODEL), dlnb=(1, D_MODEL), dvng=(1, SGU_W), dvnb=(1, SGU_W),
                   dws=(SGU_G, BLK, BLK), dbs=(BLK, SGU_W), dsink=(8, 128), loss=(8, 128))


def layer_backward(x, h, aux, wts, S, is_last, name):
    T = x.shape[0]
    R = ROWS_BWD
    nT = T // R
    npt = R // BLK
    rev = lambda i: (nT - 1 - i, 0)
    halo = lambda i: (jnp.maximum((nT - 1 - i) * npt - 1, 0), _KV_BLOCK)
    names = _W_NAMES + _WT_NAMES
    out_shape = ([jax.ShapeDtypeStruct((T, D_MODEL), F32), jax.ShapeDtypeStruct((T, N_COLS), MXU_DTYPE)]
                 + [jax.ShapeDtypeStruct(_ACC_SHAPES[n], F32) for n in _ACC_NAMES])
    outs = pl.pallas_call(
        _make_bwd_body(R, S, nT, is_last),
        name=name,
        grid=(nT,),
        in_specs=[pl.BlockSpec((R, D_MODEL), rev), pl.BlockSpec((R, N_COLS), rev), pl.BlockSpec((BLK, 2 * KV_W), halo),
                  pl.BlockSpec((R, D_MODEL), rev)] + _weight_specs(names),
        out_specs=[pl.BlockSpec((R, D_MODEL), rev), pl.BlockSpec((R, N_COLS), rev)] + [_resident() for _ in _ACC_NAMES],
        out_shape=out_shape,
        scratch_shapes=[pltpu.VMEM((R, ATTN_W), F32), pltpu.VMEM((R, SGU_W), F32), pltpu.VMEM((BLK, 2 * KV_W), F32)],
        compiler_params=_layer_params([wts[n] for n in names] + out_shape[2:]),
    )(x, h, h, aux, *[wts[n] for n in names])
    return outs[0], outs[1], dict(zip(_ACC_NAMES, outs[2:]))


def input_grads(dres, dh, x, winT, name):
    T = x.shape[0]
    R = ROWS

    def body(dres_ref, dh_ref, x_ref, winT_ref, dx_ref, dwin_ref):
        @pl.when(pl.program_id(0) == 0)
        def _():
            dwin_ref[...] = jnp.zeros(dwin_ref.shape, F32)

        dh = dh_ref[...]
        dx_ref[...] = dres_ref[...] + jnp.dot(dh, winT_ref[...], preferred_element_type=F32)
        dwin_ref[...] += _mm_tn(dh, x_ref[...])

    row = lambda t: (t, 0)
    return pl.pallas_call(
        body,
        name=name,
        grid=(T // R,),
        in_specs=[pl.BlockSpec((R, D_MODEL), row), pl.BlockSpec((R, N_COLS), row), pl.BlockSpec((R, D_MODEL), row),
                  _resident()],
        out_specs=[pl.BlockSpec((R, D_MODEL), row), _resident()],
        out_shape=[jax.ShapeDtypeStruct((T, D_MODEL), F32), jax.ShapeDtypeStruct((N_COLS, D_MODEL), F32)],
        compiler_params=_layer_params([winT, jax.ShapeDtypeStruct((N_COLS, D_MODEL), F32)]),
    )(dres, dh, x, winT)


def input_norm_forward(x, g, b, name):
    T = x.shape[0]
    R = ROWS

    def body(x_ref, g_ref, b_ref, y_ref):
        y_ref[...] = _ln_fwd(x_ref[...], g_ref[...], b_ref[...])[0]

    row = lambda t: (t, 0)
    return pl.pallas_call(
        body, name=name, grid=(T // R,),
        in_specs=[pl.BlockSpec((R, D_MODEL), row), _resident(), _resident()],
        out_specs=pl.BlockSpec((R, D_MODEL), row),
        out_shape=jax.ShapeDtypeStruct((T, D_MODEL), F32),
        compiler_params=_layer_params(),
    )(x, g, b)


def input_norm_backward(x, dy, g, name):
    T = x.shape[0]
    R = ROWS

    def body(x_ref, dy_ref, g_ref, dx_ref, dg_ref, db_ref):
        @pl.when(pl.program_id(0) == 0)
        def _():
            dg_ref[...] = jnp.zeros(dg_ref.shape, F32)
            db_ref[...] = jnp.zeros(db_ref.shape, F32)

        dy = dy_ref[...]
        _, xhat, rstd = _ln_fwd(x_ref[...], g_ref[...], g_ref[...])
        dx_ref[...] = _ln_bwd(dy, xhat, rstd, g_ref[...])
        dg_ref[...] += _colsum(dy * xhat)
        db_ref[...] += _colsum(dy)

    row = lambda t: (t, 0)
    vec = jax.ShapeDtypeStruct((1, D_MODEL), F32)
    return pl.pallas_call(
        body, name=name, grid=(T // R,),
        in_specs=[pl.BlockSpec((R, D_MODEL), row), pl.BlockSpec((R, D_MODEL), row), _resident()],
        out_specs=[pl.BlockSpec((R, D_MODEL), row), _resident(), _resident()],
        out_shape=[jax.ShapeDtypeStruct((T, D_MODEL), F32), vec, vec],
        compiler_params=_layer_params(),
    )(x, dy, g)


def local_step(x, target, ln_in_g, ln_in_b, layers, S):
    x0 = input_norm_forward(x, ln_in_g, ln_in_b, "ln_in_fwd")
    h0 = project(x0, layers[0]["win"], layers[0]["bin"], "layer0_proj")
    x1 = layer_forward(x0, h0, layers[0], S, "layer0_fwd")
    h1 = project(x1, layers[1]["win"], layers[1]["bin"], "layer1_proj")
    dres1, dh1, g1 = layer_backward(x1, h1, target, layers[1], S, True, "layer1_bwd")
    dx1, g1["dwinT"] = input_grads(dres1, dh1, x1, layers[1]["winT"], "layer1_dx_dwin")
    dres0, dh0, g0 = layer_backward(x0, h0, dx1, layers[0], S, False, "layer0_bwd")
    dx0, g0["dwinT"] = input_grads(dres0, dh0, x0, layers[0]["winT"], "layer0_dx_dwin")
    gx, dg_in, db_in = input_norm_backward(x, dx0, ln_in_g, "ln_in_bwd")
    return g1["loss"][0, 0], gx, dg_in, db_in, [g0, g1]


_ANY = pl.BlockSpec(memory_space=pl.ANY)
_DMA = pltpu.SemaphoreType.DMA

_PIECES = ("winT", "pa", "pb", "wout")
_WHOLE = dict(winT=(N_COLS, D_MODEL), pa=(ATTN_W, D_MODEL), pb=(SGU_W, D_MODEL), wout=(D_MODEL, D_MODEL))
_HALF = dict(winT=(N_COLS // 8, D_MODEL), pa=(ATTN_W // 2, D_MODEL // 4), pb=(SGU_W // 2, D_MODEL // 4),
             wout=(D_MODEL // 8, D_MODEL))
SMALL_ROWS = 1280


def _region(name, ref, k, h):
    hr, hc = _HALF[name]
    if name in ("pa", "pb"):
        return ref.at[pl.ds(h * hr, hr), pl.ds(k * hc, hc)]
    return ref.at[pl.ds(k * 2 * hr + h * hr, hr), :]


def _place():
    x, y, c = lax.axis_index("x"), lax.axis_index("y"), lax.axis_index("c")
    return x, y, c, 2 * x + y


def _as_chip(j, fn):
    for jj in range(N_CHIPS):
        pl.when(j == jj)(functools.partial(fn, jj))


def _remote(src, dst, send_sem, recv_sem, to):
    return pltpu.make_async_remote_copy(src_ref=src, dst_ref=dst, send_sem=send_sem, recv_sem=recv_sem,
                                        device_id=to, device_id_type=MESH)


def _core_of(k, c):
    return (k // 2, k % 2, c)


def gather_weights(names, shards):
    n = len(names)

    def body(*refs):
        src, dst = refs[:n], refs[n:2 * n]
        send1, recv1, send2, recv2 = refs[2 * n:]
        x, y, c, j = _place()
        sib = (x, y, 1 - c)

        def run(jj):
            others = [k for k in range(N_CHIPS) if k != jj]
            first = []
            for i, name in enumerate(names):
                hr = _HALF[name][0]
                for k in others:
                    first.append(_remote(src[i].at[pl.ds(c * hr, hr)], _region(name, dst[i], jj, c),
                                         send1.at[4 * i + k], recv1.at[4 * i + jj], _core_of(k, c)))
            for cp in first:
                cp.start()
            passed = []
            for i, name in enumerate(names):
                for k in others:
                    land = _region(name, dst[i], k, c)
                    _remote(land, land, send1.at[4 * i + k], recv1.at[4 * i + k], _core_of(k, c)).wait_recv()
                    cp = _remote(land, land, send2.at[4 * i + k], recv2.at[4 * i + k], sib)
                    cp.start()
                    passed.append(cp)
            for i, name in enumerate(names):
                for k in others:
                    land = _region(name, dst[i], k, 1 - c)
                    _remote(land, land, send2.at[4 * i + k], recv2.at[4 * i + k], sib).wait_recv()
            for cp in first + passed:
                cp.wait_send()

        _as_chip(j, run)

    return pl.pallas_call(
        body, name="gather_weights", in_specs=[_ANY] * n, out_specs=[_ANY] * n,
        out_shape=[jax.ShapeDtypeStruct(_WHOLE[nm], s.dtype) for nm, s in zip(names, shards)],
        scratch_shapes=[_DMA((4 * n,)), _DMA((4 * n,)), _DMA((4 * n,)), _DMA((4 * n,))],
    )(*shards)


def pair_send_halves(names, parts, small):
    n = len(names)

    def body(*refs):
        src, small_src = refs[:n], refs[n]
        dst, small_dst = refs[n + 1:2 * n + 1], refs[2 * n + 1]
        send, recv = refs[2 * n + 2:]
        x, y, c, _ = _place()
        sib = (x, y, 1 - c)
        cps = []
        for i, name in enumerate(names):
            for k in range(N_CHIPS):
                cps.append(_remote(_region(name, src[i], k, 1 - c), dst[i].at[k], send.at[4 * i + k], recv.at[4 * i + k], sib))
        hs = SMALL_ROWS // 2
        cps.append(_remote(small_src.at[pl.ds((1 - c) * hs, hs)], small_dst, send.at[4 * n], recv.at[4 * n], sib))
        for cp in cps:
            cp.start()
        for cp in cps:
            cp.wait()

    return pl.pallas_call(
        body, name="pair_send_halves", in_specs=[_ANY] * (n + 1), out_specs=[_ANY] * (n + 1),
        out_shape=[jax.ShapeDtypeStruct((N_CHIPS,) + _HALF[nm], F32) for nm in names]
        + [jax.ShapeDtypeStruct((SMALL_ROWS // 2, 128), F32)],
        scratch_shapes=[_DMA((4 * n + 1,)), _DMA((4 * n + 1,))],
    )(*parts, small)


def chip_exchange(sums, small):
    n = len(sums)

    def body(*refs):
        src, small_src = refs[:n], refs[n]
        dst, small_dst = refs[n + 1:2 * n + 1], refs[2 * n + 1]
        send, recv, lsem = refs[2 * n + 2:]
        x, y, c, j = _place()

        def run(jj):
            others = [k for k in range(N_CHIPS) if k != jj]
            own = pltpu.make_async_copy(small_src, small_dst.at[jj], lsem)
            out = []
            for k in others:
                for i in range(n):
                    out.append(_remote(src[i].at[k], dst[i].at[jj], send.at[4 * i + k], recv.at[4 * i + jj], _core_of(k, c)))
                out.append(_remote(small_src, small_dst.at[jj], send.at[4 * n + k], recv.at[4 * n + jj], _core_of(k, c)))
            own.start()
            for cp in out:
                cp.start()
            for k in others:
                for i in range(n):
                    _remote(src[i].at[k], dst[i].at[k], send.at[4 * i + k], recv.at[4 * i + k], _core_of(k, c)).wait_recv()
                _remote(small_src, small_dst.at[k], send.at[4 * n + k], recv.at[4 * n + k], _core_of(k, c)).wait_recv()
            for cp in out:
                cp.wait_send()
            own.wait()

        _as_chip(j, run)

    return pl.pallas_call(
        body, name="chip_exchange", in_specs=[_ANY] * (n + 1), out_specs=[_ANY] * (n + 1),
        out_shape=[jax.ShapeDtypeStruct(a.shape, a.dtype) for a in sums]
        + [jax.ShapeDtypeStruct((N_CHIPS,) + small.shape, small.dtype)],
        scratch_shapes=[_DMA((4 * n + 4,)), _DMA((4 * n + 4,)), _DMA(())],
    )(*sums, small)


def pair_send_totals(arrs):
    n = len(arrs)

    def body(*refs):
        src, dst = refs[:n], refs[n:2 * n]
        send, recv = refs[2 * n:]
        x, y, c, _ = _place()
        cps = [_remote(src[i], dst[i], send.at[i], recv.at[i], (x, y, 1 - c)) for i in range(n)]
        for cp in cps:
            cp.start()
        for cp in cps:
            cp.wait()

    return pl.pallas_call(
        body, name="pair_send_totals", in_specs=[_ANY] * n, out_specs=[_ANY] * n,
        out_shape=[jax.ShapeDtypeStruct(a.shape, a.dtype) for a in arrs],
        scratch_shapes=[_DMA((n,)), _DMA((n,))],
    )(*arrs)


def _plain_params():
    return pltpu.CompilerParams(dimension_semantics=("arbitrary",) * 2, vmem_limit_bytes=V7X_VMEM_BYTES // 2)


def add_own_halves(names, parts, landed, core):
    n = len(names)

    def body(core_ref, *refs):
        for i in range(n):
            refs[2 * n + i][0] = (refs[i][...] + refs[n + i][0]).astype(MXU_DTYPE)

    in_specs, out_specs = [], []
    for name in names:
        hr, hc = _HALF[name]
        if name in ("pa", "pb"):
            in_specs.append(pl.BlockSpec((hr // 2, hc), lambda k, s, cr: (cr[0] * 2 + s, k)))
        else:
            in_specs.append(pl.BlockSpec((hr // 2, hc), lambda k, s, cr: (k * 4 + cr[0] * 2 + s, 0)))
    for name in names:
        hr, hc = _HALF[name]
        in_specs.append(pl.BlockSpec((1, hr // 2, hc), lambda k, s, cr: (k, s, 0)))
        out_specs.append(pl.BlockSpec((1, hr // 2, hc), lambda k, s, cr: (k, s, 0)))
    return pl.pallas_call(
        body, name="add_own_halves",
        grid_spec=pltpu.PrefetchScalarGridSpec(num_scalar_prefetch=1, grid=(N_CHIPS, 2), in_specs=in_specs, out_specs=out_specs),
        out_shape=[jax.ShapeDtypeStruct((N_CHIPS,) + _HALF[nm], MXU_DTYPE) for nm in names],
        compiler_params=_plain_params(),
    )(core, *parts, *landed)


def add_small_half(small, landed, core):
    hs = SMALL_ROWS // 2

    def body(core_ref, a_ref, b_ref, o_ref):
        o_ref[...] = a_ref[...] + b_ref[...]

    return pl.pallas_call(
        body, name="add_small_half",
        grid_spec=pltpu.PrefetchScalarGridSpec(
            num_scalar_prefetch=1, grid=(1,),
            in_specs=[pl.BlockSpec((hs, 128), lambda s, cr: (cr[0], 0)), pl.BlockSpec((hs, 128), lambda s, cr: (0, 0))],
            out_specs=pl.BlockSpec((hs, 128), lambda s, cr: (0, 0))),
        out_shape=jax.ShapeDtypeStruct((hs, 128), F32),
    )(core, small, landed)


def sum_chips(sums, landed, small_landed, chips):
    n = len(sums)

    def body(chips_ref, *refs):
        for i in range(n):
            tot = refs[4 * i][0].astype(F32)
            for r in range(1, N_CHIPS):
                tot = tot + refs[4 * i + r][0].astype(F32)
            refs[4 * n + 1 + i][...] = tot

        @pl.when(pl.program_id(0) == 0)
        def _():
            sm = refs[4 * n]
            refs[5 * n + 1][...] = ((sm[0] + sm[1]) + sm[2]) + sm[3]

    in_specs, out_specs, operands = [], [], []
    for a, l in zip(sums, landed):
        _, hr, hc = a.shape
        for r in range(N_CHIPS):
            in_specs.append(pl.BlockSpec((1, hr // 2, hc), functools.partial(lambda s, ch, r: (ch[r], s, 0), r=r)))
            operands.append(a if r == 0 else l)
        out_specs.append(pl.BlockSpec((hr // 2, hc), lambda s, ch: (s, 0)))
    in_specs.append(pl.BlockSpec(small_landed.shape, lambda s, ch: (0, 0, 0)))
    out_specs.append(pl.BlockSpec(small_landed.shape[1:], lambda s, ch: (0, 0)))
    return pl.pallas_call(
        body, name="sum_chips",
        grid_spec=pltpu.PrefetchScalarGridSpec(num_scalar_prefetch=1, grid=(2,), in_specs=in_specs, out_specs=out_specs),
        out_shape=[jax.ShapeDtypeStruct(a.shape[1:], F32) for a in sums] + [jax.ShapeDtypeStruct(small_landed.shape[1:], F32)],
        compiler_params=pltpu.CompilerParams(dimension_semantics=("arbitrary",), vmem_limit_bytes=V7X_VMEM_BYTES // 2),
    )(chips, *operands, small_landed)


N_STEPS = 8


def _rows2d(a):
    return a.reshape((-1, a.shape[-1]))


def adamw(ws, gs, ms, vs):
    n = len(ws)
    c1 = 1.0 / (1.0 - ADAM_B1 ** ADAM_STEP)
    c2 = 1.0 / (1.0 - ADAM_B2 ** ADAM_STEP)

    def body(*refs):
        for i in range(n):
            w, g, m, v = (refs[k * n + i][...] for k in range(4))
            m = ADAM_B1 * m + (1.0 - ADAM_B1) * g
            v = ADAM_B2 * v + (1.0 - ADAM_B2) * (g * g)
            refs[4 * n + i][...] = -ADAM_LR * ((m * c1) / (jnp.sqrt(v * c2) + ADAM_EPS) + ADAM_WD * w)
            refs[5 * n + i][...] = m
            refs[6 * n + i][...] = v

    spec = lambda a: pl.BlockSpec((a.shape[0] // N_STEPS, a.shape[1]), lambda s: (s, 0))
    outs = pl.pallas_call(
        body, name="adamw", grid=(N_STEPS,),
        in_specs=[spec(a) for a in ws] * 4, out_specs=[spec(a) for a in ws] * 3,
        out_shape=[jax.ShapeDtypeStruct(a.shape, F32) for a in ws] * 3,
        compiler_params=pltpu.CompilerParams(dimension_semantics=("arbitrary",), vmem_limit_bytes=V7X_VMEM_BYTES // 2),
    )(*ws, *gs, *ms, *vs)
    return outs[:n], outs[n:2 * n], outs[2 * n:]


_SMALL = (("ln_in_g", (D_MODEL,)), ("ln_in_b", (D_MODEL,)), ("b_in", (DEPTH, N_COLS)), ("sinks", (DEPTH, 8)),
          ("vn_g", (DEPTH, SGU_W)), ("vn_b", (DEPTH, SGU_W)), ("w_s", (DEPTH, SGU_G, BLK, BLK)),
          ("b_s", (DEPTH, SGU_G, BLK)), ("b_out", (DEPTH, D_MODEL)), ("ln_g", (DEPTH, D_MODEL)), ("ln_b", (DEPTH, D_MODEL)))
_BIG = ("w_in", "p_a", "p_b", "w_out")


def _pack_small(vals):
    flat = []
    for name, shape in _SMALL:
        v = vals[name].reshape((-1,))
        flat.append(jnp.pad(v, (0, (-v.shape[0]) % 128)))
    flat = jnp.concatenate(flat)
    return jnp.pad(flat, (0, SMALL_ROWS * 128 - flat.shape[0])).reshape((SMALL_ROWS, 128))


def _unpack_small(packed):
    flat = packed.reshape((-1,))
    out, pos = {}, 0
    for name, shape in _SMALL:
        size = 1
        for d in shape:
            size *= d
        out[name] = flat[pos:pos + size].reshape(shape)
        pos += size + (-size) % 128
    return out


def _shard_of(name, whole, j):
    hr, hc = _HALF[name]
    if name in ("pa", "pb"):
        return (0, j * hc), (2 * hr, hc)
    return (j * 2 * hr, 0), (2 * hr, hc)


def kernel(x, ln_in_g, ln_in_b, w_in, b_in, sinks, vn_g, vn_b, w_s, b_s, p_a, p_b, w_out, b_out, ln_g, ln_b, loss_target, m_ln_in_g, m_ln_in_b, m_w_in, m_b_in, m_sinks, m_vn_g, m_vn_b, m_w_s, m_b_s, m_p_a, m_p_b, m_w_out, m_b_out, m_ln_g, m_ln_b, v_ln_in_g, v_ln_in_b, v_w_in, v_b_in, v_sinks, v_vn_g, v_vn_b, v_w_s, v_b_s, v_p_a, v_p_b, v_w_out, v_b_out, v_ln_g, v_ln_b):
    weights = dict(ln_in_g=ln_in_g, ln_in_b=ln_in_b, w_in=w_in, b_in=b_in, sinks=sinks, vn_g=vn_g, vn_b=vn_b, w_s=w_s,
                   b_s=b_s, p_a=p_a, p_b=p_b, w_out=w_out, b_out=b_out, ln_g=ln_g, ln_b=ln_b)
    mom1 = dict(ln_in_g=m_ln_in_g, ln_in_b=m_ln_in_b, w_in=m_w_in, b_in=m_b_in, sinks=m_sinks, vn_g=m_vn_g, vn_b=m_vn_b,
                w_s=m_w_s, b_s=m_b_s, p_a=m_p_a, p_b=m_p_b, w_out=m_w_out, b_out=m_b_out, ln_g=m_ln_g, ln_b=m_ln_b)
    mom2 = dict(ln_in_g=v_ln_in_g, ln_in_b=v_ln_in_b, w_in=v_w_in, b_in=v_b_in, sinks=v_sinks, vn_g=v_vn_g, vn_b=v_vn_b,
                w_s=v_w_s, b_s=v_b_s, p_a=v_p_a, p_b=v_p_b, w_out=v_w_out, b_out=v_b_out, ln_g=v_ln_g, ln_b=v_ln_b)
    n_seq, S, _ = x.shape
    T = n_seq * S
    c = lax.axis_index("c")
    j = 2 * lax.axis_index("x") + lax.axis_index("y")
    core = c.astype(jnp.int32).reshape((1,))
    chips = jnp.stack([j] + [r + (r >= j) for r in range(N_CHIPS - 1)]).astype(jnp.int32)
    names = [nm for _ in range(DEPTH) for nm in _PIECES]

    own = []
    for l in range(DEPTH):
        own += [w_in[l].T.astype(MXU_DTYPE), p_a[l].astype(MXU_DTYPE), p_b[l].astype(MXU_DTYPE), w_out[l].astype(MXU_DTYPE)]
    gathered = gather_weights(names, own)
    gathered = [lax.dynamic_update_slice(g, s, _shard_of(nm, g, j)[0]) for nm, g, s in zip(names, gathered, own)]
    layers = []
    for l in range(DEPTH):
        d = dict(zip(_PIECES, gathered[4 * l:4 * l + 4]))
        d.update(win=d["winT"].T, paT=d["pa"].T, pbT=d["pb"].T, woutT=d["wout"].T)
        d.update(bin=b_in[l][None], sinks=sinks[l], vng=vn_g[l][None], vnb=vn_b[l][None], ws=w_s[l],
                 bs=jnp.repeat(b_s[l].T, 128, axis=1), bout=b_out[l][None], lng=ln_g[l][None], lnb=ln_b[l][None])
        layers.append(d)

    loss_part, grad_x, d_ln_in_g, d_ln_in_b, gl = local_step(
        x.reshape((T, D_MODEL)), loss_target.reshape((T, D_MODEL)), ln_in_g[None], ln_in_b[None], layers, S)
    loss = lax.psum(loss_part, ("x", "y", "c"))

    per_layer = lambda key, pick: jnp.stack([pick(gl[l][key]) for l in range(DEPTH)])
    row0 = lambda key: per_layer(key, lambda a: a[0])
    small_part = _pack_small(dict(
        ln_in_g=d_ln_in_g[0], ln_in_b=d_ln_in_b[0], b_in=row0("dbin"), sinks=per_layer("dsink", lambda a: a[:, 0]),
        vn_g=row0("dvng"), vn_b=row0("dvnb"), w_s=per_layer("dws", lambda a: a),
        b_s=per_layer("dbs", lambda a: a[:, ::128].T), b_out=row0("dbout"), ln_g=row0("dlng"), ln_b=row0("dlnb")))
    parts = []
    for l in range(DEPTH):
        parts += [gl[l]["dwinT"], gl[l]["dpa"], gl[l]["dpb"], gl[l]["dwout"]]
    *landed, small_landed = pair_send_halves(names, parts, small_part)
    chip_sums = add_own_halves(names, parts, landed, core)
    small_sum = add_small_half(small_part, small_landed, core)
    *from_chips, small_from_chips = chip_exchange(chip_sums, small_sum)
    *totals, small_total = sum_chips(chip_sums, from_chips, small_from_chips, chips)
    *sib_totals, sib_small = pair_send_totals(totals + [small_total])

    def both_halves(mine, theirs):
        return jnp.where(c == 0, jnp.concatenate([mine, theirs], axis=0), jnp.concatenate([theirs, mine], axis=0))

    g_big = {}
    for pi, pname in enumerate(_BIG):
        per = [both_halves(totals[4 * l + pi], sib_totals[4 * l + pi]) for l in range(DEPTH)]
        g_big[pname] = jnp.stack([p.T for p in per] if pname == "w_in" else per)
    g_small = both_halves(small_total, sib_small)

    ws = [_rows2d(weights[n]) for n in _BIG] + [_pack_small(weights)]
    gs = [_rows2d(g_big[n]) for n in _BIG] + [g_small]
    ms = [_rows2d(mom1[n]) for n in _BIG] + [_pack_small(mom1)]
    vs = [_rows2d(mom2[n]) for n in _BIG] + [_pack_small(mom2)]
    deltas, new_ms, new_vs = adamw(ws, gs, ms, vs)

    def named(outs):
        d = _unpack_small(outs[-1])
        d.update({n: o.reshape(weights[n].shape) for n, o in zip(_BIG, outs[:-1])})
        return d

    grads = named([g_big[n] for n in _BIG] + [g_small])
    order = ("ln_in_g", "ln_in_b", "w_in", "b_in", "sinks", "vn_g", "vn_b", "w_s", "b_s", "p_a", "p_b", "w_out", "b_out",
             "ln_g", "ln_b")
    res = [loss, grad_x.reshape(x.shape)]
    for group in (grads, named(deltas), named(new_ms), named(new_vs)):
        res.extend(group[n] for n in order)
    return tuple(res)
```

```python
import functools

import jax
import jax.numpy as jnp
from jax import lax
from jax.experimental import pallas as pl
from jax.experimental.pallas import tpu as pltpu

F32 = jnp.float32
MXU_DTYPE = jnp.bfloat16

D_MODEL = 1024
DEPTH = 2
HEAD_DIM = 64
ATTN_W = 512
KV_W = 128
BLK = 128
SGU_W = 512
SGU_G = 4
N_COLS = 4864
C_Q, C_K, C_V, C_GA, C_UB, C_VB, C_GB, C_RA, C_RB = 0, 512, 640, 768, 1280, 1792, 2304, 2816, 3840
ALPHA = (2.0 * DEPTH) ** 0.25
LN_EPS = 1e-5
SCALE = HEAD_DIM ** -0.5
NEG = float(jnp.finfo(jnp.float32).min)
GELU_C = 0.7978845608028654
GELU_A = 0.044715

ADAM_LR, ADAM_B1, ADAM_B2, ADAM_EPS, ADAM_WD, ADAM_STEP = 0.001, 0.9, 0.999, 1e-08, 0.01, 10

V7X_VMEM_BYTES = 64 * 1024 * 1024
V7X_VMEM_RESERVE = 3 * 1024 * 1024
ROWS = 256
ROWS_BWD = 256
ROWS_PROJ = 512

MESH = pl.DeviceIdType.MESH
N_CHIPS = 4


def _mm(a, b):
    return jnp.dot(a.astype(MXU_DTYPE), b.astype(MXU_DTYPE), preferred_element_type=F32)


def _mm_nt(a, b):
    return lax.dot_general(a.astype(MXU_DTYPE), b.astype(MXU_DTYPE), (((1,), (1,)), ((), ())),
                           preferred_element_type=F32)


def _mm_tn(a, b):
    return lax.dot_general(a.astype(MXU_DTYPE), b.astype(MXU_DTYPE), (((0,), (0,)), ((), ())),
                           preferred_element_type=F32)


def _sigmoid(x):
    return 1.0 / (1.0 + jnp.exp(-x))


def _gelu_parts(x):
    x2 = x * x
    t = jnp.tanh(GELU_C * (x + GELU_A * x2 * x))
    g = 0.5 * x * (1.0 + t)
    dg = 0.5 * (1.0 + t) + 0.5 * x * (1.0 - t * t) * (GELU_C * (1.0 + 3.0 * GELU_A * x2))
    return g, dg


def _ln_fwd(z, g, b):
    mu = jnp.mean(z, axis=-1, keepdims=True)
    zc = z - mu
    var = jnp.mean(zc * zc, axis=-1, keepdims=True)
    rstd = lax.rsqrt(var + LN_EPS)
    zhat = zc * rstd
    return zhat * g + b, zhat, rstd


def _ln_bwd(dy, zhat, rstd, g):
    dzh = dy * g
    m1 = jnp.mean(dzh, axis=-1, keepdims=True)
    m2 = jnp.mean(dzh * zhat, axis=-1, keepdims=True)
    return rstd * (dzh - m1 - zhat * m2)


def _colsum(v):
    return jnp.sum(v, axis=0, keepdims=True)


def _lane_lo(rows):
    return lax.broadcasted_iota(jnp.int32, (rows, 128), 1) < HEAD_DIM


def _dup_head(x2, hk, lo):
    xr = pltpu.roll(x2, HEAD_DIM, 1)
    return jnp.where(lo, x2, xr) if hk == 0 else jnp.where(lo, xr, x2)


def _fold_head(y0, y1, lo):
    f0 = y0 + pltpu.roll(y0, HEAD_DIM, 1)
    f1 = y1 + pltpu.roll(y1, HEAD_DIM, 1)
    return jnp.where(lo, f0, f1)


def _stack_heads(t0, t1, lo):
    z = jnp.zeros_like(t0)
    return jnp.concatenate([jnp.where(lo, t0, z), jnp.where(lo, z, t0),
                            jnp.where(lo, t1, z), jnp.where(lo, z, t1)], axis=0)


def _unstack_heads(o, lo):
    return (jnp.where(lo, o[0:128], o[128:256]), jnp.where(lo, o[256:384], o[384:512]))


def _attn_probs(qs, kd, sink4, start):
    s = _mm_nt(qs, kd) * SCALE
    row = lax.broadcasted_iota(jnp.int32, (4 * BLK, 2 * BLK), 0) & (BLK - 1)
    kpos = lax.broadcasted_iota(jnp.int32, (4 * BLK, 2 * BLK), 1)
    valid = (kpos > row) & (kpos <= row + BLK)
    if start is not None:
        valid = valid & (jnp.logical_not(start) | (kpos >= BLK))
    s = jnp.where(valid, s, NEG)
    r1 = lax.broadcasted_iota(jnp.int32, (4 * BLK, 1), 0)
    sk = jnp.where(r1 < BLK, sink4[0], jnp.where(r1 < 2 * BLK, sink4[1], jnp.where(r1 < 3 * BLK, sink4[2], sink4[3])))
    m = jnp.maximum(jnp.max(s, axis=-1, keepdims=True), sk)
    p = jnp.exp(s - m)
    es = jnp.exp(sk - m)
    denom = jnp.sum(p, axis=-1, keepdims=True) + es
    return p / denom, es / denom


_PROJ_CHUNK = 512


def _kv_blocks(kvh, h_s, nb):
    ks = [kvh[:, 0:KV_W]] + [h_s[n * BLK:(n + 1) * BLK, C_K:C_K + KV_W] for n in range(nb)]
    vs = [kvh[:, KV_W:2 * KV_W]] + [h_s[n * BLK:(n + 1) * BLK, C_V:C_V + KV_W] for n in range(nb)]
    return ks, vs


def _attn_operands(h_s, ks, vs, n, hk, lo, lo2):
    rows = slice(n * BLK, (n + 1) * BLK)
    kd = _dup_head(jnp.concatenate([ks[n], ks[n + 1]], axis=0), hk, lo2)
    vd = _dup_head(jnp.concatenate([vs[n], vs[n + 1]], axis=0), hk, lo2)
    c0 = C_Q + 2 * hk * 128
    qs = _stack_heads(h_s[rows, c0:c0 + 128], h_s[rows, c0 + 128:c0 + 256], lo)
    return qs, kd, vd


def _sgu_weights(w):
    tri = (lax.broadcasted_iota(jnp.int32, (BLK, BLK), 0) >= lax.broadcasted_iota(jnp.int32, (BLK, BLK), 1))
    return tri, [jnp.where(tri, w["ws"][g], 0.0) for g in range(SGU_G)]


def _layer_forward(x, kvh, start, w, h_s, attn_s, mix_s, keep):
    R = x.shape[0]
    nb = R // BLK
    lo = _lane_lo(BLK)
    lo2 = _lane_lo(2 * BLK)
    ks, vs = _kv_blocks(kvh, h_s, nb)
    sinks = [w["sinks"][j] for j in range(8)]
    probs_l = []
    for n in range(nb):
        rows = slice(n * BLK, (n + 1) * BLK)
        for hk in range(2):
            qs, kd, vd = _attn_operands(h_s, ks, vs, n, hk, lo, lo2)
            probs, ps = _attn_probs(qs, kd, sinks[4 * hk:4 * hk + 4], start if n == 0 else None)
            t0, t1 = _unstack_heads(_mm(probs, vd), lo)
            attn_s[rows, 2 * hk * 128:(2 * hk + 1) * 128] = t0
            attn_s[rows, (2 * hk + 1) * 128:(2 * hk + 2) * 128] = t1
            if keep:
                probs_l.append((probs, ps))
    ga = h_s[:, C_GA:C_GA + ATTN_W]
    ya = attn_s[...] * (ga * _sigmoid(ga))
    gu, _ = _gelu_parts(h_s[:, C_UB:C_UB + SGU_W])
    gv, _ = _gelu_parts(h_s[:, C_VB:C_VB + SGU_W])
    vn, _, _ = _ln_fwd(gv, w["vng"][...], w["vnb"][...])
    _, wms = _sgu_weights(w)
    for n in range(nb):
        rows = slice(n * BLK, (n + 1) * BLK)
        for g in range(SGU_G):
            cols = slice(g * 128, (g + 1) * 128)
            mix_s[rows, cols] = _mm(wms[g], vn[rows, cols]) + w["bs"][:, cols]
    gb = h_s[:, C_GB:C_GB + SGU_W]
    yb = gu * mix_s[...] * (gb * _sigmoid(gb))
    a = _mm(ya, w["pa"][...])
    b = _mm(yb, w["pb"][...])
    merged = _sigmoid(h_s[:, C_RA:C_RA + D_MODEL]) * a + _sigmoid(h_s[:, C_RB:C_RB + D_MODEL]) * b
    out = _mm(merged, w["wout"][...]) + w["bout"][...]
    y, zhat, rstd = _ln_fwd(ALPHA * x + out, w["lng"][...], w["lnb"][...])
    if not keep:
        return y, None
    return y, dict(ks=ks, vs=vs, probs=probs_l, ya=ya, yb=yb, a=a, b=b, merged=merged, zhat=zhat, rstd=rstd)


def _dsilu(g, sg):
    return sg * (1.0 + g * (1.0 - sg))


_W_NAMES = ("sinks", "vng", "vnb", "ws", "bs", "pa", "pb", "wout", "bout", "lng", "lnb")
_WT_NAMES = ("paT", "pbT", "woutT")
_ACC_NAMES = ("dwout", "dpa", "dpb", "dbin", "dbout", "dlng", "dlnb", "dvng", "dvnb", "dws", "dbs", "dsink", "loss")


def _make_fwd_body(R, S):
    def body(x_ref, h_ref, kvh_ref, *rest):
        w = dict(zip(_W_NAMES, rest[:len(_W_NAMES)]))
        y_ref, attn_s, mix_s = rest[len(_W_NAMES):]
        start = (pl.program_id(0) % (S // R)) == 0
        y, _ = _layer_forward(x_ref[...], kvh_ref[...], start, w, h_ref, attn_s, mix_s, keep=False)
        y_ref[...] = y
    return body


def _make_bwd_body(R, S, nT, is_last):
    nb = R // BLK
    names = _W_NAMES + _WT_NAMES
    nw = len(names)

    def body(x_ref, h_s, kvh_ref, aux_ref, *rest):
        w = dict(zip(names, rest[:nw]))
        dres_ref, dh_ref = rest[nw:nw + 2]
        acc = dict(zip(_ACC_NAMES, rest[nw + 2:nw + 2 + len(_ACC_NAMES)]))
        attn_s, mix_s, ckv_s = rest[nw + 2 + len(_ACC_NAMES):]
        i = pl.program_id(0)
        start = ((nT - 1 - i) % (S // R)) == 0
        lo = _lane_lo(BLK)
        lo2 = _lane_lo(2 * BLK)

        @pl.when(i == 0)
        def _():
            for name in _ACC_NAMES:
                acc[name][...] = jnp.zeros(acc[name].shape, F32)
            ckv_s[...] = jnp.zeros(ckv_s.shape, F32)

        x = x_ref[...]
        y, sv = _layer_forward(x, kvh_ref[...], start, w, h_s, attn_s, mix_s, keep=True)
        if is_last:
            diff = y - aux_ref[...]
            part = 0.5 * jnp.sum(jnp.mean(diff * diff, axis=-1, keepdims=True), axis=0, keepdims=True)
            acc["loss"][...] += jnp.broadcast_to(part, acc["loss"].shape)
            dy = diff * (1.0 / D_MODEL)
        else:
            dy = aux_ref[...]

        def put(c0, val, rows=slice(None)):
            width = val.shape[1]
            dh_ref[rows, c0:c0 + width] = val.astype(dh_ref.dtype)
            acc["dbin"][:, c0:c0 + width] += _colsum(val)

        acc["dlng"][...] += _colsum(dy * sv["zhat"])
        acc["dlnb"][...] += _colsum(dy)
        dz = _ln_bwd(dy, sv["zhat"], sv["rstd"], w["lng"][...])
        dres_ref[...] = ALPHA * dz
        acc["dbout"][...] += _colsum(dz)
        acc["dwout"][...] += _mm_tn(sv["merged"], dz)
        dmerged = _mm(dz, w["woutT"][...])
        sa = _sigmoid(h_s[:, C_RA:C_RA + D_MODEL])
        da = dmerged * sa
        put(C_RA, da * sv["a"] * (1.0 - sa))
        sb = _sigmoid(h_s[:, C_RB:C_RB + D_MODEL])
        db = dmerged * sb
        put(C_RB, db * sv["b"] * (1.0 - sb))
        acc["dpa"][...] += _mm_tn(sv["ya"], da)
        acc["dpb"][...] += _mm_tn(sv["yb"], db)
        dya = _mm(da, w["paT"][...])
        dyb = _mm(db, w["pbT"][...])
        ga = h_s[:, C_GA:C_GA + ATTN_W]
        sga = _sigmoid(ga)
        put(C_GA, dya * attn_s[...] * _dsilu(ga, sga))
        attn_s[...] = dya * (ga * sga)
        ks, vs = sv["ks"], sv["vs"]
        dks = [jnp.zeros((BLK, KV_W), F32) for _ in range(nb + 1)]
        dvs = [jnp.zeros((BLK, KV_W), F32) for _ in range(nb + 1)]
        for n in range(nb):
            rows = slice(n * BLK, (n + 1) * BLK)
            ydk, ydv = [], []
            for hk in range(2):
                qs, kd, vd = _attn_operands(h_s, ks, vs, n, hk, lo, lo2)
                probs, ps = sv["probs"][2 * n + hk]
                dos = _stack_heads(attn_s[rows, 2 * hk * 128:(2 * hk + 1) * 128],
                                   attn_s[rows, (2 * hk + 1) * 128:(2 * hk + 2) * 128], lo)
                dp = _mm_nt(dos, vd)
                delta = jnp.sum(probs * dp, axis=-1, keepdims=True)
                dsk = -(ps * delta)
                for g in range(4):
                    j = 4 * hk + g
                    acc["dsink"][j:j + 1, :] += jnp.broadcast_to(_colsum(dsk[g * BLK:(g + 1) * BLK]), (1, 128))
                dss = probs * (dp - delta) * SCALE
                q0, q1 = _unstack_heads(_mm(dss, kd), lo)
                put(C_Q + 2 * hk * 128, q0, rows)
                put(C_Q + (2 * hk + 1) * 128, q1, rows)
                ydk.append(_mm_tn(dss, qs))
                ydv.append(_mm_tn(probs, dos))
            dk2 = _fold_head(ydk[0], ydk[1], lo2)
            dv2 = _fold_head(ydv[0], ydv[1], lo2)
            dks[n] = dks[n] + dk2[0:BLK]
            dks[n + 1] = dks[n + 1] + dk2[BLK:2 * BLK]
            dvs[n] = dvs[n] + dv2[0:BLK]
            dvs[n + 1] = dvs[n + 1] + dv2[BLK:2 * BLK]
        dks[nb] = dks[nb] + ckv_s[:, 0:KV_W]
        dvs[nb] = dvs[nb] + ckv_s[:, KV_W:2 * KV_W]
        ckv_s[:, 0:KV_W] = dks[0]
        ckv_s[:, KV_W:2 * KV_W] = dvs[0]
        put(C_K, jnp.concatenate(dks[1:], axis=0))
        put(C_V, jnp.concatenate(dvs[1:], axis=0))
        gb = h_s[:, C_GB:C_GB + SGU_W]
        sgb = _sigmoid(gb)
        gu, dgu = _gelu_parts(h_s[:, C_UB:C_UB + SGU_W])
        mixed = mix_s[...]
        put(C_GB, dyb * (gu * mixed) * _dsilu(gb, sgb))
        dsgu = dyb * (gb * sgb)
        put(C_UB, dsgu * mixed * dgu)
        dmixed = dsgu * gu
        gv, dgv = _gelu_parts(h_s[:, C_VB:C_VB + SGU_W])
        vn, vhat, vrstd = _ln_fwd(gv, w["vng"][...], w["vnb"][...])
        tri, wms = _sgu_weights(w)
        dbs_part = jnp.zeros((BLK, SGU_W), F32)
        for n in range(nb):
            rows = slice(n * BLK, (n + 1) * BLK)
            dbs_part = dbs_part + dmixed[rows]
            for g in range(SGU_G):
                cols = slice(g * 128, (g + 1) * 128)
                acc["dws"][g] += jnp.where(tri, _mm_nt(dmixed[rows, cols], vn[rows, cols]), 0.0)
                mix_s[rows, cols] = _mm_tn(wms[g], dmixed[rows, cols])
        acc["dbs"][...] += dbs_part
        dvn = mix_s[...]
        acc["dvng"][...] += _colsum(dvn * vhat)
        acc["dvnb"][...] += _colsum(dvn)
        put(C_VB, _ln_bwd(dvn, vhat, vrstd, w["vng"][...]) * dgv)

        @pl.when(i == nT - 1)
        def _():
            for g in range(SGU_G):
                cols = slice(g * 128, (g + 1) * 128)
                tot = jnp.sum(acc["dbs"][:, cols], axis=1, keepdims=True)
                acc["dbs"][:, cols] = jnp.broadcast_to(tot, (BLK, 128))

    return body


def _resident():
    return pl.BlockSpec(memory_space=pltpu.VMEM)


def _weight_specs(names):
    return [pl.BlockSpec(memory_space=pltpu.SMEM) if n == "sinks" else _resident() for n in names]


def _nbytes(a):
    n = jnp.dtype(a.dtype).itemsize
    for d in a.shape:
        n *= d
    return n


def _layer_params(resident=()):
    scoped = V7X_VMEM_BYTES - V7X_VMEM_RESERVE - sum(_nbytes(a) for a in resident)
    return pltpu.CompilerParams(dimension_semantics=("arbitrary",), vmem_limit_bytes=scoped)


class _Comm:
    def __init__(self, ins, out_shapes, sems, start, finish, aliases=None):
        self.ins = [pltpu.with_memory_space_constraint(a, pltpu.HBM) for a in ins]
        self.out_shapes = [pltpu.HBM(s.shape, s.dtype) for s in out_shapes]
        self.sems = list(sems)
        self.start, self.finish, self.aliases = start, finish, dict(aliases or {})


_ANY = pl.BlockSpec(memory_space=pltpu.HBM)
_DMA = pltpu.SemaphoreType.DMA


def _call(body, *, name, grid, in_specs, out_specs, out_shape, scratch_shapes, compiler_params, operands, comm=None):
    if comm is None:
        res = pl.pallas_call(body, name=name, grid=grid, in_specs=in_specs, out_specs=out_specs, out_shape=out_shape,
                             scratch_shapes=scratch_shapes, compiler_params=compiler_params)(*operands)
        return list(res), []
    n_in, n_out, n_scr = len(in_specs), len(out_specs), len(scratch_shapes)
    ci, co = len(comm.ins), len(comm.out_shapes)
    last = grid[0] - 1

    def hosted(*refs):
        ins, cin = refs[:n_in], refs[n_in:n_in + ci]
        p = n_in + ci
        outs, cout = refs[p:p + n_out], refs[p + n_out:p + n_out + co]
        p += n_out + co
        scr, sems = refs[p:p + n_scr], refs[p + n_scr:]
        pl.when(pl.program_id(0) == 0)(lambda: comm.start(cin, cout, sems))
        body(*ins, *outs, *scr)
        pl.when(pl.program_id(0) == last)(lambda: comm.finish(cin, cout, sems))

    res = pl.pallas_call(
        hosted, name=name, grid=grid, in_specs=list(in_specs) + [_ANY] * ci, out_specs=list(out_specs) + [_ANY] * co,
        out_shape=list(out_shape) + comm.out_shapes, scratch_shapes=list(scratch_shapes) + comm.sems,
        input_output_aliases={n_in + a: n_out + b for a, b in comm.aliases.items()},
        compiler_params=compiler_params)(*operands, *comm.ins)
    return list(res[:n_out]), list(res[n_out:])


def _run_comm(comm, name):
    ci, co = len(comm.ins), len(comm.out_shapes)

    def body(*refs):
        cin, cout, sems = refs[:ci], refs[ci:ci + co], refs[ci + co:]
        comm.start(cin, cout, sems)
        comm.finish(cin, cout, sems)

    return list(pl.pallas_call(body, name=name, in_specs=[_ANY] * ci, out_specs=[_ANY] * co, out_shape=comm.out_shapes,
                               scratch_shapes=comm.sems, input_output_aliases=comm.aliases)(*comm.ins))


def project(x, win, b, name, comm=None):
    T = x.shape[0]
    R = ROWS_PROJ

    def body(x_ref, w_ref, b_ref, h_ref):
        xb = x_ref[...].astype(MXU_DTYPE)
        for c0 in range(0, N_COLS, _PROJ_CHUNK):
            c1 = min(c0 + _PROJ_CHUNK, N_COLS)
            h_ref[:, c0:c1] = jnp.dot(xb, w_ref[:, c0:c1], preferred_element_type=F32) + b_ref[:, c0:c1]

    (h,), extra = _call(
        body, name=name, grid=(T // R,),
        in_specs=[pl.BlockSpec((R, D_MODEL), lambda t: (t, 0)), _resident(), _resident()],
        out_specs=[pl.BlockSpec((R, N_COLS), lambda t: (t, 0))],
        out_shape=[jax.ShapeDtypeStruct((T, N_COLS), F32)], scratch_shapes=[],
        compiler_params=_layer_params([win, b]), operands=(x, win, b), comm=comm)
    return h, extra


_KV_BLOCK = C_K // (2 * KV_W)


def layer_forward(x, h, wts, S, name, comm=None):
    T = x.shape[0]
    R = ROWS
    npt = R // BLK
    (y,), extra = _call(
        _make_fwd_body(R, S), name=name, grid=(T // R,),
        in_specs=[pl.BlockSpec((R, D_MODEL), lambda t: (t, 0)), pl.BlockSpec((R, N_COLS), lambda t: (t, 0)),
                  pl.BlockSpec((BLK, 2 * KV_W), lambda t: (jnp.maximum(t * npt - 1, 0), _KV_BLOCK))]
        + _weight_specs(_W_NAMES),
        out_specs=[pl.BlockSpec((R, D_MODEL), lambda t: (t, 0))],
        out_shape=[jax.ShapeDtypeStruct((T, D_MODEL), F32)],
        scratch_shapes=[pltpu.VMEM((R, ATTN_W), F32), pltpu.VMEM((R, SGU_W), F32)],
        compiler_params=_layer_params([wts[n] for n in _W_NAMES]),
        operands=(x, h, h, *[wts[n] for n in _W_NAMES]), comm=comm)
    return y, extra


_ACC_SHAPES = dict(dwout=(D_MODEL, D_MODEL), dpa=(ATTN_W, D_MODEL), dpb=(SGU_W, D_MODEL), dbin=(1, N_COLS),
                   dbout=(1, D_MODEL), dlng=(1, D_MODEL), dlnb=(1, D_MODEL), dvng=(1, SGU_W), dvnb=(1, SGU_W),
                   dws=(SGU_G, BLK, BLK), dbs=(BLK, SGU_W), dsink=(8, 128), loss=(8, 128))


def layer_backward(x, h, aux, wts, S, is_last, name, comm=None):
    T = x.shape[0]
    R = ROWS_BWD
    nT = T // R
    npt = R // BLK
    rev = lambda i: (nT - 1 - i, 0)
    halo = lambda i: (jnp.maximum((nT - 1 - i) * npt - 1, 0), _KV_BLOCK)
    names = _W_NAMES + _WT_NAMES
    out_shape = ([jax.ShapeDtypeStruct((T, D_MODEL), F32), jax.ShapeDtypeStruct((T, N_COLS), MXU_DTYPE)]
                 + [jax.ShapeDtypeStruct(_ACC_SHAPES[n], F32) for n in _ACC_NAMES])
    outs, extra = _call(
        _make_bwd_body(R, S, nT, is_last), name=name, grid=(nT,),
        in_specs=[pl.BlockSpec((R, D_MODEL), rev), pl.BlockSpec((R, N_COLS), rev), pl.BlockSpec((BLK, 2 * KV_W), halo),
                  pl.BlockSpec((R, D_MODEL), rev)] + _weight_specs(names),
        out_specs=[pl.BlockSpec((R, D_MODEL), rev), pl.BlockSpec((R, N_COLS), rev)] + [_resident() for _ in _ACC_NAMES],
        out_shape=out_shape,
        scratch_shapes=[pltpu.VMEM((R, ATTN_W), F32), pltpu.VMEM((R, SGU_W), F32), pltpu.VMEM((BLK, 2 * KV_W), F32)],
        compiler_params=_layer_params([wts[n] for n in names] + out_shape[2:]),
        operands=(x, h, h, aux, *[wts[n] for n in names]), comm=comm)
    return outs[0], outs[1], dict(zip(_ACC_NAMES, outs[2:])), extra


def input_grads(dres, dh, x, winT, name, comm=None):
    T = x.shape[0]
    R = ROWS

    def body(dres_ref, dh_ref, x_ref, winT_ref, dx_ref, dwin_ref):
        @pl.when(pl.program_id(0) == 0)
        def _():
            dwin_ref[...] = jnp.zeros(dwin_ref.shape, F32)

        dh = dh_ref[...]
        dx_ref[...] = dres_ref[...] + jnp.dot(dh, winT_ref[...], preferred_element_type=F32)
        dwin_ref[...] += _mm_tn(dh, x_ref[...])

    row = lambda t: (t, 0)
    dwin_shape = jax.ShapeDtypeStruct((N_COLS, D_MODEL), F32)
    (dx, dwinT), extra = _call(
        body, name=name, grid=(T // R,),
        in_specs=[pl.BlockSpec((R, D_MODEL), row), pl.BlockSpec((R, N_COLS), row), pl.BlockSpec((R, D_MODEL), row),
                  _resident()],
        out_specs=[pl.BlockSpec((R, D_MODEL), row), _resident()],
        out_shape=[jax.ShapeDtypeStruct((T, D_MODEL), F32), dwin_shape], scratch_shapes=[],
        compiler_params=_layer_params([winT, dwin_shape]), operands=(dres, dh, x, winT), comm=comm)
    return dx, dwinT, extra


def input_norm_forward(x, g, b, name, comm=None):
    T = x.shape[0]
    R = ROWS

    def body(x_ref, g_ref, b_ref, y_ref):
        y_ref[...] = _ln_fwd(x_ref[...], g_ref[...], b_ref[...])[0]

    row = lambda t: (t, 0)
    (y,), extra = _call(
        body, name=name, grid=(T // R,),
        in_specs=[pl.BlockSpec((R, D_MODEL), row), _resident(), _resident()],
        out_specs=[pl.BlockSpec((R, D_MODEL), row)],
        out_shape=[jax.ShapeDtypeStruct((T, D_MODEL), F32)], scratch_shapes=[],
        compiler_params=_layer_params(), operands=(x, g, b), comm=comm)
    return y, extra


def input_norm_backward(x, dy, g, name, comm=None):
    T = x.shape[0]
    R = ROWS

    def body(x_ref, dy_ref, g_ref, dx_ref, dg_ref, db_ref):
        @pl.when(pl.program_id(0) == 0)
        def _():
            dg_ref[...] = jnp.zeros(dg_ref.shape, F32)
            db_ref[...] = jnp.zeros(db_ref.shape, F32)

        dy = dy_ref[...]
        _, xhat, rstd = _ln_fwd(x_ref[...], g_ref[...], g_ref[...])
        dx_ref[...] = _ln_bwd(dy, xhat, rstd, g_ref[...])
        dg_ref[...] += _colsum(dy * xhat)
        db_ref[...] += _colsum(dy)

    row = lambda t: (t, 0)
    vec = jax.ShapeDtypeStruct((1, D_MODEL), F32)
    (dx, dg, db), extra = _call(
        body, name=name, grid=(T // R,),
        in_specs=[pl.BlockSpec((R, D_MODEL), row), pl.BlockSpec((R, D_MODEL), row), _resident()],
        out_specs=[pl.BlockSpec((R, D_MODEL), row), _resident(), _resident()],
        out_shape=[jax.ShapeDtypeStruct((T, D_MODEL), F32), vec, vec], scratch_shapes=[],
        compiler_params=_layer_params(), operands=(x, dy, g), comm=comm)
    return dx, dg, db, extra


_PIECES = ("winT", "pa", "pb", "wout")
_WHOLE = dict(winT=(N_COLS, D_MODEL), pa=(ATTN_W, D_MODEL), pb=(SGU_W, D_MODEL), wout=(D_MODEL, D_MODEL))
_HALF = dict(winT=(N_COLS // 8, D_MODEL), pa=(ATTN_W // 2, D_MODEL // 4), pb=(SGU_W // 2, D_MODEL // 4),
             wout=(D_MODEL // 8, D_MODEL))
SMALL_ROWS = 1280


def _region(name, ref, k, h):
    hr, hc = _HALF[name]
    if name in ("pa", "pb"):
        return ref.at[pl.ds(h * hr, hr), pl.ds(k * hc, hc)]
    return ref.at[pl.ds(k * 2 * hr + h * hr, hr), :]


def _place():
    x, y, c = lax.axis_index("x"), lax.axis_index("y"), lax.axis_index("c")
    return x, y, c, 2 * x + y


def _as_chip(j, fn):
    for jj in range(N_CHIPS):
        pl.when(j == jj)(functools.partial(fn, jj))


def _remote(src, dst, send_sem, recv_sem, to):
    return pltpu.make_async_remote_copy(src_ref=src, dst_ref=dst, send_sem=send_sem, recv_sem=recv_sem,
                                        device_id=to, device_id_type=MESH)


def _core_of(k, c):
    return (k // 2, k % 2, c)


def _others(jj):
    return [k for k in range(N_CHIPS) if k != jj]


def _start_all(cps):
    for cp in cps:
        cp.start()


def _wait_all(cps):
    for cp in cps:
        cp.wait()


def gather_over_chips(names, shards):
    n = len(names)

    def copies(cin, cout, sems, jj, c):
        out = []
        for i, name in enumerate(names):
            hr = _HALF[name][0]
            for k in _others(jj):
                out.append(_remote(cin[i].at[pl.ds(c * hr, hr)], _region(name, cout[i], jj, c),
                                   sems[0].at[4 * i + k], sems[1].at[4 * i + jj], _core_of(k, c)))
        return out

    def start(cin, cout, sems):
        _, _, c, j = _place()
        _as_chip(j, lambda jj: _start_all(copies(cin, cout, sems, jj, c)))

    def finish(cin, cout, sems):
        _, _, c, j = _place()

        def run(jj):
            for i, name in enumerate(names):
                for k in _others(jj):
                    land = _region(name, cout[i], k, c)
                    _remote(land, land, sems[0].at[4 * i + k], sems[1].at[4 * i + k], _core_of(k, c)).wait_recv()
            for cp in copies(cin, cout, sems, jj, c):
                cp.wait_send()

        _as_chip(j, run)

    shapes = [jax.ShapeDtypeStruct(_WHOLE[nm], s.dtype) for nm, s in zip(names, shards)]
    return _Comm(shards, shapes, [_DMA((4 * n,)), _DMA((4 * n,))], start, finish)


def gather_over_pair(names, wholes):
    n = len(names)

    def start(cin, cout, sems):
        x, y, c, j = _place()

        def run(jj):
            for i, name in enumerate(names):
                for k in _others(jj):
                    land = _region(name, cout[i], k, c)
                    _remote(land, land, sems[0].at[4 * i + k], sems[1].at[4 * i + k], (x, y, 1 - c)).start()

        _as_chip(j, run)

    def finish(cin, cout, sems):
        x, y, c, j = _place()

        def run(jj):
            for i, name in enumerate(names):
                for k in _others(jj):
                    theirs = _region(name, cout[i], k, 1 - c)
                    _remote(theirs, theirs, sems[0].at[4 * i + k], sems[1].at[4 * i + k], (x, y, 1 - c)).wait_recv()
            for i, name in enumerate(names):
                for k in _others(jj):
                    land = _region(name, cout[i], k, c)
                    _remote(land, land, sems[0].at[4 * i + k], sems[1].at[4 * i + k], (x, y, 1 - c)).wait_send()

        _as_chip(j, run)

    shapes = [jax.ShapeDtypeStruct(a.shape, a.dtype) for a in wholes]
    return _Comm(wholes, shapes, [_DMA((4 * n,)), _DMA((4 * n,))], start, finish, aliases={i: i for i in range(n)})


def pair_send_halves(names, parts, small=None):
    n = len(names)
    hs = SMALL_ROWS // 2

    def copies(cin, cout, sems):
        x, y, c, _ = _place()
        sib = (x, y, 1 - c)
        cps = []
        for i, name in enumerate(names):
            for k in range(N_CHIPS):
                cps.append(_remote(_region(name, cin[i], k, 1 - c), cout[i].at[k], sems[0].at[4 * i + k],
                                   sems[1].at[4 * i + k], sib))
        if small is not None:
            cps.append(_remote(cin[n].at[pl.ds((1 - c) * hs, hs)], cout[n], sems[0].at[4 * n], sems[1].at[4 * n], sib))
        return cps

    start = lambda cin, cout, sems: _start_all(copies(cin, cout, sems))
    finish = lambda cin, cout, sems: _wait_all(copies(cin, cout, sems))
    shapes = [jax.ShapeDtypeStruct((N_CHIPS,) + _HALF[nm], F32) for nm in names]
    ins = list(parts)
    if small is not None:
        shapes.append(jax.ShapeDtypeStruct((hs, 128), F32))
        ins.append(small)
    return _Comm(ins, shapes, [_DMA((4 * n + 1,)), _DMA((4 * n + 1,))], start, finish)


def chip_exchange(sums, small=None):
    n = len(sums)

    def copies(cin, cout, sems, jj, c):
        out = []
        for k in _others(jj):
            for i in range(n):
                out.append(_remote(cin[i].at[k], cout[i].at[jj], sems[0].at[4 * i + k], sems[1].at[4 * i + jj], _core_of(k, c)))
            if small is not None:
                out.append(_remote(cin[n], cout[n].at[jj], sems[0].at[4 * n + k], sems[1].at[4 * n + jj], _core_of(k, c)))
        return out

    def start(cin, cout, sems):
        _, _, c, j = _place()

        def run(jj):
            if small is not None:
                pltpu.make_async_copy(cin[n], cout[n].at[jj], sems[2]).start()
            for cp in copies(cin, cout, sems, jj, c):
                cp.start()

        _as_chip(j, run)

    def finish(cin, cout, sems):
        _, _, c, j = _place()

        def run(jj):
            for k in _others(jj):
                for i in range(n):
                    _remote(cin[i].at[k], cout[i].at[k], sems[0].at[4 * i + k], sems[1].at[4 * i + k], _core_of(k, c)).wait_recv()
                if small is not None:
                    _remote(cin[n], cout[n].at[k], sems[0].at[4 * n + k], sems[1].at[4 * n + k], _core_of(k, c)).wait_recv()
            for cp in copies(cin, cout, sems, jj, c):
                cp.wait_send()
            if small is not None:
                pltpu.make_async_copy(cin[n], cout[n].at[jj], sems[2]).wait()

        _as_chip(j, run)

    shapes = [jax.ShapeDtypeStruct(a.shape, a.dtype) for a in sums]
    ins = list(sums)
    if small is not None:
        shapes.append(jax.ShapeDtypeStruct((N_CHIPS,) + small.shape, small.dtype))
        ins.append(small)
    return _Comm(ins, shapes, [_DMA((4 * n + 4,)), _DMA((4 * n + 4,)), _DMA(())], start, finish)


def pair_send_totals(arrs):
    n = len(arrs)

    def copies(cin, cout, sems):
        x, y, c, _ = _place()
        return [_remote(cin[i], cout[i], sems[0].at[i], sems[1].at[i], (x, y, 1 - c)) for i in range(n)]

    start = lambda cin, cout, sems: _start_all(copies(cin, cout, sems))
    finish = lambda cin, cout, sems: _wait_all(copies(cin, cout, sems))
    return _Comm(arrs, [jax.ShapeDtypeStruct(a.shape, a.dtype) for a in arrs], [_DMA((n,)), _DMA((n,))], start, finish)


def _plain_params(n_axes):
    return pltpu.CompilerParams(dimension_semantics=("arbitrary",) * n_axes, vmem_limit_bytes=V7X_VMEM_BYTES // 2)


def add_own_halves(names, parts, landed, core, name):
    n = len(names)

    def body(core_ref, *refs):
        for i in range(n):
            refs[2 * n + i][0] = (refs[i][...] + refs[n + i][0]).astype(MXU_DTYPE)

    in_specs, out_specs = [], []
    for nm in names:
        hr, hc = _HALF[nm]
        if nm in ("pa", "pb"):
            in_specs.append(pl.BlockSpec((hr // 2, hc), lambda k, s, cr: (cr[0] * 2 + s, k)))
        else:
            in_specs.append(pl.BlockSpec((hr // 2, hc), lambda k, s, cr: (k * 4 + cr[0] * 2 + s, 0)))
    for nm in names:
        hr, hc = _HALF[nm]
        in_specs.append(pl.BlockSpec((1, hr // 2, hc), lambda k, s, cr: (k, s, 0)))
        out_specs.append(pl.BlockSpec((1, hr // 2, hc), lambda k, s, cr: (k, s, 0)))
    return pl.pallas_call(
        body, name=name,
        grid_spec=pltpu.PrefetchScalarGridSpec(num_scalar_prefetch=1, grid=(N_CHIPS, 2), in_specs=in_specs, out_specs=out_specs),
        out_shape=[jax.ShapeDtypeStruct((N_CHIPS,) + _HALF[nm], MXU_DTYPE) for nm in names],
        compiler_params=_plain_params(2),
    )(core, *parts, *landed)


def add_small_half(small, landed, core):
    hs = SMALL_ROWS // 2

    def body(core_ref, a_ref, b_ref, o_ref):
        o_ref[...] = a_ref[...] + b_ref[...]

    return pl.pallas_call(
        body, name="add_small_half",
        grid_spec=pltpu.PrefetchScalarGridSpec(
            num_scalar_prefetch=1, grid=(1,),
            in_specs=[pl.BlockSpec((hs, 128), lambda s, cr: (cr[0], 0)), pl.BlockSpec((hs, 128), lambda s, cr: (0, 0))],
            out_specs=pl.BlockSpec((hs, 128), lambda s, cr: (0, 0))),
        out_shape=jax.ShapeDtypeStruct((hs, 128), F32),
    )(core, small, landed)


def sum_chips(sums, landed, chips, name, small_landed=None):
    n = len(sums)

    def body(chips_ref, *refs):
        ins, outs = refs[:4 * n + (small_landed is not None)], refs[4 * n + (small_landed is not None):]
        for i in range(n):
            tot = ins[4 * i][0].astype(F32)
            for r in range(1, N_CHIPS):
                tot = tot + ins[4 * i + r][0].astype(F32)
            outs[i][...] = tot
        if small_landed is not None:
            @pl.when(pl.program_id(0) == 0)
            def _():
                sm = ins[4 * n]
                outs[n][...] = ((sm[0] + sm[1]) + sm[2]) + sm[3]

    in_specs, out_specs, operands, shapes = [], [], [], []
    for a, l in zip(sums, landed):
        _, hr, hc = a.shape
        for r in range(N_CHIPS):
            in_specs.append(pl.BlockSpec((1, hr // 2, hc), functools.partial(lambda s, ch, r: (ch[r], s, 0), r=r)))
            operands.append(a if r == 0 else l)
        out_specs.append(pl.BlockSpec((hr // 2, hc), lambda s, ch: (s, 0)))
        shapes.append(jax.ShapeDtypeStruct((hr, hc), F32))
    if small_landed is not None:
        in_specs.append(pl.BlockSpec(small_landed.shape, lambda s, ch: (0, 0, 0)))
        out_specs.append(pl.BlockSpec(small_landed.shape[1:], lambda s, ch: (0, 0)))
        operands.append(small_landed)
        shapes.append(jax.ShapeDtypeStruct(small_landed.shape[1:], F32))
    return pl.pallas_call(
        body, name=name,
        grid_spec=pltpu.PrefetchScalarGridSpec(num_scalar_prefetch=1, grid=(2,), in_specs=in_specs, out_specs=out_specs),
        out_shape=shapes, compiler_params=_plain_params(1),
    )(chips, *operands)


N_STEPS = 8


def _rows2d(a):
    return a.reshape((-1, a.shape[-1]))


def adamw(ws, gs, ms, vs):
    n = len(ws)
    c1 = 1.0 / (1.0 - ADAM_B1 ** ADAM_STEP)
    c2 = 1.0 / (1.0 - ADAM_B2 ** ADAM_STEP)

    def body(*refs):
        for i in range(n):
            w, g, m, v = (refs[k * n + i][...] for k in range(4))
            m = ADAM_B1 * m + (1.0 - ADAM_B1) * g
            v = ADAM_B2 * v + (1.0 - ADAM_B2) * (g * g)
            refs[4 * n + i][...] = -ADAM_LR * ((m * c1) / (jnp.sqrt(v * c2) + ADAM_EPS) + ADAM_WD * w)
            refs[5 * n + i][...] = m
            refs[6 * n + i][...] = v

    spec = lambda a: pl.BlockSpec((a.shape[0] // N_STEPS, a.shape[1]), lambda s: (s, 0))
    outs = pl.pallas_call(
        body, name="adamw", grid=(N_STEPS,),
        in_specs=[spec(a) for a in ws] * 4, out_specs=[spec(a) for a in ws] * 3,
        out_shape=[jax.ShapeDtypeStruct(a.shape, F32) for a in ws] * 3,
        compiler_params=_plain_params(1),
    )(*ws, *gs, *ms, *vs)
    return outs[:n], outs[n:2 * n], outs[2 * n:]


_SMALL = (("ln_in_g", (D_MODEL,)), ("ln_in_b", (D_MODEL,)), ("b_in", (DEPTH, N_COLS)), ("sinks", (DEPTH, 8)),
          ("vn_g", (DEPTH, SGU_W)), ("vn_b", (DEPTH, SGU_W)), ("w_s", (DEPTH, SGU_G, BLK, BLK)),
          ("b_s", (DEPTH, SGU_G, BLK)), ("b_out", (DEPTH, D_MODEL)), ("ln_g", (DEPTH, D_MODEL)), ("ln_b", (DEPTH, D_MODEL)))
_BIG = ("w_in", "p_a", "p_b", "w_out")


def _pack_small(vals):
    flat = []
    for name, shape in _SMALL:
        v = vals[name].reshape((-1,))
        flat.append(jnp.pad(v, (0, (-v.shape[0]) % 128)))
    flat = jnp.concatenate(flat)
    return jnp.pad(flat, (0, SMALL_ROWS * 128 - flat.shape[0])).reshape((SMALL_ROWS, 128))


def _unpack_small(packed):
    flat = packed.reshape((-1,))
    out, pos = {}, 0
    for name, shape in _SMALL:
        size = 1
        for d in shape:
            size *= d
        out[name] = flat[pos:pos + size].reshape(shape)
        pos += size + (-size) % 128
    return out


def _own_start(name, j):
    hr, hc = _HALF[name]
    return (0, j * hc) if name in ("pa", "pb") else (j * 2 * hr, 0)


def kernel(x, ln_in_g, ln_in_b, w_in, b_in, sinks, vn_g, vn_b, w_s, b_s, p_a, p_b, w_out, b_out, ln_g, ln_b, loss_target, m_ln_in_g, m_ln_in_b, m_w_in, m_b_in, m_sinks, m_vn_g, m_vn_b, m_w_s, m_b_s, m_p_a, m_p_b, m_w_out, m_b_out, m_ln_g, m_ln_b, v_ln_in_g, v_ln_in_b, v_w_in, v_b_in, v_sinks, v_vn_g, v_vn_b, v_w_s, v_b_s, v_p_a, v_p_b, v_w_out, v_b_out, v_ln_g, v_ln_b):
    weights = dict(ln_in_g=ln_in_g, ln_in_b=ln_in_b, w_in=w_in, b_in=b_in, sinks=sinks, vn_g=vn_g, vn_b=vn_b, w_s=w_s,
                   b_s=b_s, p_a=p_a, p_b=p_b, w_out=w_out, b_out=b_out, ln_g=ln_g, ln_b=ln_b)
    mom1 = dict(ln_in_g=m_ln_in_g, ln_in_b=m_ln_in_b, w_in=m_w_in, b_in=m_b_in, sinks=m_sinks, vn_g=m_vn_g, vn_b=m_vn_b,
                w_s=m_w_s, b_s=m_b_s, p_a=m_p_a, p_b=m_p_b, w_out=m_w_out, b_out=m_b_out, ln_g=m_ln_g, ln_b=m_ln_b)
    mom2 = dict(ln_in_g=v_ln_in_g, ln_in_b=v_ln_in_b, w_in=v_w_in, b_in=v_b_in, sinks=v_sinks, vn_g=v_vn_g, vn_b=v_vn_b,
                w_s=v_w_s, b_s=v_b_s, p_a=v_p_a, p_b=v_p_b, w_out=v_w_out, b_out=v_b_out, ln_g=v_ln_g, ln_b=v_ln_b)
    n_seq, S, _ = x.shape
    T = n_seq * S
    c = lax.axis_index("c")
    j = 2 * lax.axis_index("x") + lax.axis_index("y")
    core = c.astype(jnp.int32).reshape((1,))
    chips = jnp.stack([j] + [r + (r >= j) for r in range(N_CHIPS - 1)]).astype(jnp.int32)
    names = list(_PIECES)
    xt, tt = x.reshape((T, D_MODEL)), loss_target.reshape((T, D_MODEL))

    def own_shards(l):
        return [w_in[l].T.astype(MXU_DTYPE), p_a[l].astype(MXU_DTYPE), p_b[l].astype(MXU_DTYPE), w_out[l].astype(MXU_DTYPE)]

    def layer_weights(l, gathered, own):
        whole = [lax.dynamic_update_slice(g, s, _own_start(nm, j)) for nm, g, s in zip(names, gathered, own)]
        d = dict(zip(_PIECES, whole))
        d.update(win=d["winT"].T, paT=d["pa"].T, pbT=d["pb"].T, woutT=d["wout"].T)
        d.update(bin=b_in[l][None], sinks=sinks[l], vng=vn_g[l][None], vnb=vn_b[l][None], ws=w_s[l],
                 bs=jnp.repeat(b_s[l].T, 128, axis=1), bout=b_out[l][None], lng=ln_g[l][None], lnb=ln_b[l][None])
        return d

    own0, own1 = own_shards(0), own_shards(1)
    x0, g0a = input_norm_forward(xt, ln_in_g[None], ln_in_b[None], "ln_in_fwd", comm=gather_over_chips(names, own0))
    lw0 = layer_weights(0, _run_comm(gather_over_pair(names, g0a), "gather0_pair"), own0)
    h0, g1a = project(x0, lw0["win"], lw0["bin"], "layer0_proj", comm=gather_over_chips(names, own1))
    x1, g1b = layer_forward(x0, h0, lw0, S, "layer0_fwd", comm=gather_over_pair(names, g1a))
    lw1 = layer_weights(1, g1b, own1)
    h1, _ = project(x1, lw1["win"], lw1["bin"], "layer1_proj")

    dres1, dh1, acc1, _ = layer_backward(x1, h1, tt, lw1, S, True, "layer1_bwd")
    dx1, dwinT1, _ = input_grads(dres1, dh1, x1, lw1["winT"], "layer1_dx_dwin")
    parts1 = [dwinT1, acc1["dpa"], acc1["dpb"], acc1["dwout"]]
    dres0, dh0, acc0, landed1 = layer_backward(x0, h0, dx1, lw0, S, False, "layer0_bwd", comm=pair_send_halves(names, parts1))
    sums1 = add_own_halves(names, parts1, landed1, core, "add_own_halves1")
    dx0, dwinT0, from_chips1 = input_grads(dres0, dh0, x0, lw0["winT"], "layer0_dx_dwin", comm=chip_exchange(sums1))
    totals1 = sum_chips(sums1, from_chips1, chips, "sum_chips1")
    grad_x, d_ln_in_g, d_ln_in_b, sib_totals1 = input_norm_backward(xt, dx0, ln_in_g, "ln_in_bwd", comm=pair_send_totals(totals1))
    loss = lax.psum(acc1["loss"][0, 0], ("x", "y", "c"))

    gl = [acc0, acc1]
    per_layer = lambda key, pick: jnp.stack([pick(gl[l][key]) for l in range(DEPTH)])
    row0 = lambda key: per_layer(key, lambda a: a[0])
    small_part = _pack_small(dict(
        ln_in_g=d_ln_in_g[0], ln_in_b=d_ln_in_b[0], b_in=row0("dbin"), sinks=per_layer("dsink", lambda a: a[:, 0]),
        vn_g=row0("dvng"), vn_b=row0("dvnb"), w_s=per_layer("dws", lambda a: a),
        b_s=per_layer("dbs", lambda a: a[:, ::128].T), b_out=row0("dbout"), ln_g=row0("dlng"), ln_b=row0("dlnb")))
    parts0 = [dwinT0, acc0["dpa"], acc0["dpb"], acc0["dwout"]]
    *landed0, small_landed = _run_comm(pair_send_halves(names, parts0, small_part), "pair_send_halves0")
    sums0 = add_own_halves(names, parts0, landed0, core, "add_own_halves0")
    small_sum = add_small_half(small_part, small_landed, core)
    *from_chips0, small_from_chips = _run_comm(chip_exchange(sums0, small_sum), "chip_exchange0")
    *totals0, small_total = sum_chips(sums0, from_chips0, chips, "sum_chips0", small_landed=small_from_chips)
    *sib_totals0, sib_small = _run_comm(pair_send_totals(totals0 + [small_total]), "pair_send_totals0")

    def both_halves(mine, theirs):
        return jnp.where(c == 0, jnp.concatenate([mine, theirs], axis=0), jnp.concatenate([theirs, mine], axis=0))

    g_big = {}
    for pi, pname in enumerate(_BIG):
        per = [both_halves(totals0[pi], sib_totals0[pi]), both_halves(totals1[pi], sib_totals1[pi])]
        g_big[pname] = jnp.stack([p.T for p in per] if pname == "w_in" else per)
    g_small = both_halves(small_total, sib_small)

    ws = [_rows2d(weights[n]) for n in _BIG] + [_pack_small(weights)]
    gs = [_rows2d(g_big[n]) for n in _BIG] + [g_small]
    ms = [_rows2d(mom1[n]) for n in _BIG] + [_pack_small(mom1)]
    vs = [_rows2d(mom2[n]) for n in _BIG] + [_pack_small(mom2)]
    deltas, new_ms, new_vs = adamw(ws, gs, ms, vs)

    def named(outs):
        d = _unpack_small(outs[-1])
        d.update({n: o.reshape(weights[n].shape) for n, o in zip(_BIG, outs[:-1])})
        return d

    grads = named([g_big[n] for n in _BIG] + [g_small])
    order = ("ln_in_g", "ln_in_b", "w_in", "b_in", "sinks", "vn_g", "vn_b", "w_s", "b_s", "p_a", "p_b", "w_out", "b_out",
             "ln_g", "ln_b")
    res = [loss, grad_x.reshape(x.shape)]
    for group in (grads, named(deltas), named(new_ms), named(new_vs)):
        res.extend(group[n] for n in order)
    return tuple(res)
```

```python
import functools

import jax
import jax.numpy as jnp
from jax import lax
from jax.experimental import pallas as pl
from jax.experimental.pallas import tpu as pltpu

F32 = jnp.float32
MXU_DTYPE = jnp.bfloat16

D_MODEL = 1024
DEPTH = 2
HEAD_DIM = 64
ATTN_W = 512
KV_W = 128
BLK = 128
SGU_W = 512
SGU_G = 4
N_COLS = 4864
C_Q, C_K, C_V, C_GA, C_UB, C_VB, C_GB, C_RA, C_RB = 0, 512, 640, 768, 1280, 1792, 2304, 2816, 3840
ALPHA = (2.0 * DEPTH) ** 0.25
LN_EPS = 1e-5
SCALE = HEAD_DIM ** -0.5
NEG = float(jnp.finfo(jnp.float32).min)
GELU_C = 0.7978845608028654
GELU_A = 0.044715

ADAM_LR, ADAM_B1, ADAM_B2, ADAM_EPS, ADAM_WD, ADAM_STEP = 0.001, 0.9, 0.999, 1e-08, 0.01, 10

V7X_VMEM_BYTES = 64 * 1024 * 1024
V7X_VMEM_RESERVE = 3 * 1024 * 1024
ROWS = 256
ROWS_BWD = 256
ROWS_PROJ = 512

MESH = pl.DeviceIdType.MESH
N_CHIPS = 4


def _mm(a, b):
    return jnp.dot(a.astype(MXU_DTYPE), b.astype(MXU_DTYPE), preferred_element_type=F32)


def _mm_nt(a, b):
    return lax.dot_general(a.astype(MXU_DTYPE), b.astype(MXU_DTYPE), (((1,), (1,)), ((), ())),
                           preferred_element_type=F32)


def _mm_tn(a, b):
    return lax.dot_general(a.astype(MXU_DTYPE), b.astype(MXU_DTYPE), (((0,), (0,)), ((), ())),
                           preferred_element_type=F32)


def _sigmoid(x):
    return 0.5 * jnp.tanh(0.5 * x) + 0.5


def _gelu_parts(x):
    x2 = x * x
    t = jnp.tanh(GELU_C * (x + GELU_A * x2 * x))
    g = 0.5 * x * (1.0 + t)
    dg = 0.5 * (1.0 + t) + 0.5 * x * (1.0 - t * t) * (GELU_C * (1.0 + 3.0 * GELU_A * x2))
    return g, dg


def _ln_fwd(z, g, b):
    mu = jnp.mean(z, axis=-1, keepdims=True)
    zc = z - mu
    var = jnp.mean(zc * zc, axis=-1, keepdims=True)
    rstd = lax.rsqrt(var + LN_EPS)
    zhat = zc * rstd
    return zhat * g + b, zhat, rstd


def _ln_bwd(dy, zhat, rstd, g):
    dzh = dy * g
    m1 = jnp.mean(dzh, axis=-1, keepdims=True)
    m2 = jnp.mean(dzh * zhat, axis=-1, keepdims=True)
    return rstd * (dzh - m1 - zhat * m2)


def _colsum(v):
    return jnp.sum(v, axis=0, keepdims=True)


def _lane_lo(rows):
    return lax.broadcasted_iota(jnp.int32, (rows, 128), 1) < HEAD_DIM


def _dup_head(x2, hk, lo):
    xr = pltpu.roll(x2, HEAD_DIM, 1)
    return jnp.where(lo, x2, xr) if hk == 0 else jnp.where(lo, xr, x2)


def _fold_head(y0, y1, lo):
    f0 = y0 + pltpu.roll(y0, HEAD_DIM, 1)
    f1 = y1 + pltpu.roll(y1, HEAD_DIM, 1)
    return jnp.where(lo, f0, f1)


def _stack_heads(t0, t1, lo):
    z = jnp.zeros_like(t0)
    return jnp.concatenate([jnp.where(lo, t0, z), jnp.where(lo, z, t0),
                            jnp.where(lo, t1, z), jnp.where(lo, z, t1)], axis=0)


def _unstack_heads(o, lo):
    return (jnp.where(lo, o[0:128], o[128:256]), jnp.where(lo, o[256:384], o[384:512]))


def _attn_probs(qs, kd, sink4, start):
    s = _mm_nt(qs, kd) * SCALE
    row = lax.broadcasted_iota(jnp.int32, (4 * BLK, 2 * BLK), 0) & (BLK - 1)
    kpos = lax.broadcasted_iota(jnp.int32, (4 * BLK, 2 * BLK), 1)
    valid = (kpos > row) & (kpos <= row + BLK)
    if start is not None:
        valid = valid & (jnp.logical_not(start) | (kpos >= BLK))
    s = jnp.where(valid, s, NEG)
    r1 = lax.broadcasted_iota(jnp.int32, (4 * BLK, 1), 0)
    sk = jnp.where(r1 < BLK, sink4[0], jnp.where(r1 < 2 * BLK, sink4[1], jnp.where(r1 < 3 * BLK, sink4[2], sink4[3])))
    m = jnp.maximum(jnp.max(s, axis=-1, keepdims=True), sk)
    p = jnp.exp(s - m)
    es = jnp.exp(sk - m)
    inv = 1.0 / (jnp.sum(p, axis=-1, keepdims=True) + es)
    return p * inv, es * inv


_PROJ_CHUNK = 512


def _kv_blocks(kvh, h_s, nb):
    ks = [kvh[:, 0:KV_W]] + [h_s[n * BLK:(n + 1) * BLK, C_K:C_K + KV_W] for n in range(nb)]
    vs = [kvh[:, KV_W:2 * KV_W]] + [h_s[n * BLK:(n + 1) * BLK, C_V:C_V + KV_W] for n in range(nb)]
    return ks, vs


def _attn_operands(h_s, ks, vs, n, hk, lo, lo2):
    rows = slice(n * BLK, (n + 1) * BLK)
    kd = _dup_head(jnp.concatenate([ks[n], ks[n + 1]], axis=0), hk, lo2)
    vd = _dup_head(jnp.concatenate([vs[n], vs[n + 1]], axis=0), hk, lo2)
    c0 = C_Q + 2 * hk * 128
    qs = _stack_heads(h_s[rows, c0:c0 + 128], h_s[rows, c0 + 128:c0 + 256], lo)
    return qs, kd, vd


def _sgu_weights(w):
    tri = (lax.broadcasted_iota(jnp.int32, (BLK, BLK), 0) >= lax.broadcasted_iota(jnp.int32, (BLK, BLK), 1))
    return tri, [jnp.where(tri, w["ws"][g], 0.0) for g in range(SGU_G)]


def _layer_forward(x, kvh, start, w, h_s, attn_s, mix_s, keep):
    R = x.shape[0]
    nb = R // BLK
    lo = _lane_lo(BLK)
    lo2 = _lane_lo(2 * BLK)
    ks, vs = _kv_blocks(kvh, h_s, nb)
    sinks = [w["sinks"][j] for j in range(8)]
    probs_l = []
    for n in range(nb):
        rows = slice(n * BLK, (n + 1) * BLK)
        for hk in range(2):
            qs, kd, vd = _attn_operands(h_s, ks, vs, n, hk, lo, lo2)
            probs, ps = _attn_probs(qs, kd, sinks[4 * hk:4 * hk + 4], start if n == 0 else None)
            t0, t1 = _unstack_heads(_mm(probs, vd), lo)
            attn_s[rows, 2 * hk * 128:(2 * hk + 1) * 128] = t0
            attn_s[rows, (2 * hk + 1) * 128:(2 * hk + 2) * 128] = t1
            if keep:
                probs_l.append((probs, ps))
    ga = h_s[:, C_GA:C_GA + ATTN_W]
    ya = attn_s[...] * (ga * _sigmoid(ga))
    gu, _ = _gelu_parts(h_s[:, C_UB:C_UB + SGU_W])
    gv, _ = _gelu_parts(h_s[:, C_VB:C_VB + SGU_W])
    vn, _, _ = _ln_fwd(gv, w["vng"][...], w["vnb"][...])
    _, wms = _sgu_weights(w)
    for n in range(nb):
        rows = slice(n * BLK, (n + 1) * BLK)
        for g in range(SGU_G):
            cols = slice(g * 128, (g + 1) * 128)
            mix_s[rows, cols] = _mm(wms[g], vn[rows, cols]) + w["bs"][:, cols]
    gb = h_s[:, C_GB:C_GB + SGU_W]
    yb = gu * mix_s[...] * (gb * _sigmoid(gb))
    a = _mm(ya, w["pa"][...])
    b = _mm(yb, w["pb"][...])
    merged = _sigmoid(h_s[:, C_RA:C_RA + D_MODEL]) * a + _sigmoid(h_s[:, C_RB:C_RB + D_MODEL]) * b
    out = _mm(merged, w["wout"][...]) + w["bout"][...]
    y, zhat, rstd = _ln_fwd(ALPHA * x + out, w["lng"][...], w["lnb"][...])
    if not keep:
        return y, None
    return y, dict(ks=ks, vs=vs, probs=probs_l, ya=ya, yb=yb, a=a, b=b, merged=merged, zhat=zhat, rstd=rstd)


def _dsilu(g, sg):
    return sg * (1.0 + g * (1.0 - sg))


_W_NAMES = ("sinks", "vng", "vnb", "ws", "bs", "pa", "pb", "wout", "bout", "lng", "lnb")
_WT_NAMES = ("paT", "pbT", "woutT")
_ACC_NAMES = ("dwout", "dpa", "dpb", "dbin", "dbout", "dlng", "dlnb", "dvng", "dvnb", "dws", "dbs", "dsink", "loss")


def _make_fwd_body(R, S):
    def body(x_ref, h_ref, kvh_ref, *rest):
        w = dict(zip(_W_NAMES, rest[:len(_W_NAMES)]))
        y_ref, attn_s, mix_s = rest[len(_W_NAMES):]
        start = (pl.program_id(0) % (S // R)) == 0
        y, _ = _layer_forward(x_ref[...], kvh_ref[...], start, w, h_ref, attn_s, mix_s, keep=False)
        y_ref[...] = y
    return body


def _make_bwd_body(R, S, nT, is_last):
    nb = R // BLK
    names = _W_NAMES + _WT_NAMES
    nw = len(names)

    def body(x_ref, h_s, kvh_ref, aux_ref, *rest):
        w = dict(zip(names, rest[:nw]))
        dres_ref, dh_ref = rest[nw:nw + 2]
        acc = dict(zip(_ACC_NAMES, rest[nw + 2:nw + 2 + len(_ACC_NAMES)]))
        attn_s, mix_s, ckv_s = rest[nw + 2 + len(_ACC_NAMES):]
        i = pl.program_id(0)
        start = ((nT - 1 - i) % (S // R)) == 0
        lo = _lane_lo(BLK)
        lo2 = _lane_lo(2 * BLK)

        @pl.when(i == 0)
        def _():
            for name in _ACC_NAMES:
                acc[name][...] = jnp.zeros(acc[name].shape, F32)
            ckv_s[...] = jnp.zeros(ckv_s.shape, F32)

        x = x_ref[...]
        y, sv = _layer_forward(x, kvh_ref[...], start, w, h_s, attn_s, mix_s, keep=True)
        if is_last:
            diff = y - aux_ref[...]
            part = 0.5 * jnp.sum(jnp.mean(diff * diff, axis=-1, keepdims=True), axis=0, keepdims=True)
            acc["loss"][...] += jnp.broadcast_to(part, acc["loss"].shape)
            dy = diff * (1.0 / D_MODEL)
        else:
            dy = aux_ref[...]

        def put(c0, val, rows=slice(None)):
            width = val.shape[1]
            dh_ref[rows, c0:c0 + width] = val.astype(dh_ref.dtype)
            acc["dbin"][:, c0:c0 + width] += _colsum(val)

        acc["dlng"][...] += _colsum(dy * sv["zhat"])
        acc["dlnb"][...] += _colsum(dy)
        dz = _ln_bwd(dy, sv["zhat"], sv["rstd"], w["lng"][...])
        dres_ref[...] = ALPHA * dz
        acc["dbout"][...] += _colsum(dz)
        acc["dwout"][...] += _mm_tn(sv["merged"], dz)
        dmerged = _mm(dz, w["woutT"][...])
        sa = _sigmoid(h_s[:, C_RA:C_RA + D_MODEL])
        da = dmerged * sa
        put(C_RA, da * sv["a"] * (1.0 - sa))
        sb = _sigmoid(h_s[:, C_RB:C_RB + D_MODEL])
        db = dmerged * sb
        put(C_RB, db * sv["b"] * (1.0 - sb))
        acc["dpa"][...] += _mm_tn(sv["ya"], da)
        acc["dpb"][...] += _mm_tn(sv["yb"], db)
        dya = _mm(da, w["paT"][...])
        dyb = _mm(db, w["pbT"][...])
        ga = h_s[:, C_GA:C_GA + ATTN_W]
        sga = _sigmoid(ga)
        put(C_GA, dya * attn_s[...] * _dsilu(ga, sga))
        attn_s[...] = dya * (ga * sga)
        ks, vs = sv["ks"], sv["vs"]
        dks = [jnp.zeros((BLK, KV_W), F32) for _ in range(nb + 1)]
        dvs = [jnp.zeros((BLK, KV_W), F32) for _ in range(nb + 1)]
        for n in range(nb):
            rows = slice(n * BLK, (n + 1) * BLK)
            ydk, ydv = [], []
            for hk in range(2):
                qs, kd, vd = _attn_operands(h_s, ks, vs, n, hk, lo, lo2)
                probs, ps = sv["probs"][2 * n + hk]
                dos = _stack_heads(attn_s[rows, 2 * hk * 128:(2 * hk + 1) * 128],
                                   attn_s[rows, (2 * hk + 1) * 128:(2 * hk + 2) * 128], lo)
                dp = _mm_nt(dos, vd)
                delta = jnp.sum(probs * dp, axis=-1, keepdims=True)
                dsk = -(ps * delta)
                for g in range(4):
                    j = 4 * hk + g
                    acc["dsink"][j:j + 1, :] += jnp.broadcast_to(_colsum(dsk[g * BLK:(g + 1) * BLK]), (1, 128))
                dss = probs * (dp - delta) * SCALE
                q0, q1 = _unstack_heads(_mm(dss, kd), lo)
                put(C_Q + 2 * hk * 128, q0, rows)
                put(C_Q + (2 * hk + 1) * 128, q1, rows)
                ydk.append(_mm_tn(dss, qs))
                ydv.append(_mm_tn(probs, dos))
            dk2 = _fold_head(ydk[0], ydk[1], lo2)
            dv2 = _fold_head(ydv[0], ydv[1], lo2)
            dks[n] = dks[n] + dk2[0:BLK]
            dks[n + 1] = dks[n + 1] + dk2[BLK:2 * BLK]
            dvs[n] = dvs[n] + dv2[0:BLK]
            dvs[n + 1] = dvs[n + 1] + dv2[BLK:2 * BLK]
        dks[nb] = dks[nb] + ckv_s[:, 0:KV_W]
        dvs[nb] = dvs[nb] + ckv_s[:, KV_W:2 * KV_W]
        ckv_s[:, 0:KV_W] = dks[0]
        ckv_s[:, KV_W:2 * KV_W] = dvs[0]
        put(C_K, jnp.concatenate(dks[1:], axis=0))
        put(C_V, jnp.concatenate(dvs[1:], axis=0))
        gb = h_s[:, C_GB:C_GB + SGU_W]
        sgb = _sigmoid(gb)
        gu, dgu = _gelu_parts(h_s[:, C_UB:C_UB + SGU_W])
        mixed = mix_s[...]
        put(C_GB, dyb * (gu * mixed) * _dsilu(gb, sgb))
        dsgu = dyb * (gb * sgb)
        put(C_UB, dsgu * mixed * dgu)
        dmixed = dsgu * gu
        gv, dgv = _gelu_parts(h_s[:, C_VB:C_VB + SGU_W])
        vn, vhat, vrstd = _ln_fwd(gv, w["vng"][...], w["vnb"][...])
        tri, wms = _sgu_weights(w)
        dbs_part = jnp.zeros((BLK, SGU_W), F32)
        for n in range(nb):
            rows = slice(n * BLK, (n + 1) * BLK)
            dbs_part = dbs_part + dmixed[rows]
            for g in range(SGU_G):
                cols = slice(g * 128, (g + 1) * 128)
                acc["dws"][g] += jnp.where(tri, _mm_nt(dmixed[rows, cols], vn[rows, cols]), 0.0)
                mix_s[rows, cols] = _mm_tn(wms[g], dmixed[rows, cols])
        acc["dbs"][...] += dbs_part
        dvn = mix_s[...]
        acc["dvng"][...] += _colsum(dvn * vhat)
        acc["dvnb"][...] += _colsum(dvn)
        put(C_VB, _ln_bwd(dvn, vhat, vrstd, w["vng"][...]) * dgv)

        @pl.when(i == nT - 1)
        def _():
            for g in range(SGU_G):
                cols = slice(g * 128, (g + 1) * 128)
                tot = jnp.sum(acc["dbs"][:, cols], axis=1, keepdims=True)
                acc["dbs"][:, cols] = jnp.broadcast_to(tot, (BLK, 128))

    return body


def _resident():
    return pl.BlockSpec(memory_space=pltpu.VMEM)


def _weight_specs(names):
    return [pl.BlockSpec(memory_space=pltpu.SMEM) if n == "sinks" else _resident() for n in names]


def _nbytes(a):
    n = jnp.dtype(a.dtype).itemsize
    for d in a.shape:
        n *= d
    return n


def _layer_params(resident=()):
    scoped = V7X_VMEM_BYTES - V7X_VMEM_RESERVE - sum(_nbytes(a) for a in resident)
    return pltpu.CompilerParams(dimension_semantics=("arbitrary",), vmem_limit_bytes=scoped)


class _Comm:
    def __init__(self, ins, out_shapes, sems, start, finish, aliases=None):
        self.ins = [pltpu.with_memory_space_constraint(a, pltpu.HBM) for a in ins]
        self.out_shapes = [pltpu.HBM(s.shape, s.dtype) for s in out_shapes]
        self.sems = list(sems)
        self.start, self.finish, self.aliases = start, finish, dict(aliases or {})


_ANY = pl.BlockSpec(memory_space=pltpu.HBM)
_DMA = pltpu.SemaphoreType.DMA


def _call(body, *, name, grid, in_specs, out_specs, out_shape, scratch_shapes, compiler_params, operands, comm=None):
    if comm is None:
        res = pl.pallas_call(body, name=name, grid=grid, in_specs=in_specs, out_specs=out_specs, out_shape=out_shape,
                             scratch_shapes=scratch_shapes, compiler_params=compiler_params)(*operands)
        return list(res), []
    n_in, n_out, n_scr = len(in_specs), len(out_specs), len(scratch_shapes)
    ci, co = len(comm.ins), len(comm.out_shapes)
    last = grid[0] - 1

    def hosted(*refs):
        ins, cin = refs[:n_in], refs[n_in:n_in + ci]
        p = n_in + ci
        outs, cout = refs[p:p + n_out], refs[p + n_out:p + n_out + co]
        p += n_out + co
        scr, sems = refs[p:p + n_scr], refs[p + n_scr:]
        pl.when(pl.program_id(0) == 0)(lambda: comm.start(cin, cout, sems))
        body(*ins, *outs, *scr)
        pl.when(pl.program_id(0) == last)(lambda: comm.finish(cin, cout, sems))

    res = pl.pallas_call(
        hosted, name=name, grid=grid, in_specs=list(in_specs) + [_ANY] * ci, out_specs=list(out_specs) + [_ANY] * co,
        out_shape=list(out_shape) + comm.out_shapes, scratch_shapes=list(scratch_shapes) + comm.sems,
        input_output_aliases={n_in + a: n_out + b for a, b in comm.aliases.items()},
        compiler_params=compiler_params)(*operands, *comm.ins)
    return list(res[:n_out]), list(res[n_out:])


def _run_comm(comm, name):
    ci, co = len(comm.ins), len(comm.out_shapes)

    def body(*refs):
        cin, cout, sems = refs[:ci], refs[ci:ci + co], refs[ci + co:]
        comm.start(cin, cout, sems)
        comm.finish(cin, cout, sems)

    return list(pl.pallas_call(body, name=name, in_specs=[_ANY] * ci, out_specs=[_ANY] * co, out_shape=comm.out_shapes,
                               scratch_shapes=comm.sems, input_output_aliases=comm.aliases)(*comm.ins))


def project(x, winT, b, name, comm=None):
    T = x.shape[0]
    R = ROWS_PROJ

    def body(x_ref, w_ref, b_ref, h_ref):
        xb = x_ref[...].astype(MXU_DTYPE)
        for c0 in range(0, N_COLS, _PROJ_CHUNK):
            c1 = min(c0 + _PROJ_CHUNK, N_COLS)
            h_ref[:, c0:c1] = _mm_nt(xb, w_ref[c0:c1, :]) + b_ref[:, c0:c1]

    (h,), extra = _call(
        body, name=name, grid=(T // R,),
        in_specs=[pl.BlockSpec((R, D_MODEL), lambda t: (t, 0)), _resident(), _resident()],
        out_specs=[pl.BlockSpec((R, N_COLS), lambda t: (t, 0))],
        out_shape=[jax.ShapeDtypeStruct((T, N_COLS), F32)], scratch_shapes=[],
        compiler_params=_layer_params([winT, b]), operands=(x, winT, b), comm=comm)
    return h, extra


_KV_BLOCK = C_K // (2 * KV_W)


def layer_forward(x, h, wts, S, name, comm=None):
    T = x.shape[0]
    R = ROWS
    npt = R // BLK
    (y,), extra = _call(
        _make_fwd_body(R, S), name=name, grid=(T // R,),
        in_specs=[pl.BlockSpec((R, D_MODEL), lambda t: (t, 0)), pl.BlockSpec((R, N_COLS), lambda t: (t, 0)),
                  pl.BlockSpec((BLK, 2 * KV_W), lambda t: (jnp.maximum(t * npt - 1, 0), _KV_BLOCK))]
        + _weight_specs(_W_NAMES),
        out_specs=[pl.BlockSpec((R, D_MODEL), lambda t: (t, 0))],
        out_shape=[jax.ShapeDtypeStruct((T, D_MODEL), F32)],
        scratch_shapes=[pltpu.VMEM((R, ATTN_W), F32), pltpu.VMEM((R, SGU_W), F32)],
        compiler_params=_layer_params([wts[n] for n in _W_NAMES]),
        operands=(x, h, h, *[wts[n] for n in _W_NAMES]), comm=comm)
    return y, extra


_ACC_SHAPES = dict(dwout=(D_MODEL, D_MODEL), dpa=(ATTN_W, D_MODEL), dpb=(SGU_W, D_MODEL), dbin=(1, N_COLS),
                   dbout=(1, D_MODEL), dlng=(1, D_MODEL), dlnb=(1, D_MODEL), dvng=(1, SGU_W), dvnb=(1, SGU_W),
                   dws=(SGU_G, BLK, BLK), dbs=(BLK, SGU_W), dsink=(8, 128), loss=(8, 128))


def layer_backward(x, h, aux, wts, S, is_last, name, comm=None):
    T = x.shape[0]
    R = ROWS_BWD
    nT = T // R
    npt = R // BLK
    rev = lambda i: (nT - 1 - i, 0)
    halo = lambda i: (jnp.maximum((nT - 1 - i) * npt - 1, 0), _KV_BLOCK)
    names = _W_NAMES + _WT_NAMES
    out_shape = ([jax.ShapeDtypeStruct((T, D_MODEL), F32), jax.ShapeDtypeStruct((T, N_COLS), MXU_DTYPE)]
                 + [jax.ShapeDtypeStruct(_ACC_SHAPES[n], F32) for n in _ACC_NAMES])
    outs, extra = _call(
        _make_bwd_body(R, S, nT, is_last), name=name, grid=(nT,),
        in_specs=[pl.BlockSpec((R, D_MODEL), rev), pl.BlockSpec((R, N_COLS), rev), pl.BlockSpec((BLK, 2 * KV_W), halo),
                  pl.BlockSpec((R, D_MODEL), rev)] + _weight_specs(names),
        out_specs=[pl.BlockSpec((R, D_MODEL), rev), pl.BlockSpec((R, N_COLS), rev)] + [_resident() for _ in _ACC_NAMES],
        out_shape=out_shape,
        scratch_shapes=[pltpu.VMEM((R, ATTN_W), F32), pltpu.VMEM((R, SGU_W), F32), pltpu.VMEM((BLK, 2 * KV_W), F32)],
        compiler_params=_layer_params([wts[n] for n in names] + out_shape[2:]),
        operands=(x, h, h, aux, *[wts[n] for n in names]), comm=comm)
    return outs[0], outs[1], dict(zip(_ACC_NAMES, outs[2:])), extra


def input_grads(dres, dh, x, winT, name, comm=None):
    T = x.shape[0]
    R = ROWS

    def body(dres_ref, dh_ref, x_ref, winT_ref, dx_ref, dwin_ref):
        @pl.when(pl.program_id(0) == 0)
        def _():
            dwin_ref[...] = jnp.zeros(dwin_ref.shape, F32)

        dh = dh_ref[...]
        dx_ref[...] = dres_ref[...] + jnp.dot(dh, winT_ref[...], preferred_element_type=F32)
        dwin_ref[...] += _mm_tn(dh, x_ref[...])

    row = lambda t: (t, 0)
    dwin_shape = jax.ShapeDtypeStruct((N_COLS, D_MODEL), F32)
    (dx, dwinT), extra = _call(
        body, name=name, grid=(T // R,),
        in_specs=[pl.BlockSpec((R, D_MODEL), row), pl.BlockSpec((R, N_COLS), row), pl.BlockSpec((R, D_MODEL), row),
                  _resident()],
        out_specs=[pl.BlockSpec((R, D_MODEL), row), _resident()],
        out_shape=[jax.ShapeDtypeStruct((T, D_MODEL), F32), dwin_shape], scratch_shapes=[],
        compiler_params=_layer_params([winT, dwin_shape]), operands=(dres, dh, x, winT), comm=comm)
    return dx, dwinT, extra


def input_norm_forward(x, g, b, name, comm=None):
    T = x.shape[0]
    R = ROWS

    def body(x_ref, g_ref, b_ref, y_ref):
        y_ref[...] = _ln_fwd(x_ref[...], g_ref[...], b_ref[...])[0]

    row = lambda t: (t, 0)
    (y,), extra = _call(
        body, name=name, grid=(T // R,),
        in_specs=[pl.BlockSpec((R, D_MODEL), row), _resident(), _resident()],
        out_specs=[pl.BlockSpec((R, D_MODEL), row)],
        out_shape=[jax.ShapeDtypeStruct((T, D_MODEL), F32)], scratch_shapes=[],
        compiler_params=_layer_params(), operands=(x, g, b), comm=comm)
    return y, extra


def input_norm_backward(x, dy, g, name, comm=None):
    T = x.shape[0]
    R = ROWS

    def body(x_ref, dy_ref, g_ref, dx_ref, dg_ref, db_ref):
        @pl.when(pl.program_id(0) == 0)
        def _():
            dg_ref[...] = jnp.zeros(dg_ref.shape, F32)
            db_ref[...] = jnp.zeros(db_ref.shape, F32)

        dy = dy_ref[...]
        _, xhat, rstd = _ln_fwd(x_ref[...], g_ref[...], g_ref[...])
        dx_ref[...] = _ln_bwd(dy, xhat, rstd, g_ref[...])
        dg_ref[...] += _colsum(dy * xhat)
        db_ref[...] += _colsum(dy)

    row = lambda t: (t, 0)
    vec = jax.ShapeDtypeStruct((1, D_MODEL), F32)
    (dx, dg, db), extra = _call(
        body, name=name, grid=(T // R,),
        in_specs=[pl.BlockSpec((R, D_MODEL), row), pl.BlockSpec((R, D_MODEL), row), _resident()],
        out_specs=[pl.BlockSpec((R, D_MODEL), row), _resident(), _resident()],
        out_shape=[jax.ShapeDtypeStruct((T, D_MODEL), F32), vec, vec], scratch_shapes=[],
        compiler_params=_layer_params(), operands=(x, dy, g), comm=comm)
    return dx, dg, db, extra


_PIECES = ("winT", "pa", "pb", "wout")
_WHOLE = dict(winT=(N_COLS, D_MODEL), pa=(ATTN_W, D_MODEL), pb=(SGU_W, D_MODEL), wout=(D_MODEL, D_MODEL))
_HALF = dict(winT=(N_COLS // 8, D_MODEL), pa=(ATTN_W // 2, D_MODEL // 4), pb=(SGU_W // 2, D_MODEL // 4),
             wout=(D_MODEL // 8, D_MODEL))
SMALL_ROWS = 1280


def _region(name, ref, k, h):
    hr, hc = _HALF[name]
    if name in ("pa", "pb"):
        return ref.at[pl.ds(h * hr, hr), pl.ds(k * hc, hc)]
    return ref.at[pl.ds(k * 2 * hr + h * hr, hr), :]


def _place():
    x, y, c = lax.axis_index("x"), lax.axis_index("y"), lax.axis_index("c")
    return x, y, c, 2 * x + y


def _as_chip(j, fn):
    for jj in range(N_CHIPS):
        pl.when(j == jj)(functools.partial(fn, jj))


def _remote(src, dst, send_sem, recv_sem, to):
    return pltpu.make_async_remote_copy(src_ref=src, dst_ref=dst, send_sem=send_sem, recv_sem=recv_sem,
                                        device_id=to, device_id_type=MESH)


def _core_of(k, c):
    return (k // 2, k % 2, c)


def _others(jj):
    return [k for k in range(N_CHIPS) if k != jj]


def _start_all(cps):
    for cp in cps:
        cp.start()


def _wait_all(cps):
    for cp in cps:
        cp.wait()


def gather_over_chips(names, shards):
    n = len(names)

    def copies(cin, cout, sems, jj, c):
        out = []
        for i, name in enumerate(names):
            hr = _HALF[name][0]
            for k in _others(jj):
                out.append(_remote(cin[i].at[pl.ds(c * hr, hr)], _region(name, cout[i], jj, c),
                                   sems[0].at[4 * i + k], sems[1].at[4 * i + jj], _core_of(k, c)))
        return out

    def start(cin, cout, sems):
        _, _, c, j = _place()
        _as_chip(j, lambda jj: _start_all(copies(cin, cout, sems, jj, c)))

    def finish(cin, cout, sems):
        _, _, c, j = _place()

        def run(jj):
            for i, name in enumerate(names):
                for k in _others(jj):
                    land = _region(name, cout[i], k, c)
                    _remote(land, land, sems[0].at[4 * i + k], sems[1].at[4 * i + k], _core_of(k, c)).wait_recv()
            for cp in copies(cin, cout, sems, jj, c):
                cp.wait_send()

        _as_chip(j, run)

    shapes = [jax.ShapeDtypeStruct(_WHOLE[nm], s.dtype) for nm, s in zip(names, shards)]
    return _Comm(shards, shapes, [_DMA((4 * n,)), _DMA((4 * n,))], start, finish)


def gather_over_pair(names, wholes):
    n = len(names)

    def start(cin, cout, sems):
        x, y, c, j = _place()

        def run(jj):
            for i, name in enumerate(names):
                for k in _others(jj):
                    land = _region(name, cout[i], k, c)
                    _remote(land, land, sems[0].at[4 * i + k], sems[1].at[4 * i + k], (x, y, 1 - c)).start()

        _as_chip(j, run)

    def finish(cin, cout, sems):
        x, y, c, j = _place()

        def run(jj):
            for i, name in enumerate(names):
                for k in _others(jj):
                    theirs = _region(name, cout[i], k, 1 - c)
                    _remote(theirs, theirs, sems[0].at[4 * i + k], sems[1].at[4 * i + k], (x, y, 1 - c)).wait_recv()
            for i, name in enumerate(names):
                for k in _others(jj):
                    land = _region(name, cout[i], k, c)
                    _remote(land, land, sems[0].at[4 * i + k], sems[1].at[4 * i + k], (x, y, 1 - c)).wait_send()

        _as_chip(j, run)

    shapes = [jax.ShapeDtypeStruct(a.shape, a.dtype) for a in wholes]
    return _Comm(wholes, shapes, [_DMA((4 * n,)), _DMA((4 * n,))], start, finish, aliases={i: i for i in range(n)})


def pair_send_halves(names, parts, small=None, extra=()):
    n = len(names)
    hs = SMALL_ROWS // 2
    first_extra = n + (small is not None)

    def copies(cin, cout, sems):
        x, y, c, _ = _place()
        sib = (x, y, 1 - c)
        cps = []
        for i, name in enumerate(names):
            for k in range(N_CHIPS):
                cps.append(_remote(_region(name, cin[i], k, 1 - c), cout[i].at[k], sems[0].at[4 * i + k],
                                   sems[1].at[4 * i + k], sib))
        if small is not None:
            cps.append(_remote(cin[n].at[pl.ds((1 - c) * hs, hs)], cout[n], sems[0].at[4 * n], sems[1].at[4 * n], sib))
        for e in range(len(extra)):
            cps.append(_remote(cin[first_extra + e], cout[first_extra + e], sems[0].at[4 * n + 1 + e],
                               sems[1].at[4 * n + 1 + e], sib))
        return cps

    start = lambda cin, cout, sems: _start_all(copies(cin, cout, sems))
    finish = lambda cin, cout, sems: _wait_all(copies(cin, cout, sems))
    shapes = [jax.ShapeDtypeStruct((N_CHIPS,) + _HALF[nm], F32) for nm in names]
    ins = list(parts)
    if small is not None:
        shapes.append(jax.ShapeDtypeStruct((hs, 128), F32))
        ins.append(small)
    shapes += [jax.ShapeDtypeStruct(a.shape, a.dtype) for a in extra]
    ins += list(extra)
    n_sem = 4 * n + 1 + len(extra)
    return _Comm(ins, shapes, [_DMA((n_sem,)), _DMA((n_sem,))], start, finish)


def chip_exchange(sums, small=None):
    n = len(sums)

    def copies(cin, cout, sems, jj, c):
        out = []
        for k in _others(jj):
            for i in range(n):
                out.append(_remote(cin[i].at[k], cout[i].at[jj], sems[0].at[4 * i + k], sems[1].at[4 * i + jj], _core_of(k, c)))
            if small is not None:
                out.append(_remote(cin[n], cout[n].at[jj], sems[0].at[4 * n + k], sems[1].at[4 * n + jj], _core_of(k, c)))
        return out

    def start(cin, cout, sems):
        _, _, c, j = _place()

        def run(jj):
            if small is not None:
                pltpu.make_async_copy(cin[n], cout[n].at[jj], sems[2]).start()
            for cp in copies(cin, cout, sems, jj, c):
                cp.start()

        _as_chip(j, run)

    def finish(cin, cout, sems):
        _, _, c, j = _place()

        def run(jj):
            for k in _others(jj):
                for i in range(n):
                    _remote(cin[i].at[k], cout[i].at[k], sems[0].at[4 * i + k], sems[1].at[4 * i + k], _core_of(k, c)).wait_recv()
                if small is not None:
                    _remote(cin[n], cout[n].at[k], sems[0].at[4 * n + k], sems[1].at[4 * n + k], _core_of(k, c)).wait_recv()
            for cp in copies(cin, cout, sems, jj, c):
                cp.wait_send()
            if small is not None:
                pltpu.make_async_copy(cin[n], cout[n].at[jj], sems[2]).wait()

        _as_chip(j, run)

    shapes = [jax.ShapeDtypeStruct(a.shape, a.dtype) for a in sums]
    ins = list(sums)
    if small is not None:
        shapes.append(jax.ShapeDtypeStruct((N_CHIPS,) + small.shape, small.dtype))
        ins.append(small)
    return _Comm(ins, shapes, [_DMA((4 * n + 4,)), _DMA((4 * n + 4,)), _DMA(())], start, finish)


def pair_send_totals(arrs):
    n = len(arrs)

    def copies(cin, cout, sems):
        x, y, c, _ = _place()
        return [_remote(cin[i], cout[i], sems[0].at[i], sems[1].at[i], (x, y, 1 - c)) for i in range(n)]

    start = lambda cin, cout, sems: _start_all(copies(cin, cout, sems))
    finish = lambda cin, cout, sems: _wait_all(copies(cin, cout, sems))
    return _Comm(arrs, [jax.ShapeDtypeStruct(a.shape, a.dtype) for a in arrs], [_DMA((n,)), _DMA((n,))], start, finish)


def _plain_params(n_axes):
    return pltpu.CompilerParams(dimension_semantics=("arbitrary",) * n_axes, vmem_limit_bytes=V7X_VMEM_BYTES // 2)


def add_own_halves(names, parts, landed, core, name):
    n = len(names)

    def body(core_ref, *refs):
        for i in range(n):
            refs[2 * n + i][0] = (refs[i][...] + refs[n + i][0]).astype(MXU_DTYPE)

    in_specs, out_specs = [], []
    for nm in names:
        hr, hc = _HALF[nm]
        if nm in ("pa", "pb"):
            in_specs.append(pl.BlockSpec((hr // 2, hc), lambda k, s, cr: (cr[0] * 2 + s, k)))
        else:
            in_specs.append(pl.BlockSpec((hr // 2, hc), lambda k, s, cr: (k * 4 + cr[0] * 2 + s, 0)))
    for nm in names:
        hr, hc = _HALF[nm]
        in_specs.append(pl.BlockSpec((1, hr // 2, hc), lambda k, s, cr: (k, s, 0)))
        out_specs.append(pl.BlockSpec((1, hr // 2, hc), lambda k, s, cr: (k, s, 0)))
    return pl.pallas_call(
        body, name=name,
        grid_spec=pltpu.PrefetchScalarGridSpec(num_scalar_prefetch=1, grid=(N_CHIPS, 2), in_specs=in_specs, out_specs=out_specs),
        out_shape=[jax.ShapeDtypeStruct((N_CHIPS,) + _HALF[nm], MXU_DTYPE) for nm in names],
        compiler_params=_plain_params(2),
    )(core, *parts, *landed)


def add_small_half(small, landed, core):
    hs = SMALL_ROWS // 2

    def body(core_ref, a_ref, b_ref, o_ref):
        o_ref[...] = a_ref[...] + b_ref[...]

    return pl.pallas_call(
        body, name="add_small_half",
        grid_spec=pltpu.PrefetchScalarGridSpec(
            num_scalar_prefetch=1, grid=(1,),
            in_specs=[pl.BlockSpec((hs, 128), lambda s, cr: (cr[0], 0)), pl.BlockSpec((hs, 128), lambda s, cr: (0, 0))],
            out_specs=pl.BlockSpec((hs, 128), lambda s, cr: (0, 0))),
        out_shape=jax.ShapeDtypeStruct((hs, 128), F32),
    )(core, small, landed)


def sum_chips(sums, landed, chips, name, small_landed=None):
    n = len(sums)

    def body(chips_ref, *refs):
        ins, outs = refs[:4 * n + (small_landed is not None)], refs[4 * n + (small_landed is not None):]
        for i in range(n):
            tot = ins[4 * i][0].astype(F32)
            for r in range(1, N_CHIPS):
                tot = tot + ins[4 * i + r][0].astype(F32)
            outs[i][...] = tot
        if small_landed is not None:
            @pl.when(pl.program_id(0) == 0)
            def _():
                sm = ins[4 * n]
                outs[n][...] = ((sm[0] + sm[1]) + sm[2]) + sm[3]

    in_specs, out_specs, operands, shapes = [], [], [], []
    for a, l in zip(sums, landed):
        _, hr, hc = a.shape
        for r in range(N_CHIPS):
            in_specs.append(pl.BlockSpec((1, hr // 2, hc), functools.partial(lambda s, ch, r: (ch[r], s, 0), r=r)))
            operands.append(a if r == 0 else l)
        out_specs.append(pl.BlockSpec((hr // 2, hc), lambda s, ch: (s, 0)))
        shapes.append(jax.ShapeDtypeStruct((hr, hc), F32))
    if small_landed is not None:
        in_specs.append(pl.BlockSpec(small_landed.shape, lambda s, ch: (0, 0, 0)))
        out_specs.append(pl.BlockSpec(small_landed.shape[1:], lambda s, ch: (0, 0)))
        operands.append(small_landed)
        shapes.append(jax.ShapeDtypeStruct(small_landed.shape[1:], F32))
    return pl.pallas_call(
        body, name=name,
        grid_spec=pltpu.PrefetchScalarGridSpec(num_scalar_prefetch=1, grid=(2,), in_specs=in_specs, out_specs=out_specs),
        out_shape=shapes, compiler_params=_plain_params(1),
    )(chips, *operands)


N_STEPS = 8


def adamw(ws, gs, ms, vs):
    n = len(ws)
    c1 = 1.0 / (1.0 - ADAM_B1 ** ADAM_STEP)
    c2 = 1.0 / (1.0 - ADAM_B2 ** ADAM_STEP)

    def body(*refs):
        for i in range(n):
            w, g, m, v = (refs[k * n + i][...] for k in range(4))
            m = ADAM_B1 * m + (1.0 - ADAM_B1) * g
            v = ADAM_B2 * v + (1.0 - ADAM_B2) * (g * g)
            refs[4 * n + i][...] = -ADAM_LR * ((m * c1) / (jnp.sqrt(v * c2) + ADAM_EPS) + ADAM_WD * w)
            refs[5 * n + i][...] = m
            refs[6 * n + i][...] = v

    per_layer = N_STEPS // DEPTH

    def spec(a):
        if a.ndim == 2:
            return pl.BlockSpec((a.shape[0] // N_STEPS, a.shape[1]), lambda s: (s, 0))
        return pl.BlockSpec((1, a.shape[1] // per_layer, a.shape[2]), lambda s: (s // per_layer, s % per_layer, 0))

    outs = pl.pallas_call(
        body, name="adamw", grid=(N_STEPS,),
        in_specs=[spec(a) for a in ws] * 4, out_specs=[spec(a) for a in ws] * 3,
        out_shape=[jax.ShapeDtypeStruct(a.shape, F32) for a in ws] * 3,
        compiler_params=_plain_params(1),
    )(*ws, *gs, *ms, *vs)
    return outs[:n], outs[n:2 * n], outs[2 * n:]


_SMALL = (("ln_in_g", (D_MODEL,)), ("ln_in_b", (D_MODEL,)), ("b_in", (DEPTH, N_COLS)), ("sinks", (DEPTH, 8)),
          ("vn_g", (DEPTH, SGU_W)), ("vn_b", (DEPTH, SGU_W)), ("w_s", (DEPTH, SGU_G, BLK, BLK)),
          ("b_s", (DEPTH, SGU_G, BLK)), ("b_out", (DEPTH, D_MODEL)), ("ln_g", (DEPTH, D_MODEL)), ("ln_b", (DEPTH, D_MODEL)))
_BIG = ("w_in", "p_a", "p_b", "w_out")


def _size(shape):
    n = 1
    for d in shape:
        n *= d
    return n


def _pack_small(vals, last_row=None):
    rows = []
    for name, shape in _SMALL:
        v = vals[name]
        if _size(shape) % 128:
            v = jnp.pad(v.reshape((1, -1)), ((0, 0), (0, (-_size(shape)) % 128)))
        rows.append(v.reshape((-1, 128)))
    used = sum(r.shape[0] for r in rows)
    tail = jnp.zeros((SMALL_ROWS - used, 128), F32)
    if last_row is not None:
        tail = tail.at[-1, 0].set(last_row)
    return jnp.concatenate(rows + [tail], axis=0)


def _unpack_small(packed):
    out, pos = {}, 0
    for name, shape in _SMALL:
        n = -(-_size(shape) // 128)
        rows = packed[pos:pos + n]
        out[name] = (rows.reshape((-1,))[:_size(shape)] if _size(shape) % 128 else rows).reshape(shape)
        pos += n
    return out


def _own_start(name, j):
    hr, hc = _HALF[name]
    return (0, j * hc) if name in ("pa", "pb") else (j * 2 * hr, 0)


def kernel(x, ln_in_g, ln_in_b, w_in, b_in, sinks, vn_g, vn_b, w_s, b_s, p_a, p_b, w_out, b_out, ln_g, ln_b, loss_target, m_ln_in_g, m_ln_in_b, m_w_in, m_b_in, m_sinks, m_vn_g, m_vn_b, m_w_s, m_b_s, m_p_a, m_p_b, m_w_out, m_b_out, m_ln_g, m_ln_b, v_ln_in_g, v_ln_in_b, v_w_in, v_b_in, v_sinks, v_vn_g, v_vn_b, v_w_s, v_b_s, v_p_a, v_p_b, v_w_out, v_b_out, v_ln_g, v_ln_b):
    weights = dict(ln_in_g=ln_in_g, ln_in_b=ln_in_b, w_in=w_in, b_in=b_in, sinks=sinks, vn_g=vn_g, vn_b=vn_b, w_s=w_s,
                   b_s=b_s, p_a=p_a, p_b=p_b, w_out=w_out, b_out=b_out, ln_g=ln_g, ln_b=ln_b)
    mom1 = dict(ln_in_g=m_ln_in_g, ln_in_b=m_ln_in_b, w_in=m_w_in, b_in=m_b_in, sinks=m_sinks, vn_g=m_vn_g, vn_b=m_vn_b,
                w_s=m_w_s, b_s=m_b_s, p_a=m_p_a, p_b=m_p_b, w_out=m_w_out, b_out=m_b_out, ln_g=m_ln_g, ln_b=m_ln_b)
    mom2 = dict(ln_in_g=v_ln_in_g, ln_in_b=v_ln_in_b, w_in=v_w_in, b_in=v_b_in, sinks=v_sinks, vn_g=v_vn_g, vn_b=v_vn_b,
                w_s=v_w_s, b_s=v_b_s, p_a=v_p_a, p_b=v_p_b, w_out=v_w_out, b_out=v_b_out, ln_g=v_ln_g, ln_b=v_ln_b)
    n_seq, S, _ = x.shape
    T = n_seq * S
    c = lax.axis_index("c")
    j = 2 * lax.axis_index("x") + lax.axis_index("y")
    core = c.astype(jnp.int32).reshape((1,))
    chips = jnp.stack([j] + [r + (r >= j) for r in range(N_CHIPS - 1)]).astype(jnp.int32)
    names = list(_PIECES)
    xt, tt = x.reshape((T, D_MODEL)), loss_target.reshape((T, D_MODEL))

    def own_shards(l):
        return [w_in[l].T.astype(MXU_DTYPE), p_a[l].astype(MXU_DTYPE), p_b[l].astype(MXU_DTYPE), w_out[l].astype(MXU_DTYPE)]

    def layer_weights(l, gathered, own):
        whole = [lax.dynamic_update_slice(g, s, _own_start(nm, j)) for nm, g, s in zip(names, gathered, own)]
        d = dict(zip(_PIECES, whole))
        d.update(paT=d["pa"].T, pbT=d["pb"].T, woutT=d["wout"].T)
        d.update(bin=b_in[l][None], sinks=sinks[l], vng=vn_g[l][None], vnb=vn_b[l][None], ws=w_s[l],
                 bs=jnp.repeat(b_s[l].T, 128, axis=1), bout=b_out[l][None], lng=ln_g[l][None], lnb=ln_b[l][None])
        return d

    own0, own1 = own_shards(0), own_shards(1)
    x0, g0a = input_norm_forward(xt, ln_in_g[None], ln_in_b[None], "ln_in_fwd", comm=gather_over_chips(names, own0))
    lw0 = layer_weights(0, _run_comm(gather_over_pair(names, g0a), "gather0_pair"), own0)
    h0, g1a = project(x0, lw0["winT"], lw0["bin"], "layer0_proj", comm=gather_over_chips(names, own1))
    x1, g1b = layer_forward(x0, h0, lw0, S, "layer0_fwd", comm=gather_over_pair(names, g1a))
    lw1 = layer_weights(1, g1b, own1)
    h1, _ = project(x1, lw1["winT"], lw1["bin"], "layer1_proj")

    dres1, dh1, acc1, _ = layer_backward(x1, h1, tt, lw1, S, True, "layer1_bwd")
    dx1, dwinT1, _ = input_grads(dres1, dh1, x1, lw1["winT"], "layer1_dx_dwin")
    parts1 = [dwinT1, acc1["dpa"], acc1["dpb"], acc1["dwout"]]
    dres0, dh0, acc0, landed1 = layer_backward(x0, h0, dx1, lw0, S, False, "layer0_bwd", comm=pair_send_halves(names, parts1))
    sums1 = add_own_halves(names, parts1, landed1, core, "add_own_halves1")
    dx0, dwinT0, from_chips1 = input_grads(dres0, dh0, x0, lw0["winT"], "layer0_dx_dwin", comm=chip_exchange(sums1))
    totals1 = sum_chips(sums1, from_chips1, chips, "sum_chips1")
    grad_x, d_ln_in_g, d_ln_in_b, _ = input_norm_backward(xt, dx0, ln_in_g, "ln_in_bwd")

    gl = [acc0, acc1]
    per_layer = lambda key, pick: jnp.stack([pick(gl[l][key]) for l in range(DEPTH)])
    row0 = lambda key: per_layer(key, lambda a: a[0])
    small_part = _pack_small(dict(
        ln_in_g=d_ln_in_g[0], ln_in_b=d_ln_in_b[0], b_in=row0("dbin"), sinks=per_layer("dsink", lambda a: a[:, 0]),
        vn_g=row0("dvng"), vn_b=row0("dvnb"), w_s=per_layer("dws", lambda a: a),
        b_s=per_layer("dbs", lambda a: a[:, ::128].T), b_out=row0("dbout"), ln_g=row0("dlng"), ln_b=row0("dlnb")),
        last_row=acc1["loss"][0, 0])
    parts0 = [dwinT0, acc0["dpa"], acc0["dpb"], acc0["dwout"]]
    first = _run_comm(pair_send_halves(names, parts0, small_part, extra=totals1), "pair_send_halves0")
    landed0, small_landed, sib_totals1 = first[:4], first[4], first[5:]
    sums0 = add_own_halves(names, parts0, landed0, core, "add_own_halves0")
    small_sum = add_small_half(small_part, small_landed, core)
    *from_chips0, small_from_chips = _run_comm(chip_exchange(sums0, small_sum), "chip_exchange0")
    *totals0, small_total = sum_chips(sums0, from_chips0, chips, "sum_chips0", small_landed=small_from_chips)
    *sib_totals0, sib_small = _run_comm(pair_send_totals(totals0 + [small_total]), "pair_send_totals0")

    def both_halves(mine, theirs):
        return jnp.where(c == 0, jnp.concatenate([mine, theirs], axis=0), jnp.concatenate([theirs, mine], axis=0))

    g_big = {}
    for pi, pname in enumerate(_BIG):
        per = [both_halves(totals0[pi], sib_totals0[pi]), both_halves(totals1[pi], sib_totals1[pi])]
        g_big[pname] = jnp.stack([p.T for p in per] if pname == "w_in" else per)
    g_small = both_halves(small_total, sib_small)
    loss = g_small[-1, 0]

    ws = [weights[n] for n in _BIG] + [_pack_small(weights)]
    gs = [g_big[n] for n in _BIG] + [g_small]
    ms = [mom1[n] for n in _BIG] + [_pack_small(mom1)]
    vs = [mom2[n] for n in _BIG] + [_pack_small(mom2)]
    deltas, new_ms, new_vs = adamw(ws, gs, ms, vs)

    def named(outs):
        d = _unpack_small(outs[-1])
        d.update(dict(zip(_BIG, outs[:-1])))
        return d

    grads = named([g_big[n] for n in _BIG] + [g_small])
    order = ("ln_in_g", "ln_in_b", "w_in", "b_in", "sinks", "vn_g", "vn_b", "w_s", "b_s", "p_a", "p_b", "w_out", "b_out",
             "ln_g", "ln_b")
    res = [loss, grad_x.reshape(x.shape)]
    for group in (grads, named(deltas), named(new_ms), named(new_vs)):
        res.extend(group[n] for n in order)
    return tuple(res)
```

```python
import functools

import jax
import jax.numpy as jnp
from jax import lax
from jax.experimental import pallas as pl
from jax.experimental.pallas import tpu as pltpu

F32 = jnp.float32
MXU_DTYPE = jnp.bfloat16

D_MODEL = 1024
DEPTH = 2
HEAD_DIM = 64
ATTN_W = 512
KV_W = 128
BLK = 128
SGU_W = 512
SGU_G = 4
N_COLS = 4864
C_Q, C_K, C_V, C_GA, C_UB, C_VB, C_GB, C_RA, C_RB = 0, 512, 640, 768, 1280, 1792, 2304, 2816, 3840
ALPHA = (2.0 * DEPTH) ** 0.25
LN_EPS = 1e-5
SCALE = HEAD_DIM ** -0.5
NEG = float(jnp.finfo(jnp.float32).min)
GELU_C = 0.7978845608028654
GELU_A = 0.044715

ADAM_LR, ADAM_B1, ADAM_B2, ADAM_EPS, ADAM_WD, ADAM_STEP = 0.001, 0.9, 0.999, 1e-08, 0.01, 10

V7X_VMEM_BYTES = 64 * 1024 * 1024
V7X_VMEM_RESERVE = 3 * 1024 * 1024
ROWS = 256
ROWS_BWD = 256
ROWS_PROJ = 512

MESH = pl.DeviceIdType.MESH
N_CHIPS = 4


def _mm(a, b):
    return jnp.dot(a.astype(MXU_DTYPE), b.astype(MXU_DTYPE), preferred_element_type=F32)


def _mm_nt(a, b):
    return lax.dot_general(a.astype(MXU_DTYPE), b.astype(MXU_DTYPE), (((1,), (1,)), ((), ())),
                           preferred_element_type=F32)


def _mm_tn(a, b):
    return lax.dot_general(a.astype(MXU_DTYPE), b.astype(MXU_DTYPE), (((0,), (0,)), ((), ())),
                           preferred_element_type=F32)


def _sigmoid(x):
    return 0.5 * jnp.tanh(0.5 * x) + 0.5


def _gelu_parts(x):
    x2 = x * x
    t = jnp.tanh(GELU_C * (x + GELU_A * x2 * x))
    g = 0.5 * x * (1.0 + t)
    dg = 0.5 * (1.0 + t) + 0.5 * x * (1.0 - t * t) * (GELU_C * (1.0 + 3.0 * GELU_A * x2))
    return g, dg


def _ln_fwd(z, g, b):
    mu = jnp.mean(z, axis=-1, keepdims=True)
    zc = z - mu
    var = jnp.mean(zc * zc, axis=-1, keepdims=True)
    rstd = lax.rsqrt(var + LN_EPS)
    zhat = zc * rstd
    return zhat * g + b, zhat, rstd


def _ln_bwd(dy, zhat, rstd, g):
    dzh = dy * g
    m1 = jnp.mean(dzh, axis=-1, keepdims=True)
    m2 = jnp.mean(dzh * zhat, axis=-1, keepdims=True)
    return rstd * (dzh - m1 - zhat * m2)


def _colsum(v):
    return jnp.sum(v, axis=0, keepdims=True)


def _lane_lo(rows):
    return lax.broadcasted_iota(jnp.int32, (rows, 128), 1) < HEAD_DIM


def _dup_head(x2, hk, lo):
    xr = pltpu.roll(x2, HEAD_DIM, 1)
    return jnp.where(lo, x2, xr) if hk == 0 else jnp.where(lo, xr, x2)


def _fold_head(y0, y1, lo):
    f0 = y0 + pltpu.roll(y0, HEAD_DIM, 1)
    f1 = y1 + pltpu.roll(y1, HEAD_DIM, 1)
    return jnp.where(lo, f0, f1)


def _stack_heads(t0, t1, lo):
    z = jnp.zeros_like(t0)
    return jnp.concatenate([jnp.where(lo, t0, z), jnp.where(lo, z, t0),
                            jnp.where(lo, t1, z), jnp.where(lo, z, t1)], axis=0)


def _unstack_heads(o, lo):
    return (jnp.where(lo, o[0:128], o[128:256]), jnp.where(lo, o[256:384], o[384:512]))


def _attn_probs(qs, kd, sink4, start):
    s = _mm_nt(qs, kd) * SCALE
    row = lax.broadcasted_iota(jnp.int32, (4 * BLK, 2 * BLK), 0) & (BLK - 1)
    kpos = lax.broadcasted_iota(jnp.int32, (4 * BLK, 2 * BLK), 1)
    valid = (kpos > row) & (kpos <= row + BLK)
    if start is not None:
        valid = valid & (jnp.logical_not(start) | (kpos >= BLK))
    s = jnp.where(valid, s, NEG)
    r1 = lax.broadcasted_iota(jnp.int32, (4 * BLK, 1), 0)
    sk = jnp.where(r1 < BLK, sink4[0], jnp.where(r1 < 2 * BLK, sink4[1], jnp.where(r1 < 3 * BLK, sink4[2], sink4[3])))
    m = jnp.maximum(jnp.max(s, axis=-1, keepdims=True), sk)
    p = jnp.exp(s - m)
    es = jnp.exp(sk - m)
    inv = 1.0 / (jnp.sum(p, axis=-1, keepdims=True) + es)
    return p * inv, es * inv


_PROJ_CHUNK = 512


def _kv_blocks(kvh, h_s, nb):
    ks = [kvh[:, 0:KV_W]] + [h_s[n * BLK:(n + 1) * BLK, C_K:C_K + KV_W] for n in range(nb)]
    vs = [kvh[:, KV_W:2 * KV_W]] + [h_s[n * BLK:(n + 1) * BLK, C_V:C_V + KV_W] for n in range(nb)]
    return ks, vs


def _attn_operands(h_s, ks, vs, n, hk, lo, lo2):
    rows = slice(n * BLK, (n + 1) * BLK)
    kd = _dup_head(jnp.concatenate([ks[n], ks[n + 1]], axis=0), hk, lo2)
    vd = _dup_head(jnp.concatenate([vs[n], vs[n + 1]], axis=0), hk, lo2)
    c0 = C_Q + 2 * hk * 128
    qs = _stack_heads(h_s[rows, c0:c0 + 128], h_s[rows, c0 + 128:c0 + 256], lo)
    return qs, kd, vd


def _sgu_weights(w):
    tri = (lax.broadcasted_iota(jnp.int32, (BLK, BLK), 0) >= lax.broadcasted_iota(jnp.int32, (BLK, BLK), 1))
    return tri, [jnp.where(tri, w["ws"][g], 0.0) for g in range(SGU_G)]


def _layer_forward(x, kvh, start, w, h_s, attn_s, mix_s, keep):
    R = x.shape[0]
    nb = R // BLK
    lo = _lane_lo(BLK)
    lo2 = _lane_lo(2 * BLK)
    ks, vs = _kv_blocks(kvh, h_s, nb)
    sinks = [w["sinks"][j] for j in range(8)]
    probs_l = []
    for n in range(nb):
        rows = slice(n * BLK, (n + 1) * BLK)
        for hk in range(2):
            qs, kd, vd = _attn_operands(h_s, ks, vs, n, hk, lo, lo2)
            probs, ps = _attn_probs(qs, kd, sinks[4 * hk:4 * hk + 4], start if n == 0 else None)
            t0, t1 = _unstack_heads(_mm(probs, vd), lo)
            attn_s[rows, 2 * hk * 128:(2 * hk + 1) * 128] = t0
            attn_s[rows, (2 * hk + 1) * 128:(2 * hk + 2) * 128] = t1
            if keep:
                probs_l.append((probs, ps))
    ga = h_s[:, C_GA:C_GA + ATTN_W]
    ya = attn_s[...] * (ga * _sigmoid(ga))
    gu, _ = _gelu_parts(h_s[:, C_UB:C_UB + SGU_W])
    gv, _ = _gelu_parts(h_s[:, C_VB:C_VB + SGU_W])
    vn, _, _ = _ln_fwd(gv, w["vng"][...], w["vnb"][...])
    _, wms = _sgu_weights(w)
    for n in range(nb):
        rows = slice(n * BLK, (n + 1) * BLK)
        for g in range(SGU_G):
            cols = slice(g * 128, (g + 1) * 128)
            mix_s[rows, cols] = _mm(wms[g], vn[rows, cols]) + w["bs"][:, cols]
    gb = h_s[:, C_GB:C_GB + SGU_W]
    yb = gu * mix_s[...] * (gb * _sigmoid(gb))
    a = _mm(ya, w["pa"][...])
    b = _mm(yb, w["pb"][...])
    merged = _sigmoid(h_s[:, C_RA:C_RA + D_MODEL]) * a + _sigmoid(h_s[:, C_RB:C_RB + D_MODEL]) * b
    out = _mm(merged, w["wout"][...]) + w["bout"][...]
    y, zhat, rstd = _ln_fwd(ALPHA * x + out, w["lng"][...], w["lnb"][...])
    if not keep:
        return y, None
    return y, dict(ks=ks, vs=vs, probs=probs_l, ya=ya, yb=yb, a=a, b=b, merged=merged, zhat=zhat, rstd=rstd)


def _dsilu(g, sg):
    return sg * (1.0 + g * (1.0 - sg))


_W_NAMES = ("sinks", "vng", "vnb", "ws", "bs", "pa", "pb", "wout", "bout", "lng", "lnb")
_WT_NAMES = ("paT", "pbT", "woutT")
_ACC_NAMES = ("dwout", "dpa", "dpb", "dbin", "dbout", "dlng", "dlnb", "dvng", "dvnb", "dws", "dbs", "dsink", "loss")


def _make_fwd_body(R, S):
    def body(x_ref, h_ref, kvh_ref, *rest):
        w = dict(zip(_W_NAMES, rest[:len(_W_NAMES)]))
        y_ref, attn_s, mix_s = rest[len(_W_NAMES):]
        start = (pl.program_id(0) % (S // R)) == 0
        y, _ = _layer_forward(x_ref[...], kvh_ref[...], start, w, h_ref, attn_s, mix_s, keep=False)
        y_ref[...] = y
    return body


def _make_bwd_body(R, S, nT, is_last):
    nb = R // BLK
    names = _W_NAMES + _WT_NAMES
    nw = len(names)

    def body(x_ref, h_s, kvh_ref, aux_ref, *rest):
        w = dict(zip(names, rest[:nw]))
        dres_ref, dh_ref = rest[nw:nw + 2]
        acc = dict(zip(_ACC_NAMES, rest[nw + 2:nw + 2 + len(_ACC_NAMES)]))
        attn_s, mix_s, ckv_s = rest[nw + 2 + len(_ACC_NAMES):]
        i = pl.program_id(0)
        start = ((nT - 1 - i) % (S // R)) == 0
        lo = _lane_lo(BLK)
        lo2 = _lane_lo(2 * BLK)

        @pl.when(i == 0)
        def _():
            for name in _ACC_NAMES:
                acc[name][...] = jnp.zeros(acc[name].shape, F32)
            ckv_s[...] = jnp.zeros(ckv_s.shape, F32)

        x = x_ref[...]
        y, sv = _layer_forward(x, kvh_ref[...], start, w, h_s, attn_s, mix_s, keep=True)
        if is_last:
            diff = y - aux_ref[...]
            part = 0.5 * jnp.sum(jnp.mean(diff * diff, axis=-1, keepdims=True), axis=0, keepdims=True)
            acc["loss"][...] += jnp.broadcast_to(part, acc["loss"].shape)
            dy = diff * (1.0 / D_MODEL)
        else:
            dy = aux_ref[...]

        def put(c0, val, rows=slice(None)):
            width = val.shape[1]
            dh_ref[rows, c0:c0 + width] = val.astype(dh_ref.dtype)
            acc["dbin"][:, c0:c0 + width] += _colsum(val)

        acc["dlng"][...] += _colsum(dy * sv["zhat"])
        acc["dlnb"][...] += _colsum(dy)
        dz = _ln_bwd(dy, sv["zhat"], sv["rstd"], w["lng"][...])
        dres_ref[...] = ALPHA * dz
        acc["dbout"][...] += _colsum(dz)
        acc["dwout"][...] += _mm_tn(sv["merged"], dz)
        dmerged = _mm(dz, w["woutT"][...])
        sa = _sigmoid(h_s[:, C_RA:C_RA + D_MODEL])
        da = dmerged * sa
        put(C_RA, da * sv["a"] * (1.0 - sa))
        sb = _sigmoid(h_s[:, C_RB:C_RB + D_MODEL])
        db = dmerged * sb
        put(C_RB, db * sv["b"] * (1.0 - sb))
        acc["dpa"][...] += _mm_tn(sv["ya"], da)
        acc["dpb"][...] += _mm_tn(sv["yb"], db)
        dya = _mm(da, w["paT"][...])
        dyb = _mm(db, w["pbT"][...])
        ga = h_s[:, C_GA:C_GA + ATTN_W]
        sga = _sigmoid(ga)
        put(C_GA, dya * attn_s[...] * _dsilu(ga, sga))
        attn_s[...] = dya * (ga * sga)
        ks, vs = sv["ks"], sv["vs"]
        dks = [jnp.zeros((BLK, KV_W), F32) for _ in range(nb + 1)]
        dvs = [jnp.zeros((BLK, KV_W), F32) for _ in range(nb + 1)]
        for n in range(nb):
            rows = slice(n * BLK, (n + 1) * BLK)
            ydk, ydv = [], []
            for hk in range(2):
                qs, kd, vd = _attn_operands(h_s, ks, vs, n, hk, lo, lo2)
                probs, ps = sv["probs"][2 * n + hk]
                dos = _stack_heads(attn_s[rows, 2 * hk * 128:(2 * hk + 1) * 128],
                                   attn_s[rows, (2 * hk + 1) * 128:(2 * hk + 2) * 128], lo)
                dp = _mm_nt(dos, vd)
                delta = jnp.sum(probs * dp, axis=-1, keepdims=True)
                dsk = -(ps * delta)
                for g in range(4):
                    j = 4 * hk + g
                    acc["dsink"][j:j + 1, :] += jnp.broadcast_to(_colsum(dsk[g * BLK:(g + 1) * BLK]), (1, 128))
                dss = probs * (dp - delta) * SCALE
                q0, q1 = _unstack_heads(_mm(dss, kd), lo)
                put(C_Q + 2 * hk * 128, q0, rows)
                put(C_Q + (2 * hk + 1) * 128, q1, rows)
                ydk.append(_mm_tn(dss, qs))
                ydv.append(_mm_tn(probs, dos))
            dk2 = _fold_head(ydk[0], ydk[1], lo2)
            dv2 = _fold_head(ydv[0], ydv[1], lo2)
            dks[n] = dks[n] + dk2[0:BLK]
            dks[n + 1] = dks[n + 1] + dk2[BLK:2 * BLK]
            dvs[n] = dvs[n] + dv2[0:BLK]
            dvs[n + 1] = dvs[n + 1] + dv2[BLK:2 * BLK]
        dks[nb] = dks[nb] + ckv_s[:, 0:KV_W]
        dvs[nb] = dvs[nb] + ckv_s[:, KV_W:2 * KV_W]
        ckv_s[:, 0:KV_W] = dks[0]
        ckv_s[:, KV_W:2 * KV_W] = dvs[0]
        put(C_K, jnp.concatenate(dks[1:], axis=0))
        put(C_V, jnp.concatenate(dvs[1:], axis=0))
        gb = h_s[:, C_GB:C_GB + SGU_W]
        sgb = _sigmoid(gb)
        gu, dgu = _gelu_parts(h_s[:, C_UB:C_UB + SGU_W])
        mixed = mix_s[...]
        put(C_GB, dyb * (gu * mixed) * _dsilu(gb, sgb))
        dsgu = dyb * (gb * sgb)
        put(C_UB, dsgu * mixed * dgu)
        dmixed = dsgu * gu
        gv, dgv = _gelu_parts(h_s[:, C_VB:C_VB + SGU_W])
        vn, vhat, vrstd = _ln_fwd(gv, w["vng"][...], w["vnb"][...])
        tri, wms = _sgu_weights(w)
        dbs_part = jnp.zeros((BLK, SGU_W), F32)
        for n in range(nb):
            rows = slice(n * BLK, (n + 1) * BLK)
            dbs_part = dbs_part + dmixed[rows]
            for g in range(SGU_G):
                cols = slice(g * 128, (g + 1) * 128)
                acc["dws"][g] += jnp.where(tri, _mm_nt(dmixed[rows, cols], vn[rows, cols]), 0.0)
                mix_s[rows, cols] = _mm_tn(wms[g], dmixed[rows, cols])
        acc["dbs"][...] += dbs_part
        dvn = mix_s[...]
        acc["dvng"][...] += _colsum(dvn * vhat)
        acc["dvnb"][...] += _colsum(dvn)
        put(C_VB, _ln_bwd(dvn, vhat, vrstd, w["vng"][...]) * dgv)

        @pl.when(i == nT - 1)
        def _():
            for g in range(SGU_G):
                cols = slice(g * 128, (g + 1) * 128)
                tot = jnp.sum(acc["dbs"][:, cols], axis=1, keepdims=True)
                acc["dbs"][:, cols] = jnp.broadcast_to(tot, (BLK, 128))

    return body


def _resident():
    return pl.BlockSpec(memory_space=pltpu.VMEM)


_BIG_RESIDENT = ("pa", "pb", "wout", "paT", "pbT", "woutT", "dwout", "dpa", "dpb")


def _const_block(shape):
    return pl.BlockSpec(tuple(shape), lambda *_: (0,) * len(shape))


def _operand_spec(name, shape):
    if name == "sinks":
        return pl.BlockSpec(memory_space=pltpu.SMEM)
    return _resident() if name in _BIG_RESIDENT else _const_block(shape)


def _weight_specs(names, wts):
    return [_operand_spec(n, wts[n].shape) for n in names]


def _nbytes(a):
    n = jnp.dtype(a.dtype).itemsize
    for d in a.shape:
        n *= d
    return n


def _layer_params(resident=()):
    scoped = V7X_VMEM_BYTES - V7X_VMEM_RESERVE - sum(_nbytes(a) for a in resident)
    return pltpu.CompilerParams(dimension_semantics=("arbitrary",), vmem_limit_bytes=scoped)


class _Comm:
    def __init__(self, ins, out_shapes, sems, start, finish, aliases=None):
        self.ins = [pltpu.with_memory_space_constraint(a, pltpu.HBM) for a in ins]
        self.out_shapes = [pltpu.HBM(s.shape, s.dtype) for s in out_shapes]
        self.sems = list(sems)
        self.start, self.finish, self.aliases = start, finish, dict(aliases or {})


_ANY = pl.BlockSpec(memory_space=pltpu.HBM)
_DMA = pltpu.SemaphoreType.DMA


def _call(body, *, name, grid, in_specs, out_specs, out_shape, scratch_shapes, compiler_params, operands, comm=None):
    if comm is None:
        res = pl.pallas_call(body, name=name, grid=grid, in_specs=in_specs, out_specs=out_specs, out_shape=out_shape,
                             scratch_shapes=scratch_shapes, compiler_params=compiler_params)(*operands)
        return list(res), []
    n_in, n_out, n_scr = len(in_specs), len(out_specs), len(scratch_shapes)
    ci, co = len(comm.ins), len(comm.out_shapes)
    last = grid[0] - 1

    def hosted(*refs):
        ins, cin = refs[:n_in], refs[n_in:n_in + ci]
        p = n_in + ci
        outs, cout = refs[p:p + n_out], refs[p + n_out:p + n_out + co]
        p += n_out + co
        scr, sems = refs[p:p + n_scr], refs[p + n_scr:]
        pl.when(pl.program_id(0) == 0)(lambda: comm.start(cin, cout, sems))
        body(*ins, *outs, *scr)
        pl.when(pl.program_id(0) == last)(lambda: comm.finish(cin, cout, sems))

    res = pl.pallas_call(
        hosted, name=name, grid=grid, in_specs=list(in_specs) + [_ANY] * ci, out_specs=list(out_specs) + [_ANY] * co,
        out_shape=list(out_shape) + comm.out_shapes, scratch_shapes=list(scratch_shapes) + comm.sems,
        input_output_aliases={n_in + a: n_out + b for a, b in comm.aliases.items()},
        compiler_params=compiler_params)(*operands, *comm.ins)
    return list(res[:n_out]), list(res[n_out:])


def _run_comm(comm, name):
    ci, co = len(comm.ins), len(comm.out_shapes)

    def body(*refs):
        cin, cout, sems = refs[:ci], refs[ci:ci + co], refs[ci + co:]
        comm.start(cin, cout, sems)
        comm.finish(cin, cout, sems)

    return list(pl.pallas_call(body, name=name, in_specs=[_ANY] * ci, out_specs=[_ANY] * co, out_shape=comm.out_shapes,
                               scratch_shapes=comm.sems, input_output_aliases=comm.aliases)(*comm.ins))


def project(x, winT, b, name, comm=None):
    T = x.shape[0]
    R = ROWS_PROJ

    def body(x_ref, w_ref, b_ref, h_ref):
        xb = x_ref[...].astype(MXU_DTYPE)
        for c0 in range(0, N_COLS, _PROJ_CHUNK):
            c1 = min(c0 + _PROJ_CHUNK, N_COLS)
            h_ref[:, c0:c1] = _mm_nt(xb, w_ref[c0:c1, :]) + b_ref[:, c0:c1]

    (h,), extra = _call(
        body, name=name, grid=(T // R,),
        in_specs=[pl.BlockSpec((R, D_MODEL), lambda t: (t, 0)), _resident(), _const_block(b.shape)],
        out_specs=[pl.BlockSpec((R, N_COLS), lambda t: (t, 0))],
        out_shape=[jax.ShapeDtypeStruct((T, N_COLS), F32)], scratch_shapes=[],
        compiler_params=_layer_params([winT]), operands=(x, winT, b), comm=comm)
    return h, extra


_KV_BLOCK = C_K // (2 * KV_W)


def layer_forward(x, h, wts, S, name, comm=None):
    T = x.shape[0]
    R = ROWS
    npt = R // BLK
    (y,), extra = _call(
        _make_fwd_body(R, S), name=name, grid=(T // R,),
        in_specs=[pl.BlockSpec((R, D_MODEL), lambda t: (t, 0)), pl.BlockSpec((R, N_COLS), lambda t: (t, 0)),
                  pl.BlockSpec((BLK, 2 * KV_W), lambda t: (jnp.maximum(t * npt - 1, 0), _KV_BLOCK))]
        + _weight_specs(_W_NAMES, wts),
        out_specs=[pl.BlockSpec((R, D_MODEL), lambda t: (t, 0))],
        out_shape=[jax.ShapeDtypeStruct((T, D_MODEL), F32)],
        scratch_shapes=[pltpu.VMEM((R, ATTN_W), F32), pltpu.VMEM((R, SGU_W), F32)],
        compiler_params=_layer_params([wts[n] for n in _W_NAMES if n in _BIG_RESIDENT]),
        operands=(x, h, h, *[wts[n] for n in _W_NAMES]), comm=comm)
    return y, extra


_ACC_SHAPES = dict(dwout=(D_MODEL, D_MODEL), dpa=(ATTN_W, D_MODEL), dpb=(SGU_W, D_MODEL), dbin=(1, N_COLS),
                   dbout=(1, D_MODEL), dlng=(1, D_MODEL), dlnb=(1, D_MODEL), dvng=(1, SGU_W), dvnb=(1, SGU_W),
                   dws=(SGU_G, BLK, BLK), dbs=(BLK, SGU_W), dsink=(8, 128), loss=(8, 128))


def layer_backward(x, h, aux, wts, S, is_last, name, comm=None):
    T = x.shape[0]
    R = ROWS_BWD
    nT = T // R
    npt = R // BLK
    rev = lambda i: (nT - 1 - i, 0)
    halo = lambda i: (jnp.maximum((nT - 1 - i) * npt - 1, 0), _KV_BLOCK)
    names = _W_NAMES + _WT_NAMES
    out_shape = ([jax.ShapeDtypeStruct((T, D_MODEL), F32), jax.ShapeDtypeStruct((T, N_COLS), MXU_DTYPE)]
                 + [jax.ShapeDtypeStruct(_ACC_SHAPES[n], F32) for n in _ACC_NAMES])
    outs, extra = _call(
        _make_bwd_body(R, S, nT, is_last), name=name, grid=(nT,),
        in_specs=[pl.BlockSpec((R, D_MODEL), rev), pl.BlockSpec((R, N_COLS), rev), pl.BlockSpec((BLK, 2 * KV_W), halo),
                  pl.BlockSpec((R, D_MODEL), rev)] + _weight_specs(names, wts),
        out_specs=[pl.BlockSpec((R, D_MODEL), rev), pl.BlockSpec((R, N_COLS), rev)]
        + [_operand_spec(n, _ACC_SHAPES[n]) for n in _ACC_NAMES],
        out_shape=out_shape,
        scratch_shapes=[pltpu.VMEM((R, ATTN_W), F32), pltpu.VMEM((R, SGU_W), F32), pltpu.VMEM((BLK, 2 * KV_W), F32)],
        compiler_params=_layer_params([wts[n] for n in names if n in _BIG_RESIDENT]
                                      + [jax.ShapeDtypeStruct(_ACC_SHAPES[n], F32) for n in _ACC_NAMES if n in _BIG_RESIDENT]),
        operands=(x, h, h, aux, *[wts[n] for n in names]), comm=comm)
    return outs[0], outs[1], dict(zip(_ACC_NAMES, outs[2:])), extra


def input_grads(dres, dh, x, winT, name, comm=None):
    T = x.shape[0]
    R = ROWS

    def body(dres_ref, dh_ref, x_ref, winT_ref, dx_ref, dwin_ref):
        @pl.when(pl.program_id(0) == 0)
        def _():
            dwin_ref[...] = jnp.zeros(dwin_ref.shape, F32)

        dh = dh_ref[...]
        dx_ref[...] = dres_ref[...] + jnp.dot(dh, winT_ref[...], preferred_element_type=F32)
        dwin_ref[...] += _mm_tn(dh, x_ref[...])

    row = lambda t: (t, 0)
    dwin_shape = jax.ShapeDtypeStruct((N_COLS, D_MODEL), F32)
    (dx, dwinT), extra = _call(
        body, name=name, grid=(T // R,),
        in_specs=[pl.BlockSpec((R, D_MODEL), row), pl.BlockSpec((R, N_COLS), row), pl.BlockSpec((R, D_MODEL), row),
                  _resident()],
        out_specs=[pl.BlockSpec((R, D_MODEL), row), _resident()],
        out_shape=[jax.ShapeDtypeStruct((T, D_MODEL), F32), dwin_shape], scratch_shapes=[],
        compiler_params=_layer_params([winT, dwin_shape]), operands=(dres, dh, x, winT), comm=comm)
    return dx, dwinT, extra


def input_norm_forward(x, g, b, name, comm=None):
    T = x.shape[0]
    R = ROWS

    def body(x_ref, g_ref, b_ref, y_ref):
        y_ref[...] = _ln_fwd(x_ref[...], g_ref[...], b_ref[...])[0]

    row = lambda t: (t, 0)
    (y,), extra = _call(
        body, name=name, grid=(T // R,),
        in_specs=[pl.BlockSpec((R, D_MODEL), row), _const_block(g.shape), _const_block(b.shape)],
        out_specs=[pl.BlockSpec((R, D_MODEL), row)],
        out_shape=[jax.ShapeDtypeStruct((T, D_MODEL), F32)], scratch_shapes=[],
        compiler_params=_layer_params(), operands=(x, g, b), comm=comm)
    return y, extra


def input_norm_backward(x, dy, g, name, comm=None):
    T = x.shape[0]
    R = ROWS

    def body(x_ref, dy_ref, g_ref, dx_ref, dg_ref, db_ref):
        @pl.when(pl.program_id(0) == 0)
        def _():
            dg_ref[...] = jnp.zeros(dg_ref.shape, F32)
            db_ref[...] = jnp.zeros(db_ref.shape, F32)

        dy = dy_ref[...]
        _, xhat, rstd = _ln_fwd(x_ref[...], g_ref[...], g_ref[...])
        dx_ref[...] = _ln_bwd(dy, xhat, rstd, g_ref[...])
        dg_ref[...] += _colsum(dy * xhat)
        db_ref[...] += _colsum(dy)

    row = lambda t: (t, 0)
    vec = jax.ShapeDtypeStruct((1, D_MODEL), F32)
    (dx, dg, db), extra = _call(
        body, name=name, grid=(T // R,),
        in_specs=[pl.BlockSpec((R, D_MODEL), row), pl.BlockSpec((R, D_MODEL), row), _const_block(g.shape)],
        out_specs=[pl.BlockSpec((R, D_MODEL), row), _const_block(vec.shape), _const_block(vec.shape)],
        out_shape=[jax.ShapeDtypeStruct((T, D_MODEL), F32), vec, vec], scratch_shapes=[],
        compiler_params=_layer_params(), operands=(x, dy, g), comm=comm)
    return dx, dg, db, extra


_PIECES = ("winT", "pa", "pb", "wout")
_WHOLE = dict(winT=(N_COLS, D_MODEL), pa=(ATTN_W, D_MODEL), pb=(SGU_W, D_MODEL), wout=(D_MODEL, D_MODEL))
_HALF = dict(winT=(N_COLS // 8, D_MODEL), pa=(ATTN_W // 2, D_MODEL // 4), pb=(SGU_W // 2, D_MODEL // 4),
             wout=(D_MODEL // 8, D_MODEL))
SMALL_ROWS = 1280


def _region(name, ref, k, h):
    hr, hc = _HALF[name]
    if name in ("pa", "pb"):
        return ref.at[pl.ds(h * hr, hr), pl.ds(k * hc, hc)]
    return ref.at[pl.ds(k * 2 * hr + h * hr, hr), :]


def _place():
    x, y, c = lax.axis_index("x"), lax.axis_index("y"), lax.axis_index("c")
    return x, y, c, 2 * x + y


def _as_chip(j, fn):
    for jj in range(N_CHIPS):
        pl.when(j == jj)(functools.partial(fn, jj))


def _remote(src, dst, send_sem, recv_sem, to):
    return pltpu.make_async_remote_copy(src_ref=src, dst_ref=dst, send_sem=send_sem, recv_sem=recv_sem,
                                        device_id=to, device_id_type=MESH)


def _core_of(k, c):
    return (k // 2, k % 2, c)


def _others(jj):
    return [k for k in range(N_CHIPS) if k != jj]


def _start_all(cps):
    for cp in cps:
        cp.start()


def _wait_all(cps):
    for cp in cps:
        cp.wait()


def gather_over_chips(names, shards):
    n = len(names)

    def own(cin, cout, sems, jj):
        cps = []
        for i, name in enumerate(names):
            hr, hc = _HALF[name]
            place = (cout[i].at[:, pl.ds(jj * hc, hc)] if name in ("pa", "pb")
                     else cout[i].at[pl.ds(jj * 2 * hr, 2 * hr), :])
            cps.append(pltpu.make_async_copy(cin[i], place, sems[2].at[i]))
        return cps

    def copies(cin, cout, sems, jj, c):
        out = []
        for i, name in enumerate(names):
            hr = _HALF[name][0]
            for k in _others(jj):
                out.append(_remote(cin[i].at[pl.ds(c * hr, hr)], _region(name, cout[i], jj, c),
                                   sems[0].at[4 * i + k], sems[1].at[4 * i + jj], _core_of(k, c)))
        return out

    def start(cin, cout, sems):
        _, _, c, j = _place()
        _as_chip(j, lambda jj: _start_all(copies(cin, cout, sems, jj, c) + own(cin, cout, sems, jj)))

    def finish(cin, cout, sems):
        _, _, c, j = _place()

        def run(jj):
            for i, name in enumerate(names):
                for k in _others(jj):
                    land = _region(name, cout[i], k, c)
                    _remote(land, land, sems[0].at[4 * i + k], sems[1].at[4 * i + k], _core_of(k, c)).wait_recv()
            for cp in copies(cin, cout, sems, jj, c):
                cp.wait_send()
            _wait_all(own(cin, cout, sems, jj))

        _as_chip(j, run)

    shapes = [jax.ShapeDtypeStruct(_WHOLE[nm], s.dtype) for nm, s in zip(names, shards)]
    return _Comm(shards, shapes, [_DMA((4 * n,)), _DMA((4 * n,)), _DMA((n,))], start, finish)


def gather_over_pair(names, wholes):
    n = len(names)

    def start(cin, cout, sems):
        x, y, c, j = _place()

        def run(jj):
            for i, name in enumerate(names):
                for k in _others(jj):
                    land = _region(name, cout[i], k, c)
                    _remote(land, land, sems[0].at[4 * i + k], sems[1].at[4 * i + k], (x, y, 1 - c)).start()

        _as_chip(j, run)

    def finish(cin, cout, sems):
        x, y, c, j = _place()

        def run(jj):
            for i, name in enumerate(names):
                for k in _others(jj):
                    theirs = _region(name, cout[i], k, 1 - c)
                    _remote(theirs, theirs, sems[0].at[4 * i + k], sems[1].at[4 * i + k], (x, y, 1 - c)).wait_recv()
            for i, name in enumerate(names):
                for k in _others(jj):
                    land = _region(name, cout[i], k, c)
                    _remote(land, land, sems[0].at[4 * i + k], sems[1].at[4 * i + k], (x, y, 1 - c)).wait_send()

        _as_chip(j, run)

    shapes = [jax.ShapeDtypeStruct(a.shape, a.dtype) for a in wholes]
    return _Comm(wholes, shapes, [_DMA((4 * n,)), _DMA((4 * n,))], start, finish, aliases={i: i for i in range(n)})


def pair_send_halves(names, parts, small=None, extra=()):
    n = len(names)
    hs = SMALL_ROWS // 2
    first_extra = n + (small is not None)

    def copies(cin, cout, sems):
        x, y, c, _ = _place()
        sib = (x, y, 1 - c)
        cps = []
        for i, name in enumerate(names):
            for k in range(N_CHIPS):
                cps.append(_remote(_region(name, cin[i], k, 1 - c), cout[i].at[k], sems[0].at[4 * i + k],
                                   sems[1].at[4 * i + k], sib))
        if small is not None:
            cps.append(_remote(cin[n].at[pl.ds((1 - c) * hs, hs)], cout[n], sems[0].at[4 * n], sems[1].at[4 * n], sib))
        for e in range(len(extra)):
            cps.append(_remote(cin[first_extra + e], cout[first_extra + e], sems[0].at[4 * n + 1 + e],
                               sems[1].at[4 * n + 1 + e], sib))
        return cps

    start = lambda cin, cout, sems: _start_all(copies(cin, cout, sems))
    finish = lambda cin, cout, sems: _wait_all(copies(cin, cout, sems))
    shapes = [jax.ShapeDtypeStruct((N_CHIPS,) + _HALF[nm], F32) for nm in names]
    ins = list(parts)
    if small is not None:
        shapes.append(jax.ShapeDtypeStruct((hs, 128), F32))
        ins.append(small)
    shapes += [jax.ShapeDtypeStruct(a.shape, a.dtype) for a in extra]
    ins += list(extra)
    n_sem = 4 * n + 1 + len(extra)
    return _Comm(ins, shapes, [_DMA((n_sem,)), _DMA((n_sem,))], start, finish)


def chip_exchange(sums, small=None):
    n = len(sums)

    def copies(cin, cout, sems, jj, c):
        out = []
        for k in _others(jj):
            for i in range(n):
                out.append(_remote(cin[i].at[k], cout[i].at[jj], sems[0].at[4 * i + k], sems[1].at[4 * i + jj], _core_of(k, c)))
            if small is not None:
                out.append(_remote(cin[n], cout[n].at[jj], sems[0].at[4 * n + k], sems[1].at[4 * n + jj], _core_of(k, c)))
        return out

    def start(cin, cout, sems):
        _, _, c, j = _place()

        def run(jj):
            if small is not None:
                pltpu.make_async_copy(cin[n], cout[n].at[jj], sems[2]).start()
            for cp in copies(cin, cout, sems, jj, c):
                cp.start()

        _as_chip(j, run)

    def finish(cin, cout, sems):
        _, _, c, j = _place()

        def run(jj):
            for k in _others(jj):
                for i in range(n):
                    _remote(cin[i].at[k], cout[i].at[k], sems[0].at[4 * i + k], sems[1].at[4 * i + k], _core_of(k, c)).wait_recv()
                if small is not None:
                    _remote(cin[n], cout[n].at[k], sems[0].at[4 * n + k], sems[1].at[4 * n + k], _core_of(k, c)).wait_recv()
            for cp in copies(cin, cout, sems, jj, c):
                cp.wait_send()
            if small is not None:
                pltpu.make_async_copy(cin[n], cout[n].at[jj], sems[2]).wait()

        _as_chip(j, run)

    shapes = [jax.ShapeDtypeStruct(a.shape, a.dtype) for a in sums]
    ins = list(sums)
    if small is not None:
        shapes.append(jax.ShapeDtypeStruct((N_CHIPS,) + small.shape, small.dtype))
        ins.append(small)
    return _Comm(ins, shapes, [_DMA((4 * n + 4,)), _DMA((4 * n + 4,)), _DMA(())], start, finish)


def pair_send_totals(arrs):
    n = len(arrs)

    def copies(cin, cout, sems):
        x, y, c, _ = _place()
        return [_remote(cin[i], cout[i], sems[0].at[i], sems[1].at[i], (x, y, 1 - c)) for i in range(n)]

    start = lambda cin, cout, sems: _start_all(copies(cin, cout, sems))
    finish = lambda cin, cout, sems: _wait_all(copies(cin, cout, sems))
    return _Comm(arrs, [jax.ShapeDtypeStruct(a.shape, a.dtype) for a in arrs], [_DMA((n,)), _DMA((n,))], start, finish)


def _plain_params(n_axes):
    return pltpu.CompilerParams(dimension_semantics=("arbitrary",) * n_axes, vmem_limit_bytes=V7X_VMEM_BYTES // 2)


def add_own_halves(names, parts, landed, core, name):
    n = len(names)

    def body(core_ref, *refs):
        for i in range(n):
            refs[2 * n + i][0] = (refs[i][...] + refs[n + i][0]).astype(MXU_DTYPE)

    in_specs, out_specs = [], []
    for nm in names:
        hr, hc = _HALF[nm]
        if nm in ("pa", "pb"):
            in_specs.append(pl.BlockSpec((hr // 2, hc), lambda k, s, cr: (cr[0] * 2 + s, k)))
        else:
            in_specs.append(pl.BlockSpec((hr // 2, hc), lambda k, s, cr: (k * 4 + cr[0] * 2 + s, 0)))
    for nm in names:
        hr, hc = _HALF[nm]
        in_specs.append(pl.BlockSpec((1, hr // 2, hc), lambda k, s, cr: (k, s, 0)))
        out_specs.append(pl.BlockSpec((1, hr // 2, hc), lambda k, s, cr: (k, s, 0)))
    return pl.pallas_call(
        body, name=name,
        grid_spec=pltpu.PrefetchScalarGridSpec(num_scalar_prefetch=1, grid=(N_CHIPS, 2), in_specs=in_specs, out_specs=out_specs),
        out_shape=[jax.ShapeDtypeStruct((N_CHIPS,) + _HALF[nm], MXU_DTYPE) for nm in names],
        compiler_params=_plain_params(2),
    )(core, *parts, *landed)


def add_small_half(small, landed, core):
    hs = SMALL_ROWS // 2

    def body(core_ref, a_ref, b_ref, o_ref):
        o_ref[...] = a_ref[...] + b_ref[...]

    return pl.pallas_call(
        body, name="add_small_half",
        grid_spec=pltpu.PrefetchScalarGridSpec(
            num_scalar_prefetch=1, grid=(1,),
            in_specs=[pl.BlockSpec((hs, 128), lambda s, cr: (cr[0], 0)), pl.BlockSpec((hs, 128), lambda s, cr: (0, 0))],
            out_specs=pl.BlockSpec((hs, 128), lambda s, cr: (0, 0))),
        out_shape=jax.ShapeDtypeStruct((hs, 128), F32),
    )(core, small, landed)


def sum_chips(sums, landed, chips, name, small_landed=None):
    n = len(sums)

    def body(chips_ref, *refs):
        ins, outs = refs[:4 * n + (small_landed is not None)], refs[4 * n + (small_landed is not None):]
        for i in range(n):
            tot = ins[4 * i][0].astype(F32)
            for r in range(1, N_CHIPS):
                tot = tot + ins[4 * i + r][0].astype(F32)
            outs[i][...] = tot
        if small_landed is not None:
            @pl.when(pl.program_id(0) == 0)
            def _():
                sm = ins[4 * n]
                outs[n][...] = ((sm[0] + sm[1]) + sm[2]) + sm[3]

    in_specs, out_specs, operands, shapes = [], [], [], []
    for a, l in zip(sums, landed):
        _, hr, hc = a.shape
        for r in range(N_CHIPS):
            in_specs.append(pl.BlockSpec((1, hr // 2, hc), functools.partial(lambda s, ch, r: (ch[r], s, 0), r=r)))
            operands.append(a if r == 0 else l)
        out_specs.append(pl.BlockSpec((hr // 2, hc), lambda s, ch: (s, 0)))
        shapes.append(jax.ShapeDtypeStruct((hr, hc), F32))
    if small_landed is not None:
        in_specs.append(pl.BlockSpec(small_landed.shape, lambda s, ch: (0, 0, 0)))
        out_specs.append(pl.BlockSpec(small_landed.shape[1:], lambda s, ch: (0, 0)))
        operands.append(small_landed)
        shapes.append(jax.ShapeDtypeStruct(small_landed.shape[1:], F32))
    return pl.pallas_call(
        body, name=name,
        grid_spec=pltpu.PrefetchScalarGridSpec(num_scalar_prefetch=1, grid=(2,), in_specs=in_specs, out_specs=out_specs),
        out_shape=shapes, compiler_params=_plain_params(1),
    )(chips, *operands)


N_STEPS = 8


def adamw(ws, gs, ms, vs):
    n = len(ws)
    c1 = 1.0 / (1.0 - ADAM_B1 ** ADAM_STEP)
    c2 = 1.0 / (1.0 - ADAM_B2 ** ADAM_STEP)

    def body(*refs):
        for i in range(n):
            w, g, m, v = (refs[k * n + i][...] for k in range(4))
            m = ADAM_B1 * m + (1.0 - ADAM_B1) * g
            v = ADAM_B2 * v + (1.0 - ADAM_B2) * (g * g)
            refs[4 * n + i][...] = -ADAM_LR * ((m * c1) / (jnp.sqrt(v * c2) + ADAM_EPS) + ADAM_WD * w)
            refs[5 * n + i][...] = m
            refs[6 * n + i][...] = v

    per_layer = N_STEPS // DEPTH

    def spec(a):
        if a.ndim == 2:
            return pl.BlockSpec((a.shape[0] // N_STEPS, a.shape[1]), lambda s: (s, 0))
        return pl.BlockSpec((1, a.shape[1] // per_layer, a.shape[2]), lambda s: (s // per_layer, s % per_layer, 0))

    outs = pl.pallas_call(
        body, name="adamw", grid=(N_STEPS,),
        in_specs=[spec(a) for a in ws] * 4, out_specs=[spec(a) for a in ws] * 3,
        out_shape=[jax.ShapeDtypeStruct(a.shape, F32) for a in ws] * 3,
        compiler_params=_plain_params(1),
    )(*ws, *gs, *ms, *vs)
    return outs[:n], outs[n:2 * n], outs[2 * n:]


_SMALL = (("w_s", (DEPTH, SGU_G, BLK, BLK)), ("ln_in_g", (D_MODEL,)), ("ln_in_b", (D_MODEL,)), ("b_in", (DEPTH, N_COLS)),
          ("sinks", (DEPTH, 8)), ("vn_g", (DEPTH, SGU_W)), ("vn_b", (DEPTH, SGU_W)),
          ("b_s", (DEPTH, SGU_G, BLK)), ("b_out", (DEPTH, D_MODEL)), ("ln_g", (DEPTH, D_MODEL)), ("ln_b", (DEPTH, D_MODEL)))
_BIG = ("w_in", "p_a", "p_b", "w_out")


def _size(shape):
    n = 1
    for d in shape:
        n *= d
    return n


def _pack_small(vals, last_row=None):
    rows = []
    for name, shape in _SMALL:
        v = vals[name]
        if _size(shape) % 128:
            v = jnp.pad(v.reshape((1, -1)), ((0, 0), (0, (-_size(shape)) % 128)))
        rows.append(v.reshape((-1, 128)))
    used = sum(r.shape[0] for r in rows)
    tail = jnp.zeros((SMALL_ROWS - used, 128), F32)
    if last_row is not None:
        tail = tail.at[-1, 0].set(last_row)
    return jnp.concatenate([rows[0], jnp.concatenate(rows[1:] + [tail], axis=0)], axis=0)


def _unpack_small(packed):
    out, pos = {}, 0
    for name, shape in _SMALL:
        n = -(-_size(shape) // 128)
        rows = packed[pos:pos + n]
        out[name] = (rows.reshape((-1,))[:_size(shape)] if _size(shape) % 128 else rows).reshape(shape)
        pos += n
    return out


def kernel(x, ln_in_g, ln_in_b, w_in, b_in, sinks, vn_g, vn_b, w_s, b_s, p_a, p_b, w_out, b_out, ln_g, ln_b, loss_target, m_ln_in_g, m_ln_in_b, m_w_in, m_b_in, m_sinks, m_vn_g, m_vn_b, m_w_s, m_b_s, m_p_a, m_p_b, m_w_out, m_b_out, m_ln_g, m_ln_b, v_ln_in_g, v_ln_in_b, v_w_in, v_b_in, v_sinks, v_vn_g, v_vn_b, v_w_s, v_b_s, v_p_a, v_p_b, v_w_out, v_b_out, v_ln_g, v_ln_b):
    weights = dict(ln_in_g=ln_in_g, ln_in_b=ln_in_b, w_in=w_in, b_in=b_in, sinks=sinks, vn_g=vn_g, vn_b=vn_b, w_s=w_s,
                   b_s=b_s, p_a=p_a, p_b=p_b, w_out=w_out, b_out=b_out, ln_g=ln_g, ln_b=ln_b)
    mom1 = dict(ln_in_g=m_ln_in_g, ln_in_b=m_ln_in_b, w_in=m_w_in, b_in=m_b_in, sinks=m_sinks, vn_g=m_vn_g, vn_b=m_vn_b,
                w_s=m_w_s, b_s=m_b_s, p_a=m_p_a, p_b=m_p_b, w_out=m_w_out, b_out=m_b_out, ln_g=m_ln_g, ln_b=m_ln_b)
    mom2 = dict(ln_in_g=v_ln_in_g, ln_in_b=v_ln_in_b, w_in=v_w_in, b_in=v_b_in, sinks=v_sinks, vn_g=v_vn_g, vn_b=v_vn_b,
                w_s=v_w_s, b_s=v_b_s, p_a=v_p_a, p_b=v_p_b, w_out=v_w_out, b_out=v_b_out, ln_g=v_ln_g, ln_b=v_ln_b)
    n_seq, S, _ = x.shape
    T = n_seq * S
    c = lax.axis_index("c")
    j = 2 * lax.axis_index("x") + lax.axis_index("y")
    core = c.astype(jnp.int32).reshape((1,))
    chips = jnp.stack([j] + [r + (r >= j) for r in range(N_CHIPS - 1)]).astype(jnp.int32)
    names = list(_PIECES)
    xt, tt = x.reshape((T, D_MODEL)), loss_target.reshape((T, D_MODEL))

    tview = lambda d: dict(d, w_in=jnp.swapaxes(d["w_in"], 1, 2))
    weights_t, mom1_t, mom2_t = tview(weights), tview(mom1), tview(mom2)

    def own_shards(l):
        return [weights_t[n][l].astype(MXU_DTYPE) for n in _BIG]

    def layer_weights(l, gathered):
        d = dict(zip(_PIECES, gathered))
        d.update(paT=d["pa"].T, pbT=d["pb"].T, woutT=d["wout"].T)
        d.update(bin=b_in[l][None], sinks=sinks[l], vng=vn_g[l][None], vnb=vn_b[l][None], ws=w_s[l],
                 bs=jnp.repeat(b_s[l].T, 128, axis=1), bout=b_out[l][None], lng=ln_g[l][None], lnb=ln_b[l][None])
        return d

    own0, own1 = own_shards(0), own_shards(1)
    x0, g0a = input_norm_forward(xt, ln_in_g[None], ln_in_b[None], "ln_in_fwd", comm=gather_over_chips(names, own0))
    lw0 = layer_weights(0, _run_comm(gather_over_pair(names, g0a), "gather0_pair"))
    h0, g1a = project(x0, lw0["winT"], lw0["bin"], "layer0_proj", comm=gather_over_chips(names, own1))
    x1, g1b = layer_forward(x0, h0, lw0, S, "layer0_fwd", comm=gather_over_pair(names, g1a))
    lw1 = layer_weights(1, g1b)
    h1, _ = project(x1, lw1["winT"], lw1["bin"], "layer1_proj")

    dres1, dh1, acc1, _ = layer_backward(x1, h1, tt, lw1, S, True, "layer1_bwd")
    dx1, dwinT1, _ = input_grads(dres1, dh1, x1, lw1["winT"], "layer1_dx_dwin")
    parts1 = [dwinT1, acc1["dpa"], acc1["dpb"], acc1["dwout"]]
    dres0, dh0, acc0, landed1 = layer_backward(x0, h0, dx1, lw0, S, False, "layer0_bwd", comm=pair_send_halves(names, parts1))
    sums1 = add_own_halves(names, parts1, landed1, core, "add_own_halves1")
    dx0, dwinT0, from_chips1 = input_grads(dres0, dh0, x0, lw0["winT"], "layer0_dx_dwin", comm=chip_exchange(sums1))
    totals1 = sum_chips(sums1, from_chips1, chips, "sum_chips1")
    grad_x, d_ln_in_g, d_ln_in_b, _ = input_norm_backward(xt, dx0, ln_in_g, "ln_in_bwd")

    gl = [acc0, acc1]
    per_layer = lambda key, pick: jnp.stack([pick(gl[l][key]) for l in range(DEPTH)])
    row0 = lambda key: per_layer(key, lambda a: a[0])
    small_part = _pack_small(dict(
        ln_in_g=d_ln_in_g[0], ln_in_b=d_ln_in_b[0], b_in=row0("dbin"), sinks=per_layer("dsink", lambda a: a[:, 0]),
        vn_g=row0("dvng"), vn_b=row0("dvnb"), w_s=per_layer("dws", lambda a: a),
        b_s=per_layer("dbs", lambda a: a[:, ::128].T), b_out=row0("dbout"), ln_g=row0("dlng"), ln_b=row0("dlnb")),
        last_row=acc1["loss"][0, 0])
    parts0 = [dwinT0, acc0["dpa"], acc0["dpb"], acc0["dwout"]]
    first = _run_comm(pair_send_halves(names, parts0, small_part, extra=totals1), "pair_send_halves0")
    landed0, small_landed, sib_totals1 = first[:4], first[4], first[5:]
    sums0 = add_own_halves(names, parts0, landed0, core, "add_own_halves0")
    small_sum = add_small_half(small_part, small_landed, core)
    *from_chips0, small_from_chips = _run_comm(chip_exchange(sums0, small_sum), "chip_exchange0")
    *totals0, small_total = sum_chips(sums0, from_chips0, chips, "sum_chips0", small_landed=small_from_chips)
    *sib_totals0, sib_small = _run_comm(pair_send_totals(totals0 + [small_total]), "pair_send_totals0")

    def both_halves(mine, theirs):
        return jnp.where(c == 0, jnp.concatenate([mine, theirs], axis=0), jnp.concatenate([theirs, mine], axis=0))

    g_big = {}
    for pi, pname in enumerate(_BIG):
        g_big[pname] = jnp.stack([both_halves(totals0[pi], sib_totals0[pi]), both_halves(totals1[pi], sib_totals1[pi])])
    g_small = both_halves(small_total, sib_small)
    loss = g_small[-1, 0]

    ws = [weights_t[n] for n in _BIG] + [_pack_small(weights)]
    gs = [g_big[n] for n in _BIG] + [g_small]
    ms = [mom1_t[n] for n in _BIG] + [_pack_small(mom1)]
    vs = [mom2_t[n] for n in _BIG] + [_pack_small(mom2)]
    deltas, new_ms, new_vs = adamw(ws, gs, ms, vs)

    def named(outs):
        d = _unpack_small(outs[-1])
        d.update(dict(zip(_BIG, outs[:-1])))
        d["w_in"] = jnp.swapaxes(d["w_in"], 1, 2)
        return d

    grads = named([g_big[n] for n in _BIG] + [g_small])
    order = ("ln_in_g", "ln_in_b", "w_in", "b_in", "sinks", "vn_g", "vn_b", "w_s", "b_s", "p_a", "p_b", "w_out", "b_out",
             "ln_g", "ln_b")
    res = [loss, grad_x.reshape(x.shape)]
    for group in (grads, named(deltas), named(new_ms), named(new_vs)):
        res.extend(group[n] for n in order)
    return tuple(res)
```

```python
import functools

import jax
import jax.numpy as jnp
from jax import lax
from jax.experimental import pallas as pl
from jax.experimental.pallas import tpu as pltpu

F32 = jnp.float32
MXU_DTYPE = jnp.bfloat16

D_MODEL = 1024
DEPTH = 2
HEAD_DIM = 64
ATTN_W = 512
KV_W = 128
BLK = 128
SGU_W = 512
SGU_G = 4
N_COLS = 4864
C_Q, C_K, C_V, C_GA, C_UB, C_VB, C_GB, C_RA, C_RB = 0, 512, 640, 768, 1280, 1792, 2304, 2816, 3840
ALPHA = (2.0 * DEPTH) ** 0.25
LN_EPS = 1e-5
SCALE = HEAD_DIM ** -0.5
NEG = float(jnp.finfo(jnp.float32).min)
GELU_C = 0.7978845608028654
GELU_A = 0.044715

ADAM_LR, ADAM_B1, ADAM_B2, ADAM_EPS, ADAM_WD, ADAM_STEP = 0.001, 0.9, 0.999, 1e-08, 0.01, 10

V7X_VMEM_BYTES = 64 * 1024 * 1024
V7X_VMEM_RESERVE = 3 * 1024 * 1024
ROWS = 256
ROWS_BWD = 256
ROWS_PROJ = 512

MESH = pl.DeviceIdType.MESH
N_CHIPS = 4


def _mm(a, b):
    return jnp.dot(a.astype(MXU_DTYPE), b.astype(MXU_DTYPE), preferred_element_type=F32)


def _mm_nt(a, b):
    return lax.dot_general(a.astype(MXU_DTYPE), b.astype(MXU_DTYPE), (((1,), (1,)), ((), ())),
                           preferred_element_type=F32)


def _mm_tn(a, b):
    return lax.dot_general(a.astype(MXU_DTYPE), b.astype(MXU_DTYPE), (((0,), (0,)), ((), ())),
                           preferred_element_type=F32)


def _sigmoid(x):
    return 0.5 * jnp.tanh(0.5 * x) + 0.5


def _gelu_parts(x):
    x2 = x * x
    u = 0.5 * jnp.tanh(x * (GELU_C + (GELU_C * GELU_A) * x2)) + 0.5
    dg = u * (1.0 + x * (1.0 - u) * (2.0 * GELU_C + (6.0 * GELU_C * GELU_A) * x2))
    return x * u, dg


def _ln_fwd(z, g, b):
    mu = jnp.mean(z, axis=-1, keepdims=True)
    zc = z - mu
    var = jnp.mean(zc * zc, axis=-1, keepdims=True)
    rstd = lax.rsqrt(var + LN_EPS)
    zhat = zc * rstd
    return zhat * g + b, zhat, rstd


def _ln_bwd(dy, zhat, rstd, g):
    dzh = dy * g
    m1 = jnp.mean(dzh, axis=-1, keepdims=True)
    m2 = jnp.mean(dzh * zhat, axis=-1, keepdims=True)
    return rstd * (dzh - m1 - zhat * m2)


def _colsum(v):
    return jnp.sum(v, axis=0, keepdims=True)


def _lane_lo(rows):
    return lax.broadcasted_iota(jnp.int32, (rows, 128), 1) < HEAD_DIM


def _dup_head(x2, hk, lo):
    xr = pltpu.roll(x2, HEAD_DIM, 1)
    return jnp.where(lo, x2, xr) if hk == 0 else jnp.where(lo, xr, x2)


def _fold_head(y0, y1, lo):
    f0 = y0 + pltpu.roll(y0, HEAD_DIM, 1)
    f1 = y1 + pltpu.roll(y1, HEAD_DIM, 1)
    return jnp.where(lo, f0, f1)


def _stack_heads(t0, t1, lo):
    z = jnp.zeros_like(t0)
    return jnp.concatenate([jnp.where(lo, t0, z), jnp.where(lo, z, t0),
                            jnp.where(lo, t1, z), jnp.where(lo, z, t1)], axis=0)


def _unstack_heads(o, lo):
    return (jnp.where(lo, o[0:128], o[128:256]), jnp.where(lo, o[256:384], o[384:512]))


def _attn_probs(qs, kd, sink4, start):
    s = _mm_nt(qs, kd) * SCALE
    row = lax.broadcasted_iota(jnp.int32, (4 * BLK, 2 * BLK), 0) & (BLK - 1)
    kpos = lax.broadcasted_iota(jnp.int32, (4 * BLK, 2 * BLK), 1)
    valid = (kpos > row) & (kpos <= row + BLK)
    if start is not None:
        valid = valid & (jnp.logical_not(start) | (kpos >= BLK))
    s = jnp.where(valid, s, NEG)
    r1 = lax.broadcasted_iota(jnp.int32, (4 * BLK, 1), 0)
    sk = jnp.where(r1 < BLK, sink4[0], jnp.where(r1 < 2 * BLK, sink4[1], jnp.where(r1 < 3 * BLK, sink4[2], sink4[3])))
    m = jnp.maximum(jnp.max(s, axis=-1, keepdims=True), sk)
    p = jnp.exp(s - m)
    es = jnp.exp(sk - m)
    inv = 1.0 / (jnp.sum(p, axis=-1, keepdims=True) + es)
    return p * inv, es * inv


_PROJ_CHUNK = 512


def _kv_blocks(kvh, h_s, nb):
    ks = [kvh[:, 0:KV_W]] + [h_s[n * BLK:(n + 1) * BLK, C_K:C_K + KV_W] for n in range(nb)]
    vs = [kvh[:, KV_W:2 * KV_W]] + [h_s[n * BLK:(n + 1) * BLK, C_V:C_V + KV_W] for n in range(nb)]
    return ks, vs


def _attn_operands(h_s, ks, vs, n, hk, lo, lo2):
    rows = slice(n * BLK, (n + 1) * BLK)
    kd = _dup_head(jnp.concatenate([ks[n], ks[n + 1]], axis=0), hk, lo2)
    vd = _dup_head(jnp.concatenate([vs[n], vs[n + 1]], axis=0), hk, lo2)
    c0 = C_Q + 2 * hk * 128
    qs = _stack_heads(h_s[rows, c0:c0 + 128], h_s[rows, c0 + 128:c0 + 256], lo)
    return qs, kd, vd


def _sgu_weights(w):
    tri = (lax.broadcasted_iota(jnp.int32, (BLK, BLK), 0) >= lax.broadcasted_iota(jnp.int32, (BLK, BLK), 1))
    return tri, [jnp.where(tri, w["ws"][g], 0.0) for g in range(SGU_G)]


def _layer_forward(x, kvh, start, w, h_s, attn_s, mix_s, keep):
    R = x.shape[0]
    nb = R // BLK
    lo = _lane_lo(BLK)
    lo2 = _lane_lo(2 * BLK)
    ks, vs = _kv_blocks(kvh, h_s, nb)
    sinks = [w["sinks"][j] for j in range(8)]
    probs_l = []
    for n in range(nb):
        rows = slice(n * BLK, (n + 1) * BLK)
        for hk in range(2):
            qs, kd, vd = _attn_operands(h_s, ks, vs, n, hk, lo, lo2)
            probs, ps = _attn_probs(qs, kd, sinks[4 * hk:4 * hk + 4], start if n == 0 else None)
            t0, t1 = _unstack_heads(_mm(probs, vd), lo)
            attn_s[rows, 2 * hk * 128:(2 * hk + 1) * 128] = t0
            attn_s[rows, (2 * hk + 1) * 128:(2 * hk + 2) * 128] = t1
            if keep:
                probs_l.append((probs, ps))
    ga = h_s[:, C_GA:C_GA + ATTN_W]
    ya = attn_s[...] * (ga * _sigmoid(ga))
    gu, _ = _gelu_parts(h_s[:, C_UB:C_UB + SGU_W])
    gv, _ = _gelu_parts(h_s[:, C_VB:C_VB + SGU_W])
    vn, _, _ = _ln_fwd(gv, w["vng"][...], w["vnb"][...])
    _, wms = _sgu_weights(w)
    for n in range(nb):
        rows = slice(n * BLK, (n + 1) * BLK)
        for g in range(SGU_G):
            cols = slice(g * 128, (g + 1) * 128)
            mix_s[rows, cols] = _mm(wms[g], vn[rows, cols]) + w["bs"][:, cols]
    gb = h_s[:, C_GB:C_GB + SGU_W]
    yb = gu * mix_s[...] * (gb * _sigmoid(gb))
    a = _mm(ya, w["pa"][...])
    b = _mm(yb, w["pb"][...])
    merged = _sigmoid(h_s[:, C_RA:C_RA + D_MODEL]) * a + _sigmoid(h_s[:, C_RB:C_RB + D_MODEL]) * b
    out = _mm(merged, w["wout"][...]) + w["bout"][...]
    y, zhat, rstd = _ln_fwd(ALPHA * x + out, w["lng"][...], w["lnb"][...])
    if not keep:
        return y, None
    return y, dict(ks=ks, vs=vs, probs=probs_l, ya=ya, yb=yb, a=a, b=b, merged=merged, zhat=zhat, rstd=rstd)


def _dsilu(g, sg):
    return sg * (1.0 + g * (1.0 - sg))


_W_NAMES = ("sinks", "vng", "vnb", "ws", "bs", "pa", "pb", "wout", "bout", "lng", "lnb")
_WT_NAMES = ("paT", "pbT", "woutT")
_ACC_NAMES = ("dwout", "dpa", "dpb", "dbin", "dbout", "dlng", "dlnb", "dvng", "dvnb", "dws", "dbs", "dsink", "loss")


def _make_fwd_body(R, S):
    def body(x_ref, h_ref, kvh_ref, *rest):
        w = dict(zip(_W_NAMES, rest[:len(_W_NAMES)]))
        y_ref, attn_s, mix_s = rest[len(_W_NAMES):]
        start = (pl.program_id(0) % (S // R)) == 0
        y, _ = _layer_forward(x_ref[...], kvh_ref[...], start, w, h_ref, attn_s, mix_s, keep=False)
        y_ref[...] = y
    return body


def _make_bwd_body(R, S, nT, is_last):
    nb = R // BLK
    names = _W_NAMES + _WT_NAMES
    nw = len(names)

    def body(x_ref, h_s, kvh_ref, aux_ref, *rest):
        w = dict(zip(names, rest[:nw]))
        dres_ref, dh_ref = rest[nw:nw + 2]
        acc = dict(zip(_ACC_NAMES, rest[nw + 2:nw + 2 + len(_ACC_NAMES)]))
        attn_s, mix_s, ckv_s = rest[nw + 2 + len(_ACC_NAMES):]
        i = pl.program_id(0)
        start = ((nT - 1 - i) % (S // R)) == 0
        lo = _lane_lo(BLK)
        lo2 = _lane_lo(2 * BLK)

        @pl.when(i == 0)
        def _():
            for name in _ACC_NAMES:
                acc[name][...] = jnp.zeros(acc[name].shape, F32)
            ckv_s[...] = jnp.zeros(ckv_s.shape, F32)

        x = x_ref[...]
        y, sv = _layer_forward(x, kvh_ref[...], start, w, h_s, attn_s, mix_s, keep=True)
        if is_last:
            diff = y - aux_ref[...]
            part = 0.5 * jnp.sum(jnp.mean(diff * diff, axis=-1, keepdims=True), axis=0, keepdims=True)
            acc["loss"][...] += jnp.broadcast_to(part, acc["loss"].shape)
            dy = diff * (1.0 / D_MODEL)
        else:
            dy = aux_ref[...]

        def put(c0, val, rows=slice(None)):
            width = val.shape[1]
            dh_ref[rows, c0:c0 + width] = val.astype(dh_ref.dtype)
            acc["dbin"][:, c0:c0 + width] += _colsum(val)

        acc["dlng"][...] += _colsum(dy * sv["zhat"])
        acc["dlnb"][...] += _colsum(dy)
        dz = _ln_bwd(dy, sv["zhat"], sv["rstd"], w["lng"][...])
        dres_ref[...] = ALPHA * dz
        acc["dbout"][...] += _colsum(dz)
        acc["dwout"][...] += _mm_tn(sv["merged"], dz)
        dmerged = _mm(dz, w["woutT"][...])
        sa = _sigmoid(h_s[:, C_RA:C_RA + D_MODEL])
        da = dmerged * sa
        put(C_RA, da * sv["a"] * (1.0 - sa))
        sb = _sigmoid(h_s[:, C_RB:C_RB + D_MODEL])
        db = dmerged * sb
        put(C_RB, db * sv["b"] * (1.0 - sb))
        acc["dpa"][...] += _mm_tn(sv["ya"], da)
        acc["dpb"][...] += _mm_tn(sv["yb"], db)
        dya = _mm(da, w["paT"][...])
        dyb = _mm(db, w["pbT"][...])
        ga = h_s[:, C_GA:C_GA + ATTN_W]
        sga = _sigmoid(ga)
        put(C_GA, dya * attn_s[...] * _dsilu(ga, sga))
        attn_s[...] = dya * (ga * sga)
        ks, vs = sv["ks"], sv["vs"]
        dks = [jnp.zeros((BLK, KV_W), F32) for _ in range(nb + 1)]
        dvs = [jnp.zeros((BLK, KV_W), F32) for _ in range(nb + 1)]
        for n in range(nb):
            rows = slice(n * BLK, (n + 1) * BLK)
            ydk, ydv = [], []
            for hk in range(2):
                qs, kd, vd = _attn_operands(h_s, ks, vs, n, hk, lo, lo2)
                probs, ps = sv["probs"][2 * n + hk]
                dos = _stack_heads(attn_s[rows, 2 * hk * 128:(2 * hk + 1) * 128],
                                   attn_s[rows, (2 * hk + 1) * 128:(2 * hk + 2) * 128], lo)
                dp = _mm_nt(dos, vd)
                delta = jnp.sum(probs * dp, axis=-1, keepdims=True)
                dsk = -(ps * delta)
                for g in range(4):
                    j = 4 * hk + g
                    acc["dsink"][j:j + 1, :] += jnp.broadcast_to(_colsum(dsk[g * BLK:(g + 1) * BLK]), (1, 128))
                dss = probs * (dp - delta) * SCALE
                q0, q1 = _unstack_heads(_mm(dss, kd), lo)
                put(C_Q + 2 * hk * 128, q0, rows)
                put(C_Q + (2 * hk + 1) * 128, q1, rows)
                ydk.append(_mm_tn(dss, qs))
                ydv.append(_mm_tn(probs, dos))
            dk2 = _fold_head(ydk[0], ydk[1], lo2)
            dv2 = _fold_head(ydv[0], ydv[1], lo2)
            dks[n] = dks[n] + dk2[0:BLK]
            dks[n + 1] = dks[n + 1] + dk2[BLK:2 * BLK]
            dvs[n] = dvs[n] + dv2[0:BLK]
            dvs[n + 1] = dvs[n + 1] + dv2[BLK:2 * BLK]
        dks[nb] = dks[nb] + ckv_s[:, 0:KV_W]
        dvs[nb] = dvs[nb] + ckv_s[:, KV_W:2 * KV_W]
        ckv_s[:, 0:KV_W] = dks[0]
        ckv_s[:, KV_W:2 * KV_W] = dvs[0]
        put(C_K, jnp.concatenate(dks[1:], axis=0))
        put(C_V, jnp.concatenate(dvs[1:], axis=0))
        gb = h_s[:, C_GB:C_GB + SGU_W]
        sgb = _sigmoid(gb)
        gu, dgu = _gelu_parts(h_s[:, C_UB:C_UB + SGU_W])
        mixed = mix_s[...]
        put(C_GB, dyb * (gu * mixed) * _dsilu(gb, sgb))
        dsgu = dyb * (gb * sgb)
        put(C_UB, dsgu * mixed * dgu)
        dmixed = dsgu * gu
        gv, dgv = _gelu_parts(h_s[:, C_VB:C_VB + SGU_W])
        vn, vhat, vrstd = _ln_fwd(gv, w["vng"][...], w["vnb"][...])
        tri, wms = _sgu_weights(w)
        dbs_part = jnp.zeros((BLK, SGU_W), F32)
        for n in range(nb):
            rows = slice(n * BLK, (n + 1) * BLK)
            dbs_part = dbs_part + dmixed[rows]
            for g in range(SGU_G):
                cols = slice(g * 128, (g + 1) * 128)
                acc["dws"][g] += jnp.where(tri, _mm_nt(dmixed[rows, cols], vn[rows, cols]), 0.0)
                mix_s[rows, cols] = _mm_tn(wms[g], dmixed[rows, cols])
        acc["dbs"][...] += dbs_part
        dvn = mix_s[...]
        acc["dvng"][...] += _colsum(dvn * vhat)
        acc["dvnb"][...] += _colsum(dvn)
        put(C_VB, _ln_bwd(dvn, vhat, vrstd, w["vng"][...]) * dgv)

        @pl.when(i == nT - 1)
        def _():
            for g in range(SGU_G):
                cols = slice(g * 128, (g + 1) * 128)
                tot = jnp.sum(acc["dbs"][:, cols], axis=1, keepdims=True)
                acc["dbs"][:, cols] = jnp.broadcast_to(tot, (BLK, 128))

    return body


def _resident():
    return pl.BlockSpec(memory_space=pltpu.VMEM)


_BIG_RESIDENT = ("pa", "pb", "wout", "paT", "pbT", "woutT", "dwout", "dpa", "dpb")


def _const_block(shape):
    return pl.BlockSpec(tuple(shape), lambda *_: (0,) * len(shape))


def _operand_spec(name, shape):
    if name == "sinks":
        return pl.BlockSpec(memory_space=pltpu.SMEM)
    return _resident() if name in _BIG_RESIDENT else _const_block(shape)


def _weight_specs(names, wts):
    return [_operand_spec(n, wts[n].shape) for n in names]


def _nbytes(a):
    n = jnp.dtype(a.dtype).itemsize
    for d in a.shape:
        n *= d
    return n


def _layer_params(resident=()):
    scoped = V7X_VMEM_BYTES - V7X_VMEM_RESERVE - sum(_nbytes(a) for a in resident)
    return pltpu.CompilerParams(dimension_semantics=("arbitrary",), vmem_limit_bytes=scoped)


class _Comm:
    def __init__(self, ins, out_shapes, sems, start, finish, aliases=None):
        self.ins = [pltpu.with_memory_space_constraint(a, pltpu.HBM) for a in ins]
        self.out_shapes = [pltpu.HBM(s.shape, s.dtype) for s in out_shapes]
        self.sems = list(sems)
        self.start, self.finish, self.aliases = start, finish, dict(aliases or {})


_ANY = pl.BlockSpec(memory_space=pltpu.HBM)
_DMA = pltpu.SemaphoreType.DMA


def _call(body, *, name, grid, in_specs, out_specs, out_shape, scratch_shapes, compiler_params, operands, comm=None):
    if comm is None:
        res = pl.pallas_call(body, name=name, grid=grid, in_specs=in_specs, out_specs=out_specs, out_shape=out_shape,
                             scratch_shapes=scratch_shapes, compiler_params=compiler_params)(*operands)
        return list(res), []
    n_in, n_out, n_scr = len(in_specs), len(out_specs), len(scratch_shapes)
    ci, co = len(comm.ins), len(comm.out_shapes)
    last = grid[0] - 1

    def hosted(*refs):
        ins, cin = refs[:n_in], refs[n_in:n_in + ci]
        p = n_in + ci
        outs, cout = refs[p:p + n_out], refs[p + n_out:p + n_out + co]
        p += n_out + co
        scr, sems = refs[p:p + n_scr], refs[p + n_scr:]
        pl.when(pl.program_id(0) == 0)(lambda: comm.start(cin, cout, sems))
        body(*ins, *outs, *scr)
        pl.when(pl.program_id(0) == last)(lambda: comm.finish(cin, cout, sems))

    res = pl.pallas_call(
        hosted, name=name, grid=grid, in_specs=list(in_specs) + [_ANY] * ci, out_specs=list(out_specs) + [_ANY] * co,
        out_shape=list(out_shape) + comm.out_shapes, scratch_shapes=list(scratch_shapes) + comm.sems,
        input_output_aliases={n_in + a: n_out + b for a, b in comm.aliases.items()},
        compiler_params=compiler_params)(*operands, *comm.ins)
    return list(res[:n_out]), list(res[n_out:])


def _run_comm(comm, name):
    ci, co = len(comm.ins), len(comm.out_shapes)

    def body(*refs):
        cin, cout, sems = refs[:ci], refs[ci:ci + co], refs[ci + co:]
        comm.start(cin, cout, sems)
        comm.finish(cin, cout, sems)

    return list(pl.pallas_call(body, name=name, in_specs=[_ANY] * ci, out_specs=[_ANY] * co, out_shape=comm.out_shapes,
                               scratch_shapes=comm.sems, input_output_aliases=comm.aliases)(*comm.ins))


def project(x, winT, b, name, comm=None):
    T = x.shape[0]
    R = ROWS_PROJ

    def body(x_ref, w_ref, b_ref, h_ref):
        xb = x_ref[...].astype(MXU_DTYPE)
        for c0 in range(0, N_COLS, _PROJ_CHUNK):
            c1 = min(c0 + _PROJ_CHUNK, N_COLS)
            h_ref[:, c0:c1] = _mm_nt(xb, w_ref[c0:c1, :]) + b_ref[:, c0:c1]

    (h,), extra = _call(
        body, name=name, grid=(T // R,),
        in_specs=[pl.BlockSpec((R, D_MODEL), lambda t: (t, 0)), _resident(), _const_block(b.shape)],
        out_specs=[pl.BlockSpec((R, N_COLS), lambda t: (t, 0))],
        out_shape=[jax.ShapeDtypeStruct((T, N_COLS), F32)], scratch_shapes=[],
        compiler_params=_layer_params([winT]), operands=(x, winT, b), comm=comm)
    return h, extra


_KV_BLOCK = C_K // (2 * KV_W)


def layer_forward(x, h, wts, S, name, comm=None):
    T = x.shape[0]
    R = ROWS
    npt = R // BLK
    (y,), extra = _call(
        _make_fwd_body(R, S), name=name, grid=(T // R,),
        in_specs=[pl.BlockSpec((R, D_MODEL), lambda t: (t, 0)), pl.BlockSpec((R, N_COLS), lambda t: (t, 0)),
                  pl.BlockSpec((BLK, 2 * KV_W), lambda t: (jnp.maximum(t * npt - 1, 0), _KV_BLOCK))]
        + _weight_specs(_W_NAMES, wts),
        out_specs=[pl.BlockSpec((R, D_MODEL), lambda t: (t, 0))],
        out_shape=[jax.ShapeDtypeStruct((T, D_MODEL), F32)],
        scratch_shapes=[pltpu.VMEM((R, ATTN_W), F32), pltpu.VMEM((R, SGU_W), F32)],
        compiler_params=_layer_params([wts[n] for n in _W_NAMES if n in _BIG_RESIDENT]),
        operands=(x, h, h, *[wts[n] for n in _W_NAMES]), comm=comm)
    return y, extra


_ACC_SHAPES = dict(dwout=(D_MODEL, D_MODEL), dpa=(ATTN_W, D_MODEL), dpb=(SGU_W, D_MODEL), dbin=(1, N_COLS),
                   dbout=(1, D_MODEL), dlng=(1, D_MODEL), dlnb=(1, D_MODEL), dvng=(1, SGU_W), dvnb=(1, SGU_W),
                   dws=(SGU_G, BLK, BLK), dbs=(BLK, SGU_W), dsink=(8, 128), loss=(8, 128))


def layer_backward(x, h, aux, wts, S, is_last, name, comm=None):
    T = x.shape[0]
    R = ROWS_BWD
    nT = T // R
    npt = R // BLK
    rev = lambda i: (nT - 1 - i, 0)
    halo = lambda i: (jnp.maximum((nT - 1 - i) * npt - 1, 0), _KV_BLOCK)
    names = _W_NAMES + _WT_NAMES
    out_shape = ([jax.ShapeDtypeStruct((T, D_MODEL), F32), jax.ShapeDtypeStruct((T, N_COLS), MXU_DTYPE)]
                 + [jax.ShapeDtypeStruct(_ACC_SHAPES[n], F32) for n in _ACC_NAMES])
    outs, extra = _call(
        _make_bwd_body(R, S, nT, is_last), name=name, grid=(nT,),
        in_specs=[pl.BlockSpec((R, D_MODEL), rev), pl.BlockSpec((R, N_COLS), rev), pl.BlockSpec((BLK, 2 * KV_W), halo),
                  pl.BlockSpec((R, D_MODEL), rev)] + _weight_specs(names, wts),
        out_specs=[pl.BlockSpec((R, D_MODEL), rev), pl.BlockSpec((R, N_COLS), rev)]
        + [_operand_spec(n, _ACC_SHAPES[n]) for n in _ACC_NAMES],
        out_shape=out_shape,
        scratch_shapes=[pltpu.VMEM((R, ATTN_W), F32), pltpu.VMEM((R, SGU_W), F32), pltpu.VMEM((BLK, 2 * KV_W), F32)],
        compiler_params=_layer_params([wts[n] for n in names if n in _BIG_RESIDENT]
                                      + [jax.ShapeDtypeStruct(_ACC_SHAPES[n], F32) for n in _ACC_NAMES if n in _BIG_RESIDENT]),
        operands=(x, h, h, aux, *[wts[n] for n in names]), comm=comm)
    return outs[0], outs[1], dict(zip(_ACC_NAMES, outs[2:])), extra


def input_grads(dres, dh, x, winT, name, comm=None):
    T = x.shape[0]
    R = ROWS

    def body(dres_ref, dh_ref, x_ref, winT_ref, dx_ref, dwin_ref):
        @pl.when(pl.program_id(0) == 0)
        def _():
            dwin_ref[...] = jnp.zeros(dwin_ref.shape, F32)

        dh = dh_ref[...]
        dx_ref[...] = dres_ref[...] + jnp.dot(dh, winT_ref[...], preferred_element_type=F32)
        dwin_ref[...] += _mm_tn(dh, x_ref[...])

    row = lambda t: (t, 0)
    dwin_shape = jax.ShapeDtypeStruct((N_COLS, D_MODEL), F32)
    (dx, dwinT), extra = _call(
        body, name=name, grid=(T // R,),
        in_specs=[pl.BlockSpec((R, D_MODEL), row), pl.BlockSpec((R, N_COLS), row), pl.BlockSpec((R, D_MODEL), row),
                  _resident()],
        out_specs=[pl.BlockSpec((R, D_MODEL), row), _resident()],
        out_shape=[jax.ShapeDtypeStruct((T, D_MODEL), F32), dwin_shape], scratch_shapes=[],
        compiler_params=_layer_params([winT, dwin_shape]), operands=(dres, dh, x, winT), comm=comm)
    return dx, dwinT, extra


def input_norm_forward(x, g, b, name, comm=None):
    T = x.shape[0]
    R = ROWS

    def body(x_ref, g_ref, b_ref, y_ref):
        y_ref[...] = _ln_fwd(x_ref[...], g_ref[...], b_ref[...])[0]

    row = lambda t: (t, 0)
    (y,), extra = _call(
        body, name=name, grid=(T // R,),
        in_specs=[pl.BlockSpec((R, D_MODEL), row), _const_block(g.shape), _const_block(b.shape)],
        out_specs=[pl.BlockSpec((R, D_MODEL), row)],
        out_shape=[jax.ShapeDtypeStruct((T, D_MODEL), F32)], scratch_shapes=[],
        compiler_params=_layer_params(), operands=(x, g, b), comm=comm)
    return y, extra


def input_norm_backward(x, dy, g, name, comm=None):
    T = x.shape[0]
    R = ROWS

    def body(x_ref, dy_ref, g_ref, dx_ref, dg_ref, db_ref):
        @pl.when(pl.program_id(0) == 0)
        def _():
            dg_ref[...] = jnp.zeros(dg_ref.shape, F32)
            db_ref[...] = jnp.zeros(db_ref.shape, F32)

        dy = dy_ref[...]
        _, xhat, rstd = _ln_fwd(x_ref[...], g_ref[...], g_ref[...])
        dx_ref[...] = _ln_bwd(dy, xhat, rstd, g_ref[...])
        dg_ref[...] += _colsum(dy * xhat)
        db_ref[...] += _colsum(dy)

    row = lambda t: (t, 0)
    vec = jax.ShapeDtypeStruct((1, D_MODEL), F32)
    (dx, dg, db), extra = _call(
        body, name=name, grid=(T // R,),
        in_specs=[pl.BlockSpec((R, D_MODEL), row), pl.BlockSpec((R, D_MODEL), row), _const_block(g.shape)],
        out_specs=[pl.BlockSpec((R, D_MODEL), row), _const_block(vec.shape), _const_block(vec.shape)],
        out_shape=[jax.ShapeDtypeStruct((T, D_MODEL), F32), vec, vec], scratch_shapes=[],
        compiler_params=_layer_params(), operands=(x, dy, g), comm=comm)
    return dx, dg, db, extra


_PIECES = ("winT", "pa", "pb", "wout")
_WHOLE = dict(winT=(N_COLS, D_MODEL), pa=(ATTN_W, D_MODEL), pb=(SGU_W, D_MODEL), wout=(D_MODEL, D_MODEL))
_HALF = dict(winT=(N_COLS // 8, D_MODEL), pa=(ATTN_W // 2, D_MODEL // 4), pb=(SGU_W // 2, D_MODEL // 4),
             wout=(D_MODEL // 8, D_MODEL))
SMALL_ROWS = 1280


def _region(name, ref, k, h):
    hr, hc = _HALF[name]
    if name in ("pa", "pb"):
        return ref.at[pl.ds(h * hr, hr), pl.ds(k * hc, hc)]
    return ref.at[pl.ds(k * 2 * hr + h * hr, hr), :]


def _place():
    x, y, c = lax.axis_index("x"), lax.axis_index("y"), lax.axis_index("c")
    return x, y, c, 2 * x + y


def _as_chip(j, fn):
    for jj in range(N_CHIPS):
        pl.when(j == jj)(functools.partial(fn, jj))


def _remote(src, dst, send_sem, recv_sem, to):
    return pltpu.make_async_remote_copy(src_ref=src, dst_ref=dst, send_sem=send_sem, recv_sem=recv_sem,
                                        device_id=to, device_id_type=MESH)


def _core_of(k, c):
    return (k // 2, k % 2, c)


def _others(jj):
    return [k for k in range(N_CHIPS) if k != jj]


def _start_all(cps):
    for cp in cps:
        cp.start()


def _wait_all(cps):
    for cp in cps:
        cp.wait()


def gather_over_chips(names, shards):
    n = len(names)

    def own(cin, cout, sems, jj):
        x, y, c, _ = _place()
        cps = []
        for i, name in enumerate(names):
            hr, hc = _HALF[name]
            place = (cout[i].at[:, pl.ds(jj * hc, hc)] if name in ("pa", "pb")
                     else cout[i].at[pl.ds(jj * 2 * hr, 2 * hr), :])
            cps.append(_remote(cin[i], place, sems[2].at[i], sems[3].at[i], (x, y, 1 - c)))
        return cps

    def copies(cin, cout, sems, jj, c):
        out = []
        for i, name in enumerate(names):
            hr = _HALF[name][0]
            for k in _others(jj):
                out.append(_remote(cin[i].at[pl.ds(c * hr, hr)], _region(name, cout[i], jj, c),
                                   sems[0].at[4 * i + k], sems[1].at[4 * i + jj], _core_of(k, c)))
        return out

    def start(cin, cout, sems):
        _, _, c, j = _place()
        _as_chip(j, lambda jj: _start_all(copies(cin, cout, sems, jj, c) + own(cin, cout, sems, jj)))

    def finish(cin, cout, sems):
        _, _, c, j = _place()

        def run(jj):
            for i, name in enumerate(names):
                for k in _others(jj):
                    land = _region(name, cout[i], k, c)
                    _remote(land, land, sems[0].at[4 * i + k], sems[1].at[4 * i + k], _core_of(k, c)).wait_recv()
            for cp in copies(cin, cout, sems, jj, c):
                cp.wait_send()
            _wait_all(own(cin, cout, sems, jj))

        _as_chip(j, run)

    shapes = [jax.ShapeDtypeStruct(_WHOLE[nm], s.dtype) for nm, s in zip(names, shards)]
    return _Comm(shards, shapes, [_DMA((4 * n,)), _DMA((4 * n,)), _DMA((n,)), _DMA((n,))], start, finish)


def gather_over_pair(names, wholes):
    n = len(names)

    def start(cin, cout, sems):
        x, y, c, j = _place()

        def run(jj):
            for i, name in enumerate(names):
                for k in _others(jj):
                    land = _region(name, cout[i], k, c)
                    _remote(land, land, sems[0].at[4 * i + k], sems[1].at[4 * i + k], (x, y, 1 - c)).start()

        _as_chip(j, run)

    def finish(cin, cout, sems):
        x, y, c, j = _place()

        def run(jj):
            for i, name in enumerate(names):
                for k in _others(jj):
                    theirs = _region(name, cout[i], k, 1 - c)
                    _remote(theirs, theirs, sems[0].at[4 * i + k], sems[1].at[4 * i + k], (x, y, 1 - c)).wait_recv()
            for i, name in enumerate(names):
                for k in _others(jj):
                    land = _region(name, cout[i], k, c)
                    _remote(land, land, sems[0].at[4 * i + k], sems[1].at[4 * i + k], (x, y, 1 - c)).wait_send()

        _as_chip(j, run)

    shapes = [jax.ShapeDtypeStruct(a.shape, a.dtype) for a in wholes]
    return _Comm(wholes, shapes, [_DMA((4 * n,)), _DMA((4 * n,))], start, finish, aliases={i: i for i in range(n)})


def pair_send_halves(names, parts, small=None, extra=()):
    n = len(names)
    hs = SMALL_ROWS // 2
    first_extra = n + (small is not None)

    def copies(cin, cout, sems):
        x, y, c, _ = _place()
        sib = (x, y, 1 - c)
        cps = []
        for i, name in enumerate(names):
            for k in range(N_CHIPS):
                cps.append(_remote(_region(name, cin[i], k, 1 - c), cout[i].at[k], sems[0].at[4 * i + k],
                                   sems[1].at[4 * i + k], sib))
        if small is not None:
            cps.append(_remote(cin[n].at[pl.ds((1 - c) * hs, hs)], cout[n], sems[0].at[4 * n], sems[1].at[4 * n], sib))
        for e in range(len(extra)):
            cps.append(_remote(cin[first_extra + e], cout[first_extra + e], sems[0].at[4 * n + 1 + e],
                               sems[1].at[4 * n + 1 + e], sib))
        return cps

    start = lambda cin, cout, sems: _start_all(copies(cin, cout, sems))
    finish = lambda cin, cout, sems: _wait_all(copies(cin, cout, sems))
    shapes = [jax.ShapeDtypeStruct((N_CHIPS,) + _HALF[nm], F32) for nm in names]
    ins = list(parts)
    if small is not None:
        shapes.append(jax.ShapeDtypeStruct((hs, 128), F32))
        ins.append(small)
    shapes += [jax.ShapeDtypeStruct(a.shape, a.dtype) for a in extra]
    ins += list(extra)
    n_sem = 4 * n + 1 + len(extra)
    return _Comm(ins, shapes, [_DMA((n_sem,)), _DMA((n_sem,))], start, finish)


def chip_exchange(sums, small=None):
    n = len(sums)

    def copies(cin, cout, sems, jj, c):
        out = []
        for k in _others(jj):
            for i in range(n):
                out.append(_remote(cin[i].at[k], cout[i].at[jj], sems[0].at[4 * i + k], sems[1].at[4 * i + jj], _core_of(k, c)))
            if small is not None:
                out.append(_remote(cin[n], cout[n].at[jj], sems[0].at[4 * n + k], sems[1].at[4 * n + jj], _core_of(k, c)))
        return out

    def start(cin, cout, sems):
        _, _, c, j = _place()

        def run(jj):
            if small is not None:
                pltpu.make_async_copy(cin[n], cout[n].at[jj], sems[2]).start()
            for cp in copies(cin, cout, sems, jj, c):
                cp.start()

        _as_chip(j, run)

    def finish(cin, cout, sems):
        _, _, c, j = _place()

        def run(jj):
            for k in _others(jj):
                for i in range(n):
                    _remote(cin[i].at[k], cout[i].at[k], sems[0].at[4 * i + k], sems[1].at[4 * i + k], _core_of(k, c)).wait_recv()
                if small is not None:
                    _remote(cin[n], cout[n].at[k], sems[0].at[4 * n + k], sems[1].at[4 * n + k], _core_of(k, c)).wait_recv()
            for cp in copies(cin, cout, sems, jj, c):
                cp.wait_send()
            if small is not None:
                pltpu.make_async_copy(cin[n], cout[n].at[jj], sems[2]).wait()

        _as_chip(j, run)

    shapes = [jax.ShapeDtypeStruct(a.shape, a.dtype) for a in sums]
    ins = list(sums)
    if small is not None:
        shapes.append(jax.ShapeDtypeStruct((N_CHIPS,) + small.shape, small.dtype))
        ins.append(small)
    return _Comm(ins, shapes, [_DMA((4 * n + 4,)), _DMA((4 * n + 4,)), _DMA(())], start, finish)


def pair_send_totals(arrs):
    n = len(arrs)

    def copies(cin, cout, sems):
        x, y, c, _ = _place()
        return [_remote(cin[i], cout[i], sems[0].at[i], sems[1].at[i], (x, y, 1 - c)) for i in range(n)]

    start = lambda cin, cout, sems: _start_all(copies(cin, cout, sems))
    finish = lambda cin, cout, sems: _wait_all(copies(cin, cout, sems))
    return _Comm(arrs, [jax.ShapeDtypeStruct(a.shape, a.dtype) for a in arrs], [_DMA((n,)), _DMA((n,))], start, finish)


def _plain_params(n_axes):
    return pltpu.CompilerParams(dimension_semantics=("arbitrary",) * n_axes, vmem_limit_bytes=V7X_VMEM_BYTES // 2)


def add_own_halves(names, parts, landed, core, name):
    n = len(names)

    def body(core_ref, *refs):
        for i in range(n):
            refs[2 * n + i][0] = (refs[i][...] + refs[n + i][0]).astype(MXU_DTYPE)

    in_specs, out_specs = [], []
    for nm in names:
        hr, hc = _HALF[nm]
        if nm in ("pa", "pb"):
            in_specs.append(pl.BlockSpec((hr // 2, hc), lambda k, s, cr: (cr[0] * 2 + s, k)))
        else:
            in_specs.append(pl.BlockSpec((hr // 2, hc), lambda k, s, cr: (k * 4 + cr[0] * 2 + s, 0)))
    for nm in names:
        hr, hc = _HALF[nm]
        in_specs.append(pl.BlockSpec((1, hr // 2, hc), lambda k, s, cr: (k, s, 0)))
        out_specs.append(pl.BlockSpec((1, hr // 2, hc), lambda k, s, cr: (k, s, 0)))
    return pl.pallas_call(
        body, name=name,
        grid_spec=pltpu.PrefetchScalarGridSpec(num_scalar_prefetch=1, grid=(N_CHIPS, 2), in_specs=in_specs, out_specs=out_specs),
        out_shape=[jax.ShapeDtypeStruct((N_CHIPS,) + _HALF[nm], MXU_DTYPE) for nm in names],
        compiler_params=_plain_params(2),
    )(core, *parts, *landed)


def add_small_half(small, landed, core):
    hs = SMALL_ROWS // 2

    def body(core_ref, a_ref, b_ref, o_ref):
        o_ref[...] = a_ref[...] + b_ref[...]

    return pl.pallas_call(
        body, name="add_small_half",
        grid_spec=pltpu.PrefetchScalarGridSpec(
            num_scalar_prefetch=1, grid=(1,),
            in_specs=[pl.BlockSpec((hs, 128), lambda s, cr: (cr[0], 0)), pl.BlockSpec((hs, 128), lambda s, cr: (0, 0))],
            out_specs=pl.BlockSpec((hs, 128), lambda s, cr: (0, 0))),
        out_shape=jax.ShapeDtypeStruct((hs, 128), F32),
    )(core, small, landed)


def sum_chips(sums, landed, chips, name, small_landed=None):
    n = len(sums)

    def body(chips_ref, *refs):
        ins, outs = refs[:4 * n + (small_landed is not None)], refs[4 * n + (small_landed is not None):]
        for i in range(n):
            tot = ins[4 * i][0].astype(F32)
            for r in range(1, N_CHIPS):
                tot = tot + ins[4 * i + r][0].astype(F32)
            outs[i][...] = tot
        if small_landed is not None:
            @pl.when(pl.program_id(0) == 0)
            def _():
                sm = ins[4 * n]
                outs[n][...] = ((sm[0] + sm[1]) + sm[2]) + sm[3]

    in_specs, out_specs, operands, shapes = [], [], [], []
    for a, l in zip(sums, landed):
        _, hr, hc = a.shape
        for r in range(N_CHIPS):
            in_specs.append(pl.BlockSpec((1, hr // 2, hc), functools.partial(lambda s, ch, r: (ch[r], s, 0), r=r)))
            operands.append(a if r == 0 else l)
        out_specs.append(pl.BlockSpec((hr // 2, hc), lambda s, ch: (s, 0)))
        shapes.append(jax.ShapeDtypeStruct((hr, hc), F32))
    if small_landed is not None:
        in_specs.append(pl.BlockSpec(small_landed.shape, lambda s, ch: (0, 0, 0)))
        out_specs.append(pl.BlockSpec(small_landed.shape[1:], lambda s, ch: (0, 0)))
        operands.append(small_landed)
        shapes.append(jax.ShapeDtypeStruct(small_landed.shape[1:], F32))
    return pl.pallas_call(
        body, name=name,
        grid_spec=pltpu.PrefetchScalarGridSpec(num_scalar_prefetch=1, grid=(2,), in_specs=in_specs, out_specs=out_specs),
        out_shape=shapes, compiler_params=_plain_params(1),
    )(chips, *operands)


N_STEPS = 8


def adamw(ws, gs, ms, vs):
    n = len(ws)
    c1 = 1.0 / (1.0 - ADAM_B1 ** ADAM_STEP)
    c2 = 1.0 / (1.0 - ADAM_B2 ** ADAM_STEP)

    def body(*refs):
        for i in range(n):
            w, g, m, v = (refs[k * n + i][...] for k in range(4))
            m = ADAM_B1 * m + (1.0 - ADAM_B1) * g
            v = ADAM_B2 * v + (1.0 - ADAM_B2) * (g * g)
            refs[4 * n + i][...] = -ADAM_LR * ((m * c1) / (jnp.sqrt(v * c2) + ADAM_EPS) + ADAM_WD * w)
            refs[5 * n + i][...] = m
            refs[6 * n + i][...] = v

    per_layer = N_STEPS // DEPTH

    def spec(a):
        if a.ndim == 2:
            return pl.BlockSpec((a.shape[0] // N_STEPS, a.shape[1]), lambda s: (s, 0))
        return pl.BlockSpec((1, a.shape[1] // per_layer, a.shape[2]), lambda s: (s // per_layer, s % per_layer, 0))

    outs = pl.pallas_call(
        body, name="adamw", grid=(N_STEPS,),
        in_specs=[spec(a) for a in ws] * 4, out_specs=[spec(a) for a in ws] * 3,
        out_shape=[jax.ShapeDtypeStruct(a.shape, F32) for a in ws] * 3,
        compiler_params=_plain_params(1),
    )(*ws, *gs, *ms, *vs)
    return outs[:n], outs[n:2 * n], outs[2 * n:]


_SMALL = (("w_s", (DEPTH, SGU_G, BLK, BLK)), ("ln_in_g", (D_MODEL,)), ("ln_in_b", (D_MODEL,)), ("b_in", (DEPTH, N_COLS)),
          ("sinks", (DEPTH, 8)), ("vn_g", (DEPTH, SGU_W)), ("vn_b", (DEPTH, SGU_W)),
          ("b_s", (DEPTH, SGU_G, BLK)), ("b_out", (DEPTH, D_MODEL)), ("ln_g", (DEPTH, D_MODEL)), ("ln_b", (DEPTH, D_MODEL)))
_BIG = ("w_in", "p_a", "p_b", "w_out")


def _size(shape):
    n = 1
    for d in shape:
        n *= d
    return n


def _pack_small(vals, last_row=None):
    rows = []
    for name, shape in _SMALL:
        v = vals[name]
        if _size(shape) % 128:
            v = jnp.pad(v.reshape((1, -1)), ((0, 0), (0, (-_size(shape)) % 128)))
        rows.append(v.reshape((-1, 128)))
    used = sum(r.shape[0] for r in rows)
    tail = jnp.zeros((SMALL_ROWS - used, 128), F32)
    if last_row is not None:
        tail = tail.at[-1, 0].set(last_row)
    return jnp.concatenate([rows[0], jnp.concatenate(rows[1:] + [tail], axis=0)], axis=0)


def _unpack_small(packed):
    out, pos = {}, 0
    for name, shape in _SMALL:
        n = -(-_size(shape) // 128)
        rows = packed[pos:pos + n]
        out[name] = (rows.reshape((-1,))[:_size(shape)] if _size(shape) % 128 else rows).reshape(shape)
        pos += n
    return out


def kernel(x, ln_in_g, ln_in_b, w_in, b_in, sinks, vn_g, vn_b, w_s, b_s, p_a, p_b, w_out, b_out, ln_g, ln_b, loss_target, m_ln_in_g, m_ln_in_b, m_w_in, m_b_in, m_sinks, m_vn_g, m_vn_b, m_w_s, m_b_s, m_p_a, m_p_b, m_w_out, m_b_out, m_ln_g, m_ln_b, v_ln_in_g, v_ln_in_b, v_w_in, v_b_in, v_sinks, v_vn_g, v_vn_b, v_w_s, v_b_s, v_p_a, v_p_b, v_w_out, v_b_out, v_ln_g, v_ln_b):
    weights = dict(ln_in_g=ln_in_g, ln_in_b=ln_in_b, w_in=w_in, b_in=b_in, sinks=sinks, vn_g=vn_g, vn_b=vn_b, w_s=w_s,
                   b_s=b_s, p_a=p_a, p_b=p_b, w_out=w_out, b_out=b_out, ln_g=ln_g, ln_b=ln_b)
    mom1 = dict(ln_in_g=m_ln_in_g, ln_in_b=m_ln_in_b, w_in=m_w_in, b_in=m_b_in, sinks=m_sinks, vn_g=m_vn_g, vn_b=m_vn_b,
                w_s=m_w_s, b_s=m_b_s, p_a=m_p_a, p_b=m_p_b, w_out=m_w_out, b_out=m_b_out, ln_g=m_ln_g, ln_b=m_ln_b)
    mom2 = dict(ln_in_g=v_ln_in_g, ln_in_b=v_ln_in_b, w_in=v_w_in, b_in=v_b_in, sinks=v_sinks, vn_g=v_vn_g, vn_b=v_vn_b,
                w_s=v_w_s, b_s=v_b_s, p_a=v_p_a, p_b=v_p_b, w_out=v_w_out, b_out=v_b_out, ln_g=v_ln_g, ln_b=v_ln_b)
    n_seq, S, _ = x.shape
    T = n_seq * S
    c = lax.axis_index("c")
    j = 2 * lax.axis_index("x") + lax.axis_index("y")
    core = c.astype(jnp.int32).reshape((1,))
    chips = jnp.stack([j] + [r + (r >= j) for r in range(N_CHIPS - 1)]).astype(jnp.int32)
    names = list(_PIECES)
    xt, tt = x.reshape((T, D_MODEL)), loss_target.reshape((T, D_MODEL))

    tview = lambda d: dict(d, w_in=jnp.swapaxes(d["w_in"], 1, 2))
    weights_t, mom1_t, mom2_t = tview(weights), tview(mom1), tview(mom2)

    def own_shards(l):
        return [weights_t[n][l].astype(MXU_DTYPE) for n in _BIG]

    def layer_weights(l, gathered):
        d = dict(zip(_PIECES, gathered))
        d.update(paT=d["pa"].T, pbT=d["pb"].T, woutT=d["wout"].T)
        d.update(bin=b_in[l][None], sinks=sinks[l], vng=vn_g[l][None], vnb=vn_b[l][None], ws=w_s[l],
                 bs=jnp.repeat(b_s[l].T, 128, axis=1), bout=b_out[l][None], lng=ln_g[l][None], lnb=ln_b[l][None])
        return d

    own0, own1 = own_shards(0), own_shards(1)
    x0, g0a = input_norm_forward(xt, ln_in_g[None], ln_in_b[None], "ln_in_fwd", comm=gather_over_chips(names, own0))
    lw0 = layer_weights(0, _run_comm(gather_over_pair(names, g0a), "gather0_pair"))
    h0, g1a = project(x0, lw0["winT"], lw0["bin"], "layer0_proj", comm=gather_over_chips(names, own1))
    x1, g1b = layer_forward(x0, h0, lw0, S, "layer0_fwd", comm=gather_over_pair(names, g1a))
    lw1 = layer_weights(1, g1b)
    h1, _ = project(x1, lw1["winT"], lw1["bin"], "layer1_proj")

    dres1, dh1, acc1, _ = layer_backward(x1, h1, tt, lw1, S, True, "layer1_bwd")
    dx1, dwinT1, _ = input_grads(dres1, dh1, x1, lw1["winT"], "layer1_dx_dwin")
    parts1 = [dwinT1, acc1["dpa"], acc1["dpb"], acc1["dwout"]]
    dres0, dh0, acc0, landed1 = layer_backward(x0, h0, dx1, lw0, S, False, "layer0_bwd", comm=pair_send_halves(names, parts1))
    sums1 = add_own_halves(names, parts1, landed1, core, "add_own_halves1")
    dx0, dwinT0, from_chips1 = input_grads(dres0, dh0, x0, lw0["winT"], "layer0_dx_dwin", comm=chip_exchange(sums1))
    totals1 = sum_chips(sums1, from_chips1, chips, "sum_chips1")
    grad_x, d_ln_in_g, d_ln_in_b, _ = input_norm_backward(xt, dx0, ln_in_g, "ln_in_bwd")

    gl = [acc0, acc1]
    per_layer = lambda key, pick: jnp.stack([pick(gl[l][key]) for l in range(DEPTH)])
    row0 = lambda key: per_layer(key, lambda a: a[0])
    small_part = _pack_small(dict(
        ln_in_g=d_ln_in_g[0], ln_in_b=d_ln_in_b[0], b_in=row0("dbin"), sinks=per_layer("dsink", lambda a: a[:, 0]),
        vn_g=row0("dvng"), vn_b=row0("dvnb"), w_s=per_layer("dws", lambda a: a),
        b_s=per_layer("dbs", lambda a: a[:, ::128].T), b_out=row0("dbout"), ln_g=row0("dlng"), ln_b=row0("dlnb")),
        last_row=acc1["loss"][0, 0])
    parts0 = [dwinT0, acc0["dpa"], acc0["dpb"], acc0["dwout"]]
    first = _run_comm(pair_send_halves(names, parts0, small_part, extra=totals1), "pair_send_halves0")
    landed0, small_landed, sib_totals1 = first[:4], first[4], first[5:]
    sums0 = add_own_halves(names, parts0, landed0, core, "add_own_halves0")
    small_sum = add_small_half(small_part, small_landed, core)
    *from_chips0, small_from_chips = _run_comm(chip_exchange(sums0, small_sum), "chip_exchange0")
    *totals0, small_total = sum_chips(sums0, from_chips0, chips, "sum_chips0", small_landed=small_from_chips)
    *sib_totals0, sib_small = _run_comm(pair_send_totals(totals0 + [small_total]), "pair_send_totals0")

    def both_halves(mine, theirs):
        return jnp.where(c == 0, jnp.concatenate([mine, theirs], axis=0), jnp.concatenate([theirs, mine], axis=0))

    g_big = {}
    for pi, pname in enumerate(_BIG):
        g_big[pname] = jnp.stack([both_halves(totals0[pi], sib_totals0[pi]), both_halves(totals1[pi], sib_totals1[pi])])
    g_small = both_halves(small_total, sib_small)
    loss = g_small[-1, 0]

    ws = [weights_t[n] for n in _BIG] + [_pack_small(weights)]
    gs = [g_big[n] for n in _BIG] + [g_small]
    ms = [mom1_t[n] for n in _BIG] + [_pack_small(mom1)]
    vs = [mom2_t[n] for n in _BIG] + [_pack_small(mom2)]
    deltas, new_ms, new_vs = adamw(ws, gs, ms, vs)

    def named(outs):
        d = _unpack_small(outs[-1])
        d.update(dict(zip(_BIG, outs[:-1])))
        d["w_in"] = jnp.swapaxes(d["w_in"], 1, 2)
        return d

    grads = named([g_big[n] for n in _BIG] + [g_small])
    order = ("ln_in_g", "ln_in_b", "w_in", "b_in", "sinks", "vn_g", "vn_b", "w_s", "b_s", "p_a", "p_b", "w_out", "b_out",
             "ln_g", "ln_b")
    res = [loss, grad_x.reshape(x.shape)]
    for group in (grads, named(deltas), named(new_ms), named(new_vs)):
        res.extend(group[n] for n in order)
    return tuple(res)
```

```python
import functools

import jax
import jax.numpy as jnp
from jax import lax
from jax.experimental import pallas as pl
from jax.experimental.pallas import tpu as pltpu

F32 = jnp.float32
MXU_DTYPE = jnp.bfloat16

D_MODEL = 1024
DEPTH = 2
HEAD_DIM = 64
ATTN_W = 512
KV_W = 128
BLK = 128
SGU_W = 512
SGU_G = 4
N_COLS = 4864
C_Q, C_K, C_V, C_GA, C_UB, C_VB, C_GB, C_RA, C_RB = 0, 512, 640, 768, 1280, 1792, 2304, 2816, 3840
ALPHA = (2.0 * DEPTH) ** 0.25
LN_EPS = 1e-5
SCALE = HEAD_DIM ** -0.5
NEG = float(jnp.finfo(jnp.float32).min)
GELU_C = 0.7978845608028654
GELU_A = 0.044715

ADAM_LR, ADAM_B1, ADAM_B2, ADAM_EPS, ADAM_WD, ADAM_STEP = 0.001, 0.9, 0.999, 1e-08, 0.01, 10

V7X_VMEM_BYTES = 64 * 1024 * 1024
V7X_VMEM_RESERVE = 3 * 1024 * 1024
ROWS = 256
ROWS_BWD = 256
ROWS_PROJ = 512

MESH = pl.DeviceIdType.MESH
N_CHIPS = 4


def _mm(a, b):
    return jnp.dot(a.astype(MXU_DTYPE), b.astype(MXU_DTYPE), preferred_element_type=F32)


def _mm_nt(a, b):
    return lax.dot_general(a.astype(MXU_DTYPE), b.astype(MXU_DTYPE), (((1,), (1,)), ((), ())),
                           preferred_element_type=F32)


def _mm_tn(a, b):
    return lax.dot_general(a.astype(MXU_DTYPE), b.astype(MXU_DTYPE), (((0,), (0,)), ((), ())),
                           preferred_element_type=F32)


def _sigmoid(x):
    return 0.5 * jnp.tanh(0.5 * x) + 0.5


def _gelu_parts(x):
    x2 = x * x
    u = 0.5 * jnp.tanh(x * (GELU_C + (GELU_C * GELU_A) * x2)) + 0.5
    dg = u * (1.0 + x * (1.0 - u) * (2.0 * GELU_C + (6.0 * GELU_C * GELU_A) * x2))
    return x * u, dg


def _ln_fwd(z, g, b):
    mu = jnp.mean(z, axis=-1, keepdims=True)
    zc = z - mu
    var = jnp.mean(zc * zc, axis=-1, keepdims=True)
    rstd = lax.rsqrt(var + LN_EPS)
    zhat = zc * rstd
    return zhat * g + b, zhat, rstd


def _ln_bwd(dy, zhat, rstd, g):
    dzh = dy * g
    m1 = jnp.mean(dzh, axis=-1, keepdims=True)
    m2 = jnp.mean(dzh * zhat, axis=-1, keepdims=True)
    return rstd * (dzh - m1 - zhat * m2)


def _colsum(v):
    return jnp.sum(v, axis=0, keepdims=True)


def _lane_lo(rows):
    return lax.broadcasted_iota(jnp.int32, (rows, 128), 1) < HEAD_DIM


def _dup_head(x2, hk, lo):
    xr = pltpu.roll(x2, HEAD_DIM, 1)
    return jnp.where(lo, x2, xr) if hk == 0 else jnp.where(lo, xr, x2)


def _fold_head(y0, y1, lo):
    f0 = y0 + pltpu.roll(y0, HEAD_DIM, 1)
    f1 = y1 + pltpu.roll(y1, HEAD_DIM, 1)
    return jnp.where(lo, f0, f1)


def _stack_heads(t0, t1, lo):
    z = jnp.zeros_like(t0)
    return jnp.concatenate([jnp.where(lo, t0, z), jnp.where(lo, z, t0),
                            jnp.where(lo, t1, z), jnp.where(lo, z, t1)], axis=0)


def _unstack_heads(o, lo):
    return (jnp.where(lo, o[0:128], o[128:256]), jnp.where(lo, o[256:384], o[384:512]))


def _attn_probs(qs, kd, sink4, start):
    s = _mm_nt(qs, kd) * SCALE
    row = lax.broadcasted_iota(jnp.int32, (4 * BLK, 2 * BLK), 0) & (BLK - 1)
    kpos = lax.broadcasted_iota(jnp.int32, (4 * BLK, 2 * BLK), 1)
    valid = (kpos > row) & (kpos <= row + BLK)
    if start is not None:
        valid = valid & (jnp.logical_not(start) | (kpos >= BLK))
    s = jnp.where(valid, s, NEG)
    r1 = lax.broadcasted_iota(jnp.int32, (4 * BLK, 1), 0)
    sk = jnp.where(r1 < BLK, sink4[0], jnp.where(r1 < 2 * BLK, sink4[1], jnp.where(r1 < 3 * BLK, sink4[2], sink4[3])))
    m = jnp.maximum(jnp.max(s, axis=-1, keepdims=True), sk)
    p = jnp.exp(s - m)
    es = jnp.exp(sk - m)
    inv = 1.0 / (jnp.sum(p, axis=-1, keepdims=True) + es)
    return p * inv, es * inv


_PROJ_CHUNK = 512


def _kv_blocks(kvh, h_s, nb):
    ks = [kvh[:, 0:KV_W]] + [h_s[n * BLK:(n + 1) * BLK, C_K:C_K + KV_W] for n in range(nb)]
    vs = [kvh[:, KV_W:2 * KV_W]] + [h_s[n * BLK:(n + 1) * BLK, C_V:C_V + KV_W] for n in range(nb)]
    return ks, vs


def _attn_operands(h_s, ks, vs, n, hk, lo, lo2):
    rows = slice(n * BLK, (n + 1) * BLK)
    kd = _dup_head(jnp.concatenate([ks[n], ks[n + 1]], axis=0), hk, lo2)
    vd = _dup_head(jnp.concatenate([vs[n], vs[n + 1]], axis=0), hk, lo2)
    c0 = C_Q + 2 * hk * 128
    qs = _stack_heads(h_s[rows, c0:c0 + 128], h_s[rows, c0 + 128:c0 + 256], lo)
    return qs, kd, vd


def _sgu_weights(w):
    tri = (lax.broadcasted_iota(jnp.int32, (BLK, BLK), 0) >= lax.broadcasted_iota(jnp.int32, (BLK, BLK), 1))
    return tri, [jnp.where(tri, w["ws"][g], 0.0) for g in range(SGU_G)]


def _layer_forward(x, kvh, start, w, h_s, attn_s, mix_s, keep):
    R = x.shape[0]
    nb = R // BLK
    lo = _lane_lo(BLK)
    lo2 = _lane_lo(2 * BLK)
    ks, vs = _kv_blocks(kvh, h_s, nb)
    sinks = [w["sinks"][j] for j in range(8)]
    probs_l = []
    for n in range(nb):
        rows = slice(n * BLK, (n + 1) * BLK)
        for hk in range(2):
            qs, kd, vd = _attn_operands(h_s, ks, vs, n, hk, lo, lo2)
            probs, ps = _attn_probs(qs, kd, sinks[4 * hk:4 * hk + 4], start if n == 0 else None)
            t0, t1 = _unstack_heads(_mm(probs, vd), lo)
            attn_s[rows, 2 * hk * 128:(2 * hk + 1) * 128] = t0
            attn_s[rows, (2 * hk + 1) * 128:(2 * hk + 2) * 128] = t1
            if keep:
                probs_l.append((probs, ps))
    ga = h_s[:, C_GA:C_GA + ATTN_W]
    ya = attn_s[...] * (ga * _sigmoid(ga))
    gu, _ = _gelu_parts(h_s[:, C_UB:C_UB + SGU_W])
    gv, _ = _gelu_parts(h_s[:, C_VB:C_VB + SGU_W])
    vn, _, _ = _ln_fwd(gv, w["vng"][...], w["vnb"][...])
    _, wms = _sgu_weights(w)
    for n in range(nb):
        rows = slice(n * BLK, (n + 1) * BLK)
        for g in range(SGU_G):
            cols = slice(g * 128, (g + 1) * 128)
            mix_s[rows, cols] = _mm(wms[g], vn[rows, cols]) + w["bs"][:, cols]
    gb = h_s[:, C_GB:C_GB + SGU_W]
    yb = gu * mix_s[...] * (gb * _sigmoid(gb))
    a = _mm(ya, w["pa"][...])
    b = _mm(yb, w["pb"][...])
    merged = _sigmoid(h_s[:, C_RA:C_RA + D_MODEL]) * a + _sigmoid(h_s[:, C_RB:C_RB + D_MODEL]) * b
    out = _mm(merged, w["wout"][...]) + w["bout"][...]
    y, zhat, rstd = _ln_fwd(ALPHA * x + out, w["lng"][...], w["lnb"][...])
    if not keep:
        return y, None
    return y, dict(ks=ks, vs=vs, probs=probs_l, ya=ya, yb=yb, a=a, b=b, merged=merged, zhat=zhat, rstd=rstd)


def _dsilu(g, sg):
    return sg * (1.0 + g * (1.0 - sg))


_W_NAMES = ("sinks", "vng", "vnb", "ws", "bs", "pa", "pb", "wout", "bout", "lng", "lnb")
_WT_NAMES = ("paT", "pbT", "woutT")
_ACC_NAMES = ("dwout", "dpa", "dpb", "dbin", "dbout", "dlng", "dlnb", "dvng", "dvnb", "dws", "dbs", "dsink", "loss")


def _make_fwd_body(R, S):
    def body(x_ref, h_ref, kvh_ref, *rest):
        w = dict(zip(_W_NAMES, rest[:len(_W_NAMES)]))
        y_ref, attn_s, mix_s = rest[len(_W_NAMES):]
        start = (pl.program_id(0) % (S // R)) == 0
        y, _ = _layer_forward(x_ref[...], kvh_ref[...], start, w, h_ref, attn_s, mix_s, keep=False)
        y_ref[...] = y
    return body


def _make_bwd_body(R, S, nT, is_last):
    nb = R // BLK
    names = _W_NAMES + _WT_NAMES
    nw = len(names)

    def body(x_ref, h_s, kvh_ref, aux_ref, *rest):
        w = dict(zip(names, rest[:nw]))
        dres_ref, dh_ref = rest[nw:nw + 2]
        acc = dict(zip(_ACC_NAMES, rest[nw + 2:nw + 2 + len(_ACC_NAMES)]))
        attn_s, mix_s, ckv_s = rest[nw + 2 + len(_ACC_NAMES):]
        i = pl.program_id(0)
        start = ((nT - 1 - i) % (S // R)) == 0
        lo = _lane_lo(BLK)
        lo2 = _lane_lo(2 * BLK)

        @pl.when(i == 0)
        def _():
            for name in _ACC_NAMES:
                acc[name][...] = jnp.zeros(acc[name].shape, F32)
            ckv_s[...] = jnp.zeros(ckv_s.shape, F32)

        x = x_ref[...]
        y, sv = _layer_forward(x, kvh_ref[...], start, w, h_s, attn_s, mix_s, keep=True)
        if is_last:
            diff = y - aux_ref[...]
            part = 0.5 * jnp.sum(jnp.mean(diff * diff, axis=-1, keepdims=True), axis=0, keepdims=True)
            acc["loss"][...] += jnp.broadcast_to(part, acc["loss"].shape)
            dy = diff * (1.0 / D_MODEL)
        else:
            dy = aux_ref[...]

        def put(c0, val, rows=slice(None)):
            width = val.shape[1]
            dh_ref[rows, c0:c0 + width] = val.astype(dh_ref.dtype)
            acc["dbin"][:, c0:c0 + width] += _colsum(val)

        acc["dlng"][...] += _colsum(dy * sv["zhat"])
        acc["dlnb"][...] += _colsum(dy)
        dz = _ln_bwd(dy, sv["zhat"], sv["rstd"], w["lng"][...])
        dres_ref[...] = ALPHA * dz
        acc["dbout"][...] += _colsum(dz)
        acc["dwout"][...] += _mm_tn(sv["merged"], dz)
        dmerged = _mm(dz, w["woutT"][...])
        sa = _sigmoid(h_s[:, C_RA:C_RA + D_MODEL])
        da = dmerged * sa
        put(C_RA, da * sv["a"] * (1.0 - sa))
        sb = _sigmoid(h_s[:, C_RB:C_RB + D_MODEL])
        db = dmerged * sb
        put(C_RB, db * sv["b"] * (1.0 - sb))
        acc["dpa"][...] += _mm_tn(sv["ya"], da)
        acc["dpb"][...] += _mm_tn(sv["yb"], db)
        dya = _mm(da, w["paT"][...])
        dyb = _mm(db, w["pbT"][...])
        ga = h_s[:, C_GA:C_GA + ATTN_W]
        sga = _sigmoid(ga)
        put(C_GA, dya * attn_s[...] * _dsilu(ga, sga))
        attn_s[...] = dya * (ga * sga)
        ks, vs = sv["ks"], sv["vs"]
        dks = [jnp.zeros((BLK, KV_W), F32) for _ in range(nb + 1)]
        dvs = [jnp.zeros((BLK, KV_W), F32) for _ in range(nb + 1)]
        for n in range(nb):
            rows = slice(n * BLK, (n + 1) * BLK)
            ydk, ydv = [], []
            for hk in range(2):
                qs, kd, vd = _attn_operands(h_s, ks, vs, n, hk, lo, lo2)
                probs, ps = sv["probs"][2 * n + hk]
                dos = _stack_heads(attn_s[rows, 2 * hk * 128:(2 * hk + 1) * 128],
                                   attn_s[rows, (2 * hk + 1) * 128:(2 * hk + 2) * 128], lo)
                dp = _mm_nt(dos, vd)
                delta = jnp.sum(probs * dp, axis=-1, keepdims=True)
                dsk = -(ps * delta)
                for g in range(4):
                    j = 4 * hk + g
                    acc["dsink"][j:j + 1, :] += jnp.broadcast_to(_colsum(dsk[g * BLK:(g + 1) * BLK]), (1, 128))
                dss = probs * (dp - delta) * SCALE
                q0, q1 = _unstack_heads(_mm(dss, kd), lo)
                put(C_Q + 2 * hk * 128, q0, rows)
                put(C_Q + (2 * hk + 1) * 128, q1, rows)
                ydk.append(_mm_tn(dss, qs))
                ydv.append(_mm_tn(probs, dos))
            dk2 = _fold_head(ydk[0], ydk[1], lo2)
            dv2 = _fold_head(ydv[0], ydv[1], lo2)
            dks[n] = dks[n] + dk2[0:BLK]
            dks[n + 1] = dks[n + 1] + dk2[BLK:2 * BLK]
            dvs[n] = dvs[n] + dv2[0:BLK]
            dvs[n + 1] = dvs[n + 1] + dv2[BLK:2 * BLK]
        dks[nb] = dks[nb] + ckv_s[:, 0:KV_W]
        dvs[nb] = dvs[nb] + ckv_s[:, KV_W:2 * KV_W]
        ckv_s[:, 0:KV_W] = dks[0]
        ckv_s[:, KV_W:2 * KV_W] = dvs[0]
        put(C_K, jnp.concatenate(dks[1:], axis=0))
        put(C_V, jnp.concatenate(dvs[1:], axis=0))
        gb = h_s[:, C_GB:C_GB + SGU_W]
        sgb = _sigmoid(gb)
        gu, dgu = _gelu_parts(h_s[:, C_UB:C_UB + SGU_W])
        mixed = mix_s[...]
        put(C_GB, dyb * (gu * mixed) * _dsilu(gb, sgb))
        dsgu = dyb * (gb * sgb)
        put(C_UB, dsgu * mixed * dgu)
        dmixed = dsgu * gu
        gv, dgv = _gelu_parts(h_s[:, C_VB:C_VB + SGU_W])
        vn, vhat, vrstd = _ln_fwd(gv, w["vng"][...], w["vnb"][...])
        tri, wms = _sgu_weights(w)
        dbs_part = jnp.zeros((BLK, SGU_W), F32)
        for n in range(nb):
            rows = slice(n * BLK, (n + 1) * BLK)
            dbs_part = dbs_part + dmixed[rows]
            for g in range(SGU_G):
                cols = slice(g * 128, (g + 1) * 128)
                acc["dws"][g] += jnp.where(tri, _mm_nt(dmixed[rows, cols], vn[rows, cols]), 0.0)
                mix_s[rows, cols] = _mm_tn(wms[g], dmixed[rows, cols])
        acc["dbs"][...] += dbs_part
        dvn = mix_s[...]
        acc["dvng"][...] += _colsum(dvn * vhat)
        acc["dvnb"][...] += _colsum(dvn)
        put(C_VB, _ln_bwd(dvn, vhat, vrstd, w["vng"][...]) * dgv)

        @pl.when(i == nT - 1)
        def _():
            for g in range(SGU_G):
                cols = slice(g * 128, (g + 1) * 128)
                tot = jnp.sum(acc["dbs"][:, cols], axis=1, keepdims=True)
                acc["dbs"][:, cols] = jnp.broadcast_to(tot, (BLK, 128))

    return body


SUB_ROWS = BLK
_KEEP = (("mg", D_MODEL), ("dz", D_MODEL), ("ya", ATTN_W), ("da", D_MODEL), ("yb", SGU_W), ("db", D_MODEL))


def _make_bwd_body2(R, S, nT, is_last):
    names = _W_NAMES + _WT_NAMES
    nw = len(names)
    n_sub = R // SUB_ROWS

    def one_block(x, kvh, aux, start, w, h_v, attn_v, mix_v, dres_v, dh_v, acc, carry, keep):
        lo = _lane_lo(BLK)
        lo2 = _lane_lo(2 * BLK)
        y, sv = _layer_forward(x, kvh, start, w, h_v, attn_v, mix_v, keep=True)
        if is_last:
            diff = y - aux
            part = 0.5 * jnp.sum(jnp.mean(diff * diff, axis=-1, keepdims=True), axis=0, keepdims=True)
            acc["loss"][...] += jnp.broadcast_to(part, acc["loss"].shape)
            dy = diff * (1.0 / D_MODEL)
        else:
            dy = aux

        def put(c0, val):
            width = val.shape[1]
            dh_v[:, c0:c0 + width] = val.astype(dh_v.dtype)
            acc["dbin"][:, c0:c0 + width] += _colsum(val)

        acc["dlng"][...] += _colsum(dy * sv["zhat"])
        acc["dlnb"][...] += _colsum(dy)
        dz = _ln_bwd(dy, sv["zhat"], sv["rstd"], w["lng"][...])
        dres_v[...] = ALPHA * dz
        acc["dbout"][...] += _colsum(dz)
        keep["mg"][...] = sv["merged"].astype(MXU_DTYPE)
        keep["dz"][...] = dz.astype(MXU_DTYPE)
        dmerged = _mm(dz, w["woutT"][...])
        sa = _sigmoid(h_v[:, C_RA:C_RA + D_MODEL])
        da = dmerged * sa
        put(C_RA, da * sv["a"] * (1.0 - sa))
        sb = _sigmoid(h_v[:, C_RB:C_RB + D_MODEL])
        db = dmerged * sb
        put(C_RB, db * sv["b"] * (1.0 - sb))
        keep["ya"][...] = sv["ya"].astype(MXU_DTYPE)
        keep["yb"][...] = sv["yb"].astype(MXU_DTYPE)
        keep["da"][...] = da.astype(MXU_DTYPE)
        keep["db"][...] = db.astype(MXU_DTYPE)
        dya = _mm(da, w["paT"][...])
        dyb = _mm(db, w["pbT"][...])
        ga = h_v[:, C_GA:C_GA + ATTN_W]
        sga = _sigmoid(ga)
        put(C_GA, dya * attn_v[...] * _dsilu(ga, sga))
        attn_v[...] = dya * (ga * sga)
        ks, vs = sv["ks"], sv["vs"]
        ydk, ydv = [], []
        for hk in range(2):
            qs, kd, vd = _attn_operands(h_v, ks, vs, 0, hk, lo, lo2)
            probs, ps = sv["probs"][hk]
            dos = _stack_heads(attn_v[:, 2 * hk * 128:(2 * hk + 1) * 128],
                               attn_v[:, (2 * hk + 1) * 128:(2 * hk + 2) * 128], lo)
            dp = _mm_nt(dos, vd)
            delta = jnp.sum(probs * dp, axis=-1, keepdims=True)
            dsk = -(ps * delta)
            for g in range(4):
                j = 4 * hk + g
                acc["dsink"][j:j + 1, :] += jnp.broadcast_to(_colsum(dsk[g * BLK:(g + 1) * BLK]), (1, 128))
            dss = probs * (dp - delta) * SCALE
            q0, q1 = _unstack_heads(_mm(dss, kd), lo)
            put(C_Q + 2 * hk * 128, q0)
            put(C_Q + (2 * hk + 1) * 128, q1)
            ydk.append(_mm_tn(dss, qs))
            ydv.append(_mm_tn(probs, dos))
        dk2 = _fold_head(ydk[0], ydk[1], lo2)
        dv2 = _fold_head(ydv[0], ydv[1], lo2)
        put(C_K, dk2[BLK:2 * BLK] + carry[0])
        put(C_V, dv2[BLK:2 * BLK] + carry[1])
        gb = h_v[:, C_GB:C_GB + SGU_W]
        sgb = _sigmoid(gb)
        gu, dgu = _gelu_parts(h_v[:, C_UB:C_UB + SGU_W])
        mixed = mix_v[...]
        put(C_GB, dyb * (gu * mixed) * _dsilu(gb, sgb))
        dsgu = dyb * (gb * sgb)
        put(C_UB, dsgu * mixed * dgu)
        dmixed = dsgu * gu
        gv, dgv = _gelu_parts(h_v[:, C_VB:C_VB + SGU_W])
        vn, vhat, vrstd = _ln_fwd(gv, w["vng"][...], w["vnb"][...])
        tri, wms = _sgu_weights(w)
        for g in range(SGU_G):
            cols = slice(g * 128, (g + 1) * 128)
            acc["dws"][g] += jnp.where(tri, _mm_nt(dmixed[:, cols], vn[:, cols]), 0.0)
            mix_v[:, cols] = _mm_tn(wms[g], dmixed[:, cols])
        acc["dbs"][...] += dmixed
        dvn = mix_v[...]
        acc["dvng"][...] += _colsum(dvn * vhat)
        acc["dvnb"][...] += _colsum(dvn)
        put(C_VB, _ln_bwd(dvn, vhat, vrstd, w["vng"][...]) * dgv)
        return dk2[0:BLK], dv2[0:BLK]

    def body(x_ref, h_s, kvh_ref, aux_ref, *rest):
        w = dict(zip(names, rest[:nw]))
        dres_ref, dh_ref = rest[nw:nw + 2]
        acc = dict(zip(_ACC_NAMES, rest[nw + 2:nw + 2 + len(_ACC_NAMES)]))
        scratch = rest[nw + 2 + len(_ACC_NAMES):]
        attn_s, mix_s, ckv_s = scratch[:3]
        keep_s = dict(zip([k for k, _ in _KEEP], scratch[3:]))
        i = pl.program_id(0)
        start = ((nT - 1 - i) % (S // R)) == 0

        @pl.when(i == 0)
        def _():
            for name in _ACC_NAMES:
                acc[name][...] = jnp.zeros(acc[name].shape, F32)
            ckv_s[...] = jnp.zeros(ckv_s.shape, F32)

        carry = (ckv_s[:, 0:KV_W], ckv_s[:, KV_W:2 * KV_W])
        for sub in reversed(range(n_sub)):
            rows = pl.ds(sub * SUB_ROWS, SUB_ROWS)
            view = lambda ref: ref.at[rows, :]
            kvh = kvh_ref[...] if sub == 0 else h_s[(sub - 1) * SUB_ROWS:sub * SUB_ROWS, C_K:C_K + 2 * KV_W]
            carry = one_block(x_ref[rows, :], kvh, aux_ref[rows, :], start if sub == 0 else None, w, view(h_s),
                              view(attn_s), view(mix_s), view(dres_ref), view(dh_ref), acc, carry,
                              {k: view(r) for k, r in keep_s.items()})
        ckv_s[:, 0:KV_W] = carry[0]
        ckv_s[:, KV_W:2 * KV_W] = carry[1]
        acc["dwout"][...] += _mm_tn(keep_s["mg"][...], keep_s["dz"][...])
        acc["dpa"][...] += _mm_tn(keep_s["ya"][...], keep_s["da"][...])
        acc["dpb"][...] += _mm_tn(keep_s["yb"][...], keep_s["db"][...])

        @pl.when(i == nT - 1)
        def _():
            for g in range(SGU_G):
                cols = slice(g * 128, (g + 1) * 128)
                tot = jnp.sum(acc["dbs"][:, cols], axis=1, keepdims=True)
                acc["dbs"][:, cols] = jnp.broadcast_to(tot, (BLK, 128))

    return body


def _resident():
    return pl.BlockSpec(memory_space=pltpu.VMEM)


_BIG_RESIDENT = ("pa", "pb", "wout", "paT", "pbT", "woutT", "dwout", "dpa", "dpb")


def _const_block(shape):
    return pl.BlockSpec(tuple(shape), lambda *_: (0,) * len(shape))


def _operand_spec(name, shape):
    if name == "sinks":
        return pl.BlockSpec(memory_space=pltpu.SMEM)
    return _resident() if name in _BIG_RESIDENT else _const_block(shape)


def _weight_specs(names, wts):
    return [_operand_spec(n, wts[n].shape) for n in names]


def _nbytes(a):
    n = jnp.dtype(a.dtype).itemsize
    for d in a.shape:
        n *= d
    return n


def _layer_params(resident=()):
    scoped = V7X_VMEM_BYTES - V7X_VMEM_RESERVE - sum(_nbytes(a) for a in resident)
    return pltpu.CompilerParams(dimension_semantics=("arbitrary",), vmem_limit_bytes=scoped)


class _Comm:
    def __init__(self, ins, out_shapes, sems, start, finish, aliases=None):
        self.ins = [pltpu.with_memory_space_constraint(a, pltpu.HBM) for a in ins]
        self.out_shapes = [pltpu.HBM(s.shape, s.dtype) for s in out_shapes]
        self.sems = list(sems)
        self.start, self.finish, self.aliases = start, finish, dict(aliases or {})


_ANY = pl.BlockSpec(memory_space=pltpu.HBM)
_DMA = pltpu.SemaphoreType.DMA


def _call(body, *, name, grid, in_specs, out_specs, out_shape, scratch_shapes, compiler_params, operands, comm=None):
    if comm is None:
        res = pl.pallas_call(body, name=name, grid=grid, in_specs=in_specs, out_specs=out_specs, out_shape=out_shape,
                             scratch_shapes=scratch_shapes, compiler_params=compiler_params)(*operands)
        return list(res), []
    n_in, n_out, n_scr = len(in_specs), len(out_specs), len(scratch_shapes)
    ci, co = len(comm.ins), len(comm.out_shapes)
    last = grid[0] - 1

    def hosted(*refs):
        ins, cin = refs[:n_in], refs[n_in:n_in + ci]
        p = n_in + ci
        outs, cout = refs[p:p + n_out], refs[p + n_out:p + n_out + co]
        p += n_out + co
        scr, sems = refs[p:p + n_scr], refs[p + n_scr:]
        pl.when(pl.program_id(0) == 0)(lambda: comm.start(cin, cout, sems))
        body(*ins, *outs, *scr)
        pl.when(pl.program_id(0) == last)(lambda: comm.finish(cin, cout, sems))

    res = pl.pallas_call(
        hosted, name=name, grid=grid, in_specs=list(in_specs) + [_ANY] * ci, out_specs=list(out_specs) + [_ANY] * co,
        out_shape=list(out_shape) + comm.out_shapes, scratch_shapes=list(scratch_shapes) + comm.sems,
        input_output_aliases={n_in + a: n_out + b for a, b in comm.aliases.items()},
        compiler_params=compiler_params)(*operands, *comm.ins)
    return list(res[:n_out]), list(res[n_out:])


def _run_comm(comm, name):
    ci, co = len(comm.ins), len(comm.out_shapes)

    def body(*refs):
        cin, cout, sems = refs[:ci], refs[ci:ci + co], refs[ci + co:]
        comm.start(cin, cout, sems)
        comm.finish(cin, cout, sems)

    return list(pl.pallas_call(body, name=name, in_specs=[_ANY] * ci, out_specs=[_ANY] * co, out_shape=comm.out_shapes,
                               scratch_shapes=comm.sems, input_output_aliases=comm.aliases)(*comm.ins))


def project(x, winT, b, name, comm=None):
    T = x.shape[0]
    R = ROWS_PROJ

    def body(x_ref, w_ref, b_ref, h_ref):
        xb = x_ref[...].astype(MXU_DTYPE)
        for c0 in range(0, N_COLS, _PROJ_CHUNK):
            c1 = min(c0 + _PROJ_CHUNK, N_COLS)
            h_ref[:, c0:c1] = _mm_nt(xb, w_ref[c0:c1, :]) + b_ref[:, c0:c1]

    (h,), extra = _call(
        body, name=name, grid=(T // R,),
        in_specs=[pl.BlockSpec((R, D_MODEL), lambda t: (t, 0)), _resident(), _const_block(b.shape)],
        out_specs=[pl.BlockSpec((R, N_COLS), lambda t: (t, 0))],
        out_shape=[jax.ShapeDtypeStruct((T, N_COLS), F32)], scratch_shapes=[],
        compiler_params=_layer_params([winT]), operands=(x, winT, b), comm=comm)
    return h, extra


_KV_BLOCK = C_K // (2 * KV_W)


def layer_forward(x, h, wts, S, name, comm=None):
    T = x.shape[0]
    R = ROWS
    npt = R // BLK
    (y,), extra = _call(
        _make_fwd_body(R, S), name=name, grid=(T // R,),
        in_specs=[pl.BlockSpec((R, D_MODEL), lambda t: (t, 0)), pl.BlockSpec((R, N_COLS), lambda t: (t, 0)),
                  pl.BlockSpec((BLK, 2 * KV_W), lambda t: (jnp.maximum(t * npt - 1, 0), _KV_BLOCK))]
        + _weight_specs(_W_NAMES, wts),
        out_specs=[pl.BlockSpec((R, D_MODEL), lambda t: (t, 0))],
        out_shape=[jax.ShapeDtypeStruct((T, D_MODEL), F32)],
        scratch_shapes=[pltpu.VMEM((R, ATTN_W), F32), pltpu.VMEM((R, SGU_W), F32)],
        compiler_params=_layer_params([wts[n] for n in _W_NAMES if n in _BIG_RESIDENT]),
        operands=(x, h, h, *[wts[n] for n in _W_NAMES]), comm=comm)
    return y, extra


_ACC_SHAPES = dict(dwout=(D_MODEL, D_MODEL), dpa=(ATTN_W, D_MODEL), dpb=(SGU_W, D_MODEL), dbin=(1, N_COLS),
                   dbout=(1, D_MODEL), dlng=(1, D_MODEL), dlnb=(1, D_MODEL), dvng=(1, SGU_W), dvnb=(1, SGU_W),
                   dws=(SGU_G, BLK, BLK), dbs=(BLK, SGU_W), dsink=(8, 128), loss=(8, 128))


def layer_backward(x, h, aux, wts, S, is_last, name, comm=None):
    T = x.shape[0]
    R = ROWS_BWD
    nT = T // R
    npt = R // BLK
    rev = lambda i: (nT - 1 - i, 0)
    halo = lambda i: (jnp.maximum((nT - 1 - i) * npt - 1, 0), _KV_BLOCK)
    names = _W_NAMES + _WT_NAMES
    out_shape = ([jax.ShapeDtypeStruct((T, D_MODEL), F32), jax.ShapeDtypeStruct((T, N_COLS), MXU_DTYPE)]
                 + [jax.ShapeDtypeStruct(_ACC_SHAPES[n], F32) for n in _ACC_NAMES])
    outs, extra = _call(
        _make_bwd_body(R, S, nT, is_last), name=name, grid=(nT,),
        in_specs=[pl.BlockSpec((R, D_MODEL), rev), pl.BlockSpec((R, N_COLS), rev), pl.BlockSpec((BLK, 2 * KV_W), halo),
                  pl.BlockSpec((R, D_MODEL), rev)] + _weight_specs(names, wts),
        out_specs=[pl.BlockSpec((R, D_MODEL), rev), pl.BlockSpec((R, N_COLS), rev)]
        + [_operand_spec(n, _ACC_SHAPES[n]) for n in _ACC_NAMES],
        out_shape=out_shape,
        scratch_shapes=[pltpu.VMEM((R, ATTN_W), F32), pltpu.VMEM((R, SGU_W), F32), pltpu.VMEM((BLK, 2 * KV_W), F32)],
        compiler_params=_layer_params([wts[n] for n in names if n in _BIG_RESIDENT]
                                      + [jax.ShapeDtypeStruct(_ACC_SHAPES[n], F32) for n in _ACC_NAMES if n in _BIG_RESIDENT]),
        operands=(x, h, h, aux, *[wts[n] for n in names]), comm=comm)
    return outs[0], outs[1], dict(zip(_ACC_NAMES, outs[2:])), extra


def input_grads(dres, dh, x, winT, name, comm=None):
    T = x.shape[0]
    R = ROWS

    def body(dres_ref, dh_ref, x_ref, winT_ref, dx_ref, dwin_ref):
        @pl.when(pl.program_id(0) == 0)
        def _():
            dwin_ref[...] = jnp.zeros(dwin_ref.shape, F32)

        dh = dh_ref[...]
        dx_ref[...] = dres_ref[...] + jnp.dot(dh, winT_ref[...], preferred_element_type=F32)
        dwin_ref[...] += _mm_tn(dh, x_ref[...])

    row = lambda t: (t, 0)
    dwin_shape = jax.ShapeDtypeStruct((N_COLS, D_MODEL), F32)
    (dx, dwinT), extra = _call(
        body, name=name, grid=(T // R,),
        in_specs=[pl.BlockSpec((R, D_MODEL), row), pl.BlockSpec((R, N_COLS), row), pl.BlockSpec((R, D_MODEL), row),
                  _resident()],
        out_specs=[pl.BlockSpec((R, D_MODEL), row), _resident()],
        out_shape=[jax.ShapeDtypeStruct((T, D_MODEL), F32), dwin_shape], scratch_shapes=[],
        compiler_params=_layer_params([winT, dwin_shape]), operands=(dres, dh, x, winT), comm=comm)
    return dx, dwinT, extra


def input_norm_forward(x, g, b, name, comm=None):
    T = x.shape[0]
    R = ROWS

    def body(x_ref, g_ref, b_ref, y_ref):
        y_ref[...] = _ln_fwd(x_ref[...], g_ref[...], b_ref[...])[0]

    row = lambda t: (t, 0)
    (y,), extra = _call(
        body, name=name, grid=(T // R,),
        in_specs=[pl.BlockSpec((R, D_MODEL), row), _const_block(g.shape), _const_block(b.shape)],
        out_specs=[pl.BlockSpec((R, D_MODEL), row)],
        out_shape=[jax.ShapeDtypeStruct((T, D_MODEL), F32)], scratch_shapes=[],
        compiler_params=_layer_params(), operands=(x, g, b), comm=comm)
    return y, extra


def input_norm_backward(x, dy, g, name, comm=None):
    T = x.shape[0]
    R = ROWS

    def body(x_ref, dy_ref, g_ref, dx_ref, dg_ref, db_ref):
        @pl.when(pl.program_id(0) == 0)
        def _():
            dg_ref[...] = jnp.zeros(dg_ref.shape, F32)
            db_ref[...] = jnp.zeros(db_ref.shape, F32)

        dy = dy_ref[...]
        _, xhat, rstd = _ln_fwd(x_ref[...], g_ref[...], g_ref[...])
        dx_ref[...] = _ln_bwd(dy, xhat, rstd, g_ref[...])
        dg_ref[...] += _colsum(dy * xhat)
        db_ref[...] += _colsum(dy)

    row = lambda t: (t, 0)
    vec = jax.ShapeDtypeStruct((1, D_MODEL), F32)
    (dx, dg, db), extra = _call(
        body, name=name, grid=(T // R,),
        in_specs=[pl.BlockSpec((R, D_MODEL), row), pl.BlockSpec((R, D_MODEL), row), _const_block(g.shape)],
        out_specs=[pl.BlockSpec((R, D_MODEL), row), _const_block(vec.shape), _const_block(vec.shape)],
        out_shape=[jax.ShapeDtypeStruct((T, D_MODEL), F32), vec, vec], scratch_shapes=[],
        compiler_params=_layer_params(), operands=(x, dy, g), comm=comm)
    return dx, dg, db, extra


_PIECES = ("winT", "pa", "pb", "wout")
_WHOLE = dict(winT=(N_COLS, D_MODEL), pa=(ATTN_W, D_MODEL), pb=(SGU_W, D_MODEL), wout=(D_MODEL, D_MODEL))
_HALF = dict(winT=(N_COLS // 8, D_MODEL), pa=(ATTN_W // 2, D_MODEL // 4), pb=(SGU_W // 2, D_MODEL // 4),
             wout=(D_MODEL // 8, D_MODEL))
SMALL_ROWS = 1280


def _region(name, ref, k, h):
    hr, hc = _HALF[name]
    if name in ("pa", "pb"):
        return ref.at[pl.ds(h * hr, hr), pl.ds(k * hc, hc)]
    return ref.at[pl.ds(k * 2 * hr + h * hr, hr), :]


def _place():
    x, y, c = lax.axis_index("x"), lax.axis_index("y"), lax.axis_index("c")
    return x, y, c, 2 * x + y


def _as_chip(j, fn):
    for jj in range(N_CHIPS):
        pl.when(j == jj)(functools.partial(fn, jj))


def _remote(src, dst, send_sem, recv_sem, to):
    return pltpu.make_async_remote_copy(src_ref=src, dst_ref=dst, send_sem=send_sem, recv_sem=recv_sem,
                                        device_id=to, device_id_type=MESH)


def _core_of(k, c):
    return (k // 2, k % 2, c)


def _others(jj):
    return [k for k in range(N_CHIPS) if k != jj]


def _start_all(cps):
    for cp in cps:
        cp.start()


def _wait_all(cps):
    for cp in cps:
        cp.wait()


def gather_over_chips(names, shards):
    n = len(names)

    def own(cin, cout, sems, jj):
        x, y, c, _ = _place()
        cps = []
        for i, name in enumerate(names):
            hr, hc = _HALF[name]
            place = (cout[i].at[:, pl.ds(jj * hc, hc)] if name in ("pa", "pb")
                     else cout[i].at[pl.ds(jj * 2 * hr, 2 * hr), :])
            cps.append(_remote(cin[i], place, sems[2].at[i], sems[3].at[i], (x, y, 1 - c)))
        return cps

    def copies(cin, cout, sems, jj, c):
        out = []
        for i, name in enumerate(names):
            hr = _HALF[name][0]
            for k in _others(jj):
                out.append(_remote(cin[i].at[pl.ds(c * hr, hr)], _region(name, cout[i], jj, c),
                                   sems[0].at[4 * i + k], sems[1].at[4 * i + jj], _core_of(k, c)))
        return out

    def start(cin, cout, sems):
        _, _, c, j = _place()
        _as_chip(j, lambda jj: _start_all(copies(cin, cout, sems, jj, c) + own(cin, cout, sems, jj)))

    def finish(cin, cout, sems):
        _, _, c, j = _place()

        def run(jj):
            for i, name in enumerate(names):
                for k in _others(jj):
                    land = _region(name, cout[i], k, c)
                    _remote(land, land, sems[0].at[4 * i + k], sems[1].at[4 * i + k], _core_of(k, c)).wait_recv()
            for cp in copies(cin, cout, sems, jj, c):
                cp.wait_send()
            _wait_all(own(cin, cout, sems, jj))

        _as_chip(j, run)

    shapes = [jax.ShapeDtypeStruct(_WHOLE[nm], s.dtype) for nm, s in zip(names, shards)]
    return _Comm(shards, shapes, [_DMA((4 * n,)), _DMA((4 * n,)), _DMA((n,)), _DMA((n,))], start, finish)


def gather_over_pair(names, wholes):
    n = len(names)

    def start(cin, cout, sems):
        x, y, c, j = _place()

        def run(jj):
            for i, name in enumerate(names):
                for k in _others(jj):
                    land = _region(name, cout[i], k, c)
                    _remote(land, land, sems[0].at[4 * i + k], sems[1].at[4 * i + k], (x, y, 1 - c)).start()

        _as_chip(j, run)

    def finish(cin, cout, sems):
        x, y, c, j = _place()

        def run(jj):
            for i, name in enumerate(names):
                for k in _others(jj):
                    theirs = _region(name, cout[i], k, 1 - c)
                    _remote(theirs, theirs, sems[0].at[4 * i + k], sems[1].at[4 * i + k], (x, y, 1 - c)).wait_recv()
            for i, name in enumerate(names):
                for k in _others(jj):
                    land = _region(name, cout[i], k, c)
                    _remote(land, land, sems[0].at[4 * i + k], sems[1].at[4 * i + k], (x, y, 1 - c)).wait_send()

        _as_chip(j, run)

    shapes = [jax.ShapeDtypeStruct(a.shape, a.dtype) for a in wholes]
    return _Comm(wholes, shapes, [_DMA((4 * n,)), _DMA((4 * n,))], start, finish, aliases={i: i for i in range(n)})


def pair_send_halves(names, parts, small=None, extra=()):
    n = len(names)
    hs = SMALL_ROWS // 2
    first_extra = n + (small is not None)

    def copies(cin, cout, sems):
        x, y, c, _ = _place()
        sib = (x, y, 1 - c)
        cps = []
        for i, name in enumerate(names):
            for k in range(N_CHIPS):
                cps.append(_remote(_region(name, cin[i], k, 1 - c), cout[i].at[k], sems[0].at[4 * i + k],
                                   sems[1].at[4 * i + k], sib))
        if small is not None:
            cps.append(_remote(cin[n].at[pl.ds((1 - c) * hs, hs)], cout[n], sems[0].at[4 * n], sems[1].at[4 * n], sib))
        for e in range(len(extra)):
            cps.append(_remote(cin[first_extra + e], cout[first_extra + e], sems[0].at[4 * n + 1 + e],
                               sems[1].at[4 * n + 1 + e], sib))
        return cps

    start = lambda cin, cout, sems: _start_all(copies(cin, cout, sems))
    finish = lambda cin, cout, sems: _wait_all(copies(cin, cout, sems))
    shapes = [jax.ShapeDtypeStruct((N_CHIPS,) + _HALF[nm], F32) for nm in names]
    ins = list(parts)
    if small is not None:
        shapes.append(jax.ShapeDtypeStruct((hs, 128), F32))
        ins.append(small)
    shapes += [jax.ShapeDtypeStruct(a.shape, a.dtype) for a in extra]
    ins += list(extra)
    n_sem = 4 * n + 1 + len(extra)
    return _Comm(ins, shapes, [_DMA((n_sem,)), _DMA((n_sem,))], start, finish)


def chip_exchange(sums, small=None):
    n = len(sums)

    def copies(cin, cout, sems, jj, c):
        out = []
        for k in _others(jj):
            for i in range(n):
                out.append(_remote(cin[i].at[k], cout[i].at[jj], sems[0].at[4 * i + k], sems[1].at[4 * i + jj], _core_of(k, c)))
            if small is not None:
                out.append(_remote(cin[n], cout[n].at[jj], sems[0].at[4 * n + k], sems[1].at[4 * n + jj], _core_of(k, c)))
        return out

    def start(cin, cout, sems):
        _, _, c, j = _place()

        def run(jj):
            if small is not None:
                pltpu.make_async_copy(cin[n], cout[n].at[jj], sems[2]).start()
            for cp in copies(cin, cout, sems, jj, c):
                cp.start()

        _as_chip(j, run)

    def finish(cin, cout, sems):
        _, _, c, j = _place()

        def run(jj):
            for k in _others(jj):
                for i in range(n):
                    _remote(cin[i].at[k], cout[i].at[k], sems[0].at[4 * i + k], sems[1].at[4 * i + k], _core_of(k, c)).wait_recv()
                if small is not None:
                    _remote(cin[n], cout[n].at[k], sems[0].at[4 * n + k], sems[1].at[4 * n + k], _core_of(k, c)).wait_recv()
            for cp in copies(cin, cout, sems, jj, c):
                cp.wait_send()
            if small is not None:
                pltpu.make_async_copy(cin[n], cout[n].at[jj], sems[2]).wait()

        _as_chip(j, run)

    shapes = [jax.ShapeDtypeStruct(a.shape, a.dtype) for a in sums]
    ins = list(sums)
    if small is not None:
        shapes.append(jax.ShapeDtypeStruct((N_CHIPS,) + small.shape, small.dtype))
        ins.append(small)
    return _Comm(ins, shapes, [_DMA((4 * n + 4,)), _DMA((4 * n + 4,)), _DMA(())], start, finish)


def pair_send_totals(arrs):
    n = len(arrs)

    def copies(cin, cout, sems):
        x, y, c, _ = _place()
        return [_remote(cin[i], cout[i], sems[0].at[i], sems[1].at[i], (x, y, 1 - c)) for i in range(n)]

    start = lambda cin, cout, sems: _start_all(copies(cin, cout, sems))
    finish = lambda cin, cout, sems: _wait_all(copies(cin, cout, sems))
    return _Comm(arrs, [jax.ShapeDtypeStruct(a.shape, a.dtype) for a in arrs], [_DMA((n,)), _DMA((n,))], start, finish)


def _pin(a):
    return pltpu.with_memory_space_constraint(a, pltpu.HBM)


def _plain_params(n_axes):
    return pltpu.CompilerParams(dimension_semantics=("arbitrary",) * n_axes, vmem_limit_bytes=V7X_VMEM_BYTES // 2)


def add_own_halves(names, parts, landed, core, name):
    n = len(names)

    def body(core_ref, *refs):
        for i in range(n):
            refs[2 * n + i][0] = (refs[i][...] + refs[n + i][0]).astype(MXU_DTYPE)

    in_specs, out_specs = [], []
    for nm in names:
        hr, hc = _HALF[nm]
        if nm in ("pa", "pb"):
            in_specs.append(pl.BlockSpec((hr // 2, hc), lambda k, s, cr: (cr[0] * 2 + s, k)))
        else:
            in_specs.append(pl.BlockSpec((hr // 2, hc), lambda k, s, cr: (k * 4 + cr[0] * 2 + s, 0)))
    for nm in names:
        hr, hc = _HALF[nm]
        in_specs.append(pl.BlockSpec((1, hr // 2, hc), lambda k, s, cr: (k, s, 0)))
        out_specs.append(pl.BlockSpec((1, hr // 2, hc), lambda k, s, cr: (k, s, 0)))
    return pl.pallas_call(
        body, name=name,
        grid_spec=pltpu.PrefetchScalarGridSpec(num_scalar_prefetch=1, grid=(N_CHIPS, 2), in_specs=in_specs, out_specs=out_specs),
        out_shape=[pltpu.HBM((N_CHIPS,) + _HALF[nm], MXU_DTYPE) for nm in names],
        compiler_params=_plain_params(2),
    )(core, *[_pin(a) for a in parts], *[_pin(a) for a in landed])


def add_small_half(small, landed, core):
    hs = SMALL_ROWS // 2

    def body(core_ref, a_ref, b_ref, o_ref):
        o_ref[...] = a_ref[...] + b_ref[...]

    return pl.pallas_call(
        body, name="add_small_half",
        grid_spec=pltpu.PrefetchScalarGridSpec(
            num_scalar_prefetch=1, grid=(1,),
            in_specs=[pl.BlockSpec((hs, 128), lambda s, cr: (cr[0], 0)), pl.BlockSpec((hs, 128), lambda s, cr: (0, 0))],
            out_specs=pl.BlockSpec((hs, 128), lambda s, cr: (0, 0))),
        out_shape=jax.ShapeDtypeStruct((hs, 128), F32),
    )(core, small, landed)


def sum_chips(sums, landed, chips, name, small_landed=None):
    n = len(sums)

    def body(chips_ref, *refs):
        ins, outs = refs[:4 * n + (small_landed is not None)], refs[4 * n + (small_landed is not None):]
        for i in range(n):
            tot = ins[4 * i][0].astype(F32)
            for r in range(1, N_CHIPS):
                tot = tot + ins[4 * i + r][0].astype(F32)
            outs[i][...] = tot
        if small_landed is not None:
            @pl.when(pl.program_id(0) == 0)
            def _():
                sm = ins[4 * n]
                outs[n][...] = ((sm[0] + sm[1]) + sm[2]) + sm[3]

    in_specs, out_specs, operands, shapes = [], [], [], []
    for a, l in zip(sums, landed):
        _, hr, hc = a.shape
        for r in range(N_CHIPS):
            in_specs.append(pl.BlockSpec((1, hr // 2, hc), functools.partial(lambda s, ch, r: (ch[r], s, 0), r=r)))
            operands.append(a if r == 0 else l)
        out_specs.append(pl.BlockSpec((hr // 2, hc), lambda s, ch: (s, 0)))
        shapes.append(jax.ShapeDtypeStruct((hr, hc), F32))
    if small_landed is not None:
        in_specs.append(pl.BlockSpec(small_landed.shape, lambda s, ch: (0, 0, 0)))
        out_specs.append(pl.BlockSpec(small_landed.shape[1:], lambda s, ch: (0, 0)))
        operands.append(small_landed)
        shapes.append(jax.ShapeDtypeStruct(small_landed.shape[1:], F32))
    return pl.pallas_call(
        body, name=name,
        grid_spec=pltpu.PrefetchScalarGridSpec(num_scalar_prefetch=1, grid=(2,), in_specs=in_specs, out_specs=out_specs),
        out_shape=[pltpu.HBM(s.shape, s.dtype) for s in shapes], compiler_params=_plain_params(1),
    )(chips, *[_pin(a) for a in operands])


N_STEPS = 8


def _adam_update(w, g, m, v):
    m = ADAM_B1 * m + (1.0 - ADAM_B1) * g
    v = ADAM_B2 * v + (1.0 - ADAM_B2) * (g * g)
    m_hat = m * (1.0 / (1.0 - ADAM_B1 ** ADAM_STEP))
    v_hat = v * (1.0 / (1.0 - ADAM_B2 ** ADAM_STEP))
    return -ADAM_LR * (m_hat / (jnp.sqrt(v_hat) + ADAM_EPS) + ADAM_WD * w), m, v


def adamw_big(ws, g_layers, ms, vs):
    n = len(ws)
    per_layer = N_STEPS // DEPTH

    def body(*refs):
        first = pl.program_id(0) < per_layer
        for i in range(n):
            w, m, v = (refs[k * n + i][...] for k in range(3))
            g = jnp.where(first, refs[3 * n + 2 * i][...], refs[3 * n + 2 * i + 1][...])[None]
            delta, m, v = _adam_update(w, g, m, v)
            for k, val in enumerate((g, delta, m, v)):
                refs[5 * n + k * n + i][...] = val

    whole = lambda a: pl.BlockSpec((1, a.shape[1] // per_layer, a.shape[2]), lambda s: (s // per_layer, s % per_layer, 0))
    layer0 = lambda a: pl.BlockSpec((a.shape[1] // per_layer, a.shape[2]), lambda s: (jnp.minimum(s, per_layer - 1), 0))
    layer1 = lambda a: pl.BlockSpec((a.shape[1] // per_layer, a.shape[2]), lambda s: (jnp.maximum(s - per_layer, 0), 0))
    g_specs, g_ops = [], []
    for a, (g0, g1) in zip(ws, g_layers):
        g_specs += [layer0(a), layer1(a)]
        g_ops += [g0, g1]
    outs = pl.pallas_call(
        body, name="adamw_big", grid=(N_STEPS,),
        in_specs=[whole(a) for a in ws] * 3 + g_specs, out_specs=[whole(a) for a in ws] * 4,
        out_shape=[jax.ShapeDtypeStruct(a.shape, F32) for a in ws] * 4,
        compiler_params=_plain_params(1),
    )(*ws, *ms, *vs, *g_ops)
    return outs[:n], outs[n:2 * n], outs[2 * n:3 * n], outs[3 * n:]


def adamw_small(ws, gs, ms, vs):
    n = len(ws)

    def body(*refs):
        for i in range(n):
            outs = _adam_update(*(refs[k * n + i][...] for k in range(4)))
            for k, val in enumerate(outs):
                refs[4 * n + k * n + i][...] = val

    specs = [_const_block(a.shape) for a in ws]
    outs = pl.pallas_call(
        body, name="adamw_small", grid=(1,), in_specs=specs * 4, out_specs=specs * 3,
        out_shape=[jax.ShapeDtypeStruct(a.shape, F32) for a in ws] * 3,
        compiler_params=_plain_params(1),
    )(*ws, *gs, *ms, *vs)
    return outs[:n], outs[n:2 * n], outs[2 * n:]


_SMALL = (("w_s", (DEPTH, SGU_G, BLK, BLK)), ("ln_in_g", (D_MODEL,)), ("ln_in_b", (D_MODEL,)), ("b_in", (DEPTH, N_COLS)),
          ("sinks", (DEPTH, 8)), ("vn_g", (DEPTH, SGU_W)), ("vn_b", (DEPTH, SGU_W)),
          ("b_s", (DEPTH, SGU_G, BLK)), ("b_out", (DEPTH, D_MODEL)), ("ln_g", (DEPTH, D_MODEL)), ("ln_b", (DEPTH, D_MODEL)))
_BIG = ("w_in", "p_a", "p_b", "w_out")


def _size(shape):
    n = 1
    for d in shape:
        n *= d
    return n


def _pack_small(vals, last_row=None):
    rows = []
    for name, shape in _SMALL:
        v = vals[name]
        if _size(shape) % 128:
            v = jnp.pad(v.reshape((1, -1)), ((0, 0), (0, (-_size(shape)) % 128)))
        rows.append(v.reshape((-1, 128)))
    used = sum(r.shape[0] for r in rows)
    tail = jnp.zeros((SMALL_ROWS - used, 128), F32)
    if last_row is not None:
        tail = tail.at[-1, 0].set(last_row)
    return jnp.concatenate([rows[0], jnp.concatenate(rows[1:] + [tail], axis=0)], axis=0)


def _unpack_small(packed):
    out, pos = {}, 0
    for name, shape in _SMALL:
        n = -(-_size(shape) // 128)
        rows = packed[pos:pos + n]
        out[name] = (rows.reshape((-1,))[:_size(shape)] if _size(shape) % 128 else rows).reshape(shape)
        pos += n
    return out


def kernel(x, ln_in_g, ln_in_b, w_in, b_in, sinks, vn_g, vn_b, w_s, b_s, p_a, p_b, w_out, b_out, ln_g, ln_b, loss_target, m_ln_in_g, m_ln_in_b, m_w_in, m_b_in, m_sinks, m_vn_g, m_vn_b, m_w_s, m_b_s, m_p_a, m_p_b, m_w_out, m_b_out, m_ln_g, m_ln_b, v_ln_in_g, v_ln_in_b, v_w_in, v_b_in, v_sinks, v_vn_g, v_vn_b, v_w_s, v_b_s, v_p_a, v_p_b, v_w_out, v_b_out, v_ln_g, v_ln_b):
    weights = dict(ln_in_g=ln_in_g, ln_in_b=ln_in_b, w_in=w_in, b_in=b_in, sinks=sinks, vn_g=vn_g, vn_b=vn_b, w_s=w_s,
                   b_s=b_s, p_a=p_a, p_b=p_b, w_out=w_out, b_out=b_out, ln_g=ln_g, ln_b=ln_b)
    mom1 = dict(ln_in_g=m_ln_in_g, ln_in_b=m_ln_in_b, w_in=m_w_in, b_in=m_b_in, sinks=m_sinks, vn_g=m_vn_g, vn_b=m_vn_b,
                w_s=m_w_s, b_s=m_b_s, p_a=m_p_a, p_b=m_p_b, w_out=m_w_out, b_out=m_b_out, ln_g=m_ln_g, ln_b=m_ln_b)
    mom2 = dict(ln_in_g=v_ln_in_g, ln_in_b=v_ln_in_b, w_in=v_w_in, b_in=v_b_in, sinks=v_sinks, vn_g=v_vn_g, vn_b=v_vn_b,
                w_s=v_w_s, b_s=v_b_s, p_a=v_p_a, p_b=v_p_b, w_out=v_w_out, b_out=v_b_out, ln_g=v_ln_g, ln_b=v_ln_b)
    n_seq, S, _ = x.shape
    T = n_seq * S
    c = lax.axis_index("c")
    j = 2 * lax.axis_index("x") + lax.axis_index("y")
    core = c.astype(jnp.int32).reshape((1,))
    chips = jnp.stack([j] + [r + (r >= j) for r in range(N_CHIPS - 1)]).astype(jnp.int32)
    names = list(_PIECES)
    xt, tt = x.reshape((T, D_MODEL)), loss_target.reshape((T, D_MODEL))

    tview = lambda d: dict(d, w_in=jnp.swapaxes(d["w_in"], 1, 2))
    weights_t, mom1_t, mom2_t = tview(weights), tview(mom1), tview(mom2)

    def own_shards(l):
        return [weights_t[n][l].astype(MXU_DTYPE) for n in _BIG]

    def layer_weights(l, gathered):
        d = dict(zip(_PIECES, gathered))
        d.update(paT=d["pa"].T, pbT=d["pb"].T, woutT=d["wout"].T)
        d.update(bin=b_in[l][None], sinks=sinks[l], vng=vn_g[l][None], vnb=vn_b[l][None], ws=w_s[l],
                 bs=jnp.repeat(b_s[l].T, 128, axis=1), bout=b_out[l][None], lng=ln_g[l][None], lnb=ln_b[l][None])
        return d

    own0, own1 = own_shards(0), own_shards(1)
    x0, g0a = input_norm_forward(xt, ln_in_g[None], ln_in_b[None], "ln_in_fwd", comm=gather_over_chips(names, own0))
    lw0 = layer_weights(0, _run_comm(gather_over_pair(names, g0a), "gather0_pair"))
    h0, g1a = project(x0, lw0["winT"], lw0["bin"], "layer0_proj", comm=gather_over_chips(names, own1))
    x1, g1b = layer_forward(x0, h0, lw0, S, "layer0_fwd", comm=gather_over_pair(names, g1a))
    lw1 = layer_weights(1, g1b)
    h1, _ = project(x1, lw1["winT"], lw1["bin"], "layer1_proj")

    dres1, dh1, acc1, _ = layer_backward(x1, h1, tt, lw1, S, True, "layer1_bwd")
    dx1, dwinT1, _ = input_grads(dres1, dh1, x1, lw1["winT"], "layer1_dx_dwin")
    parts1 = [dwinT1, acc1["dpa"], acc1["dpb"], acc1["dwout"]]
    dres0, dh0, acc0, landed1 = layer_backward(x0, h0, dx1, lw0, S, False, "layer0_bwd", comm=pair_send_halves(names, parts1))
    sums1 = add_own_halves(names, parts1, landed1, core, "add_own_halves1")
    dx0, dwinT0, from_chips1 = input_grads(dres0, dh0, x0, lw0["winT"], "layer0_dx_dwin", comm=chip_exchange(sums1))
    totals1 = sum_chips(sums1, from_chips1, chips, "sum_chips1")
    grad_x, d_ln_in_g, d_ln_in_b, _ = input_norm_backward(xt, dx0, ln_in_g, "ln_in_bwd")

    gl = [acc0, acc1]
    per_layer = lambda key, pick: jnp.stack([pick(gl[l][key]) for l in range(DEPTH)])
    row0 = lambda key: per_layer(key, lambda a: a[0])
    small_part = _pack_small(dict(
        ln_in_g=d_ln_in_g[0], ln_in_b=d_ln_in_b[0], b_in=row0("dbin"), sinks=per_layer("dsink", lambda a: a[:, 0]),
        vn_g=row0("dvng"), vn_b=row0("dvnb"), w_s=per_layer("dws", lambda a: a),
        b_s=per_layer("dbs", lambda a: a[:, ::128].T), b_out=row0("dbout"), ln_g=row0("dlng"), ln_b=row0("dlnb")),
        last_row=acc1["loss"][0, 0])
    parts0 = [dwinT0, acc0["dpa"], acc0["dpb"], acc0["dwout"]]
    first = _run_comm(pair_send_halves(names, parts0, small_part, extra=totals1), "pair_send_halves0")
    landed0, small_landed, sib_totals1 = first[:4], first[4], first[5:]
    sums0 = add_own_halves(names, parts0, landed0, core, "add_own_halves0")
    small_sum = add_small_half(small_part, small_landed, core)
    *from_chips0, small_from_chips = _run_comm(chip_exchange(sums0, small_sum), "chip_exchange0")
    *totals0, small_total = sum_chips(sums0, from_chips0, chips, "sum_chips0", small_landed=small_from_chips)
    *sib_totals0, sib_small = _run_comm(pair_send_totals(totals0 + [small_total]), "pair_send_totals0")

    def both_halves(mine, theirs):
        return jnp.where(c == 0, jnp.concatenate([mine, theirs], axis=0), jnp.concatenate([theirs, mine], axis=0))

    g_layers = [(both_halves(totals0[pi], sib_totals0[pi]), both_halves(totals1[pi], sib_totals1[pi]))
                for pi in range(len(_BIG))]
    g_small = both_halves(small_total, sib_small)
    loss = g_small[-1, 0]

    big = adamw_big([weights_t[n] for n in _BIG], g_layers, [mom1_t[n] for n in _BIG], [mom2_t[n] for n in _BIG])
    small_names = [n for n, _ in _SMALL]
    as2d = lambda a: a.reshape((1, -1)) if a.ndim == 1 else a
    g_named = _unpack_small(g_small)
    small = adamw_small(*[[as2d(d[n]) for n in small_names] for d in (weights, g_named, mom1, mom2)])

    def named(big_outs, small_outs):
        d = dict(zip(_BIG, big_outs))
        d["w_in"] = jnp.swapaxes(d["w_in"], 1, 2)
        d.update({n: o.reshape(weights[n].shape) for n, o in zip(small_names, small_outs)})
        return d

    order = ("ln_in_g", "ln_in_b", "w_in", "b_in", "sinks", "vn_g", "vn_b", "w_s", "b_s", "p_a", "p_b", "w_out", "b_out",
             "ln_g", "ln_b")
    res = [loss, grad_x.reshape(x.shape)]
    groups = [named(big[0], [g_named[n] for n in small_names])] + [named(big[k + 1], small[k]) for k in range(3)]
    for group in groups:
        res.extend(group[n] for n in order)
    return tuple(res)
```

```python
import functools

import jax
import jax.numpy as jnp
from jax import lax
from jax.experimental import pallas as pl
from jax.experimental.pallas import tpu as pltpu

F32 = jnp.float32
MXU_DTYPE = jnp.bfloat16

D_MODEL = 1024
DEPTH = 2
HEAD_DIM = 64
ATTN_W = 512
KV_W = 128
BLK = 128
SGU_W = 512
SGU_G = 4
N_COLS = 4864
C_Q, C_K, C_V, C_GA, C_UB, C_VB, C_GB, C_RA, C_RB = 0, 512, 640, 768, 1280, 1792, 2304, 2816, 3840
ALPHA = (2.0 * DEPTH) ** 0.25
LN_EPS = 1e-5
SCALE = HEAD_DIM ** -0.5
NEG = float(jnp.finfo(jnp.float32).min)
GELU_C = 0.7978845608028654
GELU_A = 0.044715

ADAM_LR, ADAM_B1, ADAM_B2, ADAM_EPS, ADAM_WD, ADAM_STEP = 0.001, 0.9, 0.999, 1e-08, 0.01, 10

V7X_VMEM_BYTES = 64 * 1024 * 1024
V7X_VMEM_RESERVE = 3 * 1024 * 1024
ROWS = 256
ROWS_BWD = 256
ROWS_PROJ = 512

MESH = pl.DeviceIdType.MESH
N_CHIPS = 4


def _mm(a, b):
    return jnp.dot(a.astype(MXU_DTYPE), b.astype(MXU_DTYPE), preferred_element_type=F32)


def _mm_nt(a, b):
    return lax.dot_general(a.astype(MXU_DTYPE), b.astype(MXU_DTYPE), (((1,), (1,)), ((), ())),
                           preferred_element_type=F32)


def _mm_tn(a, b):
    return lax.dot_general(a.astype(MXU_DTYPE), b.astype(MXU_DTYPE), (((0,), (0,)), ((), ())),
                           preferred_element_type=F32)


def _sigmoid(x):
    return 0.5 * jnp.tanh(0.5 * x) + 0.5


def _gelu_parts(x):
    x2 = x * x
    u = 0.5 * jnp.tanh(x * (GELU_C + (GELU_C * GELU_A) * x2)) + 0.5
    dg = u * (1.0 + x * (1.0 - u) * (2.0 * GELU_C + (6.0 * GELU_C * GELU_A) * x2))
    return x * u, dg


def _ln_fwd(z, g, b):
    mu = jnp.mean(z, axis=-1, keepdims=True)
    zc = z - mu
    var = jnp.mean(zc * zc, axis=-1, keepdims=True)
    rstd = lax.rsqrt(var + LN_EPS)
    zhat = zc * rstd
    return zhat * g + b, zhat, rstd


def _ln_bwd(dy, zhat, rstd, g):
    dzh = dy * g
    m1 = jnp.mean(dzh, axis=-1, keepdims=True)
    m2 = jnp.mean(dzh * zhat, axis=-1, keepdims=True)
    return rstd * (dzh - m1 - zhat * m2)


def _colsum(v):
    return jnp.sum(v, axis=0, keepdims=True)


def _lane_lo(rows):
    return lax.broadcasted_iota(jnp.int32, (rows, 128), 1) < HEAD_DIM


def _dup_head(x2, hk, lo):
    xr = pltpu.roll(x2, HEAD_DIM, 1)
    return jnp.where(lo, x2, xr) if hk == 0 else jnp.where(lo, xr, x2)


def _fold_head(y0, y1, lo):
    f0 = y0 + pltpu.roll(y0, HEAD_DIM, 1)
    f1 = y1 + pltpu.roll(y1, HEAD_DIM, 1)
    return jnp.where(lo, f0, f1)


def _stack_heads(t0, t1, lo):
    z = jnp.zeros_like(t0)
    return jnp.concatenate([jnp.where(lo, t0, z), jnp.where(lo, z, t0),
                            jnp.where(lo, t1, z), jnp.where(lo, z, t1)], axis=0)


def _unstack_heads(o, lo):
    return (jnp.where(lo, o[0:128], o[128:256]), jnp.where(lo, o[256:384], o[384:512]))


def _band_bias(start):
    row = lax.broadcasted_iota(jnp.int32, (4 * BLK, 2 * BLK), 0) & (BLK - 1)
    kpos = lax.broadcasted_iota(jnp.int32, (4 * BLK, 2 * BLK), 1)
    valid = (kpos > row) & (kpos <= row + BLK)
    if start is not None:
        valid = valid & (jnp.logical_not(start) | (kpos >= BLK))
    return jnp.where(valid, 0.0, NEG)


def _attn_probs(qs, kd, sink4, bias):
    s = _mm_nt(qs, kd) * SCALE + bias
    r1 = lax.broadcasted_iota(jnp.int32, (4 * BLK, 1), 0)
    sk = jnp.where(r1 < BLK, sink4[0], jnp.where(r1 < 2 * BLK, sink4[1], jnp.where(r1 < 3 * BLK, sink4[2], sink4[3])))
    m = jnp.maximum(jnp.max(s, axis=-1, keepdims=True), sk)
    p = jnp.exp(s - m)
    es = jnp.exp(sk - m)
    inv = 1.0 / (jnp.sum(p, axis=-1, keepdims=True) + es)
    return p * inv, es * inv


_PROJ_CHUNK = 512


def _kv_blocks(kvh, h_s, nb):
    ks = [kvh[:, 0:KV_W]] + [h_s[n * BLK:(n + 1) * BLK, C_K:C_K + KV_W] for n in range(nb)]
    vs = [kvh[:, KV_W:2 * KV_W]] + [h_s[n * BLK:(n + 1) * BLK, C_V:C_V + KV_W] for n in range(nb)]
    return ks, vs


def _attn_operands(h_s, ks, vs, n, hk, lo, lo2):
    rows = slice(n * BLK, (n + 1) * BLK)
    kd = _dup_head(jnp.concatenate([ks[n], ks[n + 1]], axis=0), hk, lo2)
    vd = _dup_head(jnp.concatenate([vs[n], vs[n + 1]], axis=0), hk, lo2)
    c0 = C_Q + 2 * hk * 128
    qs = _stack_heads(h_s[rows, c0:c0 + 128], h_s[rows, c0 + 128:c0 + 256], lo)
    return qs, kd, vd


def _sgu_weights(w):
    tri = (lax.broadcasted_iota(jnp.int32, (BLK, BLK), 0) >= lax.broadcasted_iota(jnp.int32, (BLK, BLK), 1))
    return tri, [jnp.where(tri, w["ws"][g], 0.0) for g in range(SGU_G)]


def _layer_forward(x, kvh, start, w, h_s, attn_s, mix_s, keep, tick=lambda: None):
    R = x.shape[0]
    nb = R // BLK
    lo = _lane_lo(BLK)
    lo2 = _lane_lo(2 * BLK)
    ks, vs = _kv_blocks(kvh, h_s, nb)
    sinks = [w["sinks"][j] for j in range(8)]
    bias_first = _band_bias(start)
    bias_rest = _band_bias(None) if nb > 1 else None
    probs_l = []
    for n in range(nb):
        rows = slice(n * BLK, (n + 1) * BLK)
        for hk in range(2):
            qs, kd, vd = _attn_operands(h_s, ks, vs, n, hk, lo, lo2)
            probs, ps = _attn_probs(qs, kd, sinks[4 * hk:4 * hk + 4], bias_first if n == 0 else bias_rest)
            t0, t1 = _unstack_heads(_mm(probs, vd), lo)
            attn_s[rows, 2 * hk * 128:(2 * hk + 1) * 128] = t0
            attn_s[rows, (2 * hk + 1) * 128:(2 * hk + 2) * 128] = t1
            if keep:
                probs_l.append((probs, ps))
            tick()
    ga = h_s[:, C_GA:C_GA + ATTN_W]
    ya = attn_s[...] * (ga * _sigmoid(ga))
    tick()
    gu, _ = _gelu_parts(h_s[:, C_UB:C_UB + SGU_W])
    tick()
    gv, _ = _gelu_parts(h_s[:, C_VB:C_VB + SGU_W])
    vn, _, _ = _ln_fwd(gv, w["vng"][...], w["vnb"][...])
    tick()
    _, wms = _sgu_weights(w)
    for n in range(nb):
        rows = slice(n * BLK, (n + 1) * BLK)
        for g in range(SGU_G):
            cols = slice(g * 128, (g + 1) * 128)
            mix_s[rows, cols] = _mm(wms[g], vn[rows, cols]) + w["bs"][:, cols]
    gb = h_s[:, C_GB:C_GB + SGU_W]
    yb = gu * mix_s[...] * (gb * _sigmoid(gb))
    tick()
    a = _mm(ya, w["pa"][...])
    b = _mm(yb, w["pb"][...])
    tick()
    merged = _sigmoid(h_s[:, C_RA:C_RA + D_MODEL]) * a + _sigmoid(h_s[:, C_RB:C_RB + D_MODEL]) * b
    tick()
    out = _mm(merged, w["wout"][...]) + w["bout"][...]
    y, zhat, rstd = _ln_fwd(ALPHA * x + out, w["lng"][...], w["lnb"][...])
    if not keep:
        return y, None
    return y, dict(ks=ks, vs=vs, probs=probs_l, ya=ya, yb=yb, a=a, b=b, merged=merged, zhat=zhat, rstd=rstd)


def _dsilu(g, sg):
    return sg * (1.0 + g * (1.0 - sg))


_W_NAMES = ("sinks", "vng", "vnb", "ws", "bs", "pa", "pb", "wout", "bout", "lng", "lnb")
_WT_NAMES = ("paT", "pbT", "woutT")
_ACC_NAMES = ("dwout", "dpa", "dpb", "dbin", "dbout", "dlng", "dlnb", "dvng", "dvnb", "dws", "dbs", "dsink", "loss")


def _make_fwd_body(R, S):
    def body(x_ref, h_ref, kvh_ref, *rest):
        w = dict(zip(_W_NAMES, rest[:len(_W_NAMES)]))
        y_ref, attn_s, mix_s = rest[len(_W_NAMES):]
        start = (pl.program_id(0) % (S // R)) == 0
        y, _ = _layer_forward(x_ref[...], kvh_ref[...], start, w, h_ref, attn_s, mix_s, keep=False)
        y_ref[...] = y
    return body


def _make_fwd_proj_body(R, S, nT):
    def body(x_ref, h_ref, kvh_ref, wn_ref, bn_ref, *rest):
        w = dict(zip(_W_NAMES, rest[:len(_W_NAMES)]))
        y_ref, hn_ref, attn_s, mix_s, yprev_s = rest[len(_W_NAMES):]

        @pl.when(pl.program_id(0) == 0)
        def _():
            yprev_s[...] = jnp.zeros(yprev_s.shape, yprev_s.dtype)

        t = jnp.minimum(pl.program_id(0), nT - 1)
        start = (t % (S // R)) == 0
        chunks = [(c0, min(c0 + _PROJ_CHUNK, N_COLS)) for c0 in range(0, N_COLS, _PROJ_CHUNK)]

        def tick():
            if chunks:
                c0, c1 = chunks.pop(0)
                hn_ref[:, c0:c1] = _mm_nt(yprev_s[...], wn_ref[c0:c1, :]) + bn_ref[:, c0:c1]

        y, _ = _layer_forward(x_ref[...], kvh_ref[...], start, w, h_ref, attn_s, mix_s, keep=False, tick=tick)
        while chunks:
            tick()
        y_ref[...] = y
        yprev_s[...] = y.astype(yprev_s.dtype)
    return body


def _make_bwd_body(R, S, nT, is_last):
    nb = R // BLK
    names = _W_NAMES + _WT_NAMES
    nw = len(names)

    def body(x_ref, h_s, kvh_ref, aux_ref, *rest):
        w = dict(zip(names, rest[:nw]))
        dres_ref, dh_ref = rest[nw:nw + 2]
        acc = dict(zip(_ACC_NAMES, rest[nw + 2:nw + 2 + len(_ACC_NAMES)]))
        attn_s, mix_s, ckv_s = rest[nw + 2 + len(_ACC_NAMES):]
        i = pl.program_id(0)
        start = ((nT - 1 - i) % (S // R)) == 0
        lo = _lane_lo(BLK)
        lo2 = _lane_lo(2 * BLK)

        @pl.when(i == 0)
        def _():
            for name in _ACC_NAMES:
                acc[name][...] = jnp.zeros(acc[name].shape, F32)
            ckv_s[...] = jnp.zeros(ckv_s.shape, F32)

        x = x_ref[...]
        y, sv = _layer_forward(x, kvh_ref[...], start, w, h_s, attn_s, mix_s, keep=True)
        if is_last:
            diff = y - aux_ref[...]
            part = 0.5 * jnp.sum(jnp.mean(diff * diff, axis=-1, keepdims=True), axis=0, keepdims=True)
            acc["loss"][...] += jnp.broadcast_to(part, acc["loss"].shape)
            dy = diff * (1.0 / D_MODEL)
        else:
            dy = aux_ref[...]

        def put(c0, val, rows=slice(None)):
            width = val.shape[1]
            dh_ref[rows, c0:c0 + width] = val.astype(dh_ref.dtype)
            acc["dbin"][:, c0:c0 + width] += _colsum(val)

        acc["dlng"][...] += _colsum(dy * sv["zhat"])
        acc["dlnb"][...] += _colsum(dy)
        dz = _ln_bwd(dy, sv["zhat"], sv["rstd"], w["lng"][...])
        dres_ref[...] = ALPHA * dz
        acc["dbout"][...] += _colsum(dz)
        acc["dwout"][...] += _mm_tn(sv["merged"], dz)
        dmerged = _mm(dz, w["woutT"][...])
        sa = _sigmoid(h_s[:, C_RA:C_RA + D_MODEL])
        da = dmerged * sa
        put(C_RA, da * sv["a"] * (1.0 - sa))
        sb = _sigmoid(h_s[:, C_RB:C_RB + D_MODEL])
        db = dmerged * sb
        put(C_RB, db * sv["b"] * (1.0 - sb))
        acc["dpa"][...] += _mm_tn(sv["ya"], da)
        acc["dpb"][...] += _mm_tn(sv["yb"], db)
        dya = _mm(da, w["paT"][...])
        dyb = _mm(db, w["pbT"][...])
        ga = h_s[:, C_GA:C_GA + ATTN_W]
        sga = _sigmoid(ga)
        put(C_GA, dya * attn_s[...] * _dsilu(ga, sga))
        attn_s[...] = dya * (ga * sga)
        ks, vs = sv["ks"], sv["vs"]
        dks = [jnp.zeros((BLK, KV_W), F32) for _ in range(nb + 1)]
        dvs = [jnp.zeros((BLK, KV_W), F32) for _ in range(nb + 1)]
        for n in range(nb):
            rows = slice(n * BLK, (n + 1) * BLK)
            ydk, ydv = [], []
            for hk in range(2):
                qs, kd, vd = _attn_operands(h_s, ks, vs, n, hk, lo, lo2)
                probs, ps = sv["probs"][2 * n + hk]
                dos = _stack_heads(attn_s[rows, 2 * hk * 128:(2 * hk + 1) * 128],
                                   attn_s[rows, (2 * hk + 1) * 128:(2 * hk + 2) * 128], lo)
                dp = _mm_nt(dos, vd)
                delta = jnp.sum(probs * dp, axis=-1, keepdims=True)
                dsk = -(ps * delta)
                for g in range(4):
                    j = 4 * hk + g
                    acc["dsink"][j:j + 1, :] += jnp.broadcast_to(_colsum(dsk[g * BLK:(g + 1) * BLK]), (1, 128))
                dss = probs * (dp - delta) * SCALE
                q0, q1 = _unstack_heads(_mm(dss, kd), lo)
                put(C_Q + 2 * hk * 128, q0, rows)
                put(C_Q + (2 * hk + 1) * 128, q1, rows)
                ydk.append(_mm_tn(dss, qs))
                ydv.append(_mm_tn(probs, dos))
            dk2 = _fold_head(ydk[0], ydk[1], lo2)
            dv2 = _fold_head(ydv[0], ydv[1], lo2)
            dks[n] = dks[n] + dk2[0:BLK]
            dks[n + 1] = dks[n + 1] + dk2[BLK:2 * BLK]
            dvs[n] = dvs[n] + dv2[0:BLK]
            dvs[n + 1] = dvs[n + 1] + dv2[BLK:2 * BLK]
        dks[nb] = dks[nb] + ckv_s[:, 0:KV_W]
        dvs[nb] = dvs[nb] + ckv_s[:, KV_W:2 * KV_W]
        ckv_s[:, 0:KV_W] = dks[0]
        ckv_s[:, KV_W:2 * KV_W] = dvs[0]
        put(C_K, jnp.concatenate(dks[1:], axis=0))
        put(C_V, jnp.concatenate(dvs[1:], axis=0))
        gb = h_s[:, C_GB:C_GB + SGU_W]
        sgb = _sigmoid(gb)
        gu, dgu = _gelu_parts(h_s[:, C_UB:C_UB + SGU_W])
        mixed = mix_s[...]
        put(C_GB, dyb * (gu * mixed) * _dsilu(gb, sgb))
        dsgu = dyb * (gb * sgb)
        put(C_UB, dsgu * mixed * dgu)
        dmixed = dsgu * gu
        gv, dgv = _gelu_parts(h_s[:, C_VB:C_VB + SGU_W])
        vn, vhat, vrstd = _ln_fwd(gv, w["vng"][...], w["vnb"][...])
        tri, wms = _sgu_weights(w)
        dbs_part = jnp.zeros((BLK, SGU_W), F32)
        for n in range(nb):
            rows = slice(n * BLK, (n + 1) * BLK)
            dbs_part = dbs_part + dmixed[rows]
            for g in range(SGU_G):
                cols = slice(g * 128, (g + 1) * 128)
                acc["dws"][g] += jnp.where(tri, _mm_nt(dmixed[rows, cols], vn[rows, cols]), 0.0)
                mix_s[rows, cols] = _mm_tn(wms[g], dmixed[rows, cols])
        acc["dbs"][...] += dbs_part
        dvn = mix_s[...]
        acc["dvng"][...] += _colsum(dvn * vhat)
        acc["dvnb"][...] += _colsum(dvn)
        put(C_VB, _ln_bwd(dvn, vhat, vrstd, w["vng"][...]) * dgv)

        @pl.when(i == nT - 1)
        def _():
            for g in range(SGU_G):
                cols = slice(g * 128, (g + 1) * 128)
                tot = jnp.sum(acc["dbs"][:, cols], axis=1, keepdims=True)
                acc["dbs"][:, cols] = jnp.broadcast_to(tot, (BLK, 128))

    return body


SUB_ROWS = BLK
_KEEP = (("mg", D_MODEL), ("dz", D_MODEL), ("ya", ATTN_W), ("da", D_MODEL), ("yb", SGU_W), ("db", D_MODEL))


def _make_bwd_body2(R, S, nT, is_last):
    names = _W_NAMES + _WT_NAMES
    nw = len(names)
    n_sub = R // SUB_ROWS

    def one_block(x, kvh, aux, start, w, h_v, attn_v, mix_v, dres_v, dh_v, acc, carry, keep):
        lo = _lane_lo(BLK)
        lo2 = _lane_lo(2 * BLK)
        y, sv = _layer_forward(x, kvh, start, w, h_v, attn_v, mix_v, keep=True)
        if is_last:
            diff = y - aux
            part = 0.5 * jnp.sum(jnp.mean(diff * diff, axis=-1, keepdims=True), axis=0, keepdims=True)
            acc["loss"][...] += jnp.broadcast_to(part, acc["loss"].shape)
            dy = diff * (1.0 / D_MODEL)
        else:
            dy = aux

        def put(c0, val):
            width = val.shape[1]
            dh_v[:, c0:c0 + width] = val.astype(dh_v.dtype)
            acc["dbin"][:, c0:c0 + width] += _colsum(val)

        acc["dlng"][...] += _colsum(dy * sv["zhat"])
        acc["dlnb"][...] += _colsum(dy)
        dz = _ln_bwd(dy, sv["zhat"], sv["rstd"], w["lng"][...])
        dres_v[...] = ALPHA * dz
        acc["dbout"][...] += _colsum(dz)
        keep["mg"][...] = sv["merged"].astype(MXU_DTYPE)
        keep["dz"][...] = dz.astype(MXU_DTYPE)
        dmerged = _mm(dz, w["woutT"][...])
        sa = _sigmoid(h_v[:, C_RA:C_RA + D_MODEL])
        da = dmerged * sa
        put(C_RA, da * sv["a"] * (1.0 - sa))
        sb = _sigmoid(h_v[:, C_RB:C_RB + D_MODEL])
        db = dmerged * sb
        put(C_RB, db * sv["b"] * (1.0 - sb))
        keep["ya"][...] = sv["ya"].astype(MXU_DTYPE)
        keep["yb"][...] = sv["yb"].astype(MXU_DTYPE)
        keep["da"][...] = da.astype(MXU_DTYPE)
        keep["db"][...] = db.astype(MXU_DTYPE)
        dya = _mm(da, w["paT"][...])
        dyb = _mm(db, w["pbT"][...])
        ga = h_v[:, C_GA:C_GA + ATTN_W]
        sga = _sigmoid(ga)
        put(C_GA, dya * attn_v[...] * _dsilu(ga, sga))
        attn_v[...] = dya * (ga * sga)
        ks, vs = sv["ks"], sv["vs"]
        ydk, ydv = [], []
        for hk in range(2):
            qs, kd, vd = _attn_operands(h_v, ks, vs, 0, hk, lo, lo2)
            probs, ps = sv["probs"][hk]
            dos = _stack_heads(attn_v[:, 2 * hk * 128:(2 * hk + 1) * 128],
                               attn_v[:, (2 * hk + 1) * 128:(2 * hk + 2) * 128], lo)
            dp = _mm_nt(dos, vd)
            delta = jnp.sum(probs * dp, axis=-1, keepdims=True)
            dsk = -(ps * delta)
            for g in range(4):
                j = 4 * hk + g
                acc["dsink"][j:j + 1, :] += jnp.broadcast_to(_colsum(dsk[g * BLK:(g + 1) * BLK]), (1, 128))
            dss = probs * (dp - delta) * SCALE
            q0, q1 = _unstack_heads(_mm(dss, kd), lo)
            put(C_Q + 2 * hk * 128, q0)
            put(C_Q + (2 * hk + 1) * 128, q1)
            ydk.append(_mm_tn(dss, qs))
            ydv.append(_mm_tn(probs, dos))
        dk2 = _fold_head(ydk[0], ydk[1], lo2)
        dv2 = _fold_head(ydv[0], ydv[1], lo2)
        put(C_K, dk2[BLK:2 * BLK] + carry[0])
        put(C_V, dv2[BLK:2 * BLK] + carry[1])
        gb = h_v[:, C_GB:C_GB + SGU_W]
        sgb = _sigmoid(gb)
        gu, dgu = _gelu_parts(h_v[:, C_UB:C_UB + SGU_W])
        mixed = mix_v[...]
        put(C_GB, dyb * (gu * mixed) * _dsilu(gb, sgb))
        dsgu = dyb * (gb * sgb)
        put(C_UB, dsgu * mixed * dgu)
        dmixed = dsgu * gu
        gv, dgv = _gelu_parts(h_v[:, C_VB:C_VB + SGU_W])
        vn, vhat, vrstd = _ln_fwd(gv, w["vng"][...], w["vnb"][...])
        tri, wms = _sgu_weights(w)
        for g in range(SGU_G):
            cols = slice(g * 128, (g + 1) * 128)
            acc["dws"][g] += jnp.where(tri, _mm_nt(dmixed[:, cols], vn[:, cols]), 0.0)
            mix_v[:, cols] = _mm_tn(wms[g], dmixed[:, cols])
        acc["dbs"][...] += dmixed
        dvn = mix_v[...]
        acc["dvng"][...] += _colsum(dvn * vhat)
        acc["dvnb"][...] += _colsum(dvn)
        put(C_VB, _ln_bwd(dvn, vhat, vrstd, w["vng"][...]) * dgv)
        return dk2[0:BLK], dv2[0:BLK]

    def body(x_ref, h_s, kvh_ref, aux_ref, *rest):
        w = dict(zip(names, rest[:nw]))
        dres_ref, dh_ref = rest[nw:nw + 2]
        acc = dict(zip(_ACC_NAMES, rest[nw + 2:nw + 2 + len(_ACC_NAMES)]))
        scratch = rest[nw + 2 + len(_ACC_NAMES):]
        attn_s, mix_s, ckv_s = scratch[:3]
        keep_s = dict(zip([k for k, _ in _KEEP], scratch[3:]))
        i = pl.program_id(0)
        start = ((nT - 1 - i) % (S // R)) == 0

        @pl.when(i == 0)
        def _():
            for name in _ACC_NAMES:
                acc[name][...] = jnp.zeros(acc[name].shape, F32)
            ckv_s[...] = jnp.zeros(ckv_s.shape, F32)

        carry = (ckv_s[:, 0:KV_W], ckv_s[:, KV_W:2 * KV_W])
        for sub in reversed(range(n_sub)):
            rows = pl.ds(sub * SUB_ROWS, SUB_ROWS)
            view = lambda ref: ref.at[rows, :]
            kvh = kvh_ref[...] if sub == 0 else h_s[(sub - 1) * SUB_ROWS:sub * SUB_ROWS, C_K:C_K + 2 * KV_W]
            carry = one_block(x_ref[rows, :], kvh, aux_ref[rows, :], start if sub == 0 else None, w, view(h_s),
                              view(attn_s), view(mix_s), view(dres_ref), view(dh_ref), acc, carry,
                              {k: view(r) for k, r in keep_s.items()})
        ckv_s[:, 0:KV_W] = carry[0]
        ckv_s[:, KV_W:2 * KV_W] = carry[1]
        acc["dwout"][...] += _mm_tn(keep_s["mg"][...], keep_s["dz"][...])
        acc["dpa"][...] += _mm_tn(keep_s["ya"][...], keep_s["da"][...])
        acc["dpb"][...] += _mm_tn(keep_s["yb"][...], keep_s["db"][...])

        @pl.when(i == nT - 1)
        def _():
            for g in range(SGU_G):
                cols = slice(g * 128, (g + 1) * 128)
                tot = jnp.sum(acc["dbs"][:, cols], axis=1, keepdims=True)
                acc["dbs"][:, cols] = jnp.broadcast_to(tot, (BLK, 128))

    return body


def _resident():
    return pl.BlockSpec(memory_space=pltpu.VMEM)


_BIG_RESIDENT = ("pa", "pb", "wout", "paT", "pbT", "woutT", "dwout", "dpa", "dpb")


def _const_block(shape):
    return pl.BlockSpec(tuple(shape), lambda *_: (0,) * len(shape))


def _operand_spec(name, shape):
    if name == "sinks":
        return pl.BlockSpec(memory_space=pltpu.SMEM)
    return _resident() if name in _BIG_RESIDENT else _const_block(shape)


def _weight_specs(names, wts):
    return [_operand_spec(n, wts[n].shape) for n in names]


def _nbytes(a):
    n = jnp.dtype(a.dtype).itemsize
    for d in a.shape:
        n *= d
    return n


def _layer_params(resident=()):
    scoped = V7X_VMEM_BYTES - V7X_VMEM_RESERVE - sum(_nbytes(a) for a in resident)
    return pltpu.CompilerParams(dimension_semantics=("arbitrary",), vmem_limit_bytes=scoped)


class _Comm:
    def __init__(self, ins, out_shapes, sems, start, finish, aliases=None):
        self.ins = [pltpu.with_memory_space_constraint(a, pltpu.HBM) for a in ins]
        self.out_shapes = [pltpu.HBM(s.shape, s.dtype) for s in out_shapes]
        self.sems = list(sems)
        self.start, self.finish, self.aliases = start, finish, dict(aliases or {})


_ANY = pl.BlockSpec(memory_space=pltpu.HBM)
_DMA = pltpu.SemaphoreType.DMA


def _call(body, *, name, grid, in_specs, out_specs, out_shape, scratch_shapes, compiler_params, operands, comm=None):
    if comm is None:
        res = pl.pallas_call(body, name=name, grid=grid, in_specs=in_specs, out_specs=out_specs, out_shape=out_shape,
                             scratch_shapes=scratch_shapes, compiler_params=compiler_params)(*operands)
        return list(res), []
    n_in, n_out, n_scr = len(in_specs), len(out_specs), len(scratch_shapes)
    ci, co = len(comm.ins), len(comm.out_shapes)
    last = grid[0] - 1

    def hosted(*refs):
        ins, cin = refs[:n_in], refs[n_in:n_in + ci]
        p = n_in + ci
        outs, cout = refs[p:p + n_out], refs[p + n_out:p + n_out + co]
        p += n_out + co
        scr, sems = refs[p:p + n_scr], refs[p + n_scr:]
        pl.when(pl.program_id(0) == 0)(lambda: comm.start(cin, cout, sems))
        body(*ins, *outs, *scr)
        pl.when(pl.program_id(0) == last)(lambda: comm.finish(cin, cout, sems))

    res = pl.pallas_call(
        hosted, name=name, grid=grid, in_specs=list(in_specs) + [_ANY] * ci, out_specs=list(out_specs) + [_ANY] * co,
        out_shape=list(out_shape) + comm.out_shapes, scratch_shapes=list(scratch_shapes) + comm.sems,
        input_output_aliases={n_in + a: n_out + b for a, b in comm.aliases.items()},
        compiler_params=compiler_params)(*operands, *comm.ins)
    return list(res[:n_out]), list(res[n_out:])


def _run_comm(comm, name):
    ci, co = len(comm.ins), len(comm.out_shapes)

    def body(*refs):
        cin, cout, sems = refs[:ci], refs[ci:ci + co], refs[ci + co:]
        comm.start(cin, cout, sems)
        comm.finish(cin, cout, sems)

    return list(pl.pallas_call(body, name=name, in_specs=[_ANY] * ci, out_specs=[_ANY] * co, out_shape=comm.out_shapes,
                               scratch_shapes=comm.sems, input_output_aliases=comm.aliases)(*comm.ins))


def project(x, winT, b, name, comm=None):
    T = x.shape[0]
    R = ROWS_PROJ

    def body(x_ref, w_ref, b_ref, h_ref):
        xb = x_ref[...].astype(MXU_DTYPE)
        for c0 in range(0, N_COLS, _PROJ_CHUNK):
            c1 = min(c0 + _PROJ_CHUNK, N_COLS)
            h_ref[:, c0:c1] = _mm_nt(xb, w_ref[c0:c1, :]) + b_ref[:, c0:c1]

    (h,), extra = _call(
        body, name=name, grid=(T // R,),
        in_specs=[pl.BlockSpec((R, D_MODEL), lambda t: (t, 0)), _resident(), _const_block(b.shape)],
        out_specs=[pl.BlockSpec((R, N_COLS), lambda t: (t, 0))],
        out_shape=[jax.ShapeDtypeStruct((T, N_COLS), F32)], scratch_shapes=[],
        compiler_params=_layer_params([winT]), operands=(x, winT, b), comm=comm)
    return h, extra


_KV_BLOCK = C_K // (2 * KV_W)


def layer_forward(x, h, wts, S, name, comm=None):
    T = x.shape[0]
    R = ROWS
    npt = R // BLK
    (y,), extra = _call(
        _make_fwd_body(R, S), name=name, grid=(T // R,),
        in_specs=[pl.BlockSpec((R, D_MODEL), lambda t: (t, 0)), pl.BlockSpec((R, N_COLS), lambda t: (t, 0)),
                  pl.BlockSpec((BLK, 2 * KV_W), lambda t: (jnp.maximum(t * npt - 1, 0), _KV_BLOCK))]
        + _weight_specs(_W_NAMES, wts),
        out_specs=[pl.BlockSpec((R, D_MODEL), lambda t: (t, 0))],
        out_shape=[jax.ShapeDtypeStruct((T, D_MODEL), F32)],
        scratch_shapes=[pltpu.VMEM((R, ATTN_W), F32), pltpu.VMEM((R, SGU_W), F32)],
        compiler_params=_layer_params([wts[n] for n in _W_NAMES if n in _BIG_RESIDENT]),
        operands=(x, h, h, *[wts[n] for n in _W_NAMES]), comm=comm)
    return y, extra


def layer_forward_and_projection(x, h, wts, winT_next, b_next, S, name):
    T = x.shape[0]
    R = ROWS
    nT = T // R
    npt = R // BLK
    cur = lambda t: jnp.minimum(t, nT - 1)
    (y, hn), _ = _call(
        _make_fwd_proj_body(R, S, nT), name=name, grid=(nT + 1,),
        in_specs=[pl.BlockSpec((R, D_MODEL), lambda t: (cur(t), 0)), pl.BlockSpec((R, N_COLS), lambda t: (cur(t), 0)),
                  pl.BlockSpec((BLK, 2 * KV_W), lambda t: (jnp.maximum(cur(t) * npt - 1, 0), _KV_BLOCK)),
                  _resident(), _const_block(b_next.shape)] + _weight_specs(_W_NAMES, wts),
        out_specs=[pl.BlockSpec((R, D_MODEL), lambda t: (cur(t), 0)),
                   pl.BlockSpec((R, N_COLS), lambda t: (jnp.maximum(t - 1, 0), 0))],
        out_shape=[jax.ShapeDtypeStruct((T, D_MODEL), F32), jax.ShapeDtypeStruct((T, N_COLS), F32)],
        scratch_shapes=[pltpu.VMEM((R, ATTN_W), F32), pltpu.VMEM((R, SGU_W), F32), pltpu.VMEM((R, D_MODEL), MXU_DTYPE)],
        compiler_params=_layer_params([winT_next] + [wts[n] for n in _W_NAMES if n in _BIG_RESIDENT]),
        operands=(x, h, h, winT_next, b_next, *[wts[n] for n in _W_NAMES]))
    return y, hn


_ACC_SHAPES = dict(dwout=(D_MODEL, D_MODEL), dpa=(ATTN_W, D_MODEL), dpb=(SGU_W, D_MODEL), dbin=(1, N_COLS),
                   dbout=(1, D_MODEL), dlng=(1, D_MODEL), dlnb=(1, D_MODEL), dvng=(1, SGU_W), dvnb=(1, SGU_W),
                   dws=(SGU_G, BLK, BLK), dbs=(BLK, SGU_W), dsink=(8, 128), loss=(8, 128))


def layer_backward(x, h, aux, wts, S, is_last, name, comm=None):
    T = x.shape[0]
    R = ROWS_BWD
    nT = T // R
    npt = R // BLK
    rev = lambda i: (nT - 1 - i, 0)
    halo = lambda i: (jnp.maximum((nT - 1 - i) * npt - 1, 0), _KV_BLOCK)
    names = _W_NAMES + _WT_NAMES
    out_shape = ([jax.ShapeDtypeStruct((T, D_MODEL), F32), jax.ShapeDtypeStruct((T, N_COLS), MXU_DTYPE)]
                 + [jax.ShapeDtypeStruct(_ACC_SHAPES[n], F32) for n in _ACC_NAMES])
    outs, extra = _call(
        _make_bwd_body(R, S, nT, is_last), name=name, grid=(nT,),
        in_specs=[pl.BlockSpec((R, D_MODEL), rev), pl.BlockSpec((R, N_COLS), rev), pl.BlockSpec((BLK, 2 * KV_W), halo),
                  pl.BlockSpec((R, D_MODEL), rev)] + _weight_specs(names, wts),
        out_specs=[pl.BlockSpec((R, D_MODEL), rev), pl.BlockSpec((R, N_COLS), rev)]
        + [_operand_spec(n, _ACC_SHAPES[n]) for n in _ACC_NAMES],
        out_shape=out_shape,
        scratch_shapes=[pltpu.VMEM((R, ATTN_W), F32), pltpu.VMEM((R, SGU_W), F32), pltpu.VMEM((BLK, 2 * KV_W), F32)],
        compiler_params=_layer_params([wts[n] for n in names if n in _BIG_RESIDENT]
                                      + [jax.ShapeDtypeStruct(_ACC_SHAPES[n], F32) for n in _ACC_NAMES if n in _BIG_RESIDENT]),
        operands=(x, h, h, aux, *[wts[n] for n in names]), comm=comm)
    return outs[0], outs[1], dict(zip(_ACC_NAMES, outs[2:])), extra


def input_grads(dres, dh, x, winT, name, comm=None):
    T = x.shape[0]
    R = ROWS

    def body(dres_ref, dh_ref, x_ref, winT_ref, dx_ref, dwin_ref):
        @pl.when(pl.program_id(0) == 0)
        def _():
            dwin_ref[...] = jnp.zeros(dwin_ref.shape, F32)

        dh = dh_ref[...]
        dx_ref[...] = dres_ref[...] + jnp.dot(dh, winT_ref[...], preferred_element_type=F32)
        dwin_ref[...] += _mm_tn(dh, x_ref[...])

    row = lambda t: (t, 0)
    dwin_shape = jax.ShapeDtypeStruct((N_COLS, D_MODEL), F32)
    (dx, dwinT), extra = _call(
        body, name=name, grid=(T // R,),
        in_specs=[pl.BlockSpec((R, D_MODEL), row), pl.BlockSpec((R, N_COLS), row), pl.BlockSpec((R, D_MODEL), row),
                  _resident()],
        out_specs=[pl.BlockSpec((R, D_MODEL), row), _resident()],
        out_shape=[jax.ShapeDtypeStruct((T, D_MODEL), F32), dwin_shape], scratch_shapes=[],
        compiler_params=_layer_params([winT, dwin_shape]), operands=(dres, dh, x, winT), comm=comm)
    return dx, dwinT, extra


def input_norm_forward(x, g, b, name, comm=None):
    T = x.shape[0]
    R = ROWS

    def body(x_ref, g_ref, b_ref, y_ref):
        y_ref[...] = _ln_fwd(x_ref[...], g_ref[...], b_ref[...])[0]

    row = lambda t: (t, 0)
    (y,), extra = _call(
        body, name=name, grid=(T // R,),
        in_specs=[pl.BlockSpec((R, D_MODEL), row), _const_block(g.shape), _const_block(b.shape)],
        out_specs=[pl.BlockSpec((R, D_MODEL), row)],
        out_shape=[jax.ShapeDtypeStruct((T, D_MODEL), F32)], scratch_shapes=[],
        compiler_params=_layer_params(), operands=(x, g, b), comm=comm)
    return y, extra


def input_norm_backward(x, dy, g, name, comm=None):
    T = x.shape[0]
    R = ROWS

    def body(x_ref, dy_ref, g_ref, dx_ref, dg_ref, db_ref):
        @pl.when(pl.program_id(0) == 0)
        def _():
            dg_ref[...] = jnp.zeros(dg_ref.shape, F32)
            db_ref[...] = jnp.zeros(db_ref.shape, F32)

        dy = dy_ref[...]
        _, xhat, rstd = _ln_fwd(x_ref[...], g_ref[...], g_ref[...])
        dx_ref[...] = _ln_bwd(dy, xhat, rstd, g_ref[...])
        dg_ref[...] += _colsum(dy * xhat)
        db_ref[...] += _colsum(dy)

    row = lambda t: (t, 0)
    vec = jax.ShapeDtypeStruct((1, D_MODEL), F32)
    (dx, dg, db), extra = _call(
        body, name=name, grid=(T // R,),
        in_specs=[pl.BlockSpec((R, D_MODEL), row), pl.BlockSpec((R, D_MODEL), row), _const_block(g.shape)],
        out_specs=[pl.BlockSpec((R, D_MODEL), row), _const_block(vec.shape), _const_block(vec.shape)],
        out_shape=[jax.ShapeDtypeStruct((T, D_MODEL), F32), vec, vec], scratch_shapes=[],
        compiler_params=_layer_params(), operands=(x, dy, g), comm=comm)
    return dx, dg, db, extra


_PIECES = ("winT", "pa", "pb", "wout")
_WHOLE = dict(winT=(N_COLS, D_MODEL), pa=(ATTN_W, D_MODEL), pb=(SGU_W, D_MODEL), wout=(D_MODEL, D_MODEL))
_HALF = dict(winT=(N_COLS // 8, D_MODEL), pa=(ATTN_W // 2, D_MODEL // 4), pb=(SGU_W // 2, D_MODEL // 4),
             wout=(D_MODEL // 8, D_MODEL))
SMALL_ROWS = 1280


def _region(name, ref, k, h):
    hr, hc = _HALF[name]
    if name in ("pa", "pb"):
        return ref.at[pl.ds(h * hr, hr), pl.ds(k * hc, hc)]
    return ref.at[pl.ds(k * 2 * hr + h * hr, hr), :]


def _place():
    x, y, c = lax.axis_index("x"), lax.axis_index("y"), lax.axis_index("c")
    return x, y, c, 2 * x + y


def _as_chip(j, fn):
    for jj in range(N_CHIPS):
        pl.when(j == jj)(functools.partial(fn, jj))


def _remote(src, dst, send_sem, recv_sem, to):
    return pltpu.make_async_remote_copy(src_ref=src, dst_ref=dst, send_sem=send_sem, recv_sem=recv_sem,
                                        device_id=to, device_id_type=MESH)


def _core_of(k, c):
    return (k // 2, k % 2, c)


def _others(jj):
    return [k for k in range(N_CHIPS) if k != jj]


def _start_all(cps):
    for cp in cps:
        cp.start()


def _wait_all(cps):
    for cp in cps:
        cp.wait()


def gather_over_chips(names, shards):
    n = len(names)

    def own(cin, cout, sems, jj):
        x, y, c, _ = _place()
        cps = []
        for i, name in enumerate(names):
            hr, hc = _HALF[name]
            place = (cout[i].at[:, pl.ds(jj * hc, hc)] if name in ("pa", "pb")
                     else cout[i].at[pl.ds(jj * 2 * hr, 2 * hr), :])
            cps.append(_remote(cin[i], place, sems[2].at[i], sems[3].at[i], (x, y, 1 - c)))
        return cps

    def copies(cin, cout, sems, jj, c):
        out = []
        for i, name in enumerate(names):
            hr = _HALF[name][0]
            for k in _others(jj):
                out.append(_remote(cin[i].at[pl.ds(c * hr, hr)], _region(name, cout[i], jj, c),
                                   sems[0].at[4 * i + k], sems[1].at[4 * i + jj], _core_of(k, c)))
        return out

    def start(cin, cout, sems):
        _, _, c, j = _place()
        _as_chip(j, lambda jj: _start_all(copies(cin, cout, sems, jj, c) + own(cin, cout, sems, jj)))

    def finish(cin, cout, sems):
        _, _, c, j = _place()

        def run(jj):
            for i, name in enumerate(names):
                for k in _others(jj):
                    land = _region(name, cout[i], k, c)
                    _remote(land, land, sems[0].at[4 * i + k], sems[1].at[4 * i + k], _core_of(k, c)).wait_recv()
            for cp in copies(cin, cout, sems, jj, c):
                cp.wait_send()
            _wait_all(own(cin, cout, sems, jj))

        _as_chip(j, run)

    shapes = [jax.ShapeDtypeStruct(_WHOLE[nm], s.dtype) for nm, s in zip(names, shards)]
    return _Comm(shards, shapes, [_DMA((4 * n,)), _DMA((4 * n,)), _DMA((n,)), _DMA((n,))], start, finish)


def gather_over_pair(names, wholes):
    n = len(names)

    def start(cin, cout, sems):
        x, y, c, j = _place()

        def run(jj):
            for i, name in enumerate(names):
                for k in _others(jj):
                    land = _region(name, cout[i], k, c)
                    _remote(land, land, sems[0].at[4 * i + k], sems[1].at[4 * i + k], (x, y, 1 - c)).start()

        _as_chip(j, run)

    def finish(cin, cout, sems):
        x, y, c, j = _place()

        def run(jj):
            for i, name in enumerate(names):
                for k in _others(jj):
                    theirs = _region(name, cout[i], k, 1 - c)
                    _remote(theirs, theirs, sems[0].at[4 * i + k], sems[1].at[4 * i + k], (x, y, 1 - c)).wait_recv()
            for i, name in enumerate(names):
                for k in _others(jj):
                    land = _region(name, cout[i], k, c)
                    _remote(land, land, sems[0].at[4 * i + k], sems[1].at[4 * i + k], (x, y, 1 - c)).wait_send()

        _as_chip(j, run)

    shapes = [jax.ShapeDtypeStruct(a.shape, a.dtype) for a in wholes]
    return _Comm(wholes, shapes, [_DMA((4 * n,)), _DMA((4 * n,))], start, finish, aliases={i: i for i in range(n)})


def pair_send_halves(names, parts, small=None, extra=()):
    n = len(names)
    hs = SMALL_ROWS // 2
    first_extra = n + (small is not None)

    def copies(cin, cout, sems):
        x, y, c, _ = _place()
        sib = (x, y, 1 - c)
        cps = []
        for i, name in enumerate(names):
            for k in range(N_CHIPS):
                cps.append(_remote(_region(name, cin[i], k, 1 - c), cout[i].at[k], sems[0].at[4 * i + k],
                                   sems[1].at[4 * i + k], sib))
        if small is not None:
            cps.append(_remote(cin[n].at[pl.ds((1 - c) * hs, hs)], cout[n], sems[0].at[4 * n], sems[1].at[4 * n], sib))
        for e in range(len(extra)):
            cps.append(_remote(cin[first_extra + e], cout[first_extra + e], sems[0].at[4 * n + 1 + e],
                               sems[1].at[4 * n + 1 + e], sib))
        return cps

    start = lambda cin, cout, sems: _start_all(copies(cin, cout, sems))
    finish = lambda cin, cout, sems: _wait_all(copies(cin, cout, sems))
    shapes = [jax.ShapeDtypeStruct((N_CHIPS,) + _HALF[nm], F32) for nm in names]
    ins = list(parts)
    if small is not None:
        shapes.append(jax.ShapeDtypeStruct((hs, 128), F32))
        ins.append(small)
    shapes += [jax.ShapeDtypeStruct(a.shape, a.dtype) for a in extra]
    ins += list(extra)
    n_sem = 4 * n + 1 + len(extra)
    return _Comm(ins, shapes, [_DMA((n_sem,)), _DMA((n_sem,))], start, finish)


def chip_exchange(sums, small=None):
    n = len(sums)

    def copies(cin, cout, sems, jj, c):
        out = []
        for k in _others(jj):
            for i in range(n):
                out.append(_remote(cin[i].at[k], cout[i].at[jj], sems[0].at[4 * i + k], sems[1].at[4 * i + jj], _core_of(k, c)))
            if small is not None:
                out.append(_remote(cin[n], cout[n].at[jj], sems[0].at[4 * n + k], sems[1].at[4 * n + jj], _core_of(k, c)))
        return out

    def start(cin, cout, sems):
        _, _, c, j = _place()

        def run(jj):
            if small is not None:
                pltpu.make_async_copy(cin[n], cout[n].at[jj], sems[2]).start()
            for cp in copies(cin, cout, sems, jj, c):
                cp.start()

        _as_chip(j, run)

    def finish(cin, cout, sems):
        _, _, c, j = _place()

        def run(jj):
            for k in _others(jj):
                for i in range(n):
                    _remote(cin[i].at[k], cout[i].at[k], sems[0].at[4 * i + k], sems[1].at[4 * i + k], _core_of(k, c)).wait_recv()
                if small is not None:
                    _remote(cin[n], cout[n].at[k], sems[0].at[4 * n + k], sems[1].at[4 * n + k], _core_of(k, c)).wait_recv()
            for cp in copies(cin, cout, sems, jj, c):
                cp.wait_send()
            if small is not None:
                pltpu.make_async_copy(cin[n], cout[n].at[jj], sems[2]).wait()

        _as_chip(j, run)

    shapes = [jax.ShapeDtypeStruct(a.shape, a.dtype) for a in sums]
    ins = list(sums)
    if small is not None:
        shapes.append(jax.ShapeDtypeStruct((N_CHIPS,) + small.shape, small.dtype))
        ins.append(small)
    return _Comm(ins, shapes, [_DMA((4 * n + 4,)), _DMA((4 * n + 4,)), _DMA(())], start, finish)


def pair_send_totals(arrs):
    n = len(arrs)

    def copies(cin, cout, sems):
        x, y, c, _ = _place()
        return [_remote(cin[i], cout[i], sems[0].at[i], sems[1].at[i], (x, y, 1 - c)) for i in range(n)]

    start = lambda cin, cout, sems: _start_all(copies(cin, cout, sems))
    finish = lambda cin, cout, sems: _wait_all(copies(cin, cout, sems))
    return _Comm(arrs, [jax.ShapeDtypeStruct(a.shape, a.dtype) for a in arrs], [_DMA((n,)), _DMA((n,))], start, finish)


def _pin(a):
    return pltpu.with_memory_space_constraint(a, pltpu.HBM)


def _plain_params(n_axes):
    return pltpu.CompilerParams(dimension_semantics=("arbitrary",) * n_axes, vmem_limit_bytes=V7X_VMEM_BYTES // 2)


def add_own_halves(names, parts, landed, core, name):
    n = len(names)

    def body(core_ref, *refs):
        for i in range(n):
            refs[2 * n + i][0] = (refs[i][...] + refs[n + i][0]).astype(MXU_DTYPE)

    in_specs, out_specs = [], []
    for nm in names:
        hr, hc = _HALF[nm]
        if nm in ("pa", "pb"):
            in_specs.append(pl.BlockSpec((hr // 2, hc), lambda k, s, cr: (cr[0] * 2 + s, k)))
        else:
            in_specs.append(pl.BlockSpec((hr // 2, hc), lambda k, s, cr: (k * 4 + cr[0] * 2 + s, 0)))
    for nm in names:
        hr, hc = _HALF[nm]
        in_specs.append(pl.BlockSpec((1, hr // 2, hc), lambda k, s, cr: (k, s, 0)))
        out_specs.append(pl.BlockSpec((1, hr // 2, hc), lambda k, s, cr: (k, s, 0)))
    return pl.pallas_call(
        body, name=name,
        grid_spec=pltpu.PrefetchScalarGridSpec(num_scalar_prefetch=1, grid=(N_CHIPS, 2), in_specs=in_specs, out_specs=out_specs),
        out_shape=[pltpu.HBM((N_CHIPS,) + _HALF[nm], MXU_DTYPE) for nm in names],
        compiler_params=_plain_params(2),
    )(core, *[_pin(a) for a in parts], *[_pin(a) for a in landed])


def add_small_half(small, landed, core):
    hs = SMALL_ROWS // 2

    def body(core_ref, a_ref, b_ref, o_ref):
        o_ref[...] = a_ref[...] + b_ref[...]

    return pl.pallas_call(
        body, name="add_small_half",
        grid_spec=pltpu.PrefetchScalarGridSpec(
            num_scalar_prefetch=1, grid=(1,),
            in_specs=[pl.BlockSpec((hs, 128), lambda s, cr: (cr[0], 0)), pl.BlockSpec((hs, 128), lambda s, cr: (0, 0))],
            out_specs=pl.BlockSpec((hs, 128), lambda s, cr: (0, 0))),
        out_shape=jax.ShapeDtypeStruct((hs, 128), F32),
    )(core, small, landed)


def sum_chips(sums, landed, chips, name, small_landed=None):
    n = len(sums)

    def body(chips_ref, *refs):
        ins, outs = refs[:4 * n + (small_landed is not None)], refs[4 * n + (small_landed is not None):]
        for i in range(n):
            tot = ins[4 * i][0].astype(F32)
            for r in range(1, N_CHIPS):
                tot = tot + ins[4 * i + r][0].astype(F32)
            outs[i][...] = tot
        if small_landed is not None:
            @pl.when(pl.program_id(0) == 0)
            def _():
                sm = ins[4 * n]
                outs[n][...] = ((sm[0] + sm[1]) + sm[2]) + sm[3]

    in_specs, out_specs, operands, shapes = [], [], [], []
    for a, l in zip(sums, landed):
        _, hr, hc = a.shape
        for r in range(N_CHIPS):
            in_specs.append(pl.BlockSpec((1, hr // 2, hc), functools.partial(lambda s, ch, r: (ch[r], s, 0), r=r)))
            operands.append(a if r == 0 else l)
        out_specs.append(pl.BlockSpec((hr // 2, hc), lambda s, ch: (s, 0)))
        shapes.append(jax.ShapeDtypeStruct((hr, hc), F32))
    if small_landed is not None:
        in_specs.append(pl.BlockSpec(small_landed.shape, lambda s, ch: (0, 0, 0)))
        out_specs.append(pl.BlockSpec(small_landed.shape[1:], lambda s, ch: (0, 0)))
        operands.append(small_landed)
        shapes.append(jax.ShapeDtypeStruct(small_landed.shape[1:], F32))
    return pl.pallas_call(
        body, name=name,
        grid_spec=pltpu.PrefetchScalarGridSpec(num_scalar_prefetch=1, grid=(2,), in_specs=in_specs, out_specs=out_specs),
        out_shape=[pltpu.HBM(s.shape, s.dtype) for s in shapes], compiler_params=_plain_params(1),
    )(chips, *[_pin(a) for a in operands])


N_STEPS = 8


def _adam_update(w, g, m, v):
    m = ADAM_B1 * m + (1.0 - ADAM_B1) * g
    v = ADAM_B2 * v + (1.0 - ADAM_B2) * (g * g)
    m_hat = m * (1.0 / (1.0 - ADAM_B1 ** ADAM_STEP))
    v_hat = v * (1.0 / (1.0 - ADAM_B2 ** ADAM_STEP))
    return -ADAM_LR * (m_hat / (jnp.sqrt(v_hat) + ADAM_EPS) + ADAM_WD * w), m, v


def adamw_big(ws, g_layers, ms, vs):
    n = len(ws)
    per_layer = N_STEPS // DEPTH

    def body(*refs):
        first = pl.program_id(0) < per_layer
        for i in range(n):
            w, m, v = (refs[k * n + i][...] for k in range(3))
            g = jnp.where(first, refs[3 * n + 2 * i][...], refs[3 * n + 2 * i + 1][...])[None]
            delta, m, v = _adam_update(w, g, m, v)
            for k, val in enumerate((g, delta, m, v)):
                refs[5 * n + k * n + i][...] = val

    whole = lambda a: pl.BlockSpec((1, a.shape[1] // per_layer, a.shape[2]), lambda s: (s // per_layer, s % per_layer, 0))
    layer0 = lambda a: pl.BlockSpec((a.shape[1] // per_layer, a.shape[2]), lambda s: (jnp.minimum(s, per_layer - 1), 0))
    layer1 = lambda a: pl.BlockSpec((a.shape[1] // per_layer, a.shape[2]), lambda s: (jnp.maximum(s - per_layer, 0), 0))
    g_specs, g_ops = [], []
    for a, (g0, g1) in zip(ws, g_layers):
        g_specs += [layer0(a), layer1(a)]
        g_ops += [g0, g1]
    outs = pl.pallas_call(
        body, name="adamw_big", grid=(N_STEPS,),
        in_specs=[whole(a) for a in ws] * 3 + g_specs, out_specs=[whole(a) for a in ws] * 4,
        out_shape=[jax.ShapeDtypeStruct(a.shape, F32) for a in ws] * 4,
        compiler_params=_plain_params(1),
    )(*ws, *ms, *vs, *g_ops)
    return outs[:n], outs[n:2 * n], outs[2 * n:3 * n], outs[3 * n:]


def adamw_small(ws, gs, ms, vs):
    n = len(ws)

    def body(*refs):
        for i in range(n):
            outs = _adam_update(*(refs[k * n + i][...] for k in range(4)))
            for k, val in enumerate(outs):
                refs[4 * n + k * n + i][...] = val

    specs = [_const_block(a.shape) for a in ws]
    outs = pl.pallas_call(
        body, name="adamw_small", grid=(1,), in_specs=specs * 4, out_specs=specs * 3,
        out_shape=[jax.ShapeDtypeStruct(a.shape, F32) for a in ws] * 3,
        compiler_params=_plain_params(1),
    )(*ws, *gs, *ms, *vs)
    return outs[:n], outs[n:2 * n], outs[2 * n:]


_SMALL = (("w_s", (DEPTH, SGU_G, BLK, BLK)), ("ln_in_g", (D_MODEL,)), ("ln_in_b", (D_MODEL,)), ("b_in", (DEPTH, N_COLS)),
          ("sinks", (DEPTH, 8)), ("vn_g", (DEPTH, SGU_W)), ("vn_b", (DEPTH, SGU_W)),
          ("b_s", (DEPTH, SGU_G, BLK)), ("b_out", (DEPTH, D_MODEL)), ("ln_g", (DEPTH, D_MODEL)), ("ln_b", (DEPTH, D_MODEL)))
_BIG = ("w_in", "p_a", "p_b", "w_out")


def _size(shape):
    n = 1
    for d in shape:
        n *= d
    return n


def _pack_small(vals, last_row=None):
    rows = []
    for name, shape in _SMALL:
        v = vals[name]
        if _size(shape) % 128:
            v = jnp.pad(v.reshape((1, -1)), ((0, 0), (0, (-_size(shape)) % 128)))
        rows.append(v.reshape((-1, 128)))
    used = sum(r.shape[0] for r in rows)
    tail = jnp.zeros((SMALL_ROWS - used, 128), F32)
    if last_row is not None:
        tail = tail.at[-1, 0].set(last_row)
    return jnp.concatenate([rows[0], jnp.concatenate(rows[1:] + [tail], axis=0)], axis=0)


def _unpack_small(packed):
    out, pos = {}, 0
    for name, shape in _SMALL:
        n = -(-_size(shape) // 128)
        rows = packed[pos:pos + n]
        out[name] = (rows.reshape((-1,))[:_size(shape)] if _size(shape) % 128 else rows).reshape(shape)
        pos += n
    return out


def kernel(x, ln_in_g, ln_in_b, w_in, b_in, sinks, vn_g, vn_b, w_s, b_s, p_a, p_b, w_out, b_out, ln_g, ln_b, loss_target, m_ln_in_g, m_ln_in_b, m_w_in, m_b_in, m_sinks, m_vn_g, m_vn_b, m_w_s, m_b_s, m_p_a, m_p_b, m_w_out, m_b_out, m_ln_g, m_ln_b, v_ln_in_g, v_ln_in_b, v_w_in, v_b_in, v_sinks, v_vn_g, v_vn_b, v_w_s, v_b_s, v_p_a, v_p_b, v_w_out, v_b_out, v_ln_g, v_ln_b):
    weights = dict(ln_in_g=ln_in_g, ln_in_b=ln_in_b, w_in=w_in, b_in=b_in, sinks=sinks, vn_g=vn_g, vn_b=vn_b, w_s=w_s,
                   b_s=b_s, p_a=p_a, p_b=p_b, w_out=w_out, b_out=b_out, ln_g=ln_g, ln_b=ln_b)
    mom1 = dict(ln_in_g=m_ln_in_g, ln_in_b=m_ln_in_b, w_in=m_w_in, b_in=m_b_in, sinks=m_sinks, vn_g=m_vn_g, vn_b=m_vn_b,
                w_s=m_w_s, b_s=m_b_s, p_a=m_p_a, p_b=m_p_b, w_out=m_w_out, b_out=m_b_out, ln_g=m_ln_g, ln_b=m_ln_b)
    mom2 = dict(ln_in_g=v_ln_in_g, ln_in_b=v_ln_in_b, w_in=v_w_in, b_in=v_b_in, sinks=v_sinks, vn_g=v_vn_g, vn_b=v_vn_b,
                w_s=v_w_s, b_s=v_b_s, p_a=v_p_a, p_b=v_p_b, w_out=v_w_out, b_out=v_b_out, ln_g=v_ln_g, ln_b=v_ln_b)
    n_seq, S, _ = x.shape
    T = n_seq * S
    c = lax.axis_index("c")
    j = 2 * lax.axis_index("x") + lax.axis_index("y")
    core = c.astype(jnp.int32).reshape((1,))
    chips = jnp.stack([j] + [r + (r >= j) for r in range(N_CHIPS - 1)]).astype(jnp.int32)
    names = list(_PIECES)
    xt, tt = x.reshape((T, D_MODEL)), loss_target.reshape((T, D_MODEL))

    tview = lambda d: dict(d, w_in=jnp.swapaxes(d["w_in"], 1, 2))
    weights_t, mom1_t, mom2_t = tview(weights), tview(mom1), tview(mom2)

    def own_shards(l):
        return [weights_t[n][l].astype(MXU_DTYPE) for n in _BIG]

    def layer_weights(l, gathered):
        d = dict(zip(_PIECES, gathered))
        d.update(paT=d["pa"].T, pbT=d["pb"].T, woutT=d["wout"].T)
        d.update(bin=b_in[l][None], sinks=sinks[l], vng=vn_g[l][None], vnb=vn_b[l][None], ws=w_s[l],
                 bs=jnp.repeat(b_s[l].T, 128, axis=1), bout=b_out[l][None], lng=ln_g[l][None], lnb=ln_b[l][None])
        return d

    own0, own1 = own_shards(0), own_shards(1)
    x0, g0a = input_norm_forward(xt, ln_in_g[None], ln_in_b[None], "ln_in_fwd", comm=gather_over_chips(names, own0))
    lw0 = layer_weights(0, _run_comm(gather_over_pair(names, g0a), "gather0_pair"))
    h0, g1a = project(x0, lw0["winT"], lw0["bin"], "layer0_proj", comm=gather_over_chips(names, own1))
    lw1 = layer_weights(1, _run_comm(gather_over_pair(names, g1a), "gather1_pair"))
    x1, h1 = layer_forward_and_projection(x0, h0, lw0, lw1["winT"], lw1["bin"], S, "layer0_fwd_layer1_proj")

    dres1, dh1, acc1, _ = layer_backward(x1, h1, tt, lw1, S, True, "layer1_bwd")
    dx1, dwinT1, _ = input_grads(dres1, dh1, x1, lw1["winT"], "layer1_dx_dwin")
    parts1 = [dwinT1, acc1["dpa"], acc1["dpb"], acc1["dwout"]]
    dres0, dh0, acc0, landed1 = layer_backward(x0, h0, dx1, lw0, S, False, "layer0_bwd", comm=pair_send_halves(names, parts1))
    sums1 = add_own_halves(names, parts1, landed1, core, "add_own_halves1")
    dx0, dwinT0, from_chips1 = input_grads(dres0, dh0, x0, lw0["winT"], "layer0_dx_dwin", comm=chip_exchange(sums1))
    totals1 = sum_chips(sums1, from_chips1, chips, "sum_chips1")
    grad_x, d_ln_in_g, d_ln_in_b, _ = input_norm_backward(xt, dx0, ln_in_g, "ln_in_bwd")

    gl = [acc0, acc1]
    per_layer = lambda key, pick: jnp.stack([pick(gl[l][key]) for l in range(DEPTH)])
    row0 = lambda key: per_layer(key, lambda a: a[0])
    small_part = _pack_small(dict(
        ln_in_g=d_ln_in_g[0], ln_in_b=d_ln_in_b[0], b_in=row0("dbin"), sinks=per_layer("dsink", lambda a: a[:, 0]),
        vn_g=row0("dvng"), vn_b=row0("dvnb"), w_s=per_layer("dws", lambda a: a),
        b_s=per_layer("dbs", lambda a: a[:, ::128].T), b_out=row0("dbout"), ln_g=row0("dlng"), ln_b=row0("dlnb")),
        last_row=acc1["loss"][0, 0])
    parts0 = [dwinT0, acc0["dpa"], acc0["dpb"], acc0["dwout"]]
    first = _run_comm(pair_send_halves(names, parts0, small_part, extra=totals1), "pair_send_halves0")
    landed0, small_landed, sib_totals1 = first[:4], first[4], first[5:]
    sums0 = add_own_halves(names, parts0, landed0, core, "add_own_halves0")
    small_sum = add_small_half(small_part, small_landed, core)
    *from_chips0, small_from_chips = _run_comm(chip_exchange(sums0, small_sum), "chip_exchange0")
    *totals0, small_total = sum_chips(sums0, from_chips0, chips, "sum_chips0", small_landed=small_from_chips)
    *sib_totals0, sib_small = _run_comm(pair_send_totals(totals0 + [small_total]), "pair_send_totals0")

    def both_halves(mine, theirs):
        return jnp.where(c == 0, jnp.concatenate([mine, theirs], axis=0), jnp.concatenate([theirs, mine], axis=0))

    g_layers = [(both_halves(totals0[pi], sib_totals0[pi]), both_halves(totals1[pi], sib_totals1[pi]))
                for pi in range(len(_BIG))]
    g_small = both_halves(small_total, sib_small)
    loss = g_small[-1, 0]

    big = adamw_big([weights_t[n] for n in _BIG], g_layers, [mom1_t[n] for n in _BIG], [mom2_t[n] for n in _BIG])
    small_names = [n for n, _ in _SMALL]
    as2d = lambda a: a.reshape((1, -1)) if a.ndim == 1 else a
    g_named = _unpack_small(g_small)
    small = adamw_small(*[[as2d(d[n]) for n in small_names] for d in (weights, g_named, mom1, mom2)])

    def named(big_outs, small_outs):
        d = dict(zip(_BIG, big_outs))
        d["w_in"] = jnp.swapaxes(d["w_in"], 1, 2)
        d.update({n: o.reshape(weights[n].shape) for n, o in zip(small_names, small_outs)})
        return d

    order = ("ln_in_g", "ln_in_b", "w_in", "b_in", "sinks", "vn_g", "vn_b", "w_s", "b_s", "p_a", "p_b", "w_out", "b_out",
             "ln_g", "ln_b")
    res = [loss, grad_x.reshape(x.shape)]
    groups = [named(big[0], [g_named[n] for n in small_names])] + [named(big[k + 1], small[k]) for k in range(3)]
    for group in groups:
        res.extend(group[n] for n in order)
    return tuple(res)
```

```python
import functools

import jax
import jax.numpy as jnp
from jax import lax
from jax.experimental import pallas as pl
from jax.experimental.pallas import tpu as pltpu

F32 = jnp.float32
MXU_DTYPE = jnp.bfloat16

D_MODEL = 1024
DEPTH = 2
HEAD_DIM = 64
ATTN_W = 512
KV_W = 128
BLK = 128
SGU_W = 512
SGU_G = 4
N_COLS = 4864
C_Q, C_K, C_V, C_GA, C_UB, C_VB, C_GB, C_RA, C_RB = 0, 512, 640, 768, 1280, 1792, 2304, 2816, 3840
ALPHA = (2.0 * DEPTH) ** 0.25
LN_EPS = 1e-5
SCALE = HEAD_DIM ** -0.5
NEG = float(jnp.finfo(jnp.float32).min)
GELU_C = 0.7978845608028654
GELU_A = 0.044715

ADAM_LR, ADAM_B1, ADAM_B2, ADAM_EPS, ADAM_WD, ADAM_STEP = 0.001, 0.9, 0.999, 1e-08, 0.01, 10

V7X_VMEM_BYTES = 64 * 1024 * 1024
V7X_VMEM_RESERVE = 3 * 1024 * 1024
ROWS = 256
ROWS_BWD = 256
ROWS_PROJ = 512

MESH = pl.DeviceIdType.MESH
N_CHIPS = 4


def _mm(a, b):
    return jnp.dot(a.astype(MXU_DTYPE), b.astype(MXU_DTYPE), preferred_element_type=F32)


def _mm_nt(a, b):
    return lax.dot_general(a.astype(MXU_DTYPE), b.astype(MXU_DTYPE), (((1,), (1,)), ((), ())),
                           preferred_element_type=F32)


def _mm_tn(a, b):
    return lax.dot_general(a.astype(MXU_DTYPE), b.astype(MXU_DTYPE), (((0,), (0,)), ((), ())),
                           preferred_element_type=F32)


def _sigmoid(x):
    return 0.5 * jnp.tanh(0.5 * x) + 0.5


def _gelu_parts(x):
    x2 = x * x
    u = 0.5 * jnp.tanh(x * (GELU_C + (GELU_C * GELU_A) * x2)) + 0.5
    dg = u * (1.0 + x * (1.0 - u) * (2.0 * GELU_C + (6.0 * GELU_C * GELU_A) * x2))
    return x * u, dg


def _ln_fwd(z, g, b):
    mu = jnp.mean(z, axis=-1, keepdims=True)
    zc = z - mu
    var = jnp.mean(zc * zc, axis=-1, keepdims=True)
    rstd = lax.rsqrt(var + LN_EPS)
    zhat = zc * rstd
    return zhat * g + b, zhat, rstd


def _ln_bwd(dy, zhat, rstd, g):
    dzh = dy * g
    m1 = jnp.mean(dzh, axis=-1, keepdims=True)
    m2 = jnp.mean(dzh * zhat, axis=-1, keepdims=True)
    return rstd * (dzh - m1 - zhat * m2)


def _colsum(v):
    return jnp.sum(v, axis=0, keepdims=True)


def _lane_lo(rows):
    return lax.broadcasted_iota(jnp.int32, (rows, 128), 1) < HEAD_DIM


def _dup_head(x2, hk, lo):
    xr = pltpu.roll(x2, HEAD_DIM, 1)
    return jnp.where(lo, x2, xr) if hk == 0 else jnp.where(lo, xr, x2)


def _fold_head(y0, y1, lo):
    f0 = y0 + pltpu.roll(y0, HEAD_DIM, 1)
    f1 = y1 + pltpu.roll(y1, HEAD_DIM, 1)
    return jnp.where(lo, f0, f1)


def _stack_heads(t0, t1, lo):
    z = jnp.zeros_like(t0)
    return jnp.concatenate([jnp.where(lo, t0, z), jnp.where(lo, z, t0),
                            jnp.where(lo, t1, z), jnp.where(lo, z, t1)], axis=0)


def _unstack_heads(o, lo):
    return (jnp.where(lo, o[0:128], o[128:256]), jnp.where(lo, o[256:384], o[384:512]))


def _band_bias(start):
    row = lax.broadcasted_iota(jnp.int32, (4 * BLK, 2 * BLK), 0) & (BLK - 1)
    kpos = lax.broadcasted_iota(jnp.int32, (4 * BLK, 2 * BLK), 1)
    valid = (kpos > row) & (kpos <= row + BLK)
    if start is not None:
        valid = valid & (jnp.logical_not(start) | (kpos >= BLK))
    return jnp.where(valid, 0.0, NEG)


def _attn_probs(s, sink4, bias):
    s = s * SCALE + bias
    r1 = lax.broadcasted_iota(jnp.int32, (4 * BLK, 1), 0)
    sk = jnp.where(r1 < BLK, sink4[0], jnp.where(r1 < 2 * BLK, sink4[1], jnp.where(r1 < 3 * BLK, sink4[2], sink4[3])))
    m = jnp.maximum(jnp.max(s, axis=-1, keepdims=True), sk)
    p = jnp.exp(s - m)
    es = jnp.exp(sk - m)
    inv = 1.0 / (jnp.sum(p, axis=-1, keepdims=True) + es)
    return p * inv, es * inv


_PROJ_CHUNK = 512


def _kv_blocks(kvh, h_s, nb):
    ks = [kvh[:, 0:KV_W]] + [h_s[n * BLK:(n + 1) * BLK, C_K:C_K + KV_W] for n in range(nb)]
    vs = [kvh[:, KV_W:2 * KV_W]] + [h_s[n * BLK:(n + 1) * BLK, C_V:C_V + KV_W] for n in range(nb)]
    return ks, vs


def _attn_operands(h_s, ks, vs, n, hk, lo, lo2):
    rows = slice(n * BLK, (n + 1) * BLK)
    kd = _dup_head(jnp.concatenate([ks[n], ks[n + 1]], axis=0), hk, lo2)
    vd = _dup_head(jnp.concatenate([vs[n], vs[n + 1]], axis=0), hk, lo2)
    c0 = C_Q + 2 * hk * 128
    qs = _stack_heads(h_s[rows, c0:c0 + 128], h_s[rows, c0 + 128:c0 + 256], lo)
    return qs, kd, vd


def _sgu_weights(w):
    tri = (lax.broadcasted_iota(jnp.int32, (BLK, BLK), 0) >= lax.broadcasted_iota(jnp.int32, (BLK, BLK), 1))
    return tri, [jnp.where(tri, w["ws"][g], 0.0) for g in range(SGU_G)]


def _layer_forward(x, kvh, start, w, h_s, attn_s, mix_s, keep, tick=lambda: None):
    R = x.shape[0]
    nb = R // BLK
    lo = _lane_lo(BLK)
    lo2 = _lane_lo(2 * BLK)
    ks, vs = _kv_blocks(kvh, h_s, nb)
    sinks = [w["sinks"][j] for j in range(8)]
    bias_first = _band_bias(start)
    bias_rest = _band_bias(None) if nb > 1 else None
    pairs = [(n, hk) for n in range(nb) for hk in range(2)]
    ops = [_attn_operands(h_s, ks, vs, n, hk, lo, lo2) for n, hk in pairs]
    scores = [_mm_nt(qs, kd) for qs, kd, _ in ops]
    probs_l = [_attn_probs(s, sinks[4 * hk:4 * hk + 4], bias_first if n == 0 else bias_rest)
               for s, (n, hk) in zip(scores, pairs)]
    outs = [_mm(p, vd) for (p, _), (_, _, vd) in zip(probs_l, ops)]
    for o, (n, hk) in zip(outs, pairs):
        rows = slice(n * BLK, (n + 1) * BLK)
        t0, t1 = _unstack_heads(o, lo)
        attn_s[rows, 2 * hk * 128:(2 * hk + 1) * 128] = t0
        attn_s[rows, (2 * hk + 1) * 128:(2 * hk + 2) * 128] = t1
        tick()
    ga = h_s[:, C_GA:C_GA + ATTN_W]
    ya = attn_s[...] * (ga * _sigmoid(ga))
    tick()
    gu, _ = _gelu_parts(h_s[:, C_UB:C_UB + SGU_W])
    tick()
    gv, _ = _gelu_parts(h_s[:, C_VB:C_VB + SGU_W])
    vn, _, _ = _ln_fwd(gv, w["vng"][...], w["vnb"][...])
    tick()
    _, wms = _sgu_weights(w)
    for n in range(nb):
        rows = slice(n * BLK, (n + 1) * BLK)
        for g in range(SGU_G):
            cols = slice(g * 128, (g + 1) * 128)
            mix_s[rows, cols] = _mm(wms[g], vn[rows, cols]) + w["bs"][:, cols]
    gb = h_s[:, C_GB:C_GB + SGU_W]
    yb = gu * mix_s[...] * (gb * _sigmoid(gb))
    tick()
    a = _mm(ya, w["pa"][...])
    b = _mm(yb, w["pb"][...])
    tick()
    merged = _sigmoid(h_s[:, C_RA:C_RA + D_MODEL]) * a + _sigmoid(h_s[:, C_RB:C_RB + D_MODEL]) * b
    tick()
    out = _mm(merged, w["wout"][...]) + w["bout"][...]
    y, zhat, rstd = _ln_fwd(ALPHA * x + out, w["lng"][...], w["lnb"][...])
    if not keep:
        return y, None
    return y, dict(ks=ks, vs=vs, probs=probs_l, ya=ya, yb=yb, a=a, b=b, merged=merged, zhat=zhat, rstd=rstd)


def _dsilu(g, sg):
    return sg * (1.0 + g * (1.0 - sg))


_W_NAMES = ("sinks", "vng", "vnb", "ws", "bs", "pa", "pb", "wout", "bout", "lng", "lnb")
_WT_NAMES = ("paT", "pbT", "woutT")
_ACC_NAMES = ("dwout", "dpa", "dpb", "dbin", "dbout", "dlng", "dlnb", "dvng", "dvnb", "dws", "dbs", "dsink", "loss")


def _make_fwd_body(R, S):
    def body(x_ref, h_ref, kvh_ref, *rest):
        w = dict(zip(_W_NAMES, rest[:len(_W_NAMES)]))
        y_ref, attn_s, mix_s = rest[len(_W_NAMES):]
        start = (pl.program_id(0) % (S // R)) == 0
        y, _ = _layer_forward(x_ref[...], kvh_ref[...], start, w, h_ref, attn_s, mix_s, keep=False)
        y_ref[...] = y
    return body


def _make_fwd_proj_body(R, S, nT):
    def body(x_ref, h_ref, kvh_ref, wn_ref, bn_ref, *rest):
        w = dict(zip(_W_NAMES, rest[:len(_W_NAMES)]))
        y_ref, hn_ref, attn_s, mix_s, yprev_s = rest[len(_W_NAMES):]

        @pl.when(pl.program_id(0) == 0)
        def _():
            yprev_s[...] = jnp.zeros(yprev_s.shape, yprev_s.dtype)

        t = jnp.minimum(pl.program_id(0), nT - 1)
        start = (t % (S // R)) == 0
        chunks = [(c0, min(c0 + _PROJ_CHUNK, N_COLS)) for c0 in range(0, N_COLS, _PROJ_CHUNK)]

        def tick():
            if chunks:
                c0, c1 = chunks.pop(0)
                hn_ref[:, c0:c1] = _mm_nt(yprev_s[...], wn_ref[c0:c1, :]) + bn_ref[:, c0:c1]

        y, _ = _layer_forward(x_ref[...], kvh_ref[...], start, w, h_ref, attn_s, mix_s, keep=False, tick=tick)
        while chunks:
            tick()
        y_ref[...] = y
        yprev_s[...] = y.astype(yprev_s.dtype)
    return body


def _make_bwd_body(R, S, nT, is_last):
    nb = R // BLK
    names = _W_NAMES + _WT_NAMES
    nw = len(names)

    def body(x_ref, h_s, kvh_ref, aux_ref, *rest):
        w = dict(zip(names, rest[:nw]))
        dres_ref, dh_ref = rest[nw:nw + 2]
        acc = dict(zip(_ACC_NAMES, rest[nw + 2:nw + 2 + len(_ACC_NAMES)]))
        attn_s, mix_s, ckv_s = rest[nw + 2 + len(_ACC_NAMES):]
        i = pl.program_id(0)
        start = ((nT - 1 - i) % (S // R)) == 0
        lo = _lane_lo(BLK)
        lo2 = _lane_lo(2 * BLK)

        @pl.when(i == 0)
        def _():
            for name in _ACC_NAMES:
                acc[name][...] = jnp.zeros(acc[name].shape, F32)
            ckv_s[...] = jnp.zeros(ckv_s.shape, F32)

        x = x_ref[...]
        y, sv = _layer_forward(x, kvh_ref[...], start, w, h_s, attn_s, mix_s, keep=True)
        if is_last:
            diff = y - aux_ref[...]
            part = 0.5 * jnp.sum(jnp.mean(diff * diff, axis=-1, keepdims=True), axis=0, keepdims=True)
            acc["loss"][...] += jnp.broadcast_to(part, acc["loss"].shape)
            dy = diff * (1.0 / D_MODEL)
        else:
            dy = aux_ref[...]

        def put(c0, val, rows=slice(None)):
            width = val.shape[1]
            dh_ref[rows, c0:c0 + width] = val.astype(dh_ref.dtype)
            acc["dbin"][:, c0:c0 + width] += _colsum(val)

        acc["dlng"][...] += _colsum(dy * sv["zhat"])
        acc["dlnb"][...] += _colsum(dy)
        dz = _ln_bwd(dy, sv["zhat"], sv["rstd"], w["lng"][...])
        dres_ref[...] = ALPHA * dz
        acc["dbout"][...] += _colsum(dz)
        acc["dwout"][...] += _mm_tn(sv["merged"], dz)
        dmerged = _mm(dz, w["woutT"][...])
        sa = _sigmoid(h_s[:, C_RA:C_RA + D_MODEL])
        da = dmerged * sa
        put(C_RA, da * sv["a"] * (1.0 - sa))
        sb = _sigmoid(h_s[:, C_RB:C_RB + D_MODEL])
        db = dmerged * sb
        put(C_RB, db * sv["b"] * (1.0 - sb))
        acc["dpa"][...] += _mm_tn(sv["ya"], da)
        acc["dpb"][...] += _mm_tn(sv["yb"], db)
        dya = _mm(da, w["paT"][...])
        dyb = _mm(db, w["pbT"][...])
        ga = h_s[:, C_GA:C_GA + ATTN_W]
        sga = _sigmoid(ga)
        put(C_GA, dya * attn_s[...] * _dsilu(ga, sga))
        attn_s[...] = dya * (ga * sga)
        ks, vs = sv["ks"], sv["vs"]
        pairs = [(n, hk) for n in range(nb) for hk in range(2)]
        ops = [_attn_operands(h_s, ks, vs, n, hk, lo, lo2) for n, hk in pairs]
        doss = [_stack_heads(attn_s[n * BLK:(n + 1) * BLK, 2 * hk * 128:(2 * hk + 1) * 128],
                             attn_s[n * BLK:(n + 1) * BLK, (2 * hk + 1) * 128:(2 * hk + 2) * 128], lo) for n, hk in pairs]
        dps = [_mm_nt(dos, vd) for dos, (_, _, vd) in zip(doss, ops)]
        gb = h_s[:, C_GB:C_GB + SGU_W]
        sgb = _sigmoid(gb)
        gu, dgu = _gelu_parts(h_s[:, C_UB:C_UB + SGU_W])
        mixed = mix_s[...]
        put(C_GB, dyb * (gu * mixed) * _dsilu(gb, sgb))
        dsgu = dyb * (gb * sgb)
        put(C_UB, dsgu * mixed * dgu)
        dmixed = dsgu * gu
        dsss = []
        for dp, (probs, ps), (n, hk) in zip(dps, sv["probs"], pairs):
            delta = jnp.sum(probs * dp, axis=-1, keepdims=True)
            dsk = -(ps * delta)
            for g in range(4):
                j = 4 * hk + g
                acc["dsink"][j:j + 1, :] += jnp.broadcast_to(_colsum(dsk[g * BLK:(g + 1) * BLK]), (1, 128))
            dsss.append(probs * (dp - delta) * SCALE)
        dqs = [_mm(dss, kd) for dss, (_, kd, _) in zip(dsss, ops)]
        ydks = [_mm_tn(dss, qs) for dss, (qs, _, _) in zip(dsss, ops)]
        ydvs = [_mm_tn(probs, dos) for (probs, _), dos in zip(sv["probs"], doss)]
        gv, dgv = _gelu_parts(h_s[:, C_VB:C_VB + SGU_W])
        vn, vhat, vrstd = _ln_fwd(gv, w["vng"][...], w["vnb"][...])
        tri, wms = _sgu_weights(w)
        dbs_part = jnp.zeros((BLK, SGU_W), F32)
        for n in range(nb):
            rows = slice(n * BLK, (n + 1) * BLK)
            dbs_part = dbs_part + dmixed[rows]
            for g in range(SGU_G):
                cols = slice(g * 128, (g + 1) * 128)
                acc["dws"][g] += jnp.where(tri, _mm_nt(dmixed[rows, cols], vn[rows, cols]), 0.0)
                mix_s[rows, cols] = _mm_tn(wms[g], dmixed[rows, cols])
        acc["dbs"][...] += dbs_part
        for dq, (n, hk) in zip(dqs, pairs):
            q0, q1 = _unstack_heads(dq, lo)
            put(C_Q + 2 * hk * 128, q0, slice(n * BLK, (n + 1) * BLK))
            put(C_Q + (2 * hk + 1) * 128, q1, slice(n * BLK, (n + 1) * BLK))
        dks = [jnp.zeros((BLK, KV_W), F32) for _ in range(nb + 1)]
        dvs = [jnp.zeros((BLK, KV_W), F32) for _ in range(nb + 1)]
        for n in range(nb):
            dk2 = _fold_head(ydks[2 * n], ydks[2 * n + 1], lo2)
            dv2 = _fold_head(ydvs[2 * n], ydvs[2 * n + 1], lo2)
            dks[n] = dks[n] + dk2[0:BLK]
            dks[n + 1] = dks[n + 1] + dk2[BLK:2 * BLK]
            dvs[n] = dvs[n] + dv2[0:BLK]
            dvs[n + 1] = dvs[n + 1] + dv2[BLK:2 * BLK]
        dks[nb] = dks[nb] + ckv_s[:, 0:KV_W]
        dvs[nb] = dvs[nb] + ckv_s[:, KV_W:2 * KV_W]
        ckv_s[:, 0:KV_W] = dks[0]
        ckv_s[:, KV_W:2 * KV_W] = dvs[0]
        put(C_K, jnp.concatenate(dks[1:], axis=0))
        put(C_V, jnp.concatenate(dvs[1:], axis=0))
        dvn = mix_s[...]
        acc["dvng"][...] += _colsum(dvn * vhat)
        acc["dvnb"][...] += _colsum(dvn)
        put(C_VB, _ln_bwd(dvn, vhat, vrstd, w["vng"][...]) * dgv)

        @pl.when(i == nT - 1)
        def _():
            for g in range(SGU_G):
                cols = slice(g * 128, (g + 1) * 128)
                tot = jnp.sum(acc["dbs"][:, cols], axis=1, keepdims=True)
                acc["dbs"][:, cols] = jnp.broadcast_to(tot, (BLK, 128))

    return body


SUB_ROWS = BLK
_KEEP = (("mg", D_MODEL), ("dz", D_MODEL), ("ya", ATTN_W), ("da", D_MODEL), ("yb", SGU_W), ("db", D_MODEL))


def _make_bwd_body2(R, S, nT, is_last):
    names = _W_NAMES + _WT_NAMES
    nw = len(names)
    n_sub = R // SUB_ROWS

    def one_block(x, kvh, aux, start, w, h_v, attn_v, mix_v, dres_v, dh_v, acc, carry, keep):
        lo = _lane_lo(BLK)
        lo2 = _lane_lo(2 * BLK)
        y, sv = _layer_forward(x, kvh, start, w, h_v, attn_v, mix_v, keep=True)
        if is_last:
            diff = y - aux
            part = 0.5 * jnp.sum(jnp.mean(diff * diff, axis=-1, keepdims=True), axis=0, keepdims=True)
            acc["loss"][...] += jnp.broadcast_to(part, acc["loss"].shape)
            dy = diff * (1.0 / D_MODEL)
        else:
            dy = aux

        def put(c0, val):
            width = val.shape[1]
            dh_v[:, c0:c0 + width] = val.astype(dh_v.dtype)
            acc["dbin"][:, c0:c0 + width] += _colsum(val)

        acc["dlng"][...] += _colsum(dy * sv["zhat"])
        acc["dlnb"][...] += _colsum(dy)
        dz = _ln_bwd(dy, sv["zhat"], sv["rstd"], w["lng"][...])
        dres_v[...] = ALPHA * dz
        acc["dbout"][...] += _colsum(dz)
        keep["mg"][...] = sv["merged"].astype(MXU_DTYPE)
        keep["dz"][...] = dz.astype(MXU_DTYPE)
        dmerged = _mm(dz, w["woutT"][...])
        sa = _sigmoid(h_v[:, C_RA:C_RA + D_MODEL])
        da = dmerged * sa
        put(C_RA, da * sv["a"] * (1.0 - sa))
        sb = _sigmoid(h_v[:, C_RB:C_RB + D_MODEL])
        db = dmerged * sb
        put(C_RB, db * sv["b"] * (1.0 - sb))
        keep["ya"][...] = sv["ya"].astype(MXU_DTYPE)
        keep["yb"][...] = sv["yb"].astype(MXU_DTYPE)
        keep["da"][...] = da.astype(MXU_DTYPE)
        keep["db"][...] = db.astype(MXU_DTYPE)
        dya = _mm(da, w["paT"][...])
        dyb = _mm(db, w["pbT"][...])
        ga = h_v[:, C_GA:C_GA + ATTN_W]
        sga = _sigmoid(ga)
        put(C_GA, dya * attn_v[...] * _dsilu(ga, sga))
        attn_v[...] = dya * (ga * sga)
        ks, vs = sv["ks"], sv["vs"]
        ydk, ydv = [], []
        for hk in range(2):
            qs, kd, vd = _attn_operands(h_v, ks, vs, 0, hk, lo, lo2)
            probs, ps = sv["probs"][hk]
            dos = _stack_heads(attn_v[:, 2 * hk * 128:(2 * hk + 1) * 128],
                               attn_v[:, (2 * hk + 1) * 128:(2 * hk + 2) * 128], lo)
            dp = _mm_nt(dos, vd)
            delta = jnp.sum(probs * dp, axis=-1, keepdims=True)
            dsk = -(ps * delta)
            for g in range(4):
                j = 4 * hk + g
                acc["dsink"][j:j + 1, :] += jnp.broadcast_to(_colsum(dsk[g * BLK:(g + 1) * BLK]), (1, 128))
            dss = probs * (dp - delta) * SCALE
            q0, q1 = _unstack_heads(_mm(dss, kd), lo)
            put(C_Q + 2 * hk * 128, q0)
            put(C_Q + (2 * hk + 1) * 128, q1)
            ydk.append(_mm_tn(dss, qs))
            ydv.append(_mm_tn(probs, dos))
        dk2 = _fold_head(ydk[0], ydk[1], lo2)
        dv2 = _fold_head(ydv[0], ydv[1], lo2)
        put(C_K, dk2[BLK:2 * BLK] + carry[0])
        put(C_V, dv2[BLK:2 * BLK] + carry[1])
        gb = h_v[:, C_GB:C_GB + SGU_W]
        sgb = _sigmoid(gb)
        gu, dgu = _gelu_parts(h_v[:, C_UB:C_UB + SGU_W])
        mixed = mix_v[...]
        put(C_GB, dyb * (gu * mixed) * _dsilu(gb, sgb))
        dsgu = dyb * (gb * sgb)
        put(C_UB, dsgu * mixed * dgu)
        dmixed = dsgu * gu
        gv, dgv = _gelu_parts(h_v[:, C_VB:C_VB + SGU_W])
        vn, vhat, vrstd = _ln_fwd(gv, w["vng"][...], w["vnb"][...])
        tri, wms = _sgu_weights(w)
        for g in range(SGU_G):
            cols = slice(g * 128, (g + 1) * 128)
            acc["dws"][g] += jnp.where(tri, _mm_nt(dmixed[:, cols], vn[:, cols]), 0.0)
            mix_v[:, cols] = _mm_tn(wms[g], dmixed[:, cols])
        acc["dbs"][...] += dmixed
        dvn = mix_v[...]
        acc["dvng"][...] += _colsum(dvn * vhat)
        acc["dvnb"][...] += _colsum(dvn)
        put(C_VB, _ln_bwd(dvn, vhat, vrstd, w["vng"][...]) * dgv)
        return dk2[0:BLK], dv2[0:BLK]

    def body(x_ref, h_s, kvh_ref, aux_ref, *rest):
        w = dict(zip(names, rest[:nw]))
        dres_ref, dh_ref = rest[nw:nw + 2]
        acc = dict(zip(_ACC_NAMES, rest[nw + 2:nw + 2 + len(_ACC_NAMES)]))
        scratch = rest[nw + 2 + len(_ACC_NAMES):]
        attn_s, mix_s, ckv_s = scratch[:3]
        keep_s = dict(zip([k for k, _ in _KEEP], scratch[3:]))
        i = pl.program_id(0)
        start = ((nT - 1 - i) % (S // R)) == 0

        @pl.when(i == 0)
        def _():
            for name in _ACC_NAMES:
                acc[name][...] = jnp.zeros(acc[name].shape, F32)
            ckv_s[...] = jnp.zeros(ckv_s.shape, F32)

        carry = (ckv_s[:, 0:KV_W], ckv_s[:, KV_W:2 * KV_W])
        for sub in reversed(range(n_sub)):
            rows = pl.ds(sub * SUB_ROWS, SUB_ROWS)
            view = lambda ref: ref.at[rows, :]
            kvh = kvh_ref[...] if sub == 0 else h_s[(sub - 1) * SUB_ROWS:sub * SUB_ROWS, C_K:C_K + 2 * KV_W]
            carry = one_block(x_ref[rows, :], kvh, aux_ref[rows, :], start if sub == 0 else None, w, view(h_s),
                              view(attn_s), view(mix_s), view(dres_ref), view(dh_ref), acc, carry,
                              {k: view(r) for k, r in keep_s.items()})
        ckv_s[:, 0:KV_W] = carry[0]
        ckv_s[:, KV_W:2 * KV_W] = carry[1]
        acc["dwout"][...] += _mm_tn(keep_s["mg"][...], keep_s["dz"][...])
        acc["dpa"][...] += _mm_tn(keep_s["ya"][...], keep_s["da"][...])
        acc["dpb"][...] += _mm_tn(keep_s["yb"][...], keep_s["db"][...])

        @pl.when(i == nT - 1)
        def _():
            for g in range(SGU_G):
                cols = slice(g * 128, (g + 1) * 128)
                tot = jnp.sum(acc["dbs"][:, cols], axis=1, keepdims=True)
                acc["dbs"][:, cols] = jnp.broadcast_to(tot, (BLK, 128))

    return body


def _resident():
    return pl.BlockSpec(memory_space=pltpu.VMEM)


_BIG_RESIDENT = ("pa", "pb", "wout", "paT", "pbT", "woutT", "dwout", "dpa", "dpb")


def _const_block(shape):
    return pl.BlockSpec(tuple(shape), lambda *_: (0,) * len(shape))


def _operand_spec(name, shape):
    if name == "sinks":
        return pl.BlockSpec(memory_space=pltpu.SMEM)
    return _resident() if name in _BIG_RESIDENT else _const_block(shape)


def _weight_specs(names, wts):
    return [_operand_spec(n, wts[n].shape) for n in names]


def _nbytes(a):
    n = jnp.dtype(a.dtype).itemsize
    for d in a.shape:
        n *= d
    return n


def _layer_params(resident=()):
    scoped = V7X_VMEM_BYTES - V7X_VMEM_RESERVE - sum(_nbytes(a) for a in resident)
    return pltpu.CompilerParams(dimension_semantics=("arbitrary",), vmem_limit_bytes=scoped)


class _Comm:
    def __init__(self, ins, out_shapes, sems, start, finish, aliases=None):
        self.ins = [pltpu.with_memory_space_constraint(a, pltpu.HBM) for a in ins]
        self.out_shapes = [pltpu.HBM(s.shape, s.dtype) for s in out_shapes]
        self.sems = list(sems)
        self.start, self.finish, self.aliases = start, finish, dict(aliases or {})


_ANY = pl.BlockSpec(memory_space=pltpu.HBM)
_DMA = pltpu.SemaphoreType.DMA


def _call(body, *, name, grid, in_specs, out_specs, out_shape, scratch_shapes, compiler_params, operands, comm=None):
    if comm is None:
        res = pl.pallas_call(body, name=name, grid=grid, in_specs=in_specs, out_specs=out_specs, out_shape=out_shape,
                             scratch_shapes=scratch_shapes, compiler_params=compiler_params)(*operands)
        return list(res), []
    n_in, n_out, n_scr = len(in_specs), len(out_specs), len(scratch_shapes)
    ci, co = len(comm.ins), len(comm.out_shapes)
    last = grid[0] - 1

    def hosted(*refs):
        ins, cin = refs[:n_in], refs[n_in:n_in + ci]
        p = n_in + ci
        outs, cout = refs[p:p + n_out], refs[p + n_out:p + n_out + co]
        p += n_out + co
        scr, sems = refs[p:p + n_scr], refs[p + n_scr:]
        pl.when(pl.program_id(0) == 0)(lambda: comm.start(cin, cout, sems))
        body(*ins, *outs, *scr)
        pl.when(pl.program_id(0) == last)(lambda: comm.finish(cin, cout, sems))

    res = pl.pallas_call(
        hosted, name=name, grid=grid, in_specs=list(in_specs) + [_ANY] * ci, out_specs=list(out_specs) + [_ANY] * co,
        out_shape=list(out_shape) + comm.out_shapes, scratch_shapes=list(scratch_shapes) + comm.sems,
        input_output_aliases={n_in + a: n_out + b for a, b in comm.aliases.items()},
        compiler_params=compiler_params)(*operands, *comm.ins)
    return list(res[:n_out]), list(res[n_out:])


def _run_comm(comm, name):
    ci, co = len(comm.ins), len(comm.out_shapes)

    def body(*refs):
        cin, cout, sems = refs[:ci], refs[ci:ci + co], refs[ci + co:]
        comm.start(cin, cout, sems)
        comm.finish(cin, cout, sems)

    return list(pl.pallas_call(body, name=name, in_specs=[_ANY] * ci, out_specs=[_ANY] * co, out_shape=comm.out_shapes,
                               scratch_shapes=comm.sems, input_output_aliases=comm.aliases)(*comm.ins))


def project(x, winT, b, name, comm=None):
    T = x.shape[0]
    R = ROWS_PROJ

    def body(x_ref, w_ref, b_ref, h_ref):
        xb = x_ref[...].astype(MXU_DTYPE)
        for c0 in range(0, N_COLS, _PROJ_CHUNK):
            c1 = min(c0 + _PROJ_CHUNK, N_COLS)
            h_ref[:, c0:c1] = _mm_nt(xb, w_ref[c0:c1, :]) + b_ref[:, c0:c1]

    (h,), extra = _call(
        body, name=name, grid=(T // R,),
        in_specs=[pl.BlockSpec((R, D_MODEL), lambda t: (t, 0)), _resident(), _const_block(b.shape)],
        out_specs=[pl.BlockSpec((R, N_COLS), lambda t: (t, 0))],
        out_shape=[jax.ShapeDtypeStruct((T, N_COLS), F32)], scratch_shapes=[],
        compiler_params=_layer_params([winT]), operands=(x, winT, b), comm=comm)
    return h, extra


_KV_BLOCK = C_K // (2 * KV_W)


def layer_forward(x, h, wts, S, name, comm=None):
    T = x.shape[0]
    R = ROWS
    npt = R // BLK
    (y,), extra = _call(
        _make_fwd_body(R, S), name=name, grid=(T // R,),
        in_specs=[pl.BlockSpec((R, D_MODEL), lambda t: (t, 0)), pl.BlockSpec((R, N_COLS), lambda t: (t, 0)),
                  pl.BlockSpec((BLK, 2 * KV_W), lambda t: (jnp.maximum(t * npt - 1, 0), _KV_BLOCK))]
        + _weight_specs(_W_NAMES, wts),
        out_specs=[pl.BlockSpec((R, D_MODEL), lambda t: (t, 0))],
        out_shape=[jax.ShapeDtypeStruct((T, D_MODEL), F32)],
        scratch_shapes=[pltpu.VMEM((R, ATTN_W), F32), pltpu.VMEM((R, SGU_W), F32)],
        compiler_params=_layer_params([wts[n] for n in _W_NAMES if n in _BIG_RESIDENT]),
        operands=(x, h, h, *[wts[n] for n in _W_NAMES]), comm=comm)
    return y, extra


def layer_forward_and_projection(x, h, wts, winT_next, b_next, S, name):
    T = x.shape[0]
    R = ROWS
    nT = T // R
    npt = R // BLK
    cur = lambda t: jnp.minimum(t, nT - 1)
    (y, hn), _ = _call(
        _make_fwd_proj_body(R, S, nT), name=name, grid=(nT + 1,),
        in_specs=[pl.BlockSpec((R, D_MODEL), lambda t: (cur(t), 0)), pl.BlockSpec((R, N_COLS), lambda t: (cur(t), 0)),
                  pl.BlockSpec((BLK, 2 * KV_W), lambda t: (jnp.maximum(cur(t) * npt - 1, 0), _KV_BLOCK)),
                  _resident(), _const_block(b_next.shape)] + _weight_specs(_W_NAMES, wts),
        out_specs=[pl.BlockSpec((R, D_MODEL), lambda t: (cur(t), 0)),
                   pl.BlockSpec((R, N_COLS), lambda t: (jnp.maximum(t - 1, 0), 0))],
        out_shape=[jax.ShapeDtypeStruct((T, D_MODEL), F32), jax.ShapeDtypeStruct((T, N_COLS), F32)],
        scratch_shapes=[pltpu.VMEM((R, ATTN_W), F32), pltpu.VMEM((R, SGU_W), F32), pltpu.VMEM((R, D_MODEL), MXU_DTYPE)],
        compiler_params=_layer_params([winT_next] + [wts[n] for n in _W_NAMES if n in _BIG_RESIDENT]),
        operands=(x, h, h, winT_next, b_next, *[wts[n] for n in _W_NAMES]))
    return y, hn


_ACC_SHAPES = dict(dwout=(D_MODEL, D_MODEL), dpa=(ATTN_W, D_MODEL), dpb=(SGU_W, D_MODEL), dbin=(1, N_COLS),
                   dbout=(1, D_MODEL), dlng=(1, D_MODEL), dlnb=(1, D_MODEL), dvng=(1, SGU_W), dvnb=(1, SGU_W),
                   dws=(SGU_G, BLK, BLK), dbs=(BLK, SGU_W), dsink=(8, 128), loss=(8, 128))


def layer_backward(x, h, aux, wts, S, is_last, name, comm=None):
    T = x.shape[0]
    R = ROWS_BWD
    nT = T // R
    npt = R // BLK
    rev = lambda i: (nT - 1 - i, 0)
    halo = lambda i: (jnp.maximum((nT - 1 - i) * npt - 1, 0), _KV_BLOCK)
    names = _W_NAMES + _WT_NAMES
    out_shape = ([jax.ShapeDtypeStruct((T, D_MODEL), F32), jax.ShapeDtypeStruct((T, N_COLS), MXU_DTYPE)]
                 + [jax.ShapeDtypeStruct(_ACC_SHAPES[n], F32) for n in _ACC_NAMES])
    outs, extra = _call(
        _make_bwd_body(R, S, nT, is_last), name=name, grid=(nT,),
        in_specs=[pl.BlockSpec((R, D_MODEL), rev), pl.BlockSpec((R, N_COLS), rev), pl.BlockSpec((BLK, 2 * KV_W), halo),
                  pl.BlockSpec((R, D_MODEL), rev)] + _weight_specs(names, wts),
        out_specs=[pl.BlockSpec((R, D_MODEL), rev), pl.BlockSpec((R, N_COLS), rev)]
        + [_operand_spec(n, _ACC_SHAPES[n]) for n in _ACC_NAMES],
        out_shape=out_shape,
        scratch_shapes=[pltpu.VMEM((R, ATTN_W), F32), pltpu.VMEM((R, SGU_W), F32), pltpu.VMEM((BLK, 2 * KV_W), F32)],
        compiler_params=_layer_params([wts[n] for n in names if n in _BIG_RESIDENT]
                                      + [jax.ShapeDtypeStruct(_ACC_SHAPES[n], F32) for n in _ACC_NAMES if n in _BIG_RESIDENT]),
        operands=(x, h, h, aux, *[wts[n] for n in names]), comm=comm)
    return outs[0], outs[1], dict(zip(_ACC_NAMES, outs[2:])), extra


def input_grads(dres, dh, x, winT, name, comm=None):
    T = x.shape[0]
    R = ROWS

    def body(dres_ref, dh_ref, x_ref, winT_ref, dx_ref, dwin_ref):
        @pl.when(pl.program_id(0) == 0)
        def _():
            dwin_ref[...] = jnp.zeros(dwin_ref.shape, F32)

        dh = dh_ref[...]
        dx_ref[...] = dres_ref[...] + jnp.dot(dh, winT_ref[...], preferred_element_type=F32)
        dwin_ref[...] += _mm_tn(dh, x_ref[...])

    row = lambda t: (t, 0)
    dwin_shape = jax.ShapeDtypeStruct((N_COLS, D_MODEL), F32)
    (dx, dwinT), extra = _call(
        body, name=name, grid=(T // R,),
        in_specs=[pl.BlockSpec((R, D_MODEL), row), pl.BlockSpec((R, N_COLS), row), pl.BlockSpec((R, D_MODEL), row),
                  _resident()],
        out_specs=[pl.BlockSpec((R, D_MODEL), row), _resident()],
        out_shape=[jax.ShapeDtypeStruct((T, D_MODEL), F32), dwin_shape], scratch_shapes=[],
        compiler_params=_layer_params([winT, dwin_shape]), operands=(dres, dh, x, winT), comm=comm)
    return dx, dwinT, extra


def input_norm_forward(x, g, b, name, comm=None):
    T = x.shape[0]
    R = ROWS

    def body(x_ref, g_ref, b_ref, y_ref):
        y_ref[...] = _ln_fwd(x_ref[...], g_ref[...], b_ref[...])[0]

    row = lambda t: (t, 0)
    (y,), extra = _call(
        body, name=name, grid=(T // R,),
        in_specs=[pl.BlockSpec((R, D_MODEL), row), _const_block(g.shape), _const_block(b.shape)],
        out_specs=[pl.BlockSpec((R, D_MODEL), row)],
        out_shape=[jax.ShapeDtypeStruct((T, D_MODEL), F32)], scratch_shapes=[],
        compiler_params=_layer_params(), operands=(x, g, b), comm=comm)
    return y, extra


def input_norm_backward(x, dy, g, name, comm=None):
    T = x.shape[0]
    R = ROWS

    def body(x_ref, dy_ref, g_ref, dx_ref, dg_ref, db_ref):
        @pl.when(pl.program_id(0) == 0)
        def _():
            dg_ref[...] = jnp.zeros(dg_ref.shape, F32)
            db_ref[...] = jnp.zeros(db_ref.shape, F32)

        dy = dy_ref[...]
        _, xhat, rstd = _ln_fwd(x_ref[...], g_ref[...], g_ref[...])
        dx_ref[...] = _ln_bwd(dy, xhat, rstd, g_ref[...])
        dg_ref[...] += _colsum(dy * xhat)
        db_ref[...] += _colsum(dy)

    row = lambda t: (t, 0)
    vec = jax.ShapeDtypeStruct((1, D_MODEL), F32)
    (dx, dg, db), extra = _call(
        body, name=name, grid=(T // R,),
        in_specs=[pl.BlockSpec((R, D_MODEL), row), pl.BlockSpec((R, D_MODEL), row), _const_block(g.shape)],
        out_specs=[pl.BlockSpec((R, D_MODEL), row), _const_block(vec.shape), _const_block(vec.shape)],
        out_shape=[jax.ShapeDtypeStruct((T, D_MODEL), F32), vec, vec], scratch_shapes=[],
        compiler_params=_layer_params(), operands=(x, dy, g), comm=comm)
    return dx, dg, db, extra


_PIECES = ("winT", "pa", "pb", "wout")
_WHOLE = dict(winT=(N_COLS, D_MODEL), pa=(ATTN_W, D_MODEL), pb=(SGU_W, D_MODEL), wout=(D_MODEL, D_MODEL))
_HALF = dict(winT=(N_COLS // 8, D_MODEL), pa=(ATTN_W // 2, D_MODEL // 4), pb=(SGU_W // 2, D_MODEL // 4),
             wout=(D_MODEL // 8, D_MODEL))
SMALL_ROWS = 1280


def _region(name, ref, k, h):
    hr, hc = _HALF[name]
    if name in ("pa", "pb"):
        return ref.at[pl.ds(h * hr, hr), pl.ds(k * hc, hc)]
    return ref.at[pl.ds(k * 2 * hr + h * hr, hr), :]


def _place():
    x, y, c = lax.axis_index("x"), lax.axis_index("y"), lax.axis_index("c")
    return x, y, c, 2 * x + y


def _as_chip(j, fn):
    for jj in range(N_CHIPS):
        pl.when(j == jj)(functools.partial(fn, jj))


def _remote(src, dst, send_sem, recv_sem, to):
    return pltpu.make_async_remote_copy(src_ref=src, dst_ref=dst, send_sem=send_sem, recv_sem=recv_sem,
                                        device_id=to, device_id_type=MESH)


def _core_of(k, c):
    return (k // 2, k % 2, c)


def _others(jj):
    return [k for k in range(N_CHIPS) if k != jj]


def _start_all(cps):
    for cp in cps:
        cp.start()


def _wait_all(cps):
    for cp in cps:
        cp.wait()


def gather_over_chips(names, shards):
    n = len(names)

    def own(cin, cout, sems, jj):
        x, y, c, _ = _place()
        cps = []
        for i, name in enumerate(names):
            hr, hc = _HALF[name]
            place = (cout[i].at[:, pl.ds(jj * hc, hc)] if name in ("pa", "pb")
                     else cout[i].at[pl.ds(jj * 2 * hr, 2 * hr), :])
            cps.append(_remote(cin[i], place, sems[2].at[i], sems[3].at[i], (x, y, 1 - c)))
        return cps

    def copies(cin, cout, sems, jj, c):
        out = []
        for i, name in enumerate(names):
            hr = _HALF[name][0]
            for k in _others(jj):
                out.append(_remote(cin[i].at[pl.ds(c * hr, hr)], _region(name, cout[i], jj, c),
                                   sems[0].at[4 * i + k], sems[1].at[4 * i + jj], _core_of(k, c)))
        return out

    def start(cin, cout, sems):
        _, _, c, j = _place()
        _as_chip(j, lambda jj: _start_all(copies(cin, cout, sems, jj, c) + own(cin, cout, sems, jj)))

    def finish(cin, cout, sems):
        _, _, c, j = _place()

        def run(jj):
            for i, name in enumerate(names):
                for k in _others(jj):
                    land = _region(name, cout[i], k, c)
                    _remote(land, land, sems[0].at[4 * i + k], sems[1].at[4 * i + k], _core_of(k, c)).wait_recv()
            for cp in copies(cin, cout, sems, jj, c):
                cp.wait_send()
            _wait_all(own(cin, cout, sems, jj))

        _as_chip(j, run)

    shapes = [jax.ShapeDtypeStruct(_WHOLE[nm], s.dtype) for nm, s in zip(names, shards)]
    return _Comm(shards, shapes, [_DMA((4 * n,)), _DMA((4 * n,)), _DMA((n,)), _DMA((n,))], start, finish)


def gather_over_pair(names, wholes):
    n = len(names)

    def start(cin, cout, sems):
        x, y, c, j = _place()

        def run(jj):
            for i, name in enumerate(names):
                for k in _others(jj):
                    land = _region(name, cout[i], k, c)
                    _remote(land, land, sems[0].at[4 * i + k], sems[1].at[4 * i + k], (x, y, 1 - c)).start()

        _as_chip(j, run)

    def finish(cin, cout, sems):
        x, y, c, j = _place()

        def run(jj):
            for i, name in enumerate(names):
                for k in _others(jj):
                    theirs = _region(name, cout[i], k, 1 - c)
                    _remote(theirs, theirs, sems[0].at[4 * i + k], sems[1].at[4 * i + k], (x, y, 1 - c)).wait_recv()
            for i, name in enumerate(names):
                for k in _others(jj):
                    land = _region(name, cout[i], k, c)
                    _remote(land, land, sems[0].at[4 * i + k], sems[1].at[4 * i + k], (x, y, 1 - c)).wait_send()

        _as_chip(j, run)

    shapes = [jax.ShapeDtypeStruct(a.shape, a.dtype) for a in wholes]
    return _Comm(wholes, shapes, [_DMA((4 * n,)), _DMA((4 * n,))], start, finish, aliases={i: i for i in range(n)})


def pair_send_halves(names, parts, small=None, extra=()):
    n = len(names)
    hs = SMALL_ROWS // 2
    first_extra = n + (small is not None)

    def copies(cin, cout, sems):
        x, y, c, _ = _place()
        sib = (x, y, 1 - c)
        cps = []
        for i, name in enumerate(names):
            for k in range(N_CHIPS):
                cps.append(_remote(_region(name, cin[i], k, 1 - c), cout[i].at[k], sems[0].at[4 * i + k],
                                   sems[1].at[4 * i + k], sib))
        if small is not None:
            cps.append(_remote(cin[n].at[pl.ds((1 - c) * hs, hs)], cout[n], sems[0].at[4 * n], sems[1].at[4 * n], sib))
        for e in range(len(extra)):
            cps.append(_remote(cin[first_extra + e], cout[first_extra + e], sems[0].at[4 * n + 1 + e],
                               sems[1].at[4 * n + 1 + e], sib))
        return cps

    start = lambda cin, cout, sems: _start_all(copies(cin, cout, sems))
    finish = lambda cin, cout, sems: _wait_all(copies(cin, cout, sems))
    shapes = [jax.ShapeDtypeStruct((N_CHIPS,) + _HALF[nm], F32) for nm in names]
    ins = list(parts)
    if small is not None:
        shapes.append(jax.ShapeDtypeStruct((hs, 128), F32))
        ins.append(small)
    shapes += [jax.ShapeDtypeStruct(a.shape, a.dtype) for a in extra]
    ins += list(extra)
    n_sem = 4 * n + 1 + len(extra)
    return _Comm(ins, shapes, [_DMA((n_sem,)), _DMA((n_sem,))], start, finish)


def chip_exchange(sums, small=None):
    n = len(sums)

    def copies(cin, cout, sems, jj, c):
        out = []
        for k in _others(jj):
            for i in range(n):
                out.append(_remote(cin[i].at[k], cout[i].at[jj], sems[0].at[4 * i + k], sems[1].at[4 * i + jj], _core_of(k, c)))
            if small is not None:
                out.append(_remote(cin[n], cout[n].at[jj], sems[0].at[4 * n + k], sems[1].at[4 * n + jj], _core_of(k, c)))
        return out

    def start(cin, cout, sems):
        _, _, c, j = _place()

        def run(jj):
            if small is not None:
                pltpu.make_async_copy(cin[n], cout[n].at[jj], sems[2]).start()
            for cp in copies(cin, cout, sems, jj, c):
                cp.start()

        _as_chip(j, run)

    def finish(cin, cout, sems):
        _, _, c, j = _place()

        def run(jj):
            for k in _others(jj):
                for i in range(n):
                    _remote(cin[i].at[k], cout[i].at[k], sems[0].at[4 * i + k], sems[1].at[4 * i + k], _core_of(k, c)).wait_recv()
                if small is not None:
                    _remote(cin[n], cout[n].at[k], sems[0].at[4 * n + k], sems[1].at[4 * n + k], _core_of(k, c)).wait_recv()
            for cp in copies(cin, cout, sems, jj, c):
                cp.wait_send()
            if small is not None:
                pltpu.make_async_copy(cin[n], cout[n].at[jj], sems[2]).wait()

        _as_chip(j, run)

    shapes = [jax.ShapeDtypeStruct(a.shape, a.dtype) for a in sums]
    ins = list(sums)
    if small is not None:
        shapes.append(jax.ShapeDtypeStruct((N_CHIPS,) + small.shape, small.dtype))
        ins.append(small)
    return _Comm(ins, shapes, [_DMA((4 * n + 4,)), _DMA((4 * n + 4,)), _DMA(())], start, finish)


def pair_send_totals(arrs):
    n = len(arrs)

    def copies(cin, cout, sems):
        x, y, c, _ = _place()
        return [_remote(cin[i], cout[i], sems[0].at[i], sems[1].at[i], (x, y, 1 - c)) for i in range(n)]

    start = lambda cin, cout, sems: _start_all(copies(cin, cout, sems))
    finish = lambda cin, cout, sems: _wait_all(copies(cin, cout, sems))
    return _Comm(arrs, [jax.ShapeDtypeStruct(a.shape, a.dtype) for a in arrs], [_DMA((n,)), _DMA((n,))], start, finish)


def _pin(a):
    return pltpu.with_memory_space_constraint(a, pltpu.HBM)


def _plain_params(n_axes):
    return pltpu.CompilerParams(dimension_semantics=("arbitrary",) * n_axes, vmem_limit_bytes=V7X_VMEM_BYTES // 2)


def add_own_halves(names, parts, landed, core, name):
    n = len(names)

    def body(core_ref, *refs):
        for i in range(n):
            refs[2 * n + i][0] = (refs[i][...] + refs[n + i][0]).astype(MXU_DTYPE)

    in_specs, out_specs = [], []
    for nm in names:
        hr, hc = _HALF[nm]
        if nm in ("pa", "pb"):
            in_specs.append(pl.BlockSpec((hr // 2, hc), lambda k, s, cr: (cr[0] * 2 + s, k)))
        else:
            in_specs.append(pl.BlockSpec((hr // 2, hc), lambda k, s, cr: (k * 4 + cr[0] * 2 + s, 0)))
    for nm in names:
        hr, hc = _HALF[nm]
        in_specs.append(pl.BlockSpec((1, hr // 2, hc), lambda k, s, cr: (k, s, 0)))
        out_specs.append(pl.BlockSpec((1, hr // 2, hc), lambda k, s, cr: (k, s, 0)))
    return pl.pallas_call(
        body, name=name,
        grid_spec=pltpu.PrefetchScalarGridSpec(num_scalar_prefetch=1, grid=(N_CHIPS, 2), in_specs=in_specs, out_specs=out_specs),
        out_shape=[pltpu.HBM((N_CHIPS,) + _HALF[nm], MXU_DTYPE) for nm in names],
        compiler_params=_plain_params(2),
    )(core, *[_pin(a) for a in parts], *[_pin(a) for a in landed])


def add_small_half(small, landed, core):
    hs = SMALL_ROWS // 2

    def body(core_ref, a_ref, b_ref, o_ref):
        o_ref[...] = a_ref[...] + b_ref[...]

    return pl.pallas_call(
        body, name="add_small_half",
        grid_spec=pltpu.PrefetchScalarGridSpec(
            num_scalar_prefetch=1, grid=(1,),
            in_specs=[pl.BlockSpec((hs, 128), lambda s, cr: (cr[0], 0)), pl.BlockSpec((hs, 128), lambda s, cr: (0, 0))],
            out_specs=pl.BlockSpec((hs, 128), lambda s, cr: (0, 0))),
        out_shape=jax.ShapeDtypeStruct((hs, 128), F32),
    )(core, small, landed)


def sum_chips(sums, landed, chips, name, small_landed=None):
    n = len(sums)

    def body(chips_ref, *refs):
        ins, outs = refs[:4 * n + (small_landed is not None)], refs[4 * n + (small_landed is not None):]
        for i in range(n):
            tot = ins[4 * i][0].astype(F32)
            for r in range(1, N_CHIPS):
                tot = tot + ins[4 * i + r][0].astype(F32)
            outs[i][...] = tot
        if small_landed is not None:
            @pl.when(pl.program_id(0) == 0)
            def _():
                sm = ins[4 * n]
                outs[n][...] = ((sm[0] + sm[1]) + sm[2]) + sm[3]

    in_specs, out_specs, operands, shapes = [], [], [], []
    for a, l in zip(sums, landed):
        _, hr, hc = a.shape
        for r in range(N_CHIPS):
            in_specs.append(pl.BlockSpec((1, hr // 2, hc), functools.partial(lambda s, ch, r: (ch[r], s, 0), r=r)))
            operands.append(a if r == 0 else l)
        out_specs.append(pl.BlockSpec((hr // 2, hc), lambda s, ch: (s, 0)))
        shapes.append(jax.ShapeDtypeStruct((hr, hc), F32))
    if small_landed is not None:
        in_specs.append(pl.BlockSpec(small_landed.shape, lambda s, ch: (0, 0, 0)))
        out_specs.append(pl.BlockSpec(small_landed.shape[1:], lambda s, ch: (0, 0)))
        operands.append(small_landed)
        shapes.append(jax.ShapeDtypeStruct(small_landed.shape[1:], F32))
    return pl.pallas_call(
        body, name=name,
        grid_spec=pltpu.PrefetchScalarGridSpec(num_scalar_prefetch=1, grid=(2,), in_specs=in_specs, out_specs=out_specs),
        out_shape=[pltpu.HBM(s.shape, s.dtype) for s in shapes], compiler_params=_plain_params(1),
    )(chips, *[_pin(a) for a in operands])


N_STEPS = 8


def _adam_update(w, g, m, v):
    m = ADAM_B1 * m + (1.0 - ADAM_B1) * g
    v = ADAM_B2 * v + (1.0 - ADAM_B2) * (g * g)
    m_hat = m * (1.0 / (1.0 - ADAM_B1 ** ADAM_STEP))
    v_hat = v * (1.0 / (1.0 - ADAM_B2 ** ADAM_STEP))
    return -ADAM_LR * (m_hat / (jnp.sqrt(v_hat) + ADAM_EPS) + ADAM_WD * w), m, v


def adamw_big(ws, g_layers, ms, vs):
    n = len(ws)
    per_layer = N_STEPS // DEPTH

    def body(*refs):
        first = pl.program_id(0) < per_layer
        for i in range(n):
            w, m, v = (refs[k * n + i][...] for k in range(3))
            g = jnp.where(first, refs[3 * n + 2 * i][...], refs[3 * n + 2 * i + 1][...])[None]
            delta, m, v = _adam_update(w, g, m, v)
            for k, val in enumerate((g, delta, m, v)):
                refs[5 * n + k * n + i][...] = val

    whole = lambda a: pl.BlockSpec((1, a.shape[1] // per_layer, a.shape[2]), lambda s: (s // per_layer, s % per_layer, 0))
    layer0 = lambda a: pl.BlockSpec((a.shape[1] // per_layer, a.shape[2]), lambda s: (jnp.minimum(s, per_layer - 1), 0))
    layer1 = lambda a: pl.BlockSpec((a.shape[1] // per_layer, a.shape[2]), lambda s: (jnp.maximum(s - per_layer, 0), 0))
    g_specs, g_ops = [], []
    for a, (g0, g1) in zip(ws, g_layers):
        g_specs += [layer0(a), layer1(a)]
        g_ops += [g0, g1]
    outs = pl.pallas_call(
        body, name="adamw_big", grid=(N_STEPS,),
        in_specs=[whole(a) for a in ws] * 3 + g_specs, out_specs=[whole(a) for a in ws] * 4,
        out_shape=[jax.ShapeDtypeStruct(a.shape, F32) for a in ws] * 4,
        compiler_params=_plain_params(1),
    )(*ws, *ms, *vs, *g_ops)
    return outs[:n], outs[n:2 * n], outs[2 * n:3 * n], outs[3 * n:]


def adamw_small(ws, gs, ms, vs):
    n = len(ws)

    def body(*refs):
        for i in range(n):
            outs = _adam_update(*(refs[k * n + i][...] for k in range(4)))
            for k, val in enumerate(outs):
                refs[4 * n + k * n + i][...] = val

    specs = [_const_block(a.shape) for a in ws]
    outs = pl.pallas_call(
        body, name="adamw_small", grid=(1,), in_specs=specs * 4, out_specs=specs * 3,
        out_shape=[jax.ShapeDtypeStruct(a.shape, F32) for a in ws] * 3,
        compiler_params=_plain_params(1),
    )(*ws, *gs, *ms, *vs)
    return outs[:n], outs[n:2 * n], outs[2 * n:]


_SMALL = (("w_s", (DEPTH, SGU_G, BLK, BLK)), ("ln_in_g", (D_MODEL,)), ("ln_in_b", (D_MODEL,)), ("b_in", (DEPTH, N_COLS)),
          ("sinks", (DEPTH, 8)), ("vn_g", (DEPTH, SGU_W)), ("vn_b", (DEPTH, SGU_W)),
          ("b_s", (DEPTH, SGU_G, BLK)), ("b_out", (DEPTH, D_MODEL)), ("ln_g", (DEPTH, D_MODEL)), ("ln_b", (DEPTH, D_MODEL)))
_BIG = ("w_in", "p_a", "p_b", "w_out")


def _size(shape):
    n = 1
    for d in shape:
        n *= d
    return n


def _pack_small(vals, last_row=None):
    rows = []
    for name, shape in _SMALL:
        v = vals[name]
        if _size(shape) % 128:
            v = jnp.pad(v.reshape((1, -1)), ((0, 0), (0, (-_size(shape)) % 128)))
        rows.append(v.reshape((-1, 128)))
    used = sum(r.shape[0] for r in rows)
    tail = jnp.zeros((SMALL_ROWS - used, 128), F32)
    if last_row is not None:
        tail = tail.at[-1, 0].set(last_row)
    return jnp.concatenate([rows[0], jnp.concatenate(rows[1:] + [tail], axis=0)], axis=0)


def _unpack_small(packed):
    out, pos = {}, 0
    for name, shape in _SMALL:
        n = -(-_size(shape) // 128)
        rows = packed[pos:pos + n]
        out[name] = (rows.reshape((-1,))[:_size(shape)] if _size(shape) % 128 else rows).reshape(shape)
        pos += n
    return out


def kernel(x, ln_in_g, ln_in_b, w_in, b_in, sinks, vn_g, vn_b, w_s, b_s, p_a, p_b, w_out, b_out, ln_g, ln_b, loss_target, m_ln_in_g, m_ln_in_b, m_w_in, m_b_in, m_sinks, m_vn_g, m_vn_b, m_w_s, m_b_s, m_p_a, m_p_b, m_w_out, m_b_out, m_ln_g, m_ln_b, v_ln_in_g, v_ln_in_b, v_w_in, v_b_in, v_sinks, v_vn_g, v_vn_b, v_w_s, v_b_s, v_p_a, v_p_b, v_w_out, v_b_out, v_ln_g, v_ln_b):
    weights = dict(ln_in_g=ln_in_g, ln_in_b=ln_in_b, w_in=w_in, b_in=b_in, sinks=sinks, vn_g=vn_g, vn_b=vn_b, w_s=w_s,
                   b_s=b_s, p_a=p_a, p_b=p_b, w_out=w_out, b_out=b_out, ln_g=ln_g, ln_b=ln_b)
    mom1 = dict(ln_in_g=m_ln_in_g, ln_in_b=m_ln_in_b, w_in=m_w_in, b_in=m_b_in, sinks=m_sinks, vn_g=m_vn_g, vn_b=m_vn_b,
                w_s=m_w_s, b_s=m_b_s, p_a=m_p_a, p_b=m_p_b, w_out=m_w_out, b_out=m_b_out, ln_g=m_ln_g, ln_b=m_ln_b)
    mom2 = dict(ln_in_g=v_ln_in_g, ln_in_b=v_ln_in_b, w_in=v_w_in, b_in=v_b_in, sinks=v_sinks, vn_g=v_vn_g, vn_b=v_vn_b,
                w_s=v_w_s, b_s=v_b_s, p_a=v_p_a, p_b=v_p_b, w_out=v_w_out, b_out=v_b_out, ln_g=v_ln_g, ln_b=v_ln_b)
    n_seq, S, _ = x.shape
    T = n_seq * S
    c = lax.axis_index("c")
    j = 2 * lax.axis_index("x") + lax.axis_index("y")
    core = c.astype(jnp.int32).reshape((1,))
    chips = jnp.stack([j] + [r + (r >= j) for r in range(N_CHIPS - 1)]).astype(jnp.int32)
    names = list(_PIECES)
    xt, tt = x.reshape((T, D_MODEL)), loss_target.reshape((T, D_MODEL))

    tview = lambda d: dict(d, w_in=jnp.swapaxes(d["w_in"], 1, 2))
    weights_t, mom1_t, mom2_t = tview(weights), tview(mom1), tview(mom2)

    def own_shards(l):
        return [weights_t[n][l].astype(MXU_DTYPE) for n in _BIG]

    def layer_weights(l, gathered):
        d = dict(zip(_PIECES, gathered))
        d.update(paT=d["pa"].T, pbT=d["pb"].T, woutT=d["wout"].T)
        d.update(bin=b_in[l][None], sinks=sinks[l], vng=vn_g[l][None], vnb=vn_b[l][None], ws=w_s[l],
                 bs=jnp.repeat(b_s[l].T, 128, axis=1), bout=b_out[l][None], lng=ln_g[l][None], lnb=ln_b[l][None])
        return d

    own0, own1 = own_shards(0), own_shards(1)
    x0, g0a = input_norm_forward(xt, ln_in_g[None], ln_in_b[None], "ln_in_fwd", comm=gather_over_chips(names, own0))
    lw0 = layer_weights(0, _run_comm(gather_over_pair(names, g0a), "gather0_pair"))
    h0, g1a = project(x0, lw0["winT"], lw0["bin"], "layer0_proj", comm=gather_over_chips(names, own1))
    lw1 = layer_weights(1, _run_comm(gather_over_pair(names, g1a), "gather1_pair"))
    x1, h1 = layer_forward_and_projection(x0, h0, lw0, lw1["winT"], lw1["bin"], S, "layer0_fwd_layer1_proj")

    dres1, dh1, acc1, _ = layer_backward(x1, h1, tt, lw1, S, True, "layer1_bwd")
    dx1, dwinT1, _ = input_grads(dres1, dh1, x1, lw1["winT"], "layer1_dx_dwin")
    parts1 = [dwinT1, acc1["dpa"], acc1["dpb"], acc1["dwout"]]
    dres0, dh0, acc0, landed1 = layer_backward(x0, h0, dx1, lw0, S, False, "layer0_bwd", comm=pair_send_halves(names, parts1))
    sums1 = add_own_halves(names, parts1, landed1, core, "add_own_halves1")
    dx0, dwinT0, from_chips1 = input_grads(dres0, dh0, x0, lw0["winT"], "layer0_dx_dwin", comm=chip_exchange(sums1))
    totals1 = sum_chips(sums1, from_chips1, chips, "sum_chips1")
    grad_x, d_ln_in_g, d_ln_in_b, _ = input_norm_backward(xt, dx0, ln_in_g, "ln_in_bwd")

    gl = [acc0, acc1]
    per_layer = lambda key, pick: jnp.stack([pick(gl[l][key]) for l in range(DEPTH)])
    row0 = lambda key: per_layer(key, lambda a: a[0])
    small_part = _pack_small(dict(
        ln_in_g=d_ln_in_g[0], ln_in_b=d_ln_in_b[0], b_in=row0("dbin"), sinks=per_layer("dsink", lambda a: a[:, 0]),
        vn_g=row0("dvng"), vn_b=row0("dvnb"), w_s=per_layer("dws", lambda a: a),
        b_s=per_layer("dbs", lambda a: a[:, ::128].T), b_out=row0("dbout"), ln_g=row0("dlng"), ln_b=row0("dlnb")),
        last_row=acc1["loss"][0, 0])
    parts0 = [dwinT0, acc0["dpa"], acc0["dpb"], acc0["dwout"]]
    first = _run_comm(pair_send_halves(names, parts0, small_part, extra=totals1), "pair_send_halves0")
    landed0, small_landed, sib_totals1 = first[:4], first[4], first[5:]
    sums0 = add_own_halves(names, parts0, landed0, core, "add_own_halves0")
    small_sum = add_small_half(small_part, small_landed, core)
    *from_chips0, small_from_chips = _run_comm(chip_exchange(sums0, small_sum), "chip_exchange0")
    *totals0, small_total = sum_chips(sums0, from_chips0, chips, "sum_chips0", small_landed=small_from_chips)
    *sib_totals0, sib_small = _run_comm(pair_send_totals(totals0 + [small_total]), "pair_send_totals0")

    def both_halves(mine, theirs):
        return jnp.where(c == 0, jnp.concatenate([mine, theirs], axis=0), jnp.concatenate([theirs, mine], axis=0))

    g_layers = [(both_halves(totals0[pi], sib_totals0[pi]), both_halves(totals1[pi], sib_totals1[pi]))
                for pi in range(len(_BIG))]
    g_small = both_halves(small_total, sib_small)
    loss = g_small[-1, 0]

    big = adamw_big([weights_t[n] for n in _BIG], g_layers, [mom1_t[n] for n in _BIG], [mom2_t[n] for n in _BIG])
    small_names = [n for n, _ in _SMALL]
    as2d = lambda a: a.reshape((1, -1)) if a.ndim == 1 else a
    g_named = _unpack_small(g_small)
    small = adamw_small(*[[as2d(d[n]) for n in small_names] for d in (weights, g_named, mom1, mom2)])

    def named(big_outs, small_outs):
        d = dict(zip(_BIG, big_outs))
        d["w_in"] = jnp.swapaxes(d["w_in"], 1, 2)
        d.update({n: o.reshape(weights[n].shape) for n, o in zip(small_names, small_outs)})
        return d

    order = ("ln_in_g", "ln_in_b", "w_in", "b_in", "sinks", "vn_g", "vn_b", "w_s", "b_s", "p_a", "p_b", "w_out", "b_out",
             "ln_g", "ln_b")
    res = [loss, grad_x.reshape(x.shape)]
    groups = [named(big[0], [g_named[n] for n in small_names])] + [named(big[k + 1], small[k]) for k in range(3)]
    for group in groups:
        res.extend(group[n] for n in order)
    return tuple(res)
```

```python
import functools

import jax
import jax.numpy as jnp
from jax import lax
from jax.experimental import pallas as pl
from jax.experimental.pallas import tpu as pltpu

F32 = jnp.float32
MXU_DTYPE = jnp.bfloat16

D_MODEL = 1024
DEPTH = 2
HEAD_DIM = 64
ATTN_W = 512
KV_W = 128
BLK = 128
SGU_W = 512
SGU_G = 4
N_COLS = 4864
C_Q, C_K, C_V, C_GA, C_UB, C_VB, C_GB, C_RA, C_RB = 0, 512, 640, 768, 1280, 1792, 2304, 2816, 3840
ALPHA = (2.0 * DEPTH) ** 0.25
LN_EPS = 1e-5
SCALE = HEAD_DIM ** -0.5
NEG = float(jnp.finfo(jnp.float32).min)
GELU_C = 0.7978845608028654
GELU_A = 0.044715

ADAM_LR, ADAM_B1, ADAM_B2, ADAM_EPS, ADAM_WD, ADAM_STEP = 0.001, 0.9, 0.999, 1e-08, 0.01, 10

V7X_VMEM_BYTES = 64 * 1024 * 1024
V7X_VMEM_RESERVE = 3 * 1024 * 1024
ROWS = 256
ROWS_BWD = 256
ROWS_PROJ = 512

MESH = pl.DeviceIdType.MESH
N_CHIPS = 4


def _mm(a, b):
    return jnp.dot(a.astype(MXU_DTYPE), b.astype(MXU_DTYPE), preferred_element_type=F32)


def _mm_nt(a, b):
    return lax.dot_general(a.astype(MXU_DTYPE), b.astype(MXU_DTYPE), (((1,), (1,)), ((), ())),
                           preferred_element_type=F32)


def _mm_tn(a, b):
    return lax.dot_general(a.astype(MXU_DTYPE), b.astype(MXU_DTYPE), (((0,), (0,)), ((), ())),
                           preferred_element_type=F32)


def _sigmoid(x):
    return 0.5 * jnp.tanh(0.5 * x) + 0.5


def _gelu_parts(x):
    x2 = x * x
    u = 0.5 * jnp.tanh(x * (GELU_C + (GELU_C * GELU_A) * x2)) + 0.5
    dg = u * (1.0 + x * (1.0 - u) * (2.0 * GELU_C + (6.0 * GELU_C * GELU_A) * x2))
    return x * u, dg


def _ln_fwd(z, g, b):
    mu = jnp.mean(z, axis=-1, keepdims=True)
    zc = z - mu
    var = jnp.mean(zc * zc, axis=-1, keepdims=True)
    rstd = lax.rsqrt(var + LN_EPS)
    zhat = zc * rstd
    return zhat * g + b, zhat, rstd


def _ln_bwd(dy, zhat, rstd, g):
    dzh = dy * g
    m1 = jnp.mean(dzh, axis=-1, keepdims=True)
    m2 = jnp.mean(dzh * zhat, axis=-1, keepdims=True)
    return rstd * (dzh - m1 - zhat * m2)


def _colsum(v):
    return jnp.sum(v, axis=0, keepdims=True)


def _lane_lo(rows):
    return lax.broadcasted_iota(jnp.int32, (rows, 128), 1) < HEAD_DIM


def _dup_head(x2, hk, lo):
    xr = pltpu.roll(x2, HEAD_DIM, 1)
    return jnp.where(lo, x2, xr) if hk == 0 else jnp.where(lo, xr, x2)


def _fold_head(y0, y1, lo):
    f0 = y0 + pltpu.roll(y0, HEAD_DIM, 1)
    f1 = y1 + pltpu.roll(y1, HEAD_DIM, 1)
    return jnp.where(lo, f0, f1)


def _stack_heads(t0, t1, lo):
    z = jnp.zeros_like(t0)
    return jnp.concatenate([jnp.where(lo, t0, z), jnp.where(lo, z, t0),
                            jnp.where(lo, t1, z), jnp.where(lo, z, t1)], axis=0)


def _unstack_heads(o, lo):
    return (jnp.where(lo, o[0:128], o[128:256]), jnp.where(lo, o[256:384], o[384:512]))


def _band_bias(start):
    row = lax.broadcasted_iota(jnp.int32, (4 * BLK, 2 * BLK), 0) & (BLK - 1)
    kpos = lax.broadcasted_iota(jnp.int32, (4 * BLK, 2 * BLK), 1)
    valid = (kpos > row) & (kpos <= row + BLK)
    if start is not None:
        valid = valid & (jnp.logical_not(start) | (kpos >= BLK))
    return jnp.where(valid, 0.0, NEG)


def _attn_probs(s, sink4, bias):
    s = s * SCALE + bias
    r1 = lax.broadcasted_iota(jnp.int32, (4 * BLK, 1), 0)
    sk = jnp.where(r1 < BLK, sink4[0], jnp.where(r1 < 2 * BLK, sink4[1], jnp.where(r1 < 3 * BLK, sink4[2], sink4[3])))
    m = jnp.maximum(jnp.max(s, axis=-1, keepdims=True), sk)
    p = jnp.exp(s - m)
    es = jnp.exp(sk - m)
    inv = 1.0 / (jnp.sum(p, axis=-1, keepdims=True) + es)
    return p * inv, es * inv


_PROJ_CHUNK = 512


def _kv_blocks(kvh, h_s, nb):
    ks = [kvh[:, 0:KV_W]] + [h_s[n * BLK:(n + 1) * BLK, C_K:C_K + KV_W] for n in range(nb)]
    vs = [kvh[:, KV_W:2 * KV_W]] + [h_s[n * BLK:(n + 1) * BLK, C_V:C_V + KV_W] for n in range(nb)]
    return ks, vs


def _attn_operands(h_s, ks, vs, n, hk, lo, lo2):
    rows = slice(n * BLK, (n + 1) * BLK)
    kd = _dup_head(jnp.concatenate([ks[n], ks[n + 1]], axis=0), hk, lo2)
    vd = _dup_head(jnp.concatenate([vs[n], vs[n + 1]], axis=0), hk, lo2)
    c0 = C_Q + 2 * hk * 128
    qs = _stack_heads(h_s[rows, c0:c0 + 128], h_s[rows, c0 + 128:c0 + 256], lo)
    return qs, kd, vd


def _sgu_weights(w):
    tri = (lax.broadcasted_iota(jnp.int32, (BLK, BLK), 0) >= lax.broadcasted_iota(jnp.int32, (BLK, BLK), 1))
    return tri, [jnp.where(tri, w["ws"][g], 0.0) for g in range(SGU_G)]


def _layer_forward(x, kvh, start, w, h_s, attn_s, mix_s, keep, tick=lambda: None):
    R = x.shape[0]
    nb = R // BLK
    lo = _lane_lo(BLK)
    lo2 = _lane_lo(2 * BLK)
    ks, vs = _kv_blocks(kvh, h_s, nb)
    sinks = [w["sinks"][j] for j in range(8)]
    bias_first = _band_bias(start)
    bias_rest = _band_bias(None) if nb > 1 else None
    pairs = [(n, hk) for n in range(nb) for hk in range(2)]
    ops = [_attn_operands(h_s, ks, vs, n, hk, lo, lo2) for n, hk in pairs]
    scores = [_mm_nt(qs, kd) for qs, kd, _ in ops]
    probs_l = [_attn_probs(s, sinks[4 * hk:4 * hk + 4], bias_first if n == 0 else bias_rest)
               for s, (n, hk) in zip(scores, pairs)]
    outs = [_mm(p, vd) for (p, _), (_, _, vd) in zip(probs_l, ops)]
    for o, (n, hk) in zip(outs, pairs):
        rows = slice(n * BLK, (n + 1) * BLK)
        t0, t1 = _unstack_heads(o, lo)
        attn_s[rows, 2 * hk * 128:(2 * hk + 1) * 128] = t0
        attn_s[rows, (2 * hk + 1) * 128:(2 * hk + 2) * 128] = t1
        tick()
    ga = h_s[:, C_GA:C_GA + ATTN_W]
    ya = attn_s[...] * (ga * _sigmoid(ga))
    tick()
    gu, _ = _gelu_parts(h_s[:, C_UB:C_UB + SGU_W])
    tick()
    gv, _ = _gelu_parts(h_s[:, C_VB:C_VB + SGU_W])
    vn, _, _ = _ln_fwd(gv, w["vng"][...], w["vnb"][...])
    tick()
    _, wms = _sgu_weights(w)
    tiles = [(slice(n * BLK, (n + 1) * BLK), g, slice(g * 128, (g + 1) * 128)) for n in range(nb) for g in range(SGU_G)]
    mixes = [_mm(wms[g], vn[rows, cols]) for rows, g, cols in tiles]
    for m, (rows, g, cols) in zip(mixes, tiles):
        mix_s[rows, cols] = m + w["bs"][:, cols]
    gb = h_s[:, C_GB:C_GB + SGU_W]
    yb = gu * mix_s[...] * (gb * _sigmoid(gb))
    tick()
    a = _mm(ya, w["pa"][...])
    b = _mm(yb, w["pb"][...])
    tick()
    merged = _sigmoid(h_s[:, C_RA:C_RA + D_MODEL]) * a + _sigmoid(h_s[:, C_RB:C_RB + D_MODEL]) * b
    tick()
    out = _mm(merged, w["wout"][...]) + w["bout"][...]
    y, zhat, rstd = _ln_fwd(ALPHA * x + out, w["lng"][...], w["lnb"][...])
    if not keep:
        return y, None
    return y, dict(ks=ks, vs=vs, probs=probs_l, ya=ya, yb=yb, a=a, b=b, merged=merged, zhat=zhat, rstd=rstd)


def _dsilu(g, sg):
    return sg * (1.0 + g * (1.0 - sg))


_W_NAMES = ("sinks", "vng", "vnb", "ws", "bs", "pa", "pb", "wout", "bout", "lng", "lnb")
_WT_NAMES = ("paT", "pbT", "woutT")
_ACC_NAMES = ("dwout", "dpa", "dpb", "dbin", "dbout", "dlng", "dlnb", "dvng", "dvnb", "dws", "dbs", "dsink", "loss")


def _make_fwd_body(R, S):
    def body(x_ref, h_ref, kvh_ref, *rest):
        w = dict(zip(_W_NAMES, rest[:len(_W_NAMES)]))
        y_ref, attn_s, mix_s = rest[len(_W_NAMES):]
        start = (pl.program_id(0) % (S // R)) == 0
        y, _ = _layer_forward(x_ref[...], kvh_ref[...], start, w, h_ref, attn_s, mix_s, keep=False)
        y_ref[...] = y
    return body


def _make_fwd_proj_body(R, S, nT):
    def body(x_ref, h_ref, kvh_ref, wn_ref, bn_ref, *rest):
        w = dict(zip(_W_NAMES, rest[:len(_W_NAMES)]))
        y_ref, hn_ref, attn_s, mix_s, yprev_s = rest[len(_W_NAMES):]

        @pl.when(pl.program_id(0) == 0)
        def _():
            yprev_s[...] = jnp.zeros(yprev_s.shape, yprev_s.dtype)

        t = jnp.minimum(pl.program_id(0), nT - 1)
        start = (t % (S // R)) == 0
        chunks = [(c0, min(c0 + _PROJ_CHUNK, N_COLS)) for c0 in range(0, N_COLS, _PROJ_CHUNK)]

        def tick():
            if chunks:
                c0, c1 = chunks.pop(0)
                hn_ref[:, c0:c1] = _mm_nt(yprev_s[...], wn_ref[c0:c1, :]) + bn_ref[:, c0:c1]

        y, _ = _layer_forward(x_ref[...], kvh_ref[...], start, w, h_ref, attn_s, mix_s, keep=False, tick=tick)
        while chunks:
            tick()
        y_ref[...] = y
        yprev_s[...] = y.astype(yprev_s.dtype)
    return body


def _make_bwd_body(R, S, nT, is_last):
    nb = R // BLK
    names = _W_NAMES + _WT_NAMES
    nw = len(names)

    def body(x_ref, h_s, kvh_ref, aux_ref, *rest):
        w = dict(zip(names, rest[:nw]))
        dres_ref, dh_ref = rest[nw:nw + 2]
        acc = dict(zip(_ACC_NAMES, rest[nw + 2:nw + 2 + len(_ACC_NAMES)]))
        attn_s, mix_s, ckv_s = rest[nw + 2 + len(_ACC_NAMES):]
        i = pl.program_id(0)
        start = ((nT - 1 - i) % (S // R)) == 0
        lo = _lane_lo(BLK)
        lo2 = _lane_lo(2 * BLK)

        @pl.when(i == 0)
        def _():
            for name in _ACC_NAMES:
                acc[name][...] = jnp.zeros(acc[name].shape, F32)
            ckv_s[...] = jnp.zeros(ckv_s.shape, F32)

        x = x_ref[...]
        y, sv = _layer_forward(x, kvh_ref[...], start, w, h_s, attn_s, mix_s, keep=True)
        if is_last:
            diff = y - aux_ref[...]
            part = 0.5 * jnp.sum(jnp.mean(diff * diff, axis=-1, keepdims=True), axis=0, keepdims=True)
            acc["loss"][...] += jnp.broadcast_to(part, acc["loss"].shape)
            dy = diff * (1.0 / D_MODEL)
        else:
            dy = aux_ref[...]

        def put(c0, val, rows=slice(None)):
            width = val.shape[1]
            dh_ref[rows, c0:c0 + width] = val.astype(dh_ref.dtype)
            acc["dbin"][:, c0:c0 + width] += _colsum(val)

        acc["dlng"][...] += _colsum(dy * sv["zhat"])
        acc["dlnb"][...] += _colsum(dy)
        dz = _ln_bwd(dy, sv["zhat"], sv["rstd"], w["lng"][...])
        dres_ref[...] = ALPHA * dz
        acc["dbout"][...] += _colsum(dz)
        acc["dwout"][...] += _mm_tn(sv["merged"], dz)
        dmerged = _mm(dz, w["woutT"][...])
        sa = _sigmoid(h_s[:, C_RA:C_RA + D_MODEL])
        da = dmerged * sa
        put(C_RA, da * sv["a"] * (1.0 - sa))
        sb = _sigmoid(h_s[:, C_RB:C_RB + D_MODEL])
        db = dmerged * sb
        put(C_RB, db * sv["b"] * (1.0 - sb))
        acc["dpa"][...] += _mm_tn(sv["ya"], da)
        acc["dpb"][...] += _mm_tn(sv["yb"], db)
        dya = _mm(da, w["paT"][...])
        dyb = _mm(db, w["pbT"][...])
        ga = h_s[:, C_GA:C_GA + ATTN_W]
        sga = _sigmoid(ga)
        put(C_GA, dya * attn_s[...] * _dsilu(ga, sga))
        attn_s[...] = dya * (ga * sga)
        ks, vs = sv["ks"], sv["vs"]
        pairs = [(n, hk) for n in range(nb) for hk in range(2)]
        ops = [_attn_operands(h_s, ks, vs, n, hk, lo, lo2) for n, hk in pairs]
        doss = [_stack_heads(attn_s[n * BLK:(n + 1) * BLK, 2 * hk * 128:(2 * hk + 1) * 128],
                             attn_s[n * BLK:(n + 1) * BLK, (2 * hk + 1) * 128:(2 * hk + 2) * 128], lo) for n, hk in pairs]
        dps = [_mm_nt(dos, vd) for dos, (_, _, vd) in zip(doss, ops)]
        gb = h_s[:, C_GB:C_GB + SGU_W]
        sgb = _sigmoid(gb)
        gu, dgu = _gelu_parts(h_s[:, C_UB:C_UB + SGU_W])
        mixed = mix_s[...]
        put(C_GB, dyb * (gu * mixed) * _dsilu(gb, sgb))
        dsgu = dyb * (gb * sgb)
        put(C_UB, dsgu * mixed * dgu)
        dmixed = dsgu * gu
        dsss = []
        for dp, (probs, ps), (n, hk) in zip(dps, sv["probs"], pairs):
            delta = jnp.sum(probs * dp, axis=-1, keepdims=True)
            dsk = -(ps * delta)
            for g in range(4):
                j = 4 * hk + g
                acc["dsink"][j:j + 1, :] += jnp.broadcast_to(_colsum(dsk[g * BLK:(g + 1) * BLK]), (1, 128))
            dsss.append(probs * (dp - delta) * SCALE)
        dqs = [_mm(dss, kd) for dss, (_, kd, _) in zip(dsss, ops)]
        ydks = [_mm_tn(dss, qs) for dss, (qs, _, _) in zip(dsss, ops)]
        ydvs = [_mm_tn(probs, dos) for (probs, _), dos in zip(sv["probs"], doss)]
        gv, dgv = _gelu_parts(h_s[:, C_VB:C_VB + SGU_W])
        vn, vhat, vrstd = _ln_fwd(gv, w["vng"][...], w["vnb"][...])
        tri, wms = _sgu_weights(w)
        tiles = [(slice(n * BLK, (n + 1) * BLK), g, slice(g * 128, (g + 1) * 128)) for n in range(nb) for g in range(SGU_G)]
        dw_parts = [_mm_nt(dmixed[rows, cols], vn[rows, cols]) for rows, g, cols in tiles]
        dvn_parts = [_mm_tn(wms[g], dmixed[rows, cols]) for rows, g, cols in tiles]
        for g in range(SGU_G):
            tot = dw_parts[g]
            for n in range(1, nb):
                tot = tot + dw_parts[n * SGU_G + g]
            acc["dws"][g] += jnp.where(tri, tot, 0.0)
        for part, (rows, g, cols) in zip(dvn_parts, tiles):
            mix_s[rows, cols] = part
        dbs_part = dmixed[0:BLK]
        for n in range(1, nb):
            dbs_part = dbs_part + dmixed[n * BLK:(n + 1) * BLK]
        acc["dbs"][...] += dbs_part
        for dq, (n, hk) in zip(dqs, pairs):
            q0, q1 = _unstack_heads(dq, lo)
            put(C_Q + 2 * hk * 128, q0, slice(n * BLK, (n + 1) * BLK))
            put(C_Q + (2 * hk + 1) * 128, q1, slice(n * BLK, (n + 1) * BLK))
        dks = [jnp.zeros((BLK, KV_W), F32) for _ in range(nb + 1)]
        dvs = [jnp.zeros((BLK, KV_W), F32) for _ in range(nb + 1)]
        for n in range(nb):
            dk2 = _fold_head(ydks[2 * n], ydks[2 * n + 1], lo2)
            dv2 = _fold_head(ydvs[2 * n], ydvs[2 * n + 1], lo2)
            dks[n] = dks[n] + dk2[0:BLK]
            dks[n + 1] = dks[n + 1] + dk2[BLK:2 * BLK]
            dvs[n] = dvs[n] + dv2[0:BLK]
            dvs[n + 1] = dvs[n + 1] + dv2[BLK:2 * BLK]
        dks[nb] = dks[nb] + ckv_s[:, 0:KV_W]
        dvs[nb] = dvs[nb] + ckv_s[:, KV_W:2 * KV_W]
        ckv_s[:, 0:KV_W] = dks[0]
        ckv_s[:, KV_W:2 * KV_W] = dvs[0]
        put(C_K, jnp.concatenate(dks[1:], axis=0))
        put(C_V, jnp.concatenate(dvs[1:], axis=0))
        dvn = mix_s[...]
        acc["dvng"][...] += _colsum(dvn * vhat)
        acc["dvnb"][...] += _colsum(dvn)
        put(C_VB, _ln_bwd(dvn, vhat, vrstd, w["vng"][...]) * dgv)

        @pl.when(i == nT - 1)
        def _():
            for g in range(SGU_G):
                cols = slice(g * 128, (g + 1) * 128)
                tot = jnp.sum(acc["dbs"][:, cols], axis=1, keepdims=True)
                acc["dbs"][:, cols] = jnp.broadcast_to(tot, (BLK, 128))

    return body


SUB_ROWS = BLK
_KEEP = (("mg", D_MODEL), ("dz", D_MODEL), ("ya", ATTN_W), ("da", D_MODEL), ("yb", SGU_W), ("db", D_MODEL))


def _make_bwd_body2(R, S, nT, is_last):
    names = _W_NAMES + _WT_NAMES
    nw = len(names)
    n_sub = R // SUB_ROWS

    def one_block(x, kvh, aux, start, w, h_v, attn_v, mix_v, dres_v, dh_v, acc, carry, keep):
        lo = _lane_lo(BLK)
        lo2 = _lane_lo(2 * BLK)
        y, sv = _layer_forward(x, kvh, start, w, h_v, attn_v, mix_v, keep=True)
        if is_last:
            diff = y - aux
            part = 0.5 * jnp.sum(jnp.mean(diff * diff, axis=-1, keepdims=True), axis=0, keepdims=True)
            acc["loss"][...] += jnp.broadcast_to(part, acc["loss"].shape)
            dy = diff * (1.0 / D_MODEL)
        else:
            dy = aux

        def put(c0, val):
            width = val.shape[1]
            dh_v[:, c0:c0 + width] = val.astype(dh_v.dtype)
            acc["dbin"][:, c0:c0 + width] += _colsum(val)

        acc["dlng"][...] += _colsum(dy * sv["zhat"])
        acc["dlnb"][...] += _colsum(dy)
        dz = _ln_bwd(dy, sv["zhat"], sv["rstd"], w["lng"][...])
        dres_v[...] = ALPHA * dz
        acc["dbout"][...] += _colsum(dz)
        keep["mg"][...] = sv["merged"].astype(MXU_DTYPE)
        keep["dz"][...] = dz.astype(MXU_DTYPE)
        dmerged = _mm(dz, w["woutT"][...])
        sa = _sigmoid(h_v[:, C_RA:C_RA + D_MODEL])
        da = dmerged * sa
        put(C_RA, da * sv["a"] * (1.0 - sa))
        sb = _sigmoid(h_v[:, C_RB:C_RB + D_MODEL])
        db = dmerged * sb
        put(C_RB, db * sv["b"] * (1.0 - sb))
        keep["ya"][...] = sv["ya"].astype(MXU_DTYPE)
        keep["yb"][...] = sv["yb"].astype(MXU_DTYPE)
        keep["da"][...] = da.astype(MXU_DTYPE)
        keep["db"][...] = db.astype(MXU_DTYPE)
        dya = _mm(da, w["paT"][...])
        dyb = _mm(db, w["pbT"][...])
        ga = h_v[:, C_GA:C_GA + ATTN_W]
        sga = _sigmoid(ga)
        put(C_GA, dya * attn_v[...] * _dsilu(ga, sga))
        attn_v[...] = dya * (ga * sga)
        ks, vs = sv["ks"], sv["vs"]
        ydk, ydv = [], []
        for hk in range(2):
            qs, kd, vd = _attn_operands(h_v, ks, vs, 0, hk, lo, lo2)
            probs, ps = sv["probs"][hk]
            dos = _stack_heads(attn_v[:, 2 * hk * 128:(2 * hk + 1) * 128],
                               attn_v[:, (2 * hk + 1) * 128:(2 * hk + 2) * 128], lo)
            dp = _mm_nt(dos, vd)
            delta = jnp.sum(probs * dp, axis=-1, keepdims=True)
            dsk = -(ps * delta)
            for g in range(4):
                j = 4 * hk + g
                acc["dsink"][j:j + 1, :] += jnp.broadcast_to(_colsum(dsk[g * BLK:(g + 1) * BLK]), (1, 128))
            dss = probs * (dp - delta) * SCALE
            q0, q1 = _unstack_heads(_mm(dss, kd), lo)
            put(C_Q + 2 * hk * 128, q0)
            put(C_Q + (2 * hk + 1) * 128, q1)
            ydk.append(_mm_tn(dss, qs))
            ydv.append(_mm_tn(probs, dos))
        dk2 = _fold_head(ydk[0], ydk[1], lo2)
        dv2 = _fold_head(ydv[0], ydv[1], lo2)
        put(C_K, dk2[BLK:2 * BLK] + carry[0])
        put(C_V, dv2[BLK:2 * BLK] + carry[1])
        gb = h_v[:, C_GB:C_GB + SGU_W]
        sgb = _sigmoid(gb)
        gu, dgu = _gelu_parts(h_v[:, C_UB:C_UB + SGU_W])
        mixed = mix_v[...]
        put(C_GB, dyb * (gu * mixed) * _dsilu(gb, sgb))
        dsgu = dyb * (gb * sgb)
        put(C_UB, dsgu * mixed * dgu)
        dmixed = dsgu * gu
        gv, dgv = _gelu_parts(h_v[:, C_VB:C_VB + SGU_W])
        vn, vhat, vrstd = _ln_fwd(gv, w["vng"][...], w["vnb"][...])
        tri, wms = _sgu_weights(w)
        for g in range(SGU_G):
            cols = slice(g * 128, (g + 1) * 128)
            acc["dws"][g] += jnp.where(tri, _mm_nt(dmixed[:, cols], vn[:, cols]), 0.0)
            mix_v[:, cols] = _mm_tn(wms[g], dmixed[:, cols])
        acc["dbs"][...] += dmixed
        dvn = mix_v[...]
        acc["dvng"][...] += _colsum(dvn * vhat)
        acc["dvnb"][...] += _colsum(dvn)
        put(C_VB, _ln_bwd(dvn, vhat, vrstd, w["vng"][...]) * dgv)
        return dk2[0:BLK], dv2[0:BLK]

    def body(x_ref, h_s, kvh_ref, aux_ref, *rest):
        w = dict(zip(names, rest[:nw]))
        dres_ref, dh_ref = rest[nw:nw + 2]
        acc = dict(zip(_ACC_NAMES, rest[nw + 2:nw + 2 + len(_ACC_NAMES)]))
        scratch = rest[nw + 2 + len(_ACC_NAMES):]
        attn_s, mix_s, ckv_s = scratch[:3]
        keep_s = dict(zip([k for k, _ in _KEEP], scratch[3:]))
        i = pl.program_id(0)
        start = ((nT - 1 - i) % (S // R)) == 0

        @pl.when(i == 0)
        def _():
            for name in _ACC_NAMES:
                acc[name][...] = jnp.zeros(acc[name].shape, F32)
            ckv_s[...] = jnp.zeros(ckv_s.shape, F32)

        carry = (ckv_s[:, 0:KV_W], ckv_s[:, KV_W:2 * KV_W])
        for sub in reversed(range(n_sub)):
            rows = pl.ds(sub * SUB_ROWS, SUB_ROWS)
            view = lambda ref: ref.at[rows, :]
            kvh = kvh_ref[...] if sub == 0 else h_s[(sub - 1) * SUB_ROWS:sub * SUB_ROWS, C_K:C_K + 2 * KV_W]
            carry = one_block(x_ref[rows, :], kvh, aux_ref[rows, :], start if sub == 0 else None, w, view(h_s),
                              view(attn_s), view(mix_s), view(dres_ref), view(dh_ref), acc, carry,
                              {k: view(r) for k, r in keep_s.items()})
        ckv_s[:, 0:KV_W] = carry[0]
        ckv_s[:, KV_W:2 * KV_W] = carry[1]
        acc["dwout"][...] += _mm_tn(keep_s["mg"][...], keep_s["dz"][...])
        acc["dpa"][...] += _mm_tn(keep_s["ya"][...], keep_s["da"][...])
        acc["dpb"][...] += _mm_tn(keep_s["yb"][...], keep_s["db"][...])

        @pl.when(i == nT - 1)
        def _():
            for g in range(SGU_G):
                cols = slice(g * 128, (g + 1) * 128)
                tot = jnp.sum(acc["dbs"][:, cols], axis=1, keepdims=True)
                acc["dbs"][:, cols] = jnp.broadcast_to(tot, (BLK, 128))

    return body


def _resident():
    return pl.BlockSpec(memory_space=pltpu.VMEM)


_BIG_RESIDENT = ("pa", "pb", "wout", "paT", "pbT", "woutT", "dwout", "dpa", "dpb")


def _const_block(shape):
    return pl.BlockSpec(tuple(shape), lambda *_: (0,) * len(shape))


def _operand_spec(name, shape):
    if name == "sinks":
        return pl.BlockSpec(memory_space=pltpu.SMEM)
    return _resident() if name in _BIG_RESIDENT else _const_block(shape)


def _weight_specs(names, wts):
    return [_operand_spec(n, wts[n].shape) for n in names]


def _nbytes(a):
    n = jnp.dtype(a.dtype).itemsize
    for d in a.shape:
        n *= d
    return n


def _layer_params(resident=()):
    scoped = V7X_VMEM_BYTES - V7X_VMEM_RESERVE - sum(_nbytes(a) for a in resident)
    return pltpu.CompilerParams(dimension_semantics=("arbitrary",), vmem_limit_bytes=scoped)


class _Comm:
    def __init__(self, ins, out_shapes, sems, start, finish, aliases=None):
        self.ins = [pltpu.with_memory_space_constraint(a, pltpu.HBM) for a in ins]
        self.out_shapes = [pltpu.HBM(s.shape, s.dtype) for s in out_shapes]
        self.sems = list(sems)
        self.start, self.finish, self.aliases = start, finish, dict(aliases or {})


_ANY = pl.BlockSpec(memory_space=pltpu.HBM)
_DMA = pltpu.SemaphoreType.DMA


def _call(body, *, name, grid, in_specs, out_specs, out_shape, scratch_shapes, compiler_params, operands, comm=None):
    if comm is None:
        res = pl.pallas_call(body, name=name, grid=grid, in_specs=in_specs, out_specs=out_specs, out_shape=out_shape,
                             scratch_shapes=scratch_shapes, compiler_params=compiler_params)(*operands)
        return list(res), []
    n_in, n_out, n_scr = len(in_specs), len(out_specs), len(scratch_shapes)
    ci, co = len(comm.ins), len(comm.out_shapes)
    last = grid[0] - 1

    def hosted(*refs):
        ins, cin = refs[:n_in], refs[n_in:n_in + ci]
        p = n_in + ci
        outs, cout = refs[p:p + n_out], refs[p + n_out:p + n_out + co]
        p += n_out + co
        scr, sems = refs[p:p + n_scr], refs[p + n_scr:]
        pl.when(pl.program_id(0) == 0)(lambda: comm.start(cin, cout, sems))
        body(*ins, *outs, *scr)
        pl.when(pl.program_id(0) == last)(lambda: comm.finish(cin, cout, sems))

    res = pl.pallas_call(
        hosted, name=name, grid=grid, in_specs=list(in_specs) + [_ANY] * ci, out_specs=list(out_specs) + [_ANY] * co,
        out_shape=list(out_shape) + comm.out_shapes, scratch_shapes=list(scratch_shapes) + comm.sems,
        input_output_aliases={n_in + a: n_out + b for a, b in comm.aliases.items()},
        compiler_params=compiler_params)(*operands, *comm.ins)
    return list(res[:n_out]), list(res[n_out:])


def _run_comm(comm, name):
    ci, co = len(comm.ins), len(comm.out_shapes)

    def body(*refs):
        cin, cout, sems = refs[:ci], refs[ci:ci + co], refs[ci + co:]
        comm.start(cin, cout, sems)
        comm.finish(cin, cout, sems)

    return list(pl.pallas_call(body, name=name, in_specs=[_ANY] * ci, out_specs=[_ANY] * co, out_shape=comm.out_shapes,
                               scratch_shapes=comm.sems, input_output_aliases=comm.aliases)(*comm.ins))


def project(x, winT, b, name, comm=None):
    T = x.shape[0]
    R = ROWS_PROJ

    def body(x_ref, w_ref, b_ref, h_ref):
        xb = x_ref[...].astype(MXU_DTYPE)
        for c0 in range(0, N_COLS, _PROJ_CHUNK):
            c1 = min(c0 + _PROJ_CHUNK, N_COLS)
            h_ref[:, c0:c1] = _mm_nt(xb, w_ref[c0:c1, :]) + b_ref[:, c0:c1]

    (h,), extra = _call(
        body, name=name, grid=(T // R,),
        in_specs=[pl.BlockSpec((R, D_MODEL), lambda t: (t, 0)), _resident(), _const_block(b.shape)],
        out_specs=[pl.BlockSpec((R, N_COLS), lambda t: (t, 0))],
        out_shape=[jax.ShapeDtypeStruct((T, N_COLS), F32)], scratch_shapes=[],
        compiler_params=_layer_params([winT]), operands=(x, winT, b), comm=comm)
    return h, extra


_KV_BLOCK = C_K // (2 * KV_W)


def layer_forward(x, h, wts, S, name, comm=None):
    T = x.shape[0]
    R = ROWS
    npt = R // BLK
    (y,), extra = _call(
        _make_fwd_body(R, S), name=name, grid=(T // R,),
        in_specs=[pl.BlockSpec((R, D_MODEL), lambda t: (t, 0)), pl.BlockSpec((R, N_COLS), lambda t: (t, 0)),
                  pl.BlockSpec((BLK, 2 * KV_W), lambda t: (jnp.maximum(t * npt - 1, 0), _KV_BLOCK))]
        + _weight_specs(_W_NAMES, wts),
        out_specs=[pl.BlockSpec((R, D_MODEL), lambda t: (t, 0))],
        out_shape=[jax.ShapeDtypeStruct((T, D_MODEL), F32)],
        scratch_shapes=[pltpu.VMEM((R, ATTN_W), F32), pltpu.VMEM((R, SGU_W), F32)],
        compiler_params=_layer_params([wts[n] for n in _W_NAMES if n in _BIG_RESIDENT]),
        operands=(x, h, h, *[wts[n] for n in _W_NAMES]), comm=comm)
    return y, extra


def layer_forward_and_projection(x, h, wts, winT_next, b_next, S, name):
    T = x.shape[0]
    R = ROWS
    nT = T // R
    npt = R // BLK
    cur = lambda t: jnp.minimum(t, nT - 1)
    (y, hn), _ = _call(
        _make_fwd_proj_body(R, S, nT), name=name, grid=(nT + 1,),
        in_specs=[pl.BlockSpec((R, D_MODEL), lambda t: (cur(t), 0)), pl.BlockSpec((R, N_COLS), lambda t: (cur(t), 0)),
                  pl.BlockSpec((BLK, 2 * KV_W), lambda t: (jnp.maximum(cur(t) * npt - 1, 0), _KV_BLOCK)),
                  _resident(), _const_block(b_next.shape)] + _weight_specs(_W_NAMES, wts),
        out_specs=[pl.BlockSpec((R, D_MODEL), lambda t: (cur(t), 0)),
                   pl.BlockSpec((R, N_COLS), lambda t: (jnp.maximum(t - 1, 0), 0))],
        out_shape=[jax.ShapeDtypeStruct((T, D_MODEL), F32), jax.ShapeDtypeStruct((T, N_COLS), F32)],
        scratch_shapes=[pltpu.VMEM((R, ATTN_W), F32), pltpu.VMEM((R, SGU_W), F32), pltpu.VMEM((R, D_MODEL), MXU_DTYPE)],
        compiler_params=_layer_params([winT_next] + [wts[n] for n in _W_NAMES if n in _BIG_RESIDENT]),
        operands=(x, h, h, winT_next, b_next, *[wts[n] for n in _W_NAMES]))
    return y, hn


_ACC_SHAPES = dict(dwout=(D_MODEL, D_MODEL), dpa=(ATTN_W, D_MODEL), dpb=(SGU_W, D_MODEL), dbin=(1, N_COLS),
                   dbout=(1, D_MODEL), dlng=(1, D_MODEL), dlnb=(1, D_MODEL), dvng=(1, SGU_W), dvnb=(1, SGU_W),
                   dws=(SGU_G, BLK, BLK), dbs=(BLK, SGU_W), dsink=(8, 128), loss=(8, 128))


def layer_backward(x, h, aux, wts, S, is_last, name, comm=None):
    T = x.shape[0]
    R = ROWS_BWD
    nT = T // R
    npt = R // BLK
    rev = lambda i: (nT - 1 - i, 0)
    halo = lambda i: (jnp.maximum((nT - 1 - i) * npt - 1, 0), _KV_BLOCK)
    names = _W_NAMES + _WT_NAMES
    out_shape = ([jax.ShapeDtypeStruct((T, D_MODEL), F32), jax.ShapeDtypeStruct((T, N_COLS), MXU_DTYPE)]
                 + [jax.ShapeDtypeStruct(_ACC_SHAPES[n], F32) for n in _ACC_NAMES])
    outs, extra = _call(
        _make_bwd_body(R, S, nT, is_last), name=name, grid=(nT,),
        in_specs=[pl.BlockSpec((R, D_MODEL), rev), pl.BlockSpec((R, N_COLS), rev), pl.BlockSpec((BLK, 2 * KV_W), halo),
                  pl.BlockSpec((R, D_MODEL), rev)] + _weight_specs(names, wts),
        out_specs=[pl.BlockSpec((R, D_MODEL), rev), pl.BlockSpec((R, N_COLS), rev)]
        + [_operand_spec(n, _ACC_SHAPES[n]) for n in _ACC_NAMES],
        out_shape=out_shape,
        scratch_shapes=[pltpu.VMEM((R, ATTN_W), F32), pltpu.VMEM((R, SGU_W), F32), pltpu.VMEM((BLK, 2 * KV_W), F32)],
        compiler_params=_layer_params([wts[n] for n in names if n in _BIG_RESIDENT]
                                      + [jax.ShapeDtypeStruct(_ACC_SHAPES[n], F32) for n in _ACC_NAMES if n in _BIG_RESIDENT]),
        operands=(x, h, h, aux, *[wts[n] for n in names]), comm=comm)
    return outs[0], outs[1], dict(zip(_ACC_NAMES, outs[2:])), extra


def input_grads(dres, dh, x, winT, name, comm=None):
    T = x.shape[0]
    R = ROWS

    def body(dres_ref, dh_ref, x_ref, winT_ref, dx_ref, dwin_ref):
        @pl.when(pl.program_id(0) == 0)
        def _():
            dwin_ref[...] = jnp.zeros(dwin_ref.shape, F32)

        dh = dh_ref[...]
        dx_ref[...] = dres_ref[...] + jnp.dot(dh, winT_ref[...], preferred_element_type=F32)
        dwin_ref[...] += _mm_tn(dh, x_ref[...])

    row = lambda t: (t, 0)
    dwin_shape = jax.ShapeDtypeStruct((N_COLS, D_MODEL), F32)
    (dx, dwinT), extra = _call(
        body, name=name, grid=(T // R,),
        in_specs=[pl.BlockSpec((R, D_MODEL), row), pl.BlockSpec((R, N_COLS), row), pl.BlockSpec((R, D_MODEL), row),
                  _resident()],
        out_specs=[pl.BlockSpec((R, D_MODEL), row), _resident()],
        out_shape=[jax.ShapeDtypeStruct((T, D_MODEL), F32), dwin_shape], scratch_shapes=[],
        compiler_params=_layer_params([winT, dwin_shape]), operands=(dres, dh, x, winT), comm=comm)
    return dx, dwinT, extra


def input_norm_forward(x, g, b, name, comm=None):
    T = x.shape[0]
    R = ROWS

    def body(x_ref, g_ref, b_ref, y_ref):
        y_ref[...] = _ln_fwd(x_ref[...], g_ref[...], b_ref[...])[0]

    row = lambda t: (t, 0)
    (y,), extra = _call(
        body, name=name, grid=(T // R,),
        in_specs=[pl.BlockSpec((R, D_MODEL), row), _const_block(g.shape), _const_block(b.shape)],
        out_specs=[pl.BlockSpec((R, D_MODEL), row)],
        out_shape=[jax.ShapeDtypeStruct((T, D_MODEL), F32)], scratch_shapes=[],
        compiler_params=_layer_params(), operands=(x, g, b), comm=comm)
    return y, extra


def input_norm_backward(x, dy, g, name, comm=None):
    T = x.shape[0]
    R = ROWS

    def body(x_ref, dy_ref, g_ref, dx_ref, dg_ref, db_ref):
        @pl.when(pl.program_id(0) == 0)
        def _():
            dg_ref[...] = jnp.zeros(dg_ref.shape, F32)
            db_ref[...] = jnp.zeros(db_ref.shape, F32)

        dy = dy_ref[...]
        _, xhat, rstd = _ln_fwd(x_ref[...], g_ref[...], g_ref[...])
        dx_ref[...] = _ln_bwd(dy, xhat, rstd, g_ref[...])
        dg_ref[...] += _colsum(dy * xhat)
        db_ref[...] += _colsum(dy)

    row = lambda t: (t, 0)
    vec = jax.ShapeDtypeStruct((1, D_MODEL), F32)
    (dx, dg, db), extra = _call(
        body, name=name, grid=(T // R,),
        in_specs=[pl.BlockSpec((R, D_MODEL), row), pl.BlockSpec((R, D_MODEL), row), _const_block(g.shape)],
        out_specs=[pl.BlockSpec((R, D_MODEL), row), _const_block(vec.shape), _const_block(vec.shape)],
        out_shape=[jax.ShapeDtypeStruct((T, D_MODEL), F32), vec, vec], scratch_shapes=[],
        compiler_params=_layer_params(), operands=(x, dy, g), comm=comm)
    return dx, dg, db, extra


_PIECES = ("winT", "pa", "pb", "wout")
_WHOLE = dict(winT=(N_COLS, D_MODEL), pa=(ATTN_W, D_MODEL), pb=(SGU_W, D_MODEL), wout=(D_MODEL, D_MODEL))
_HALF = dict(winT=(N_COLS // 8, D_MODEL), pa=(ATTN_W // 2, D_MODEL // 4), pb=(SGU_W // 2, D_MODEL // 4),
             wout=(D_MODEL // 8, D_MODEL))
SMALL_ROWS = 1280


def _region(name, ref, k, h):
    hr, hc = _HALF[name]
    if name in ("pa", "pb"):
        return ref.at[pl.ds(h * hr, hr), pl.ds(k * hc, hc)]
    return ref.at[pl.ds(k * 2 * hr + h * hr, hr), :]


def _place():
    x, y, c = lax.axis_index("x"), lax.axis_index("y"), lax.axis_index("c")
    return x, y, c, 2 * x + y


def _as_chip(j, fn):
    for jj in range(N_CHIPS):
        pl.when(j == jj)(functools.partial(fn, jj))


def _remote(src, dst, send_sem, recv_sem, to):
    return pltpu.make_async_remote_copy(src_ref=src, dst_ref=dst, send_sem=send_sem, recv_sem=recv_sem,
                                        device_id=to, device_id_type=MESH)


def _core_of(k, c):
    return (k // 2, k % 2, c)


def _others(jj):
    return [k for k in range(N_CHIPS) if k != jj]


def _start_all(cps):
    for cp in cps:
        cp.start()


def _wait_all(cps):
    for cp in cps:
        cp.wait()


def gather_over_chips(names, shards):
    n = len(names)

    def own(cin, cout, sems, jj):
        x, y, c, _ = _place()
        cps = []
        for i, name in enumerate(names):
            hr, hc = _HALF[name]
            place = (cout[i].at[:, pl.ds(jj * hc, hc)] if name in ("pa", "pb")
                     else cout[i].at[pl.ds(jj * 2 * hr, 2 * hr), :])
            cps.append(_remote(cin[i], place, sems[2].at[i], sems[3].at[i], (x, y, 1 - c)))
        return cps

    def copies(cin, cout, sems, jj, c):
        out = []
        for i, name in enumerate(names):
            hr = _HALF[name][0]
            for k in _others(jj):
                out.append(_remote(cin[i].at[pl.ds(c * hr, hr)], _region(name, cout[i], jj, c),
                                   sems[0].at[4 * i + k], sems[1].at[4 * i + jj], _core_of(k, c)))
        return out

    def start(cin, cout, sems):
        _, _, c, j = _place()
        _as_chip(j, lambda jj: _start_all(copies(cin, cout, sems, jj, c) + own(cin, cout, sems, jj)))

    def finish(cin, cout, sems):
        _, _, c, j = _place()

        def run(jj):
            for i, name in enumerate(names):
                for k in _others(jj):
                    land = _region(name, cout[i], k, c)
                    _remote(land, land, sems[0].at[4 * i + k], sems[1].at[4 * i + k], _core_of(k, c)).wait_recv()
            for cp in copies(cin, cout, sems, jj, c):
                cp.wait_send()
            _wait_all(own(cin, cout, sems, jj))

        _as_chip(j, run)

    shapes = [jax.ShapeDtypeStruct(_WHOLE[nm], s.dtype) for nm, s in zip(names, shards)]
    return _Comm(shards, shapes, [_DMA((4 * n,)), _DMA((4 * n,)), _DMA((n,)), _DMA((n,))], start, finish)


def gather_over_pair(names, wholes):
    n = len(names)

    def start(cin, cout, sems):
        x, y, c, j = _place()

        def run(jj):
            for i, name in enumerate(names):
                for k in _others(jj):
                    land = _region(name, cout[i], k, c)
                    _remote(land, land, sems[0].at[4 * i + k], sems[1].at[4 * i + k], (x, y, 1 - c)).start()

        _as_chip(j, run)

    def finish(cin, cout, sems):
        x, y, c, j = _place()

        def run(jj):
            for i, name in enumerate(names):
                for k in _others(jj):
                    theirs = _region(name, cout[i], k, 1 - c)
                    _remote(theirs, theirs, sems[0].at[4 * i + k], sems[1].at[4 * i + k], (x, y, 1 - c)).wait_recv()
            for i, name in enumerate(names):
                for k in _others(jj):
                    land = _region(name, cout[i], k, c)
                    _remote(land, land, sems[0].at[4 * i + k], sems[1].at[4 * i + k], (x, y, 1 - c)).wait_send()

        _as_chip(j, run)

    shapes = [jax.ShapeDtypeStruct(a.shape, a.dtype) for a in wholes]
    return _Comm(wholes, shapes, [_DMA((4 * n,)), _DMA((4 * n,))], start, finish, aliases={i: i for i in range(n)})


def pair_send_halves(names, parts, small=None, extra=()):
    n = len(names)
    hs = SMALL_ROWS // 2
    first_extra = n + (small is not None)

    def copies(cin, cout, sems):
        x, y, c, _ = _place()
        sib = (x, y, 1 - c)
        cps = []
        for i, name in enumerate(names):
            for k in range(N_CHIPS):
                cps.append(_remote(_region(name, cin[i], k, 1 - c), cout[i].at[k], sems[0].at[4 * i + k],
                                   sems[1].at[4 * i + k], sib))
        if small is not None:
            cps.append(_remote(cin[n].at[pl.ds((1 - c) * hs, hs)], cout[n], sems[0].at[4 * n], sems[1].at[4 * n], sib))
        for e in range(len(extra)):
            cps.append(_remote(cin[first_extra + e], cout[first_extra + e], sems[0].at[4 * n + 1 + e],
                               sems[1].at[4 * n + 1 + e], sib))
        return cps

    start = lambda cin, cout, sems: _start_all(copies(cin, cout, sems))
    finish = lambda cin, cout, sems: _wait_all(copies(cin, cout, sems))
    shapes = [jax.ShapeDtypeStruct((N_CHIPS,) + _HALF[nm], F32) for nm in names]
    ins = list(parts)
    if small is not None:
        shapes.append(jax.ShapeDtypeStruct((hs, 128), F32))
        ins.append(small)
    shapes += [jax.ShapeDtypeStruct(a.shape, a.dtype) for a in extra]
    ins += list(extra)
    n_sem = 4 * n + 1 + len(extra)
    return _Comm(ins, shapes, [_DMA((n_sem,)), _DMA((n_sem,))], start, finish)


def chip_exchange(sums, small=None):
    n = len(sums)

    def copies(cin, cout, sems, jj, c):
        out = []
        for k in _others(jj):
            for i in range(n):
                out.append(_remote(cin[i].at[k], cout[i].at[jj], sems[0].at[4 * i + k], sems[1].at[4 * i + jj], _core_of(k, c)))
            if small is not None:
                out.append(_remote(cin[n], cout[n].at[jj], sems[0].at[4 * n + k], sems[1].at[4 * n + jj], _core_of(k, c)))
        return out

    def start(cin, cout, sems):
        _, _, c, j = _place()

        def run(jj):
            if small is not None:
                pltpu.make_async_copy(cin[n], cout[n].at[jj], sems[2]).start()
            for cp in copies(cin, cout, sems, jj, c):
                cp.start()

        _as_chip(j, run)

    def finish(cin, cout, sems):
        _, _, c, j = _place()

        def run(jj):
            for k in _others(jj):
                for i in range(n):
                    _remote(cin[i].at[k], cout[i].at[k], sems[0].at[4 * i + k], sems[1].at[4 * i + k], _core_of(k, c)).wait_recv()
                if small is not None:
                    _remote(cin[n], cout[n].at[k], sems[0].at[4 * n + k], sems[1].at[4 * n + k], _core_of(k, c)).wait_recv()
            for cp in copies(cin, cout, sems, jj, c):
                cp.wait_send()
            if small is not None:
                pltpu.make_async_copy(cin[n], cout[n].at[jj], sems[2]).wait()

        _as_chip(j, run)

    shapes = [jax.ShapeDtypeStruct(a.shape, a.dtype) for a in sums]
    ins = list(sums)
    if small is not None:
        shapes.append(jax.ShapeDtypeStruct((N_CHIPS,) + small.shape, small.dtype))
        ins.append(small)
    return _Comm(ins, shapes, [_DMA((4 * n + 4,)), _DMA((4 * n + 4,)), _DMA(())], start, finish)


def pair_send_totals(arrs):
    n = len(arrs)

    def copies(cin, cout, sems):
        x, y, c, _ = _place()
        return [_remote(cin[i], cout[i], sems[0].at[i], sems[1].at[i], (x, y, 1 - c)) for i in range(n)]

    start = lambda cin, cout, sems: _start_all(copies(cin, cout, sems))
    finish = lambda cin, cout, sems: _wait_all(copies(cin, cout, sems))
    return _Comm(arrs, [jax.ShapeDtypeStruct(a.shape, a.dtype) for a in arrs], [_DMA((n,)), _DMA((n,))], start, finish)


def _pin(a):
    return pltpu.with_memory_space_constraint(a, pltpu.HBM)


def _plain_params(n_axes):
    return pltpu.CompilerParams(dimension_semantics=("arbitrary",) * n_axes, vmem_limit_bytes=V7X_VMEM_BYTES // 2)


def add_own_halves(names, parts, landed, core, name):
    n = len(names)

    def body(core_ref, *refs):
        for i in range(n):
            refs[2 * n + i][0] = (refs[i][...] + refs[n + i][0]).astype(MXU_DTYPE)

    in_specs, out_specs = [], []
    for nm in names:
        hr, hc = _HALF[nm]
        if nm in ("pa", "pb"):
            in_specs.append(pl.BlockSpec((hr // 2, hc), lambda k, s, cr: (cr[0] * 2 + s, k)))
        else:
            in_specs.append(pl.BlockSpec((hr // 2, hc), lambda k, s, cr: (k * 4 + cr[0] * 2 + s, 0)))
    for nm in names:
        hr, hc = _HALF[nm]
        in_specs.append(pl.BlockSpec((1, hr // 2, hc), lambda k, s, cr: (k, s, 0)))
        out_specs.append(pl.BlockSpec((1, hr // 2, hc), lambda k, s, cr: (k, s, 0)))
    return pl.pallas_call(
        body, name=name,
        grid_spec=pltpu.PrefetchScalarGridSpec(num_scalar_prefetch=1, grid=(N_CHIPS, 2), in_specs=in_specs, out_specs=out_specs),
        out_shape=[pltpu.HBM((N_CHIPS,) + _HALF[nm], MXU_DTYPE) for nm in names],
        compiler_params=_plain_params(2),
    )(core, *[_pin(a) for a in parts], *[_pin(a) for a in landed])


def add_small_half(small, landed, core):
    hs = SMALL_ROWS // 2

    def body(core_ref, a_ref, b_ref, o_ref):
        o_ref[...] = a_ref[...] + b_ref[...]

    return pl.pallas_call(
        body, name="add_small_half",
        grid_spec=pltpu.PrefetchScalarGridSpec(
            num_scalar_prefetch=1, grid=(1,),
            in_specs=[pl.BlockSpec((hs, 128), lambda s, cr: (cr[0], 0)), pl.BlockSpec((hs, 128), lambda s, cr: (0, 0))],
            out_specs=pl.BlockSpec((hs, 128), lambda s, cr: (0, 0))),
        out_shape=jax.ShapeDtypeStruct((hs, 128), F32),
    )(core, small, landed)


def sum_chips(sums, landed, chips, name, small_landed=None):
    n = len(sums)

    def body(chips_ref, *refs):
        ins, outs = refs[:4 * n + (small_landed is not None)], refs[4 * n + (small_landed is not None):]
        for i in range(n):
            tot = ins[4 * i][0].astype(F32)
            for r in range(1, N_CHIPS):
                tot = tot + ins[4 * i + r][0].astype(F32)
            outs[i][...] = tot
        if small_landed is not None:
            @pl.when(pl.program_id(0) == 0)
            def _():
                sm = ins[4 * n]
                outs[n][...] = ((sm[0] + sm[1]) + sm[2]) + sm[3]

    in_specs, out_specs, operands, shapes = [], [], [], []
    for a, l in zip(sums, landed):
        _, hr, hc = a.shape
        for r in range(N_CHIPS):
            in_specs.append(pl.BlockSpec((1, hr // 2, hc), functools.partial(lambda s, ch, r: (ch[r], s, 0), r=r)))
            operands.append(a if r == 0 else l)
        out_specs.append(pl.BlockSpec((hr // 2, hc), lambda s, ch: (s, 0)))
        shapes.append(jax.ShapeDtypeStruct((hr, hc), F32))
    if small_landed is not None:
        in_specs.append(pl.BlockSpec(small_landed.shape, lambda s, ch: (0, 0, 0)))
        out_specs.append(pl.BlockSpec(small_landed.shape[1:], lambda s, ch: (0, 0)))
        operands.append(small_landed)
        shapes.append(jax.ShapeDtypeStruct(small_landed.shape[1:], F32))
    return pl.pallas_call(
        body, name=name,
        grid_spec=pltpu.PrefetchScalarGridSpec(num_scalar_prefetch=1, grid=(2,), in_specs=in_specs, out_specs=out_specs),
        out_shape=[pltpu.HBM(s.shape, s.dtype) for s in shapes], compiler_params=_plain_params(1),
    )(chips, *[_pin(a) for a in operands])


N_STEPS = 8


def _adam_update(w, g, m, v):
    m = ADAM_B1 * m + (1.0 - ADAM_B1) * g
    v = ADAM_B2 * v + (1.0 - ADAM_B2) * (g * g)
    m_hat = m * (1.0 / (1.0 - ADAM_B1 ** ADAM_STEP))
    v_hat = v * (1.0 / (1.0 - ADAM_B2 ** ADAM_STEP))
    return -ADAM_LR * (m_hat / (jnp.sqrt(v_hat) + ADAM_EPS) + ADAM_WD * w), m, v


def adamw_big(ws, g_layers, ms, vs):
    n = len(ws)
    per_layer = N_STEPS // DEPTH

    def body(*refs):
        first = pl.program_id(0) < per_layer
        for i in range(n):
            w, m, v = (refs[k * n + i][...] for k in range(3))
            g = jnp.where(first, refs[3 * n + 2 * i][...], refs[3 * n + 2 * i + 1][...])[None]
            delta, m, v = _adam_update(w, g, m, v)
            for k, val in enumerate((g, delta, m, v)):
                refs[5 * n + k * n + i][...] = val

    whole = lambda a: pl.BlockSpec((1, a.shape[1] // per_layer, a.shape[2]), lambda s: (s // per_layer, s % per_layer, 0))
    layer0 = lambda a: pl.BlockSpec((a.shape[1] // per_layer, a.shape[2]), lambda s: (jnp.minimum(s, per_layer - 1), 0))
    layer1 = lambda a: pl.BlockSpec((a.shape[1] // per_layer, a.shape[2]), lambda s: (jnp.maximum(s - per_layer, 0), 0))
    g_specs, g_ops = [], []
    for a, (g0, g1) in zip(ws, g_layers):
        g_specs += [layer0(a), layer1(a)]
        g_ops += [g0, g1]
    outs = pl.pallas_call(
        body, name="adamw_big", grid=(N_STEPS,),
        in_specs=[whole(a) for a in ws] * 3 + g_specs, out_specs=[whole(a) for a in ws] * 4,
        out_shape=[jax.ShapeDtypeStruct(a.shape, F32) for a in ws] * 4,
        compiler_params=_plain_params(1),
    )(*ws, *ms, *vs, *g_ops)
    return outs[:n], outs[n:2 * n], outs[2 * n:3 * n], outs[3 * n:]


def adamw_small(ws, gs, ms, vs):
    n = len(ws)

    def body(*refs):
        for i in range(n):
            outs = _adam_update(*(refs[k * n + i][...] for k in range(4)))
            for k, val in enumerate(outs):
                refs[4 * n + k * n + i][...] = val

    specs = [_const_block(a.shape) for a in ws]
    outs = pl.pallas_call(
        body, name="adamw_small", grid=(1,), in_specs=specs * 4, out_specs=specs * 3,
        out_shape=[jax.ShapeDtypeStruct(a.shape, F32) for a in ws] * 3,
        compiler_params=_plain_params(1),
    )(*ws, *gs, *ms, *vs)
    return outs[:n], outs[n:2 * n], outs[2 * n:]


_SMALL = (("w_s", (DEPTH, SGU_G, BLK, BLK)), ("ln_in_g", (D_MODEL,)), ("ln_in_b", (D_MODEL,)), ("b_in", (DEPTH, N_COLS)),
          ("sinks", (DEPTH, 8)), ("vn_g", (DEPTH, SGU_W)), ("vn_b", (DEPTH, SGU_W)),
          ("b_s", (DEPTH, SGU_G, BLK)), ("b_out", (DEPTH, D_MODEL)), ("ln_g", (DEPTH, D_MODEL)), ("ln_b", (DEPTH, D_MODEL)))
_BIG = ("w_in", "p_a", "p_b", "w_out")


def _size(shape):
    n = 1
    for d in shape:
        n *= d
    return n


def _pack_small(vals, last_row=None):
    rows = []
    for name, shape in _SMALL:
        v = vals[name]
        if _size(shape) % 128:
            v = jnp.pad(v.reshape((1, -1)), ((0, 0), (0, (-_size(shape)) % 128)))
        rows.append(v.reshape((-1, 128)))
    used = sum(r.shape[0] for r in rows)
    tail = jnp.zeros((SMALL_ROWS - used, 128), F32)
    if last_row is not None:
        tail = tail.at[-1, 0].set(last_row)
    return jnp.concatenate([rows[0], jnp.concatenate(rows[1:] + [tail], axis=0)], axis=0)


def _unpack_small(packed):
    out, pos = {}, 0
    for name, shape in _SMALL:
        n = -(-_size(shape) // 128)
        rows = packed[pos:pos + n]
        out[name] = (rows.reshape((-1,))[:_size(shape)] if _size(shape) % 128 else rows).reshape(shape)
        pos += n
    return out


def kernel(x, ln_in_g, ln_in_b, w_in, b_in, sinks, vn_g, vn_b, w_s, b_s, p_a, p_b, w_out, b_out, ln_g, ln_b, loss_target, m_ln_in_g, m_ln_in_b, m_w_in, m_b_in, m_sinks, m_vn_g, m_vn_b, m_w_s, m_b_s, m_p_a, m_p_b, m_w_out, m_b_out, m_ln_g, m_ln_b, v_ln_in_g, v_ln_in_b, v_w_in, v_b_in, v_sinks, v_vn_g, v_vn_b, v_w_s, v_b_s, v_p_a, v_p_b, v_w_out, v_b_out, v_ln_g, v_ln_b):
    weights = dict(ln_in_g=ln_in_g, ln_in_b=ln_in_b, w_in=w_in, b_in=b_in, sinks=sinks, vn_g=vn_g, vn_b=vn_b, w_s=w_s,
                   b_s=b_s, p_a=p_a, p_b=p_b, w_out=w_out, b_out=b_out, ln_g=ln_g, ln_b=ln_b)
    mom1 = dict(ln_in_g=m_ln_in_g, ln_in_b=m_ln_in_b, w_in=m_w_in, b_in=m_b_in, sinks=m_sinks, vn_g=m_vn_g, vn_b=m_vn_b,
                w_s=m_w_s, b_s=m_b_s, p_a=m_p_a, p_b=m_p_b, w_out=m_w_out, b_out=m_b_out, ln_g=m_ln_g, ln_b=m_ln_b)
    mom2 = dict(ln_in_g=v_ln_in_g, ln_in_b=v_ln_in_b, w_in=v_w_in, b_in=v_b_in, sinks=v_sinks, vn_g=v_vn_g, vn_b=v_vn_b,
                w_s=v_w_s, b_s=v_b_s, p_a=v_p_a, p_b=v_p_b, w_out=v_w_out, b_out=v_b_out, ln_g=v_ln_g, ln_b=v_ln_b)
    n_seq, S, _ = x.shape
    T = n_seq * S
    c = lax.axis_index("c")
    j = 2 * lax.axis_index("x") + lax.axis_index("y")
    core = c.astype(jnp.int32).reshape((1,))
    chips = jnp.stack([j] + [r + (r >= j) for r in range(N_CHIPS - 1)]).astype(jnp.int32)
    names = list(_PIECES)
    xt, tt = x.reshape((T, D_MODEL)), loss_target.reshape((T, D_MODEL))

    tview = lambda d: dict(d, w_in=jnp.swapaxes(d["w_in"], 1, 2))
    weights_t, mom1_t, mom2_t = tview(weights), tview(mom1), tview(mom2)

    def own_shards(l):
        return [weights_t[n][l].astype(MXU_DTYPE) for n in _BIG]

    def layer_weights(l, gathered):
        d = dict(zip(_PIECES, gathered))
        d.update(paT=d["pa"].T, pbT=d["pb"].T, woutT=d["wout"].T)
        d.update(bin=b_in[l][None], sinks=sinks[l], vng=vn_g[l][None], vnb=vn_b[l][None], ws=w_s[l],
                 bs=jnp.repeat(b_s[l].T, 128, axis=1), bout=b_out[l][None], lng=ln_g[l][None], lnb=ln_b[l][None])
        return d

    own0, own1 = own_shards(0), own_shards(1)
    x0, g0a = input_norm_forward(xt, ln_in_g[None], ln_in_b[None], "ln_in_fwd", comm=gather_over_chips(names, own0))
    lw0 = layer_weights(0, _run_comm(gather_over_pair(names, g0a), "gather0_pair"))
    h0, g1a = project(x0, lw0["winT"], lw0["bin"], "layer0_proj", comm=gather_over_chips(names, own1))
    lw1 = layer_weights(1, _run_comm(gather_over_pair(names, g1a), "gather1_pair"))
    x1, h1 = layer_forward_and_projection(x0, h0, lw0, lw1["winT"], lw1["bin"], S, "layer0_fwd_layer1_proj")

    dres1, dh1, acc1, _ = layer_backward(x1, h1, tt, lw1, S, True, "layer1_bwd")
    dx1, dwinT1, _ = input_grads(dres1, dh1, x1, lw1["winT"], "layer1_dx_dwin")
    parts1 = [dwinT1, acc1["dpa"], acc1["dpb"], acc1["dwout"]]
    dres0, dh0, acc0, landed1 = layer_backward(x0, h0, dx1, lw0, S, False, "layer0_bwd", comm=pair_send_halves(names, parts1))
    sums1 = add_own_halves(names, parts1, landed1, core, "add_own_halves1")
    dx0, dwinT0, from_chips1 = input_grads(dres0, dh0, x0, lw0["winT"], "layer0_dx_dwin", comm=chip_exchange(sums1))
    totals1 = sum_chips(sums1, from_chips1, chips, "sum_chips1")
    grad_x, d_ln_in_g, d_ln_in_b, _ = input_norm_backward(xt, dx0, ln_in_g, "ln_in_bwd")

    gl = [acc0, acc1]
    per_layer = lambda key, pick: jnp.stack([pick(gl[l][key]) for l in range(DEPTH)])
    row0 = lambda key: per_layer(key, lambda a: a[0])
    small_part = _pack_small(dict(
        ln_in_g=d_ln_in_g[0], ln_in_b=d_ln_in_b[0], b_in=row0("dbin"), sinks=per_layer("dsink", lambda a: a[:, 0]),
        vn_g=row0("dvng"), vn_b=row0("dvnb"), w_s=per_layer("dws", lambda a: a),
        b_s=per_layer("dbs", lambda a: a[:, ::128].T), b_out=row0("dbout"), ln_g=row0("dlng"), ln_b=row0("dlnb")),
        last_row=acc1["loss"][0, 0])
    parts0 = [dwinT0, acc0["dpa"], acc0["dpb"], acc0["dwout"]]
    first = _run_comm(pair_send_halves(names, parts0, small_part, extra=totals1), "pair_send_halves0")
    landed0, small_landed, sib_totals1 = first[:4], first[4], first[5:]
    sums0 = add_own_halves(names, parts0, landed0, core, "add_own_halves0")
    small_sum = add_small_half(small_part, small_landed, core)
    *from_chips0, small_from_chips = _run_comm(chip_exchange(sums0, small_sum), "chip_exchange0")
    *totals0, small_total = sum_chips(sums0, from_chips0, chips, "sum_chips0", small_landed=small_from_chips)
    *sib_totals0, sib_small = _run_comm(pair_send_totals(totals0 + [small_total]), "pair_send_totals0")

    def both_halves(mine, theirs):
        return jnp.where(c == 0, jnp.concatenate([mine, theirs], axis=0), jnp.concatenate([theirs, mine], axis=0))

    g_layers = [(both_halves(totals0[pi], sib_totals0[pi]), both_halves(totals1[pi], sib_totals1[pi]))
                for pi in range(len(_BIG))]
    g_small = both_halves(small_total, sib_small)
    loss = g_small[-1, 0]

    big = adamw_big([weights_t[n] for n in _BIG], g_layers, [mom1_t[n] for n in _BIG], [mom2_t[n] for n in _BIG])
    small_names = [n for n, _ in _SMALL]
    as2d = lambda a: a.reshape((1, -1)) if a.ndim == 1 else a
    g_named = _unpack_small(g_small)
    small = adamw_small(*[[as2d(d[n]) for n in small_names] for d in (weights, g_named, mom1, mom2)])

    def named(big_outs, small_outs):
        d = dict(zip(_BIG, big_outs))
        d["w_in"] = jnp.swapaxes(d["w_in"], 1, 2)
        d.update({n: o.reshape(weights[n].shape) for n, o in zip(small_names, small_outs)})
        return d

    order = ("ln_in_g", "ln_in_b", "w_in", "b_in", "sinks", "vn_g", "vn_b", "w_s", "b_s", "p_a", "p_b", "w_out", "b_out",
             "ln_g", "ln_b")
    res = [loss, grad_x.reshape(x.shape)]
    groups = [named(big[0], [g_named[n] for n in small_names])] + [named(big[k + 1], small[k]) for k in range(3)]
    for group in groups:
        res.extend(group[n] for n in order)
    return tuple(res)
```

```python
import functools

import jax
import jax.numpy as jnp
from jax import lax
from jax.experimental import pallas as pl
from jax.experimental.pallas import tpu as pltpu

F32 = jnp.float32
MXU_DTYPE = jnp.bfloat16

D_MODEL = 1024
DEPTH = 2
HEAD_DIM = 64
ATTN_W = 512
KV_W = 128
BLK = 128
SGU_W = 512
SGU_G = 4
N_COLS = 4864
C_Q, C_K, C_V, C_GA, C_UB, C_VB, C_GB, C_RA, C_RB = 0, 512, 640, 768, 1280, 1792, 2304, 2816, 3840
ALPHA = (2.0 * DEPTH) ** 0.25
LN_EPS = 1e-5
SCALE = HEAD_DIM ** -0.5
NEG = float(jnp.finfo(jnp.float32).min)
GELU_C = 0.7978845608028654
GELU_A = 0.044715

ADAM_LR, ADAM_B1, ADAM_B2, ADAM_EPS, ADAM_WD, ADAM_STEP = 0.001, 0.9, 0.999, 1e-08, 0.01, 10

V7X_VMEM_BYTES = 64 * 1024 * 1024
V7X_VMEM_RESERVE = 3 * 1024 * 1024
ROWS = 256
ROWS_BWD = 256
ROWS_PROJ = 512

MESH = pl.DeviceIdType.MESH
N_CHIPS = 4


def _mm(a, b):
    return jnp.dot(a.astype(MXU_DTYPE), b.astype(MXU_DTYPE), preferred_element_type=F32)


def _mm_nt(a, b):
    return lax.dot_general(a.astype(MXU_DTYPE), b.astype(MXU_DTYPE), (((1,), (1,)), ((), ())),
                           preferred_element_type=F32)


def _mm_tn(a, b):
    return lax.dot_general(a.astype(MXU_DTYPE), b.astype(MXU_DTYPE), (((0,), (0,)), ((), ())),
                           preferred_element_type=F32)


def _sigmoid(x):
    return 0.5 * jnp.tanh(0.5 * x) + 0.5


def _gelu_parts(x):
    x2 = x * x
    u = 0.5 * jnp.tanh(x * (GELU_C + (GELU_C * GELU_A) * x2)) + 0.5
    dg = u * (1.0 + x * (1.0 - u) * (2.0 * GELU_C + (6.0 * GELU_C * GELU_A) * x2))
    return x * u, dg


def _ln_fwd(z, g, b):
    mu = jnp.mean(z, axis=-1, keepdims=True)
    zc = z - mu
    var = jnp.mean(zc * zc, axis=-1, keepdims=True)
    rstd = lax.rsqrt(var + LN_EPS)
    zhat = zc * rstd
    return zhat * g + b, zhat, rstd


def _ln_bwd(dy, zhat, rstd, g):
    dzh = dy * g
    m1 = jnp.mean(dzh, axis=-1, keepdims=True)
    m2 = jnp.mean(dzh * zhat, axis=-1, keepdims=True)
    return rstd * (dzh - m1 - zhat * m2)


def _colsum(v):
    return jnp.sum(v, axis=0, keepdims=True)


def _lane_lo(rows):
    return lax.broadcasted_iota(jnp.int32, (rows, 128), 1) < HEAD_DIM


def _dup_head(x2, hk, lo):
    xr = pltpu.roll(x2, HEAD_DIM, 1)
    return jnp.where(lo, x2, xr) if hk == 0 else jnp.where(lo, xr, x2)


def _fold_head(y0, y1, lo):
    f0 = y0 + pltpu.roll(y0, HEAD_DIM, 1)
    f1 = y1 + pltpu.roll(y1, HEAD_DIM, 1)
    return jnp.where(lo, f0, f1)


def _stack_heads(t0, t1, lo):
    z = jnp.zeros_like(t0)
    return jnp.concatenate([jnp.where(lo, t0, z), jnp.where(lo, z, t0),
                            jnp.where(lo, t1, z), jnp.where(lo, z, t1)], axis=0)


def _unstack_heads(o, lo):
    return (jnp.where(lo, o[0:128], o[128:256]), jnp.where(lo, o[256:384], o[384:512]))


def _band_bias(start):
    row = lax.broadcasted_iota(jnp.int32, (4 * BLK, 2 * BLK), 0) & (BLK - 1)
    kpos = lax.broadcasted_iota(jnp.int32, (4 * BLK, 2 * BLK), 1)
    valid = (kpos > row) & (kpos <= row + BLK)
    if start is not None:
        valid = valid & (jnp.logical_not(start) | (kpos >= BLK))
    return jnp.where(valid, 0.0, NEG)


def _attn_probs(s, sink4, bias):
    s = s * SCALE + bias
    r1 = lax.broadcasted_iota(jnp.int32, (4 * BLK, 1), 0)
    sk = jnp.where(r1 < BLK, sink4[0], jnp.where(r1 < 2 * BLK, sink4[1], jnp.where(r1 < 3 * BLK, sink4[2], sink4[3])))
    m = jnp.maximum(jnp.max(s, axis=-1, keepdims=True), sk)
    p = jnp.exp(s - m)
    es = jnp.exp(sk - m)
    inv = 1.0 / (jnp.sum(p, axis=-1, keepdims=True) + es)
    return p * inv, es * inv


_PROJ_CHUNK = 512


def _kv_blocks(kvh, h_s, nb):
    ks = [kvh[:, 0:KV_W]] + [h_s[n * BLK:(n + 1) * BLK, C_K:C_K + KV_W] for n in range(nb)]
    vs = [kvh[:, KV_W:2 * KV_W]] + [h_s[n * BLK:(n + 1) * BLK, C_V:C_V + KV_W] for n in range(nb)]
    return ks, vs


def _attn_operands(h_s, ks, vs, n, hk, lo, lo2):
    rows = slice(n * BLK, (n + 1) * BLK)
    kd = _dup_head(jnp.concatenate([ks[n], ks[n + 1]], axis=0), hk, lo2)
    vd = _dup_head(jnp.concatenate([vs[n], vs[n + 1]], axis=0), hk, lo2)
    c0 = C_Q + 2 * hk * 128
    qs = _stack_heads(h_s[rows, c0:c0 + 128], h_s[rows, c0 + 128:c0 + 256], lo)
    return qs, kd, vd


def _sgu_weights(w):
    tri = (lax.broadcasted_iota(jnp.int32, (BLK, BLK), 0) >= lax.broadcasted_iota(jnp.int32, (BLK, BLK), 1))
    return tri, [jnp.where(tri, w["ws"][g], 0.0) for g in range(SGU_G)]


def _layer_forward(x, kvh, start, w, h_s, attn_s, mix_s, keep, tick=lambda: None):
    R = x.shape[0]
    nb = R // BLK
    lo = _lane_lo(BLK)
    lo2 = _lane_lo(2 * BLK)
    ks, vs = _kv_blocks(kvh, h_s, nb)
    sinks = [w["sinks"][j] for j in range(8)]
    bias_first = _band_bias(start)
    bias_rest = _band_bias(None) if nb > 1 else None
    pairs = [(n, hk) for n in range(nb) for hk in range(2)]
    ops = [_attn_operands(h_s, ks, vs, n, hk, lo, lo2) for n, hk in pairs]
    scores = [_mm_nt(qs, kd) for qs, kd, _ in ops]
    probs_l = [_attn_probs(s, sinks[4 * hk:4 * hk + 4], bias_first if n == 0 else bias_rest)
               for s, (n, hk) in zip(scores, pairs)]
    outs = [_mm(p, vd) for (p, _), (_, _, vd) in zip(probs_l, ops)]
    for o, (n, hk) in zip(outs, pairs):
        rows = slice(n * BLK, (n + 1) * BLK)
        t0, t1 = _unstack_heads(o, lo)
        attn_s[rows, 2 * hk * 128:(2 * hk + 1) * 128] = t0
        attn_s[rows, (2 * hk + 1) * 128:(2 * hk + 2) * 128] = t1
        tick()
    ga = h_s[:, C_GA:C_GA + ATTN_W]
    ya = attn_s[...] * (ga * _sigmoid(ga))
    tick()
    gu, _ = _gelu_parts(h_s[:, C_UB:C_UB + SGU_W])
    tick()
    gv, _ = _gelu_parts(h_s[:, C_VB:C_VB + SGU_W])
    vn, _, _ = _ln_fwd(gv, w["vng"][...], w["vnb"][...])
    tick()
    _, wms = _sgu_weights(w)
    tiles = [(slice(n * BLK, (n + 1) * BLK), g, slice(g * 128, (g + 1) * 128)) for n in range(nb) for g in range(SGU_G)]
    mixes = [_mm(wms[g], vn[rows, cols]) for rows, g, cols in tiles]
    for m, (rows, g, cols) in zip(mixes, tiles):
        mix_s[rows, cols] = m + w["bs"][:, cols]
    gb = h_s[:, C_GB:C_GB + SGU_W]
    yb = gu * mix_s[...] * (gb * _sigmoid(gb))
    tick()
    a = _mm(ya, w["pa"][...])
    b = _mm(yb, w["pb"][...])
    tick()
    merged = _sigmoid(h_s[:, C_RA:C_RA + D_MODEL]) * a + _sigmoid(h_s[:, C_RB:C_RB + D_MODEL]) * b
    tick()
    out = _mm(merged, w["wout"][...]) + w["bout"][...]
    y, zhat, rstd = _ln_fwd(ALPHA * x + out, w["lng"][...], w["lnb"][...])
    if not keep:
        return y, None
    return y, dict(ks=ks, vs=vs, probs=probs_l, ya=ya, yb=yb, a=a, b=b, merged=merged, zhat=zhat, rstd=rstd)


def _dsilu(g, sg):
    return sg * (1.0 + g * (1.0 - sg))


_W_NAMES = ("sinks", "vng", "vnb", "ws", "bs", "pa", "pb", "wout", "bout", "lng", "lnb")
_WT_NAMES = ("paT", "pbT", "woutT")
_ACC_NAMES = ("dwout", "dpa", "dpb", "dbin", "dbout", "dlng", "dlnb", "dvng", "dvnb", "dws", "dbs", "dsink", "loss")


def _make_fwd_proj_body(R, S, nT):
    def body(x_ref, h_ref, kvh_ref, wn_ref, bn_ref, *rest):
        w = dict(zip(_W_NAMES, rest[:len(_W_NAMES)]))
        y_ref, hn_ref, attn_s, mix_s, yprev_s = rest[len(_W_NAMES):]

        @pl.when(pl.program_id(0) == 0)
        def _():
            yprev_s[...] = jnp.zeros(yprev_s.shape, yprev_s.dtype)

        t = jnp.minimum(pl.program_id(0), nT - 1)
        start = (t % (S // R)) == 0
        chunks = [(c0, min(c0 + _PROJ_CHUNK, N_COLS)) for c0 in range(0, N_COLS, _PROJ_CHUNK)]

        def tick():
            if chunks:
                c0, c1 = chunks.pop(0)
                hn_ref[:, c0:c1] = _mm_nt(yprev_s[...], wn_ref[c0:c1, :]) + bn_ref[:, c0:c1]

        y, _ = _layer_forward(x_ref[...], kvh_ref[...], start, w, h_ref, attn_s, mix_s, keep=False, tick=tick)
        while chunks:
            tick()
        y_ref[...] = y
        yprev_s[...] = y.astype(yprev_s.dtype)
    return body


def _make_bwd_body(R, S, nT, is_last):
    nb = R // BLK
    names = _W_NAMES + _WT_NAMES
    nw = len(names)

    def body(x_ref, h_s, kvh_ref, aux_ref, *rest):
        w = dict(zip(names, rest[:nw]))
        dres_ref, dh_ref = rest[nw:nw + 2]
        acc = dict(zip(_ACC_NAMES, rest[nw + 2:nw + 2 + len(_ACC_NAMES)]))
        attn_s, mix_s, ckv_s = rest[nw + 2 + len(_ACC_NAMES):]
        i = pl.program_id(0)
        start = ((nT - 1 - i) % (S // R)) == 0
        lo = _lane_lo(BLK)
        lo2 = _lane_lo(2 * BLK)

        @pl.when(i == 0)
        def _():
            for name in _ACC_NAMES:
                acc[name][...] = jnp.zeros(acc[name].shape, F32)
            ckv_s[...] = jnp.zeros(ckv_s.shape, F32)

        x = x_ref[...]
        y, sv = _layer_forward(x, kvh_ref[...], start, w, h_s, attn_s, mix_s, keep=True)
        if is_last:
            diff = y - aux_ref[...]
            part = 0.5 * jnp.sum(jnp.mean(diff * diff, axis=-1, keepdims=True), axis=0, keepdims=True)
            acc["loss"][...] += jnp.broadcast_to(part, acc["loss"].shape)
            dy = diff * (1.0 / D_MODEL)
        else:
            dy = aux_ref[...]

        def put(c0, val, rows=slice(None)):
            width = val.shape[1]
            dh_ref[rows, c0:c0 + width] = val.astype(dh_ref.dtype)
            acc["dbin"][:, c0:c0 + width] += _colsum(val)

        acc["dlng"][...] += _colsum(dy * sv["zhat"])
        acc["dlnb"][...] += _colsum(dy)
        dz = _ln_bwd(dy, sv["zhat"], sv["rstd"], w["lng"][...])
        dres_ref[...] = ALPHA * dz
        acc["dbout"][...] += _colsum(dz)
        acc["dwout"][...] += _mm_tn(sv["merged"], dz)
        dmerged = _mm(dz, w["woutT"][...])
        sa = _sigmoid(h_s[:, C_RA:C_RA + D_MODEL])
        da = dmerged * sa
        put(C_RA, da * sv["a"] * (1.0 - sa))
        sb = _sigmoid(h_s[:, C_RB:C_RB + D_MODEL])
        db = dmerged * sb
        put(C_RB, db * sv["b"] * (1.0 - sb))
        acc["dpa"][...] += _mm_tn(sv["ya"], da)
        acc["dpb"][...] += _mm_tn(sv["yb"], db)
        dya = _mm(da, w["paT"][...])
        dyb = _mm(db, w["pbT"][...])
        ga = h_s[:, C_GA:C_GA + ATTN_W]
        sga = _sigmoid(ga)
        put(C_GA, dya * attn_s[...] * _dsilu(ga, sga))
        attn_s[...] = dya * (ga * sga)
        ks, vs = sv["ks"], sv["vs"]
        pairs = [(n, hk) for n in range(nb) for hk in range(2)]
        ops = [_attn_operands(h_s, ks, vs, n, hk, lo, lo2) for n, hk in pairs]
        doss = [_stack_heads(attn_s[n * BLK:(n + 1) * BLK, 2 * hk * 128:(2 * hk + 1) * 128],
                             attn_s[n * BLK:(n + 1) * BLK, (2 * hk + 1) * 128:(2 * hk + 2) * 128], lo) for n, hk in pairs]
        dps = [_mm_nt(dos, vd) for dos, (_, _, vd) in zip(doss, ops)]
        gb = h_s[:, C_GB:C_GB + SGU_W]
        sgb = _sigmoid(gb)
        gu, dgu = _gelu_parts(h_s[:, C_UB:C_UB + SGU_W])
        mixed = mix_s[...]
        put(C_GB, dyb * (gu * mixed) * _dsilu(gb, sgb))
        dsgu = dyb * (gb * sgb)
        put(C_UB, dsgu * mixed * dgu)
        dmixed = dsgu * gu
        dsss = []
        for dp, (probs, ps), (n, hk) in zip(dps, sv["probs"], pairs):
            delta = jnp.sum(probs * dp, axis=-1, keepdims=True)
            dsk = -(ps * delta)
            for g in range(4):
                j = 4 * hk + g
                acc["dsink"][j:j + 1, :] += jnp.broadcast_to(_colsum(dsk[g * BLK:(g + 1) * BLK]), (1, 128))
            dsss.append(probs * (dp - delta) * SCALE)
        dqs = [_mm(dss, kd) for dss, (_, kd, _) in zip(dsss, ops)]
        ydks = [_mm_tn(dss, qs) for dss, (qs, _, _) in zip(dsss, ops)]
        ydvs = [_mm_tn(probs, dos) for (probs, _), dos in zip(sv["probs"], doss)]
        gv, dgv = _gelu_parts(h_s[:, C_VB:C_VB + SGU_W])
        vn, vhat, vrstd = _ln_fwd(gv, w["vng"][...], w["vnb"][...])
        tri, wms = _sgu_weights(w)
        tiles = [(slice(n * BLK, (n + 1) * BLK), g, slice(g * 128, (g + 1) * 128)) for n in range(nb) for g in range(SGU_G)]
        dw_parts = [_mm_nt(dmixed[rows, cols], vn[rows, cols]) for rows, g, cols in tiles]
        dvn_parts = [_mm_tn(wms[g], dmixed[rows, cols]) for rows, g, cols in tiles]
        for g in range(SGU_G):
            tot = dw_parts[g]
            for n in range(1, nb):
                tot = tot + dw_parts[n * SGU_G + g]
            acc["dws"][g] += jnp.where(tri, tot, 0.0)
        for part, (rows, g, cols) in zip(dvn_parts, tiles):
            mix_s[rows, cols] = part
        dbs_part = dmixed[0:BLK]
        for n in range(1, nb):
            dbs_part = dbs_part + dmixed[n * BLK:(n + 1) * BLK]
        acc["dbs"][...] += dbs_part
        for dq, (n, hk) in zip(dqs, pairs):
            q0, q1 = _unstack_heads(dq, lo)
            put(C_Q + 2 * hk * 128, q0, slice(n * BLK, (n + 1) * BLK))
            put(C_Q + (2 * hk + 1) * 128, q1, slice(n * BLK, (n + 1) * BLK))
        dks = [jnp.zeros((BLK, KV_W), F32) for _ in range(nb + 1)]
        dvs = [jnp.zeros((BLK, KV_W), F32) for _ in range(nb + 1)]
        for n in range(nb):
            dk2 = _fold_head(ydks[2 * n], ydks[2 * n + 1], lo2)
            dv2 = _fold_head(ydvs[2 * n], ydvs[2 * n + 1], lo2)
            dks[n] = dks[n] + dk2[0:BLK]
            dks[n + 1] = dks[n + 1] + dk2[BLK:2 * BLK]
            dvs[n] = dvs[n] + dv2[0:BLK]
            dvs[n + 1] = dvs[n + 1] + dv2[BLK:2 * BLK]
        dks[nb] = dks[nb] + ckv_s[:, 0:KV_W]
        dvs[nb] = dvs[nb] + ckv_s[:, KV_W:2 * KV_W]
        ckv_s[:, 0:KV_W] = dks[0]
        ckv_s[:, KV_W:2 * KV_W] = dvs[0]
        put(C_K, jnp.concatenate(dks[1:], axis=0))
        put(C_V, jnp.concatenate(dvs[1:], axis=0))
        dvn = mix_s[...]
        acc["dvng"][...] += _colsum(dvn * vhat)
        acc["dvnb"][...] += _colsum(dvn)
        put(C_VB, _ln_bwd(dvn, vhat, vrstd, w["vng"][...]) * dgv)

        @pl.when(i == nT - 1)
        def _():
            for g in range(SGU_G):
                cols = slice(g * 128, (g + 1) * 128)
                tot = jnp.sum(acc["dbs"][:, cols], axis=1, keepdims=True)
                acc["dbs"][:, cols] = jnp.broadcast_to(tot, (BLK, 128))

    return body


def _resident():
    return pl.BlockSpec(memory_space=pltpu.VMEM)


_BIG_RESIDENT = ("pa", "pb", "wout", "paT", "pbT", "woutT", "dwout", "dpa", "dpb")


def _const_block(shape):
    return pl.BlockSpec(tuple(shape), lambda *_: (0,) * len(shape))


def _operand_spec(name, shape):
    if name == "sinks":
        return pl.BlockSpec(memory_space=pltpu.SMEM)
    return _resident() if name in _BIG_RESIDENT else _const_block(shape)


def _weight_specs(names, wts):
    return [_operand_spec(n, wts[n].shape) for n in names]


def _nbytes(a):
    n = jnp.dtype(a.dtype).itemsize
    for d in a.shape:
        n *= d
    return n


def _layer_params(resident=()):
    scoped = V7X_VMEM_BYTES - V7X_VMEM_RESERVE - sum(_nbytes(a) for a in resident)
    return pltpu.CompilerParams(dimension_semantics=("arbitrary",), vmem_limit_bytes=scoped)


class _Comm:
    def __init__(self, ins, out_shapes, sems, start, finish, aliases=None, turn=None, turn_step=None):
        self.ins = [pltpu.with_memory_space_constraint(a, pltpu.HBM) for a in ins]
        self.out_shapes = [pltpu.HBM(s.shape, s.dtype) for s in out_shapes]
        self.sems = list(sems)
        self.start, self.finish, self.aliases = start, finish, dict(aliases or {})
        self.turn, self.turn_step = turn, turn_step


_ANY = pl.BlockSpec(memory_space=pltpu.HBM)
_DMA = pltpu.SemaphoreType.DMA


def _call(body, *, name, grid, in_specs, out_specs, out_shape, scratch_shapes, compiler_params, operands, comm=None):
    if comm is None:
        res = pl.pallas_call(body, name=name, grid=grid, in_specs=in_specs, out_specs=out_specs, out_shape=out_shape,
                             scratch_shapes=scratch_shapes, compiler_params=compiler_params)(*operands)
        return list(res), []
    n_in, n_out, n_scr = len(in_specs), len(out_specs), len(scratch_shapes)
    ci, co = len(comm.ins), len(comm.out_shapes)
    last = grid[0] - 1

    def hosted(*refs):
        ins, cin = refs[:n_in], refs[n_in:n_in + ci]
        p = n_in + ci
        outs, cout = refs[p:p + n_out], refs[p + n_out:p + n_out + co]
        p += n_out + co
        scr, sems = refs[p:p + n_scr], refs[p + n_scr:]
        pl.when(pl.program_id(0) == 0)(lambda: comm.start(cin, cout, sems))
        body(*ins, *outs, *scr)
        if comm.turn is not None:
            at = last if comm.turn_step is None else comm.turn_step
            pl.when(pl.program_id(0) == at)(lambda: comm.turn(cin, cout, sems))
        pl.when(pl.program_id(0) == last)(lambda: comm.finish(cin, cout, sems))

    res = pl.pallas_call(
        hosted, name=name, grid=grid, in_specs=list(in_specs) + [_ANY] * ci, out_specs=list(out_specs) + [_ANY] * co,
        out_shape=list(out_shape) + comm.out_shapes, scratch_shapes=list(scratch_shapes) + comm.sems,
        input_output_aliases={n_in + a: n_out + b for a, b in comm.aliases.items()},
        compiler_params=compiler_params)(*operands, *comm.ins)
    return list(res[:n_out]), list(res[n_out:])


def _run_comm(comm, name):
    ci, co = len(comm.ins), len(comm.out_shapes)

    def body(*refs):
        cin, cout, sems = refs[:ci], refs[ci:ci + co], refs[ci + co:]
        comm.start(cin, cout, sems)
        if comm.turn is not None:
            comm.turn(cin, cout, sems)
        comm.finish(cin, cout, sems)

    return list(pl.pallas_call(body, name=name, in_specs=[_ANY] * ci, out_specs=[_ANY] * co, out_shape=comm.out_shapes,
                               scratch_shapes=comm.sems, input_output_aliases=comm.aliases)(*comm.ins))


def project(x, winT, b, name, comm=None):
    T = x.shape[0]
    R = ROWS_PROJ

    def body(x_ref, w_ref, b_ref, h_ref):
        xb = x_ref[...].astype(MXU_DTYPE)
        for c0 in range(0, N_COLS, _PROJ_CHUNK):
            c1 = min(c0 + _PROJ_CHUNK, N_COLS)
            h_ref[:, c0:c1] = _mm_nt(xb, w_ref[c0:c1, :]) + b_ref[:, c0:c1]

    (h,), extra = _call(
        body, name=name, grid=(T // R,),
        in_specs=[pl.BlockSpec((R, D_MODEL), lambda t: (t, 0)), _resident(), _const_block(b.shape)],
        out_specs=[pl.BlockSpec((R, N_COLS), lambda t: (t, 0))],
        out_shape=[jax.ShapeDtypeStruct((T, N_COLS), F32)], scratch_shapes=[],
        compiler_params=_layer_params([winT]), operands=(x, winT, b), comm=comm)
    return h, extra


_KV_BLOCK = C_K // (2 * KV_W)


def layer_forward_and_projection(x, h, wts, winT_next, b_next, S, name):
    T = x.shape[0]
    R = ROWS
    nT = T // R
    npt = R // BLK
    cur = lambda t: jnp.minimum(t, nT - 1)
    (y, hn), _ = _call(
        _make_fwd_proj_body(R, S, nT), name=name, grid=(nT + 1,),
        in_specs=[pl.BlockSpec((R, D_MODEL), lambda t: (cur(t), 0)), pl.BlockSpec((R, N_COLS), lambda t: (cur(t), 0)),
                  pl.BlockSpec((BLK, 2 * KV_W), lambda t: (jnp.maximum(cur(t) * npt - 1, 0), _KV_BLOCK)),
                  _resident(), _const_block(b_next.shape)] + _weight_specs(_W_NAMES, wts),
        out_specs=[pl.BlockSpec((R, D_MODEL), lambda t: (cur(t), 0)),
                   pl.BlockSpec((R, N_COLS), lambda t: (jnp.maximum(t - 1, 0), 0))],
        out_shape=[jax.ShapeDtypeStruct((T, D_MODEL), F32), jax.ShapeDtypeStruct((T, N_COLS), F32)],
        scratch_shapes=[pltpu.VMEM((R, ATTN_W), F32), pltpu.VMEM((R, SGU_W), F32), pltpu.VMEM((R, D_MODEL), MXU_DTYPE)],
        compiler_params=_layer_params([winT_next] + [wts[n] for n in _W_NAMES if n in _BIG_RESIDENT]),
        operands=(x, h, h, winT_next, b_next, *[wts[n] for n in _W_NAMES]))
    return y, hn


_ACC_SHAPES = dict(dwout=(D_MODEL, D_MODEL), dpa=(ATTN_W, D_MODEL), dpb=(SGU_W, D_MODEL), dbin=(1, N_COLS),
                   dbout=(1, D_MODEL), dlng=(1, D_MODEL), dlnb=(1, D_MODEL), dvng=(1, SGU_W), dvnb=(1, SGU_W),
                   dws=(SGU_G, BLK, BLK), dbs=(BLK, SGU_W), dsink=(8, 128), loss=(8, 128))


def layer_backward(x, h, aux, wts, S, is_last, name, comm=None):
    T = x.shape[0]
    R = ROWS_BWD
    nT = T // R
    npt = R // BLK
    rev = lambda i: (nT - 1 - i, 0)
    halo = lambda i: (jnp.maximum((nT - 1 - i) * npt - 1, 0), _KV_BLOCK)
    names = _W_NAMES + _WT_NAMES
    out_shape = ([jax.ShapeDtypeStruct((T, D_MODEL), F32), jax.ShapeDtypeStruct((T, N_COLS), MXU_DTYPE)]
                 + [jax.ShapeDtypeStruct(_ACC_SHAPES[n], F32) for n in _ACC_NAMES])
    outs, extra = _call(
        _make_bwd_body(R, S, nT, is_last), name=name, grid=(nT,),
        in_specs=[pl.BlockSpec((R, D_MODEL), rev), pl.BlockSpec((R, N_COLS), rev), pl.BlockSpec((BLK, 2 * KV_W), halo),
                  pl.BlockSpec((R, D_MODEL), rev)] + _weight_specs(names, wts),
        out_specs=[pl.BlockSpec((R, D_MODEL), rev), pl.BlockSpec((R, N_COLS), rev)]
        + [_operand_spec(n, _ACC_SHAPES[n]) for n in _ACC_NAMES],
        out_shape=out_shape,
        scratch_shapes=[pltpu.VMEM((R, ATTN_W), F32), pltpu.VMEM((R, SGU_W), F32), pltpu.VMEM((BLK, 2 * KV_W), F32)],
        compiler_params=_layer_params([wts[n] for n in names if n in _BIG_RESIDENT]
                                      + [jax.ShapeDtypeStruct(_ACC_SHAPES[n], F32) for n in _ACC_NAMES if n in _BIG_RESIDENT]),
        operands=(x, h, h, aux, *[wts[n] for n in names]), comm=comm)
    return outs[0], outs[1], dict(zip(_ACC_NAMES, outs[2:])), extra


def input_grads(dres, dh, x, winT, name, comm=None):
    T = x.shape[0]
    R = ROWS

    def body(dres_ref, dh_ref, x_ref, winT_ref, dx_ref, dwin_ref):
        @pl.when(pl.program_id(0) == 0)
        def _():
            dwin_ref[...] = jnp.zeros(dwin_ref.shape, F32)

        dh = dh_ref[...]
        dx_ref[...] = dres_ref[...] + jnp.dot(dh, winT_ref[...], preferred_element_type=F32)
        dwin_ref[...] += _mm_tn(dh, x_ref[...])

    row = lambda t: (t, 0)
    dwin_shape = jax.ShapeDtypeStruct((N_COLS, D_MODEL), F32)
    (dx, dwinT), extra = _call(
        body, name=name, grid=(T // R,),
        in_specs=[pl.BlockSpec((R, D_MODEL), row), pl.BlockSpec((R, N_COLS), row), pl.BlockSpec((R, D_MODEL), row),
                  _resident()],
        out_specs=[pl.BlockSpec((R, D_MODEL), row), _resident()],
        out_shape=[jax.ShapeDtypeStruct((T, D_MODEL), F32), dwin_shape], scratch_shapes=[],
        compiler_params=_layer_params([winT, dwin_shape]), operands=(dres, dh, x, winT), comm=comm)
    return dx, dwinT, extra


def input_norm_forward(x, g, b, name, comm=None):
    T = x.shape[0]
    R = ROWS_PROJ

    def body(x_ref, g_ref, b_ref, y_ref):
        y_ref[...] = _ln_fwd(x_ref[...], g_ref[...], b_ref[...])[0]

    row = lambda t: (t, 0)
    (y,), extra = _call(
        body, name=name, grid=(T // R,),
        in_specs=[pl.BlockSpec((R, D_MODEL), row), _const_block(g.shape), _const_block(b.shape)],
        out_specs=[pl.BlockSpec((R, D_MODEL), row)],
        out_shape=[jax.ShapeDtypeStruct((T, D_MODEL), F32)], scratch_shapes=[],
        compiler_params=_layer_params(), operands=(x, g, b), comm=comm)
    return y, extra


def input_norm_backward(x, dy, g, name, comm=None):
    T = x.shape[0]
    R = ROWS_PROJ

    def body(x_ref, dy_ref, g_ref, dx_ref, dg_ref, db_ref):
        @pl.when(pl.program_id(0) == 0)
        def _():
            dg_ref[...] = jnp.zeros(dg_ref.shape, F32)
            db_ref[...] = jnp.zeros(db_ref.shape, F32)

        dy = dy_ref[...]
        _, xhat, rstd = _ln_fwd(x_ref[...], g_ref[...], g_ref[...])
        dx_ref[...] = _ln_bwd(dy, xhat, rstd, g_ref[...])
        dg_ref[...] += _colsum(dy * xhat)
        db_ref[...] += _colsum(dy)

    row = lambda t: (t, 0)
    vec = jax.ShapeDtypeStruct((1, D_MODEL), F32)
    (dx, dg, db), extra = _call(
        body, name=name, grid=(T // R,),
        in_specs=[pl.BlockSpec((R, D_MODEL), row), pl.BlockSpec((R, D_MODEL), row), _const_block(g.shape)],
        out_specs=[pl.BlockSpec((R, D_MODEL), row), _const_block(vec.shape), _const_block(vec.shape)],
        out_shape=[jax.ShapeDtypeStruct((T, D_MODEL), F32), vec, vec], scratch_shapes=[],
        compiler_params=_layer_params(), operands=(x, dy, g), comm=comm)
    return dx, dg, db, extra


_PIECES = ("winT", "pa", "pb", "wout")
_WHOLE = dict(winT=(N_COLS, D_MODEL), pa=(ATTN_W, D_MODEL), pb=(SGU_W, D_MODEL), wout=(D_MODEL, D_MODEL))
_HALF = dict(winT=(N_COLS // 8, D_MODEL), pa=(ATTN_W // 2, D_MODEL // 4), pb=(SGU_W // 2, D_MODEL // 4),
             wout=(D_MODEL // 8, D_MODEL))
SMALL_ROWS = 1280


def _region(name, ref, k, h):
    hr, hc = _HALF[name]
    if name in ("pa", "pb"):
        return ref.at[pl.ds(h * hr, hr), pl.ds(k * hc, hc)]
    return ref.at[pl.ds(k * 2 * hr + h * hr, hr), :]


def _place():
    x, y, c = lax.axis_index("x"), lax.axis_index("y"), lax.axis_index("c")
    return x, y, c, 2 * x + y


def _as_chip(j, fn):
    for jj in range(N_CHIPS):
        pl.when(j == jj)(functools.partial(fn, jj))


def _remote(src, dst, send_sem, recv_sem, to):
    return pltpu.make_async_remote_copy(src_ref=src, dst_ref=dst, send_sem=send_sem, recv_sem=recv_sem,
                                        device_id=to, device_id_type=MESH)


def _core_of(k, c):
    return (k // 2, k % 2, c)


def _others(jj):
    return [k for k in range(N_CHIPS) if k != jj]


def _start_all(cps):
    for cp in cps:
        cp.start()


def _wait_all(cps):
    for cp in cps:
        cp.wait()


def gather_over_chips(names, shards):
    n = len(names)

    def own(cin, cout, sems, jj):
        x, y, c, _ = _place()
        cps = []
        for i, name in enumerate(names):
            hr, hc = _HALF[name]
            place = (cout[i].at[:, pl.ds(jj * hc, hc)] if name in ("pa", "pb")
                     else cout[i].at[pl.ds(jj * 2 * hr, 2 * hr), :])
            cps.append(_remote(cin[i], place, sems[2].at[i], sems[3].at[i], (x, y, 1 - c)))
        return cps

    def copies(cin, cout, sems, jj, c):
        out = []
        for i, name in enumerate(names):
            hr = _HALF[name][0]
            for k in _others(jj):
                out.append(_remote(cin[i].at[pl.ds(c * hr, hr)], _region(name, cout[i], jj, c),
                                   sems[0].at[4 * i + k], sems[1].at[4 * i + jj], _core_of(k, c)))
        return out

    def start(cin, cout, sems):
        _, _, c, j = _place()
        _as_chip(j, lambda jj: _start_all(copies(cin, cout, sems, jj, c) + own(cin, cout, sems, jj)))

    def finish(cin, cout, sems):
        _, _, c, j = _place()

        def run(jj):
            for i, name in enumerate(names):
                for k in _others(jj):
                    land = _region(name, cout[i], k, c)
                    _remote(land, land, sems[0].at[4 * i + k], sems[1].at[4 * i + k], _core_of(k, c)).wait_recv()
            for cp in copies(cin, cout, sems, jj, c):
                cp.wait_send()
            _wait_all(own(cin, cout, sems, jj))

        _as_chip(j, run)

    shapes = [jax.ShapeDtypeStruct(_WHOLE[nm], s.dtype) for nm, s in zip(names, shards)]
    return _Comm(shards, shapes, [_DMA((4 * n,)), _DMA((4 * n,)), _DMA((n,)), _DMA((n,))], start, finish)


def gather_weights(names, shards, turn_step=None):
    n = len(names)
    chips = gather_over_chips(names, shards)
    pair_start, pair_finish = _pair_phase(names)

    def turn(cin, cout, sems):
        chips.finish(cin, cout, sems[:4])
        pair_start(cin, cout, sems[4:])

    return _Comm(shards, chips.out_shapes, chips.sems + [_DMA((4 * n,)), _DMA((4 * n,))], chips.start,
                 lambda cin, cout, sems: pair_finish(cin, cout, sems[4:]), turn=turn, turn_step=turn_step)


def _pair_phase(names):
    def start(cin, cout, sems):
        x, y, c, j = _place()

        def run(jj):
            for i, name in enumerate(names):
                for k in _others(jj):
                    land = _region(name, cout[i], k, c)
                    _remote(land, land, sems[0].at[4 * i + k], sems[1].at[4 * i + k], (x, y, 1 - c)).start()

        _as_chip(j, run)

    def finish(cin, cout, sems):
        x, y, c, j = _place()

        def run(jj):
            for i, name in enumerate(names):
                for k in _others(jj):
                    theirs = _region(name, cout[i], k, 1 - c)
                    _remote(theirs, theirs, sems[0].at[4 * i + k], sems[1].at[4 * i + k], (x, y, 1 - c)).wait_recv()
            for i, name in enumerate(names):
                for k in _others(jj):
                    land = _region(name, cout[i], k, c)
                    _remote(land, land, sems[0].at[4 * i + k], sems[1].at[4 * i + k], (x, y, 1 - c)).wait_send()

        _as_chip(j, run)

    return start, finish


def pair_send_halves(names, parts, small=None, extra=()):
    n = len(names)
    hs = SMALL_ROWS // 2
    first_extra = n + (small is not None)

    def copies(cin, cout, sems):
        x, y, c, _ = _place()
        sib = (x, y, 1 - c)
        cps = []
        for i, name in enumerate(names):
            for k in range(N_CHIPS):
                cps.append(_remote(_region(name, cin[i], k, 1 - c), cout[i].at[k], sems[0].at[4 * i + k],
                                   sems[1].at[4 * i + k], sib))
        if small is not None:
            cps.append(_remote(cin[n].at[pl.ds((1 - c) * hs, hs)], cout[n], sems[0].at[4 * n], sems[1].at[4 * n], sib))
        for e in range(len(extra)):
            cps.append(_remote(cin[first_extra + e], cout[first_extra + e], sems[0].at[4 * n + 1 + e],
                               sems[1].at[4 * n + 1 + e], sib))
        return cps

    start = lambda cin, cout, sems: _start_all(copies(cin, cout, sems))
    finish = lambda cin, cout, sems: _wait_all(copies(cin, cout, sems))
    shapes = [jax.ShapeDtypeStruct((N_CHIPS,) + _HALF[nm], F32) for nm in names]
    ins = list(parts)
    if small is not None:
        shapes.append(jax.ShapeDtypeStruct((hs, 128), F32))
        ins.append(small)
    shapes += [jax.ShapeDtypeStruct(a.shape, a.dtype) for a in extra]
    ins += list(extra)
    n_sem = 4 * n + 1 + len(extra)
    return _Comm(ins, shapes, [_DMA((n_sem,)), _DMA((n_sem,))], start, finish)


def chip_exchange(sums, small=None):
    n = len(sums)

    def copies(cin, cout, sems, jj, c):
        out = []
        for k in _others(jj):
            for i in range(n):
                out.append(_remote(cin[i].at[k], cout[i].at[jj], sems[0].at[4 * i + k], sems[1].at[4 * i + jj], _core_of(k, c)))
            if small is not None:
                out.append(_remote(cin[n], cout[n].at[jj], sems[0].at[4 * n + k], sems[1].at[4 * n + jj], _core_of(k, c)))
        return out

    def start(cin, cout, sems):
        _, _, c, j = _place()

        def run(jj):
            if small is not None:
                pltpu.make_async_copy(cin[n], cout[n].at[jj], sems[2]).start()
            for cp in copies(cin, cout, sems, jj, c):
                cp.start()

        _as_chip(j, run)

    def finish(cin, cout, sems):
        _, _, c, j = _place()

        def run(jj):
            for k in _others(jj):
                for i in range(n):
                    _remote(cin[i].at[k], cout[i].at[k], sems[0].at[4 * i + k], sems[1].at[4 * i + k], _core_of(k, c)).wait_recv()
                if small is not None:
                    _remote(cin[n], cout[n].at[k], sems[0].at[4 * n + k], sems[1].at[4 * n + k], _core_of(k, c)).wait_recv()
            for cp in copies(cin, cout, sems, jj, c):
                cp.wait_send()
            if small is not None:
                pltpu.make_async_copy(cin[n], cout[n].at[jj], sems[2]).wait()

        _as_chip(j, run)

    shapes = [jax.ShapeDtypeStruct(a.shape, a.dtype) for a in sums]
    ins = list(sums)
    if small is not None:
        shapes.append(jax.ShapeDtypeStruct((N_CHIPS,) + small.shape, small.dtype))
        ins.append(small)
    return _Comm(ins, shapes, [_DMA((4 * n + 4,)), _DMA((4 * n + 4,)), _DMA(())], start, finish)


def pair_send_totals(arrs):
    n = len(arrs)

    def copies(cin, cout, sems):
        x, y, c, _ = _place()
        return [_remote(cin[i], cout[i], sems[0].at[i], sems[1].at[i], (x, y, 1 - c)) for i in range(n)]

    start = lambda cin, cout, sems: _start_all(copies(cin, cout, sems))
    finish = lambda cin, cout, sems: _wait_all(copies(cin, cout, sems))
    return _Comm(arrs, [jax.ShapeDtypeStruct(a.shape, a.dtype) for a in arrs], [_DMA((n,)), _DMA((n,))], start, finish)


def _pin(a):
    return pltpu.with_memory_space_constraint(a, pltpu.HBM)


def _plain_params(n_axes):
    return pltpu.CompilerParams(dimension_semantics=("arbitrary",) * n_axes, vmem_limit_bytes=V7X_VMEM_BYTES // 2)


def add_own_halves(names, parts, landed, core, name):
    n = len(names)

    def body(core_ref, *refs):
        for i in range(n):
            refs[2 * n + i][0] = (refs[i][...] + refs[n + i][0]).astype(MXU_DTYPE)

    in_specs, out_specs = [], []
    for nm in names:
        hr, hc = _HALF[nm]
        if nm in ("pa", "pb"):
            in_specs.append(pl.BlockSpec((hr // 2, hc), lambda k, s, cr: (cr[0] * 2 + s, k)))
        else:
            in_specs.append(pl.BlockSpec((hr // 2, hc), lambda k, s, cr: (k * 4 + cr[0] * 2 + s, 0)))
    for nm in names:
        hr, hc = _HALF[nm]
        in_specs.append(pl.BlockSpec((1, hr // 2, hc), lambda k, s, cr: (k, s, 0)))
        out_specs.append(pl.BlockSpec((1, hr // 2, hc), lambda k, s, cr: (k, s, 0)))
    return pl.pallas_call(
        body, name=name,
        grid_spec=pltpu.PrefetchScalarGridSpec(num_scalar_prefetch=1, grid=(N_CHIPS, 2), in_specs=in_specs, out_specs=out_specs),
        out_shape=[pltpu.HBM((N_CHIPS,) + _HALF[nm], MXU_DTYPE) for nm in names],
        compiler_params=_plain_params(2),
    )(core, *[_pin(a) for a in parts], *[_pin(a) for a in landed])


def add_small_half(small, landed, core):
    hs = SMALL_ROWS // 2

    def body(core_ref, a_ref, b_ref, o_ref):
        o_ref[...] = a_ref[...] + b_ref[...]

    return pl.pallas_call(
        body, name="add_small_half",
        grid_spec=pltpu.PrefetchScalarGridSpec(
            num_scalar_prefetch=1, grid=(1,),
            in_specs=[pl.BlockSpec((hs, 128), lambda s, cr: (cr[0], 0)), pl.BlockSpec((hs, 128), lambda s, cr: (0, 0))],
            out_specs=pl.BlockSpec((hs, 128), lambda s, cr: (0, 0))),
        out_shape=jax.ShapeDtypeStruct((hs, 128), F32),
    )(core, small, landed)


def sum_chips(sums, landed, chips, name, small_landed=None):
    n = len(sums)

    def body(chips_ref, *refs):
        ins, outs = refs[:4 * n + (small_landed is not None)], refs[4 * n + (small_landed is not None):]
        for i in range(n):
            tot = ins[4 * i][0].astype(F32)
            for r in range(1, N_CHIPS):
                tot = tot + ins[4 * i + r][0].astype(F32)
            outs[i][...] = tot
        if small_landed is not None:
            @pl.when(pl.program_id(0) == 0)
            def _():
                sm = ins[4 * n]
                outs[n][...] = ((sm[0] + sm[1]) + sm[2]) + sm[3]

    in_specs, out_specs, operands, shapes = [], [], [], []
    for a, l in zip(sums, landed):
        _, hr, hc = a.shape
        for r in range(N_CHIPS):
            in_specs.append(pl.BlockSpec((1, hr // 2, hc), functools.partial(lambda s, ch, r: (ch[r], s, 0), r=r)))
            operands.append(a if r == 0 else l)
        out_specs.append(pl.BlockSpec((hr // 2, hc), lambda s, ch: (s, 0)))
        shapes.append(jax.ShapeDtypeStruct((hr, hc), F32))
    if small_landed is not None:
        in_specs.append(pl.BlockSpec(small_landed.shape, lambda s, ch: (0, 0, 0)))
        out_specs.append(pl.BlockSpec(small_landed.shape[1:], lambda s, ch: (0, 0)))
        operands.append(small_landed)
        shapes.append(jax.ShapeDtypeStruct(small_landed.shape[1:], F32))
    return pl.pallas_call(
        body, name=name,
        grid_spec=pltpu.PrefetchScalarGridSpec(num_scalar_prefetch=1, grid=(2,), in_specs=in_specs, out_specs=out_specs),
        out_shape=[pltpu.HBM(s.shape, s.dtype) for s in shapes], compiler_params=_plain_params(1),
    )(chips, *[_pin(a) for a in operands])


N_STEPS = 8


def _adam_update(w, g, m, v):
    m = ADAM_B1 * m + (1.0 - ADAM_B1) * g
    v = ADAM_B2 * v + (1.0 - ADAM_B2) * (g * g)
    m_hat = m * (1.0 / (1.0 - ADAM_B1 ** ADAM_STEP))
    v_hat = v * (1.0 / (1.0 - ADAM_B2 ** ADAM_STEP))
    return -ADAM_LR * (m_hat / (jnp.sqrt(v_hat) + ADAM_EPS) + ADAM_WD * w), m, v


def adamw_big(ws, g_layers, ms, vs):
    n = len(ws)
    per_layer = N_STEPS // DEPTH

    def body(*refs):
        first = pl.program_id(0) < per_layer
        for i in range(n):
            w, m, v = (refs[k * n + i][...] for k in range(3))
            g = jnp.where(first, refs[3 * n + 2 * i][...], refs[3 * n + 2 * i + 1][...])[None]
            delta, m, v = _adam_update(w, g, m, v)
            for k, val in enumerate((g, delta, m, v)):
                refs[5 * n + k * n + i][...] = val

    whole = lambda a: pl.BlockSpec((1, a.shape[1] // per_layer, a.shape[2]), lambda s: (s // per_layer, s % per_layer, 0))
    layer0 = lambda a: pl.BlockSpec((a.shape[1] // per_layer, a.shape[2]), lambda s: (jnp.minimum(s, per_layer - 1), 0))
    layer1 = lambda a: pl.BlockSpec((a.shape[1] // per_layer, a.shape[2]), lambda s: (jnp.maximum(s - per_layer, 0), 0))
    g_specs, g_ops = [], []
    for a, (g0, g1) in zip(ws, g_layers):
        g_specs += [layer0(a), layer1(a)]
        g_ops += [g0, g1]
    outs = pl.pallas_call(
        body, name="adamw_big", grid=(N_STEPS,),
        in_specs=[whole(a) for a in ws] * 3 + g_specs, out_specs=[whole(a) for a in ws] * 4,
        out_shape=[jax.ShapeDtypeStruct(a.shape, F32) for a in ws] * 4,
        compiler_params=_plain_params(1),
    )(*ws, *ms, *vs, *g_ops)
    return outs[:n], outs[n:2 * n], outs[2 * n:3 * n], outs[3 * n:]


def adamw_small(ws, gs, ms, vs):
    n = len(ws)

    def body(*refs):
        for i in range(n):
            outs = _adam_update(*(refs[k * n + i][...] for k in range(4)))
            for k, val in enumerate(outs):
                refs[4 * n + k * n + i][...] = val

    specs = [_const_block(a.shape) for a in ws]
    outs = pl.pallas_call(
        body, name="adamw_small", grid=(1,), in_specs=specs * 4, out_specs=specs * 3,
        out_shape=[jax.ShapeDtypeStruct(a.shape, F32) for a in ws] * 3,
        compiler_params=_plain_params(1),
    )(*ws, *gs, *ms, *vs)
    return outs[:n], outs[n:2 * n], outs[2 * n:]


_SMALL = (("w_s", (DEPTH, SGU_G, BLK, BLK)), ("ln_in_g", (D_MODEL,)), ("ln_in_b", (D_MODEL,)), ("b_in", (DEPTH, N_COLS)),
          ("sinks", (DEPTH, 8)), ("vn_g", (DEPTH, SGU_W)), ("vn_b", (DEPTH, SGU_W)),
          ("b_s", (DEPTH, SGU_G, BLK)), ("b_out", (DEPTH, D_MODEL)), ("ln_g", (DEPTH, D_MODEL)), ("ln_b", (DEPTH, D_MODEL)))
_BIG = ("w_in", "p_a", "p_b", "w_out")


def _size(shape):
    n = 1
    for d in shape:
        n *= d
    return n


def _pack_small(vals, last_row=None):
    rows = []
    for name, shape in _SMALL:
        v = vals[name]
        if _size(shape) % 128:
            v = jnp.pad(v.reshape((1, -1)), ((0, 0), (0, (-_size(shape)) % 128)))
        rows.append(v.reshape((-1, 128)))
    used = sum(r.shape[0] for r in rows)
    tail = jnp.zeros((SMALL_ROWS - used, 128), F32)
    if last_row is not None:
        tail = tail.at[-1, 0].set(last_row)
    return jnp.concatenate([rows[0], jnp.concatenate(rows[1:] + [tail], axis=0)], axis=0)


def _unpack_small(packed):
    out, pos = {}, 0
    for name, shape in _SMALL:
        n = -(-_size(shape) // 128)
        rows = packed[pos:pos + n]
        out[name] = (rows.reshape((-1,))[:_size(shape)] if _size(shape) % 128 else rows).reshape(shape)
        pos += n
    return out


def kernel(x, ln_in_g, ln_in_b, w_in, b_in, sinks, vn_g, vn_b, w_s, b_s, p_a, p_b, w_out, b_out, ln_g, ln_b, loss_target, m_ln_in_g, m_ln_in_b, m_w_in, m_b_in, m_sinks, m_vn_g, m_vn_b, m_w_s, m_b_s, m_p_a, m_p_b, m_w_out, m_b_out, m_ln_g, m_ln_b, v_ln_in_g, v_ln_in_b, v_w_in, v_b_in, v_sinks, v_vn_g, v_vn_b, v_w_s, v_b_s, v_p_a, v_p_b, v_w_out, v_b_out, v_ln_g, v_ln_b):
    weights = dict(ln_in_g=ln_in_g, ln_in_b=ln_in_b, w_in=w_in, b_in=b_in, sinks=sinks, vn_g=vn_g, vn_b=vn_b, w_s=w_s,
                   b_s=b_s, p_a=p_a, p_b=p_b, w_out=w_out, b_out=b_out, ln_g=ln_g, ln_b=ln_b)
    mom1 = dict(ln_in_g=m_ln_in_g, ln_in_b=m_ln_in_b, w_in=m_w_in, b_in=m_b_in, sinks=m_sinks, vn_g=m_vn_g, vn_b=m_vn_b,
                w_s=m_w_s, b_s=m_b_s, p_a=m_p_a, p_b=m_p_b, w_out=m_w_out, b_out=m_b_out, ln_g=m_ln_g, ln_b=m_ln_b)
    mom2 = dict(ln_in_g=v_ln_in_g, ln_in_b=v_ln_in_b, w_in=v_w_in, b_in=v_b_in, sinks=v_sinks, vn_g=v_vn_g, vn_b=v_vn_b,
                w_s=v_w_s, b_s=v_b_s, p_a=v_p_a, p_b=v_p_b, w_out=v_w_out, b_out=v_b_out, ln_g=v_ln_g, ln_b=v_ln_b)
    n_seq, S, _ = x.shape
    T = n_seq * S
    c = lax.axis_index("c")
    j = 2 * lax.axis_index("x") + lax.axis_index("y")
    core = c.astype(jnp.int32).reshape((1,))
    chips = jnp.stack([j] + [r + (r >= j) for r in range(N_CHIPS - 1)]).astype(jnp.int32)
    names = list(_PIECES)
    xt, tt = x.reshape((T, D_MODEL)), loss_target.reshape((T, D_MODEL))

    tview = lambda d: dict(d, w_in=jnp.swapaxes(d["w_in"], 1, 2))
    weights_t, mom1_t, mom2_t = tview(weights), tview(mom1), tview(mom2)

    def own_shards(l):
        return [weights_t[n][l].astype(MXU_DTYPE) for n in _BIG]

    def layer_weights(l, gathered):
        d = dict(zip(_PIECES, gathered))
        d.update(paT=d["pa"].T, pbT=d["pb"].T, woutT=d["wout"].T)
        d.update(bin=b_in[l][None], sinks=sinks[l], vng=vn_g[l][None], vnb=vn_b[l][None], ws=w_s[l],
                 bs=jnp.repeat(b_s[l].T, 128, axis=1), bout=b_out[l][None], lng=ln_g[l][None], lnb=ln_b[l][None])
        return d

    own0, own1 = own_shards(0), own_shards(1)
    x0, g0 = input_norm_forward(xt, ln_in_g[None], ln_in_b[None], "ln_in_fwd", comm=gather_weights(names, own0))
    lw0 = layer_weights(0, g0)
    n_proj = T // ROWS_PROJ
    h0, g1 = project(x0, lw0["winT"], lw0["bin"], "layer0_proj", comm=gather_weights(names, own1, turn_step=(3 * n_proj) // 4))
    lw1 = layer_weights(1, g1)
    x1, h1 = layer_forward_and_projection(x0, h0, lw0, lw1["winT"], lw1["bin"], S, "layer0_fwd_layer1_proj")

    dres1, dh1, acc1, _ = layer_backward(x1, h1, tt, lw1, S, True, "layer1_bwd")
    dx1, dwinT1, _ = input_grads(dres1, dh1, x1, lw1["winT"], "layer1_dx_dwin")
    parts1 = [dwinT1, acc1["dpa"], acc1["dpb"], acc1["dwout"]]
    dres0, dh0, acc0, landed1 = layer_backward(x0, h0, dx1, lw0, S, False, "layer0_bwd", comm=pair_send_halves(names, parts1))
    sums1 = add_own_halves(names, parts1, landed1, core, "add_own_halves1")
    dx0, dwinT0, from_chips1 = input_grads(dres0, dh0, x0, lw0["winT"], "layer0_dx_dwin", comm=chip_exchange(sums1))
    totals1 = sum_chips(sums1, from_chips1, chips, "sum_chips1")
    grad_x, d_ln_in_g, d_ln_in_b, _ = input_norm_backward(xt, dx0, ln_in_g, "ln_in_bwd")

    gl = [acc0, acc1]
    per_layer = lambda key, pick: jnp.stack([pick(gl[l][key]) for l in range(DEPTH)])
    row0 = lambda key: per_layer(key, lambda a: a[0])
    small_part = _pack_small(dict(
        ln_in_g=d_ln_in_g[0], ln_in_b=d_ln_in_b[0], b_in=row0("dbin"), sinks=per_layer("dsink", lambda a: a[:, 0]),
        vn_g=row0("dvng"), vn_b=row0("dvnb"), w_s=per_layer("dws", lambda a: a),
        b_s=per_layer("dbs", lambda a: a[:, ::128].T), b_out=row0("dbout"), ln_g=row0("dlng"), ln_b=row0("dlnb")),
        last_row=acc1["loss"][0, 0])
    parts0 = [dwinT0, acc0["dpa"], acc0["dpb"], acc0["dwout"]]
    first = _run_comm(pair_send_halves(names, parts0, small_part, extra=totals1), "pair_send_halves0")
    landed0, small_landed, sib_totals1 = first[:4], first[4], first[5:]
    sums0 = add_own_halves(names, parts0, landed0, core, "add_own_halves0")
    small_sum = add_small_half(small_part, small_landed, core)
    *from_chips0, small_from_chips = _run_comm(chip_exchange(sums0, small_sum), "chip_exchange0")
    *totals0, small_total = sum_chips(sums0, from_chips0, chips, "sum_chips0", small_landed=small_from_chips)
    *sib_totals0, sib_small = _run_comm(pair_send_totals(totals0 + [small_total]), "pair_send_totals0")

    def both_halves(mine, theirs):
        return jnp.where(c == 0, jnp.concatenate([mine, theirs], axis=0), jnp.concatenate([theirs, mine], axis=0))

    g_layers = [(both_halves(totals0[pi], sib_totals0[pi]), both_halves(totals1[pi], sib_totals1[pi]))
                for pi in range(len(_BIG))]
    g_small = both_halves(small_total, sib_small)
    loss = g_small[-1, 0]

    big = adamw_big([weights_t[n] for n in _BIG], g_layers, [mom1_t[n] for n in _BIG], [mom2_t[n] for n in _BIG])
    small_names = [n for n, _ in _SMALL]
    as2d = lambda a: a.reshape((1, -1)) if a.ndim == 1 else a
    g_named = _unpack_small(g_small)
    small = adamw_small(*[[as2d(d[n]) for n in small_names] for d in (weights, g_named, mom1, mom2)])

    def named(big_outs, small_outs):
        d = dict(zip(_BIG, big_outs))
        d["w_in"] = jnp.swapaxes(d["w_in"], 1, 2)
        d.update({n: o.reshape(weights[n].shape) for n, o in zip(small_names, small_outs)})
        return d

    order = ("ln_in_g", "ln_in_b", "w_in", "b_in", "sinks", "vn_g", "vn_b", "w_s", "b_s", "p_a", "p_b", "w_out", "b_out",
             "ln_g", "ln_b")
    res = [loss, grad_x.reshape(x.shape)]
    groups = [named(big[0], [g_named[n] for n in small_names])] + [named(big[k + 1], small[k]) for k in range(3)]
    for group in groups:
        res.extend(group[n] for n in order)
    return tuple(res)
```

```python
import functools

import jax
import jax.numpy as jnp
from jax import lax
from jax.experimental import pallas as pl
from jax.experimental.pallas import tpu as pltpu

F32 = jnp.float32
MXU_DTYPE = jnp.bfloat16

D_MODEL = 1024
DEPTH = 2
HEAD_DIM = 64
ATTN_W = 512
KV_W = 128
BLK = 128
SGU_W = 512
SGU_G = 4
N_COLS = 4864
C_Q, C_K, C_V, C_GA, C_UB, C_VB, C_GB, C_RA, C_RB = 0, 512, 640, 768, 1280, 1792, 2304, 2816, 3840
ALPHA = (2.0 * DEPTH) ** 0.25
LN_EPS = 1e-5
SCALE = HEAD_DIM ** -0.5
NEG = float(jnp.finfo(jnp.float32).min)
GELU_C = 0.7978845608028654
GELU_A = 0.044715

ADAM_LR, ADAM_B1, ADAM_B2, ADAM_EPS, ADAM_WD, ADAM_STEP = 0.001, 0.9, 0.999, 1e-08, 0.01, 10

V7X_VMEM_BYTES = 64 * 1024 * 1024
V7X_VMEM_RESERVE = 3 * 1024 * 1024
ROWS = 256
ROWS_BWD = 256
ROWS_PROJ = 512

MESH = pl.DeviceIdType.MESH
N_CHIPS = 4


def _mm(a, b):
    return jnp.dot(a.astype(MXU_DTYPE), b.astype(MXU_DTYPE), preferred_element_type=F32)


def _mm_nt(a, b):
    return lax.dot_general(a.astype(MXU_DTYPE), b.astype(MXU_DTYPE), (((1,), (1,)), ((), ())),
                           preferred_element_type=F32)


def _mm_tn(a, b):
    return lax.dot_general(a.astype(MXU_DTYPE), b.astype(MXU_DTYPE), (((0,), (0,)), ((), ())),
                           preferred_element_type=F32)


def _sigmoid(x):
    return 0.5 * jnp.tanh(0.5 * x) + 0.5


def _gelu_parts(x):
    x2 = x * x
    u = 0.5 * jnp.tanh(x * (GELU_C + (GELU_C * GELU_A) * x2)) + 0.5
    dg = u * (1.0 + x * (1.0 - u) * (2.0 * GELU_C + (6.0 * GELU_C * GELU_A) * x2))
    return x * u, dg


def _ln_fwd(z, g, b):
    mu = jnp.mean(z, axis=-1, keepdims=True)
    zc = z - mu
    var = jnp.mean(zc * zc, axis=-1, keepdims=True)
    rstd = lax.rsqrt(var + LN_EPS)
    zhat = zc * rstd
    return zhat * g + b, zhat, rstd


def _ln_bwd(dy, zhat, rstd, g):
    dzh = dy * g
    m1 = jnp.mean(dzh, axis=-1, keepdims=True)
    m2 = jnp.mean(dzh * zhat, axis=-1, keepdims=True)
    return rstd * (dzh - m1 - zhat * m2)


def _colsum(v):
    return jnp.sum(v, axis=0, keepdims=True)


def _lane_lo(rows):
    return lax.broadcasted_iota(jnp.int32, (rows, 128), 1) < HEAD_DIM


def _dup_head(x2, hk, lo):
    xr = pltpu.roll(x2, HEAD_DIM, 1)
    return jnp.where(lo, x2, xr) if hk == 0 else jnp.where(lo, xr, x2)


def _fold_head(y0, y1, lo):
    f0 = y0 + pltpu.roll(y0, HEAD_DIM, 1)
    f1 = y1 + pltpu.roll(y1, HEAD_DIM, 1)
    return jnp.where(lo, f0, f1)


def _stack_heads(t0, t1, lo):
    z = jnp.zeros_like(t0)
    return jnp.concatenate([jnp.where(lo, t0, z), jnp.where(lo, z, t0),
                            jnp.where(lo, t1, z), jnp.where(lo, z, t1)], axis=0)


def _unstack_heads(o, lo):
    return (jnp.where(lo, o[0:128], o[128:256]), jnp.where(lo, o[256:384], o[384:512]))


def _band_bias(start):
    row = lax.broadcasted_iota(jnp.int32, (4 * BLK, 2 * BLK), 0) & (BLK - 1)
    kpos = lax.broadcasted_iota(jnp.int32, (4 * BLK, 2 * BLK), 1)
    valid = (kpos > row) & (kpos <= row + BLK)
    if start is not None:
        valid = valid & (jnp.logical_not(start) | (kpos >= BLK))
    return jnp.where(valid, 0.0, NEG)


def _attn_probs(s, sink4, bias):
    s = s * SCALE + bias
    r1 = lax.broadcasted_iota(jnp.int32, (4 * BLK, 1), 0)
    sk = jnp.where(r1 < BLK, sink4[0], jnp.where(r1 < 2 * BLK, sink4[1], jnp.where(r1 < 3 * BLK, sink4[2], sink4[3])))
    m = jnp.maximum(jnp.max(s, axis=-1, keepdims=True), sk)
    p = jnp.exp(s - m)
    es = jnp.exp(sk - m)
    inv = 1.0 / (jnp.sum(p, axis=-1, keepdims=True) + es)
    return p * inv, es * inv


_PROJ_CHUNK = 512


def _kv_blocks(kvh, h_s, nb):
    ks = [kvh[:, 0:KV_W]] + [h_s[n * BLK:(n + 1) * BLK, C_K:C_K + KV_W] for n in range(nb)]
    vs = [kvh[:, KV_W:2 * KV_W]] + [h_s[n * BLK:(n + 1) * BLK, C_V:C_V + KV_W] for n in range(nb)]
    return ks, vs


def _attn_operands(h_s, ks, vs, n, hk, lo, lo2):
    rows = slice(n * BLK, (n + 1) * BLK)
    kd = _dup_head(jnp.concatenate([ks[n], ks[n + 1]], axis=0), hk, lo2)
    vd = _dup_head(jnp.concatenate([vs[n], vs[n + 1]], axis=0), hk, lo2)
    c0 = C_Q + 2 * hk * 128
    qs = _stack_heads(h_s[rows, c0:c0 + 128], h_s[rows, c0 + 128:c0 + 256], lo)
    return qs, kd, vd


def _sgu_weights(w):
    tri = (lax.broadcasted_iota(jnp.int32, (BLK, BLK), 0) >= lax.broadcasted_iota(jnp.int32, (BLK, BLK), 1))
    return tri, [jnp.where(tri, w["ws"][g], 0.0) for g in range(SGU_G)]


def _layer_forward(x, kvh, start, w, h_s, attn_s, mix_s, keep, tick=lambda: None):
    R = x.shape[0]
    nb = R // BLK
    lo = _lane_lo(BLK)
    lo2 = _lane_lo(2 * BLK)
    ks, vs = _kv_blocks(kvh, h_s, nb)
    sinks = [w["sinks"][j] for j in range(8)]
    bias_first = _band_bias(start)
    bias_rest = _band_bias(None) if nb > 1 else None
    pairs = [(n, hk) for n in range(nb) for hk in range(2)]
    ops = [_attn_operands(h_s, ks, vs, n, hk, lo, lo2) for n, hk in pairs]
    scores = [_mm_nt(qs, kd) for qs, kd, _ in ops]
    probs_l = [_attn_probs(s, sinks[4 * hk:4 * hk + 4], bias_first if n == 0 else bias_rest)
               for s, (n, hk) in zip(scores, pairs)]
    outs = [_mm(p, vd) for (p, _), (_, _, vd) in zip(probs_l, ops)]
    for o, (n, hk) in zip(outs, pairs):
        rows = slice(n * BLK, (n + 1) * BLK)
        t0, t1 = _unstack_heads(o, lo)
        attn_s[rows, 2 * hk * 128:(2 * hk + 1) * 128] = t0
        attn_s[rows, (2 * hk + 1) * 128:(2 * hk + 2) * 128] = t1
        tick()
    ga = h_s[:, C_GA:C_GA + ATTN_W]
    ya = attn_s[...] * (ga * _sigmoid(ga))
    tick()
    gu, _ = _gelu_parts(h_s[:, C_UB:C_UB + SGU_W])
    tick()
    gv, _ = _gelu_parts(h_s[:, C_VB:C_VB + SGU_W])
    vn, _, _ = _ln_fwd(gv, w["vng"][...], w["vnb"][...])
    tick()
    _, wms = _sgu_weights(w)
    tiles = [(slice(n * BLK, (n + 1) * BLK), g, slice(g * 128, (g + 1) * 128)) for n in range(nb) for g in range(SGU_G)]
    mixes = [_mm(wms[g], vn[rows, cols]) for rows, g, cols in tiles]
    for m, (rows, g, cols) in zip(mixes, tiles):
        mix_s[rows, cols] = m + w["bs"][:, cols]
    gb = h_s[:, C_GB:C_GB + SGU_W]
    yb = gu * mix_s[...] * (gb * _sigmoid(gb))
    tick()
    a = _mm(ya, w["pa"][...])
    b = _mm(yb, w["pb"][...])
    tick()
    merged = _sigmoid(h_s[:, C_RA:C_RA + D_MODEL]) * a + _sigmoid(h_s[:, C_RB:C_RB + D_MODEL]) * b
    tick()
    out = _mm(merged, w["wout"][...]) + w["bout"][...]
    y, zhat, rstd = _ln_fwd(ALPHA * x + out, w["lng"][...], w["lnb"][...])
    if not keep:
        return y, None
    return y, dict(ks=ks, vs=vs, probs=probs_l, ya=ya, yb=yb, a=a, b=b, merged=merged, zhat=zhat, rstd=rstd)


def _dsilu(g, sg):
    return sg * (1.0 + g * (1.0 - sg))


_W_NAMES = ("sinks", "vng", "vnb", "ws", "bs", "pa", "pb", "wout", "bout", "lng", "lnb")
_WT_NAMES = ("paT", "pbT", "woutT")
_ACC_NAMES = ("dwout", "dpa", "dpb", "dbin", "dbout", "dlng", "dlnb", "dvng", "dvnb", "dws", "dbs", "dsink", "loss")


def _make_fwd_proj_body(R, S, nT):
    def body(x_ref, h_ref, kvh_ref, wn_ref, bn_ref, *rest):
        w = dict(zip(_W_NAMES, rest[:len(_W_NAMES)]))
        y_ref, hn_ref, attn_s, mix_s, yprev_s = rest[len(_W_NAMES):]

        @pl.when(pl.program_id(0) == 0)
        def _():
            yprev_s[...] = jnp.zeros(yprev_s.shape, yprev_s.dtype)

        t = jnp.minimum(pl.program_id(0), nT - 1)
        start = (t % (S // R)) == 0
        chunks = [(c0, min(c0 + _PROJ_CHUNK, N_COLS)) for c0 in range(0, N_COLS, _PROJ_CHUNK)]

        def tick():
            if chunks:
                c0, c1 = chunks.pop(0)
                hn_ref[:, c0:c1] = _mm_nt(yprev_s[...], wn_ref[c0:c1, :]) + bn_ref[:, c0:c1]

        y, _ = _layer_forward(x_ref[...], kvh_ref[...], start, w, h_ref, attn_s, mix_s, keep=False, tick=tick)
        while chunks:
            tick()
        y_ref[...] = y
        yprev_s[...] = y.astype(yprev_s.dtype)
    return body


def _make_bwd_body(R, S, nT, is_last):
    nb = R // BLK
    names = _W_NAMES + _WT_NAMES
    nw = len(names)

    def body(x_ref, h_s, kvh_ref, aux_ref, *rest):
        w = dict(zip(names, rest[:nw]))
        dres_ref, dh_ref = rest[nw:nw + 2]
        acc = dict(zip(_ACC_NAMES, rest[nw + 2:nw + 2 + len(_ACC_NAMES)]))
        attn_s, mix_s, ckv_s = rest[nw + 2 + len(_ACC_NAMES):]
        i = pl.program_id(0)
        start = ((nT - 1 - i) % (S // R)) == 0
        lo = _lane_lo(BLK)
        lo2 = _lane_lo(2 * BLK)

        @pl.when(i == 0)
        def _():
            for name in _ACC_NAMES:
                acc[name][...] = jnp.zeros(acc[name].shape, F32)
            ckv_s[...] = jnp.zeros(ckv_s.shape, F32)

        x = x_ref[...]
        y, sv = _layer_forward(x, kvh_ref[...], start, w, h_s, attn_s, mix_s, keep=True)
        if is_last:
            diff = y - aux_ref[...]
            part = 0.5 * jnp.sum(jnp.mean(diff * diff, axis=-1, keepdims=True), axis=0, keepdims=True)
            acc["loss"][...] += jnp.broadcast_to(part, acc["loss"].shape)
            dy = diff * (1.0 / D_MODEL)
        else:
            dy = aux_ref[...]

        def put(c0, val, rows=slice(None)):
            width = val.shape[1]
            dh_ref[rows, c0:c0 + width] = val.astype(dh_ref.dtype)
            acc["dbin"][:, c0:c0 + width] += _colsum(val)

        acc["dlng"][...] += _colsum(dy * sv["zhat"])
        acc["dlnb"][...] += _colsum(dy)
        dz = _ln_bwd(dy, sv["zhat"], sv["rstd"], w["lng"][...])
        dres_ref[...] = ALPHA * dz
        acc["dbout"][...] += _colsum(dz)
        acc["dwout"][...] += _mm_tn(sv["merged"], dz)
        dmerged = _mm(dz, w["woutT"][...])
        sa = _sigmoid(h_s[:, C_RA:C_RA + D_MODEL])
        da = dmerged * sa
        put(C_RA, da * sv["a"] * (1.0 - sa))
        sb = _sigmoid(h_s[:, C_RB:C_RB + D_MODEL])
        db = dmerged * sb
        put(C_RB, db * sv["b"] * (1.0 - sb))
        acc["dpa"][...] += _mm_tn(sv["ya"], da)
        acc["dpb"][...] += _mm_tn(sv["yb"], db)
        dya = _mm(da, w["paT"][...])
        dyb = _mm(db, w["pbT"][...])
        ga = h_s[:, C_GA:C_GA + ATTN_W]
        sga = _sigmoid(ga)
        put(C_GA, dya * attn_s[...] * _dsilu(ga, sga))
        attn_s[...] = dya * (ga * sga)
        ks, vs = sv["ks"], sv["vs"]
        pairs = [(n, hk) for n in range(nb) for hk in range(2)]
        ops = [_attn_operands(h_s, ks, vs, n, hk, lo, lo2) for n, hk in pairs]
        doss = [_stack_heads(attn_s[n * BLK:(n + 1) * BLK, 2 * hk * 128:(2 * hk + 1) * 128],
                             attn_s[n * BLK:(n + 1) * BLK, (2 * hk + 1) * 128:(2 * hk + 2) * 128], lo) for n, hk in pairs]
        dps = [_mm_nt(dos, vd) for dos, (_, _, vd) in zip(doss, ops)]
        gb = h_s[:, C_GB:C_GB + SGU_W]
        sgb = _sigmoid(gb)
        gu, dgu = _gelu_parts(h_s[:, C_UB:C_UB + SGU_W])
        mixed = mix_s[...]
        put(C_GB, dyb * (gu * mixed) * _dsilu(gb, sgb))
        dsgu = dyb * (gb * sgb)
        put(C_UB, dsgu * mixed * dgu)
        dmixed = dsgu * gu
        dsss = []
        for dp, (probs, ps), (n, hk) in zip(dps, sv["probs"], pairs):
            delta = jnp.sum(probs * dp, axis=-1, keepdims=True)
            dsk = -(ps * delta)
            for g in range(4):
                j = 4 * hk + g
                acc["dsink"][j:j + 1, :] += jnp.broadcast_to(_colsum(dsk[g * BLK:(g + 1) * BLK]), (1, 128))
            dsss.append(probs * (dp - delta) * SCALE)
        dqs = [_mm(dss, kd) for dss, (_, kd, _) in zip(dsss, ops)]
        ydks = [_mm_tn(dss, qs) for dss, (qs, _, _) in zip(dsss, ops)]
        ydvs = [_mm_tn(probs, dos) for (probs, _), dos in zip(sv["probs"], doss)]
        gv, dgv = _gelu_parts(h_s[:, C_VB:C_VB + SGU_W])
        vn, vhat, vrstd = _ln_fwd(gv, w["vng"][...], w["vnb"][...])
        tri, wms = _sgu_weights(w)
        tiles = [(slice(n * BLK, (n + 1) * BLK), g, slice(g * 128, (g + 1) * 128)) for n in range(nb) for g in range(SGU_G)]
        dw_parts = [_mm_nt(dmixed[rows, cols], vn[rows, cols]) for rows, g, cols in tiles]
        dvn_parts = [_mm_tn(wms[g], dmixed[rows, cols]) for rows, g, cols in tiles]
        for g in range(SGU_G):
            tot = dw_parts[g]
            for n in range(1, nb):
                tot = tot + dw_parts[n * SGU_G + g]
            acc["dws"][g] += jnp.where(tri, tot, 0.0)
        for part, (rows, g, cols) in zip(dvn_parts, tiles):
            mix_s[rows, cols] = part
        dbs_part = dmixed[0:BLK]
        for n in range(1, nb):
            dbs_part = dbs_part + dmixed[n * BLK:(n + 1) * BLK]
        acc["dbs"][...] += dbs_part
        for dq, (n, hk) in zip(dqs, pairs):
            q0, q1 = _unstack_heads(dq, lo)
            put(C_Q + 2 * hk * 128, q0, slice(n * BLK, (n + 1) * BLK))
            put(C_Q + (2 * hk + 1) * 128, q1, slice(n * BLK, (n + 1) * BLK))
        dks = [jnp.zeros((BLK, KV_W), F32) for _ in range(nb + 1)]
        dvs = [jnp.zeros((BLK, KV_W), F32) for _ in range(nb + 1)]
        for n in range(nb):
            dk2 = _fold_head(ydks[2 * n], ydks[2 * n + 1], lo2)
            dv2 = _fold_head(ydvs[2 * n], ydvs[2 * n + 1], lo2)
            dks[n] = dks[n] + dk2[0:BLK]
            dks[n + 1] = dks[n + 1] + dk2[BLK:2 * BLK]
            dvs[n] = dvs[n] + dv2[0:BLK]
            dvs[n + 1] = dvs[n + 1] + dv2[BLK:2 * BLK]
        dks[nb] = dks[nb] + ckv_s[:, 0:KV_W]
        dvs[nb] = dvs[nb] + ckv_s[:, KV_W:2 * KV_W]
        ckv_s[:, 0:KV_W] = dks[0]
        ckv_s[:, KV_W:2 * KV_W] = dvs[0]
        put(C_K, jnp.concatenate(dks[1:], axis=0))
        put(C_V, jnp.concatenate(dvs[1:], axis=0))
        dvn = mix_s[...]
        acc["dvng"][...] += _colsum(dvn * vhat)
        acc["dvnb"][...] += _colsum(dvn)
        put(C_VB, _ln_bwd(dvn, vhat, vrstd, w["vng"][...]) * dgv)

        @pl.when(i == nT - 1)
        def _():
            for g in range(SGU_G):
                cols = slice(g * 128, (g + 1) * 128)
                tot = jnp.sum(acc["dbs"][:, cols], axis=1, keepdims=True)
                acc["dbs"][:, cols] = jnp.broadcast_to(tot, (BLK, 128))

    return body


def _resident():
    return pl.BlockSpec(memory_space=pltpu.VMEM)


_BIG_RESIDENT = ("pa", "pb", "wout", "paT", "pbT", "woutT", "dwout", "dpa", "dpb")


def _const_block(shape):
    return pl.BlockSpec(tuple(shape), lambda *_: (0,) * len(shape))


def _operand_spec(name, shape):
    if name == "sinks":
        return pl.BlockSpec(memory_space=pltpu.SMEM)
    return _resident() if name in _BIG_RESIDENT else _const_block(shape)


def _weight_specs(names, wts):
    return [_operand_spec(n, wts[n].shape) for n in names]


def _nbytes(a):
    n = jnp.dtype(a.dtype).itemsize
    for d in a.shape:
        n *= d
    return n


def _layer_params(resident=()):
    scoped = V7X_VMEM_BYTES - V7X_VMEM_RESERVE - sum(_nbytes(a) for a in resident)
    return pltpu.CompilerParams(dimension_semantics=("arbitrary",), vmem_limit_bytes=scoped)


class _Comm:
    def __init__(self, ins, out_shapes, sems, start, finish, aliases=None, turn=None, turn_step=None):
        self.ins = [pltpu.with_memory_space_constraint(a, pltpu.HBM) for a in ins]
        self.out_shapes = [pltpu.HBM(s.shape, s.dtype) for s in out_shapes]
        self.sems = list(sems)
        self.start, self.finish, self.aliases = start, finish, dict(aliases or {})
        self.turn, self.turn_step = turn, turn_step


_ANY = pl.BlockSpec(memory_space=pltpu.HBM)
_DMA = pltpu.SemaphoreType.DMA


def _call(body, *, name, grid, in_specs, out_specs, out_shape, scratch_shapes, compiler_params, operands, comm=None):
    if comm is None:
        res = pl.pallas_call(body, name=name, grid=grid, in_specs=in_specs, out_specs=out_specs, out_shape=out_shape,
                             scratch_shapes=scratch_shapes, compiler_params=compiler_params)(*operands)
        return list(res), []
    n_in, n_out, n_scr = len(in_specs), len(out_specs), len(scratch_shapes)
    ci, co = len(comm.ins), len(comm.out_shapes)
    last = grid[0] - 1

    def hosted(*refs):
        ins, cin = refs[:n_in], refs[n_in:n_in + ci]
        p = n_in + ci
        outs, cout = refs[p:p + n_out], refs[p + n_out:p + n_out + co]
        p += n_out + co
        scr, sems = refs[p:p + n_scr], refs[p + n_scr:]
        pl.when(pl.program_id(0) == 0)(lambda: comm.start(cin, cout, sems))
        body(*ins, *outs, *scr)
        if comm.turn is not None:
            at = last if comm.turn_step is None else comm.turn_step
            pl.when(pl.program_id(0) == at)(lambda: comm.turn(cin, cout, sems))
        pl.when(pl.program_id(0) == last)(lambda: comm.finish(cin, cout, sems))

    res = pl.pallas_call(
        hosted, name=name, grid=grid, in_specs=list(in_specs) + [_ANY] * ci, out_specs=list(out_specs) + [_ANY] * co,
        out_shape=list(out_shape) + comm.out_shapes, scratch_shapes=list(scratch_shapes) + comm.sems,
        input_output_aliases={n_in + a: n_out + b for a, b in comm.aliases.items()},
        compiler_params=compiler_params)(*operands, *comm.ins)
    return list(res[:n_out]), list(res[n_out:])


def _run_comm(comm, name):
    ci, co = len(comm.ins), len(comm.out_shapes)

    def body(*refs):
        cin, cout, sems = refs[:ci], refs[ci:ci + co], refs[ci + co:]
        comm.start(cin, cout, sems)
        if comm.turn is not None:
            comm.turn(cin, cout, sems)
        comm.finish(cin, cout, sems)

    return list(pl.pallas_call(body, name=name, in_specs=[_ANY] * ci, out_specs=[_ANY] * co, out_shape=comm.out_shapes,
                               scratch_shapes=comm.sems, input_output_aliases=comm.aliases)(*comm.ins))


def project(x, winT, b, name, comm=None):
    T = x.shape[0]
    R = ROWS_PROJ

    def body(x_ref, w_ref, b_ref, h_ref):
        xb = x_ref[...].astype(MXU_DTYPE)
        for c0 in range(0, N_COLS, _PROJ_CHUNK):
            c1 = min(c0 + _PROJ_CHUNK, N_COLS)
            h_ref[:, c0:c1] = _mm_nt(xb, w_ref[c0:c1, :]) + b_ref[:, c0:c1]

    (h,), extra = _call(
        body, name=name, grid=(T // R,),
        in_specs=[pl.BlockSpec((R, D_MODEL), lambda t: (t, 0)), _resident(), _const_block(b.shape)],
        out_specs=[pl.BlockSpec((R, N_COLS), lambda t: (t, 0))],
        out_shape=[jax.ShapeDtypeStruct((T, N_COLS), F32)], scratch_shapes=[],
        compiler_params=_layer_params([winT]), operands=(x, winT, b), comm=comm)
    return h, extra


_KV_BLOCK = C_K // (2 * KV_W)


def layer_forward_and_projection(x, h, wts, winT_next, b_next, S, name):
    T = x.shape[0]
    R = ROWS
    nT = T // R
    npt = R // BLK
    cur = lambda t: jnp.minimum(t, nT - 1)
    (y, hn), _ = _call(
        _make_fwd_proj_body(R, S, nT), name=name, grid=(nT + 1,),
        in_specs=[pl.BlockSpec((R, D_MODEL), lambda t: (cur(t), 0)), pl.BlockSpec((R, N_COLS), lambda t: (cur(t), 0)),
                  pl.BlockSpec((BLK, 2 * KV_W), lambda t: (jnp.maximum(cur(t) * npt - 1, 0), _KV_BLOCK)),
                  _resident(), _const_block(b_next.shape)] + _weight_specs(_W_NAMES, wts),
        out_specs=[pl.BlockSpec((R, D_MODEL), lambda t: (cur(t), 0)),
                   pl.BlockSpec((R, N_COLS), lambda t: (jnp.maximum(t - 1, 0), 0))],
        out_shape=[jax.ShapeDtypeStruct((T, D_MODEL), F32), jax.ShapeDtypeStruct((T, N_COLS), F32)],
        scratch_shapes=[pltpu.VMEM((R, ATTN_W), F32), pltpu.VMEM((R, SGU_W), F32), pltpu.VMEM((R, D_MODEL), MXU_DTYPE)],
        compiler_params=_layer_params([winT_next] + [wts[n] for n in _W_NAMES if n in _BIG_RESIDENT]),
        operands=(x, h, h, winT_next, b_next, *[wts[n] for n in _W_NAMES]))
    return y, hn


_ACC_SHAPES = dict(dwout=(D_MODEL, D_MODEL), dpa=(ATTN_W, D_MODEL), dpb=(SGU_W, D_MODEL), dbin=(1, N_COLS),
                   dbout=(1, D_MODEL), dlng=(1, D_MODEL), dlnb=(1, D_MODEL), dvng=(1, SGU_W), dvnb=(1, SGU_W),
                   dws=(SGU_G, BLK, BLK), dbs=(BLK, SGU_W), dsink=(8, 128), loss=(8, 128))


def layer_backward(x, h, aux, wts, S, is_last, name, comm=None):
    T = x.shape[0]
    R = ROWS_BWD
    nT = T // R
    npt = R // BLK
    rev = lambda i: (nT - 1 - i, 0)
    halo = lambda i: (jnp.maximum((nT - 1 - i) * npt - 1, 0), _KV_BLOCK)
    names = _W_NAMES + _WT_NAMES
    out_shape = ([jax.ShapeDtypeStruct((T, D_MODEL), F32), jax.ShapeDtypeStruct((T, N_COLS), MXU_DTYPE)]
                 + [jax.ShapeDtypeStruct(_ACC_SHAPES[n], F32) for n in _ACC_NAMES])
    outs, extra = _call(
        _make_bwd_body(R, S, nT, is_last), name=name, grid=(nT,),
        in_specs=[pl.BlockSpec((R, D_MODEL), rev), pl.BlockSpec((R, N_COLS), rev), pl.BlockSpec((BLK, 2 * KV_W), halo),
                  pl.BlockSpec((R, D_MODEL), rev)] + _weight_specs(names, wts),
        out_specs=[pl.BlockSpec((R, D_MODEL), rev), pl.BlockSpec((R, N_COLS), rev)]
        + [_operand_spec(n, _ACC_SHAPES[n]) for n in _ACC_NAMES],
        out_shape=out_shape,
        scratch_shapes=[pltpu.VMEM((R, ATTN_W), F32), pltpu.VMEM((R, SGU_W), F32), pltpu.VMEM((BLK, 2 * KV_W), F32)],
        compiler_params=_layer_params([wts[n] for n in names if n in _BIG_RESIDENT]
                                      + [jax.ShapeDtypeStruct(_ACC_SHAPES[n], F32) for n in _ACC_NAMES if n in _BIG_RESIDENT]),
        operands=(x, h, h, aux, *[wts[n] for n in names]), comm=comm)
    return outs[0], outs[1], dict(zip(_ACC_NAMES, outs[2:])), extra


def input_grads(dres, dh, x, winT, name, comm=None):
    T = x.shape[0]
    R = ROWS

    def body(dres_ref, dh_ref, x_ref, winT_ref, dx_ref, dwin_ref):
        @pl.when(pl.program_id(0) == 0)
        def _():
            dwin_ref[...] = jnp.zeros(dwin_ref.shape, F32)

        dh = dh_ref[...]
        dx_ref[...] = dres_ref[...] + jnp.dot(dh, winT_ref[...], preferred_element_type=F32)
        dwin_ref[...] += _mm_tn(dh, x_ref[...])

    row = lambda t: (t, 0)
    dwin_shape = jax.ShapeDtypeStruct((N_COLS, D_MODEL), F32)
    (dx, dwinT), extra = _call(
        body, name=name, grid=(T // R,),
        in_specs=[pl.BlockSpec((R, D_MODEL), row), pl.BlockSpec((R, N_COLS), row), pl.BlockSpec((R, D_MODEL), row),
                  _resident()],
        out_specs=[pl.BlockSpec((R, D_MODEL), row), _resident()],
        out_shape=[jax.ShapeDtypeStruct((T, D_MODEL), F32), dwin_shape], scratch_shapes=[],
        compiler_params=_layer_params([winT, dwin_shape]), operands=(dres, dh, x, winT), comm=comm)
    return dx, dwinT, extra


def input_norm_forward(x, g, b, name, comm=None):
    T = x.shape[0]
    R = ROWS_PROJ

    def body(x_ref, g_ref, b_ref, y_ref):
        y_ref[...] = _ln_fwd(x_ref[...], g_ref[...], b_ref[...])[0]

    row = lambda t: (t, 0)
    (y,), extra = _call(
        body, name=name, grid=(T // R,),
        in_specs=[pl.BlockSpec((R, D_MODEL), row), _const_block(g.shape), _const_block(b.shape)],
        out_specs=[pl.BlockSpec((R, D_MODEL), row)],
        out_shape=[jax.ShapeDtypeStruct((T, D_MODEL), F32)], scratch_shapes=[],
        compiler_params=_layer_params(), operands=(x, g, b), comm=comm)
    return y, extra


def input_norm_backward(x, dy, g, name, comm=None):
    T = x.shape[0]
    R = ROWS_PROJ

    def body(x_ref, dy_ref, g_ref, dx_ref, dg_ref, db_ref):
        @pl.when(pl.program_id(0) == 0)
        def _():
            dg_ref[...] = jnp.zeros(dg_ref.shape, F32)
            db_ref[...] = jnp.zeros(db_ref.shape, F32)

        dy = dy_ref[...]
        _, xhat, rstd = _ln_fwd(x_ref[...], g_ref[...], g_ref[...])
        dx_ref[...] = _ln_bwd(dy, xhat, rstd, g_ref[...])
        dg_ref[...] += _colsum(dy * xhat)
        db_ref[...] += _colsum(dy)

    row = lambda t: (t, 0)
    vec = jax.ShapeDtypeStruct((1, D_MODEL), F32)
    (dx, dg, db), extra = _call(
        body, name=name, grid=(T // R,),
        in_specs=[pl.BlockSpec((R, D_MODEL), row), pl.BlockSpec((R, D_MODEL), row), _const_block(g.shape)],
        out_specs=[pl.BlockSpec((R, D_MODEL), row), _const_block(vec.shape), _const_block(vec.shape)],
        out_shape=[jax.ShapeDtypeStruct((T, D_MODEL), F32), vec, vec], scratch_shapes=[],
        compiler_params=_layer_params(), operands=(x, dy, g), comm=comm)
    return dx, dg, db, extra


_PIECES = ("winT", "pa", "pb", "wout")
_WHOLE = dict(winT=(N_COLS, D_MODEL), pa=(ATTN_W, D_MODEL), pb=(SGU_W, D_MODEL), wout=(D_MODEL, D_MODEL))
_HALF = dict(winT=(N_COLS // 8, D_MODEL), pa=(ATTN_W // 2, D_MODEL // 4), pb=(SGU_W // 2, D_MODEL // 4),
             wout=(D_MODEL // 8, D_MODEL))
SMALL_ROWS = 1280


def _region(name, ref, k, h):
    hr, hc = _HALF[name]
    if name in ("pa", "pb"):
        return ref.at[pl.ds(h * hr, hr), pl.ds(k * hc, hc)]
    return ref.at[pl.ds(k * 2 * hr + h * hr, hr), :]


def _place():
    x, y, c = lax.axis_index("x"), lax.axis_index("y"), lax.axis_index("c")
    return x, y, c, 2 * x + y


def _as_chip(j, fn):
    for jj in range(N_CHIPS):
        pl.when(j == jj)(functools.partial(fn, jj))


def _remote(src, dst, send_sem, recv_sem, to):
    return pltpu.make_async_remote_copy(src_ref=src, dst_ref=dst, send_sem=send_sem, recv_sem=recv_sem,
                                        device_id=to, device_id_type=MESH)


def _core_of(k, c):
    return (k // 2, k % 2, c)


def _others(jj):
    return [k for k in range(N_CHIPS) if k != jj]


def _start_all(cps):
    for cp in cps:
        cp.start()


def _wait_all(cps):
    for cp in cps:
        cp.wait()


def gather_over_chips(names, shards):
    n = len(names)

    def own(cin, cout, sems, jj):
        x, y, c, _ = _place()
        cps = []
        for i, name in enumerate(names):
            hr, hc = _HALF[name]
            place = (cout[i].at[:, pl.ds(jj * hc, hc)] if name in ("pa", "pb")
                     else cout[i].at[pl.ds(jj * 2 * hr, 2 * hr), :])
            cps.append(_remote(cin[i], place, sems[2].at[i], sems[3].at[i], (x, y, 1 - c)))
        return cps

    def copies(cin, cout, sems, jj, c):
        out = []
        for i, name in enumerate(names):
            hr = _HALF[name][0]
            for k in _others(jj):
                out.append(_remote(cin[i].at[pl.ds(c * hr, hr)], _region(name, cout[i], jj, c),
                                   sems[0].at[4 * i + k], sems[1].at[4 * i + jj], _core_of(k, c)))
        return out

    def start(cin, cout, sems):
        _, _, c, j = _place()
        _as_chip(j, lambda jj: _start_all(copies(cin, cout, sems, jj, c) + own(cin, cout, sems, jj)))

    def finish(cin, cout, sems):
        _, _, c, j = _place()

        def run(jj):
            for i, name in enumerate(names):
                for k in _others(jj):
                    land = _region(name, cout[i], k, c)
                    _remote(land, land, sems[0].at[4 * i + k], sems[1].at[4 * i + k], _core_of(k, c)).wait_recv()
            for cp in copies(cin, cout, sems, jj, c):
                cp.wait_send()
            _wait_all(own(cin, cout, sems, jj))

        _as_chip(j, run)

    shapes = [jax.ShapeDtypeStruct(_WHOLE[nm], s.dtype) for nm, s in zip(names, shards)]
    return _Comm(shards, shapes, [_DMA((4 * n,)), _DMA((4 * n,)), _DMA((n,)), _DMA((n,))], start, finish)


def gather_weights(names, shards, turn_step=None):
    n = len(names)
    chips = gather_over_chips(names, shards)
    pair_start, pair_finish = _pair_phase(names)

    def turn(cin, cout, sems):
        chips.finish(cin, cout, sems[:4])
        pair_start(cin, cout, sems[4:])

    return _Comm(shards, chips.out_shapes, chips.sems + [_DMA((4 * n,)), _DMA((4 * n,))], chips.start,
                 lambda cin, cout, sems: pair_finish(cin, cout, sems[4:]), turn=turn, turn_step=turn_step)


def _pair_phase(names):
    def start(cin, cout, sems):
        x, y, c, j = _place()

        def run(jj):
            for i, name in enumerate(names):
                for k in _others(jj):
                    land = _region(name, cout[i], k, c)
                    _remote(land, land, sems[0].at[4 * i + k], sems[1].at[4 * i + k], (x, y, 1 - c)).start()

        _as_chip(j, run)

    def finish(cin, cout, sems):
        x, y, c, j = _place()

        def run(jj):
            for i, name in enumerate(names):
                for k in _others(jj):
                    theirs = _region(name, cout[i], k, 1 - c)
                    _remote(theirs, theirs, sems[0].at[4 * i + k], sems[1].at[4 * i + k], (x, y, 1 - c)).wait_recv()
            for i, name in enumerate(names):
                for k in _others(jj):
                    land = _region(name, cout[i], k, c)
                    _remote(land, land, sems[0].at[4 * i + k], sems[1].at[4 * i + k], (x, y, 1 - c)).wait_send()

        _as_chip(j, run)

    return start, finish


def pair_send_halves(names, parts, small=None, extra=()):
    n = len(names)
    hs = SMALL_ROWS // 2
    first_extra = n + (small is not None)

    def copies(cin, cout, sems):
        x, y, c, _ = _place()
        sib = (x, y, 1 - c)
        cps = []
        for i, name in enumerate(names):
            for k in range(N_CHIPS):
                cps.append(_remote(_region(name, cin[i], k, 1 - c), cout[i].at[k], sems[0].at[4 * i + k],
                                   sems[1].at[4 * i + k], sib))
        if small is not None:
            cps.append(_remote(cin[n].at[pl.ds((1 - c) * hs, hs)], cout[n], sems[0].at[4 * n], sems[1].at[4 * n], sib))
        for e in range(len(extra)):
            cps.append(_remote(cin[first_extra + e], cout[first_extra + e], sems[0].at[4 * n + 1 + e],
                               sems[1].at[4 * n + 1 + e], sib))
        return cps

    start = lambda cin, cout, sems: _start_all(copies(cin, cout, sems))
    finish = lambda cin, cout, sems: _wait_all(copies(cin, cout, sems))
    shapes = [jax.ShapeDtypeStruct((N_CHIPS,) + _HALF[nm], F32) for nm in names]
    ins = list(parts)
    if small is not None:
        shapes.append(jax.ShapeDtypeStruct((hs, 128), F32))
        ins.append(small)
    shapes += [jax.ShapeDtypeStruct(a.shape, a.dtype) for a in extra]
    ins += list(extra)
    n_sem = 4 * n + 1 + len(extra)
    return _Comm(ins, shapes, [_DMA((n_sem,)), _DMA((n_sem,))], start, finish)


def chip_exchange(sums, small=None):
    n = len(sums)

    def copies(cin, cout, sems, jj, c):
        out = []
        for k in _others(jj):
            for i in range(n):
                out.append(_remote(cin[i].at[k], cout[i].at[jj], sems[0].at[4 * i + k], sems[1].at[4 * i + jj], _core_of(k, c)))
            if small is not None:
                out.append(_remote(cin[n], cout[n].at[jj], sems[0].at[4 * n + k], sems[1].at[4 * n + jj], _core_of(k, c)))
        return out

    def start(cin, cout, sems):
        _, _, c, j = _place()

        def run(jj):
            if small is not None:
                pltpu.make_async_copy(cin[n], cout[n].at[jj], sems[2]).start()
            for cp in copies(cin, cout, sems, jj, c):
                cp.start()

        _as_chip(j, run)

    def finish(cin, cout, sems):
        _, _, c, j = _place()

        def run(jj):
            for k in _others(jj):
                for i in range(n):
                    _remote(cin[i].at[k], cout[i].at[k], sems[0].at[4 * i + k], sems[1].at[4 * i + k], _core_of(k, c)).wait_recv()
                if small is not None:
                    _remote(cin[n], cout[n].at[k], sems[0].at[4 * n + k], sems[1].at[4 * n + k], _core_of(k, c)).wait_recv()
            for cp in copies(cin, cout, sems, jj, c):
                cp.wait_send()
            if small is not None:
                pltpu.make_async_copy(cin[n], cout[n].at[jj], sems[2]).wait()

        _as_chip(j, run)

    shapes = [jax.ShapeDtypeStruct(a.shape, a.dtype) for a in sums]
    ins = list(sums)
    if small is not None:
        shapes.append(jax.ShapeDtypeStruct((N_CHIPS,) + small.shape, small.dtype))
        ins.append(small)
    return _Comm(ins, shapes, [_DMA((4 * n + 4,)), _DMA((4 * n + 4,)), _DMA(())], start, finish)


def pair_send_totals(arrs):
    n = len(arrs)

    def copies(cin, cout, sems):
        x, y, c, _ = _place()
        return [_remote(cin[i], cout[i], sems[0].at[i], sems[1].at[i], (x, y, 1 - c)) for i in range(n)]

    start = lambda cin, cout, sems: _start_all(copies(cin, cout, sems))
    finish = lambda cin, cout, sems: _wait_all(copies(cin, cout, sems))
    return _Comm(arrs, [jax.ShapeDtypeStruct(a.shape, a.dtype) for a in arrs], [_DMA((n,)), _DMA((n,))], start, finish)


def _pin(a):
    return pltpu.with_memory_space_constraint(a, pltpu.HBM)


def _plain_params(n_axes):
    return pltpu.CompilerParams(dimension_semantics=("arbitrary",) * n_axes, vmem_limit_bytes=V7X_VMEM_BYTES // 2)


def add_own_halves(names, parts, landed, core, name):
    n = len(names)

    def body(core_ref, *refs):
        for i in range(n):
            refs[2 * n + i][0] = (refs[i][...] + refs[n + i][0]).astype(MXU_DTYPE)

    in_specs, out_specs = [], []
    for nm in names:
        hr, hc = _HALF[nm]
        if nm in ("pa", "pb"):
            in_specs.append(pl.BlockSpec((hr // 2, hc), lambda k, s, cr: (cr[0] * 2 + s, k)))
        else:
            in_specs.append(pl.BlockSpec((hr // 2, hc), lambda k, s, cr: (k * 4 + cr[0] * 2 + s, 0)))
    for nm in names:
        hr, hc = _HALF[nm]
        in_specs.append(pl.BlockSpec((1, hr // 2, hc), lambda k, s, cr: (k, s, 0)))
        out_specs.append(pl.BlockSpec((1, hr // 2, hc), lambda k, s, cr: (k, s, 0)))
    return pl.pallas_call(
        body, name=name,
        grid_spec=pltpu.PrefetchScalarGridSpec(num_scalar_prefetch=1, grid=(N_CHIPS, 2), in_specs=in_specs, out_specs=out_specs),
        out_shape=[pltpu.HBM((N_CHIPS,) + _HALF[nm], MXU_DTYPE) for nm in names],
        compiler_params=_plain_params(2),
    )(core, *[_pin(a) for a in parts], *[_pin(a) for a in landed])


def add_small_half(small, landed, core):
    hs = SMALL_ROWS // 2

    def body(core_ref, a_ref, b_ref, o_ref):
        o_ref[...] = a_ref[...] + b_ref[...]

    return pl.pallas_call(
        body, name="add_small_half",
        grid_spec=pltpu.PrefetchScalarGridSpec(
            num_scalar_prefetch=1, grid=(1,),
            in_specs=[pl.BlockSpec((hs, 128), lambda s, cr: (cr[0], 0)), pl.BlockSpec((hs, 128), lambda s, cr: (0, 0))],
            out_specs=pl.BlockSpec((hs, 128), lambda s, cr: (0, 0))),
        out_shape=jax.ShapeDtypeStruct((hs, 128), F32),
    )(core, small, landed)


def sum_chips(sums, landed, chips, name, small_landed=None):
    n = len(sums)

    def body(chips_ref, *refs):
        ins, outs = refs[:4 * n + (small_landed is not None)], refs[4 * n + (small_landed is not None):]
        for i in range(n):
            tot = ins[4 * i][0].astype(F32)
            for r in range(1, N_CHIPS):
                tot = tot + ins[4 * i + r][0].astype(F32)
            outs[i][...] = tot
        if small_landed is not None:
            @pl.when(pl.program_id(0) == 0)
            def _():
                sm = ins[4 * n]
                outs[n][...] = ((sm[0] + sm[1]) + sm[2]) + sm[3]

    in_specs, out_specs, operands, shapes = [], [], [], []
    for a, l in zip(sums, landed):
        _, hr, hc = a.shape
        for r in range(N_CHIPS):
            in_specs.append(pl.BlockSpec((1, hr // 2, hc), functools.partial(lambda s, ch, r: (ch[r], s, 0), r=r)))
            operands.append(a if r == 0 else l)
        out_specs.append(pl.BlockSpec((hr // 2, hc), lambda s, ch: (s, 0)))
        shapes.append(jax.ShapeDtypeStruct((hr, hc), F32))
    if small_landed is not None:
        in_specs.append(pl.BlockSpec(small_landed.shape, lambda s, ch: (0, 0, 0)))
        out_specs.append(pl.BlockSpec(small_landed.shape[1:], lambda s, ch: (0, 0)))
        operands.append(small_landed)
        shapes.append(jax.ShapeDtypeStruct(small_landed.shape[1:], F32))
    return pl.pallas_call(
        body, name=name,
        grid_spec=pltpu.PrefetchScalarGridSpec(num_scalar_prefetch=1, grid=(2,), in_specs=in_specs, out_specs=out_specs),
        out_shape=[pltpu.HBM(s.shape, s.dtype) for s in shapes], compiler_params=_plain_params(1),
    )(chips, *[_pin(a) for a in operands])


N_STEPS = 8


def _adam_update(w, g, m, v):
    m = ADAM_B1 * m + (1.0 - ADAM_B1) * g
    v = ADAM_B2 * v + (1.0 - ADAM_B2) * (g * g)
    m_hat = m * (1.0 / (1.0 - ADAM_B1 ** ADAM_STEP))
    v_hat = v * (1.0 / (1.0 - ADAM_B2 ** ADAM_STEP))
    return -ADAM_LR * (m_hat / (jnp.sqrt(v_hat) + ADAM_EPS) + ADAM_WD * w), m, v


def adamw_big(ws, g_halves, ms, vs, core):
    n = len(ws)
    per_layer = N_STEPS // DEPTH
    per_half = per_layer // 2

    def body(core_ref, *refs):
        s = pl.program_id(0)
        first_layer = s < per_layer
        mine = ((s % per_layer) // per_half) == core_ref[0]
        for i in range(n):
            w, m, v = (refs[k * n + i][...] for k in range(3))
            g00, g01, g10, g11 = (refs[3 * n + 4 * i + k][...] for k in range(4))
            g = jnp.where(first_layer, jnp.where(mine, g00, g01), jnp.where(mine, g10, g11))[None]
            delta, m, v = _adam_update(w, g, m, v)
            for k, val in enumerate((g, delta, m, v)):
                refs[7 * n + k * n + i][...] = val

    whole = lambda a: pl.BlockSpec((1, a.shape[1] // per_layer, a.shape[2]),
                                   lambda s, cr: (s // per_layer, s % per_layer, 0))

    def half(a, layer, own):
        def index(s, cr):
            first = layer * per_layer + per_half * jnp.where(own, cr[0], 1 - cr[0])
            return (jnp.clip(s - first, 0, per_half - 1), 0)
        return pl.BlockSpec((a.shape[1] // per_layer, a.shape[2]), index)

    g_specs, g_ops = [], []
    for a, halves in zip(ws, g_halves):
        g_specs += [half(a, 0, True), half(a, 0, False), half(a, 1, True), half(a, 1, False)]
        g_ops += list(halves)
    outs = pl.pallas_call(
        body, name="adamw_big",
        grid_spec=pltpu.PrefetchScalarGridSpec(
            num_scalar_prefetch=1, grid=(N_STEPS,), in_specs=[whole(a) for a in ws] * 3 + g_specs,
            out_specs=[whole(a) for a in ws] * 4),
        out_shape=[jax.ShapeDtypeStruct(a.shape, F32) for a in ws] * 4,
        compiler_params=_plain_params(1),
    )(core, *ws, *ms, *vs, *g_ops)
    return outs[:n], outs[n:2 * n], outs[2 * n:3 * n], outs[3 * n:]


def adamw_small(ws, gs, ms, vs):
    n = len(ws)

    def body(*refs):
        for i in range(n):
            outs = _adam_update(*(refs[k * n + i][...] for k in range(4)))
            for k, val in enumerate(outs):
                refs[4 * n + k * n + i][...] = val

    specs = [_const_block(a.shape) for a in ws]
    outs = pl.pallas_call(
        body, name="adamw_small", grid=(1,), in_specs=specs * 4, out_specs=specs * 3,
        out_shape=[jax.ShapeDtypeStruct(a.shape, F32) for a in ws] * 3,
        compiler_params=_plain_params(1),
    )(*ws, *gs, *ms, *vs)
    return outs[:n], outs[n:2 * n], outs[2 * n:]


_SMALL = (("w_s", (DEPTH, SGU_G, BLK, BLK)), ("ln_in_g", (D_MODEL,)), ("ln_in_b", (D_MODEL,)), ("b_in", (DEPTH, N_COLS)),
          ("sinks", (DEPTH, 8)), ("vn_g", (DEPTH, SGU_W)), ("vn_b", (DEPTH, SGU_W)),
          ("b_s", (DEPTH, SGU_G, BLK)), ("b_out", (DEPTH, D_MODEL)), ("ln_g", (DEPTH, D_MODEL)), ("ln_b", (DEPTH, D_MODEL)))
_BIG = ("w_in", "p_a", "p_b", "w_out")


def _size(shape):
    n = 1
    for d in shape:
        n *= d
    return n


def _pack_small(vals, last_row=None):
    rows = []
    for name, shape in _SMALL:
        v = vals[name]
        if _size(shape) % 128:
            v = jnp.pad(v.reshape((1, -1)), ((0, 0), (0, (-_size(shape)) % 128)))
        rows.append(v.reshape((-1, 128)))
    used = sum(r.shape[0] for r in rows)
    tail = jnp.zeros((SMALL_ROWS - used, 128), F32)
    if last_row is not None:
        tail = tail.at[-1, 0].set(last_row)
    return jnp.concatenate([rows[0], jnp.concatenate(rows[1:] + [tail], axis=0)], axis=0)


def _unpack_small(packed):
    out, pos = {}, 0
    for name, shape in _SMALL:
        n = -(-_size(shape) // 128)
        rows = packed[pos:pos + n]
        out[name] = (rows.reshape((-1,))[:_size(shape)] if _size(shape) % 128 else rows).reshape(shape)
        pos += n
    return out


def kernel(x, ln_in_g, ln_in_b, w_in, b_in, sinks, vn_g, vn_b, w_s, b_s, p_a, p_b, w_out, b_out, ln_g, ln_b, loss_target, m_ln_in_g, m_ln_in_b, m_w_in, m_b_in, m_sinks, m_vn_g, m_vn_b, m_w_s, m_b_s, m_p_a, m_p_b, m_w_out, m_b_out, m_ln_g, m_ln_b, v_ln_in_g, v_ln_in_b, v_w_in, v_b_in, v_sinks, v_vn_g, v_vn_b, v_w_s, v_b_s, v_p_a, v_p_b, v_w_out, v_b_out, v_ln_g, v_ln_b):
    weights = dict(ln_in_g=ln_in_g, ln_in_b=ln_in_b, w_in=w_in, b_in=b_in, sinks=sinks, vn_g=vn_g, vn_b=vn_b, w_s=w_s,
                   b_s=b_s, p_a=p_a, p_b=p_b, w_out=w_out, b_out=b_out, ln_g=ln_g, ln_b=ln_b)
    mom1 = dict(ln_in_g=m_ln_in_g, ln_in_b=m_ln_in_b, w_in=m_w_in, b_in=m_b_in, sinks=m_sinks, vn_g=m_vn_g, vn_b=m_vn_b,
                w_s=m_w_s, b_s=m_b_s, p_a=m_p_a, p_b=m_p_b, w_out=m_w_out, b_out=m_b_out, ln_g=m_ln_g, ln_b=m_ln_b)
    mom2 = dict(ln_in_g=v_ln_in_g, ln_in_b=v_ln_in_b, w_in=v_w_in, b_in=v_b_in, sinks=v_sinks, vn_g=v_vn_g, vn_b=v_vn_b,
                w_s=v_w_s, b_s=v_b_s, p_a=v_p_a, p_b=v_p_b, w_out=v_w_out, b_out=v_b_out, ln_g=v_ln_g, ln_b=v_ln_b)
    n_seq, S, _ = x.shape
    T = n_seq * S
    c = lax.axis_index("c")
    j = 2 * lax.axis_index("x") + lax.axis_index("y")
    core = c.astype(jnp.int32).reshape((1,))
    chips = jnp.stack([j] + [r + (r >= j) for r in range(N_CHIPS - 1)]).astype(jnp.int32)
    names = list(_PIECES)
    xt, tt = x.reshape((T, D_MODEL)), loss_target.reshape((T, D_MODEL))

    tview = lambda d: dict(d, w_in=jnp.swapaxes(d["w_in"], 1, 2))
    weights_t, mom1_t, mom2_t = tview(weights), tview(mom1), tview(mom2)

    def own_shards(l):
        return [weights_t[n][l].astype(MXU_DTYPE) for n in _BIG]

    def layer_weights(l, gathered):
        d = dict(zip(_PIECES, gathered))
        d.update(paT=d["pa"].T, pbT=d["pb"].T, woutT=d["wout"].T)
        d.update(bin=b_in[l][None], sinks=sinks[l], vng=vn_g[l][None], vnb=vn_b[l][None], ws=w_s[l],
                 bs=jnp.repeat(b_s[l].T, 128, axis=1), bout=b_out[l][None], lng=ln_g[l][None], lnb=ln_b[l][None])
        return d

    own0, own1 = own_shards(0), own_shards(1)
    x0, g0 = input_norm_forward(xt, ln_in_g[None], ln_in_b[None], "ln_in_fwd", comm=gather_weights(names, own0))
    lw0 = layer_weights(0, g0)
    n_proj = T // ROWS_PROJ
    h0, g1 = project(x0, lw0["winT"], lw0["bin"], "layer0_proj", comm=gather_weights(names, own1, turn_step=(3 * n_proj) // 4))
    lw1 = layer_weights(1, g1)
    x1, h1 = layer_forward_and_projection(x0, h0, lw0, lw1["winT"], lw1["bin"], S, "layer0_fwd_layer1_proj")

    dres1, dh1, acc1, _ = layer_backward(x1, h1, tt, lw1, S, True, "layer1_bwd")
    dx1, dwinT1, _ = input_grads(dres1, dh1, x1, lw1["winT"], "layer1_dx_dwin")
    parts1 = [dwinT1, acc1["dpa"], acc1["dpb"], acc1["dwout"]]
    dres0, dh0, acc0, landed1 = layer_backward(x0, h0, dx1, lw0, S, False, "layer0_bwd", comm=pair_send_halves(names, parts1))
    sums1 = add_own_halves(names, parts1, landed1, core, "add_own_halves1")
    dx0, dwinT0, from_chips1 = input_grads(dres0, dh0, x0, lw0["winT"], "layer0_dx_dwin", comm=chip_exchange(sums1))
    totals1 = sum_chips(sums1, from_chips1, chips, "sum_chips1")
    grad_x, d_ln_in_g, d_ln_in_b, _ = input_norm_backward(xt, dx0, ln_in_g, "ln_in_bwd")

    gl = [acc0, acc1]
    per_layer = lambda key, pick: jnp.stack([pick(gl[l][key]) for l in range(DEPTH)])
    row0 = lambda key: per_layer(key, lambda a: a[0])
    small_part = _pack_small(dict(
        ln_in_g=d_ln_in_g[0], ln_in_b=d_ln_in_b[0], b_in=row0("dbin"), sinks=per_layer("dsink", lambda a: a[:, 0]),
        vn_g=row0("dvng"), vn_b=row0("dvnb"), w_s=per_layer("dws", lambda a: a),
        b_s=per_layer("dbs", lambda a: a[:, ::128].T), b_out=row0("dbout"), ln_g=row0("dlng"), ln_b=row0("dlnb")),
        last_row=acc1["loss"][0, 0])
    parts0 = [dwinT0, acc0["dpa"], acc0["dpb"], acc0["dwout"]]
    first = _run_comm(pair_send_halves(names, parts0, small_part, extra=totals1), "pair_send_halves0")
    landed0, small_landed, sib_totals1 = first[:4], first[4], first[5:]
    sums0 = add_own_halves(names, parts0, landed0, core, "add_own_halves0")
    small_sum = add_small_half(small_part, small_landed, core)
    *from_chips0, small_from_chips = _run_comm(chip_exchange(sums0, small_sum), "chip_exchange0")
    *totals0, small_total = sum_chips(sums0, from_chips0, chips, "sum_chips0", small_landed=small_from_chips)
    *sib_totals0, sib_small = _run_comm(pair_send_totals(totals0 + [small_total]), "pair_send_totals0")

    def both_halves(mine, theirs):
        return jnp.where(c == 0, jnp.concatenate([mine, theirs], axis=0), jnp.concatenate([theirs, mine], axis=0))

    g_halves = [(totals0[pi], sib_totals0[pi], totals1[pi], sib_totals1[pi]) for pi in range(len(_BIG))]
    g_small = both_halves(small_total, sib_small)
    loss = g_small[-1, 0]

    big = adamw_big([weights_t[n] for n in _BIG], g_halves, [mom1_t[n] for n in _BIG], [mom2_t[n] for n in _BIG], core)
    small_names = [n for n, _ in _SMALL]
    as2d = lambda a: a.reshape((1, -1)) if a.ndim == 1 else a
    g_named = _unpack_small(g_small)
    small = adamw_small(*[[as2d(d[n]) for n in small_names] for d in (weights, g_named, mom1, mom2)])

    def named(big_outs, small_outs):
        d = dict(zip(_BIG, big_outs))
        d["w_in"] = jnp.swapaxes(d["w_in"], 1, 2)
        d.update({n: o.reshape(weights[n].shape) for n, o in zip(small_names, small_outs)})
        return d

    order = ("ln_in_g", "ln_in_b", "w_in", "b_in", "sinks", "vn_g", "vn_b", "w_s", "b_s", "p_a", "p_b", "w_out", "b_out",
             "ln_g", "ln_b")
    res = [loss, grad_x.reshape(x.shape)]
    groups = [named(big[0], [g_named[n] for n in small_names])] + [named(big[k + 1], small[k]) for k in range(3)]
    for group in groups:
        res.extend(group[n] for n in order)
    return tuple(res)
```

```python
import functools

import jax
import jax.numpy as jnp
from jax import lax
from jax.experimental import pallas as pl
from jax.experimental.pallas import tpu as pltpu

F32 = jnp.float32
MXU_DTYPE = jnp.bfloat16

D_MODEL = 1024
DEPTH = 2
HEAD_DIM = 64
ATTN_W = 512
KV_W = 128
BLK = 128
SGU_W = 512
SGU_G = 4
N_COLS = 4864
C_Q, C_K, C_V, C_GA, C_UB, C_VB, C_GB, C_RA, C_RB = 0, 512, 640, 768, 1280, 1792, 2304, 2816, 3840
ALPHA = (2.0 * DEPTH) ** 0.25
LN_EPS = 1e-5
SCALE = HEAD_DIM ** -0.5
NEG = float(jnp.finfo(jnp.float32).min)
GELU_C = 0.7978845608028654
GELU_A = 0.044715

ADAM_LR, ADAM_B1, ADAM_B2, ADAM_EPS, ADAM_WD, ADAM_STEP = 0.001, 0.9, 0.999, 1e-08, 0.01, 10

V7X_VMEM_BYTES = 64 * 1024 * 1024
V7X_VMEM_RESERVE = 3 * 1024 * 1024
ROWS = 256
ROWS_BWD = 256
ROWS_PROJ = 512

MESH = pl.DeviceIdType.MESH
N_CHIPS = 4


def _mm(a, b):
    return jnp.dot(a.astype(MXU_DTYPE), b.astype(MXU_DTYPE), preferred_element_type=F32)


def _mm_nt(a, b):
    return lax.dot_general(a.astype(MXU_DTYPE), b.astype(MXU_DTYPE), (((1,), (1,)), ((), ())),
                           preferred_element_type=F32)


def _mm_tn(a, b):
    return lax.dot_general(a.astype(MXU_DTYPE), b.astype(MXU_DTYPE), (((0,), (0,)), ((), ())),
                           preferred_element_type=F32)


def _sigmoid(x):
    return 0.5 * jnp.tanh(0.5 * x) + 0.5


def _gelu_parts(x):
    x2 = x * x
    u = 0.5 * jnp.tanh(x * (GELU_C + (GELU_C * GELU_A) * x2)) + 0.5
    dg = u * (1.0 + x * (1.0 - u) * (2.0 * GELU_C + (6.0 * GELU_C * GELU_A) * x2))
    return x * u, dg


def _ln_fwd(z, g, b):
    mu = jnp.mean(z, axis=-1, keepdims=True)
    zc = z - mu
    var = jnp.mean(zc * zc, axis=-1, keepdims=True)
    rstd = lax.rsqrt(var + LN_EPS)
    zhat = zc * rstd
    return zhat * g + b, zhat, rstd


def _ln_bwd(dy, zhat, rstd, g):
    dzh = dy * g
    m1 = jnp.mean(dzh, axis=-1, keepdims=True)
    m2 = jnp.mean(dzh * zhat, axis=-1, keepdims=True)
    return rstd * (dzh - m1 - zhat * m2)


def _colsum(v):
    return jnp.sum(v, axis=0, keepdims=True)


def _lane_lo(rows):
    return lax.broadcasted_iota(jnp.int32, (rows, 128), 1) < HEAD_DIM


def _dup_head(x2, hk, lo):
    xr = pltpu.roll(x2, HEAD_DIM, 1)
    return jnp.where(lo, x2, xr) if hk == 0 else jnp.where(lo, xr, x2)


def _fold_head(y0, y1, lo):
    f0 = y0 + pltpu.roll(y0, HEAD_DIM, 1)
    f1 = y1 + pltpu.roll(y1, HEAD_DIM, 1)
    return jnp.where(lo, f0, f1)


def _stack_heads(t0, t1, lo):
    z = jnp.zeros_like(t0)
    return jnp.concatenate([jnp.where(lo, t0, z), jnp.where(lo, z, t0),
                            jnp.where(lo, t1, z), jnp.where(lo, z, t1)], axis=0)


def _unstack_heads(o, lo):
    return (jnp.where(lo, o[0:128], o[128:256]), jnp.where(lo, o[256:384], o[384:512]))


def _band_bias(start):
    row = lax.broadcasted_iota(jnp.int32, (4 * BLK, 2 * BLK), 0) & (BLK - 1)
    kpos = lax.broadcasted_iota(jnp.int32, (4 * BLK, 2 * BLK), 1)
    valid = (kpos > row) & (kpos <= row + BLK)
    if start is not None:
        valid = valid & (jnp.logical_not(start) | (kpos >= BLK))
    return jnp.where(valid, 0.0, NEG)


def _attn_probs(s, sink4, bias):
    s = s * SCALE + bias
    r1 = lax.broadcasted_iota(jnp.int32, (4 * BLK, 1), 0)
    sk = jnp.where(r1 < BLK, sink4[0], jnp.where(r1 < 2 * BLK, sink4[1], jnp.where(r1 < 3 * BLK, sink4[2], sink4[3])))
    m = jnp.maximum(jnp.max(s, axis=-1, keepdims=True), sk)
    p = jnp.exp(s - m)
    es = jnp.exp(sk - m)
    inv = 1.0 / (jnp.sum(p, axis=-1, keepdims=True) + es)
    return p * inv, es * inv


_PROJ_CHUNK = 512


def _kv_blocks(kvh, h_s, nb):
    ks = [kvh[:, 0:KV_W]] + [h_s[n * BLK:(n + 1) * BLK, C_K:C_K + KV_W] for n in range(nb)]
    vs = [kvh[:, KV_W:2 * KV_W]] + [h_s[n * BLK:(n + 1) * BLK, C_V:C_V + KV_W] for n in range(nb)]
    return ks, vs


def _attn_operands(h_s, ks, vs, n, hk, lo, lo2):
    rows = slice(n * BLK, (n + 1) * BLK)
    kd = _dup_head(jnp.concatenate([ks[n], ks[n + 1]], axis=0), hk, lo2)
    vd = _dup_head(jnp.concatenate([vs[n], vs[n + 1]], axis=0), hk, lo2)
    c0 = C_Q + 2 * hk * 128
    qs = _stack_heads(h_s[rows, c0:c0 + 128], h_s[rows, c0 + 128:c0 + 256], lo)
    return qs, kd, vd


def _sgu_weights(w):
    tri = (lax.broadcasted_iota(jnp.int32, (BLK, BLK), 0) >= lax.broadcasted_iota(jnp.int32, (BLK, BLK), 1))
    return tri, [jnp.where(tri, w["ws"][g], 0.0) for g in range(SGU_G)]


def _layer_forward(x, kvh, start, w, h_s, attn_s, mix_s, keep, tick=lambda: None):
    R = x.shape[0]
    nb = R // BLK
    lo = _lane_lo(BLK)
    lo2 = _lane_lo(2 * BLK)
    ks, vs = _kv_blocks(kvh, h_s, nb)
    sinks = [w["sinks"][j] for j in range(8)]
    bias_first = _band_bias(start)
    bias_rest = _band_bias(None) if nb > 1 else None
    pairs = [(n, hk) for n in range(nb) for hk in range(2)]
    ops = [_attn_operands(h_s, ks, vs, n, hk, lo, lo2) for n, hk in pairs]
    scores = [_mm_nt(qs, kd) for qs, kd, _ in ops]
    probs_l = [_attn_probs(s, sinks[4 * hk:4 * hk + 4], bias_first if n == 0 else bias_rest)
               for s, (n, hk) in zip(scores, pairs)]
    outs = [_mm(p, vd) for (p, _), (_, _, vd) in zip(probs_l, ops)]
    for o, (n, hk) in zip(outs, pairs):
        rows = slice(n * BLK, (n + 1) * BLK)
        t0, t1 = _unstack_heads(o, lo)
        attn_s[rows, 2 * hk * 128:(2 * hk + 1) * 128] = t0
        attn_s[rows, (2 * hk + 1) * 128:(2 * hk + 2) * 128] = t1
        tick()
    ga = h_s[:, C_GA:C_GA + ATTN_W]
    ya = attn_s[...] * (ga * _sigmoid(ga))
    tick()
    gu, _ = _gelu_parts(h_s[:, C_UB:C_UB + SGU_W])
    tick()
    gv, _ = _gelu_parts(h_s[:, C_VB:C_VB + SGU_W])
    vn, _, _ = _ln_fwd(gv, w["vng"][...], w["vnb"][...])
    tick()
    _, wms = _sgu_weights(w)
    tiles = [(slice(n * BLK, (n + 1) * BLK), g, slice(g * 128, (g + 1) * 128)) for n in range(nb) for g in range(SGU_G)]
    mixes = [_mm(wms[g], vn[rows, cols]) for rows, g, cols in tiles]
    for m, (rows, g, cols) in zip(mixes, tiles):
        mix_s[rows, cols] = m + w["bs"][:, cols]
    gb = h_s[:, C_GB:C_GB + SGU_W]
    yb = gu * mix_s[...] * (gb * _sigmoid(gb))
    tick()
    a = _mm(ya, w["pa"][...])
    b = _mm(yb, w["pb"][...])
    tick()
    merged = _sigmoid(h_s[:, C_RA:C_RA + D_MODEL]) * a + _sigmoid(h_s[:, C_RB:C_RB + D_MODEL]) * b
    tick()
    out = _mm(merged, w["wout"][...]) + w["bout"][...]
    y, zhat, rstd = _ln_fwd(ALPHA * x + out, w["lng"][...], w["lnb"][...])
    if not keep:
        return y, None
    return y, dict(ks=ks, vs=vs, probs=probs_l, ya=ya, yb=yb, a=a, b=b, merged=merged, zhat=zhat, rstd=rstd)


def _dsilu(g, sg):
    return sg * (1.0 + g * (1.0 - sg))


_W_NAMES = ("sinks", "vng", "vnb", "ws", "bs", "pa", "pb", "wout", "bout", "lng", "lnb")
_WT_NAMES = ("paT", "pbT", "woutT")
_ACC_NAMES = ("dwout", "dpa", "dpb", "dbin", "dbout", "dlng", "dlnb", "dvng", "dvnb", "dws", "dbs", "dsink", "loss")


def _make_fwd_proj_body(R, S, nT):
    def body(x_ref, h_ref, kvh_ref, wn_ref, bn_ref, *rest):
        w = dict(zip(_W_NAMES, rest[:len(_W_NAMES)]))
        y_ref, hn_ref, attn_s, mix_s, yprev_s = rest[len(_W_NAMES):]

        @pl.when(pl.program_id(0) == 0)
        def _():
            yprev_s[...] = jnp.zeros(yprev_s.shape, yprev_s.dtype)

        t = jnp.minimum(pl.program_id(0), nT - 1)
        start = (t % (S // R)) == 0
        chunks = [(c0, min(c0 + _PROJ_CHUNK, N_COLS)) for c0 in range(0, N_COLS, _PROJ_CHUNK)]

        def tick():
            if chunks:
                c0, c1 = chunks.pop(0)
                hn_ref[:, c0:c1] = _mm_nt(yprev_s[...], wn_ref[c0:c1, :]) + bn_ref[:, c0:c1]

        y, _ = _layer_forward(x_ref[...], kvh_ref[...], start, w, h_ref, attn_s, mix_s, keep=False, tick=tick)
        while chunks:
            tick()
        y_ref[...] = y
        yprev_s[...] = y.astype(yprev_s.dtype)
    return body


def _make_bwd_body(R, S, nT, is_last):
    nb = R // BLK
    names = _W_NAMES + _WT_NAMES
    nw = len(names)

    def body(x_ref, h_s, kvh_ref, aux_ref, *rest):
        w = dict(zip(names, rest[:nw]))
        dres_ref, dh_ref = rest[nw:nw + 2]
        acc = dict(zip(_ACC_NAMES, rest[nw + 2:nw + 2 + len(_ACC_NAMES)]))
        attn_s, mix_s, ckv_s = rest[nw + 2 + len(_ACC_NAMES):]
        i = pl.program_id(0)
        start = ((nT - 1 - i) % (S // R)) == 0
        lo = _lane_lo(BLK)
        lo2 = _lane_lo(2 * BLK)

        @pl.when(i == 0)
        def _():
            for name in _ACC_NAMES:
                acc[name][...] = jnp.zeros(acc[name].shape, F32)
            ckv_s[...] = jnp.zeros(ckv_s.shape, F32)

        x = x_ref[...]
        y, sv = _layer_forward(x, kvh_ref[...], start, w, h_s, attn_s, mix_s, keep=True)
        if is_last:
            diff = y - aux_ref[...]
            part = 0.5 * jnp.sum(jnp.mean(diff * diff, axis=-1, keepdims=True), axis=0, keepdims=True)
            acc["loss"][...] += jnp.broadcast_to(part, acc["loss"].shape)
            dy = diff * (1.0 / D_MODEL)
        else:
            dy = aux_ref[...]

        def put(c0, val, rows=slice(None)):
            width = val.shape[1]
            dh_ref[rows, c0:c0 + width] = val.astype(dh_ref.dtype)
            acc["dbin"][:, c0:c0 + width] += _colsum(val)

        acc["dlng"][...] += _colsum(dy * sv["zhat"])
        acc["dlnb"][...] += _colsum(dy)
        dz = _ln_bwd(dy, sv["zhat"], sv["rstd"], w["lng"][...])
        dres_ref[...] = ALPHA * dz
        acc["dbout"][...] += _colsum(dz)
        acc["dwout"][...] += _mm_tn(sv["merged"], dz)
        dmerged = _mm(dz, w["woutT"][...])
        sa = _sigmoid(h_s[:, C_RA:C_RA + D_MODEL])
        da = dmerged * sa
        put(C_RA, da * sv["a"] * (1.0 - sa))
        sb = _sigmoid(h_s[:, C_RB:C_RB + D_MODEL])
        db = dmerged * sb
        put(C_RB, db * sv["b"] * (1.0 - sb))
        acc["dpa"][...] += _mm_tn(sv["ya"], da)
        acc["dpb"][...] += _mm_tn(sv["yb"], db)
        dya = _mm(da, w["paT"][...])
        dyb = _mm(db, w["pbT"][...])
        ga = h_s[:, C_GA:C_GA + ATTN_W]
        sga = _sigmoid(ga)
        put(C_GA, dya * attn_s[...] * _dsilu(ga, sga))
        attn_s[...] = dya * (ga * sga)
        ks, vs = sv["ks"], sv["vs"]
        pairs = [(n, hk) for n in range(nb) for hk in range(2)]
        ops = [_attn_operands(h_s, ks, vs, n, hk, lo, lo2) for n, hk in pairs]
        doss = [_stack_heads(attn_s[n * BLK:(n + 1) * BLK, 2 * hk * 128:(2 * hk + 1) * 128],
                             attn_s[n * BLK:(n + 1) * BLK, (2 * hk + 1) * 128:(2 * hk + 2) * 128], lo) for n, hk in pairs]
        dps = [_mm_nt(dos, vd) for dos, (_, _, vd) in zip(doss, ops)]
        gb = h_s[:, C_GB:C_GB + SGU_W]
        sgb = _sigmoid(gb)
        gu, dgu = _gelu_parts(h_s[:, C_UB:C_UB + SGU_W])
        mixed = mix_s[...]
        put(C_GB, dyb * (gu * mixed) * _dsilu(gb, sgb))
        dsgu = dyb * (gb * sgb)
        put(C_UB, dsgu * mixed * dgu)
        dmixed = dsgu * gu
        dsss = []
        for dp, (probs, ps), (n, hk) in zip(dps, sv["probs"], pairs):
            delta = jnp.sum(probs * dp, axis=-1, keepdims=True)
            dsk = -(ps * delta)
            for g in range(4):
                j = 4 * hk + g
                acc["dsink"][j:j + 1, :] += jnp.broadcast_to(_colsum(dsk[g * BLK:(g + 1) * BLK]), (1, 128))
            dsss.append(probs * (dp - delta) * SCALE)
        dqs = [_mm(dss, kd) for dss, (_, kd, _) in zip(dsss, ops)]
        ydks = [_mm_tn(dss, qs) for dss, (qs, _, _) in zip(dsss, ops)]
        ydvs = [_mm_tn(probs, dos) for (probs, _), dos in zip(sv["probs"], doss)]
        gv, dgv = _gelu_parts(h_s[:, C_VB:C_VB + SGU_W])
        vn, vhat, vrstd = _ln_fwd(gv, w["vng"][...], w["vnb"][...])
        tri, wms = _sgu_weights(w)
        tiles = [(slice(n * BLK, (n + 1) * BLK), g, slice(g * 128, (g + 1) * 128)) for n in range(nb) for g in range(SGU_G)]
        dw_parts = [_mm_nt(dmixed[rows, cols], vn[rows, cols]) for rows, g, cols in tiles]
        dvn_parts = [_mm_tn(wms[g], dmixed[rows, cols]) for rows, g, cols in tiles]
        for g in range(SGU_G):
            tot = dw_parts[g]
            for n in range(1, nb):
                tot = tot + dw_parts[n * SGU_G + g]
            acc["dws"][g] += jnp.where(tri, tot, 0.0)
        for part, (rows, g, cols) in zip(dvn_parts, tiles):
            mix_s[rows, cols] = part
        dbs_part = dmixed[0:BLK]
        for n in range(1, nb):
            dbs_part = dbs_part + dmixed[n * BLK:(n + 1) * BLK]
        acc["dbs"][...] += dbs_part
        for dq, (n, hk) in zip(dqs, pairs):
            q0, q1 = _unstack_heads(dq, lo)
            put(C_Q + 2 * hk * 128, q0, slice(n * BLK, (n + 1) * BLK))
            put(C_Q + (2 * hk + 1) * 128, q1, slice(n * BLK, (n + 1) * BLK))
        dks = [jnp.zeros((BLK, KV_W), F32) for _ in range(nb + 1)]
        dvs = [jnp.zeros((BLK, KV_W), F32) for _ in range(nb + 1)]
        for n in range(nb):
            dk2 = _fold_head(ydks[2 * n], ydks[2 * n + 1], lo2)
            dv2 = _fold_head(ydvs[2 * n], ydvs[2 * n + 1], lo2)
            dks[n] = dks[n] + dk2[0:BLK]
            dks[n + 1] = dks[n + 1] + dk2[BLK:2 * BLK]
            dvs[n] = dvs[n] + dv2[0:BLK]
            dvs[n + 1] = dvs[n + 1] + dv2[BLK:2 * BLK]
        dks[nb] = dks[nb] + ckv_s[:, 0:KV_W]
        dvs[nb] = dvs[nb] + ckv_s[:, KV_W:2 * KV_W]
        ckv_s[:, 0:KV_W] = dks[0]
        ckv_s[:, KV_W:2 * KV_W] = dvs[0]
        put(C_K, jnp.concatenate(dks[1:], axis=0))
        put(C_V, jnp.concatenate(dvs[1:], axis=0))
        dvn = mix_s[...]
        acc["dvng"][...] += _colsum(dvn * vhat)
        acc["dvnb"][...] += _colsum(dvn)
        put(C_VB, _ln_bwd(dvn, vhat, vrstd, w["vng"][...]) * dgv)

        @pl.when(i == nT - 1)
        def _():
            for g in range(SGU_G):
                cols = slice(g * 128, (g + 1) * 128)
                tot = jnp.sum(acc["dbs"][:, cols], axis=1, keepdims=True)
                acc["dbs"][:, cols] = jnp.broadcast_to(tot, (BLK, 128))

    return body


def _resident():
    return pl.BlockSpec(memory_space=pltpu.VMEM)


_BIG_RESIDENT = ("pa", "pb", "wout", "paT", "pbT", "woutT", "dwout", "dpa", "dpb")


def _const_block(shape):
    return pl.BlockSpec(tuple(shape), lambda *_: (0,) * len(shape))


def _operand_spec(name, shape):
    if name == "sinks":
        return pl.BlockSpec(memory_space=pltpu.SMEM)
    return _resident() if name in _BIG_RESIDENT else _const_block(shape)


def _weight_specs(names, wts):
    return [_operand_spec(n, wts[n].shape) for n in names]


def _nbytes(a):
    n = jnp.dtype(a.dtype).itemsize
    for d in a.shape:
        n *= d
    return n


def _layer_params(resident=()):
    scoped = V7X_VMEM_BYTES - V7X_VMEM_RESERVE - sum(_nbytes(a) for a in resident)
    return pltpu.CompilerParams(dimension_semantics=("arbitrary",), vmem_limit_bytes=scoped)


class _Comm:
    def __init__(self, ins, out_shapes, sems, start, finish, aliases=None, turn=None, turn_step=None):
        self.ins = [pltpu.with_memory_space_constraint(a, pltpu.HBM) for a in ins]
        self.out_shapes = [pltpu.HBM(s.shape, s.dtype) for s in out_shapes]
        self.sems = list(sems)
        self.start, self.finish, self.aliases = start, finish, dict(aliases or {})
        self.turn, self.turn_step = turn, turn_step


_ANY = pl.BlockSpec(memory_space=pltpu.HBM)
_DMA = pltpu.SemaphoreType.DMA


def _call(body, *, name, grid, in_specs, out_specs, out_shape, scratch_shapes, compiler_params, operands, comm=None):
    if comm is None:
        res = pl.pallas_call(body, name=name, grid=grid, in_specs=in_specs, out_specs=out_specs, out_shape=out_shape,
                             scratch_shapes=scratch_shapes, compiler_params=compiler_params)(*operands)
        return list(res), []
    n_in, n_out, n_scr = len(in_specs), len(out_specs), len(scratch_shapes)
    ci, co = len(comm.ins), len(comm.out_shapes)
    last = grid[0] - 1

    def hosted(*refs):
        ins, cin = refs[:n_in], refs[n_in:n_in + ci]
        p = n_in + ci
        outs, cout = refs[p:p + n_out], refs[p + n_out:p + n_out + co]
        p += n_out + co
        scr, sems = refs[p:p + n_scr], refs[p + n_scr:]
        pl.when(pl.program_id(0) == 0)(lambda: comm.start(cin, cout, sems))
        body(*ins, *outs, *scr)
        if comm.turn is not None:
            at = last if comm.turn_step is None else comm.turn_step
            pl.when(pl.program_id(0) == at)(lambda: comm.turn(cin, cout, sems))
        pl.when(pl.program_id(0) == last)(lambda: comm.finish(cin, cout, sems))

    res = pl.pallas_call(
        hosted, name=name, grid=grid, in_specs=list(in_specs) + [_ANY] * ci, out_specs=list(out_specs) + [_ANY] * co,
        out_shape=list(out_shape) + comm.out_shapes, scratch_shapes=list(scratch_shapes) + comm.sems,
        input_output_aliases={n_in + a: n_out + b for a, b in comm.aliases.items()},
        compiler_params=compiler_params)(*operands, *comm.ins)
    return list(res[:n_out]), list(res[n_out:])


def _run_comm(comm, name):
    ci, co = len(comm.ins), len(comm.out_shapes)

    def body(*refs):
        cin, cout, sems = refs[:ci], refs[ci:ci + co], refs[ci + co:]
        comm.start(cin, cout, sems)
        if comm.turn is not None:
            comm.turn(cin, cout, sems)
        comm.finish(cin, cout, sems)

    return list(pl.pallas_call(body, name=name, in_specs=[_ANY] * ci, out_specs=[_ANY] * co, out_shape=comm.out_shapes,
                               scratch_shapes=comm.sems, input_output_aliases=comm.aliases)(*comm.ins))


def project(x, winT, b, name, comm=None):
    T = x.shape[0]
    R = ROWS_PROJ

    def body(x_ref, w_ref, b_ref, h_ref):
        xb = x_ref[...].astype(MXU_DTYPE)
        for c0 in range(0, N_COLS, _PROJ_CHUNK):
            c1 = min(c0 + _PROJ_CHUNK, N_COLS)
            h_ref[:, c0:c1] = _mm_nt(xb, w_ref[c0:c1, :]) + b_ref[:, c0:c1]

    (h,), extra = _call(
        body, name=name, grid=(T // R,),
        in_specs=[pl.BlockSpec((R, D_MODEL), lambda t: (t, 0)), _resident(), _const_block(b.shape)],
        out_specs=[pl.BlockSpec((R, N_COLS), lambda t: (t, 0))],
        out_shape=[jax.ShapeDtypeStruct((T, N_COLS), F32)], scratch_shapes=[],
        compiler_params=_layer_params([winT]), operands=(x, winT, b), comm=comm)
    return h, extra


_KV_BLOCK = C_K // (2 * KV_W)


def layer_forward_and_projection(x, h, wts, winT_next, b_next, S, name):
    T = x.shape[0]
    R = ROWS
    nT = T // R
    npt = R // BLK
    cur = lambda t: jnp.minimum(t, nT - 1)
    (y, hn), _ = _call(
        _make_fwd_proj_body(R, S, nT), name=name, grid=(nT + 1,),
        in_specs=[pl.BlockSpec((R, D_MODEL), lambda t: (cur(t), 0)), pl.BlockSpec((R, N_COLS), lambda t: (cur(t), 0)),
                  pl.BlockSpec((BLK, 2 * KV_W), lambda t: (jnp.maximum(cur(t) * npt - 1, 0), _KV_BLOCK)),
                  _resident(), _const_block(b_next.shape)] + _weight_specs(_W_NAMES, wts),
        out_specs=[pl.BlockSpec((R, D_MODEL), lambda t: (cur(t), 0)),
                   pl.BlockSpec((R, N_COLS), lambda t: (jnp.maximum(t - 1, 0), 0))],
        out_shape=[jax.ShapeDtypeStruct((T, D_MODEL), F32), jax.ShapeDtypeStruct((T, N_COLS), F32)],
        scratch_shapes=[pltpu.VMEM((R, ATTN_W), F32), pltpu.VMEM((R, SGU_W), F32), pltpu.VMEM((R, D_MODEL), MXU_DTYPE)],
        compiler_params=_layer_params([winT_next] + [wts[n] for n in _W_NAMES if n in _BIG_RESIDENT]),
        operands=(x, h, h, winT_next, b_next, *[wts[n] for n in _W_NAMES]))
    return y, hn


_ACC_SHAPES = dict(dwout=(D_MODEL, D_MODEL), dpa=(ATTN_W, D_MODEL), dpb=(SGU_W, D_MODEL), dbin=(1, N_COLS),
                   dbout=(1, D_MODEL), dlng=(1, D_MODEL), dlnb=(1, D_MODEL), dvng=(1, SGU_W), dvnb=(1, SGU_W),
                   dws=(SGU_G, BLK, BLK), dbs=(BLK, SGU_W), dsink=(8, 128), loss=(8, 128))


def layer_backward(x, h, aux, wts, S, is_last, name, comm=None):
    T = x.shape[0]
    R = ROWS_BWD
    nT = T // R
    npt = R // BLK
    rev = lambda i: (nT - 1 - i, 0)
    halo = lambda i: (jnp.maximum((nT - 1 - i) * npt - 1, 0), _KV_BLOCK)
    names = _W_NAMES + _WT_NAMES
    out_shape = ([jax.ShapeDtypeStruct((T, D_MODEL), F32), jax.ShapeDtypeStruct((T, N_COLS), MXU_DTYPE)]
                 + [jax.ShapeDtypeStruct(_ACC_SHAPES[n], F32) for n in _ACC_NAMES])
    outs, extra = _call(
        _make_bwd_body(R, S, nT, is_last), name=name, grid=(nT,),
        in_specs=[pl.BlockSpec((R, D_MODEL), rev), pl.BlockSpec((R, N_COLS), rev), pl.BlockSpec((BLK, 2 * KV_W), halo),
                  pl.BlockSpec((R, D_MODEL), rev)] + _weight_specs(names, wts),
        out_specs=[pl.BlockSpec((R, D_MODEL), rev), pl.BlockSpec((R, N_COLS), rev)]
        + [_operand_spec(n, _ACC_SHAPES[n]) for n in _ACC_NAMES],
        out_shape=out_shape,
        scratch_shapes=[pltpu.VMEM((R, ATTN_W), F32), pltpu.VMEM((R, SGU_W), F32), pltpu.VMEM((BLK, 2 * KV_W), F32)],
        compiler_params=_layer_params([wts[n] for n in names if n in _BIG_RESIDENT]
                                      + [jax.ShapeDtypeStruct(_ACC_SHAPES[n], F32) for n in _ACC_NAMES if n in _BIG_RESIDENT]),
        operands=(x, h, h, aux, *[wts[n] for n in names]), comm=comm)
    return outs[0], outs[1], dict(zip(_ACC_NAMES, outs[2:])), extra


def input_grads(dres, dh, x, winT, name, comm=None):
    T = x.shape[0]
    R = ROWS

    def body(dres_ref, dh_ref, x_ref, winT_ref, dx_ref, dwin_ref):
        @pl.when(pl.program_id(0) == 0)
        def _():
            dwin_ref[...] = jnp.zeros(dwin_ref.shape, F32)

        dh = dh_ref[...]
        dx_ref[...] = dres_ref[...] + jnp.dot(dh, winT_ref[...], preferred_element_type=F32)
        dwin_ref[...] += _mm_tn(dh, x_ref[...])

    row = lambda t: (t, 0)
    dwin_shape = jax.ShapeDtypeStruct((N_COLS, D_MODEL), F32)
    (dx, dwinT), extra = _call(
        body, name=name, grid=(T // R,),
        in_specs=[pl.BlockSpec((R, D_MODEL), row), pl.BlockSpec((R, N_COLS), row), pl.BlockSpec((R, D_MODEL), row),
                  _resident()],
        out_specs=[pl.BlockSpec((R, D_MODEL), row), _resident()],
        out_shape=[jax.ShapeDtypeStruct((T, D_MODEL), F32), dwin_shape], scratch_shapes=[],
        compiler_params=_layer_params([winT, dwin_shape]), operands=(dres, dh, x, winT), comm=comm)
    return dx, dwinT, extra


def input_norm_forward(x, g, b, name, comm=None):
    T = x.shape[0]
    R = ROWS_PROJ

    def body(x_ref, g_ref, b_ref, y_ref):
        y_ref[...] = _ln_fwd(x_ref[...], g_ref[...], b_ref[...])[0]

    row = lambda t: (t, 0)
    (y,), extra = _call(
        body, name=name, grid=(T // R,),
        in_specs=[pl.BlockSpec((R, D_MODEL), row), _const_block(g.shape), _const_block(b.shape)],
        out_specs=[pl.BlockSpec((R, D_MODEL), row)],
        out_shape=[jax.ShapeDtypeStruct((T, D_MODEL), F32)], scratch_shapes=[],
        compiler_params=_layer_params(), operands=(x, g, b), comm=comm)
    return y, extra


def input_norm_backward(x, dy, g, name, comm=None):
    T = x.shape[0]
    R = ROWS_PROJ

    def body(x_ref, dy_ref, g_ref, dx_ref, dg_ref, db_ref):
        @pl.when(pl.program_id(0) == 0)
        def _():
            dg_ref[...] = jnp.zeros(dg_ref.shape, F32)
            db_ref[...] = jnp.zeros(db_ref.shape, F32)

        dy = dy_ref[...]
        _, xhat, rstd = _ln_fwd(x_ref[...], g_ref[...], g_ref[...])
        dx_ref[...] = _ln_bwd(dy, xhat, rstd, g_ref[...])
        dg_ref[...] += _colsum(dy * xhat)
        db_ref[...] += _colsum(dy)

    row = lambda t: (t, 0)
    vec = jax.ShapeDtypeStruct((1, D_MODEL), F32)
    (dx, dg, db), extra = _call(
        body, name=name, grid=(T // R,),
        in_specs=[pl.BlockSpec((R, D_MODEL), row), pl.BlockSpec((R, D_MODEL), row), _const_block(g.shape)],
        out_specs=[pl.BlockSpec((R, D_MODEL), row), _const_block(vec.shape), _const_block(vec.shape)],
        out_shape=[jax.ShapeDtypeStruct((T, D_MODEL), F32), vec, vec], scratch_shapes=[],
        compiler_params=_layer_params(), operands=(x, dy, g), comm=comm)
    return dx, dg, db, extra


_PIECES = ("winT", "pa", "pb", "wout")
_WHOLE = dict(winT=(N_COLS, D_MODEL), pa=(ATTN_W, D_MODEL), pb=(SGU_W, D_MODEL), wout=(D_MODEL, D_MODEL))
_HALF = dict(winT=(N_COLS // 8, D_MODEL), pa=(ATTN_W // 2, D_MODEL // 4), pb=(SGU_W // 2, D_MODEL // 4),
             wout=(D_MODEL // 8, D_MODEL))
SMALL_ROWS = 1280


def _region(name, ref, k, h):
    hr, hc = _HALF[name]
    if name in ("pa", "pb"):
        return ref.at[pl.ds(h * hr, hr), pl.ds(k * hc, hc)]
    return ref.at[pl.ds(k * 2 * hr + h * hr, hr), :]


def _place():
    x, y, c = lax.axis_index("x"), lax.axis_index("y"), lax.axis_index("c")
    return x, y, c, 2 * x + y


def _as_chip(j, fn):
    for jj in range(N_CHIPS):
        pl.when(j == jj)(functools.partial(fn, jj))


def _remote(src, dst, send_sem, recv_sem, to):
    return pltpu.make_async_remote_copy(src_ref=src, dst_ref=dst, send_sem=send_sem, recv_sem=recv_sem,
                                        device_id=to, device_id_type=MESH)


def _core_of(k, c):
    return (k // 2, k % 2, c)


def _others(jj):
    return [k for k in range(N_CHIPS) if k != jj]


def _start_all(cps):
    for cp in cps:
        cp.start()


def _wait_all(cps):
    for cp in cps:
        cp.wait()


def gather_over_chips(names, shards):
    n = len(names)

    def own(cin, cout, sems, jj):
        x, y, c, _ = _place()
        cps = []
        for i, name in enumerate(names):
            hr, hc = _HALF[name]
            place = (cout[i].at[:, pl.ds(jj * hc, hc)] if name in ("pa", "pb")
                     else cout[i].at[pl.ds(jj * 2 * hr, 2 * hr), :])
            cps.append(_remote(cin[i], place, sems[2].at[i], sems[3].at[i], (x, y, 1 - c)))
        return cps

    def copies(cin, cout, sems, jj, c):
        out = []
        for i, name in enumerate(names):
            hr = _HALF[name][0]
            for k in _others(jj):
                out.append(_remote(cin[i].at[pl.ds(c * hr, hr)], _region(name, cout[i], jj, c),
                                   sems[0].at[4 * i + k], sems[1].at[4 * i + jj], _core_of(k, c)))
        return out

    def start(cin, cout, sems):
        _, _, c, j = _place()
        _as_chip(j, lambda jj: _start_all(copies(cin, cout, sems, jj, c) + own(cin, cout, sems, jj)))

    def finish(cin, cout, sems):
        _, _, c, j = _place()

        def run(jj):
            for i, name in enumerate(names):
                for k in _others(jj):
                    land = _region(name, cout[i], k, c)
                    _remote(land, land, sems[0].at[4 * i + k], sems[1].at[4 * i + k], _core_of(k, c)).wait_recv()
            for cp in copies(cin, cout, sems, jj, c):
                cp.wait_send()
            _wait_all(own(cin, cout, sems, jj))

        _as_chip(j, run)

    shapes = [jax.ShapeDtypeStruct(_WHOLE[nm], s.dtype) for nm, s in zip(names, shards)]
    return _Comm(shards, shapes, [_DMA((4 * n,)), _DMA((4 * n,)), _DMA((n,)), _DMA((n,))], start, finish)


def gather_weights(names, shards, turn_step=None):
    n = len(names)
    chips = gather_over_chips(names, shards)
    pair_start, pair_finish = _pair_phase(names)

    def turn(cin, cout, sems):
        chips.finish(cin, cout, sems[:4])
        pair_start(cin, cout, sems[4:])

    return _Comm(shards, chips.out_shapes, chips.sems + [_DMA((4 * n,)), _DMA((4 * n,))], chips.start,
                 lambda cin, cout, sems: pair_finish(cin, cout, sems[4:]), turn=turn, turn_step=turn_step)


def _pair_phase(names):
    def start(cin, cout, sems):
        x, y, c, j = _place()

        def run(jj):
            for i, name in enumerate(names):
                for k in _others(jj):
                    land = _region(name, cout[i], k, c)
                    _remote(land, land, sems[0].at[4 * i + k], sems[1].at[4 * i + k], (x, y, 1 - c)).start()

        _as_chip(j, run)

    def finish(cin, cout, sems):
        x, y, c, j = _place()

        def run(jj):
            for i, name in enumerate(names):
                for k in _others(jj):
                    theirs = _region(name, cout[i], k, 1 - c)
                    _remote(theirs, theirs, sems[0].at[4 * i + k], sems[1].at[4 * i + k], (x, y, 1 - c)).wait_recv()
            for i, name in enumerate(names):
                for k in _others(jj):
                    land = _region(name, cout[i], k, c)
                    _remote(land, land, sems[0].at[4 * i + k], sems[1].at[4 * i + k], (x, y, 1 - c)).wait_send()

        _as_chip(j, run)

    return start, finish


def pair_send_halves(names, parts, small=None, extra=()):
    n = len(names)
    hs = SMALL_ROWS // 2
    first_extra = n + (small is not None)

    def copies(cin, cout, sems):
        x, y, c, _ = _place()
        sib = (x, y, 1 - c)
        cps = []
        for i, name in enumerate(names):
            for k in range(N_CHIPS):
                cps.append(_remote(_region(name, cin[i], k, 1 - c), cout[i].at[k], sems[0].at[4 * i + k],
                                   sems[1].at[4 * i + k], sib))
        if small is not None:
            cps.append(_remote(cin[n].at[pl.ds((1 - c) * hs, hs)], cout[n], sems[0].at[4 * n], sems[1].at[4 * n], sib))
        for e in range(len(extra)):
            cps.append(_remote(cin[first_extra + e], cout[first_extra + e], sems[0].at[4 * n + 1 + e],
                               sems[1].at[4 * n + 1 + e], sib))
        return cps

    start = lambda cin, cout, sems: _start_all(copies(cin, cout, sems))
    finish = lambda cin, cout, sems: _wait_all(copies(cin, cout, sems))
    shapes = [jax.ShapeDtypeStruct((N_CHIPS,) + _HALF[nm], F32) for nm in names]
    ins = list(parts)
    if small is not None:
        shapes.append(jax.ShapeDtypeStruct((hs, 128), F32))
        ins.append(small)
    shapes += [jax.ShapeDtypeStruct(a.shape, a.dtype) for a in extra]
    ins += list(extra)
    n_sem = 4 * n + 1 + len(extra)
    return _Comm(ins, shapes, [_DMA((n_sem,)), _DMA((n_sem,))], start, finish)


def chip_exchange(sums, small=None):
    n = len(sums)

    def copies(cin, cout, sems, jj, c):
        out = []
        for k in _others(jj):
            for i in range(n):
                out.append(_remote(cin[i].at[k], cout[i].at[jj], sems[0].at[4 * i + k], sems[1].at[4 * i + jj], _core_of(k, c)))
            if small is not None:
                out.append(_remote(cin[n], cout[n].at[jj], sems[0].at[4 * n + k], sems[1].at[4 * n + jj], _core_of(k, c)))
        return out

    def start(cin, cout, sems):
        _, _, c, j = _place()

        def run(jj):
            if small is not None:
                pltpu.make_async_copy(cin[n], cout[n].at[jj], sems[2]).start()
            for cp in copies(cin, cout, sems, jj, c):
                cp.start()

        _as_chip(j, run)

    def finish(cin, cout, sems):
        _, _, c, j = _place()

        def run(jj):
            for k in _others(jj):
                for i in range(n):
                    _remote(cin[i].at[k], cout[i].at[k], sems[0].at[4 * i + k], sems[1].at[4 * i + k], _core_of(k, c)).wait_recv()
                if small is not None:
                    _remote(cin[n], cout[n].at[k], sems[0].at[4 * n + k], sems[1].at[4 * n + k], _core_of(k, c)).wait_recv()
            for cp in copies(cin, cout, sems, jj, c):
                cp.wait_send()
            if small is not None:
                pltpu.make_async_copy(cin[n], cout[n].at[jj], sems[2]).wait()

        _as_chip(j, run)

    shapes = [jax.ShapeDtypeStruct(a.shape, a.dtype) for a in sums]
    ins = list(sums)
    if small is not None:
        shapes.append(jax.ShapeDtypeStruct((N_CHIPS,) + small.shape, small.dtype))
        ins.append(small)
    return _Comm(ins, shapes, [_DMA((4 * n + 4,)), _DMA((4 * n + 4,)), _DMA(())], start, finish)


def pair_send_totals(arrs):
    n = len(arrs)

    def copies(cin, cout, sems):
        x, y, c, _ = _place()
        return [_remote(cin[i], cout[i], sems[0].at[i], sems[1].at[i], (x, y, 1 - c)) for i in range(n)]

    start = lambda cin, cout, sems: _start_all(copies(cin, cout, sems))
    finish = lambda cin, cout, sems: _wait_all(copies(cin, cout, sems))
    return _Comm(arrs, [jax.ShapeDtypeStruct(a.shape, a.dtype) for a in arrs], [_DMA((n,)), _DMA((n,))], start, finish)


def _pin(a):
    return pltpu.with_memory_space_constraint(a, pltpu.HBM)


def _plain_params(n_axes):
    return pltpu.CompilerParams(dimension_semantics=("arbitrary",) * n_axes, vmem_limit_bytes=V7X_VMEM_BYTES // 2)


def add_own_halves(names, parts, landed, core, name):
    n = len(names)

    def body(core_ref, *refs):
        for i in range(n):
            refs[2 * n + i][0] = (refs[i][...] + refs[n + i][0]).astype(MXU_DTYPE)

    in_specs, out_specs = [], []
    for nm in names:
        hr, hc = _HALF[nm]
        if nm in ("pa", "pb"):
            in_specs.append(pl.BlockSpec((hr // 2, hc), lambda k, s, cr: (cr[0] * 2 + s, k)))
        else:
            in_specs.append(pl.BlockSpec((hr // 2, hc), lambda k, s, cr: (k * 4 + cr[0] * 2 + s, 0)))
    for nm in names:
        hr, hc = _HALF[nm]
        in_specs.append(pl.BlockSpec((1, hr // 2, hc), lambda k, s, cr: (k, s, 0)))
        out_specs.append(pl.BlockSpec((1, hr // 2, hc), lambda k, s, cr: (k, s, 0)))
    return pl.pallas_call(
        body, name=name,
        grid_spec=pltpu.PrefetchScalarGridSpec(num_scalar_prefetch=1, grid=(N_CHIPS, 2), in_specs=in_specs, out_specs=out_specs),
        out_shape=[pltpu.HBM((N_CHIPS,) + _HALF[nm], MXU_DTYPE) for nm in names],
        compiler_params=_plain_params(2),
    )(core, *[_pin(a) for a in parts], *[_pin(a) for a in landed])


def add_small_half(small, landed, core):
    hs = SMALL_ROWS // 2

    def body(core_ref, a_ref, b_ref, o_ref):
        o_ref[...] = a_ref[...] + b_ref[...]

    return pl.pallas_call(
        body, name="add_small_half",
        grid_spec=pltpu.PrefetchScalarGridSpec(
            num_scalar_prefetch=1, grid=(1,),
            in_specs=[pl.BlockSpec((hs, 128), lambda s, cr: (cr[0], 0)), pl.BlockSpec((hs, 128), lambda s, cr: (0, 0))],
            out_specs=pl.BlockSpec((hs, 128), lambda s, cr: (0, 0))),
        out_shape=jax.ShapeDtypeStruct((hs, 128), F32),
    )(core, small, landed)


def sum_chips(sums, landed, chips, name, small_landed=None):
    n = len(sums)

    def body(chips_ref, *refs):
        ins, outs = refs[:4 * n + (small_landed is not None)], refs[4 * n + (small_landed is not None):]
        for i in range(n):
            tot = ins[4 * i][0].astype(F32)
            for r in range(1, N_CHIPS):
                tot = tot + ins[4 * i + r][0].astype(F32)
            outs[i][...] = tot
        if small_landed is not None:
            @pl.when(pl.program_id(0) == 0)
            def _():
                sm = ins[4 * n]
                outs[n][...] = ((sm[0] + sm[1]) + sm[2]) + sm[3]

    in_specs, out_specs, operands, shapes = [], [], [], []
    for a, l in zip(sums, landed):
        _, hr, hc = a.shape
        for r in range(N_CHIPS):
            in_specs.append(pl.BlockSpec((1, hr // 2, hc), functools.partial(lambda s, ch, r: (ch[r], s, 0), r=r)))
            operands.append(a if r == 0 else l)
        out_specs.append(pl.BlockSpec((hr // 2, hc), lambda s, ch: (s, 0)))
        shapes.append(jax.ShapeDtypeStruct((hr, hc), F32))
    if small_landed is not None:
        in_specs.append(pl.BlockSpec(small_landed.shape, lambda s, ch: (0, 0, 0)))
        out_specs.append(pl.BlockSpec(small_landed.shape[1:], lambda s, ch: (0, 0)))
        operands.append(small_landed)
        shapes.append(jax.ShapeDtypeStruct(small_landed.shape[1:], F32))
    return pl.pallas_call(
        body, name=name,
        grid_spec=pltpu.PrefetchScalarGridSpec(num_scalar_prefetch=1, grid=(2,), in_specs=in_specs, out_specs=out_specs),
        out_shape=[pltpu.HBM(s.shape, s.dtype) for s in shapes], compiler_params=_plain_params(1),
    )(chips, *[_pin(a) for a in operands])


N_STEPS = 8


def _adam_update(w, g, m, v):
    m = ADAM_B1 * m + (1.0 - ADAM_B1) * g
    v = ADAM_B2 * v + (1.0 - ADAM_B2) * (g * g)
    m_hat = m * (1.0 / (1.0 - ADAM_B1 ** ADAM_STEP))
    v_hat = v * (1.0 / (1.0 - ADAM_B2 ** ADAM_STEP))
    return -ADAM_LR * (m_hat / (jnp.sqrt(v_hat) + ADAM_EPS) + ADAM_WD * w), m, v


def adamw_big(ws, g_halves, ms, vs, core):
    n = len(ws)
    per_layer = N_STEPS // DEPTH
    per_half = per_layer // 2

    def body(core_ref, *refs):
        s = pl.program_id(0)
        first_layer = s < per_layer
        mine = ((s % per_layer) // per_half) == core_ref[0]
        for i in range(n):
            w, m, v = (refs[k * n + i][...] for k in range(3))
            g00, g01, g10, g11 = (refs[3 * n + 4 * i + k][...] for k in range(4))
            g = jnp.where(first_layer, jnp.where(mine, g00, g01), jnp.where(mine, g10, g11))[None]
            delta, m, v = _adam_update(w, g, m, v)
            for k, val in enumerate((g, delta, m, v)):
                refs[7 * n + k * n + i][...] = val

    whole = lambda a: pl.BlockSpec((1, a.shape[1] // per_layer, a.shape[2]),
                                   lambda s, cr: (s // per_layer, s % per_layer, 0))

    def half(a, layer, own):
        def index(s, cr):
            first = layer * per_layer + per_half * jnp.where(own, cr[0], 1 - cr[0])
            return (jnp.clip(s - first, 0, per_half - 1), 0)
        return pl.BlockSpec((a.shape[1] // per_layer, a.shape[2]), index)

    g_specs, g_ops = [], []
    for a, halves in zip(ws, g_halves):
        g_specs += [half(a, 0, True), half(a, 0, False), half(a, 1, True), half(a, 1, False)]
        g_ops += [_pin(a) for a in halves]
    outs = pl.pallas_call(
        body, name="adamw_big",
        grid_spec=pltpu.PrefetchScalarGridSpec(
            num_scalar_prefetch=1, grid=(N_STEPS,), in_specs=[whole(a) for a in ws] * 3 + g_specs,
            out_specs=[whole(a) for a in ws] * 4),
        out_shape=[pltpu.HBM(a.shape, F32) for a in ws] * 4,
        compiler_params=_layer_params(),
    )(core, *[_pin(a) for a in (*ws, *ms, *vs)], *g_ops)
    return outs[:n], outs[n:2 * n], outs[2 * n:3 * n], outs[3 * n:]


def adamw_small(ws, gs, ms, vs):
    n = len(ws)

    def body(*refs):
        for i in range(n):
            outs = _adam_update(*(refs[k * n + i][...] for k in range(4)))
            for k, val in enumerate(outs):
                refs[4 * n + k * n + i][...] = val

    specs = [_const_block(a.shape) for a in ws]
    outs = pl.pallas_call(
        body, name="adamw_small", grid=(1,), in_specs=specs * 4, out_specs=specs * 3,
        out_shape=[jax.ShapeDtypeStruct(a.shape, F32) for a in ws] * 3,
        compiler_params=_plain_params(1),
    )(*ws, *gs, *ms, *vs)
    return outs[:n], outs[n:2 * n], outs[2 * n:]


_SMALL = (("w_s", (DEPTH, SGU_G, BLK, BLK)), ("ln_in_g", (D_MODEL,)), ("ln_in_b", (D_MODEL,)), ("b_in", (DEPTH, N_COLS)),
          ("sinks", (DEPTH, 8)), ("vn_g", (DEPTH, SGU_W)), ("vn_b", (DEPTH, SGU_W)),
          ("b_s", (DEPTH, SGU_G, BLK)), ("b_out", (DEPTH, D_MODEL)), ("ln_g", (DEPTH, D_MODEL)), ("ln_b", (DEPTH, D_MODEL)))
_BIG = ("w_in", "p_a", "p_b", "w_out")


def _size(shape):
    n = 1
    for d in shape:
        n *= d
    return n


def _pack_small(vals, last_row=None):
    rows = []
    for name, shape in _SMALL:
        v = vals[name]
        if _size(shape) % 128:
            v = jnp.pad(v.reshape((1, -1)), ((0, 0), (0, (-_size(shape)) % 128)))
        rows.append(v.reshape((-1, 128)))
    used = sum(r.shape[0] for r in rows)
    tail = jnp.zeros((SMALL_ROWS - used, 128), F32)
    if last_row is not None:
        tail = tail.at[-1, 0].set(last_row)
    return jnp.concatenate([rows[0], jnp.concatenate(rows[1:] + [tail], axis=0)], axis=0)


def _unpack_small(packed):
    out, pos = {}, 0
    for name, shape in _SMALL:
        n = -(-_size(shape) // 128)
        rows = packed[pos:pos + n]
        out[name] = (rows.reshape((-1,))[:_size(shape)] if _size(shape) % 128 else rows).reshape(shape)
        pos += n
    return out


def kernel(x, ln_in_g, ln_in_b, w_in, b_in, sinks, vn_g, vn_b, w_s, b_s, p_a, p_b, w_out, b_out, ln_g, ln_b, loss_target, m_ln_in_g, m_ln_in_b, m_w_in, m_b_in, m_sinks, m_vn_g, m_vn_b, m_w_s, m_b_s, m_p_a, m_p_b, m_w_out, m_b_out, m_ln_g, m_ln_b, v_ln_in_g, v_ln_in_b, v_w_in, v_b_in, v_sinks, v_vn_g, v_vn_b, v_w_s, v_b_s, v_p_a, v_p_b, v_w_out, v_b_out, v_ln_g, v_ln_b):
    weights = dict(ln_in_g=ln_in_g, ln_in_b=ln_in_b, w_in=w_in, b_in=b_in, sinks=sinks, vn_g=vn_g, vn_b=vn_b, w_s=w_s,
                   b_s=b_s, p_a=p_a, p_b=p_b, w_out=w_out, b_out=b_out, ln_g=ln_g, ln_b=ln_b)
    mom1 = dict(ln_in_g=m_ln_in_g, ln_in_b=m_ln_in_b, w_in=m_w_in, b_in=m_b_in, sinks=m_sinks, vn_g=m_vn_g, vn_b=m_vn_b,
                w_s=m_w_s, b_s=m_b_s, p_a=m_p_a, p_b=m_p_b, w_out=m_w_out, b_out=m_b_out, ln_g=m_ln_g, ln_b=m_ln_b)
    mom2 = dict(ln_in_g=v_ln_in_g, ln_in_b=v_ln_in_b, w_in=v_w_in, b_in=v_b_in, sinks=v_sinks, vn_g=v_vn_g, vn_b=v_vn_b,
                w_s=v_w_s, b_s=v_b_s, p_a=v_p_a, p_b=v_p_b, w_out=v_w_out, b_out=v_b_out, ln_g=v_ln_g, ln_b=v_ln_b)
    n_seq, S, _ = x.shape
    T = n_seq * S
    c = lax.axis_index("c")
    j = 2 * lax.axis_index("x") + lax.axis_index("y")
    core = c.astype(jnp.int32).reshape((1,))
    chips = jnp.stack([j] + [r + (r >= j) for r in range(N_CHIPS - 1)]).astype(jnp.int32)
    names = list(_PIECES)
    xt, tt = x.reshape((T, D_MODEL)), loss_target.reshape((T, D_MODEL))

    tview = lambda d: dict(d, w_in=jnp.swapaxes(d["w_in"], 1, 2))
    weights_t, mom1_t, mom2_t = tview(weights), tview(mom1), tview(mom2)

    def own_shards(l):
        return [weights_t[n][l].astype(MXU_DTYPE) for n in _BIG]

    def layer_weights(l, gathered):
        d = dict(zip(_PIECES, gathered))
        d.update(paT=d["pa"].T, pbT=d["pb"].T, woutT=d["wout"].T)
        d.update(bin=b_in[l][None], sinks=sinks[l], vng=vn_g[l][None], vnb=vn_b[l][None], ws=w_s[l],
                 bs=jnp.repeat(b_s[l].T, 128, axis=1), bout=b_out[l][None], lng=ln_g[l][None], lnb=ln_b[l][None])
        return d

    own0, own1 = own_shards(0), own_shards(1)
    x0, g0 = input_norm_forward(xt, ln_in_g[None], ln_in_b[None], "ln_in_fwd", comm=gather_weights(names, own0))
    lw0 = layer_weights(0, g0)
    n_proj = T // ROWS_PROJ
    h0, g1 = project(x0, lw0["winT"], lw0["bin"], "layer0_proj", comm=gather_weights(names, own1, turn_step=(3 * n_proj) // 4))
    lw1 = layer_weights(1, g1)
    x1, h1 = layer_forward_and_projection(x0, h0, lw0, lw1["winT"], lw1["bin"], S, "layer0_fwd_layer1_proj")

    dres1, dh1, acc1, _ = layer_backward(x1, h1, tt, lw1, S, True, "layer1_bwd")
    dx1, dwinT1, _ = input_grads(dres1, dh1, x1, lw1["winT"], "layer1_dx_dwin")
    parts1 = [dwinT1, acc1["dpa"], acc1["dpb"], acc1["dwout"]]
    dres0, dh0, acc0, landed1 = layer_backward(x0, h0, dx1, lw0, S, False, "layer0_bwd", comm=pair_send_halves(names, parts1))
    sums1 = add_own_halves(names, parts1, landed1, core, "add_own_halves1")
    dx0, dwinT0, from_chips1 = input_grads(dres0, dh0, x0, lw0["winT"], "layer0_dx_dwin", comm=chip_exchange(sums1))
    totals1 = sum_chips(sums1, from_chips1, chips, "sum_chips1")
    grad_x, d_ln_in_g, d_ln_in_b, _ = input_norm_backward(xt, dx0, ln_in_g, "ln_in_bwd")

    gl = [acc0, acc1]
    per_layer = lambda key, pick: jnp.stack([pick(gl[l][key]) for l in range(DEPTH)])
    row0 = lambda key: per_layer(key, lambda a: a[0])
    small_part = _pack_small(dict(
        ln_in_g=d_ln_in_g[0], ln_in_b=d_ln_in_b[0], b_in=row0("dbin"), sinks=per_layer("dsink", lambda a: a[:, 0]),
        vn_g=row0("dvng"), vn_b=row0("dvnb"), w_s=per_layer("dws", lambda a: a),
        b_s=per_layer("dbs", lambda a: a[:, ::128].T), b_out=row0("dbout"), ln_g=row0("dlng"), ln_b=row0("dlnb")),
        last_row=acc1["loss"][0, 0])
    parts0 = [dwinT0, acc0["dpa"], acc0["dpb"], acc0["dwout"]]
    first = _run_comm(pair_send_halves(names, parts0, small_part, extra=totals1), "pair_send_halves0")
    landed0, small_landed, sib_totals1 = first[:4], first[4], first[5:]
    sums0 = add_own_halves(names, parts0, landed0, core, "add_own_halves0")
    small_sum = add_small_half(small_part, small_landed, core)
    *from_chips0, small_from_chips = _run_comm(chip_exchange(sums0, small_sum), "chip_exchange0")
    *totals0, small_total = sum_chips(sums0, from_chips0, chips, "sum_chips0", small_landed=small_from_chips)
    *sib_totals0, sib_small = _run_comm(pair_send_totals(totals0 + [small_total]), "pair_send_totals0")

    def both_halves(mine, theirs):
        return jnp.where(c == 0, jnp.concatenate([mine, theirs], axis=0), jnp.concatenate([theirs, mine], axis=0))

    g_halves = [(totals0[pi], sib_totals0[pi], totals1[pi], sib_totals1[pi]) for pi in range(len(_BIG))]
    g_small = both_halves(small_total, sib_small)
    loss = g_small[-1, 0]

    big = adamw_big([weights_t[n] for n in _BIG], g_halves, [mom1_t[n] for n in _BIG], [mom2_t[n] for n in _BIG], core)
    small_names = [n for n, _ in _SMALL]
    as2d = lambda a: a.reshape((1, -1)) if a.ndim == 1 else a
    g_named = _unpack_small(g_small)
    small = adamw_small(*[[as2d(d[n]) for n in small_names] for d in (weights, g_named, mom1, mom2)])

    def named(big_outs, small_outs):
        d = dict(zip(_BIG, big_outs))
        d["w_in"] = jnp.swapaxes(d["w_in"], 1, 2)
        d.update({n: o.reshape(weights[n].shape) for n, o in zip(small_names, small_outs)})
        return d

    order = ("ln_in_g", "ln_in_b", "w_in", "b_in", "sinks", "vn_g", "vn_b", "w_s", "b_s", "p_a", "p_b", "w_out", "b_out",
             "ln_g", "ln_b")
    res = [loss, grad_x.reshape(x.shape)]
    groups = [named(big[0], [g_named[n] for n in small_names])] + [named(big[k + 1], small[k]) for k in range(3)]
    for group in groups:
        res.extend(group[n] for n in order)
    return tuple(res)
```

```python
import functools

import jax
import jax.numpy as jnp
from jax import lax
from jax.experimental import pallas as pl
from jax.experimental.pallas import tpu as pltpu

F32 = jnp.float32
MXU_DTYPE = jnp.bfloat16

D_MODEL = 1024
DEPTH = 2
HEAD_DIM = 64
ATTN_W = 512
KV_W = 128
BLK = 128
SGU_W = 512
SGU_G = 4
N_COLS = 4864
C_Q, C_K, C_V, C_GA, C_UB, C_VB, C_GB, C_RA, C_RB = 0, 512, 640, 768, 1280, 1792, 2304, 2816, 3840
ALPHA = (2.0 * DEPTH) ** 0.25
LN_EPS = 1e-5
SCALE = HEAD_DIM ** -0.5
NEG = float(jnp.finfo(jnp.float32).min)
GELU_C = 0.7978845608028654
GELU_A = 0.044715

ADAM_LR, ADAM_B1, ADAM_B2, ADAM_EPS, ADAM_WD, ADAM_STEP = 0.001, 0.9, 0.999, 1e-08, 0.01, 10

V7X_VMEM_BYTES = 64 * 1024 * 1024
V7X_VMEM_RESERVE = 3 * 1024 * 1024
ROWS = 256
ROWS_BWD = 256
ROWS_PROJ = 512

MESH = pl.DeviceIdType.MESH
N_CHIPS = 4


def _mm(a, b):
    return jnp.dot(a.astype(MXU_DTYPE), b.astype(MXU_DTYPE), preferred_element_type=F32)


def _mm_nt(a, b):
    return lax.dot_general(a.astype(MXU_DTYPE), b.astype(MXU_DTYPE), (((1,), (1,)), ((), ())),
                           preferred_element_type=F32)


def _mm_tn(a, b):
    return lax.dot_general(a.astype(MXU_DTYPE), b.astype(MXU_DTYPE), (((0,), (0,)), ((), ())),
                           preferred_element_type=F32)


def _sigmoid(x):
    return 0.5 * jnp.tanh(0.5 * x) + 0.5


def _gelu_parts(x):
    x2 = x * x
    u = 0.5 * jnp.tanh(x * (GELU_C + (GELU_C * GELU_A) * x2)) + 0.5
    dg = u * (1.0 + x * (1.0 - u) * (2.0 * GELU_C + (6.0 * GELU_C * GELU_A) * x2))
    return x * u, dg


def _ln_fwd(z, g, b):
    mu = jnp.mean(z, axis=-1, keepdims=True)
    zc = z - mu
    var = jnp.mean(zc * zc, axis=-1, keepdims=True)
    rstd = lax.rsqrt(var + LN_EPS)
    zhat = zc * rstd
    return zhat * g + b, zhat, rstd


def _ln_bwd(dy, zhat, rstd, g):
    dzh = dy * g
    m1 = jnp.mean(dzh, axis=-1, keepdims=True)
    m2 = jnp.mean(dzh * zhat, axis=-1, keepdims=True)
    return rstd * (dzh - m1 - zhat * m2)


def _colsum(v):
    return jnp.sum(v, axis=0, keepdims=True)


def _lane_lo(rows):
    return lax.broadcasted_iota(jnp.int32, (rows, 128), 1) < HEAD_DIM


def _dup_head(x2, hk, lo):
    xr = pltpu.roll(x2, HEAD_DIM, 1)
    return jnp.where(lo, x2, xr) if hk == 0 else jnp.where(lo, xr, x2)


def _fold_head(y0, y1, lo):
    f0 = y0 + pltpu.roll(y0, HEAD_DIM, 1)
    f1 = y1 + pltpu.roll(y1, HEAD_DIM, 1)
    return jnp.where(lo, f0, f1)


def _stack_heads(t0, t1, lo):
    z = jnp.zeros_like(t0)
    return jnp.concatenate([jnp.where(lo, t0, z), jnp.where(lo, z, t0),
                            jnp.where(lo, t1, z), jnp.where(lo, z, t1)], axis=0)


def _unstack_heads(o, lo):
    return (jnp.where(lo, o[0:128], o[128:256]), jnp.where(lo, o[256:384], o[384:512]))


def _band_bias(start):
    row = lax.broadcasted_iota(jnp.int32, (4 * BLK, 2 * BLK), 0) & (BLK - 1)
    kpos = lax.broadcasted_iota(jnp.int32, (4 * BLK, 2 * BLK), 1)
    valid = (kpos > row) & (kpos <= row + BLK)
    if start is not None:
        valid = valid & (jnp.logical_not(start) | (kpos >= BLK))
    return jnp.where(valid, 0.0, NEG)


def _attn_probs(s, sink4, bias):
    s = s * SCALE + bias
    r1 = lax.broadcasted_iota(jnp.int32, (4 * BLK, 1), 0)
    sk = jnp.where(r1 < BLK, sink4[0], jnp.where(r1 < 2 * BLK, sink4[1], jnp.where(r1 < 3 * BLK, sink4[2], sink4[3])))
    m = jnp.maximum(jnp.max(s, axis=-1, keepdims=True), sk)
    p = jnp.exp(s - m)
    es = jnp.exp(sk - m)
    inv = 1.0 / (jnp.sum(p, axis=-1, keepdims=True) + es)
    return p * inv, es * inv


_PROJ_CHUNK = 512


def _kv_blocks(kvh, h_s, nb):
    ks = [kvh[:, 0:KV_W]] + [h_s[n * BLK:(n + 1) * BLK, C_K:C_K + KV_W] for n in range(nb)]
    vs = [kvh[:, KV_W:2 * KV_W]] + [h_s[n * BLK:(n + 1) * BLK, C_V:C_V + KV_W] for n in range(nb)]
    return ks, vs


def _attn_operands(h_s, ks, vs, n, hk, lo, lo2):
    rows = slice(n * BLK, (n + 1) * BLK)
    kd = _dup_head(jnp.concatenate([ks[n], ks[n + 1]], axis=0), hk, lo2)
    vd = _dup_head(jnp.concatenate([vs[n], vs[n + 1]], axis=0), hk, lo2)
    c0 = C_Q + 2 * hk * 128
    qs = _stack_heads(h_s[rows, c0:c0 + 128], h_s[rows, c0 + 128:c0 + 256], lo)
    return qs, kd, vd


def _sgu_weights(w):
    tri = (lax.broadcasted_iota(jnp.int32, (BLK, BLK), 0) >= lax.broadcasted_iota(jnp.int32, (BLK, BLK), 1))
    return tri, [jnp.where(tri, w["ws"][g], 0.0) for g in range(SGU_G)]


def _layer_forward(x, kvh, start, w, h_s, attn_s, mix_s, keep, tick=lambda: None):
    R = x.shape[0]
    nb = R // BLK
    lo = _lane_lo(BLK)
    lo2 = _lane_lo(2 * BLK)
    ks, vs = _kv_blocks(kvh, h_s, nb)
    sinks = [w["sinks"][j] for j in range(8)]
    bias_first = _band_bias(start)
    bias_rest = _band_bias(None) if nb > 1 else None
    pairs = [(n, hk) for n in range(nb) for hk in range(2)]
    ops = [_attn_operands(h_s, ks, vs, n, hk, lo, lo2) for n, hk in pairs]
    scores = [_mm_nt(qs, kd) for qs, kd, _ in ops]
    probs_l = [_attn_probs(s, sinks[4 * hk:4 * hk + 4], bias_first if n == 0 else bias_rest)
               for s, (n, hk) in zip(scores, pairs)]
    outs = [_mm(p, vd) for (p, _), (_, _, vd) in zip(probs_l, ops)]
    for o, (n, hk) in zip(outs, pairs):
        rows = slice(n * BLK, (n + 1) * BLK)
        t0, t1 = _unstack_heads(o, lo)
        attn_s[rows, 2 * hk * 128:(2 * hk + 1) * 128] = t0
        attn_s[rows, (2 * hk + 1) * 128:(2 * hk + 2) * 128] = t1
        tick()
    ga = h_s[:, C_GA:C_GA + ATTN_W]
    ya = attn_s[...] * (ga * _sigmoid(ga))
    tick()
    gu, _ = _gelu_parts(h_s[:, C_UB:C_UB + SGU_W])
    tick()
    gv, _ = _gelu_parts(h_s[:, C_VB:C_VB + SGU_W])
    vn, _, _ = _ln_fwd(gv, w["vng"][...], w["vnb"][...])
    tick()
    _, wms = _sgu_weights(w)
    tiles = [(slice(n * BLK, (n + 1) * BLK), g, slice(g * 128, (g + 1) * 128)) for n in range(nb) for g in range(SGU_G)]
    mixes = [_mm(wms[g], vn[rows, cols]) for rows, g, cols in tiles]
    for m, (rows, g, cols) in zip(mixes, tiles):
        mix_s[rows, cols] = m + w["bs"][:, cols]
    gb = h_s[:, C_GB:C_GB + SGU_W]
    yb = gu * mix_s[...] * (gb * _sigmoid(gb))
    tick()
    a = _mm(ya, w["pa"][...])
    b = _mm(yb, w["pb"][...])
    tick()
    merged = _sigmoid(h_s[:, C_RA:C_RA + D_MODEL]) * a + _sigmoid(h_s[:, C_RB:C_RB + D_MODEL]) * b
    tick()
    out = _mm(merged, w["wout"][...]) + w["bout"][...]
    y, zhat, rstd = _ln_fwd(ALPHA * x + out, w["lng"][...], w["lnb"][...])
    if not keep:
        return y, None
    return y, dict(ks=ks, vs=vs, probs=probs_l, ya=ya, yb=yb, a=a, b=b, merged=merged, zhat=zhat, rstd=rstd)


def _dsilu(g, sg):
    return sg * (1.0 + g * (1.0 - sg))


_W_NAMES = ("sinks", "vng", "vnb", "ws", "bs", "pa", "pb", "wout", "bout", "lng", "lnb")
_WT_NAMES = ("paT", "pbT", "woutT")
_ACC_NAMES = ("dwout", "dpa", "dpb", "dbin", "dbout", "dlng", "dlnb", "dvng", "dvnb", "dws", "dbs", "dsink", "loss")


def _make_fwd_proj_body(R, S, nT):
    def body(x_ref, h_ref, kvh_ref, wn_ref, bn_ref, *rest):
        w = dict(zip(_W_NAMES, rest[:len(_W_NAMES)]))
        y_ref, hn_ref, attn_s, mix_s, yprev_s = rest[len(_W_NAMES):]

        @pl.when(pl.program_id(0) == 0)
        def _():
            yprev_s[...] = jnp.zeros(yprev_s.shape, yprev_s.dtype)

        t = jnp.minimum(pl.program_id(0), nT - 1)
        start = (t % (S // R)) == 0
        chunks = [(c0, min(c0 + _PROJ_CHUNK, N_COLS)) for c0 in range(0, N_COLS, _PROJ_CHUNK)]

        def tick():
            if chunks:
                c0, c1 = chunks.pop(0)
                hn_ref[:, c0:c1] = _mm_nt(yprev_s[...], wn_ref[c0:c1, :]) + bn_ref[:, c0:c1]

        y, _ = _layer_forward(x_ref[...], kvh_ref[...], start, w, h_ref, attn_s, mix_s, keep=False, tick=tick)
        while chunks:
            tick()
        y_ref[...] = y
        yprev_s[...] = y.astype(yprev_s.dtype)
    return body


def _make_bwd_body(R, S, nT, is_last):
    nb = R // BLK
    names = _W_NAMES + _WT_NAMES
    nw = len(names)

    def body(x_ref, h_s, kvh_ref, aux_ref, *rest):
        w = dict(zip(names, rest[:nw]))
        dres_ref, dh_ref = rest[nw:nw + 2]
        acc = dict(zip(_ACC_NAMES, rest[nw + 2:nw + 2 + len(_ACC_NAMES)]))
        attn_s, mix_s, ckv_s = rest[nw + 2 + len(_ACC_NAMES):]
        i = pl.program_id(0)
        start = ((nT - 1 - i) % (S // R)) == 0
        lo = _lane_lo(BLK)
        lo2 = _lane_lo(2 * BLK)

        @pl.when(i == 0)
        def _():
            for name in _ACC_NAMES:
                acc[name][...] = jnp.zeros(acc[name].shape, F32)
            ckv_s[...] = jnp.zeros(ckv_s.shape, F32)

        x = x_ref[...]
        y, sv = _layer_forward(x, kvh_ref[...], start, w, h_s, attn_s, mix_s, keep=True)
        if is_last:
            diff = y - aux_ref[...]
            part = 0.5 * jnp.sum(jnp.mean(diff * diff, axis=-1, keepdims=True), axis=0, keepdims=True)
            acc["loss"][...] += jnp.broadcast_to(part, acc["loss"].shape)
            dy = diff * (1.0 / D_MODEL)
        else:
            dy = aux_ref[...]

        def put(c0, val, rows=slice(None)):
            width = val.shape[1]
            dh_ref[rows, c0:c0 + width] = val.astype(dh_ref.dtype)
            acc["dbin"][:, c0:c0 + width] += _colsum(val)

        acc["dlng"][...] += _colsum(dy * sv["zhat"])
        acc["dlnb"][...] += _colsum(dy)
        dz = _ln_bwd(dy, sv["zhat"], sv["rstd"], w["lng"][...])
        dres_ref[...] = ALPHA * dz
        acc["dbout"][...] += _colsum(dz)
        acc["dwout"][...] += _mm_tn(sv["merged"], dz)
        dmerged = _mm(dz, w["woutT"][...])
        sa = _sigmoid(h_s[:, C_RA:C_RA + D_MODEL])
        da = dmerged * sa
        put(C_RA, da * sv["a"] * (1.0 - sa))
        sb = _sigmoid(h_s[:, C_RB:C_RB + D_MODEL])
        db = dmerged * sb
        put(C_RB, db * sv["b"] * (1.0 - sb))
        acc["dpa"][...] += _mm_tn(sv["ya"], da)
        acc["dpb"][...] += _mm_tn(sv["yb"], db)
        dya = _mm(da, w["paT"][...])
        dyb = _mm(db, w["pbT"][...])
        ga = h_s[:, C_GA:C_GA + ATTN_W]
        sga = _sigmoid(ga)
        put(C_GA, dya * attn_s[...] * _dsilu(ga, sga))
        attn_s[...] = dya * (ga * sga)
        ks, vs = sv["ks"], sv["vs"]
        pairs = [(n, hk) for n in range(nb) for hk in range(2)]
        ops = [_attn_operands(h_s, ks, vs, n, hk, lo, lo2) for n, hk in pairs]
        doss = [_stack_heads(attn_s[n * BLK:(n + 1) * BLK, 2 * hk * 128:(2 * hk + 1) * 128],
                             attn_s[n * BLK:(n + 1) * BLK, (2 * hk + 1) * 128:(2 * hk + 2) * 128], lo) for n, hk in pairs]
        dps = [_mm_nt(dos, vd) for dos, (_, _, vd) in zip(doss, ops)]
        gb = h_s[:, C_GB:C_GB + SGU_W]
        sgb = _sigmoid(gb)
        gu, dgu = _gelu_parts(h_s[:, C_UB:C_UB + SGU_W])
        mixed = mix_s[...]
        put(C_GB, dyb * (gu * mixed) * _dsilu(gb, sgb))
        dsgu = dyb * (gb * sgb)
        put(C_UB, dsgu * mixed * dgu)
        dmixed = dsgu * gu
        dsss = []
        for dp, (probs, ps), (n, hk) in zip(dps, sv["probs"], pairs):
            delta = jnp.sum(probs * dp, axis=-1, keepdims=True)
            dsk = -(ps * delta)
            for g in range(4):
                j = 4 * hk + g
                acc["dsink"][j:j + 1, :] += jnp.broadcast_to(_colsum(dsk[g * BLK:(g + 1) * BLK]), (1, 128))
            dsss.append(probs * (dp - delta) * SCALE)
        dqs = [_mm(dss, kd) for dss, (_, kd, _) in zip(dsss, ops)]
        ydks = [_mm_tn(dss, qs) for dss, (qs, _, _) in zip(dsss, ops)]
        ydvs = [_mm_tn(probs, dos) for (probs, _), dos in zip(sv["probs"], doss)]
        gv, dgv = _gelu_parts(h_s[:, C_VB:C_VB + SGU_W])
        vn, vhat, vrstd = _ln_fwd(gv, w["vng"][...], w["vnb"][...])
        tri, wms = _sgu_weights(w)
        tiles = [(slice(n * BLK, (n + 1) * BLK), g, slice(g * 128, (g + 1) * 128)) for n in range(nb) for g in range(SGU_G)]
        dw_parts = [_mm_nt(dmixed[rows, cols], vn[rows, cols]) for rows, g, cols in tiles]
        dvn_parts = [_mm_tn(wms[g], dmixed[rows, cols]) for rows, g, cols in tiles]
        for g in range(SGU_G):
            tot = dw_parts[g]
            for n in range(1, nb):
                tot = tot + dw_parts[n * SGU_G + g]
            acc["dws"][g] += jnp.where(tri, tot, 0.0)
        for part, (rows, g, cols) in zip(dvn_parts, tiles):
            mix_s[rows, cols] = part
        dbs_part = dmixed[0:BLK]
        for n in range(1, nb):
            dbs_part = dbs_part + dmixed[n * BLK:(n + 1) * BLK]
        acc["dbs"][...] += dbs_part
        for dq, (n, hk) in zip(dqs, pairs):
            q0, q1 = _unstack_heads(dq, lo)
            put(C_Q + 2 * hk * 128, q0, slice(n * BLK, (n + 1) * BLK))
            put(C_Q + (2 * hk + 1) * 128, q1, slice(n * BLK, (n + 1) * BLK))
        dks = [jnp.zeros((BLK, KV_W), F32) for _ in range(nb + 1)]
        dvs = [jnp.zeros((BLK, KV_W), F32) for _ in range(nb + 1)]
        for n in range(nb):
            dk2 = _fold_head(ydks[2 * n], ydks[2 * n + 1], lo2)
            dv2 = _fold_head(ydvs[2 * n], ydvs[2 * n + 1], lo2)
            dks[n] = dks[n] + dk2[0:BLK]
            dks[n + 1] = dks[n + 1] + dk2[BLK:2 * BLK]
            dvs[n] = dvs[n] + dv2[0:BLK]
            dvs[n + 1] = dvs[n + 1] + dv2[BLK:2 * BLK]
        dks[nb] = dks[nb] + ckv_s[:, 0:KV_W]
        dvs[nb] = dvs[nb] + ckv_s[:, KV_W:2 * KV_W]
        ckv_s[:, 0:KV_W] = dks[0]
        ckv_s[:, KV_W:2 * KV_W] = dvs[0]
        put(C_K, jnp.concatenate(dks[1:], axis=0))
        put(C_V, jnp.concatenate(dvs[1:], axis=0))
        dvn = mix_s[...]
        acc["dvng"][...] += _colsum(dvn * vhat)
        acc["dvnb"][...] += _colsum(dvn)
        put(C_VB, _ln_bwd(dvn, vhat, vrstd, w["vng"][...]) * dgv)

        @pl.when(i == nT - 1)
        def _():
            for g in range(SGU_G):
                cols = slice(g * 128, (g + 1) * 128)
                tot = jnp.sum(acc["dbs"][:, cols], axis=1, keepdims=True)
                acc["dbs"][:, cols] = jnp.broadcast_to(tot, (BLK, 128))

    return body


def _resident():
    return pl.BlockSpec(memory_space=pltpu.VMEM)


_BIG_RESIDENT = ("pa", "pb", "wout", "paT", "pbT", "woutT", "dwout", "dpa", "dpb")


def _const_block(shape):
    return pl.BlockSpec(tuple(shape), lambda *_: (0,) * len(shape))


def _operand_spec(name, shape):
    if name == "sinks":
        return pl.BlockSpec(memory_space=pltpu.SMEM)
    return _resident() if name in _BIG_RESIDENT else _const_block(shape)


def _weight_specs(names, wts):
    return [_operand_spec(n, wts[n].shape) for n in names]


def _nbytes(a):
    n = jnp.dtype(a.dtype).itemsize
    for d in a.shape:
        n *= d
    return n


def _layer_params(resident=()):
    scoped = V7X_VMEM_BYTES - V7X_VMEM_RESERVE - sum(_nbytes(a) for a in resident)
    return pltpu.CompilerParams(dimension_semantics=("arbitrary",), vmem_limit_bytes=scoped)


class _Comm:
    def __init__(self, ins, out_shapes, sems, start, finish, aliases=None, turn=None, turn_step=None):
        self.ins = [pltpu.with_memory_space_constraint(a, pltpu.HBM) for a in ins]
        self.out_shapes = [pltpu.HBM(s.shape, s.dtype) for s in out_shapes]
        self.sems = list(sems)
        self.start, self.finish, self.aliases = start, finish, dict(aliases or {})
        self.turn, self.turn_step = turn, turn_step


_ANY = pl.BlockSpec(memory_space=pltpu.HBM)
_DMA = pltpu.SemaphoreType.DMA


def _call(body, *, name, grid, in_specs, out_specs, out_shape, scratch_shapes, compiler_params, operands, comm=None):
    if comm is None:
        res = pl.pallas_call(body, name=name, grid=grid, in_specs=in_specs, out_specs=out_specs, out_shape=out_shape,
                             scratch_shapes=scratch_shapes, compiler_params=compiler_params)(*operands)
        return list(res), []
    n_in, n_out, n_scr = len(in_specs), len(out_specs), len(scratch_shapes)
    ci, co = len(comm.ins), len(comm.out_shapes)
    last = grid[0] - 1

    def hosted(*refs):
        ins, cin = refs[:n_in], refs[n_in:n_in + ci]
        p = n_in + ci
        outs, cout = refs[p:p + n_out], refs[p + n_out:p + n_out + co]
        p += n_out + co
        scr, sems = refs[p:p + n_scr], refs[p + n_scr:]
        pl.when(pl.program_id(0) == 0)(lambda: comm.start(cin, cout, sems))
        body(*ins, *outs, *scr)
        if comm.turn is not None:
            at = last if comm.turn_step is None else comm.turn_step
            pl.when(pl.program_id(0) == at)(lambda: comm.turn(cin, cout, sems))
        pl.when(pl.program_id(0) == last)(lambda: comm.finish(cin, cout, sems))

    res = pl.pallas_call(
        hosted, name=name, grid=grid, in_specs=list(in_specs) + [_ANY] * ci, out_specs=list(out_specs) + [_ANY] * co,
        out_shape=list(out_shape) + comm.out_shapes, scratch_shapes=list(scratch_shapes) + comm.sems,
        input_output_aliases={n_in + a: n_out + b for a, b in comm.aliases.items()},
        compiler_params=compiler_params)(*operands, *comm.ins)
    return list(res[:n_out]), list(res[n_out:])


def _run_comm(comm, name):
    ci, co = len(comm.ins), len(comm.out_shapes)

    def body(*refs):
        cin, cout, sems = refs[:ci], refs[ci:ci + co], refs[ci + co:]
        comm.start(cin, cout, sems)
        if comm.turn is not None:
            comm.turn(cin, cout, sems)
        comm.finish(cin, cout, sems)

    return list(pl.pallas_call(body, name=name, in_specs=[_ANY] * ci, out_specs=[_ANY] * co, out_shape=comm.out_shapes,
                               scratch_shapes=comm.sems, input_output_aliases=comm.aliases)(*comm.ins))


def project(x, winT, b, name, comm=None):
    T = x.shape[0]
    R = ROWS_PROJ

    def body(x_ref, w_ref, b_ref, h_ref):
        xb = x_ref[...].astype(MXU_DTYPE)
        for c0 in range(0, N_COLS, _PROJ_CHUNK):
            c1 = min(c0 + _PROJ_CHUNK, N_COLS)
            h_ref[:, c0:c1] = _mm_nt(xb, w_ref[c0:c1, :]) + b_ref[:, c0:c1]

    (h,), extra = _call(
        body, name=name, grid=(T // R,),
        in_specs=[pl.BlockSpec((R, D_MODEL), lambda t: (t, 0)), _resident(), _const_block(b.shape)],
        out_specs=[pl.BlockSpec((R, N_COLS), lambda t: (t, 0))],
        out_shape=[jax.ShapeDtypeStruct((T, N_COLS), F32)], scratch_shapes=[],
        compiler_params=_layer_params([winT]), operands=(x, winT, b), comm=comm)
    return h, extra


_KV_BLOCK = C_K // (2 * KV_W)


def layer_forward_and_projection(x, h, wts, winT_next, b_next, S, name):
    T = x.shape[0]
    R = ROWS
    nT = T // R
    npt = R // BLK
    cur = lambda t: jnp.minimum(t, nT - 1)
    (y, hn), _ = _call(
        _make_fwd_proj_body(R, S, nT), name=name, grid=(nT + 1,),
        in_specs=[pl.BlockSpec((R, D_MODEL), lambda t: (cur(t), 0)), pl.BlockSpec((R, N_COLS), lambda t: (cur(t), 0)),
                  pl.BlockSpec((BLK, 2 * KV_W), lambda t: (jnp.maximum(cur(t) * npt - 1, 0), _KV_BLOCK)),
                  _resident(), _const_block(b_next.shape)] + _weight_specs(_W_NAMES, wts),
        out_specs=[pl.BlockSpec((R, D_MODEL), lambda t: (cur(t), 0)),
                   pl.BlockSpec((R, N_COLS), lambda t: (jnp.maximum(t - 1, 0), 0))],
        out_shape=[jax.ShapeDtypeStruct((T, D_MODEL), F32), jax.ShapeDtypeStruct((T, N_COLS), F32)],
        scratch_shapes=[pltpu.VMEM((R, ATTN_W), F32), pltpu.VMEM((R, SGU_W), F32), pltpu.VMEM((R, D_MODEL), MXU_DTYPE)],
        compiler_params=_layer_params([winT_next] + [wts[n] for n in _W_NAMES if n in _BIG_RESIDENT]),
        operands=(x, h, h, winT_next, b_next, *[wts[n] for n in _W_NAMES]))
    return y, hn


_ACC_SHAPES = dict(dwout=(D_MODEL, D_MODEL), dpa=(ATTN_W, D_MODEL), dpb=(SGU_W, D_MODEL), dbin=(1, N_COLS),
                   dbout=(1, D_MODEL), dlng=(1, D_MODEL), dlnb=(1, D_MODEL), dvng=(1, SGU_W), dvnb=(1, SGU_W),
                   dws=(SGU_G, BLK, BLK), dbs=(BLK, SGU_W), dsink=(8, 128), loss=(8, 128))


def layer_backward(x, h, aux, wts, S, is_last, name, comm=None):
    T = x.shape[0]
    R = ROWS_BWD
    nT = T // R
    npt = R // BLK
    rev = lambda i: (nT - 1 - i, 0)
    halo = lambda i: (jnp.maximum((nT - 1 - i) * npt - 1, 0), _KV_BLOCK)
    names = _W_NAMES + _WT_NAMES
    out_shape = ([jax.ShapeDtypeStruct((T, D_MODEL), F32), jax.ShapeDtypeStruct((T, N_COLS), MXU_DTYPE)]
                 + [jax.ShapeDtypeStruct(_ACC_SHAPES[n], F32) for n in _ACC_NAMES])
    outs, extra = _call(
        _make_bwd_body(R, S, nT, is_last), name=name, grid=(nT,),
        in_specs=[pl.BlockSpec((R, D_MODEL), rev), pl.BlockSpec((R, N_COLS), rev), pl.BlockSpec((BLK, 2 * KV_W), halo),
                  pl.BlockSpec((R, D_MODEL), rev)] + _weight_specs(names, wts),
        out_specs=[pl.BlockSpec((R, D_MODEL), rev), pl.BlockSpec((R, N_COLS), rev)]
        + [_operand_spec(n, _ACC_SHAPES[n]) for n in _ACC_NAMES],
        out_shape=out_shape,
        scratch_shapes=[pltpu.VMEM((R, ATTN_W), F32), pltpu.VMEM((R, SGU_W), F32), pltpu.VMEM((BLK, 2 * KV_W), F32)],
        compiler_params=_layer_params([wts[n] for n in names if n in _BIG_RESIDENT]
                                      + [jax.ShapeDtypeStruct(_ACC_SHAPES[n], F32) for n in _ACC_NAMES if n in _BIG_RESIDENT]),
        operands=(x, h, h, aux, *[wts[n] for n in names]), comm=comm)
    return outs[0], outs[1], dict(zip(_ACC_NAMES, outs[2:])), extra


def input_grads(dres, dh, x, winT, name, comm=None):
    T = x.shape[0]
    R = ROWS

    def body(dres_ref, dh_ref, x_ref, winT_ref, dx_ref, dwin_ref):
        @pl.when(pl.program_id(0) == 0)
        def _():
            dwin_ref[...] = jnp.zeros(dwin_ref.shape, F32)

        dh = dh_ref[...]
        dx_ref[...] = dres_ref[...] + jnp.dot(dh, winT_ref[...], preferred_element_type=F32)
        dwin_ref[...] += _mm_tn(dh, x_ref[...])

    row = lambda t: (t, 0)
    dwin_shape = jax.ShapeDtypeStruct((N_COLS, D_MODEL), F32)
    (dx, dwinT), extra = _call(
        body, name=name, grid=(T // R,),
        in_specs=[pl.BlockSpec((R, D_MODEL), row), pl.BlockSpec((R, N_COLS), row), pl.BlockSpec((R, D_MODEL), row),
                  _resident()],
        out_specs=[pl.BlockSpec((R, D_MODEL), row), _resident()],
        out_shape=[jax.ShapeDtypeStruct((T, D_MODEL), F32), dwin_shape], scratch_shapes=[],
        compiler_params=_layer_params([winT, dwin_shape]), operands=(dres, dh, x, winT), comm=comm)
    return dx, dwinT, extra


def input_norm_forward(x, g, b, name, comm=None):
    T = x.shape[0]
    R = ROWS_PROJ

    def body(x_ref, g_ref, b_ref, y_ref):
        y_ref[...] = _ln_fwd(x_ref[...], g_ref[...], b_ref[...])[0]

    row = lambda t: (t, 0)
    (y,), extra = _call(
        body, name=name, grid=(T // R,),
        in_specs=[pl.BlockSpec((R, D_MODEL), row), _const_block(g.shape), _const_block(b.shape)],
        out_specs=[pl.BlockSpec((R, D_MODEL), row)],
        out_shape=[jax.ShapeDtypeStruct((T, D_MODEL), F32)], scratch_shapes=[],
        compiler_params=_layer_params(), operands=(x, g, b), comm=comm)
    return y, extra


def input_norm_backward(x, dy, g, name, comm=None):
    T = x.shape[0]
    R = ROWS_PROJ

    def body(x_ref, dy_ref, g_ref, dx_ref, dg_ref, db_ref):
        @pl.when(pl.program_id(0) == 0)
        def _():
            dg_ref[...] = jnp.zeros(dg_ref.shape, F32)
            db_ref[...] = jnp.zeros(db_ref.shape, F32)

        dy = dy_ref[...]
        _, xhat, rstd = _ln_fwd(x_ref[...], g_ref[...], g_ref[...])
        dx_ref[...] = _ln_bwd(dy, xhat, rstd, g_ref[...])
        dg_ref[...] += _colsum(dy * xhat)
        db_ref[...] += _colsum(dy)

    row = lambda t: (t, 0)
    vec = jax.ShapeDtypeStruct((1, D_MODEL), F32)
    (dx, dg, db), extra = _call(
        body, name=name, grid=(T // R,),
        in_specs=[pl.BlockSpec((R, D_MODEL), row), pl.BlockSpec((R, D_MODEL), row), _const_block(g.shape)],
        out_specs=[pl.BlockSpec((R, D_MODEL), row), _const_block(vec.shape), _const_block(vec.shape)],
        out_shape=[jax.ShapeDtypeStruct((T, D_MODEL), F32), vec, vec], scratch_shapes=[],
        compiler_params=_layer_params(), operands=(x, dy, g), comm=comm)
    return dx, dg, db, extra


_PIECES = ("winT", "pa", "pb", "wout")
_WHOLE = dict(winT=(N_COLS, D_MODEL), pa=(ATTN_W, D_MODEL), pb=(SGU_W, D_MODEL), wout=(D_MODEL, D_MODEL))
_HALF = dict(winT=(N_COLS // 8, D_MODEL), pa=(ATTN_W // 2, D_MODEL // 4), pb=(SGU_W // 2, D_MODEL // 4),
             wout=(D_MODEL // 8, D_MODEL))
SMALL_ROWS = 1280


def _region(name, ref, k, h):
    hr, hc = _HALF[name]
    if name in ("pa", "pb"):
        return ref.at[pl.ds(h * hr, hr), pl.ds(k * hc, hc)]
    return ref.at[pl.ds(k * 2 * hr + h * hr, hr), :]


def _place():
    x, y, c = lax.axis_index("x"), lax.axis_index("y"), lax.axis_index("c")
    return x, y, c, 2 * x + y


def _as_chip(j, fn):
    for jj in range(N_CHIPS):
        pl.when(j == jj)(functools.partial(fn, jj))


def _remote(src, dst, send_sem, recv_sem, to):
    return pltpu.make_async_remote_copy(src_ref=src, dst_ref=dst, send_sem=send_sem, recv_sem=recv_sem,
                                        device_id=to, device_id_type=MESH)


def _core_of(k, c):
    return (k // 2, k % 2, c)


def _others(jj):
    return [k for k in range(N_CHIPS) if k != jj]


def _start_all(cps):
    for cp in cps:
        cp.start()


def _wait_all(cps):
    for cp in cps:
        cp.wait()


def gather_over_chips(names, shards):
    n = len(names)

    def own(cin, cout, sems, jj):
        x, y, c, _ = _place()
        cps = []
        for i, name in enumerate(names):
            hr, hc = _HALF[name]
            place = (cout[i].at[:, pl.ds(jj * hc, hc)] if name in ("pa", "pb")
                     else cout[i].at[pl.ds(jj * 2 * hr, 2 * hr), :])
            cps.append(_remote(cin[i], place, sems[2].at[i], sems[3].at[i], (x, y, 1 - c)))
        return cps

    def copies(cin, cout, sems, jj, c):
        out = []
        for i, name in enumerate(names):
            hr = _HALF[name][0]
            for k in _others(jj):
                out.append(_remote(cin[i].at[pl.ds(c * hr, hr)], _region(name, cout[i], jj, c),
                                   sems[0].at[4 * i + k], sems[1].at[4 * i + jj], _core_of(k, c)))
        return out

    def start(cin, cout, sems):
        _, _, c, j = _place()
        _as_chip(j, lambda jj: _start_all(copies(cin, cout, sems, jj, c) + own(cin, cout, sems, jj)))

    def finish(cin, cout, sems):
        _, _, c, j = _place()

        def run(jj):
            for i, name in enumerate(names):
                for k in _others(jj):
                    land = _region(name, cout[i], k, c)
                    _remote(land, land, sems[0].at[4 * i + k], sems[1].at[4 * i + k], _core_of(k, c)).wait_recv()
            for cp in copies(cin, cout, sems, jj, c):
                cp.wait_send()
            _wait_all(own(cin, cout, sems, jj))

        _as_chip(j, run)

    shapes = [jax.ShapeDtypeStruct(_WHOLE[nm], s.dtype) for nm, s in zip(names, shards)]
    return _Comm(shards, shapes, [_DMA((4 * n,)), _DMA((4 * n,)), _DMA((n,)), _DMA((n,))], start, finish)


def gather_weights(names, shards, turn_step=None):
    n = len(names)
    chips = gather_over_chips(names, shards)
    pair_start, pair_finish = _pair_phase(names)

    def turn(cin, cout, sems):
        chips.finish(cin, cout, sems[:4])
        pair_start(cin, cout, sems[4:])

    return _Comm(shards, chips.out_shapes, chips.sems + [_DMA((4 * n,)), _DMA((4 * n,))], chips.start,
                 lambda cin, cout, sems: pair_finish(cin, cout, sems[4:]), turn=turn, turn_step=turn_step)


def _pair_phase(names):
    def start(cin, cout, sems):
        x, y, c, j = _place()

        def run(jj):
            for i, name in enumerate(names):
                for k in _others(jj):
                    land = _region(name, cout[i], k, c)
                    _remote(land, land, sems[0].at[4 * i + k], sems[1].at[4 * i + k], (x, y, 1 - c)).start()

        _as_chip(j, run)

    def finish(cin, cout, sems):
        x, y, c, j = _place()

        def run(jj):
            for i, name in enumerate(names):
                for k in _others(jj):
                    theirs = _region(name, cout[i], k, 1 - c)
                    _remote(theirs, theirs, sems[0].at[4 * i + k], sems[1].at[4 * i + k], (x, y, 1 - c)).wait_recv()
            for i, name in enumerate(names):
                for k in _others(jj):
                    land = _region(name, cout[i], k, c)
                    _remote(land, land, sems[0].at[4 * i + k], sems[1].at[4 * i + k], (x, y, 1 - c)).wait_send()

        _as_chip(j, run)

    return start, finish


def pair_send_halves(names, parts, small=None, extra=()):
    n = len(names)
    hs = SMALL_ROWS // 2
    first_extra = n + (small is not None)

    def copies(cin, cout, sems):
        x, y, c, _ = _place()
        sib = (x, y, 1 - c)
        cps = []
        for i, name in enumerate(names):
            for k in range(N_CHIPS):
                cps.append(_remote(_region(name, cin[i], k, 1 - c), cout[i].at[k], sems[0].at[4 * i + k],
                                   sems[1].at[4 * i + k], sib))
        if small is not None:
            cps.append(_remote(cin[n].at[pl.ds((1 - c) * hs, hs)], cout[n], sems[0].at[4 * n], sems[1].at[4 * n], sib))
        for e in range(len(extra)):
            cps.append(_remote(cin[first_extra + e], cout[first_extra + e], sems[0].at[4 * n + 1 + e],
                               sems[1].at[4 * n + 1 + e], sib))
        return cps

    start = lambda cin, cout, sems: _start_all(copies(cin, cout, sems))
    finish = lambda cin, cout, sems: _wait_all(copies(cin, cout, sems))
    shapes = [jax.ShapeDtypeStruct((N_CHIPS,) + _HALF[nm], F32) for nm in names]
    ins = list(parts)
    if small is not None:
        shapes.append(jax.ShapeDtypeStruct((hs, 128), F32))
        ins.append(small)
    shapes += [jax.ShapeDtypeStruct(a.shape, a.dtype) for a in extra]
    ins += list(extra)
    n_sem = 4 * n + 1 + len(extra)
    return _Comm(ins, shapes, [_DMA((n_sem,)), _DMA((n_sem,))], start, finish)


def chip_exchange(sums, small=None):
    n = len(sums)

    def copies(cin, cout, sems, jj, c):
        out = []
        for k in _others(jj):
            for i in range(n):
                out.append(_remote(cin[i].at[k], cout[i].at[jj], sems[0].at[4 * i + k], sems[1].at[4 * i + jj], _core_of(k, c)))
            if small is not None:
                out.append(_remote(cin[n], cout[n].at[jj], sems[0].at[4 * n + k], sems[1].at[4 * n + jj], _core_of(k, c)))
        return out

    def start(cin, cout, sems):
        _, _, c, j = _place()

        def run(jj):
            if small is not None:
                pltpu.make_async_copy(cin[n], cout[n].at[jj], sems[2]).start()
            for cp in copies(cin, cout, sems, jj, c):
                cp.start()

        _as_chip(j, run)

    def finish(cin, cout, sems):
        _, _, c, j = _place()

        def run(jj):
            for k in _others(jj):
                for i in range(n):
                    _remote(cin[i].at[k], cout[i].at[k], sems[0].at[4 * i + k], sems[1].at[4 * i + k], _core_of(k, c)).wait_recv()
                if small is not None:
                    _remote(cin[n], cout[n].at[k], sems[0].at[4 * n + k], sems[1].at[4 * n + k], _core_of(k, c)).wait_recv()
            for cp in copies(cin, cout, sems, jj, c):
                cp.wait_send()
            if small is not None:
                pltpu.make_async_copy(cin[n], cout[n].at[jj], sems[2]).wait()

        _as_chip(j, run)

    shapes = [jax.ShapeDtypeStruct(a.shape, a.dtype) for a in sums]
    ins = list(sums)
    if small is not None:
        shapes.append(jax.ShapeDtypeStruct((N_CHIPS,) + small.shape, small.dtype))
        ins.append(small)
    return _Comm(ins, shapes, [_DMA((4 * n + 4,)), _DMA((4 * n + 4,)), _DMA(())], start, finish)


def pair_send_totals(arrs):
    n = len(arrs)

    def copies(cin, cout, sems):
        x, y, c, _ = _place()
        return [_remote(cin[i], cout[i], sems[0].at[i], sems[1].at[i], (x, y, 1 - c)) for i in range(n)]

    start = lambda cin, cout, sems: _start_all(copies(cin, cout, sems))
    finish = lambda cin, cout, sems: _wait_all(copies(cin, cout, sems))
    return _Comm(arrs, [jax.ShapeDtypeStruct(a.shape, a.dtype) for a in arrs], [_DMA((n,)), _DMA((n,))], start, finish)


def _pin(a):
    return pltpu.with_memory_space_constraint(a, pltpu.HBM)


def _plain_params(n_axes):
    return pltpu.CompilerParams(dimension_semantics=("arbitrary",) * n_axes, vmem_limit_bytes=V7X_VMEM_BYTES // 2)


def add_own_halves(names, parts, landed, core, name):
    n = len(names)

    def body(core_ref, *refs):
        for i in range(n):
            refs[2 * n + i][0] = (refs[i][...] + refs[n + i][0]).astype(MXU_DTYPE)

    in_specs, out_specs = [], []
    for nm in names:
        hr, hc = _HALF[nm]
        if nm in ("pa", "pb"):
            in_specs.append(pl.BlockSpec((hr // 2, hc), lambda k, s, cr: (cr[0] * 2 + s, k)))
        else:
            in_specs.append(pl.BlockSpec((hr // 2, hc), lambda k, s, cr: (k * 4 + cr[0] * 2 + s, 0)))
    for nm in names:
        hr, hc = _HALF[nm]
        in_specs.append(pl.BlockSpec((1, hr // 2, hc), lambda k, s, cr: (k, s, 0)))
        out_specs.append(pl.BlockSpec((1, hr // 2, hc), lambda k, s, cr: (k, s, 0)))
    return pl.pallas_call(
        body, name=name,
        grid_spec=pltpu.PrefetchScalarGridSpec(num_scalar_prefetch=1, grid=(N_CHIPS, 2), in_specs=in_specs, out_specs=out_specs),
        out_shape=[pltpu.HBM((N_CHIPS,) + _HALF[nm], MXU_DTYPE) for nm in names],
        compiler_params=_plain_params(2),
    )(core, *[_pin(a) for a in parts], *[_pin(a) for a in landed])


def add_small_half(small, landed, core):
    hs = SMALL_ROWS // 2

    def body(core_ref, a_ref, b_ref, o_ref):
        o_ref[...] = a_ref[...] + b_ref[...]

    return pl.pallas_call(
        body, name="add_small_half",
        grid_spec=pltpu.PrefetchScalarGridSpec(
            num_scalar_prefetch=1, grid=(1,),
            in_specs=[pl.BlockSpec((hs, 128), lambda s, cr: (cr[0], 0)), pl.BlockSpec((hs, 128), lambda s, cr: (0, 0))],
            out_specs=pl.BlockSpec((hs, 128), lambda s, cr: (0, 0))),
        out_shape=jax.ShapeDtypeStruct((hs, 128), F32),
    )(core, small, landed)


def sum_chips(sums, landed, chips, name, small_landed=None):
    n = len(sums)

    def body(chips_ref, *refs):
        ins, outs = refs[:4 * n + (small_landed is not None)], refs[4 * n + (small_landed is not None):]
        for i in range(n):
            tot = ins[4 * i][0].astype(F32)
            for r in range(1, N_CHIPS):
                tot = tot + ins[4 * i + r][0].astype(F32)
            outs[i][...] = tot
        if small_landed is not None:
            @pl.when(pl.program_id(0) == 0)
            def _():
                sm = ins[4 * n]
                outs[n][...] = ((sm[0] + sm[1]) + sm[2]) + sm[3]

    in_specs, out_specs, operands, shapes = [], [], [], []
    for a, l in zip(sums, landed):
        _, hr, hc = a.shape
        for r in range(N_CHIPS):
            in_specs.append(pl.BlockSpec((1, hr // 2, hc), functools.partial(lambda s, ch, r: (ch[r], s, 0), r=r)))
            operands.append(a if r == 0 else l)
        out_specs.append(pl.BlockSpec((hr // 2, hc), lambda s, ch: (s, 0)))
        shapes.append(jax.ShapeDtypeStruct((hr, hc), F32))
    if small_landed is not None:
        in_specs.append(pl.BlockSpec(small_landed.shape, lambda s, ch: (0, 0, 0)))
        out_specs.append(pl.BlockSpec(small_landed.shape[1:], lambda s, ch: (0, 0)))
        operands.append(small_landed)
        shapes.append(jax.ShapeDtypeStruct(small_landed.shape[1:], F32))
    return pl.pallas_call(
        body, name=name,
        grid_spec=pltpu.PrefetchScalarGridSpec(num_scalar_prefetch=1, grid=(2,), in_specs=in_specs, out_specs=out_specs),
        out_shape=[pltpu.HBM(s.shape, s.dtype) for s in shapes], compiler_params=_plain_params(1),
    )(chips, *[_pin(a) for a in operands])


N_STEPS = 8


def _adam_update(w, g, m, v):
    m = ADAM_B1 * m + (1.0 - ADAM_B1) * g
    v = ADAM_B2 * v + (1.0 - ADAM_B2) * (g * g)
    m_hat = m * (1.0 / (1.0 - ADAM_B1 ** ADAM_STEP))
    v_hat = v * (1.0 / (1.0 - ADAM_B2 ** ADAM_STEP))
    return -ADAM_LR * (m_hat / (jnp.sqrt(v_hat) + ADAM_EPS) + ADAM_WD * w), m, v


def adamw_big(ws, g_halves, ms, vs, core):
    n = len(ws)
    per_layer = N_STEPS // DEPTH
    per_half = per_layer // 2

    def body(core_ref, *refs):
        s = pl.program_id(0)
        first_layer = s < per_layer
        mine = ((s % per_layer) // per_half) == core_ref[0]
        for i in range(n):
            w, m, v = (refs[k * n + i][...] for k in range(3))
            g00, g01, g10, g11 = (refs[3 * n + 4 * i + k][...] for k in range(4))
            g = jnp.where(first_layer, jnp.where(mine, g00, g01), jnp.where(mine, g10, g11))[None]
            delta, m, v = _adam_update(w, g, m, v)
            for k, val in enumerate((g, delta, m, v)):
                refs[7 * n + k * n + i][...] = val

    whole = lambda a: pl.BlockSpec((1, a.shape[1] // per_layer, a.shape[2]),
                                   lambda s, cr: (s // per_layer, s % per_layer, 0))

    def half(a, layer, own):
        def index(s, cr):
            first = layer * per_layer + per_half * jnp.where(own, cr[0], 1 - cr[0])
            return (jnp.clip(s - first, 0, per_half - 1), 0)
        return pl.BlockSpec((a.shape[1] // per_layer, a.shape[2]), index)

    g_specs, g_ops = [], []
    for a, halves in zip(ws, g_halves):
        g_specs += [half(a, 0, True), half(a, 0, False), half(a, 1, True), half(a, 1, False)]
        g_ops += [_pin(a) for a in halves]
    outs = pl.pallas_call(
        body, name="adamw_big",
        grid_spec=pltpu.PrefetchScalarGridSpec(
            num_scalar_prefetch=1, grid=(N_STEPS,), in_specs=[whole(a) for a in ws] * 3 + g_specs,
            out_specs=[whole(a) for a in ws] * 4),
        out_shape=[pltpu.HBM(a.shape, F32) for a in ws] * 4,
        compiler_params=_layer_params(),
    )(core, *[_pin(a) for a in (*ws, *ms, *vs)], *g_ops)
    return outs[:n], outs[n:2 * n], outs[2 * n:3 * n], outs[3 * n:]


def adamw_small(ws, gs, ms, vs):
    n = len(ws)

    def body(*refs):
        for i in range(n):
            outs = _adam_update(*(refs[k * n + i][...] for k in range(4)))
            for k, val in enumerate(outs):
                refs[4 * n + k * n + i][...] = val

    specs = [_const_block(a.shape) for a in ws]
    outs = pl.pallas_call(
        body, name="adamw_small", grid=(1,), in_specs=specs * 4, out_specs=specs * 3,
        out_shape=[jax.ShapeDtypeStruct(a.shape, F32) for a in ws] * 3,
        compiler_params=_plain_params(1),
    )(*ws, *gs, *ms, *vs)
    return outs[:n], outs[n:2 * n], outs[2 * n:]


_SMALL = (("w_s", (DEPTH, SGU_G, BLK, BLK)), ("ln_in_g", (D_MODEL,)), ("ln_in_b", (D_MODEL,)), ("b_in", (DEPTH, N_COLS)),
          ("sinks", (DEPTH, 8)), ("vn_g", (DEPTH, SGU_W)), ("vn_b", (DEPTH, SGU_W)),
          ("b_s", (DEPTH, SGU_G, BLK)), ("b_out", (DEPTH, D_MODEL)), ("ln_g", (DEPTH, D_MODEL)), ("ln_b", (DEPTH, D_MODEL)))
_BIG = ("w_in", "p_a", "p_b", "w_out")


def _size(shape):
    n = 1
    for d in shape:
        n *= d
    return n


def _pack_small(vals, last_row=None):
    rows = []
    for name, shape in _SMALL:
        v = vals[name]
        if _size(shape) % 128:
            v = jnp.pad(v.reshape((1, -1)), ((0, 0), (0, (-_size(shape)) % 128)))
        rows.append(v.reshape((-1, 128)))
    used = sum(r.shape[0] for r in rows)
    tail = jnp.zeros((SMALL_ROWS - used, 128), F32)
    if last_row is not None:
        tail = tail.at[-1, 0].set(last_row)
    return jnp.concatenate([rows[0], jnp.concatenate(rows[1:] + [tail], axis=0)], axis=0)


def _unpack_small(packed):
    out, pos = {}, 0
    for name, shape in _SMALL:
        n = -(-_size(shape) // 128)
        rows = packed[pos:pos + n]
        out[name] = (rows.reshape((-1,))[:_size(shape)] if _size(shape) % 128 else rows).reshape(shape)
        pos += n
    return out


def kernel(x, ln_in_g, ln_in_b, w_in, b_in, sinks, vn_g, vn_b, w_s, b_s, p_a, p_b, w_out, b_out, ln_g, ln_b, loss_target, m_ln_in_g, m_ln_in_b, m_w_in, m_b_in, m_sinks, m_vn_g, m_vn_b, m_w_s, m_b_s, m_p_a, m_p_b, m_w_out, m_b_out, m_ln_g, m_ln_b, v_ln_in_g, v_ln_in_b, v_w_in, v_b_in, v_sinks, v_vn_g, v_vn_b, v_w_s, v_b_s, v_p_a, v_p_b, v_w_out, v_b_out, v_ln_g, v_ln_b):
    weights = dict(ln_in_g=ln_in_g, ln_in_b=ln_in_b, w_in=w_in, b_in=b_in, sinks=sinks, vn_g=vn_g, vn_b=vn_b, w_s=w_s,
                   b_s=b_s, p_a=p_a, p_b=p_b, w_out=w_out, b_out=b_out, ln_g=ln_g, ln_b=ln_b)
    mom1 = dict(ln_in_g=m_ln_in_g, ln_in_b=m_ln_in_b, w_in=m_w_in, b_in=m_b_in, sinks=m_sinks, vn_g=m_vn_g, vn_b=m_vn_b,
                w_s=m_w_s, b_s=m_b_s, p_a=m_p_a, p_b=m_p_b, w_out=m_w_out, b_out=m_b_out, ln_g=m_ln_g, ln_b=m_ln_b)
    mom2 = dict(ln_in_g=v_ln_in_g, ln_in_b=v_ln_in_b, w_in=v_w_in, b_in=v_b_in, sinks=v_sinks, vn_g=v_vn_g, vn_b=v_vn_b,
                w_s=v_w_s, b_s=v_b_s, p_a=v_p_a, p_b=v_p_b, w_out=v_w_out, b_out=v_b_out, ln_g=v_ln_g, ln_b=v_ln_b)
    n_seq, S, _ = x.shape
    T = n_seq * S
    c = lax.axis_index("c")
    j = 2 * lax.axis_index("x") + lax.axis_index("y")
    core = c.astype(jnp.int32).reshape((1,))
    chips = jnp.stack([j] + [r + (r >= j) for r in range(N_CHIPS - 1)]).astype(jnp.int32)
    names = list(_PIECES)
    xt, tt = x.reshape((T, D_MODEL)), loss_target.reshape((T, D_MODEL))

    tview = lambda d: dict(d, w_in=jnp.swapaxes(d["w_in"], 1, 2))
    weights_t, mom1_t, mom2_t = tview(weights), tview(mom1), tview(mom2)

    def own_shards(l):
        return [weights_t[n][l].astype(MXU_DTYPE) for n in _BIG]

    def layer_weights(l, gathered):
        d = dict(zip(_PIECES, gathered))
        d.update(paT=d["pa"].T, pbT=d["pb"].T, woutT=d["wout"].T)
        d.update(bin=b_in[l][None], sinks=sinks[l], vng=vn_g[l][None], vnb=vn_b[l][None], ws=w_s[l],
                 bs=jnp.repeat(b_s[l].T, 128, axis=1), bout=b_out[l][None], lng=ln_g[l][None], lnb=ln_b[l][None])
        return d

    own0, own1 = own_shards(0), own_shards(1)
    x0, g0 = input_norm_forward(xt, ln_in_g[None], ln_in_b[None], "ln_in_fwd", comm=gather_weights(names[:1], own0[:1]))
    n_proj = T // ROWS_PROJ
    h0, g1 = project(x0, g0[0], b_in[0][None], "layer0_proj",
                     comm=gather_weights(names[1:] + names, own0[1:] + own1, turn_step=(7 * n_proj) // 8))
    lw0 = layer_weights(0, g0 + g1[:len(names) - 1])
    lw1 = layer_weights(1, g1[len(names) - 1:])
    x1, h1 = layer_forward_and_projection(x0, h0, lw0, lw1["winT"], lw1["bin"], S, "layer0_fwd_layer1_proj")

    dres1, dh1, acc1, _ = layer_backward(x1, h1, tt, lw1, S, True, "layer1_bwd")
    dx1, dwinT1, _ = input_grads(dres1, dh1, x1, lw1["winT"], "layer1_dx_dwin")
    parts1 = [dwinT1, acc1["dpa"], acc1["dpb"], acc1["dwout"]]
    dres0, dh0, acc0, landed1 = layer_backward(x0, h0, dx1, lw0, S, False, "layer0_bwd", comm=pair_send_halves(names, parts1))
    sums1 = add_own_halves(names, parts1, landed1, core, "add_own_halves1")
    dx0, dwinT0, from_chips1 = input_grads(dres0, dh0, x0, lw0["winT"], "layer0_dx_dwin", comm=chip_exchange(sums1))
    totals1 = sum_chips(sums1, from_chips1, chips, "sum_chips1")
    grad_x, d_ln_in_g, d_ln_in_b, _ = input_norm_backward(xt, dx0, ln_in_g, "ln_in_bwd")

    gl = [acc0, acc1]
    per_layer = lambda key, pick: jnp.stack([pick(gl[l][key]) for l in range(DEPTH)])
    row0 = lambda key: per_layer(key, lambda a: a[0])
    small_part = _pack_small(dict(
        ln_in_g=d_ln_in_g[0], ln_in_b=d_ln_in_b[0], b_in=row0("dbin"), sinks=per_layer("dsink", lambda a: a[:, 0]),
        vn_g=row0("dvng"), vn_b=row0("dvnb"), w_s=per_layer("dws", lambda a: a),
        b_s=per_layer("dbs", lambda a: a[:, ::128].T), b_out=row0("dbout"), ln_g=row0("dlng"), ln_b=row0("dlnb")),
        last_row=acc1["loss"][0, 0])
    parts0 = [dwinT0, acc0["dpa"], acc0["dpb"], acc0["dwout"]]
    first = _run_comm(pair_send_halves(names, parts0, small_part, extra=totals1), "pair_send_halves0")
    landed0, small_landed, sib_totals1 = first[:4], first[4], first[5:]
    sums0 = add_own_halves(names, parts0, landed0, core, "add_own_halves0")
    small_sum = add_small_half(small_part, small_landed, core)
    *from_chips0, small_from_chips = _run_comm(chip_exchange(sums0, small_sum), "chip_exchange0")
    *totals0, small_total = sum_chips(sums0, from_chips0, chips, "sum_chips0", small_landed=small_from_chips)
    *sib_totals0, sib_small = _run_comm(pair_send_totals(totals0 + [small_total]), "pair_send_totals0")

    def both_halves(mine, theirs):
        return jnp.where(c == 0, jnp.concatenate([mine, theirs], axis=0), jnp.concatenate([theirs, mine], axis=0))

    g_halves = [(totals0[pi], sib_totals0[pi], totals1[pi], sib_totals1[pi]) for pi in range(len(_BIG))]
    g_small = both_halves(small_total, sib_small)
    loss = g_small[-1, 0]

    big = adamw_big([weights_t[n] for n in _BIG], g_halves, [mom1_t[n] for n in _BIG], [mom2_t[n] for n in _BIG], core)
    small_names = [n for n, _ in _SMALL]
    as2d = lambda a: a.reshape((1, -1)) if a.ndim == 1 else a
    g_named = _unpack_small(g_small)
    small = adamw_small(*[[as2d(d[n]) for n in small_names] for d in (weights, g_named, mom1, mom2)])

    def named(big_outs, small_outs):
        d = dict(zip(_BIG, big_outs))
        d["w_in"] = jnp.swapaxes(d["w_in"], 1, 2)
        d.update({n: o.reshape(weights[n].shape) for n, o in zip(small_names, small_outs)})
        return d

    order = ("ln_in_g", "ln_in_b", "w_in", "b_in", "sinks", "vn_g", "vn_b", "w_s", "b_s", "p_a", "p_b", "w_out", "b_out",
             "ln_g", "ln_b")
    res = [loss, grad_x.reshape(x.shape)]
    groups = [named(big[0], [g_named[n] for n in small_names])] + [named(big[k + 1], small[k]) for k in range(3)]
    for group in groups:
        res.extend(group[n] for n in order)
    return tuple(res)
```

```python
import functools

import jax
import jax.numpy as jnp
from jax import lax
from jax.experimental import pallas as pl
from jax.experimental.pallas import tpu as pltpu

F32 = jnp.float32
MXU_DTYPE = jnp.bfloat16

D_MODEL = 1024
DEPTH = 2
HEAD_DIM = 64
ATTN_W = 512
KV_W = 128
BLK = 128
SGU_W = 512
SGU_G = 4
N_COLS = 4864
C_Q, C_K, C_V, C_GA, C_UB, C_VB, C_GB, C_RA, C_RB = 0, 512, 640, 768, 1280, 1792, 2304, 2816, 3840
ALPHA = (2.0 * DEPTH) ** 0.25
LN_EPS = 1e-5
SCALE = HEAD_DIM ** -0.5
NEG = float(jnp.finfo(jnp.float32).min)
GELU_C = 0.7978845608028654
GELU_A = 0.044715

ADAM_LR, ADAM_B1, ADAM_B2, ADAM_EPS, ADAM_WD, ADAM_STEP = 0.001, 0.9, 0.999, 1e-08, 0.01, 10

V7X_VMEM_BYTES = 64 * 1024 * 1024
V7X_VMEM_RESERVE = 3 * 1024 * 1024
ROWS = 256
ROWS_BWD = 256
ROWS_PROJ = 512

MESH = pl.DeviceIdType.MESH
N_CHIPS = 4


def _mm(a, b):
    return jnp.dot(a.astype(MXU_DTYPE), b.astype(MXU_DTYPE), preferred_element_type=F32)


def _mm_nt(a, b):
    return lax.dot_general(a.astype(MXU_DTYPE), b.astype(MXU_DTYPE), (((1,), (1,)), ((), ())),
                           preferred_element_type=F32)


def _mm_tn(a, b):
    return lax.dot_general(a.astype(MXU_DTYPE), b.astype(MXU_DTYPE), (((0,), (0,)), ((), ())),
                           preferred_element_type=F32)


def _sigmoid(x):
    return 0.5 * jnp.tanh(0.5 * x) + 0.5


def _gelu_parts(x):
    x2 = x * x
    u = 0.5 * jnp.tanh(x * (GELU_C + (GELU_C * GELU_A) * x2)) + 0.5
    dg = u * (1.0 + x * (1.0 - u) * (2.0 * GELU_C + (6.0 * GELU_C * GELU_A) * x2))
    return x * u, dg


def _ln_fwd(z, g, b):
    mu = jnp.mean(z, axis=-1, keepdims=True)
    zc = z - mu
    var = jnp.mean(zc * zc, axis=-1, keepdims=True)
    rstd = lax.rsqrt(var + LN_EPS)
    zhat = zc * rstd
    return zhat * g + b, zhat, rstd


def _ln_bwd(dy, zhat, rstd, g):
    dzh = dy * g
    m1 = jnp.mean(dzh, axis=-1, keepdims=True)
    m2 = jnp.mean(dzh * zhat, axis=-1, keepdims=True)
    return rstd * (dzh - m1 - zhat * m2)


def _colsum(v):
    return jnp.sum(v, axis=0, keepdims=True)


def _lane_lo(rows):
    return lax.broadcasted_iota(jnp.int32, (rows, 128), 1) < HEAD_DIM


def _dup_head(x2, hk, lo):
    xr = pltpu.roll(x2, HEAD_DIM, 1)
    return jnp.where(lo, x2, xr) if hk == 0 else jnp.where(lo, xr, x2)


def _fold_head(y0, y1, lo):
    f0 = y0 + pltpu.roll(y0, HEAD_DIM, 1)
    f1 = y1 + pltpu.roll(y1, HEAD_DIM, 1)
    return jnp.where(lo, f0, f1)


def _stack_heads(t0, t1, lo):
    z = jnp.zeros_like(t0)
    return jnp.concatenate([jnp.where(lo, t0, z), jnp.where(lo, z, t0),
                            jnp.where(lo, t1, z), jnp.where(lo, z, t1)], axis=0)


def _unstack_heads(o, lo):
    return (jnp.where(lo, o[0:128], o[128:256]), jnp.where(lo, o[256:384], o[384:512]))


def _band_bias(start):
    row = lax.broadcasted_iota(jnp.int32, (4 * BLK, 2 * BLK), 0) & (BLK - 1)
    kpos = lax.broadcasted_iota(jnp.int32, (4 * BLK, 2 * BLK), 1)
    valid = (kpos > row) & (kpos <= row + BLK)
    if start is not None:
        valid = valid & (jnp.logical_not(start) | (kpos >= BLK))
    return jnp.where(valid, 0.0, NEG)


def _attn_probs(s, sink4, bias):
    s = s * SCALE + bias
    r1 = lax.broadcasted_iota(jnp.int32, (4 * BLK, 1), 0)
    sk = jnp.where(r1 < BLK, sink4[0], jnp.where(r1 < 2 * BLK, sink4[1], jnp.where(r1 < 3 * BLK, sink4[2], sink4[3])))
    m = jnp.maximum(jnp.max(s, axis=-1, keepdims=True), sk)
    p = jnp.exp(s - m)
    es = jnp.exp(sk - m)
    inv = 1.0 / (jnp.sum(p, axis=-1, keepdims=True) + es)
    return p * inv, es * inv


_PROJ_CHUNK = 512


def _kv_blocks(kvh, h_s, nb):
    ks = [kvh[:, 0:KV_W]] + [h_s[n * BLK:(n + 1) * BLK, C_K:C_K + KV_W] for n in range(nb)]
    vs = [kvh[:, KV_W:2 * KV_W]] + [h_s[n * BLK:(n + 1) * BLK, C_V:C_V + KV_W] for n in range(nb)]
    return ks, vs


def _attn_operands(h_s, ks, vs, n, hk, lo, lo2):
    rows = slice(n * BLK, (n + 1) * BLK)
    kd = _dup_head(jnp.concatenate([ks[n], ks[n + 1]], axis=0), hk, lo2)
    vd = _dup_head(jnp.concatenate([vs[n], vs[n + 1]], axis=0), hk, lo2)
    c0 = C_Q + 2 * hk * 128
    qs = _stack_heads(h_s[rows, c0:c0 + 128], h_s[rows, c0 + 128:c0 + 256], lo)
    return qs, kd, vd


def _sgu_weights(w):
    tri = (lax.broadcasted_iota(jnp.int32, (BLK, BLK), 0) >= lax.broadcasted_iota(jnp.int32, (BLK, BLK), 1))
    return tri, [jnp.where(tri, w["ws"][g], 0.0) for g in range(SGU_G)]


def _layer_forward(x, kvh, start, w, h_s, attn_s, mix_s, keep, tick=lambda: None):
    R = x.shape[0]
    nb = R // BLK
    lo = _lane_lo(BLK)
    lo2 = _lane_lo(2 * BLK)
    ks, vs = _kv_blocks(kvh, h_s, nb)
    sinks = [w["sinks"][j] for j in range(8)]
    bias_first = _band_bias(start)
    bias_rest = _band_bias(None) if nb > 1 else None
    pairs = [(n, hk) for n in range(nb) for hk in range(2)]
    ops = [_attn_operands(h_s, ks, vs, n, hk, lo, lo2) for n, hk in pairs]
    scores = [_mm_nt(qs, kd) for qs, kd, _ in ops]
    probs_l = [_attn_probs(s, sinks[4 * hk:4 * hk + 4], bias_first if n == 0 else bias_rest)
               for s, (n, hk) in zip(scores, pairs)]
    outs = [_mm(p, vd) for (p, _), (_, _, vd) in zip(probs_l, ops)]
    for o, (n, hk) in zip(outs, pairs):
        rows = slice(n * BLK, (n + 1) * BLK)
        t0, t1 = _unstack_heads(o, lo)
        attn_s[rows, 2 * hk * 128:(2 * hk + 1) * 128] = t0
        attn_s[rows, (2 * hk + 1) * 128:(2 * hk + 2) * 128] = t1
        tick()
    ga = h_s[:, C_GA:C_GA + ATTN_W]
    ya = attn_s[...] * (ga * _sigmoid(ga))
    tick()
    gu, _ = _gelu_parts(h_s[:, C_UB:C_UB + SGU_W])
    tick()
    gv, _ = _gelu_parts(h_s[:, C_VB:C_VB + SGU_W])
    vn, _, _ = _ln_fwd(gv, w["vng"][...], w["vnb"][...])
    tick()
    _, wms = _sgu_weights(w)
    tiles = [(slice(n * BLK, (n + 1) * BLK), g, slice(g * 128, (g + 1) * 128)) for n in range(nb) for g in range(SGU_G)]
    mixes = [_mm(wms[g], vn[rows, cols]) for rows, g, cols in tiles]
    for m, (rows, g, cols) in zip(mixes, tiles):
        mix_s[rows, cols] = m + w["bs"][:, cols]
    gb = h_s[:, C_GB:C_GB + SGU_W]
    yb = gu * mix_s[...] * (gb * _sigmoid(gb))
    tick()
    a = _mm(ya, w["pa"][...])
    b = _mm(yb, w["pb"][...])
    tick()
    merged = _sigmoid(h_s[:, C_RA:C_RA + D_MODEL]) * a + _sigmoid(h_s[:, C_RB:C_RB + D_MODEL]) * b
    tick()
    out = _mm(merged, w["wout"][...]) + w["bout"][...]
    y, zhat, rstd = _ln_fwd(ALPHA * x + out, w["lng"][...], w["lnb"][...])
    if not keep:
        return y, None
    return y, dict(ks=ks, vs=vs, probs=probs_l, ya=ya, yb=yb, a=a, b=b, merged=merged, zhat=zhat, rstd=rstd)


def _dsilu(g, sg):
    return sg * (1.0 + g * (1.0 - sg))


_W_NAMES = ("sinks", "vng", "vnb", "ws", "bs", "pa", "pb", "wout", "bout", "lng", "lnb")
_WT_NAMES = ("paT", "pbT", "woutT")
_ACC_NAMES = ("dwout", "dpa", "dpb", "dbin", "dbout", "dlng", "dlnb", "dvng", "dvnb", "dws", "dbs", "dsink", "loss")


def _make_fwd_proj_body(R, S, nT):
    def body(x_ref, h_ref, kvh_ref, wn_ref, bn_ref, *rest):
        w = dict(zip(_W_NAMES, rest[:len(_W_NAMES)]))
        y_ref, hn_ref, attn_s, mix_s, yprev_s = rest[len(_W_NAMES):]

        @pl.when(pl.program_id(0) == 0)
        def _():
            yprev_s[...] = jnp.zeros(yprev_s.shape, yprev_s.dtype)

        t = jnp.minimum(pl.program_id(0), nT - 1)
        start = (t % (S // R)) == 0
        chunks = [(c0, min(c0 + _PROJ_CHUNK, N_COLS)) for c0 in range(0, N_COLS, _PROJ_CHUNK)]

        def tick():
            if chunks:
                c0, c1 = chunks.pop(0)
                hn_ref[:, c0:c1] = _mm_nt(yprev_s[...], wn_ref[c0:c1, :]) + bn_ref[:, c0:c1]

        y, _ = _layer_forward(x_ref[...], kvh_ref[...], start, w, h_ref, attn_s, mix_s, keep=False, tick=tick)
        while chunks:
            tick()
        y_ref[...] = y
        yprev_s[...] = y.astype(yprev_s.dtype)
    return body


def _make_bwd_body(R, S, nT, is_last):
    nb = R // BLK
    names = _W_NAMES + _WT_NAMES
    nw = len(names)

    def body(x_ref, h_s, kvh_ref, aux_ref, *rest):
        w = dict(zip(names, rest[:nw]))
        dres_ref, dh_ref = rest[nw:nw + 2]
        acc = dict(zip(_ACC_NAMES, rest[nw + 2:nw + 2 + len(_ACC_NAMES)]))
        attn_s, mix_s, ckv_s = rest[nw + 2 + len(_ACC_NAMES):]
        i = pl.program_id(0)
        start = ((nT - 1 - i) % (S // R)) == 0
        lo = _lane_lo(BLK)
        lo2 = _lane_lo(2 * BLK)

        @pl.when(i == 0)
        def _():
            for name in _ACC_NAMES:
                acc[name][...] = jnp.zeros(acc[name].shape, F32)
            ckv_s[...] = jnp.zeros(ckv_s.shape, F32)

        x = x_ref[...]
        y, sv = _layer_forward(x, kvh_ref[...], start, w, h_s, attn_s, mix_s, keep=True)
        if is_last:
            diff = y - aux_ref[...]
            part = 0.5 * jnp.sum(jnp.mean(diff * diff, axis=-1, keepdims=True), axis=0, keepdims=True)
            acc["loss"][...] += jnp.broadcast_to(part, acc["loss"].shape)
            dy = diff * (1.0 / D_MODEL)
        else:
            dy = aux_ref[...]

        def put(c0, val, rows=slice(None)):
            width = val.shape[1]
            dh_ref[rows, c0:c0 + width] = val.astype(dh_ref.dtype)
            acc["dbin"][:, c0:c0 + width] += _colsum(val)

        acc["dlng"][...] += _colsum(dy * sv["zhat"])
        acc["dlnb"][...] += _colsum(dy)
        dz = _ln_bwd(dy, sv["zhat"], sv["rstd"], w["lng"][...])
        dres_ref[...] = ALPHA * dz
        acc["dbout"][...] += _colsum(dz)
        acc["dwout"][...] += _mm_tn(sv["merged"], dz)
        dmerged = _mm(dz, w["woutT"][...])
        sa = _sigmoid(h_s[:, C_RA:C_RA + D_MODEL])
        da = dmerged * sa
        put(C_RA, da * sv["a"] * (1.0 - sa))
        sb = _sigmoid(h_s[:, C_RB:C_RB + D_MODEL])
        db = dmerged * sb
        put(C_RB, db * sv["b"] * (1.0 - sb))
        acc["dpa"][...] += _mm_tn(sv["ya"], da)
        acc["dpb"][...] += _mm_tn(sv["yb"], db)
        dya = _mm(da, w["paT"][...])
        dyb = _mm(db, w["pbT"][...])
        ga = h_s[:, C_GA:C_GA + ATTN_W]
        sga = _sigmoid(ga)
        put(C_GA, dya * attn_s[...] * _dsilu(ga, sga))
        attn_s[...] = dya * (ga * sga)
        ks, vs = sv["ks"], sv["vs"]
        pairs = [(n, hk) for n in range(nb) for hk in range(2)]
        ops = [_attn_operands(h_s, ks, vs, n, hk, lo, lo2) for n, hk in pairs]
        doss = [_stack_heads(attn_s[n * BLK:(n + 1) * BLK, 2 * hk * 128:(2 * hk + 1) * 128],
                             attn_s[n * BLK:(n + 1) * BLK, (2 * hk + 1) * 128:(2 * hk + 2) * 128], lo) for n, hk in pairs]
        dps = [_mm_nt(dos, vd) for dos, (_, _, vd) in zip(doss, ops)]
        gb = h_s[:, C_GB:C_GB + SGU_W]
        sgb = _sigmoid(gb)
        gu, dgu = _gelu_parts(h_s[:, C_UB:C_UB + SGU_W])
        mixed = mix_s[...]
        put(C_GB, dyb * (gu * mixed) * _dsilu(gb, sgb))
        dsgu = dyb * (gb * sgb)
        put(C_UB, dsgu * mixed * dgu)
        dmixed = dsgu * gu
        dsss = []
        for dp, (probs, ps), (n, hk) in zip(dps, sv["probs"], pairs):
            delta = jnp.sum(probs * dp, axis=-1, keepdims=True)
            dsk = -(ps * delta)
            for g in range(4):
                j = 4 * hk + g
                acc["dsink"][j:j + 1, :] += jnp.broadcast_to(_colsum(dsk[g * BLK:(g + 1) * BLK]), (1, 128))
            dsss.append(probs * (dp - delta) * SCALE)
        dqs = [_mm(dss, kd) for dss, (_, kd, _) in zip(dsss, ops)]
        ydks = [_mm_tn(dss, qs) for dss, (qs, _, _) in zip(dsss, ops)]
        ydvs = [_mm_tn(probs, dos) for (probs, _), dos in zip(sv["probs"], doss)]
        gv, dgv = _gelu_parts(h_s[:, C_VB:C_VB + SGU_W])
        vn, vhat, vrstd = _ln_fwd(gv, w["vng"][...], w["vnb"][...])
        tri, wms = _sgu_weights(w)
        tiles = [(slice(n * BLK, (n + 1) * BLK), g, slice(g * 128, (g + 1) * 128)) for n in range(nb) for g in range(SGU_G)]
        dw_parts = [_mm_nt(dmixed[rows, cols], vn[rows, cols]) for rows, g, cols in tiles]
        dvn_parts = [_mm_tn(wms[g], dmixed[rows, cols]) for rows, g, cols in tiles]
        for g in range(SGU_G):
            tot = dw_parts[g]
            for n in range(1, nb):
                tot = tot + dw_parts[n * SGU_G + g]
            acc["dws"][g] += jnp.where(tri, tot, 0.0)
        for part, (rows, g, cols) in zip(dvn_parts, tiles):
            mix_s[rows, cols] = part
        dbs_part = dmixed[0:BLK]
        for n in range(1, nb):
            dbs_part = dbs_part + dmixed[n * BLK:(n + 1) * BLK]
        acc["dbs"][...] += dbs_part
        for dq, (n, hk) in zip(dqs, pairs):
            q0, q1 = _unstack_heads(dq, lo)
            put(C_Q + 2 * hk * 128, q0, slice(n * BLK, (n + 1) * BLK))
            put(C_Q + (2 * hk + 1) * 128, q1, slice(n * BLK, (n + 1) * BLK))
        dks = [jnp.zeros((BLK, KV_W), F32) for _ in range(nb + 1)]
        dvs = [jnp.zeros((BLK, KV_W), F32) for _ in range(nb + 1)]
        for n in range(nb):
            dk2 = _fold_head(ydks[2 * n], ydks[2 * n + 1], lo2)
            dv2 = _fold_head(ydvs[2 * n], ydvs[2 * n + 1], lo2)
            dks[n] = dks[n] + dk2[0:BLK]
            dks[n + 1] = dks[n + 1] + dk2[BLK:2 * BLK]
            dvs[n] = dvs[n] + dv2[0:BLK]
            dvs[n + 1] = dvs[n + 1] + dv2[BLK:2 * BLK]
        dks[nb] = dks[nb] + ckv_s[:, 0:KV_W]
        dvs[nb] = dvs[nb] + ckv_s[:, KV_W:2 * KV_W]
        ckv_s[:, 0:KV_W] = dks[0]
        ckv_s[:, KV_W:2 * KV_W] = dvs[0]
        put(C_K, jnp.concatenate(dks[1:], axis=0))
        put(C_V, jnp.concatenate(dvs[1:], axis=0))
        dvn = mix_s[...]
        acc["dvng"][...] += _colsum(dvn * vhat)
        acc["dvnb"][...] += _colsum(dvn)
        put(C_VB, _ln_bwd(dvn, vhat, vrstd, w["vng"][...]) * dgv)

        @pl.when(i == nT - 1)
        def _():
            for g in range(SGU_G):
                cols = slice(g * 128, (g + 1) * 128)
                tot = jnp.sum(acc["dbs"][:, cols], axis=1, keepdims=True)
                acc["dbs"][:, cols] = jnp.broadcast_to(tot, (BLK, 128))

    return body


def _resident():
    return pl.BlockSpec(memory_space=pltpu.VMEM)


_BIG_RESIDENT = ("pa", "pb", "wout", "paT", "pbT", "woutT", "dwout", "dpa", "dpb")


def _const_block(shape):
    return pl.BlockSpec(tuple(shape), lambda *_: (0,) * len(shape))


def _operand_spec(name, shape):
    if name == "sinks":
        return pl.BlockSpec(memory_space=pltpu.SMEM)
    return _resident() if name in _BIG_RESIDENT else _const_block(shape)


def _weight_specs(names, wts):
    return [_operand_spec(n, wts[n].shape) for n in names]


def _nbytes(a):
    n = jnp.dtype(a.dtype).itemsize
    for d in a.shape:
        n *= d
    return n


def _layer_params(resident=()):
    scoped = V7X_VMEM_BYTES - V7X_VMEM_RESERVE - sum(_nbytes(a) for a in resident)
    return pltpu.CompilerParams(dimension_semantics=("arbitrary",), vmem_limit_bytes=scoped)


class _Comm:
    def __init__(self, ins, out_shapes, sems, start, finish, aliases=None, turn=None, turn_step=None):
        self.ins = [pltpu.with_memory_space_constraint(a, pltpu.HBM) for a in ins]
        self.out_shapes = [pltpu.HBM(s.shape, s.dtype) for s in out_shapes]
        self.sems = list(sems)
        self.start, self.finish, self.aliases = start, finish, dict(aliases or {})
        self.turn, self.turn_step = turn, turn_step


_ANY = pl.BlockSpec(memory_space=pltpu.HBM)
_DMA = pltpu.SemaphoreType.DMA


def _call(body, *, name, grid, in_specs, out_specs, out_shape, scratch_shapes, compiler_params, operands, comm=None):
    if comm is None:
        res = pl.pallas_call(body, name=name, grid=grid, in_specs=in_specs, out_specs=out_specs, out_shape=out_shape,
                             scratch_shapes=scratch_shapes, compiler_params=compiler_params)(*operands)
        return list(res), []
    n_in, n_out, n_scr = len(in_specs), len(out_specs), len(scratch_shapes)
    ci, co = len(comm.ins), len(comm.out_shapes)
    last = grid[0] - 1

    def hosted(*refs):
        ins, cin = refs[:n_in], refs[n_in:n_in + ci]
        p = n_in + ci
        outs, cout = refs[p:p + n_out], refs[p + n_out:p + n_out + co]
        p += n_out + co
        scr, sems = refs[p:p + n_scr], refs[p + n_scr:]
        pl.when(pl.program_id(0) == 0)(lambda: comm.start(cin, cout, sems))
        body(*ins, *outs, *scr)
        if comm.turn is not None:
            at = last if comm.turn_step is None else comm.turn_step
            pl.when(pl.program_id(0) == at)(lambda: comm.turn(cin, cout, sems))
        pl.when(pl.program_id(0) == last)(lambda: comm.finish(cin, cout, sems))

    res = pl.pallas_call(
        hosted, name=name, grid=grid, in_specs=list(in_specs) + [_ANY] * ci, out_specs=list(out_specs) + [_ANY] * co,
        out_shape=list(out_shape) + comm.out_shapes, scratch_shapes=list(scratch_shapes) + comm.sems,
        input_output_aliases={n_in + a: n_out + b for a, b in comm.aliases.items()},
        compiler_params=compiler_params)(*operands, *comm.ins)
    return list(res[:n_out]), list(res[n_out:])


def _run_comm(comm, name):
    ci, co = len(comm.ins), len(comm.out_shapes)

    def body(*refs):
        cin, cout, sems = refs[:ci], refs[ci:ci + co], refs[ci + co:]
        comm.start(cin, cout, sems)
        if comm.turn is not None:
            comm.turn(cin, cout, sems)
        comm.finish(cin, cout, sems)

    return list(pl.pallas_call(body, name=name, in_specs=[_ANY] * ci, out_specs=[_ANY] * co, out_shape=comm.out_shapes,
                               scratch_shapes=comm.sems, input_output_aliases=comm.aliases)(*comm.ins))


def project(x, winT, b, name, comm=None):
    T = x.shape[0]
    R = ROWS_PROJ

    def body(x_ref, w_ref, b_ref, h_ref):
        xb = x_ref[...].astype(MXU_DTYPE)
        for c0 in range(0, N_COLS, _PROJ_CHUNK):
            c1 = min(c0 + _PROJ_CHUNK, N_COLS)
            h_ref[:, c0:c1] = _mm_nt(xb, w_ref[c0:c1, :]) + b_ref[:, c0:c1]

    (h,), extra = _call(
        body, name=name, grid=(T // R,),
        in_specs=[pl.BlockSpec((R, D_MODEL), lambda t: (t, 0)), _resident(), _const_block(b.shape)],
        out_specs=[pl.BlockSpec((R, N_COLS), lambda t: (t, 0))],
        out_shape=[jax.ShapeDtypeStruct((T, N_COLS), F32)], scratch_shapes=[],
        compiler_params=_layer_params([winT]), operands=(x, winT, b), comm=comm)
    return h, extra


_KV_BLOCK = C_K // (2 * KV_W)


def layer_forward_and_projection(x, h, wts, winT_next, b_next, S, name, comm=None):
    T = x.shape[0]
    R = ROWS
    nT = T // R
    npt = R // BLK
    cur = lambda t: jnp.minimum(t, nT - 1)
    (y, hn), extra = _call(
        _make_fwd_proj_body(R, S, nT), name=name, grid=(nT + 1,),
        in_specs=[pl.BlockSpec((R, D_MODEL), lambda t: (cur(t), 0)), pl.BlockSpec((R, N_COLS), lambda t: (cur(t), 0)),
                  pl.BlockSpec((BLK, 2 * KV_W), lambda t: (jnp.maximum(cur(t) * npt - 1, 0), _KV_BLOCK)),
                  _resident(), _const_block(b_next.shape)] + _weight_specs(_W_NAMES, wts),
        out_specs=[pl.BlockSpec((R, D_MODEL), lambda t: (cur(t), 0)),
                   pl.BlockSpec((R, N_COLS), lambda t: (jnp.maximum(t - 1, 0), 0))],
        out_shape=[jax.ShapeDtypeStruct((T, D_MODEL), F32), jax.ShapeDtypeStruct((T, N_COLS), F32)],
        scratch_shapes=[pltpu.VMEM((R, ATTN_W), F32), pltpu.VMEM((R, SGU_W), F32), pltpu.VMEM((R, D_MODEL), MXU_DTYPE)],
        compiler_params=_layer_params([winT_next] + [wts[n] for n in _W_NAMES if n in _BIG_RESIDENT]),
        operands=(x, h, h, winT_next, b_next, *[wts[n] for n in _W_NAMES]), comm=comm)
    return y, hn, extra


_ACC_SHAPES = dict(dwout=(D_MODEL, D_MODEL), dpa=(ATTN_W, D_MODEL), dpb=(SGU_W, D_MODEL), dbin=(1, N_COLS),
                   dbout=(1, D_MODEL), dlng=(1, D_MODEL), dlnb=(1, D_MODEL), dvng=(1, SGU_W), dvnb=(1, SGU_W),
                   dws=(SGU_G, BLK, BLK), dbs=(BLK, SGU_W), dsink=(8, 128), loss=(8, 128))


def layer_backward(x, h, aux, wts, S, is_last, name, comm=None):
    T = x.shape[0]
    R = ROWS_BWD
    nT = T // R
    npt = R // BLK
    rev = lambda i: (nT - 1 - i, 0)
    halo = lambda i: (jnp.maximum((nT - 1 - i) * npt - 1, 0), _KV_BLOCK)
    names = _W_NAMES + _WT_NAMES
    out_shape = ([jax.ShapeDtypeStruct((T, D_MODEL), F32), jax.ShapeDtypeStruct((T, N_COLS), MXU_DTYPE)]
                 + [jax.ShapeDtypeStruct(_ACC_SHAPES[n], F32) for n in _ACC_NAMES])
    outs, extra = _call(
        _make_bwd_body(R, S, nT, is_last), name=name, grid=(nT,),
        in_specs=[pl.BlockSpec((R, D_MODEL), rev), pl.BlockSpec((R, N_COLS), rev), pl.BlockSpec((BLK, 2 * KV_W), halo),
                  pl.BlockSpec((R, D_MODEL), rev)] + _weight_specs(names, wts),
        out_specs=[pl.BlockSpec((R, D_MODEL), rev), pl.BlockSpec((R, N_COLS), rev)]
        + [_operand_spec(n, _ACC_SHAPES[n]) for n in _ACC_NAMES],
        out_shape=out_shape,
        scratch_shapes=[pltpu.VMEM((R, ATTN_W), F32), pltpu.VMEM((R, SGU_W), F32), pltpu.VMEM((BLK, 2 * KV_W), F32)],
        compiler_params=_layer_params([wts[n] for n in names if n in _BIG_RESIDENT]
                                      + [jax.ShapeDtypeStruct(_ACC_SHAPES[n], F32) for n in _ACC_NAMES if n in _BIG_RESIDENT]),
        operands=(x, h, h, aux, *[wts[n] for n in names]), comm=comm)
    return outs[0], outs[1], dict(zip(_ACC_NAMES, outs[2:])), extra


def input_grads(dres, dh, x, winT, name, comm=None):
    T = x.shape[0]
    R = ROWS

    def body(dres_ref, dh_ref, x_ref, winT_ref, dx_ref, dwin_ref):
        @pl.when(pl.program_id(0) == 0)
        def _():
            dwin_ref[...] = jnp.zeros(dwin_ref.shape, F32)

        dh = dh_ref[...]
        dx_ref[...] = dres_ref[...] + jnp.dot(dh, winT_ref[...], preferred_element_type=F32)
        dwin_ref[...] += _mm_tn(dh, x_ref[...])

    row = lambda t: (t, 0)
    dwin_shape = jax.ShapeDtypeStruct((N_COLS, D_MODEL), F32)
    (dx, dwinT), extra = _call(
        body, name=name, grid=(T // R,),
        in_specs=[pl.BlockSpec((R, D_MODEL), row), pl.BlockSpec((R, N_COLS), row), pl.BlockSpec((R, D_MODEL), row),
                  _resident()],
        out_specs=[pl.BlockSpec((R, D_MODEL), row), _resident()],
        out_shape=[jax.ShapeDtypeStruct((T, D_MODEL), F32), dwin_shape], scratch_shapes=[],
        compiler_params=_layer_params([winT, dwin_shape]), operands=(dres, dh, x, winT), comm=comm)
    return dx, dwinT, extra


def input_norm_forward(x, g, b, name, comm=None):
    T = x.shape[0]
    R = ROWS_PROJ

    def body(x_ref, g_ref, b_ref, y_ref):
        y_ref[...] = _ln_fwd(x_ref[...], g_ref[...], b_ref[...])[0]

    row = lambda t: (t, 0)
    (y,), extra = _call(
        body, name=name, grid=(T // R,),
        in_specs=[pl.BlockSpec((R, D_MODEL), row), _const_block(g.shape), _const_block(b.shape)],
        out_specs=[pl.BlockSpec((R, D_MODEL), row)],
        out_shape=[jax.ShapeDtypeStruct((T, D_MODEL), F32)], scratch_shapes=[],
        compiler_params=_layer_params(), operands=(x, g, b), comm=comm)
    return y, extra


def input_norm_backward(x, dy, g, name, comm=None):
    T = x.shape[0]
    R = ROWS_PROJ

    def body(x_ref, dy_ref, g_ref, dx_ref, dg_ref, db_ref):
        @pl.when(pl.program_id(0) == 0)
        def _():
            dg_ref[...] = jnp.zeros(dg_ref.shape, F32)
            db_ref[...] = jnp.zeros(db_ref.shape, F32)

        dy = dy_ref[...]
        _, xhat, rstd = _ln_fwd(x_ref[...], g_ref[...], g_ref[...])
        dx_ref[...] = _ln_bwd(dy, xhat, rstd, g_ref[...])
        dg_ref[...] += _colsum(dy * xhat)
        db_ref[...] += _colsum(dy)

    row = lambda t: (t, 0)
    vec = jax.ShapeDtypeStruct((1, D_MODEL), F32)
    (dx, dg, db), extra = _call(
        body, name=name, grid=(T // R,),
        in_specs=[pl.BlockSpec((R, D_MODEL), row), pl.BlockSpec((R, D_MODEL), row), _const_block(g.shape)],
        out_specs=[pl.BlockSpec((R, D_MODEL), row), _const_block(vec.shape), _const_block(vec.shape)],
        out_shape=[jax.ShapeDtypeStruct((T, D_MODEL), F32), vec, vec], scratch_shapes=[],
        compiler_params=_layer_params(), operands=(x, dy, g), comm=comm)
    return dx, dg, db, extra


_PIECES = ("winT", "pa", "pb", "wout")
_WHOLE = dict(winT=(N_COLS, D_MODEL), pa=(ATTN_W, D_MODEL), pb=(SGU_W, D_MODEL), wout=(D_MODEL, D_MODEL))
_HALF = dict(winT=(N_COLS // 8, D_MODEL), pa=(ATTN_W // 2, D_MODEL // 4), pb=(SGU_W // 2, D_MODEL // 4),
             wout=(D_MODEL // 8, D_MODEL))
SMALL_ROWS = 1280


def _region(name, ref, k, h):
    hr, hc = _HALF[name]
    if name in ("pa", "pb"):
        return ref.at[pl.ds(h * hr, hr), pl.ds(k * hc, hc)]
    return ref.at[pl.ds(k * 2 * hr + h * hr, hr), :]


def _place():
    x, y, c = lax.axis_index("x"), lax.axis_index("y"), lax.axis_index("c")
    return x, y, c, 2 * x + y


def _as_chip(j, fn):
    for jj in range(N_CHIPS):
        pl.when(j == jj)(functools.partial(fn, jj))


def _remote(src, dst, send_sem, recv_sem, to):
    return pltpu.make_async_remote_copy(src_ref=src, dst_ref=dst, send_sem=send_sem, recv_sem=recv_sem,
                                        device_id=to, device_id_type=MESH)


def _core_of(k, c):
    return (k // 2, k % 2, c)


def _others(jj):
    return [k for k in range(N_CHIPS) if k != jj]


def _start_all(cps):
    for cp in cps:
        cp.start()


def _wait_all(cps):
    for cp in cps:
        cp.wait()


def gather_over_chips(names, shards):
    n = len(names)

    def own(cin, cout, sems, jj):
        x, y, c, _ = _place()
        cps = []
        for i, name in enumerate(names):
            hr, hc = _HALF[name]
            place = (cout[i].at[:, pl.ds(jj * hc, hc)] if name in ("pa", "pb")
                     else cout[i].at[pl.ds(jj * 2 * hr, 2 * hr), :])
            cps.append(_remote(cin[i], place, sems[2].at[i], sems[3].at[i], (x, y, 1 - c)))
        return cps

    def copies(cin, cout, sems, jj, c):
        out = []
        for i, name in enumerate(names):
            hr = _HALF[name][0]
            for k in _others(jj):
                out.append(_remote(cin[i].at[pl.ds(c * hr, hr)], _region(name, cout[i], jj, c),
                                   sems[0].at[4 * i + k], sems[1].at[4 * i + jj], _core_of(k, c)))
        return out

    def start(cin, cout, sems):
        _, _, c, j = _place()
        _as_chip(j, lambda jj: _start_all(copies(cin, cout, sems, jj, c) + own(cin, cout, sems, jj)))

    def finish(cin, cout, sems):
        _, _, c, j = _place()

        def run(jj):
            for i, name in enumerate(names):
                for k in _others(jj):
                    land = _region(name, cout[i], k, c)
                    _remote(land, land, sems[0].at[4 * i + k], sems[1].at[4 * i + k], _core_of(k, c)).wait_recv()
            for cp in copies(cin, cout, sems, jj, c):
                cp.wait_send()
            _wait_all(own(cin, cout, sems, jj))

        _as_chip(j, run)

    shapes = [jax.ShapeDtypeStruct(_WHOLE[nm], s.dtype) for nm, s in zip(names, shards)]
    return _Comm(shards, shapes, [_DMA((4 * n,)), _DMA((4 * n,)), _DMA((n,)), _DMA((n,))], start, finish)


def gather_weights(names, shards, turn_step=None):
    n = len(names)
    chips = gather_over_chips(names, shards)
    pair_start, pair_finish = _pair_phase(names)

    def turn(cin, cout, sems):
        chips.finish(cin, cout, sems[:4])
        pair_start(cin, cout, sems[4:])

    return _Comm(shards, chips.out_shapes, chips.sems + [_DMA((4 * n,)), _DMA((4 * n,))], chips.start,
                 lambda cin, cout, sems: pair_finish(cin, cout, sems[4:]), turn=turn, turn_step=turn_step)


def _pair_phase(names):
    def start(cin, cout, sems):
        x, y, c, j = _place()

        def run(jj):
            for i, name in enumerate(names):
                for k in _others(jj):
                    land = _region(name, cout[i], k, c)
                    _remote(land, land, sems[0].at[4 * i + k], sems[1].at[4 * i + k], (x, y, 1 - c)).start()

        _as_chip(j, run)

    def finish(cin, cout, sems):
        x, y, c, j = _place()

        def run(jj):
            for i, name in enumerate(names):
                for k in _others(jj):
                    theirs = _region(name, cout[i], k, 1 - c)
                    _remote(theirs, theirs, sems[0].at[4 * i + k], sems[1].at[4 * i + k], (x, y, 1 - c)).wait_recv()
            for i, name in enumerate(names):
                for k in _others(jj):
                    land = _region(name, cout[i], k, c)
                    _remote(land, land, sems[0].at[4 * i + k], sems[1].at[4 * i + k], (x, y, 1 - c)).wait_send()

        _as_chip(j, run)

    return start, finish


def pair_send_halves(names, parts, small=None, extra=()):
    n = len(names)
    hs = SMALL_ROWS // 2
    first_extra = n + (small is not None)

    def copies(cin, cout, sems):
        x, y, c, _ = _place()
        sib = (x, y, 1 - c)
        cps = []
        for i, name in enumerate(names):
            for k in range(N_CHIPS):
                cps.append(_remote(_region(name, cin[i], k, 1 - c), cout[i].at[k], sems[0].at[4 * i + k],
                                   sems[1].at[4 * i + k], sib))
        if small is not None:
            cps.append(_remote(cin[n].at[pl.ds((1 - c) * hs, hs)], cout[n], sems[0].at[4 * n], sems[1].at[4 * n], sib))
        for e in range(len(extra)):
            cps.append(_remote(cin[first_extra + e], cout[first_extra + e], sems[0].at[4 * n + 1 + e],
                               sems[1].at[4 * n + 1 + e], sib))
        return cps

    start = lambda cin, cout, sems: _start_all(copies(cin, cout, sems))
    finish = lambda cin, cout, sems: _wait_all(copies(cin, cout, sems))
    shapes = [jax.ShapeDtypeStruct((N_CHIPS,) + _HALF[nm], F32) for nm in names]
    ins = list(parts)
    if small is not None:
        shapes.append(jax.ShapeDtypeStruct((hs, 128), F32))
        ins.append(small)
    shapes += [jax.ShapeDtypeStruct(a.shape, a.dtype) for a in extra]
    ins += list(extra)
    n_sem = 4 * n + 1 + len(extra)
    return _Comm(ins, shapes, [_DMA((n_sem,)), _DMA((n_sem,))], start, finish)


def chip_exchange(sums, small=None):
    n = len(sums)

    def copies(cin, cout, sems, jj, c):
        out = []
        for k in _others(jj):
            for i in range(n):
                out.append(_remote(cin[i].at[k], cout[i].at[jj], sems[0].at[4 * i + k], sems[1].at[4 * i + jj], _core_of(k, c)))
            if small is not None:
                out.append(_remote(cin[n], cout[n].at[jj], sems[0].at[4 * n + k], sems[1].at[4 * n + jj], _core_of(k, c)))
        return out

    def start(cin, cout, sems):
        _, _, c, j = _place()

        def run(jj):
            if small is not None:
                pltpu.make_async_copy(cin[n], cout[n].at[jj], sems[2]).start()
            for cp in copies(cin, cout, sems, jj, c):
                cp.start()

        _as_chip(j, run)

    def finish(cin, cout, sems):
        _, _, c, j = _place()

        def run(jj):
            for k in _others(jj):
                for i in range(n):
                    _remote(cin[i].at[k], cout[i].at[k], sems[0].at[4 * i + k], sems[1].at[4 * i + k], _core_of(k, c)).wait_recv()
                if small is not None:
                    _remote(cin[n], cout[n].at[k], sems[0].at[4 * n + k], sems[1].at[4 * n + k], _core_of(k, c)).wait_recv()
            for cp in copies(cin, cout, sems, jj, c):
                cp.wait_send()
            if small is not None:
                pltpu.make_async_copy(cin[n], cout[n].at[jj], sems[2]).wait()

        _as_chip(j, run)

    shapes = [jax.ShapeDtypeStruct(a.shape, a.dtype) for a in sums]
    ins = list(sums)
    if small is not None:
        shapes.append(jax.ShapeDtypeStruct((N_CHIPS,) + small.shape, small.dtype))
        ins.append(small)
    return _Comm(ins, shapes, [_DMA((4 * n + 4,)), _DMA((4 * n + 4,)), _DMA(())], start, finish)


def pair_send_totals(arrs):
    n = len(arrs)

    def copies(cin, cout, sems):
        x, y, c, _ = _place()
        return [_remote(cin[i], cout[i], sems[0].at[i], sems[1].at[i], (x, y, 1 - c)) for i in range(n)]

    start = lambda cin, cout, sems: _start_all(copies(cin, cout, sems))
    finish = lambda cin, cout, sems: _wait_all(copies(cin, cout, sems))
    return _Comm(arrs, [jax.ShapeDtypeStruct(a.shape, a.dtype) for a in arrs], [_DMA((n,)), _DMA((n,))], start, finish)


def _pin(a):
    return pltpu.with_memory_space_constraint(a, pltpu.HBM)


def _plain_params(n_axes):
    return pltpu.CompilerParams(dimension_semantics=("arbitrary",) * n_axes, vmem_limit_bytes=V7X_VMEM_BYTES // 2)


def add_own_halves(names, parts, landed, core, name):
    n = len(names)

    def body(core_ref, *refs):
        for i in range(n):
            refs[2 * n + i][0] = (refs[i][...] + refs[n + i][0]).astype(MXU_DTYPE)

    in_specs, out_specs = [], []
    for nm in names:
        hr, hc = _HALF[nm]
        if nm in ("pa", "pb"):
            in_specs.append(pl.BlockSpec((hr // 2, hc), lambda k, s, cr: (cr[0] * 2 + s, k)))
        else:
            in_specs.append(pl.BlockSpec((hr // 2, hc), lambda k, s, cr: (k * 4 + cr[0] * 2 + s, 0)))
    for nm in names:
        hr, hc = _HALF[nm]
        in_specs.append(pl.BlockSpec((1, hr // 2, hc), lambda k, s, cr: (k, s, 0)))
        out_specs.append(pl.BlockSpec((1, hr // 2, hc), lambda k, s, cr: (k, s, 0)))
    return pl.pallas_call(
        body, name=name,
        grid_spec=pltpu.PrefetchScalarGridSpec(num_scalar_prefetch=1, grid=(N_CHIPS, 2), in_specs=in_specs, out_specs=out_specs),
        out_shape=[pltpu.HBM((N_CHIPS,) + _HALF[nm], MXU_DTYPE) for nm in names],
        compiler_params=_plain_params(2),
    )(core, *[_pin(a) for a in parts], *[_pin(a) for a in landed])


def add_small_half(small, landed, core):
    hs = SMALL_ROWS // 2

    def body(core_ref, a_ref, b_ref, o_ref):
        o_ref[...] = a_ref[...] + b_ref[...]

    return pl.pallas_call(
        body, name="add_small_half",
        grid_spec=pltpu.PrefetchScalarGridSpec(
            num_scalar_prefetch=1, grid=(1,),
            in_specs=[pl.BlockSpec((hs, 128), lambda s, cr: (cr[0], 0)), pl.BlockSpec((hs, 128), lambda s, cr: (0, 0))],
            out_specs=pl.BlockSpec((hs, 128), lambda s, cr: (0, 0))),
        out_shape=jax.ShapeDtypeStruct((hs, 128), F32),
    )(core, small, landed)


def sum_chips(sums, landed, chips, name, small_landed=None):
    n = len(sums)

    def body(chips_ref, *refs):
        ins, outs = refs[:4 * n + (small_landed is not None)], refs[4 * n + (small_landed is not None):]
        for i in range(n):
            tot = ins[4 * i][0].astype(F32)
            for r in range(1, N_CHIPS):
                tot = tot + ins[4 * i + r][0].astype(F32)
            outs[i][...] = tot
        if small_landed is not None:
            @pl.when(pl.program_id(0) == 0)
            def _():
                sm = ins[4 * n]
                outs[n][...] = ((sm[0] + sm[1]) + sm[2]) + sm[3]

    in_specs, out_specs, operands, shapes = [], [], [], []
    for a, l in zip(sums, landed):
        _, hr, hc = a.shape
        for r in range(N_CHIPS):
            in_specs.append(pl.BlockSpec((1, hr // 2, hc), functools.partial(lambda s, ch, r: (ch[r], s, 0), r=r)))
            operands.append(a if r == 0 else l)
        out_specs.append(pl.BlockSpec((hr // 2, hc), lambda s, ch: (s, 0)))
        shapes.append(jax.ShapeDtypeStruct((hr, hc), F32))
    if small_landed is not None:
        in_specs.append(pl.BlockSpec(small_landed.shape, lambda s, ch: (0, 0, 0)))
        out_specs.append(pl.BlockSpec(small_landed.shape[1:], lambda s, ch: (0, 0)))
        operands.append(small_landed)
        shapes.append(jax.ShapeDtypeStruct(small_landed.shape[1:], F32))
    return pl.pallas_call(
        body, name=name,
        grid_spec=pltpu.PrefetchScalarGridSpec(num_scalar_prefetch=1, grid=(2,), in_specs=in_specs, out_specs=out_specs),
        out_shape=[pltpu.HBM(s.shape, s.dtype) for s in shapes], compiler_params=_plain_params(1),
    )(chips, *[_pin(a) for a in operands])


N_STEPS = 8


def _adam_update(w, g, m, v):
    m = ADAM_B1 * m + (1.0 - ADAM_B1) * g
    v = ADAM_B2 * v + (1.0 - ADAM_B2) * (g * g)
    m_hat = m * (1.0 / (1.0 - ADAM_B1 ** ADAM_STEP))
    v_hat = v * (1.0 / (1.0 - ADAM_B2 ** ADAM_STEP))
    return -ADAM_LR * (m_hat / (jnp.sqrt(v_hat) + ADAM_EPS) + ADAM_WD * w), m, v


def adamw_big(ws, g_halves, ms, vs, core):
    n = len(ws)
    per_layer = N_STEPS // DEPTH
    per_half = per_layer // 2

    def body(core_ref, *refs):
        s = pl.program_id(0)
        first_layer = s < per_layer
        mine = ((s % per_layer) // per_half) == core_ref[0]
        for i in range(n):
            w, m, v = (refs[k * n + i][...] for k in range(3))
            g00, g01, g10, g11 = (refs[3 * n + 4 * i + k][...] for k in range(4))
            g = jnp.where(first_layer, jnp.where(mine, g00, g01), jnp.where(mine, g10, g11))[None]
            delta, m, v = _adam_update(w, g, m, v)
            for k, val in enumerate((g, delta, m, v)):
                refs[7 * n + k * n + i][...] = val

    whole = lambda a: pl.BlockSpec((1, a.shape[1] // per_layer, a.shape[2]),
                                   lambda s, cr: (s // per_layer, s % per_layer, 0))

    def half(a, layer, own):
        def index(s, cr):
            first = layer * per_layer + per_half * jnp.where(own, cr[0], 1 - cr[0])
            return (jnp.clip(s - first, 0, per_half - 1), 0)
        return pl.BlockSpec((a.shape[1] // per_layer, a.shape[2]), index)

    g_specs, g_ops = [], []
    for a, halves in zip(ws, g_halves):
        g_specs += [half(a, 0, True), half(a, 0, False), half(a, 1, True), half(a, 1, False)]
        g_ops += [_pin(a) for a in halves]
    outs = pl.pallas_call(
        body, name="adamw_big",
        grid_spec=pltpu.PrefetchScalarGridSpec(
            num_scalar_prefetch=1, grid=(N_STEPS,), in_specs=[whole(a) for a in ws] * 3 + g_specs,
            out_specs=[whole(a) for a in ws] * 4),
        out_shape=[pltpu.HBM(a.shape, F32) for a in ws] * 4,
        compiler_params=_layer_params(),
    )(core, *[_pin(a) for a in (*ws, *ms, *vs)], *g_ops)
    return outs[:n], outs[n:2 * n], outs[2 * n:3 * n], outs[3 * n:]


def adamw_small(ws, gs, ms, vs):
    n = len(ws)

    def body(*refs):
        for i in range(n):
            outs = _adam_update(*(refs[k * n + i][...] for k in range(4)))
            for k, val in enumerate(outs):
                refs[4 * n + k * n + i][...] = val

    specs = [_const_block(a.shape) for a in ws]
    outs = pl.pallas_call(
        body, name="adamw_small", grid=(1,), in_specs=specs * 4, out_specs=specs * 3,
        out_shape=[jax.ShapeDtypeStruct(a.shape, F32) for a in ws] * 3,
        compiler_params=_plain_params(1),
    )(*ws, *gs, *ms, *vs)
    return outs[:n], outs[n:2 * n], outs[2 * n:]


_SMALL = (("w_s", (DEPTH, SGU_G, BLK, BLK)), ("ln_in_g", (D_MODEL,)), ("ln_in_b", (D_MODEL,)), ("b_in", (DEPTH, N_COLS)),
          ("sinks", (DEPTH, 8)), ("vn_g", (DEPTH, SGU_W)), ("vn_b", (DEPTH, SGU_W)),
          ("b_s", (DEPTH, SGU_G, BLK)), ("b_out", (DEPTH, D_MODEL)), ("ln_g", (DEPTH, D_MODEL)), ("ln_b", (DEPTH, D_MODEL)))
_BIG = ("w_in", "p_a", "p_b", "w_out")


def _size(shape):
    n = 1
    for d in shape:
        n *= d
    return n


def _pack_small(vals, last_row=None):
    rows = []
    for name, shape in _SMALL:
        v = vals[name]
        if _size(shape) % 128:
            v = jnp.pad(v.reshape((1, -1)), ((0, 0), (0, (-_size(shape)) % 128)))
        rows.append(v.reshape((-1, 128)))
    used = sum(r.shape[0] for r in rows)
    tail = jnp.zeros((SMALL_ROWS - used, 128), F32)
    if last_row is not None:
        tail = tail.at[-1, 0].set(last_row)
    return jnp.concatenate([rows[0], jnp.concatenate(rows[1:] + [tail], axis=0)], axis=0)


def _unpack_small(packed):
    out, pos = {}, 0
    for name, shape in _SMALL:
        n = -(-_size(shape) // 128)
        rows = packed[pos:pos + n]
        out[name] = (rows.reshape((-1,))[:_size(shape)] if _size(shape) % 128 else rows).reshape(shape)
        pos += n
    return out


def kernel(x, ln_in_g, ln_in_b, w_in, b_in, sinks, vn_g, vn_b, w_s, b_s, p_a, p_b, w_out, b_out, ln_g, ln_b, loss_target, m_ln_in_g, m_ln_in_b, m_w_in, m_b_in, m_sinks, m_vn_g, m_vn_b, m_w_s, m_b_s, m_p_a, m_p_b, m_w_out, m_b_out, m_ln_g, m_ln_b, v_ln_in_g, v_ln_in_b, v_w_in, v_b_in, v_sinks, v_vn_g, v_vn_b, v_w_s, v_b_s, v_p_a, v_p_b, v_w_out, v_b_out, v_ln_g, v_ln_b):
    weights = dict(ln_in_g=ln_in_g, ln_in_b=ln_in_b, w_in=w_in, b_in=b_in, sinks=sinks, vn_g=vn_g, vn_b=vn_b, w_s=w_s,
                   b_s=b_s, p_a=p_a, p_b=p_b, w_out=w_out, b_out=b_out, ln_g=ln_g, ln_b=ln_b)
    mom1 = dict(ln_in_g=m_ln_in_g, ln_in_b=m_ln_in_b, w_in=m_w_in, b_in=m_b_in, sinks=m_sinks, vn_g=m_vn_g, vn_b=m_vn_b,
                w_s=m_w_s, b_s=m_b_s, p_a=m_p_a, p_b=m_p_b, w_out=m_w_out, b_out=m_b_out, ln_g=m_ln_g, ln_b=m_ln_b)
    mom2 = dict(ln_in_g=v_ln_in_g, ln_in_b=v_ln_in_b, w_in=v_w_in, b_in=v_b_in, sinks=v_sinks, vn_g=v_vn_g, vn_b=v_vn_b,
                w_s=v_w_s, b_s=v_b_s, p_a=v_p_a, p_b=v_p_b, w_out=v_w_out, b_out=v_b_out, ln_g=v_ln_g, ln_b=v_ln_b)
    n_seq, S, _ = x.shape
    T = n_seq * S
    c = lax.axis_index("c")
    j = 2 * lax.axis_index("x") + lax.axis_index("y")
    core = c.astype(jnp.int32).reshape((1,))
    chips = jnp.stack([j] + [r + (r >= j) for r in range(N_CHIPS - 1)]).astype(jnp.int32)
    names = list(_PIECES)
    xt, tt = x.reshape((T, D_MODEL)), loss_target.reshape((T, D_MODEL))

    tview = lambda d: dict(d, w_in=jnp.swapaxes(d["w_in"], 1, 2))
    weights_t, mom1_t, mom2_t = tview(weights), tview(mom1), tview(mom2)

    def own_shards(l):
        return [weights_t[n][l].astype(MXU_DTYPE) for n in _BIG]

    def layer_weights(l, gathered):
        d = dict(zip(_PIECES, gathered))
        d.update(paT=d["pa"].T, pbT=d["pb"].T, woutT=d["wout"].T)
        d.update(bin=b_in[l][None], sinks=sinks[l], vng=vn_g[l][None], vnb=vn_b[l][None], ws=w_s[l],
                 bs=jnp.repeat(b_s[l].T, 128, axis=1), bout=b_out[l][None], lng=ln_g[l][None], lnb=ln_b[l][None])
        return d

    own0, own1 = own_shards(0), own_shards(1)
    x0, g0 = input_norm_forward(xt, ln_in_g[None], ln_in_b[None], "ln_in_fwd", comm=gather_weights(names[:1], own0[:1]))
    n_proj = T // ROWS_PROJ
    h0, g1 = project(x0, g0[0], b_in[0][None], "layer0_proj",
                     comm=gather_weights(names[1:] + names[:1], own0[1:] + own1[:1], turn_step=(3 * n_proj) // 4))
    lw0 = layer_weights(0, g0 + g1[:len(names) - 1])
    x1, h1, g2 = layer_forward_and_projection(x0, h0, lw0, g1[-1], b_in[1][None], S, "layer0_fwd_layer1_proj",
                                              comm=gather_weights(names[1:], own1[1:], turn_step=T // ROWS // 2))
    lw1 = layer_weights(1, g1[-1:] + g2)

    dres1, dh1, acc1, _ = layer_backward(x1, h1, tt, lw1, S, True, "layer1_bwd")
    dx1, dwinT1, _ = input_grads(dres1, dh1, x1, lw1["winT"], "layer1_dx_dwin")
    parts1 = [dwinT1, acc1["dpa"], acc1["dpb"], acc1["dwout"]]
    dres0, dh0, acc0, landed1 = layer_backward(x0, h0, dx1, lw0, S, False, "layer0_bwd", comm=pair_send_halves(names, parts1))
    sums1 = add_own_halves(names, parts1, landed1, core, "add_own_halves1")
    dx0, dwinT0, from_chips1 = input_grads(dres0, dh0, x0, lw0["winT"], "layer0_dx_dwin", comm=chip_exchange(sums1))
    totals1 = sum_chips(sums1, from_chips1, chips, "sum_chips1")
    grad_x, d_ln_in_g, d_ln_in_b, _ = input_norm_backward(xt, dx0, ln_in_g, "ln_in_bwd")

    gl = [acc0, acc1]
    per_layer = lambda key, pick: jnp.stack([pick(gl[l][key]) for l in range(DEPTH)])
    row0 = lambda key: per_layer(key, lambda a: a[0])
    small_part = _pack_small(dict(
        ln_in_g=d_ln_in_g[0], ln_in_b=d_ln_in_b[0], b_in=row0("dbin"), sinks=per_layer("dsink", lambda a: a[:, 0]),
        vn_g=row0("dvng"), vn_b=row0("dvnb"), w_s=per_layer("dws", lambda a: a),
        b_s=per_layer("dbs", lambda a: a[:, ::128].T), b_out=row0("dbout"), ln_g=row0("dlng"), ln_b=row0("dlnb")),
        last_row=acc1["loss"][0, 0])
    parts0 = [dwinT0, acc0["dpa"], acc0["dpb"], acc0["dwout"]]
    first = _run_comm(pair_send_halves(names, parts0, small_part, extra=totals1), "pair_send_halves0")
    landed0, small_landed, sib_totals1 = first[:4], first[4], first[5:]
    sums0 = add_own_halves(names, parts0, landed0, core, "add_own_halves0")
    small_sum = add_small_half(small_part, small_landed, core)
    *from_chips0, small_from_chips = _run_comm(chip_exchange(sums0, small_sum), "chip_exchange0")
    *totals0, small_total = sum_chips(sums0, from_chips0, chips, "sum_chips0", small_landed=small_from_chips)
    *sib_totals0, sib_small = _run_comm(pair_send_totals(totals0 + [small_total]), "pair_send_totals0")

    def both_halves(mine, theirs):
        return jnp.where(c == 0, jnp.concatenate([mine, theirs], axis=0), jnp.concatenate([theirs, mine], axis=0))

    g_halves = [(totals0[pi], sib_totals0[pi], totals1[pi], sib_totals1[pi]) for pi in range(len(_BIG))]
    g_small = both_halves(small_total, sib_small)
    loss = g_small[-1, 0]

    big = adamw_big([weights_t[n] for n in _BIG], g_halves, [mom1_t[n] for n in _BIG], [mom2_t[n] for n in _BIG], core)
    small_names = [n for n, _ in _SMALL]
    as2d = lambda a: a.reshape((1, -1)) if a.ndim == 1 else a
    g_named = _unpack_small(g_small)
    small = adamw_small(*[[as2d(d[n]) for n in small_names] for d in (weights, g_named, mom1, mom2)])

    def named(big_outs, small_outs):
        d = dict(zip(_BIG, big_outs))
        d["w_in"] = jnp.swapaxes(d["w_in"], 1, 2)
        d.update({n: o.reshape(weights[n].shape) for n, o in zip(small_names, small_outs)})
        return d

    order = ("ln_in_g", "ln_in_b", "w_in", "b_in", "sinks", "vn_g", "vn_b", "w_s", "b_s", "p_a", "p_b", "w_out", "b_out",
             "ln_g", "ln_b")
    res = [loss, grad_x.reshape(x.shape)]
    groups = [named(big[0], [g_named[n] for n in small_names])] + [named(big[k + 1], small[k]) for k in range(3)]
    for group in groups:
        res.extend(group[n] for n in order)
    return tuple(res)
```

```python
import functools

import jax
import jax.numpy as jnp
from jax import lax
from jax.experimental import pallas as pl
from jax.experimental.pallas import tpu as pltpu

F32 = jnp.float32
MXU_DTYPE = jnp.bfloat16

D_MODEL = 1024
DEPTH = 2
HEAD_DIM = 64
ATTN_W = 512
KV_W = 128
BLK = 128
SGU_W = 512
SGU_G = 4
N_COLS = 4864
C_Q, C_K, C_V, C_GA, C_UB, C_VB, C_GB, C_RA, C_RB = 0, 512, 640, 768, 1280, 1792, 2304, 2816, 3840
ALPHA = (2.0 * DEPTH) ** 0.25
LN_EPS = 1e-5
SCALE = HEAD_DIM ** -0.5
NEG = float(jnp.finfo(jnp.float32).min)
GELU_C = 0.7978845608028654
GELU_A = 0.044715

ADAM_LR, ADAM_B1, ADAM_B2, ADAM_EPS, ADAM_WD, ADAM_STEP = 0.001, 0.9, 0.999, 1e-08, 0.01, 10

V7X_VMEM_BYTES = 64 * 1024 * 1024
V7X_VMEM_RESERVE = 3 * 1024 * 1024
ROWS = 256
ROWS_BWD = 256
ROWS_PROJ = 512

MESH = pl.DeviceIdType.MESH
N_CHIPS = 4


def _mm(a, b):
    return jnp.dot(a.astype(MXU_DTYPE), b.astype(MXU_DTYPE), preferred_element_type=F32)


def _mm_nt(a, b):
    return lax.dot_general(a.astype(MXU_DTYPE), b.astype(MXU_DTYPE), (((1,), (1,)), ((), ())),
                           preferred_element_type=F32)


def _mm_tn(a, b):
    return lax.dot_general(a.astype(MXU_DTYPE), b.astype(MXU_DTYPE), (((0,), (0,)), ((), ())),
                           preferred_element_type=F32)


def _sigmoid(x):
    return 0.5 * jnp.tanh(0.5 * x) + 0.5


def _gelu_parts(x):
    x2 = x * x
    u = 0.5 * jnp.tanh(x * (GELU_C + (GELU_C * GELU_A) * x2)) + 0.5
    dg = u * (1.0 + x * (1.0 - u) * (2.0 * GELU_C + (6.0 * GELU_C * GELU_A) * x2))
    return x * u, dg


def _ln_fwd(z, g, b):
    mu = jnp.mean(z, axis=-1, keepdims=True)
    zc = z - mu
    var = jnp.mean(zc * zc, axis=-1, keepdims=True)
    rstd = lax.rsqrt(var + LN_EPS)
    zhat = zc * rstd
    return zhat * g + b, zhat, rstd


def _ln_bwd(dy, zhat, rstd, g):
    dzh = dy * g
    m1 = jnp.mean(dzh, axis=-1, keepdims=True)
    m2 = jnp.mean(dzh * zhat, axis=-1, keepdims=True)
    return rstd * (dzh - m1 - zhat * m2)


def _colsum(v):
    return jnp.sum(v, axis=0, keepdims=True)


def _lane_lo(rows):
    return lax.broadcasted_iota(jnp.int32, (rows, 128), 1) < HEAD_DIM


def _dup_head(x2, hk, lo):
    xr = pltpu.roll(x2, HEAD_DIM, 1)
    return jnp.where(lo, x2, xr) if hk == 0 else jnp.where(lo, xr, x2)


def _fold_head(y0, y1, lo):
    f0 = y0 + pltpu.roll(y0, HEAD_DIM, 1)
    f1 = y1 + pltpu.roll(y1, HEAD_DIM, 1)
    return jnp.where(lo, f0, f1)


def _stack_heads(t0, t1, lo):
    z = jnp.zeros_like(t0)
    return jnp.concatenate([jnp.where(lo, t0, z), jnp.where(lo, z, t0),
                            jnp.where(lo, t1, z), jnp.where(lo, z, t1)], axis=0)


def _unstack_heads(o, lo):
    return (jnp.where(lo, o[0:128], o[128:256]), jnp.where(lo, o[256:384], o[384:512]))


def _band_bias(start):
    row = lax.broadcasted_iota(jnp.int32, (4 * BLK, 2 * BLK), 0) & (BLK - 1)
    kpos = lax.broadcasted_iota(jnp.int32, (4 * BLK, 2 * BLK), 1)
    valid = (kpos > row) & (kpos <= row + BLK)
    if start is not None:
        valid = valid & (jnp.logical_not(start) | (kpos >= BLK))
    return jnp.where(valid, 0.0, NEG)


def _attn_probs(s, sink4, bias):
    s = s * SCALE + bias
    r1 = lax.broadcasted_iota(jnp.int32, (4 * BLK, 1), 0)
    sk = jnp.where(r1 < BLK, sink4[0], jnp.where(r1 < 2 * BLK, sink4[1], jnp.where(r1 < 3 * BLK, sink4[2], sink4[3])))
    m = jnp.maximum(jnp.max(s, axis=-1, keepdims=True), sk)
    p = jnp.exp(s - m)
    es = jnp.exp(sk - m)
    inv = 1.0 / (jnp.sum(p, axis=-1, keepdims=True) + es)
    return p * inv, es * inv


_PROJ_CHUNK = 512


def _kv_blocks(kvh, h_s, nb):
    ks = [kvh[:, 0:KV_W]] + [h_s[n * BLK:(n + 1) * BLK, C_K:C_K + KV_W] for n in range(nb)]
    vs = [kvh[:, KV_W:2 * KV_W]] + [h_s[n * BLK:(n + 1) * BLK, C_V:C_V + KV_W] for n in range(nb)]
    return ks, vs


def _attn_operands(h_s, ks, vs, n, hk, lo, lo2):
    rows = slice(n * BLK, (n + 1) * BLK)
    kd = _dup_head(jnp.concatenate([ks[n], ks[n + 1]], axis=0), hk, lo2)
    vd = _dup_head(jnp.concatenate([vs[n], vs[n + 1]], axis=0), hk, lo2)
    c0 = C_Q + 2 * hk * 128
    qs = _stack_heads(h_s[rows, c0:c0 + 128], h_s[rows, c0 + 128:c0 + 256], lo)
    return qs, kd, vd


def _sgu_weights(w):
    tri = (lax.broadcasted_iota(jnp.int32, (BLK, BLK), 0) >= lax.broadcasted_iota(jnp.int32, (BLK, BLK), 1))
    return tri, [jnp.where(tri, w["ws"][g], 0.0) for g in range(SGU_G)]


def _layer_forward(x, kvh, start, w, h_s, attn_s, mix_s, keep, tick=lambda: None):
    R = x.shape[0]
    nb = R // BLK
    lo = _lane_lo(BLK)
    lo2 = _lane_lo(2 * BLK)
    ks, vs = _kv_blocks(kvh, h_s, nb)
    sinks = [w["sinks"][j] for j in range(8)]
    bias_first = _band_bias(start)
    bias_rest = _band_bias(None) if nb > 1 else None
    pairs = [(n, hk) for n in range(nb) for hk in range(2)]
    ops = [_attn_operands(h_s, ks, vs, n, hk, lo, lo2) for n, hk in pairs]
    scores = [_mm_nt(qs, kd) for qs, kd, _ in ops]
    probs_l = [_attn_probs(s, sinks[4 * hk:4 * hk + 4], bias_first if n == 0 else bias_rest)
               for s, (n, hk) in zip(scores, pairs)]
    outs = [_mm(p, vd) for (p, _), (_, _, vd) in zip(probs_l, ops)]
    for o, (n, hk) in zip(outs, pairs):
        rows = slice(n * BLK, (n + 1) * BLK)
        t0, t1 = _unstack_heads(o, lo)
        attn_s[rows, 2 * hk * 128:(2 * hk + 1) * 128] = t0
        attn_s[rows, (2 * hk + 1) * 128:(2 * hk + 2) * 128] = t1
        tick()
    ga = h_s[:, C_GA:C_GA + ATTN_W]
    ya = (attn_s[...] * (ga * _sigmoid(ga))).astype(MXU_DTYPE)
    tick()
    gu, _ = _gelu_parts(h_s[:, C_UB:C_UB + SGU_W])
    tick()
    gv, _ = _gelu_parts(h_s[:, C_VB:C_VB + SGU_W])
    vn, _, _ = _ln_fwd(gv, w["vng"][...], w["vnb"][...])
    tick()
    _, wms = _sgu_weights(w)
    tiles = [(slice(n * BLK, (n + 1) * BLK), g, slice(g * 128, (g + 1) * 128)) for n in range(nb) for g in range(SGU_G)]
    mixes = [_mm(wms[g], vn[rows, cols]) for rows, g, cols in tiles]
    for m, (rows, g, cols) in zip(mixes, tiles):
        mix_s[rows, cols] = m + w["bs"][:, cols]
    gb = h_s[:, C_GB:C_GB + SGU_W]
    yb = (gu * mix_s[...] * (gb * _sigmoid(gb))).astype(MXU_DTYPE)
    tick()
    a = _mm(ya, w["pa"][...])
    b = _mm(yb, w["pb"][...])
    tick()
    merged = (_sigmoid(h_s[:, C_RA:C_RA + D_MODEL]) * a + _sigmoid(h_s[:, C_RB:C_RB + D_MODEL]) * b).astype(MXU_DTYPE)
    tick()
    out = _mm(merged, w["wout"][...]) + w["bout"][...]
    y, zhat, rstd = _ln_fwd(ALPHA * x + out, w["lng"][...], w["lnb"][...])
    if not keep:
        return y, None
    return y, dict(ks=ks, vs=vs, probs=probs_l, ya=ya, yb=yb, a=a, b=b, merged=merged, zhat=zhat, rstd=rstd)


def _dsilu(g, sg):
    return sg * (1.0 + g * (1.0 - sg))


_W_NAMES = ("sinks", "vng", "vnb", "ws", "bs", "pa", "pb", "wout", "bout", "lng", "lnb")
_WT_NAMES = ("paT", "pbT", "woutT")
_ACC_NAMES = ("dwout", "dpa", "dpb", "dbin", "dbout", "dlng", "dlnb", "dvng", "dvnb", "dws", "dbs", "dsink", "loss")


def _make_fwd_proj_body(R, S, nT):
    def body(x_ref, h_ref, kvh_ref, wn_ref, bn_ref, *rest):
        w = dict(zip(_W_NAMES, rest[:len(_W_NAMES)]))
        y_ref, hn_ref, attn_s, mix_s, yprev_s = rest[len(_W_NAMES):]

        @pl.when(pl.program_id(0) == 0)
        def _():
            yprev_s[...] = jnp.zeros(yprev_s.shape, yprev_s.dtype)

        t = jnp.minimum(pl.program_id(0), nT - 1)
        start = (t % (S // R)) == 0
        chunks = [(c0, min(c0 + _PROJ_CHUNK, N_COLS)) for c0 in range(0, N_COLS, _PROJ_CHUNK)]

        def tick():
            if chunks:
                c0, c1 = chunks.pop(0)
                hn_ref[:, c0:c1] = _mm_nt(yprev_s[...], wn_ref[c0:c1, :]) + bn_ref[:, c0:c1]

        y, _ = _layer_forward(x_ref[...], kvh_ref[...], start, w, h_ref, attn_s, mix_s, keep=False, tick=tick)
        while chunks:
            tick()
        y_ref[...] = y
        yprev_s[...] = y.astype(yprev_s.dtype)
    return body


def _make_bwd_body(R, S, nT, is_last):
    nb = R // BLK
    names = _W_NAMES + _WT_NAMES
    nw = len(names)

    def body(x_ref, h_s, kvh_ref, aux_ref, *rest):
        w = dict(zip(names, rest[:nw]))
        dres_ref, dh_ref = rest[nw:nw + 2]
        acc = dict(zip(_ACC_NAMES, rest[nw + 2:nw + 2 + len(_ACC_NAMES)]))
        attn_s, mix_s, ckv_s = rest[nw + 2 + len(_ACC_NAMES):]
        i = pl.program_id(0)
        start = ((nT - 1 - i) % (S // R)) == 0
        lo = _lane_lo(BLK)
        lo2 = _lane_lo(2 * BLK)

        @pl.when(i == 0)
        def _():
            for name in _ACC_NAMES:
                acc[name][...] = jnp.zeros(acc[name].shape, F32)
            ckv_s[...] = jnp.zeros(ckv_s.shape, F32)

        x = x_ref[...]
        y, sv = _layer_forward(x, kvh_ref[...], start, w, h_s, attn_s, mix_s, keep=True)
        if is_last:
            diff = y - aux_ref[...]
            part = 0.5 * jnp.sum(jnp.mean(diff * diff, axis=-1, keepdims=True), axis=0, keepdims=True)
            acc["loss"][...] += jnp.broadcast_to(part, acc["loss"].shape)
            dy = diff * (1.0 / D_MODEL)
        else:
            dy = aux_ref[...]

        def put(c0, val, rows=slice(None)):
            width = val.shape[1]
            dh_ref[rows, c0:c0 + width] = val.astype(dh_ref.dtype)
            acc["dbin"][:, c0:c0 + width] += _colsum(val)

        acc["dlng"][...] += _colsum(dy * sv["zhat"])
        acc["dlnb"][...] += _colsum(dy)
        dz = _ln_bwd(dy, sv["zhat"], sv["rstd"], w["lng"][...])
        dres_ref[...] = ALPHA * dz
        acc["dbout"][...] += _colsum(dz)
        acc["dwout"][...] += _mm_tn(sv["merged"], dz)
        dmerged = _mm(dz, w["woutT"][...])
        sa = _sigmoid(h_s[:, C_RA:C_RA + D_MODEL])
        da = dmerged * sa
        put(C_RA, da * sv["a"] * (1.0 - sa))
        sb = _sigmoid(h_s[:, C_RB:C_RB + D_MODEL])
        db = dmerged * sb
        put(C_RB, db * sv["b"] * (1.0 - sb))
        acc["dpa"][...] += _mm_tn(sv["ya"], da)
        acc["dpb"][...] += _mm_tn(sv["yb"], db)
        dya = _mm(da, w["paT"][...])
        dyb = _mm(db, w["pbT"][...])
        ga = h_s[:, C_GA:C_GA + ATTN_W]
        sga = _sigmoid(ga)
        put(C_GA, dya * attn_s[...] * _dsilu(ga, sga))
        attn_s[...] = dya * (ga * sga)
        ks, vs = sv["ks"], sv["vs"]
        pairs = [(n, hk) for n in range(nb) for hk in range(2)]
        ops = [_attn_operands(h_s, ks, vs, n, hk, lo, lo2) for n, hk in pairs]
        doss = [_stack_heads(attn_s[n * BLK:(n + 1) * BLK, 2 * hk * 128:(2 * hk + 1) * 128],
                             attn_s[n * BLK:(n + 1) * BLK, (2 * hk + 1) * 128:(2 * hk + 2) * 128], lo) for n, hk in pairs]
        dps = [_mm_nt(dos, vd) for dos, (_, _, vd) in zip(doss, ops)]
        gb = h_s[:, C_GB:C_GB + SGU_W]
        sgb = _sigmoid(gb)
        gu, dgu = _gelu_parts(h_s[:, C_UB:C_UB + SGU_W])
        mixed = mix_s[...]
        put(C_GB, dyb * (gu * mixed) * _dsilu(gb, sgb))
        dsgu = dyb * (gb * sgb)
        put(C_UB, dsgu * mixed * dgu)
        dmixed = dsgu * gu
        dsss = []
        for dp, (probs, ps), (n, hk) in zip(dps, sv["probs"], pairs):
            delta = jnp.sum(probs * dp, axis=-1, keepdims=True)
            dsk = -(ps * delta)
            for g in range(4):
                j = 4 * hk + g
                acc["dsink"][j:j + 1, :] += jnp.broadcast_to(_colsum(dsk[g * BLK:(g + 1) * BLK]), (1, 128))
            dsss.append(probs * (dp - delta) * SCALE)
        dqs = [_mm(dss, kd) for dss, (_, kd, _) in zip(dsss, ops)]
        ydks = [_mm_tn(dss, qs) for dss, (qs, _, _) in zip(dsss, ops)]
        ydvs = [_mm_tn(probs, dos) for (probs, _), dos in zip(sv["probs"], doss)]
        gv, dgv = _gelu_parts(h_s[:, C_VB:C_VB + SGU_W])
        vn, vhat, vrstd = _ln_fwd(gv, w["vng"][...], w["vnb"][...])
        tri, wms = _sgu_weights(w)
        tiles = [(slice(n * BLK, (n + 1) * BLK), g, slice(g * 128, (g + 1) * 128)) for n in range(nb) for g in range(SGU_G)]
        dw_parts = [_mm_nt(dmixed[rows, cols], vn[rows, cols]) for rows, g, cols in tiles]
        dvn_parts = [_mm_tn(wms[g], dmixed[rows, cols]) for rows, g, cols in tiles]
        for g in range(SGU_G):
            tot = dw_parts[g]
            for n in range(1, nb):
                tot = tot + dw_parts[n * SGU_G + g]
            acc["dws"][g] += jnp.where(tri, tot, 0.0)
        for part, (rows, g, cols) in zip(dvn_parts, tiles):
            mix_s[rows, cols] = part
        dbs_part = dmixed[0:BLK]
        for n in range(1, nb):
            dbs_part = dbs_part + dmixed[n * BLK:(n + 1) * BLK]
        acc["dbs"][...] += dbs_part
        for dq, (n, hk) in zip(dqs, pairs):
            q0, q1 = _unstack_heads(dq, lo)
            put(C_Q + 2 * hk * 128, q0, slice(n * BLK, (n + 1) * BLK))
            put(C_Q + (2 * hk + 1) * 128, q1, slice(n * BLK, (n + 1) * BLK))
        dks = [jnp.zeros((BLK, KV_W), F32) for _ in range(nb + 1)]
        dvs = [jnp.zeros((BLK, KV_W), F32) for _ in range(nb + 1)]
        for n in range(nb):
            dk2 = _fold_head(ydks[2 * n], ydks[2 * n + 1], lo2)
            dv2 = _fold_head(ydvs[2 * n], ydvs[2 * n + 1], lo2)
            dks[n] = dks[n] + dk2[0:BLK]
            dks[n + 1] = dks[n + 1] + dk2[BLK:2 * BLK]
            dvs[n] = dvs[n] + dv2[0:BLK]
            dvs[n + 1] = dvs[n + 1] + dv2[BLK:2 * BLK]
        dks[nb] = dks[nb] + ckv_s[:, 0:KV_W]
        dvs[nb] = dvs[nb] + ckv_s[:, KV_W:2 * KV_W]
        ckv_s[:, 0:KV_W] = dks[0]
        ckv_s[:, KV_W:2 * KV_W] = dvs[0]
        put(C_K, jnp.concatenate(dks[1:], axis=0))
        put(C_V, jnp.concatenate(dvs[1:], axis=0))
        dvn = mix_s[...]
        acc["dvng"][...] += _colsum(dvn * vhat)
        acc["dvnb"][...] += _colsum(dvn)
        put(C_VB, _ln_bwd(dvn, vhat, vrstd, w["vng"][...]) * dgv)

        @pl.when(i == nT - 1)
        def _():
            for g in range(SGU_G):
                cols = slice(g * 128, (g + 1) * 128)
                tot = jnp.sum(acc["dbs"][:, cols], axis=1, keepdims=True)
                acc["dbs"][:, cols] = jnp.broadcast_to(tot, (BLK, 128))

    return body


def _resident():
    return pl.BlockSpec(memory_space=pltpu.VMEM)


_BIG_RESIDENT = ("pa", "pb", "wout", "paT", "pbT", "woutT", "dwout", "dpa", "dpb")


def _const_block(shape):
    return pl.BlockSpec(tuple(shape), lambda *_: (0,) * len(shape))


def _operand_spec(name, shape):
    if name == "sinks":
        return pl.BlockSpec(memory_space=pltpu.SMEM)
    return _resident() if name in _BIG_RESIDENT else _const_block(shape)


def _weight_specs(names, wts):
    return [_operand_spec(n, wts[n].shape) for n in names]


def _nbytes(a):
    n = jnp.dtype(a.dtype).itemsize
    for d in a.shape:
        n *= d
    return n


def _layer_params(resident=()):
    scoped = V7X_VMEM_BYTES - V7X_VMEM_RESERVE - sum(_nbytes(a) for a in resident)
    return pltpu.CompilerParams(dimension_semantics=("arbitrary",), vmem_limit_bytes=scoped)


class _Comm:
    def __init__(self, ins, out_shapes, sems, start, finish, aliases=None, turn=None, turn_step=None):
        self.ins = [pltpu.with_memory_space_constraint(a, pltpu.HBM) for a in ins]
        self.out_shapes = [pltpu.HBM(s.shape, s.dtype) for s in out_shapes]
        self.sems = list(sems)
        self.start, self.finish, self.aliases = start, finish, dict(aliases or {})
        self.turn, self.turn_step = turn, turn_step


_ANY = pl.BlockSpec(memory_space=pltpu.HBM)
_DMA = pltpu.SemaphoreType.DMA


def _call(body, *, name, grid, in_specs, out_specs, out_shape, scratch_shapes, compiler_params, operands, comm=None):
    if comm is None:
        res = pl.pallas_call(body, name=name, grid=grid, in_specs=in_specs, out_specs=out_specs, out_shape=out_shape,
                             scratch_shapes=scratch_shapes, compiler_params=compiler_params)(*operands)
        return list(res), []
    n_in, n_out, n_scr = len(in_specs), len(out_specs), len(scratch_shapes)
    ci, co = len(comm.ins), len(comm.out_shapes)
    last = grid[0] - 1

    def hosted(*refs):
        ins, cin = refs[:n_in], refs[n_in:n_in + ci]
        p = n_in + ci
        outs, cout = refs[p:p + n_out], refs[p + n_out:p + n_out + co]
        p += n_out + co
        scr, sems = refs[p:p + n_scr], refs[p + n_scr:]
        pl.when(pl.program_id(0) == 0)(lambda: comm.start(cin, cout, sems))
        body(*ins, *outs, *scr)
        if comm.turn is not None:
            at = last if comm.turn_step is None else comm.turn_step
            pl.when(pl.program_id(0) == at)(lambda: comm.turn(cin, cout, sems))
        pl.when(pl.program_id(0) == last)(lambda: comm.finish(cin, cout, sems))

    res = pl.pallas_call(
        hosted, name=name, grid=grid, in_specs=list(in_specs) + [_ANY] * ci, out_specs=list(out_specs) + [_ANY] * co,
        out_shape=list(out_shape) + comm.out_shapes, scratch_shapes=list(scratch_shapes) + comm.sems,
        input_output_aliases={n_in + a: n_out + b for a, b in comm.aliases.items()},
        compiler_params=compiler_params)(*operands, *comm.ins)
    return list(res[:n_out]), list(res[n_out:])


def _run_comm(comm, name):
    ci, co = len(comm.ins), len(comm.out_shapes)

    def body(*refs):
        cin, cout, sems = refs[:ci], refs[ci:ci + co], refs[ci + co:]
        comm.start(cin, cout, sems)
        if comm.turn is not None:
            comm.turn(cin, cout, sems)
        comm.finish(cin, cout, sems)

    return list(pl.pallas_call(body, name=name, in_specs=[_ANY] * ci, out_specs=[_ANY] * co, out_shape=comm.out_shapes,
                               scratch_shapes=comm.sems, input_output_aliases=comm.aliases)(*comm.ins))


def project(x, winT, b, name, comm=None):
    T = x.shape[0]
    R = ROWS_PROJ

    def body(x_ref, w_ref, b_ref, h_ref):
        xb = x_ref[...].astype(MXU_DTYPE)
        for c0 in range(0, N_COLS, _PROJ_CHUNK):
            c1 = min(c0 + _PROJ_CHUNK, N_COLS)
            h_ref[:, c0:c1] = _mm_nt(xb, w_ref[c0:c1, :]) + b_ref[:, c0:c1]

    (h,), extra = _call(
        body, name=name, grid=(T // R,),
        in_specs=[pl.BlockSpec((R, D_MODEL), lambda t: (t, 0)), _resident(), _const_block(b.shape)],
        out_specs=[pl.BlockSpec((R, N_COLS), lambda t: (t, 0))],
        out_shape=[jax.ShapeDtypeStruct((T, N_COLS), F32)], scratch_shapes=[],
        compiler_params=_layer_params([winT]), operands=(x, winT, b), comm=comm)
    return h, extra


_KV_BLOCK = C_K // (2 * KV_W)


def layer_forward_and_projection(x, h, wts, winT_next, b_next, S, name, comm=None):
    T = x.shape[0]
    R = ROWS
    nT = T // R
    npt = R // BLK
    cur = lambda t: jnp.minimum(t, nT - 1)
    (y, hn), extra = _call(
        _make_fwd_proj_body(R, S, nT), name=name, grid=(nT + 1,),
        in_specs=[pl.BlockSpec((R, D_MODEL), lambda t: (cur(t), 0)), pl.BlockSpec((R, N_COLS), lambda t: (cur(t), 0)),
                  pl.BlockSpec((BLK, 2 * KV_W), lambda t: (jnp.maximum(cur(t) * npt - 1, 0), _KV_BLOCK)),
                  _resident(), _const_block(b_next.shape)] + _weight_specs(_W_NAMES, wts),
        out_specs=[pl.BlockSpec((R, D_MODEL), lambda t: (cur(t), 0)),
                   pl.BlockSpec((R, N_COLS), lambda t: (jnp.maximum(t - 1, 0), 0))],
        out_shape=[jax.ShapeDtypeStruct((T, D_MODEL), F32), jax.ShapeDtypeStruct((T, N_COLS), F32)],
        scratch_shapes=[pltpu.VMEM((R, ATTN_W), F32), pltpu.VMEM((R, SGU_W), F32), pltpu.VMEM((R, D_MODEL), MXU_DTYPE)],
        compiler_params=_layer_params([winT_next] + [wts[n] for n in _W_NAMES if n in _BIG_RESIDENT]),
        operands=(x, h, h, winT_next, b_next, *[wts[n] for n in _W_NAMES]), comm=comm)
    return y, hn, extra


_ACC_SHAPES = dict(dwout=(D_MODEL, D_MODEL), dpa=(ATTN_W, D_MODEL), dpb=(SGU_W, D_MODEL), dbin=(1, N_COLS),
                   dbout=(1, D_MODEL), dlng=(1, D_MODEL), dlnb=(1, D_MODEL), dvng=(1, SGU_W), dvnb=(1, SGU_W),
                   dws=(SGU_G, BLK, BLK), dbs=(BLK, SGU_W), dsink=(8, 128), loss=(8, 128))


def layer_backward(x, h, aux, wts, S, is_last, name, comm=None):
    T = x.shape[0]
    R = ROWS_BWD
    nT = T // R
    npt = R // BLK
    rev = lambda i: (nT - 1 - i, 0)
    halo = lambda i: (jnp.maximum((nT - 1 - i) * npt - 1, 0), _KV_BLOCK)
    names = _W_NAMES + _WT_NAMES
    out_shape = ([jax.ShapeDtypeStruct((T, D_MODEL), F32), jax.ShapeDtypeStruct((T, N_COLS), MXU_DTYPE)]
                 + [jax.ShapeDtypeStruct(_ACC_SHAPES[n], F32) for n in _ACC_NAMES])
    outs, extra = _call(
        _make_bwd_body(R, S, nT, is_last), name=name, grid=(nT,),
        in_specs=[pl.BlockSpec((R, D_MODEL), rev), pl.BlockSpec((R, N_COLS), rev), pl.BlockSpec((BLK, 2 * KV_W), halo),
                  pl.BlockSpec((R, D_MODEL), rev)] + _weight_specs(names, wts),
        out_specs=[pl.BlockSpec((R, D_MODEL), rev), pl.BlockSpec((R, N_COLS), rev)]
        + [_operand_spec(n, _ACC_SHAPES[n]) for n in _ACC_NAMES],
        out_shape=out_shape,
        scratch_shapes=[pltpu.VMEM((R, ATTN_W), F32), pltpu.VMEM((R, SGU_W), F32), pltpu.VMEM((BLK, 2 * KV_W), F32)],
        compiler_params=_layer_params([wts[n] for n in names if n in _BIG_RESIDENT]
                                      + [jax.ShapeDtypeStruct(_ACC_SHAPES[n], F32) for n in _ACC_NAMES if n in _BIG_RESIDENT]),
        operands=(x, h, h, aux, *[wts[n] for n in names]), comm=comm)
    return outs[0], outs[1], dict(zip(_ACC_NAMES, outs[2:])), extra


def input_grads(dres, dh, x, winT, name, comm=None):
    T = x.shape[0]
    R = ROWS

    def body(dres_ref, dh_ref, x_ref, winT_ref, dx_ref, dwin_ref):
        @pl.when(pl.program_id(0) == 0)
        def _():
            dwin_ref[...] = jnp.zeros(dwin_ref.shape, F32)

        dh = dh_ref[...]
        dx_ref[...] = dres_ref[...] + jnp.dot(dh, winT_ref[...], preferred_element_type=F32)
        dwin_ref[...] += _mm_tn(dh, x_ref[...])

    row = lambda t: (t, 0)
    dwin_shape = jax.ShapeDtypeStruct((N_COLS, D_MODEL), F32)
    (dx, dwinT), extra = _call(
        body, name=name, grid=(T // R,),
        in_specs=[pl.BlockSpec((R, D_MODEL), row), pl.BlockSpec((R, N_COLS), row), pl.BlockSpec((R, D_MODEL), row),
                  _resident()],
        out_specs=[pl.BlockSpec((R, D_MODEL), row), _resident()],
        out_shape=[jax.ShapeDtypeStruct((T, D_MODEL), F32), dwin_shape], scratch_shapes=[],
        compiler_params=_layer_params([winT, dwin_shape]), operands=(dres, dh, x, winT), comm=comm)
    return dx, dwinT, extra


def input_norm_forward(x, g, b, name, comm=None):
    T = x.shape[0]
    R = ROWS_PROJ

    def body(x_ref, g_ref, b_ref, y_ref):
        y_ref[...] = _ln_fwd(x_ref[...], g_ref[...], b_ref[...])[0]

    row = lambda t: (t, 0)
    (y,), extra = _call(
        body, name=name, grid=(T // R,),
        in_specs=[pl.BlockSpec((R, D_MODEL), row), _const_block(g.shape), _const_block(b.shape)],
        out_specs=[pl.BlockSpec((R, D_MODEL), row)],
        out_shape=[jax.ShapeDtypeStruct((T, D_MODEL), F32)], scratch_shapes=[],
        compiler_params=_layer_params(), operands=(x, g, b), comm=comm)
    return y, extra


def input_norm_backward(x, dy, g, name, comm=None):
    T = x.shape[0]
    R = ROWS_PROJ

    def body(x_ref, dy_ref, g_ref, dx_ref, dg_ref, db_ref):
        @pl.when(pl.program_id(0) == 0)
        def _():
            dg_ref[...] = jnp.zeros(dg_ref.shape, F32)
            db_ref[...] = jnp.zeros(db_ref.shape, F32)

        dy = dy_ref[...]
        _, xhat, rstd = _ln_fwd(x_ref[...], g_ref[...], g_ref[...])
        dx_ref[...] = _ln_bwd(dy, xhat, rstd, g_ref[...])
        dg_ref[...] += _colsum(dy * xhat)
        db_ref[...] += _colsum(dy)

    row = lambda t: (t, 0)
    vec = jax.ShapeDtypeStruct((1, D_MODEL), F32)
    (dx, dg, db), extra = _call(
        body, name=name, grid=(T // R,),
        in_specs=[pl.BlockSpec((R, D_MODEL), row), pl.BlockSpec((R, D_MODEL), row), _const_block(g.shape)],
        out_specs=[pl.BlockSpec((R, D_MODEL), row), _const_block(vec.shape), _const_block(vec.shape)],
        out_shape=[jax.ShapeDtypeStruct((T, D_MODEL), F32), vec, vec], scratch_shapes=[],
        compiler_params=_layer_params(), operands=(x, dy, g), comm=comm)
    return dx, dg, db, extra


_PIECES = ("winT", "pa", "pb", "wout")
_WHOLE = dict(winT=(N_COLS, D_MODEL), pa=(ATTN_W, D_MODEL), pb=(SGU_W, D_MODEL), wout=(D_MODEL, D_MODEL))
_HALF = dict(winT=(N_COLS // 8, D_MODEL), pa=(ATTN_W // 2, D_MODEL // 4), pb=(SGU_W // 2, D_MODEL // 4),
             wout=(D_MODEL // 8, D_MODEL))
SMALL_ROWS = 1280


def _region(name, ref, k, h):
    hr, hc = _HALF[name]
    if name in ("pa", "pb"):
        return ref.at[pl.ds(h * hr, hr), pl.ds(k * hc, hc)]
    return ref.at[pl.ds(k * 2 * hr + h * hr, hr), :]


def _place():
    x, y, c = lax.axis_index("x"), lax.axis_index("y"), lax.axis_index("c")
    return x, y, c, 2 * x + y


def _as_chip(j, fn):
    for jj in range(N_CHIPS):
        pl.when(j == jj)(functools.partial(fn, jj))


def _remote(src, dst, send_sem, recv_sem, to):
    return pltpu.make_async_remote_copy(src_ref=src, dst_ref=dst, send_sem=send_sem, recv_sem=recv_sem,
                                        device_id=to, device_id_type=MESH)


def _core_of(k, c):
    return (k // 2, k % 2, c)


def _others(jj):
    return [k for k in range(N_CHIPS) if k != jj]


def _start_all(cps):
    for cp in cps:
        cp.start()


def _wait_all(cps):
    for cp in cps:
        cp.wait()


def gather_over_chips(names, shards):
    n = len(names)

    def own(cin, cout, sems, jj):
        x, y, c, _ = _place()
        cps = []
        for i, name in enumerate(names):
            hr, hc = _HALF[name]
            place = (cout[i].at[:, pl.ds(jj * hc, hc)] if name in ("pa", "pb")
                     else cout[i].at[pl.ds(jj * 2 * hr, 2 * hr), :])
            cps.append(_remote(cin[i], place, sems[2].at[i], sems[3].at[i], (x, y, 1 - c)))
        return cps

    def copies(cin, cout, sems, jj, c):
        out = []
        for i, name in enumerate(names):
            hr = _HALF[name][0]
            for k in _others(jj):
                out.append(_remote(cin[i].at[pl.ds(c * hr, hr)], _region(name, cout[i], jj, c),
                                   sems[0].at[4 * i + k], sems[1].at[4 * i + jj], _core_of(k, c)))
        return out

    def start(cin, cout, sems):
        _, _, c, j = _place()
        _as_chip(j, lambda jj: _start_all(copies(cin, cout, sems, jj, c) + own(cin, cout, sems, jj)))

    def finish(cin, cout, sems):
        _, _, c, j = _place()

        def run(jj):
            for i, name in enumerate(names):
                for k in _others(jj):
                    land = _region(name, cout[i], k, c)
                    _remote(land, land, sems[0].at[4 * i + k], sems[1].at[4 * i + k], _core_of(k, c)).wait_recv()
            for cp in copies(cin, cout, sems, jj, c):
                cp.wait_send()
            _wait_all(own(cin, cout, sems, jj))

        _as_chip(j, run)

    shapes = [jax.ShapeDtypeStruct(_WHOLE[nm], s.dtype) for nm, s in zip(names, shards)]
    return _Comm(shards, shapes, [_DMA((4 * n,)), _DMA((4 * n,)), _DMA((n,)), _DMA((n,))], start, finish)


def gather_weights(names, shards, turn_step=None):
    n = len(names)
    chips = gather_over_chips(names, shards)
    pair_start, pair_finish = _pair_phase(names)

    def turn(cin, cout, sems):
        chips.finish(cin, cout, sems[:4])
        pair_start(cin, cout, sems[4:])

    return _Comm(shards, chips.out_shapes, chips.sems + [_DMA((4 * n,)), _DMA((4 * n,))], chips.start,
                 lambda cin, cout, sems: pair_finish(cin, cout, sems[4:]), turn=turn, turn_step=turn_step)


def _pair_phase(names):
    def start(cin, cout, sems):
        x, y, c, j = _place()

        def run(jj):
            for i, name in enumerate(names):
                for k in _others(jj):
                    land = _region(name, cout[i], k, c)
                    _remote(land, land, sems[0].at[4 * i + k], sems[1].at[4 * i + k], (x, y, 1 - c)).start()

        _as_chip(j, run)

    def finish(cin, cout, sems):
        x, y, c, j = _place()

        def run(jj):
            for i, name in enumerate(names):
                for k in _others(jj):
                    theirs = _region(name, cout[i], k, 1 - c)
                    _remote(theirs, theirs, sems[0].at[4 * i + k], sems[1].at[4 * i + k], (x, y, 1 - c)).wait_recv()
            for i, name in enumerate(names):
                for k in _others(jj):
                    land = _region(name, cout[i], k, c)
                    _remote(land, land, sems[0].at[4 * i + k], sems[1].at[4 * i + k], (x, y, 1 - c)).wait_send()

        _as_chip(j, run)

    return start, finish


def pair_send_halves(names, parts, small=None, extra=()):
    n = len(names)
    hs = SMALL_ROWS // 2
    first_extra = n + (small is not None)

    def copies(cin, cout, sems):
        x, y, c, _ = _place()
        sib = (x, y, 1 - c)
        cps = []
        for i, name in enumerate(names):
            for k in range(N_CHIPS):
                cps.append(_remote(_region(name, cin[i], k, 1 - c), cout[i].at[k], sems[0].at[4 * i + k],
                                   sems[1].at[4 * i + k], sib))
        if small is not None:
            cps.append(_remote(cin[n].at[pl.ds((1 - c) * hs, hs)], cout[n], sems[0].at[4 * n], sems[1].at[4 * n], sib))
        for e in range(len(extra)):
            cps.append(_remote(cin[first_extra + e], cout[first_extra + e], sems[0].at[4 * n + 1 + e],
                               sems[1].at[4 * n + 1 + e], sib))
        return cps

    start = lambda cin, cout, sems: _start_all(copies(cin, cout, sems))
    finish = lambda cin, cout, sems: _wait_all(copies(cin, cout, sems))
    shapes = [jax.ShapeDtypeStruct((N_CHIPS,) + _HALF[nm], F32) for nm in names]
    ins = list(parts)
    if small is not None:
        shapes.append(jax.ShapeDtypeStruct((hs, 128), F32))
        ins.append(small)
    shapes += [jax.ShapeDtypeStruct(a.shape, a.dtype) for a in extra]
    ins += list(extra)
    n_sem = 4 * n + 1 + len(extra)
    return _Comm(ins, shapes, [_DMA((n_sem,)), _DMA((n_sem,))], start, finish)


def chip_exchange(sums, small=None):
    n = len(sums)

    def copies(cin, cout, sems, jj, c):
        out = []
        for k in _others(jj):
            for i in range(n):
                out.append(_remote(cin[i].at[k], cout[i].at[jj], sems[0].at[4 * i + k], sems[1].at[4 * i + jj], _core_of(k, c)))
            if small is not None:
                out.append(_remote(cin[n], cout[n].at[jj], sems[0].at[4 * n + k], sems[1].at[4 * n + jj], _core_of(k, c)))
        return out

    def start(cin, cout, sems):
        _, _, c, j = _place()

        def run(jj):
            if small is not None:
                pltpu.make_async_copy(cin[n], cout[n].at[jj], sems[2]).start()
            for cp in copies(cin, cout, sems, jj, c):
                cp.start()

        _as_chip(j, run)

    def finish(cin, cout, sems):
        _, _, c, j = _place()

        def run(jj):
            for k in _others(jj):
                for i in range(n):
                    _remote(cin[i].at[k], cout[i].at[k], sems[0].at[4 * i + k], sems[1].at[4 * i + k], _core_of(k, c)).wait_recv()
                if small is not None:
                    _remote(cin[n], cout[n].at[k], sems[0].at[4 * n + k], sems[1].at[4 * n + k], _core_of(k, c)).wait_recv()
            for cp in copies(cin, cout, sems, jj, c):
                cp.wait_send()
            if small is not None:
                pltpu.make_async_copy(cin[n], cout[n].at[jj], sems[2]).wait()

        _as_chip(j, run)

    shapes = [jax.ShapeDtypeStruct(a.shape, a.dtype) for a in sums]
    ins = list(sums)
    if small is not None:
        shapes.append(jax.ShapeDtypeStruct((N_CHIPS,) + small.shape, small.dtype))
        ins.append(small)
    return _Comm(ins, shapes, [_DMA((4 * n + 4,)), _DMA((4 * n + 4,)), _DMA(())], start, finish)


def pair_send_totals(arrs):
    n = len(arrs)

    def copies(cin, cout, sems):
        x, y, c, _ = _place()
        return [_remote(cin[i], cout[i], sems[0].at[i], sems[1].at[i], (x, y, 1 - c)) for i in range(n)]

    start = lambda cin, cout, sems: _start_all(copies(cin, cout, sems))
    finish = lambda cin, cout, sems: _wait_all(copies(cin, cout, sems))
    return _Comm(arrs, [jax.ShapeDtypeStruct(a.shape, a.dtype) for a in arrs], [_DMA((n,)), _DMA((n,))], start, finish)


def _pin(a):
    return pltpu.with_memory_space_constraint(a, pltpu.HBM)


def _plain_params(n_axes):
    return pltpu.CompilerParams(dimension_semantics=("arbitrary",) * n_axes, vmem_limit_bytes=V7X_VMEM_BYTES // 2)


def add_own_halves(names, parts, landed, core, name):
    n = len(names)

    def body(core_ref, *refs):
        for i in range(n):
            refs[2 * n + i][0] = (refs[i][...] + refs[n + i][0]).astype(MXU_DTYPE)

    in_specs, out_specs = [], []
    for nm in names:
        hr, hc = _HALF[nm]
        if nm in ("pa", "pb"):
            in_specs.append(pl.BlockSpec((hr // 2, hc), lambda k, s, cr: (cr[0] * 2 + s, k)))
        else:
            in_specs.append(pl.BlockSpec((hr // 2, hc), lambda k, s, cr: (k * 4 + cr[0] * 2 + s, 0)))
    for nm in names:
        hr, hc = _HALF[nm]
        in_specs.append(pl.BlockSpec((1, hr // 2, hc), lambda k, s, cr: (k, s, 0)))
        out_specs.append(pl.BlockSpec((1, hr // 2, hc), lambda k, s, cr: (k, s, 0)))
    return pl.pallas_call(
        body, name=name,
        grid_spec=pltpu.PrefetchScalarGridSpec(num_scalar_prefetch=1, grid=(N_CHIPS, 2), in_specs=in_specs, out_specs=out_specs),
        out_shape=[pltpu.HBM((N_CHIPS,) + _HALF[nm], MXU_DTYPE) for nm in names],
        compiler_params=_plain_params(2),
    )(core, *[_pin(a) for a in parts], *[_pin(a) for a in landed])


def add_small_half(small, landed, core):
    hs = SMALL_ROWS // 2

    def body(core_ref, a_ref, b_ref, o_ref):
        o_ref[...] = a_ref[...] + b_ref[...]

    return pl.pallas_call(
        body, name="add_small_half",
        grid_spec=pltpu.PrefetchScalarGridSpec(
            num_scalar_prefetch=1, grid=(1,),
            in_specs=[pl.BlockSpec((hs, 128), lambda s, cr: (cr[0], 0)), pl.BlockSpec((hs, 128), lambda s, cr: (0, 0))],
            out_specs=pl.BlockSpec((hs, 128), lambda s, cr: (0, 0))),
        out_shape=jax.ShapeDtypeStruct((hs, 128), F32),
    )(core, small, landed)


def sum_chips(sums, landed, chips, name, small_landed=None):
    n = len(sums)

    def body(chips_ref, *refs):
        ins, outs = refs[:4 * n + (small_landed is not None)], refs[4 * n + (small_landed is not None):]
        for i in range(n):
            tot = ins[4 * i][0].astype(F32)
            for r in range(1, N_CHIPS):
                tot = tot + ins[4 * i + r][0].astype(F32)
            outs[i][...] = tot
        if small_landed is not None:
            @pl.when(pl.program_id(0) == 0)
            def _():
                sm = ins[4 * n]
                outs[n][...] = ((sm[0] + sm[1]) + sm[2]) + sm[3]

    in_specs, out_specs, operands, shapes = [], [], [], []
    for a, l in zip(sums, landed):
        _, hr, hc = a.shape
        for r in range(N_CHIPS):
            in_specs.append(pl.BlockSpec((1, hr // 2, hc), functools.partial(lambda s, ch, r: (ch[r], s, 0), r=r)))
            operands.append(a if r == 0 else l)
        out_specs.append(pl.BlockSpec((hr // 2, hc), lambda s, ch: (s, 0)))
        shapes.append(jax.ShapeDtypeStruct((hr, hc), F32))
    if small_landed is not None:
        in_specs.append(pl.BlockSpec(small_landed.shape, lambda s, ch: (0, 0, 0)))
        out_specs.append(pl.BlockSpec(small_landed.shape[1:], lambda s, ch: (0, 0)))
        operands.append(small_landed)
        shapes.append(jax.ShapeDtypeStruct(small_landed.shape[1:], F32))
    return pl.pallas_call(
        body, name=name,
        grid_spec=pltpu.PrefetchScalarGridSpec(num_scalar_prefetch=1, grid=(2,), in_specs=in_specs, out_specs=out_specs),
        out_shape=[pltpu.HBM(s.shape, s.dtype) for s in shapes], compiler_params=_plain_params(1),
    )(chips, *[_pin(a) for a in operands])


N_STEPS = 8


def _adam_update(w, g, m, v):
    m = ADAM_B1 * m + (1.0 - ADAM_B1) * g
    v = ADAM_B2 * v + (1.0 - ADAM_B2) * (g * g)
    m_hat = m * (1.0 / (1.0 - ADAM_B1 ** ADAM_STEP))
    v_hat = v * (1.0 / (1.0 - ADAM_B2 ** ADAM_STEP))
    return -ADAM_LR * (m_hat / (jnp.sqrt(v_hat) + ADAM_EPS) + ADAM_WD * w), m, v


def adamw_big(ws, g_halves, ms, vs, core):
    n = len(ws)
    per_layer = N_STEPS // DEPTH
    per_half = per_layer // 2

    def body(core_ref, *refs):
        s = pl.program_id(0)
        first_layer = s < per_layer
        mine = ((s % per_layer) // per_half) == core_ref[0]
        for i in range(n):
            w, m, v = (refs[k * n + i][...] for k in range(3))
            g00, g01, g10, g11 = (refs[3 * n + 4 * i + k][...] for k in range(4))
            g = jnp.where(first_layer, jnp.where(mine, g00, g01), jnp.where(mine, g10, g11))[None]
            delta, m, v = _adam_update(w, g, m, v)
            for k, val in enumerate((g, delta, m, v)):
                refs[7 * n + k * n + i][...] = val

    whole = lambda a: pl.BlockSpec((1, a.shape[1] // per_layer, a.shape[2]),
                                   lambda s, cr: (s // per_layer, s % per_layer, 0))

    def half(a, layer, own):
        def index(s, cr):
            first = layer * per_layer + per_half * jnp.where(own, cr[0], 1 - cr[0])
            return (jnp.clip(s - first, 0, per_half - 1), 0)
        return pl.BlockSpec((a.shape[1] // per_layer, a.shape[2]), index)

    g_specs, g_ops = [], []
    for a, halves in zip(ws, g_halves):
        g_specs += [half(a, 0, True), half(a, 0, False), half(a, 1, True), half(a, 1, False)]
        g_ops += [_pin(a) for a in halves]
    outs = pl.pallas_call(
        body, name="adamw_big",
        grid_spec=pltpu.PrefetchScalarGridSpec(
            num_scalar_prefetch=1, grid=(N_STEPS,), in_specs=[whole(a) for a in ws] * 3 + g_specs,
            out_specs=[whole(a) for a in ws] * 4),
        out_shape=[pltpu.HBM(a.shape, F32) for a in ws] * 4,
        compiler_params=_layer_params(),
    )(core, *[_pin(a) for a in (*ws, *ms, *vs)], *g_ops)
    return outs[:n], outs[n:2 * n], outs[2 * n:3 * n], outs[3 * n:]


def adamw_small(ws, gs, ms, vs):
    n = len(ws)

    def body(*refs):
        for i in range(n):
            outs = _adam_update(*(refs[k * n + i][...] for k in range(4)))
            for k, val in enumerate(outs):
                refs[4 * n + k * n + i][...] = val

    specs = [_const_block(a.shape) for a in ws]
    outs = pl.pallas_call(
        body, name="adamw_small", grid=(1,), in_specs=specs * 4, out_specs=specs * 3,
        out_shape=[jax.ShapeDtypeStruct(a.shape, F32) for a in ws] * 3,
        compiler_params=_plain_params(1),
    )(*ws, *gs, *ms, *vs)
    return outs[:n], outs[n:2 * n], outs[2 * n:]


_SMALL = (("w_s", (DEPTH, SGU_G, BLK, BLK)), ("ln_in_g", (D_MODEL,)), ("ln_in_b", (D_MODEL,)), ("b_in", (DEPTH, N_COLS)),
          ("sinks", (DEPTH, 8)), ("vn_g", (DEPTH, SGU_W)), ("vn_b", (DEPTH, SGU_W)),
          ("b_s", (DEPTH, SGU_G, BLK)), ("b_out", (DEPTH, D_MODEL)), ("ln_g", (DEPTH, D_MODEL)), ("ln_b", (DEPTH, D_MODEL)))
_BIG = ("w_in", "p_a", "p_b", "w_out")


def _size(shape):
    n = 1
    for d in shape:
        n *= d
    return n


def _pack_small(vals, last_row=None):
    rows = []
    for name, shape in _SMALL:
        v = vals[name]
        if _size(shape) % 128:
            v = jnp.pad(v.reshape((1, -1)), ((0, 0), (0, (-_size(shape)) % 128)))
        rows.append(v.reshape((-1, 128)))
    used = sum(r.shape[0] for r in rows)
    tail = jnp.zeros((SMALL_ROWS - used, 128), F32)
    if last_row is not None:
        tail = tail.at[-1, 0].set(last_row)
    return jnp.concatenate([rows[0], jnp.concatenate(rows[1:] + [tail], axis=0)], axis=0)


def _unpack_small(packed):
    out, pos = {}, 0
    for name, shape in _SMALL:
        n = -(-_size(shape) // 128)
        rows = packed[pos:pos + n]
        out[name] = (rows.reshape((-1,))[:_size(shape)] if _size(shape) % 128 else rows).reshape(shape)
        pos += n
    return out


def kernel(x, ln_in_g, ln_in_b, w_in, b_in, sinks, vn_g, vn_b, w_s, b_s, p_a, p_b, w_out, b_out, ln_g, ln_b, loss_target, m_ln_in_g, m_ln_in_b, m_w_in, m_b_in, m_sinks, m_vn_g, m_vn_b, m_w_s, m_b_s, m_p_a, m_p_b, m_w_out, m_b_out, m_ln_g, m_ln_b, v_ln_in_g, v_ln_in_b, v_w_in, v_b_in, v_sinks, v_vn_g, v_vn_b, v_w_s, v_b_s, v_p_a, v_p_b, v_w_out, v_b_out, v_ln_g, v_ln_b):
    weights = dict(ln_in_g=ln_in_g, ln_in_b=ln_in_b, w_in=w_in, b_in=b_in, sinks=sinks, vn_g=vn_g, vn_b=vn_b, w_s=w_s,
                   b_s=b_s, p_a=p_a, p_b=p_b, w_out=w_out, b_out=b_out, ln_g=ln_g, ln_b=ln_b)
    mom1 = dict(ln_in_g=m_ln_in_g, ln_in_b=m_ln_in_b, w_in=m_w_in, b_in=m_b_in, sinks=m_sinks, vn_g=m_vn_g, vn_b=m_vn_b,
                w_s=m_w_s, b_s=m_b_s, p_a=m_p_a, p_b=m_p_b, w_out=m_w_out, b_out=m_b_out, ln_g=m_ln_g, ln_b=m_ln_b)
    mom2 = dict(ln_in_g=v_ln_in_g, ln_in_b=v_ln_in_b, w_in=v_w_in, b_in=v_b_in, sinks=v_sinks, vn_g=v_vn_g, vn_b=v_vn_b,
                w_s=v_w_s, b_s=v_b_s, p_a=v_p_a, p_b=v_p_b, w_out=v_w_out, b_out=v_b_out, ln_g=v_ln_g, ln_b=v_ln_b)
    n_seq, S, _ = x.shape
    T = n_seq * S
    c = lax.axis_index("c")
    j = 2 * lax.axis_index("x") + lax.axis_index("y")
    core = c.astype(jnp.int32).reshape((1,))
    chips = jnp.stack([j] + [r + (r >= j) for r in range(N_CHIPS - 1)]).astype(jnp.int32)
    names = list(_PIECES)
    xt, tt = x.reshape((T, D_MODEL)), loss_target.reshape((T, D_MODEL))

    tview = lambda d: dict(d, w_in=jnp.swapaxes(d["w_in"], 1, 2))
    weights_t, mom1_t, mom2_t = tview(weights), tview(mom1), tview(mom2)

    def own_shards(l):
        return [weights_t[n][l].astype(MXU_DTYPE) for n in _BIG]

    def layer_weights(l, gathered):
        d = dict(zip(_PIECES, gathered))
        d.update(paT=d["pa"].T, pbT=d["pb"].T, woutT=d["wout"].T)
        d.update(bin=b_in[l][None], sinks=sinks[l], vng=vn_g[l][None], vnb=vn_b[l][None], ws=w_s[l],
                 bs=jnp.repeat(b_s[l].T, 128, axis=1), bout=b_out[l][None], lng=ln_g[l][None], lnb=ln_b[l][None])
        return d

    own0, own1 = own_shards(0), own_shards(1)
    x0, g0 = input_norm_forward(xt, ln_in_g[None], ln_in_b[None], "ln_in_fwd", comm=gather_weights(names[:1], own0[:1]))
    n_proj = T // ROWS_PROJ
    h0, g1 = project(x0, g0[0], b_in[0][None], "layer0_proj",
                     comm=gather_weights(names[1:] + names[:1], own0[1:] + own1[:1], turn_step=(3 * n_proj) // 4))
    lw0 = layer_weights(0, g0 + g1[:len(names) - 1])
    x1, h1, g2 = layer_forward_and_projection(x0, h0, lw0, g1[-1], b_in[1][None], S, "layer0_fwd_layer1_proj",
                                              comm=gather_weights(names[1:], own1[1:], turn_step=T // ROWS // 2))
    lw1 = layer_weights(1, g1[-1:] + g2)

    dres1, dh1, acc1, _ = layer_backward(x1, h1, tt, lw1, S, True, "layer1_bwd")
    dx1, dwinT1, _ = input_grads(dres1, dh1, x1, lw1["winT"], "layer1_dx_dwin")
    parts1 = [dwinT1, acc1["dpa"], acc1["dpb"], acc1["dwout"]]
    dres0, dh0, acc0, landed1 = layer_backward(x0, h0, dx1, lw0, S, False, "layer0_bwd", comm=pair_send_halves(names, parts1))
    sums1 = add_own_halves(names, parts1, landed1, core, "add_own_halves1")
    dx0, dwinT0, from_chips1 = input_grads(dres0, dh0, x0, lw0["winT"], "layer0_dx_dwin", comm=chip_exchange(sums1))
    totals1 = sum_chips(sums1, from_chips1, chips, "sum_chips1")
    grad_x, d_ln_in_g, d_ln_in_b, _ = input_norm_backward(xt, dx0, ln_in_g, "ln_in_bwd")

    gl = [acc0, acc1]
    per_layer = lambda key, pick: jnp.stack([pick(gl[l][key]) for l in range(DEPTH)])
    row0 = lambda key: per_layer(key, lambda a: a[0])
    small_part = _pack_small(dict(
        ln_in_g=d_ln_in_g[0], ln_in_b=d_ln_in_b[0], b_in=row0("dbin"), sinks=per_layer("dsink", lambda a: a[:, 0]),
        vn_g=row0("dvng"), vn_b=row0("dvnb"), w_s=per_layer("dws", lambda a: a),
        b_s=per_layer("dbs", lambda a: a[:, ::128].T), b_out=row0("dbout"), ln_g=row0("dlng"), ln_b=row0("dlnb")),
        last_row=acc1["loss"][0, 0])
    parts0 = [dwinT0, acc0["dpa"], acc0["dpb"], acc0["dwout"]]
    first = _run_comm(pair_send_halves(names, parts0, small_part, extra=totals1), "pair_send_halves0")
    landed0, small_landed, sib_totals1 = first[:4], first[4], first[5:]
    sums0 = add_own_halves(names, parts0, landed0, core, "add_own_halves0")
    small_sum = add_small_half(small_part, small_landed, core)
    *from_chips0, small_from_chips = _run_comm(chip_exchange(sums0, small_sum), "chip_exchange0")
    *totals0, small_total = sum_chips(sums0, from_chips0, chips, "sum_chips0", small_landed=small_from_chips)
    *sib_totals0, sib_small = _run_comm(pair_send_totals(totals0 + [small_total]), "pair_send_totals0")

    def both_halves(mine, theirs):
        return jnp.where(c == 0, jnp.concatenate([mine, theirs], axis=0), jnp.concatenate([theirs, mine], axis=0))

    g_halves = [(totals0[pi], sib_totals0[pi], totals1[pi], sib_totals1[pi]) for pi in range(len(_BIG))]
    g_small = both_halves(small_total, sib_small)
    loss = g_small[-1, 0]

    big = adamw_big([weights_t[n] for n in _BIG], g_halves, [mom1_t[n] for n in _BIG], [mom2_t[n] for n in _BIG], core)
    small_names = [n for n, _ in _SMALL]
    as2d = lambda a: a.reshape((1, -1)) if a.ndim == 1 else a
    g_named = _unpack_small(g_small)
    small = adamw_small(*[[as2d(d[n]) for n in small_names] for d in (weights, g_named, mom1, mom2)])

    def named(big_outs, small_outs):
        d = dict(zip(_BIG, big_outs))
        d["w_in"] = jnp.swapaxes(d["w_in"], 1, 2)
        d.update({n: o.reshape(weights[n].shape) for n, o in zip(small_names, small_outs)})
        return d

    order = ("ln_in_g", "ln_in_b", "w_in", "b_in", "sinks", "vn_g", "vn_b", "w_s", "b_s", "p_a", "p_b", "w_out", "b_out",
             "ln_g", "ln_b")
    res = [loss, grad_x.reshape(x.shape)]
    groups = [named(big[0], [g_named[n] for n in small_names])] + [named(big[k + 1], small[k]) for k in range(3)]
    for group in groups:
        res.extend(group[n] for n in order)
    return tuple(res)
```

```python
import functools

import jax
import jax.numpy as jnp
from jax import lax
from jax.experimental import pallas as pl
from jax.experimental.pallas import tpu as pltpu

F32 = jnp.float32
MXU_DTYPE = jnp.bfloat16

D_MODEL = 1024
DEPTH = 2
HEAD_DIM = 64
ATTN_W = 512
KV_W = 128
BLK = 128
SGU_W = 512
SGU_G = 4
N_COLS = 4864
C_Q, C_K, C_V, C_GA, C_UB, C_VB, C_GB, C_RA, C_RB = 0, 512, 640, 768, 1280, 1792, 2304, 2816, 3840
ALPHA = (2.0 * DEPTH) ** 0.25
LN_EPS = 1e-5
SCALE = HEAD_DIM ** -0.5
NEG = float(jnp.finfo(jnp.float32).min)
GELU_C = 0.7978845608028654
GELU_A = 0.044715

ADAM_LR, ADAM_B1, ADAM_B2, ADAM_EPS, ADAM_WD, ADAM_STEP = 0.001, 0.9, 0.999, 1e-08, 0.01, 10

V7X_VMEM_BYTES = 64 * 1024 * 1024
V7X_VMEM_RESERVE = 3 * 1024 * 1024
ROWS = 256
ROWS_BWD = 256
ROWS_PROJ = 512

MESH = pl.DeviceIdType.MESH
N_CHIPS = 4


def _mm(a, b):
    return jnp.dot(a.astype(MXU_DTYPE), b.astype(MXU_DTYPE), preferred_element_type=F32)


def _mm_nt(a, b):
    return lax.dot_general(a.astype(MXU_DTYPE), b.astype(MXU_DTYPE), (((1,), (1,)), ((), ())),
                           preferred_element_type=F32)


def _mm_tn(a, b):
    return lax.dot_general(a.astype(MXU_DTYPE), b.astype(MXU_DTYPE), (((0,), (0,)), ((), ())),
                           preferred_element_type=F32)


def _sigmoid(x):
    return 0.5 * jnp.tanh(0.5 * x) + 0.5


def _gelu_parts(x):
    x2 = x * x
    u = 0.5 * jnp.tanh(x * (GELU_C + (GELU_C * GELU_A) * x2)) + 0.5
    dg = u * (1.0 + x * (1.0 - u) * (2.0 * GELU_C + (6.0 * GELU_C * GELU_A) * x2))
    return x * u, dg


def _ln_fwd(z, g, b):
    mu = jnp.mean(z, axis=-1, keepdims=True)
    zc = z - mu
    var = jnp.mean(zc * zc, axis=-1, keepdims=True)
    rstd = lax.rsqrt(var + LN_EPS)
    zhat = zc * rstd
    return zhat * g + b, zhat, rstd


def _ln_bwd(dy, zhat, rstd, g):
    dzh = dy * g
    m1 = jnp.mean(dzh, axis=-1, keepdims=True)
    m2 = jnp.mean(dzh * zhat, axis=-1, keepdims=True)
    return rstd * (dzh - m1 - zhat * m2)


def _colsum(v):
    return jnp.sum(v, axis=0, keepdims=True)


def _lane_lo(rows):
    return lax.broadcasted_iota(jnp.int32, (rows, 128), 1) < HEAD_DIM


def _dup_head(x2, hk, lo):
    xr = pltpu.roll(x2, HEAD_DIM, 1)
    return jnp.where(lo, x2, xr) if hk == 0 else jnp.where(lo, xr, x2)


def _fold_head(y0, y1, lo):
    f0 = y0 + pltpu.roll(y0, HEAD_DIM, 1)
    f1 = y1 + pltpu.roll(y1, HEAD_DIM, 1)
    return jnp.where(lo, f0, f1)


def _stack_heads(t0, t1, lo):
    z = jnp.zeros_like(t0)
    return jnp.concatenate([jnp.where(lo, t0, z), jnp.where(lo, z, t0),
                            jnp.where(lo, t1, z), jnp.where(lo, z, t1)], axis=0)


def _unstack_heads(o, lo):
    return (jnp.where(lo, o[0:128], o[128:256]), jnp.where(lo, o[256:384], o[384:512]))


def _band_bias(start):
    row = lax.broadcasted_iota(jnp.int32, (4 * BLK, 2 * BLK), 0) & (BLK - 1)
    kpos = lax.broadcasted_iota(jnp.int32, (4 * BLK, 2 * BLK), 1)
    valid = (kpos > row) & (kpos <= row + BLK)
    if start is not None:
        valid = valid & (jnp.logical_not(start) | (kpos >= BLK))
    return jnp.where(valid, 0.0, NEG)


def _attn_probs(s, sink4, bias):
    s = s * SCALE + bias
    r1 = lax.broadcasted_iota(jnp.int32, (4 * BLK, 1), 0)
    sk = jnp.where(r1 < BLK, sink4[0], jnp.where(r1 < 2 * BLK, sink4[1], jnp.where(r1 < 3 * BLK, sink4[2], sink4[3])))
    m = jnp.maximum(jnp.max(s, axis=-1, keepdims=True), sk)
    p = jnp.exp(s - m)
    es = jnp.exp(sk - m)
    inv = 1.0 / (jnp.sum(p, axis=-1, keepdims=True) + es)
    return p * inv, es * inv


_PROJ_CHUNK = 512


def _kv_blocks(kvh, h_s, nb):
    ks = [kvh[:, 0:KV_W]] + [h_s[n * BLK:(n + 1) * BLK, C_K:C_K + KV_W] for n in range(nb)]
    vs = [kvh[:, KV_W:2 * KV_W]] + [h_s[n * BLK:(n + 1) * BLK, C_V:C_V + KV_W] for n in range(nb)]
    return ks, vs


def _attn_operands(h_s, ks, vs, n, hk, lo, lo2):
    rows = slice(n * BLK, (n + 1) * BLK)
    kd = _dup_head(jnp.concatenate([ks[n], ks[n + 1]], axis=0), hk, lo2)
    vd = _dup_head(jnp.concatenate([vs[n], vs[n + 1]], axis=0), hk, lo2)
    c0 = C_Q + 2 * hk * 128
    qs = _stack_heads(h_s[rows, c0:c0 + 128], h_s[rows, c0 + 128:c0 + 256], lo)
    return qs, kd, vd


def _sgu_weights(w):
    tri = (lax.broadcasted_iota(jnp.int32, (BLK, BLK), 0) >= lax.broadcasted_iota(jnp.int32, (BLK, BLK), 1))
    return tri, [jnp.where(tri, w["ws"][g], 0.0) for g in range(SGU_G)]


def _layer_forward(x, kvh, start, w, h_s, attn_s, mix_s, keep, tick=lambda: None):
    R = x.shape[0]
    nb = R // BLK
    lo = _lane_lo(BLK)
    lo2 = _lane_lo(2 * BLK)
    ks, vs = _kv_blocks(kvh, h_s, nb)
    sinks = [w["sinks"][j] for j in range(8)]
    bias_first = _band_bias(start)
    bias_rest = _band_bias(None) if nb > 1 else None
    pairs = [(n, hk) for n in range(nb) for hk in range(2)]
    ops = [_attn_operands(h_s, ks, vs, n, hk, lo, lo2) for n, hk in pairs]
    scores = [_mm_nt(qs, kd) for qs, kd, _ in ops]
    probs_l = [_attn_probs(s, sinks[4 * hk:4 * hk + 4], bias_first if n == 0 else bias_rest)
               for s, (n, hk) in zip(scores, pairs)]
    outs = [_mm(p, vd) for (p, _), (_, _, vd) in zip(probs_l, ops)]
    for o, (n, hk) in zip(outs, pairs):
        rows = slice(n * BLK, (n + 1) * BLK)
        t0, t1 = _unstack_heads(o, lo)
        attn_s[rows, 2 * hk * 128:(2 * hk + 1) * 128] = t0
        attn_s[rows, (2 * hk + 1) * 128:(2 * hk + 2) * 128] = t1
        tick()
    ga = h_s[:, C_GA:C_GA + ATTN_W]
    ya = attn_s[...] * (ga * _sigmoid(ga))
    tick()
    gu, _ = _gelu_parts(h_s[:, C_UB:C_UB + SGU_W])
    tick()
    gv, _ = _gelu_parts(h_s[:, C_VB:C_VB + SGU_W])
    vn, _, _ = _ln_fwd(gv, w["vng"][...], w["vnb"][...])
    tick()
    _, wms = _sgu_weights(w)
    tiles = [(slice(n * BLK, (n + 1) * BLK), g, slice(g * 128, (g + 1) * 128)) for n in range(nb) for g in range(SGU_G)]
    mixes = [_mm(wms[g], vn[rows, cols]) for rows, g, cols in tiles]
    for m, (rows, g, cols) in zip(mixes, tiles):
        mix_s[rows, cols] = m + w["bs"][:, cols]
    gb = h_s[:, C_GB:C_GB + SGU_W]
    yb = gu * mix_s[...] * (gb * _sigmoid(gb))
    tick()
    a = _mm(ya, w["pa"][...])
    b = _mm(yb, w["pb"][...])
    tick()
    merged = _sigmoid(h_s[:, C_RA:C_RA + D_MODEL]) * a + _sigmoid(h_s[:, C_RB:C_RB + D_MODEL]) * b
    tick()
    out = _mm(merged, w["wout"][...]) + w["bout"][...]
    y, zhat, rstd = _ln_fwd(ALPHA * x + out, w["lng"][...], w["lnb"][...])
    if not keep:
        return y, None
    return y, dict(ks=ks, vs=vs, probs=probs_l, ya=ya, yb=yb, a=a, b=b, merged=merged, zhat=zhat, rstd=rstd)


def _dsilu(g, sg):
    return sg * (1.0 + g * (1.0 - sg))


_W_NAMES = ("sinks", "vng", "vnb", "ws", "bs", "pa", "pb", "wout", "bout", "lng", "lnb")
_WT_NAMES = ("paT", "pbT", "woutT")
_ACC_NAMES = ("dwout", "dpa", "dpb", "dbin", "dbout", "dlng", "dlnb", "dvng", "dvnb", "dws", "dbs", "dsink", "loss")


def _make_fwd_proj_body(R, S, nT):
    def body(x_ref, h_ref, kvh_ref, wn_ref, bn_ref, *rest):
        w = dict(zip(_W_NAMES, rest[:len(_W_NAMES)]))
        y_ref, hn_ref, attn_s, mix_s, yprev_s = rest[len(_W_NAMES):]

        @pl.when(pl.program_id(0) == 0)
        def _():
            yprev_s[...] = jnp.zeros(yprev_s.shape, yprev_s.dtype)

        t = jnp.minimum(pl.program_id(0), nT - 1)
        start = (t % (S // R)) == 0
        chunks = [(c0, min(c0 + _PROJ_CHUNK, N_COLS)) for c0 in range(0, N_COLS, _PROJ_CHUNK)]

        def tick():
            if chunks:
                c0, c1 = chunks.pop(0)
                hn_ref[:, c0:c1] = _mm_nt(yprev_s[...], wn_ref[c0:c1, :]) + bn_ref[:, c0:c1]

        y, _ = _layer_forward(x_ref[...], kvh_ref[...], start, w, h_ref, attn_s, mix_s, keep=False, tick=tick)
        while chunks:
            tick()
        y_ref[...] = y
        yprev_s[...] = y.astype(yprev_s.dtype)
    return body


def _make_bwd_body(R, S, nT, is_last):
    nb = R // BLK
    names = _W_NAMES + _WT_NAMES
    nw = len(names)

    def body(x_ref, h_s, kvh_ref, aux_ref, *rest):
        w = dict(zip(names, rest[:nw]))
        dres_ref, dh_ref = rest[nw:nw + 2]
        acc = dict(zip(_ACC_NAMES, rest[nw + 2:nw + 2 + len(_ACC_NAMES)]))
        attn_s, mix_s, ckv_s = rest[nw + 2 + len(_ACC_NAMES):]
        i = pl.program_id(0)
        start = ((nT - 1 - i) % (S // R)) == 0
        lo = _lane_lo(BLK)
        lo2 = _lane_lo(2 * BLK)

        @pl.when(i == 0)
        def _():
            for name in _ACC_NAMES:
                acc[name][...] = jnp.zeros(acc[name].shape, F32)
            ckv_s[...] = jnp.zeros(ckv_s.shape, F32)

        x = x_ref[...]
        y, sv = _layer_forward(x, kvh_ref[...], start, w, h_s, attn_s, mix_s, keep=True)
        if is_last:
            diff = y - aux_ref[...]
            part = 0.5 * jnp.sum(jnp.mean(diff * diff, axis=-1, keepdims=True), axis=0, keepdims=True)
            acc["loss"][...] += jnp.broadcast_to(part, acc["loss"].shape)
            dy = diff * (1.0 / D_MODEL)
        else:
            dy = aux_ref[...]

        def put(c0, val, rows=slice(None)):
            width = val.shape[1]
            dh_ref[rows, c0:c0 + width] = val.astype(dh_ref.dtype)
            acc["dbin"][:, c0:c0 + width] += _colsum(val)

        acc["dlng"][...] += _colsum(dy * sv["zhat"])
        acc["dlnb"][...] += _colsum(dy)
        dz = _ln_bwd(dy, sv["zhat"], sv["rstd"], w["lng"][...])
        dres_ref[...] = ALPHA * dz
        acc["dbout"][...] += _colsum(dz)
        acc["dwout"][...] += _mm_tn(sv["merged"], dz)
        dmerged = _mm(dz, w["woutT"][...])
        sa = _sigmoid(h_s[:, C_RA:C_RA + D_MODEL])
        da = dmerged * sa
        put(C_RA, da * sv["a"] * (1.0 - sa))
        sb = _sigmoid(h_s[:, C_RB:C_RB + D_MODEL])
        db = dmerged * sb
        put(C_RB, db * sv["b"] * (1.0 - sb))
        acc["dpa"][...] += _mm_tn(sv["ya"], da)
        acc["dpb"][...] += _mm_tn(sv["yb"], db)
        dya = _mm(da, w["paT"][...])
        dyb = _mm(db, w["pbT"][...])
        ga = h_s[:, C_GA:C_GA + ATTN_W]
        sga = _sigmoid(ga)
        put(C_GA, dya * attn_s[...] * _dsilu(ga, sga))
        attn_s[...] = dya * (ga * sga)
        ks, vs = sv["ks"], sv["vs"]
        pairs = [(n, hk) for n in range(nb) for hk in range(2)]
        ops = [_attn_operands(h_s, ks, vs, n, hk, lo, lo2) for n, hk in pairs]
        doss = [_stack_heads(attn_s[n * BLK:(n + 1) * BLK, 2 * hk * 128:(2 * hk + 1) * 128],
                             attn_s[n * BLK:(n + 1) * BLK, (2 * hk + 1) * 128:(2 * hk + 2) * 128], lo) for n, hk in pairs]
        dps = [_mm_nt(dos, vd) for dos, (_, _, vd) in zip(doss, ops)]
        gb = h_s[:, C_GB:C_GB + SGU_W]
        sgb = _sigmoid(gb)
        gu, dgu = _gelu_parts(h_s[:, C_UB:C_UB + SGU_W])
        mixed = mix_s[...]
        put(C_GB, dyb * (gu * mixed) * _dsilu(gb, sgb))
        dsgu = dyb * (gb * sgb)
        put(C_UB, dsgu * mixed * dgu)
        dmixed = dsgu * gu
        dsss = []
        for dp, (probs, ps), (n, hk) in zip(dps, sv["probs"], pairs):
            delta = jnp.sum(probs * dp, axis=-1, keepdims=True)
            dsk = -(ps * delta)
            for g in range(4):
                j = 4 * hk + g
                acc["dsink"][j:j + 1, :] += jnp.broadcast_to(_colsum(dsk[g * BLK:(g + 1) * BLK]), (1, 128))
            dsss.append(probs * (dp - delta) * SCALE)
        dqs = [_mm(dss, kd) for dss, (_, kd, _) in zip(dsss, ops)]
        ydks = [_mm_tn(dss, qs) for dss, (qs, _, _) in zip(dsss, ops)]
        ydvs = [_mm_tn(probs, dos) for (probs, _), dos in zip(sv["probs"], doss)]
        gv, dgv = _gelu_parts(h_s[:, C_VB:C_VB + SGU_W])
        vn, vhat, vrstd = _ln_fwd(gv, w["vng"][...], w["vnb"][...])
        tri, wms = _sgu_weights(w)
        tiles = [(slice(n * BLK, (n + 1) * BLK), g, slice(g * 128, (g + 1) * 128)) for n in range(nb) for g in range(SGU_G)]
        dw_parts = [_mm_nt(dmixed[rows, cols], vn[rows, cols]) for rows, g, cols in tiles]
        dvn_parts = [_mm_tn(wms[g], dmixed[rows, cols]) for rows, g, cols in tiles]
        for g in range(SGU_G):
            tot = dw_parts[g]
            for n in range(1, nb):
                tot = tot + dw_parts[n * SGU_G + g]
            acc["dws"][g] += jnp.where(tri, tot, 0.0)
        for part, (rows, g, cols) in zip(dvn_parts, tiles):
            mix_s[rows, cols] = part
        dbs_part = dmixed[0:BLK]
        for n in range(1, nb):
            dbs_part = dbs_part + dmixed[n * BLK:(n + 1) * BLK]
        acc["dbs"][...] += dbs_part
        for dq, (n, hk) in zip(dqs, pairs):
            q0, q1 = _unstack_heads(dq, lo)
            put(C_Q + 2 * hk * 128, q0, slice(n * BLK, (n + 1) * BLK))
            put(C_Q + (2 * hk + 1) * 128, q1, slice(n * BLK, (n + 1) * BLK))
        dks = [jnp.zeros((BLK, KV_W), F32) for _ in range(nb + 1)]
        dvs = [jnp.zeros((BLK, KV_W), F32) for _ in range(nb + 1)]
        for n in range(nb):
            dk2 = _fold_head(ydks[2 * n], ydks[2 * n + 1], lo2)
            dv2 = _fold_head(ydvs[2 * n], ydvs[2 * n + 1], lo2)
            dks[n] = dks[n] + dk2[0:BLK]
            dks[n + 1] = dks[n + 1] + dk2[BLK:2 * BLK]
            dvs[n] = dvs[n] + dv2[0:BLK]
            dvs[n + 1] = dvs[n + 1] + dv2[BLK:2 * BLK]
        dks[nb] = dks[nb] + ckv_s[:, 0:KV_W]
        dvs[nb] = dvs[nb] + ckv_s[:, KV_W:2 * KV_W]
        ckv_s[:, 0:KV_W] = dks[0]
        ckv_s[:, KV_W:2 * KV_W] = dvs[0]
        put(C_K, jnp.concatenate(dks[1:], axis=0))
        put(C_V, jnp.concatenate(dvs[1:], axis=0))
        dvn = mix_s[...]
        acc["dvng"][...] += _colsum(dvn * vhat)
        acc["dvnb"][...] += _colsum(dvn)
        put(C_VB, _ln_bwd(dvn, vhat, vrstd, w["vng"][...]) * dgv)

        @pl.when(i == nT - 1)
        def _():
            for g in range(SGU_G):
                cols = slice(g * 128, (g + 1) * 128)
                tot = jnp.sum(acc["dbs"][:, cols], axis=1, keepdims=True)
                acc["dbs"][:, cols] = jnp.broadcast_to(tot, (BLK, 128))

    return body


def _resident():
    return pl.BlockSpec(memory_space=pltpu.VMEM)


_BIG_RESIDENT = ("pa", "pb", "wout", "paT", "pbT", "woutT", "dwout", "dpa", "dpb")


def _const_block(shape):
    return pl.BlockSpec(tuple(shape), lambda *_: (0,) * len(shape))


def _operand_spec(name, shape):
    if name == "sinks":
        return pl.BlockSpec(memory_space=pltpu.SMEM)
    return _resident() if name in _BIG_RESIDENT else _const_block(shape)


def _weight_specs(names, wts):
    return [_operand_spec(n, wts[n].shape) for n in names]


def _nbytes(a):
    n = jnp.dtype(a.dtype).itemsize
    for d in a.shape:
        n *= d
    return n


def _layer_params(resident=()):
    scoped = V7X_VMEM_BYTES - V7X_VMEM_RESERVE - sum(_nbytes(a) for a in resident)
    return pltpu.CompilerParams(dimension_semantics=("arbitrary",), vmem_limit_bytes=scoped)


class _Comm:
    def __init__(self, ins, out_shapes, sems, start, finish, aliases=None, turn=None, turn_step=None):
        self.ins = [pltpu.with_memory_space_constraint(a, pltpu.HBM) for a in ins]
        self.out_shapes = [pltpu.HBM(s.shape, s.dtype) for s in out_shapes]
        self.sems = list(sems)
        self.start, self.finish, self.aliases = start, finish, dict(aliases or {})
        self.turn, self.turn_step = turn, turn_step


_ANY = pl.BlockSpec(memory_space=pltpu.HBM)
_DMA = pltpu.SemaphoreType.DMA


def _call(body, *, name, grid, in_specs, out_specs, out_shape, scratch_shapes, compiler_params, operands, comm=None):
    if comm is None:
        res = pl.pallas_call(body, name=name, grid=grid, in_specs=in_specs, out_specs=out_specs, out_shape=out_shape,
                             scratch_shapes=scratch_shapes, compiler_params=compiler_params)(*operands)
        return list(res), []
    n_in, n_out, n_scr = len(in_specs), len(out_specs), len(scratch_shapes)
    ci, co = len(comm.ins), len(comm.out_shapes)
    last = grid[0] - 1

    def hosted(*refs):
        ins, cin = refs[:n_in], refs[n_in:n_in + ci]
        p = n_in + ci
        outs, cout = refs[p:p + n_out], refs[p + n_out:p + n_out + co]
        p += n_out + co
        scr, sems = refs[p:p + n_scr], refs[p + n_scr:]
        pl.when(pl.program_id(0) == 0)(lambda: comm.start(cin, cout, sems))
        body(*ins, *outs, *scr)
        if comm.turn is not None:
            at = last if comm.turn_step is None else comm.turn_step
            pl.when(pl.program_id(0) == at)(lambda: comm.turn(cin, cout, sems))
        pl.when(pl.program_id(0) == last)(lambda: comm.finish(cin, cout, sems))

    res = pl.pallas_call(
        hosted, name=name, grid=grid, in_specs=list(in_specs) + [_ANY] * ci, out_specs=list(out_specs) + [_ANY] * co,
        out_shape=list(out_shape) + comm.out_shapes, scratch_shapes=list(scratch_shapes) + comm.sems,
        input_output_aliases={n_in + a: n_out + b for a, b in comm.aliases.items()},
        compiler_params=compiler_params)(*operands, *comm.ins)
    return list(res[:n_out]), list(res[n_out:])


def _run_comm(comm, name):
    ci, co = len(comm.ins), len(comm.out_shapes)

    def body(*refs):
        cin, cout, sems = refs[:ci], refs[ci:ci + co], refs[ci + co:]
        comm.start(cin, cout, sems)
        if comm.turn is not None:
            comm.turn(cin, cout, sems)
        comm.finish(cin, cout, sems)

    return list(pl.pallas_call(body, name=name, in_specs=[_ANY] * ci, out_specs=[_ANY] * co, out_shape=comm.out_shapes,
                               scratch_shapes=comm.sems, input_output_aliases=comm.aliases)(*comm.ins))


def project(x, winT, b, name, comm=None):
    T = x.shape[0]
    R = ROWS_PROJ

    def body(x_ref, w_ref, b_ref, h_ref):
        xb = x_ref[...].astype(MXU_DTYPE)
        for c0 in range(0, N_COLS, _PROJ_CHUNK):
            c1 = min(c0 + _PROJ_CHUNK, N_COLS)
            h_ref[:, c0:c1] = _mm_nt(xb, w_ref[c0:c1, :]) + b_ref[:, c0:c1]

    (h,), extra = _call(
        body, name=name, grid=(T // R,),
        in_specs=[pl.BlockSpec((R, D_MODEL), lambda t: (t, 0)), _resident(), _const_block(b.shape)],
        out_specs=[pl.BlockSpec((R, N_COLS), lambda t: (t, 0))],
        out_shape=[jax.ShapeDtypeStruct((T, N_COLS), F32)], scratch_shapes=[],
        compiler_params=_layer_params([winT]), operands=(x, winT, b), comm=comm)
    return h, extra


_KV_BLOCK = C_K // (2 * KV_W)


def layer_forward_and_projection(x, h, wts, winT_next, b_next, S, name, comm=None):
    T = x.shape[0]
    R = ROWS
    nT = T // R
    npt = R // BLK
    cur = lambda t: jnp.minimum(t, nT - 1)
    (y, hn), extra = _call(
        _make_fwd_proj_body(R, S, nT), name=name, grid=(nT + 1,),
        in_specs=[pl.BlockSpec((R, D_MODEL), lambda t: (cur(t), 0)), pl.BlockSpec((R, N_COLS), lambda t: (cur(t), 0)),
                  pl.BlockSpec((BLK, 2 * KV_W), lambda t: (jnp.maximum(cur(t) * npt - 1, 0), _KV_BLOCK)),
                  _resident(), _const_block(b_next.shape)] + _weight_specs(_W_NAMES, wts),
        out_specs=[pl.BlockSpec((R, D_MODEL), lambda t: (cur(t), 0)),
                   pl.BlockSpec((R, N_COLS), lambda t: (jnp.maximum(t - 1, 0), 0))],
        out_shape=[jax.ShapeDtypeStruct((T, D_MODEL), F32), jax.ShapeDtypeStruct((T, N_COLS), F32)],
        scratch_shapes=[pltpu.VMEM((R, ATTN_W), F32), pltpu.VMEM((R, SGU_W), F32), pltpu.VMEM((R, D_MODEL), MXU_DTYPE)],
        compiler_params=_layer_params([winT_next] + [wts[n] for n in _W_NAMES if n in _BIG_RESIDENT]),
        operands=(x, h, h, winT_next, b_next, *[wts[n] for n in _W_NAMES]), comm=comm)
    return y, hn, extra


_ACC_SHAPES = dict(dwout=(D_MODEL, D_MODEL), dpa=(ATTN_W, D_MODEL), dpb=(SGU_W, D_MODEL), dbin=(1, N_COLS),
                   dbout=(1, D_MODEL), dlng=(1, D_MODEL), dlnb=(1, D_MODEL), dvng=(1, SGU_W), dvnb=(1, SGU_W),
                   dws=(SGU_G, BLK, BLK), dbs=(BLK, SGU_W), dsink=(8, 128), loss=(8, 128))


def layer_backward(x, h, aux, wts, S, is_last, name, comm=None):
    T = x.shape[0]
    R = ROWS_BWD
    nT = T // R
    npt = R // BLK
    rev = lambda i: (nT - 1 - i, 0)
    halo = lambda i: (jnp.maximum((nT - 1 - i) * npt - 1, 0), _KV_BLOCK)
    names = _W_NAMES + _WT_NAMES
    out_shape = ([jax.ShapeDtypeStruct((T, D_MODEL), F32), jax.ShapeDtypeStruct((T, N_COLS), MXU_DTYPE)]
                 + [jax.ShapeDtypeStruct(_ACC_SHAPES[n], F32) for n in _ACC_NAMES])
    outs, extra = _call(
        _make_bwd_body(R, S, nT, is_last), name=name, grid=(nT,),
        in_specs=[pl.BlockSpec((R, D_MODEL), rev), pl.BlockSpec((R, N_COLS), rev), pl.BlockSpec((BLK, 2 * KV_W), halo),
                  pl.BlockSpec((R, D_MODEL), rev)] + _weight_specs(names, wts),
        out_specs=[pl.BlockSpec((R, D_MODEL), rev), pl.BlockSpec((R, N_COLS), rev)]
        + [_operand_spec(n, _ACC_SHAPES[n]) for n in _ACC_NAMES],
        out_shape=out_shape,
        scratch_shapes=[pltpu.VMEM((R, ATTN_W), F32), pltpu.VMEM((R, SGU_W), F32), pltpu.VMEM((BLK, 2 * KV_W), F32)],
        compiler_params=_layer_params([wts[n] for n in names if n in _BIG_RESIDENT]
                                      + [jax.ShapeDtypeStruct(_ACC_SHAPES[n], F32) for n in _ACC_NAMES if n in _BIG_RESIDENT]),
        operands=(x, h, h, aux, *[wts[n] for n in names]), comm=comm)
    return outs[0], outs[1], dict(zip(_ACC_NAMES, outs[2:])), extra


def input_grads(dres, dh, x, winT, name, comm=None):
    T = x.shape[0]
    R = ROWS

    def body(dres_ref, dh_ref, x_ref, winT_ref, dx_ref, dwin_ref):
        @pl.when(pl.program_id(0) == 0)
        def _():
            dwin_ref[...] = jnp.zeros(dwin_ref.shape, F32)

        dh = dh_ref[...]
        dx_ref[...] = dres_ref[...] + jnp.dot(dh, winT_ref[...], preferred_element_type=F32)
        dwin_ref[...] += _mm_tn(dh, x_ref[...])

    row = lambda t: (t, 0)
    dwin_shape = jax.ShapeDtypeStruct((N_COLS, D_MODEL), F32)
    (dx, dwinT), extra = _call(
        body, name=name, grid=(T // R,),
        in_specs=[pl.BlockSpec((R, D_MODEL), row), pl.BlockSpec((R, N_COLS), row), pl.BlockSpec((R, D_MODEL), row),
                  _resident()],
        out_specs=[pl.BlockSpec((R, D_MODEL), row), _resident()],
        out_shape=[jax.ShapeDtypeStruct((T, D_MODEL), F32), dwin_shape], scratch_shapes=[],
        compiler_params=_layer_params([winT, dwin_shape]), operands=(dres, dh, x, winT), comm=comm)
    return dx, dwinT, extra


def input_norm_forward(x, g, b, name, comm=None):
    T = x.shape[0]
    R = ROWS_PROJ

    def body(x_ref, g_ref, b_ref, y_ref):
        y_ref[...] = _ln_fwd(x_ref[...], g_ref[...], b_ref[...])[0]

    row = lambda t: (t, 0)
    (y,), extra = _call(
        body, name=name, grid=(T // R,),
        in_specs=[pl.BlockSpec((R, D_MODEL), row), _const_block(g.shape), _const_block(b.shape)],
        out_specs=[pl.BlockSpec((R, D_MODEL), row)],
        out_shape=[jax.ShapeDtypeStruct((T, D_MODEL), F32)], scratch_shapes=[],
        compiler_params=_layer_params(), operands=(x, g, b), comm=comm)
    return y, extra


def input_norm_backward(x, dy, g, name, comm=None):
    T = x.shape[0]
    R = ROWS_PROJ

    def body(x_ref, dy_ref, g_ref, dx_ref, dg_ref, db_ref):
        @pl.when(pl.program_id(0) == 0)
        def _():
            dg_ref[...] = jnp.zeros(dg_ref.shape, F32)
            db_ref[...] = jnp.zeros(db_ref.shape, F32)

        dy = dy_ref[...]
        _, xhat, rstd = _ln_fwd(x_ref[...], g_ref[...], g_ref[...])
        dx_ref[...] = _ln_bwd(dy, xhat, rstd, g_ref[...])
        dg_ref[...] += _colsum(dy * xhat)
        db_ref[...] += _colsum(dy)

    row = lambda t: (t, 0)
    vec = jax.ShapeDtypeStruct((1, D_MODEL), F32)
    (dx, dg, db), extra = _call(
        body, name=name, grid=(T // R,),
        in_specs=[pl.BlockSpec((R, D_MODEL), row), pl.BlockSpec((R, D_MODEL), row), _const_block(g.shape)],
        out_specs=[pl.BlockSpec((R, D_MODEL), row), _const_block(vec.shape), _const_block(vec.shape)],
        out_shape=[jax.ShapeDtypeStruct((T, D_MODEL), F32), vec, vec], scratch_shapes=[],
        compiler_params=_layer_params(), operands=(x, dy, g), comm=comm)
    return dx, dg, db, extra


_PIECES = ("winT", "pa", "pb", "wout")
_WHOLE = dict(winT=(N_COLS, D_MODEL), pa=(ATTN_W, D_MODEL), pb=(SGU_W, D_MODEL), wout=(D_MODEL, D_MODEL))
_HALF = dict(winT=(N_COLS // 8, D_MODEL), pa=(ATTN_W // 2, D_MODEL // 4), pb=(SGU_W // 2, D_MODEL // 4),
             wout=(D_MODEL // 8, D_MODEL))
SMALL_ROWS = 1280


def _region(name, ref, k, h):
    hr, hc = _HALF[name]
    if name in ("pa", "pb"):
        return ref.at[pl.ds(h * hr, hr), pl.ds(k * hc, hc)]
    return ref.at[pl.ds(k * 2 * hr + h * hr, hr), :]


def _place():
    x, y, c = lax.axis_index("x"), lax.axis_index("y"), lax.axis_index("c")
    return x, y, c, 2 * x + y


def _as_chip(j, fn):
    for jj in range(N_CHIPS):
        pl.when(j == jj)(functools.partial(fn, jj))


def _remote(src, dst, send_sem, recv_sem, to):
    return pltpu.make_async_remote_copy(src_ref=src, dst_ref=dst, send_sem=send_sem, recv_sem=recv_sem,
                                        device_id=to, device_id_type=MESH)


def _core_of(k, c):
    return (k // 2, k % 2, c)


def _others(jj):
    return [k for k in range(N_CHIPS) if k != jj]


def _start_all(cps):
    for cp in cps:
        cp.start()


def _wait_all(cps):
    for cp in cps:
        cp.wait()


def gather_over_chips(names, shards):
    n = len(names)

    def own(cin, cout, sems, jj):
        x, y, c, _ = _place()
        cps = []
        for i, name in enumerate(names):
            hr, hc = _HALF[name]
            place = (cout[i].at[:, pl.ds(jj * hc, hc)] if name in ("pa", "pb")
                     else cout[i].at[pl.ds(jj * 2 * hr, 2 * hr), :])
            cps.append(_remote(cin[i], place, sems[2].at[i], sems[3].at[i], (x, y, 1 - c)))
        return cps

    def copies(cin, cout, sems, jj, c):
        out = []
        for i, name in enumerate(names):
            hr = _HALF[name][0]
            for k in _others(jj):
                out.append(_remote(cin[i].at[pl.ds(c * hr, hr)], _region(name, cout[i], jj, c),
                                   sems[0].at[4 * i + k], sems[1].at[4 * i + jj], _core_of(k, c)))
        return out

    def start(cin, cout, sems):
        _, _, c, j = _place()
        _as_chip(j, lambda jj: _start_all(copies(cin, cout, sems, jj, c) + own(cin, cout, sems, jj)))

    def finish(cin, cout, sems):
        _, _, c, j = _place()

        def run(jj):
            for i, name in enumerate(names):
                for k in _others(jj):
                    land = _region(name, cout[i], k, c)
                    _remote(land, land, sems[0].at[4 * i + k], sems[1].at[4 * i + k], _core_of(k, c)).wait_recv()
            for cp in copies(cin, cout, sems, jj, c):
                cp.wait_send()
            _wait_all(own(cin, cout, sems, jj))

        _as_chip(j, run)

    shapes = [jax.ShapeDtypeStruct(_WHOLE[nm], s.dtype) for nm, s in zip(names, shards)]
    return _Comm(shards, shapes, [_DMA((4 * n,)), _DMA((4 * n,)), _DMA((n,)), _DMA((n,))], start, finish)


def gather_weights(names, shards, turn_step=None):
    n = len(names)
    chips = gather_over_chips(names, shards)
    pair_start, pair_finish = _pair_phase(names)

    def turn(cin, cout, sems):
        chips.finish(cin, cout, sems[:4])
        pair_start(cin, cout, sems[4:])

    return _Comm(shards, chips.out_shapes, chips.sems + [_DMA((4 * n,)), _DMA((4 * n,))], chips.start,
                 lambda cin, cout, sems: pair_finish(cin, cout, sems[4:]), turn=turn, turn_step=turn_step)


def _pair_phase(names):
    def start(cin, cout, sems):
        x, y, c, j = _place()

        def run(jj):
            for i, name in enumerate(names):
                for k in _others(jj):
                    land = _region(name, cout[i], k, c)
                    _remote(land, land, sems[0].at[4 * i + k], sems[1].at[4 * i + k], (x, y, 1 - c)).start()

        _as_chip(j, run)

    def finish(cin, cout, sems):
        x, y, c, j = _place()

        def run(jj):
            for i, name in enumerate(names):
                for k in _others(jj):
                    theirs = _region(name, cout[i], k, 1 - c)
                    _remote(theirs, theirs, sems[0].at[4 * i + k], sems[1].at[4 * i + k], (x, y, 1 - c)).wait_recv()
            for i, name in enumerate(names):
                for k in _others(jj):
                    land = _region(name, cout[i], k, c)
                    _remote(land, land, sems[0].at[4 * i + k], sems[1].at[4 * i + k], (x, y, 1 - c)).wait_send()

        _as_chip(j, run)

    return start, finish


def pair_send_halves(names, parts, small=None, extra=()):
    n = len(names)
    hs = SMALL_ROWS // 2
    first_extra = n + (small is not None)

    def copies(cin, cout, sems):
        x, y, c, _ = _place()
        sib = (x, y, 1 - c)
        cps = []
        for i, name in enumerate(names):
            for k in range(N_CHIPS):
                cps.append(_remote(_region(name, cin[i], k, 1 - c), cout[i].at[k], sems[0].at[4 * i + k],
                                   sems[1].at[4 * i + k], sib))
        if small is not None:
            cps.append(_remote(cin[n].at[pl.ds((1 - c) * hs, hs)], cout[n], sems[0].at[4 * n], sems[1].at[4 * n], sib))
        for e in range(len(extra)):
            cps.append(_remote(cin[first_extra + e], cout[first_extra + e], sems[0].at[4 * n + 1 + e],
                               sems[1].at[4 * n + 1 + e], sib))
        return cps

    start = lambda cin, cout, sems: _start_all(copies(cin, cout, sems))
    finish = lambda cin, cout, sems: _wait_all(copies(cin, cout, sems))
    shapes = [jax.ShapeDtypeStruct((N_CHIPS,) + _HALF[nm], F32) for nm in names]
    ins = list(parts)
    if small is not None:
        shapes.append(jax.ShapeDtypeStruct((hs, 128), F32))
        ins.append(small)
    shapes += [jax.ShapeDtypeStruct(a.shape, a.dtype) for a in extra]
    ins += list(extra)
    n_sem = 4 * n + 1 + len(extra)
    return _Comm(ins, shapes, [_DMA((n_sem,)), _DMA((n_sem,))], start, finish)


def chip_exchange(sums, small=None):
    n = len(sums)

    def copies(cin, cout, sems, jj, c):
        out = []
        for k in _others(jj):
            for i in range(n):
                out.append(_remote(cin[i].at[k], cout[i].at[jj], sems[0].at[4 * i + k], sems[1].at[4 * i + jj], _core_of(k, c)))
            if small is not None:
                out.append(_remote(cin[n], cout[n].at[jj], sems[0].at[4 * n + k], sems[1].at[4 * n + jj], _core_of(k, c)))
        return out

    def start(cin, cout, sems):
        _, _, c, j = _place()

        def run(jj):
            if small is not None:
                pltpu.make_async_copy(cin[n], cout[n].at[jj], sems[2]).start()
            for cp in copies(cin, cout, sems, jj, c):
                cp.start()

        _as_chip(j, run)

    def finish(cin, cout, sems):
        _, _, c, j = _place()

        def run(jj):
            for k in _others(jj):
                for i in range(n):
                    _remote(cin[i].at[k], cout[i].at[k], sems[0].at[4 * i + k], sems[1].at[4 * i + k], _core_of(k, c)).wait_recv()
                if small is not None:
                    _remote(cin[n], cout[n].at[k], sems[0].at[4 * n + k], sems[1].at[4 * n + k], _core_of(k, c)).wait_recv()
            for cp in copies(cin, cout, sems, jj, c):
                cp.wait_send()
            if small is not None:
                pltpu.make_async_copy(cin[n], cout[n].at[jj], sems[2]).wait()

        _as_chip(j, run)

    shapes = [jax.ShapeDtypeStruct(a.shape, a.dtype) for a in sums]
    ins = list(sums)
    if small is not None:
        shapes.append(jax.ShapeDtypeStruct((N_CHIPS,) + small.shape, small.dtype))
        ins.append(small)
    return _Comm(ins, shapes, [_DMA((4 * n + 4,)), _DMA((4 * n + 4,)), _DMA(())], start, finish)


def pair_send_totals(arrs):
    n = len(arrs)

    def copies(cin, cout, sems):
        x, y, c, _ = _place()
        return [_remote(cin[i], cout[i], sems[0].at[i], sems[1].at[i], (x, y, 1 - c)) for i in range(n)]

    start = lambda cin, cout, sems: _start_all(copies(cin, cout, sems))
    finish = lambda cin, cout, sems: _wait_all(copies(cin, cout, sems))
    return _Comm(arrs, [jax.ShapeDtypeStruct(a.shape, a.dtype) for a in arrs], [_DMA((n,)), _DMA((n,))], start, finish)


def _pin(a):
    return pltpu.with_memory_space_constraint(a, pltpu.HBM)


def _plain_params(n_axes):
    return pltpu.CompilerParams(dimension_semantics=("arbitrary",) * n_axes, vmem_limit_bytes=V7X_VMEM_BYTES // 2)


def add_own_halves(names, parts, landed, core, name):
    n = len(names)

    def body(core_ref, *refs):
        for i in range(n):
            refs[2 * n + i][0] = (refs[i][...] + refs[n + i][0]).astype(MXU_DTYPE)

    in_specs, out_specs = [], []
    for nm in names:
        hr, hc = _HALF[nm]
        if nm in ("pa", "pb"):
            in_specs.append(pl.BlockSpec((hr // 2, hc), lambda k, s, cr: (cr[0] * 2 + s, k)))
        else:
            in_specs.append(pl.BlockSpec((hr // 2, hc), lambda k, s, cr: (k * 4 + cr[0] * 2 + s, 0)))
    for nm in names:
        hr, hc = _HALF[nm]
        in_specs.append(pl.BlockSpec((1, hr // 2, hc), lambda k, s, cr: (k, s, 0)))
        out_specs.append(pl.BlockSpec((1, hr // 2, hc), lambda k, s, cr: (k, s, 0)))
    return pl.pallas_call(
        body, name=name,
        grid_spec=pltpu.PrefetchScalarGridSpec(num_scalar_prefetch=1, grid=(N_CHIPS, 2), in_specs=in_specs, out_specs=out_specs),
        out_shape=[pltpu.HBM((N_CHIPS,) + _HALF[nm], MXU_DTYPE) for nm in names],
        compiler_params=_plain_params(2),
    )(core, *[_pin(a) for a in parts], *[_pin(a) for a in landed])


def add_small_half(small, landed, core):
    hs = SMALL_ROWS // 2

    def body(core_ref, a_ref, b_ref, o_ref):
        o_ref[...] = a_ref[...] + b_ref[...]

    return pl.pallas_call(
        body, name="add_small_half",
        grid_spec=pltpu.PrefetchScalarGridSpec(
            num_scalar_prefetch=1, grid=(1,),
            in_specs=[pl.BlockSpec((hs, 128), lambda s, cr: (cr[0], 0)), pl.BlockSpec((hs, 128), lambda s, cr: (0, 0))],
            out_specs=pl.BlockSpec((hs, 128), lambda s, cr: (0, 0))),
        out_shape=jax.ShapeDtypeStruct((hs, 128), F32),
    )(core, small, landed)


def sum_chips(sums, landed, chips, name, small_landed=None):
    n = len(sums)

    def body(chips_ref, *refs):
        ins, outs = refs[:4 * n + (small_landed is not None)], refs[4 * n + (small_landed is not None):]
        for i in range(n):
            tot = ins[4 * i][0].astype(F32)
            for r in range(1, N_CHIPS):
                tot = tot + ins[4 * i + r][0].astype(F32)
            outs[i][...] = tot
        if small_landed is not None:
            @pl.when(pl.program_id(0) == 0)
            def _():
                sm = ins[4 * n]
                outs[n][...] = ((sm[0] + sm[1]) + sm[2]) + sm[3]

    in_specs, out_specs, operands, shapes = [], [], [], []
    for a, l in zip(sums, landed):
        _, hr, hc = a.shape
        for r in range(N_CHIPS):
            in_specs.append(pl.BlockSpec((1, hr // 2, hc), functools.partial(lambda s, ch, r: (ch[r], s, 0), r=r)))
            operands.append(a if r == 0 else l)
        out_specs.append(pl.BlockSpec((hr // 2, hc), lambda s, ch: (s, 0)))
        shapes.append(jax.ShapeDtypeStruct((hr, hc), F32))
    if small_landed is not None:
        in_specs.append(pl.BlockSpec(small_landed.shape, lambda s, ch: (0, 0, 0)))
        out_specs.append(pl.BlockSpec(small_landed.shape[1:], lambda s, ch: (0, 0)))
        operands.append(small_landed)
        shapes.append(jax.ShapeDtypeStruct(small_landed.shape[1:], F32))
    return pl.pallas_call(
        body, name=name,
        grid_spec=pltpu.PrefetchScalarGridSpec(num_scalar_prefetch=1, grid=(2,), in_specs=in_specs, out_specs=out_specs),
        out_shape=[pltpu.HBM(s.shape, s.dtype) for s in shapes], compiler_params=_plain_params(1),
    )(chips, *[_pin(a) for a in operands])


N_STEPS = 8


def _adam_update(w, g, m, v):
    m = ADAM_B1 * m + (1.0 - ADAM_B1) * g
    v = ADAM_B2 * v + (1.0 - ADAM_B2) * (g * g)
    m_hat = m * (1.0 / (1.0 - ADAM_B1 ** ADAM_STEP))
    v_hat = v * (1.0 / (1.0 - ADAM_B2 ** ADAM_STEP))
    return -ADAM_LR * (m_hat / (jnp.sqrt(v_hat) + ADAM_EPS) + ADAM_WD * w), m, v


def adamw_big(ws, g_halves, ms, vs, core):
    n = len(ws)
    per_layer = N_STEPS // DEPTH
    per_half = per_layer // 2

    def body(core_ref, *refs):
        s = pl.program_id(0)
        first_layer = s < per_layer
        mine = ((s % per_layer) // per_half) == core_ref[0]
        for i in range(n):
            w, m, v = (refs[k * n + i][...] for k in range(3))
            g00, g01, g10, g11 = (refs[3 * n + 4 * i + k][...] for k in range(4))
            g = jnp.where(first_layer, jnp.where(mine, g00, g01), jnp.where(mine, g10, g11))[None]
            delta, m, v = _adam_update(w, g, m, v)
            for k, val in enumerate((g, delta, m, v)):
                refs[7 * n + k * n + i][...] = val

    whole = lambda a: pl.BlockSpec((1, a.shape[1] // per_layer, a.shape[2]),
                                   lambda s, cr: (s // per_layer, s % per_layer, 0))

    def half(a, layer, own):
        def index(s, cr):
            first = layer * per_layer + per_half * jnp.where(own, cr[0], 1 - cr[0])
            return (jnp.clip(s - first, 0, per_half - 1), 0)
        return pl.BlockSpec((a.shape[1] // per_layer, a.shape[2]), index)

    g_specs, g_ops = [], []
    for a, halves in zip(ws, g_halves):
        g_specs += [half(a, 0, True), half(a, 0, False), half(a, 1, True), half(a, 1, False)]
        g_ops += [_pin(a) for a in halves]
    outs = pl.pallas_call(
        body, name="adamw_big",
        grid_spec=pltpu.PrefetchScalarGridSpec(
            num_scalar_prefetch=1, grid=(N_STEPS,), in_specs=[whole(a) for a in ws] * 3 + g_specs,
            out_specs=[whole(a) for a in ws] * 4),
        out_shape=[pltpu.HBM(a.shape, F32) for a in ws] * 4,
        compiler_params=_layer_params(),
    )(core, *[_pin(a) for a in (*ws, *ms, *vs)], *g_ops)
    return outs[:n], outs[n:2 * n], outs[2 * n:3 * n], outs[3 * n:]


def adamw_small(ws, gs, ms, vs):
    n = len(ws)

    def body(*refs):
        for i in range(n):
            outs = _adam_update(*(refs[k * n + i][...] for k in range(4)))
            for k, val in enumerate(outs):
                refs[4 * n + k * n + i][...] = val

    specs = [_const_block(a.shape) for a in ws]
    outs = pl.pallas_call(
        body, name="adamw_small", grid=(1,), in_specs=specs * 4, out_specs=specs * 3,
        out_shape=[jax.ShapeDtypeStruct(a.shape, F32) for a in ws] * 3,
        compiler_params=_plain_params(1),
    )(*ws, *gs, *ms, *vs)
    return outs[:n], outs[n:2 * n], outs[2 * n:]


_SMALL = (("w_s", (DEPTH, SGU_G, BLK, BLK)), ("ln_in_g", (D_MODEL,)), ("ln_in_b", (D_MODEL,)), ("b_in", (DEPTH, N_COLS)),
          ("sinks", (DEPTH, 8)), ("vn_g", (DEPTH, SGU_W)), ("vn_b", (DEPTH, SGU_W)),
          ("b_s", (DEPTH, SGU_G, BLK)), ("b_out", (DEPTH, D_MODEL)), ("ln_g", (DEPTH, D_MODEL)), ("ln_b", (DEPTH, D_MODEL)))
_BIG = ("w_in", "p_a", "p_b", "w_out")


def _size(shape):
    n = 1
    for d in shape:
        n *= d
    return n


def _pack_small(vals, last_row=None):
    rows = []
    for name, shape in _SMALL:
        v = vals[name]
        if _size(shape) % 128:
            v = jnp.pad(v.reshape((1, -1)), ((0, 0), (0, (-_size(shape)) % 128)))
        rows.append(v.reshape((-1, 128)))
    used = sum(r.shape[0] for r in rows)
    tail = jnp.zeros((SMALL_ROWS - used, 128), F32)
    if last_row is not None:
        tail = tail.at[-1, 0].set(last_row)
    return jnp.concatenate([rows[0], jnp.concatenate(rows[1:] + [tail], axis=0)], axis=0)


def _unpack_small(packed):
    out, pos = {}, 0
    for name, shape in _SMALL:
        n = -(-_size(shape) // 128)
        rows = packed[pos:pos + n]
        out[name] = (rows.reshape((-1,))[:_size(shape)] if _size(shape) % 128 else rows).reshape(shape)
        pos += n
    return out


def kernel(x, ln_in_g, ln_in_b, w_in, b_in, sinks, vn_g, vn_b, w_s, b_s, p_a, p_b, w_out, b_out, ln_g, ln_b, loss_target, m_ln_in_g, m_ln_in_b, m_w_in, m_b_in, m_sinks, m_vn_g, m_vn_b, m_w_s, m_b_s, m_p_a, m_p_b, m_w_out, m_b_out, m_ln_g, m_ln_b, v_ln_in_g, v_ln_in_b, v_w_in, v_b_in, v_sinks, v_vn_g, v_vn_b, v_w_s, v_b_s, v_p_a, v_p_b, v_w_out, v_b_out, v_ln_g, v_ln_b):
    weights = dict(ln_in_g=ln_in_g, ln_in_b=ln_in_b, w_in=w_in, b_in=b_in, sinks=sinks, vn_g=vn_g, vn_b=vn_b, w_s=w_s,
                   b_s=b_s, p_a=p_a, p_b=p_b, w_out=w_out, b_out=b_out, ln_g=ln_g, ln_b=ln_b)
    mom1 = dict(ln_in_g=m_ln_in_g, ln_in_b=m_ln_in_b, w_in=m_w_in, b_in=m_b_in, sinks=m_sinks, vn_g=m_vn_g, vn_b=m_vn_b,
                w_s=m_w_s, b_s=m_b_s, p_a=m_p_a, p_b=m_p_b, w_out=m_w_out, b_out=m_b_out, ln_g=m_ln_g, ln_b=m_ln_b)
    mom2 = dict(ln_in_g=v_ln_in_g, ln_in_b=v_ln_in_b, w_in=v_w_in, b_in=v_b_in, sinks=v_sinks, vn_g=v_vn_g, vn_b=v_vn_b,
                w_s=v_w_s, b_s=v_b_s, p_a=v_p_a, p_b=v_p_b, w_out=v_w_out, b_out=v_b_out, ln_g=v_ln_g, ln_b=v_ln_b)
    n_seq, S, _ = x.shape
    T = n_seq * S
    c = lax.axis_index("c")
    j = 2 * lax.axis_index("x") + lax.axis_index("y")
    core = c.astype(jnp.int32).reshape((1,))
    chips = jnp.stack([j] + [r + (r >= j) for r in range(N_CHIPS - 1)]).astype(jnp.int32)
    names = list(_PIECES)
    xt, tt = x.reshape((T, D_MODEL)), loss_target.reshape((T, D_MODEL))

    tview = lambda d: dict(d, w_in=jnp.swapaxes(d["w_in"], 1, 2))
    weights_t, mom1_t, mom2_t = tview(weights), tview(mom1), tview(mom2)

    def own_shards(l):
        return [weights_t[n][l].astype(MXU_DTYPE) for n in _BIG]

    def layer_weights(l, gathered):
        d = dict(zip(_PIECES, gathered))
        d.update(paT=d["pa"].T, pbT=d["pb"].T, woutT=d["wout"].T)
        d.update(bin=b_in[l][None], sinks=sinks[l], vng=vn_g[l][None], vnb=vn_b[l][None], ws=w_s[l],
                 bs=jnp.repeat(b_s[l].T, 128, axis=1), bout=b_out[l][None], lng=ln_g[l][None], lnb=ln_b[l][None])
        return d

    own0, own1 = own_shards(0), own_shards(1)
    x0, g0 = input_norm_forward(xt, ln_in_g[None], ln_in_b[None], "ln_in_fwd", comm=gather_weights(names[:1], own0[:1]))
    n_proj = T // ROWS_PROJ
    h0, g1 = project(x0, g0[0], b_in[0][None], "layer0_proj",
                     comm=gather_weights(names[1:] + names[:1], own0[1:] + own1[:1], turn_step=(13 * n_proj) // 16))
    lw0 = layer_weights(0, g0 + g1[:len(names) - 1])
    x1, h1, g2 = layer_forward_and_projection(x0, h0, lw0, g1[-1], b_in[1][None], S, "layer0_fwd_layer1_proj",
                                              comm=gather_weights(names[1:], own1[1:], turn_step=T // ROWS // 2))
    lw1 = layer_weights(1, g1[-1:] + g2)

    dres1, dh1, acc1, _ = layer_backward(x1, h1, tt, lw1, S, True, "layer1_bwd")
    dx1, dwinT1, _ = input_grads(dres1, dh1, x1, lw1["winT"], "layer1_dx_dwin")
    parts1 = [dwinT1, acc1["dpa"], acc1["dpb"], acc1["dwout"]]
    dres0, dh0, acc0, landed1 = layer_backward(x0, h0, dx1, lw0, S, False, "layer0_bwd", comm=pair_send_halves(names, parts1))
    sums1 = add_own_halves(names, parts1, landed1, core, "add_own_halves1")
    dx0, dwinT0, from_chips1 = input_grads(dres0, dh0, x0, lw0["winT"], "layer0_dx_dwin", comm=chip_exchange(sums1))
    totals1 = sum_chips(sums1, from_chips1, chips, "sum_chips1")
    grad_x, d_ln_in_g, d_ln_in_b, _ = input_norm_backward(xt, dx0, ln_in_g, "ln_in_bwd")

    gl = [acc0, acc1]
    per_layer = lambda key, pick: jnp.stack([pick(gl[l][key]) for l in range(DEPTH)])
    row0 = lambda key: per_layer(key, lambda a: a[0])
    small_part = _pack_small(dict(
        ln_in_g=d_ln_in_g[0], ln_in_b=d_ln_in_b[0], b_in=row0("dbin"), sinks=per_layer("dsink", lambda a: a[:, 0]),
        vn_g=row0("dvng"), vn_b=row0("dvnb"), w_s=per_layer("dws", lambda a: a),
        b_s=per_layer("dbs", lambda a: a[:, ::128].T), b_out=row0("dbout"), ln_g=row0("dlng"), ln_b=row0("dlnb")),
        last_row=acc1["loss"][0, 0])
    parts0 = [dwinT0, acc0["dpa"], acc0["dpb"], acc0["dwout"]]
    first = _run_comm(pair_send_halves(names, parts0, small_part, extra=totals1), "pair_send_halves0")
    landed0, small_landed, sib_totals1 = first[:4], first[4], first[5:]
    sums0 = add_own_halves(names, parts0, landed0, core, "add_own_halves0")
    small_sum = add_small_half(small_part, small_landed, core)
    *from_chips0, small_from_chips = _run_comm(chip_exchange(sums0, small_sum), "chip_exchange0")
    *totals0, small_total = sum_chips(sums0, from_chips0, chips, "sum_chips0", small_landed=small_from_chips)
    *sib_totals0, sib_small = _run_comm(pair_send_totals(totals0 + [small_total]), "pair_send_totals0")

    def both_halves(mine, theirs):
        return jnp.where(c == 0, jnp.concatenate([mine, theirs], axis=0), jnp.concatenate([theirs, mine], axis=0))

    g_halves = [(totals0[pi], sib_totals0[pi], totals1[pi], sib_totals1[pi]) for pi in range(len(_BIG))]
    g_small = both_halves(small_total, sib_small)
    loss = g_small[-1, 0]

    big = adamw_big([weights_t[n] for n in _BIG], g_halves, [mom1_t[n] for n in _BIG], [mom2_t[n] for n in _BIG], core)
    small_names = [n for n, _ in _SMALL]
    as2d = lambda a: a.reshape((1, -1)) if a.ndim == 1 else a
    g_named = _unpack_small(g_small)
    small = adamw_small(*[[as2d(d[n]) for n in small_names] for d in (weights, g_named, mom1, mom2)])

    def named(big_outs, small_outs):
        d = dict(zip(_BIG, big_outs))
        d["w_in"] = jnp.swapaxes(d["w_in"], 1, 2)
        d.update({n: o.reshape(weights[n].shape) for n, o in zip(small_names, small_outs)})
        return d

    order = ("ln_in_g", "ln_in_b", "w_in", "b_in", "sinks", "vn_g", "vn_b", "w_s", "b_s", "p_a", "p_b", "w_out", "b_out",
             "ln_g", "ln_b")
    res = [loss, grad_x.reshape(x.shape)]
    groups = [named(big[0], [g_named[n] for n in small_names])] + [named(big[k + 1], small[k]) for k in range(3)]
    for group in groups:
        res.extend(group[n] for n in order)
    return tuple(res)
```

```python
import functools

import jax
import jax.numpy as jnp
from jax import lax
from jax.experimental import pallas as pl
from jax.experimental.pallas import tpu as pltpu

F32 = jnp.float32
MXU_DTYPE = jnp.bfloat16

D_MODEL = 1024
DEPTH = 2
HEAD_DIM = 64
ATTN_W = 512
KV_W = 128
BLK = 128
SGU_W = 512
SGU_G = 4
N_COLS = 4864
C_Q, C_K, C_V, C_GA, C_UB, C_VB, C_GB, C_RA, C_RB = 0, 512, 640, 768, 1280, 1792, 2304, 2816, 3840
ALPHA = (2.0 * DEPTH) ** 0.25
LN_EPS = 1e-5
SCALE = HEAD_DIM ** -0.5
NEG = float(jnp.finfo(jnp.float32).min)
GELU_C = 0.7978845608028654
GELU_A = 0.044715

ADAM_LR, ADAM_B1, ADAM_B2, ADAM_EPS, ADAM_WD, ADAM_STEP = 0.001, 0.9, 0.999, 1e-08, 0.01, 10

V7X_VMEM_BYTES = 64 * 1024 * 1024
V7X_VMEM_RESERVE = 3 * 1024 * 1024
ROWS = 256
ROWS_BWD = 256
ROWS_PROJ = 512

MESH = pl.DeviceIdType.MESH
N_CHIPS = 4


def _mm(a, b):
    return jnp.dot(a.astype(MXU_DTYPE), b.astype(MXU_DTYPE), preferred_element_type=F32)


def _mm_nt(a, b):
    return lax.dot_general(a.astype(MXU_DTYPE), b.astype(MXU_DTYPE), (((1,), (1,)), ((), ())),
                           preferred_element_type=F32)


def _mm_tn(a, b):
    return lax.dot_general(a.astype(MXU_DTYPE), b.astype(MXU_DTYPE), (((0,), (0,)), ((), ())),
                           preferred_element_type=F32)


def _sigmoid(x):
    return 0.5 * jnp.tanh(0.5 * x) + 0.5


def _gelu_parts(x):
    x2 = x * x
    u = 0.5 * jnp.tanh(x * (GELU_C + (GELU_C * GELU_A) * x2)) + 0.5
    dg = u * (1.0 + x * (1.0 - u) * (2.0 * GELU_C + (6.0 * GELU_C * GELU_A) * x2))
    return x * u, dg


def _ln_fwd(z, g, b):
    mu = jnp.mean(z, axis=-1, keepdims=True)
    zc = z - mu
    var = jnp.mean(zc * zc, axis=-1, keepdims=True)
    rstd = lax.rsqrt(var + LN_EPS)
    zhat = zc * rstd
    return zhat * g + b, zhat, rstd


def _ln_bwd(dy, zhat, rstd, g):
    dzh = dy * g
    m1 = jnp.mean(dzh, axis=-1, keepdims=True)
    m2 = jnp.mean(dzh * zhat, axis=-1, keepdims=True)
    return rstd * (dzh - m1 - zhat * m2)


def _colsum(v):
    return jnp.sum(v, axis=0, keepdims=True)


def _lane_lo(rows):
    return lax.broadcasted_iota(jnp.int32, (rows, 128), 1) < HEAD_DIM


def _dup_head(x2, hk, lo):
    xr = pltpu.roll(x2, HEAD_DIM, 1)
    return jnp.where(lo, x2, xr) if hk == 0 else jnp.where(lo, xr, x2)


def _fold_head(y0, y1, lo):
    f0 = y0 + pltpu.roll(y0, HEAD_DIM, 1)
    f1 = y1 + pltpu.roll(y1, HEAD_DIM, 1)
    return jnp.where(lo, f0, f1)


def _stack_heads(t0, t1, lo):
    z = jnp.zeros_like(t0)
    return jnp.concatenate([jnp.where(lo, t0, z), jnp.where(lo, z, t0),
                            jnp.where(lo, t1, z), jnp.where(lo, z, t1)], axis=0)


def _unstack_heads(o, lo):
    return (jnp.where(lo, o[0:128], o[128:256]), jnp.where(lo, o[256:384], o[384:512]))


def _band_bias(start):
    row = lax.broadcasted_iota(jnp.int32, (4 * BLK, 2 * BLK), 0) & (BLK - 1)
    kpos = lax.broadcasted_iota(jnp.int32, (4 * BLK, 2 * BLK), 1)
    valid = (kpos > row) & (kpos <= row + BLK)
    if start is not None:
        valid = valid & (jnp.logical_not(start) | (kpos >= BLK))
    return jnp.where(valid, 0.0, NEG)


def _attn_probs(s, sink4, bias):
    s = s * SCALE + bias
    r1 = lax.broadcasted_iota(jnp.int32, (4 * BLK, 1), 0)
    sk = jnp.where(r1 < BLK, sink4[0], jnp.where(r1 < 2 * BLK, sink4[1], jnp.where(r1 < 3 * BLK, sink4[2], sink4[3])))
    m = jnp.maximum(jnp.max(s, axis=-1, keepdims=True), sk)
    p = jnp.exp(s - m)
    es = jnp.exp(sk - m)
    inv = 1.0 / (jnp.sum(p, axis=-1, keepdims=True) + es)
    return p * inv, es * inv


_PROJ_CHUNK = 512


def _kv_blocks(kvh, h_s, nb):
    ks = [kvh[:, 0:KV_W]] + [h_s[n * BLK:(n + 1) * BLK, C_K:C_K + KV_W] for n in range(nb)]
    vs = [kvh[:, KV_W:2 * KV_W]] + [h_s[n * BLK:(n + 1) * BLK, C_V:C_V + KV_W] for n in range(nb)]
    return ks, vs


def _attn_operands(h_s, ks, vs, n, hk, lo, lo2):
    rows = slice(n * BLK, (n + 1) * BLK)
    kd = _dup_head(jnp.concatenate([ks[n], ks[n + 1]], axis=0), hk, lo2)
    vd = _dup_head(jnp.concatenate([vs[n], vs[n + 1]], axis=0), hk, lo2)
    c0 = C_Q + 2 * hk * 128
    qs = _stack_heads(h_s[rows, c0:c0 + 128], h_s[rows, c0 + 128:c0 + 256], lo)
    return qs, kd, vd


def _sgu_weights(w):
    tri = (lax.broadcasted_iota(jnp.int32, (BLK, BLK), 0) >= lax.broadcasted_iota(jnp.int32, (BLK, BLK), 1))
    return tri, [jnp.where(tri, w["ws"][g], 0.0) for g in range(SGU_G)]


def _layer_forward(x, kvh, start, w, h_s, attn_s, mix_s, keep, tick=lambda: None):
    R = x.shape[0]
    nb = R // BLK
    lo = _lane_lo(BLK)
    lo2 = _lane_lo(2 * BLK)
    ks, vs = _kv_blocks(kvh, h_s, nb)
    sinks = [w["sinks"][j] for j in range(8)]
    bias_first = _band_bias(start)
    bias_rest = _band_bias(None) if nb > 1 else None
    pairs = [(n, hk) for n in range(nb) for hk in range(2)]
    ops = [_attn_operands(h_s, ks, vs, n, hk, lo, lo2) for n, hk in pairs]
    scores = [_mm_nt(qs, kd) for qs, kd, _ in ops]
    probs_l = [_attn_probs(s, sinks[4 * hk:4 * hk + 4], bias_first if n == 0 else bias_rest)
               for s, (n, hk) in zip(scores, pairs)]
    outs = [_mm(p, vd) for (p, _), (_, _, vd) in zip(probs_l, ops)]
    for o, (n, hk) in zip(outs, pairs):
        rows = slice(n * BLK, (n + 1) * BLK)
        t0, t1 = _unstack_heads(o, lo)
        attn_s[rows, 2 * hk * 128:(2 * hk + 1) * 128] = t0
        attn_s[rows, (2 * hk + 1) * 128:(2 * hk + 2) * 128] = t1
        tick()
    ga = h_s[:, C_GA:C_GA + ATTN_W]
    ya = attn_s[...] * (ga * _sigmoid(ga))
    tick()
    gu, _ = _gelu_parts(h_s[:, C_UB:C_UB + SGU_W])
    tick()
    gv, _ = _gelu_parts(h_s[:, C_VB:C_VB + SGU_W])
    vn, _, _ = _ln_fwd(gv, w["vng"][...], w["vnb"][...])
    tick()
    _, wms = _sgu_weights(w)
    tiles = [(slice(n * BLK, (n + 1) * BLK), g, slice(g * 128, (g + 1) * 128)) for n in range(nb) for g in range(SGU_G)]
    mixes = [_mm(wms[g], vn[rows, cols]) for rows, g, cols in tiles]
    for m, (rows, g, cols) in zip(mixes, tiles):
        mix_s[rows, cols] = m + w["bs"][:, cols]
    gb = h_s[:, C_GB:C_GB + SGU_W]
    yb = gu * mix_s[...] * (gb * _sigmoid(gb))
    tick()
    a = _mm(ya, w["pa"][...])
    b = _mm(yb, w["pb"][...])
    tick()
    merged = _sigmoid(h_s[:, C_RA:C_RA + D_MODEL]) * a + _sigmoid(h_s[:, C_RB:C_RB + D_MODEL]) * b
    tick()
    out = _mm(merged, w["wout"][...]) + w["bout"][...]
    y, zhat, rstd = _ln_fwd(ALPHA * x + out, w["lng"][...], w["lnb"][...])
    if not keep:
        return y, None
    return y, dict(ks=ks, vs=vs, probs=probs_l, ya=ya, yb=yb, a=a, b=b, merged=merged, zhat=zhat, rstd=rstd)


def _dsilu(g, sg):
    return sg * (1.0 + g * (1.0 - sg))


_W_NAMES = ("sinks", "vng", "vnb", "ws", "bs", "pa", "pb", "wout", "bout", "lng", "lnb")
_WT_NAMES = ("paT", "pbT", "woutT")
_ACC_NAMES = ("dwout", "dpa", "dpb", "dbin", "dbout", "dlng", "dlnb", "dvng", "dvnb", "dws", "dbs", "dsink", "loss")


def _make_fwd_proj_body(R, S, nT):
    def body(x_ref, h_ref, kvh_ref, wn_ref, bn_ref, *rest):
        w = dict(zip(_W_NAMES, rest[:len(_W_NAMES)]))
        y_ref, hn_ref, attn_s, mix_s, yprev_s = rest[len(_W_NAMES):]

        @pl.when(pl.program_id(0) == 0)
        def _():
            yprev_s[...] = jnp.zeros(yprev_s.shape, yprev_s.dtype)

        t = jnp.minimum(pl.program_id(0), nT - 1)
        start = (t % (S // R)) == 0
        chunks = [(c0, min(c0 + _PROJ_CHUNK, N_COLS)) for c0 in range(0, N_COLS, _PROJ_CHUNK)]

        def tick():
            if chunks:
                c0, c1 = chunks.pop(0)
                hn_ref[:, c0:c1] = _mm_nt(yprev_s[...], wn_ref[c0:c1, :]) + bn_ref[:, c0:c1]

        y, _ = _layer_forward(x_ref[...], kvh_ref[...], start, w, h_ref, attn_s, mix_s, keep=False, tick=tick)
        while chunks:
            tick()
        y_ref[...] = y
        yprev_s[...] = y.astype(yprev_s.dtype)
    return body


def _make_bwd_body(R, S, nT, is_last):
    nb = R // BLK
    names = _W_NAMES + _WT_NAMES
    nw = len(names)

    def body(x_ref, h_s, kvh_ref, aux_ref, *rest):
        w = dict(zip(names, rest[:nw]))
        dres_ref, dh_ref = rest[nw:nw + 2]
        acc = dict(zip(_ACC_NAMES, rest[nw + 2:nw + 2 + len(_ACC_NAMES)]))
        attn_s, mix_s, ckv_s = rest[nw + 2 + len(_ACC_NAMES):]
        i = pl.program_id(0)
        start = ((nT - 1 - i) % (S // R)) == 0
        lo = _lane_lo(BLK)
        lo2 = _lane_lo(2 * BLK)

        @pl.when(i == 0)
        def _():
            for name in _ACC_NAMES:
                acc[name][...] = jnp.zeros(acc[name].shape, F32)
            ckv_s[...] = jnp.zeros(ckv_s.shape, F32)

        x = x_ref[...]
        y, sv = _layer_forward(x, kvh_ref[...], start, w, h_s, attn_s, mix_s, keep=True)
        if is_last:
            diff = y - aux_ref[...]
            part = 0.5 * jnp.sum(jnp.mean(diff * diff, axis=-1, keepdims=True), axis=0, keepdims=True)
            acc["loss"][...] += jnp.broadcast_to(part, acc["loss"].shape)
            dy = diff * (1.0 / D_MODEL)
        else:
            dy = aux_ref[...]

        def put(c0, val, rows=slice(None)):
            width = val.shape[1]
            dh_ref[rows, c0:c0 + width] = val.astype(dh_ref.dtype)
            acc["dbin"][:, c0:c0 + width] += _colsum(val)

        acc["dlng"][...] += _colsum(dy * sv["zhat"])
        acc["dlnb"][...] += _colsum(dy)
        dz = _ln_bwd(dy, sv["zhat"], sv["rstd"], w["lng"][...])
        dres_ref[...] = ALPHA * dz
        acc["dbout"][...] += _colsum(dz)
        acc["dwout"][...] += _mm_tn(sv["merged"], dz)
        dmerged = _mm(dz, w["woutT"][...])
        sa = _sigmoid(h_s[:, C_RA:C_RA + D_MODEL])
        da = dmerged * sa
        put(C_RA, da * sv["a"] * (1.0 - sa))
        sb = _sigmoid(h_s[:, C_RB:C_RB + D_MODEL])
        db = dmerged * sb
        put(C_RB, db * sv["b"] * (1.0 - sb))
        acc["dpa"][...] += _mm_tn(sv["ya"], da)
        acc["dpb"][...] += _mm_tn(sv["yb"], db)
        dya = _mm(da, w["paT"][...])
        dyb = _mm(db, w["pbT"][...])
        ga = h_s[:, C_GA:C_GA + ATTN_W]
        sga = _sigmoid(ga)
        put(C_GA, dya * attn_s[...] * _dsilu(ga, sga))
        attn_s[...] = dya * (ga * sga)
        ks, vs = sv["ks"], sv["vs"]
        pairs = [(n, hk) for n in range(nb) for hk in range(2)]
        ops = [_attn_operands(h_s, ks, vs, n, hk, lo, lo2) for n, hk in pairs]
        doss = [_stack_heads(attn_s[n * BLK:(n + 1) * BLK, 2 * hk * 128:(2 * hk + 1) * 128],
                             attn_s[n * BLK:(n + 1) * BLK, (2 * hk + 1) * 128:(2 * hk + 2) * 128], lo) for n, hk in pairs]
        dps = [_mm_nt(dos, vd) for dos, (_, _, vd) in zip(doss, ops)]
        gb = h_s[:, C_GB:C_GB + SGU_W]
        sgb = _sigmoid(gb)
        gu, dgu = _gelu_parts(h_s[:, C_UB:C_UB + SGU_W])
        mixed = mix_s[...]
        put(C_GB, dyb * (gu * mixed) * _dsilu(gb, sgb))
        dsgu = dyb * (gb * sgb)
        put(C_UB, dsgu * mixed * dgu)
        dmixed = dsgu * gu
        dsss = []
        for dp, (probs, ps), (n, hk) in zip(dps, sv["probs"], pairs):
            delta = jnp.sum(probs * dp, axis=-1, keepdims=True)
            dsk = -(ps * delta)
            for g in range(4):
                j = 4 * hk + g
                acc["dsink"][j:j + 1, :] += jnp.broadcast_to(_colsum(dsk[g * BLK:(g + 1) * BLK]), (1, 128))
            dsss.append(probs * (dp - delta) * SCALE)
        dqs = [_mm(dss, kd) for dss, (_, kd, _) in zip(dsss, ops)]
        ydks = [_mm_tn(dss, qs) for dss, (qs, _, _) in zip(dsss, ops)]
        ydvs = [_mm_tn(probs, dos) for (probs, _), dos in zip(sv["probs"], doss)]
        gv, dgv = _gelu_parts(h_s[:, C_VB:C_VB + SGU_W])
        vn, vhat, vrstd = _ln_fwd(gv, w["vng"][...], w["vnb"][...])
        tri, wms = _sgu_weights(w)
        tiles = [(slice(n * BLK, (n + 1) * BLK), g, slice(g * 128, (g + 1) * 128)) for n in range(nb) for g in range(SGU_G)]
        dw_parts = [_mm_nt(dmixed[rows, cols], vn[rows, cols]) for rows, g, cols in tiles]
        dvn_parts = [_mm_tn(wms[g], dmixed[rows, cols]) for rows, g, cols in tiles]
        for g in range(SGU_G):
            tot = dw_parts[g]
            for n in range(1, nb):
                tot = tot + dw_parts[n * SGU_G + g]
            acc["dws"][g] += jnp.where(tri, tot, 0.0)
        for part, (rows, g, cols) in zip(dvn_parts, tiles):
            mix_s[rows, cols] = part
        dbs_part = dmixed[0:BLK]
        for n in range(1, nb):
            dbs_part = dbs_part + dmixed[n * BLK:(n + 1) * BLK]
        acc["dbs"][...] += dbs_part
        for dq, (n, hk) in zip(dqs, pairs):
            q0, q1 = _unstack_heads(dq, lo)
            put(C_Q + 2 * hk * 128, q0, slice(n * BLK, (n + 1) * BLK))
            put(C_Q + (2 * hk + 1) * 128, q1, slice(n * BLK, (n + 1) * BLK))
        dks = [jnp.zeros((BLK, KV_W), F32) for _ in range(nb + 1)]
        dvs = [jnp.zeros((BLK, KV_W), F32) for _ in range(nb + 1)]
        for n in range(nb):
            dk2 = _fold_head(ydks[2 * n], ydks[2 * n + 1], lo2)
            dv2 = _fold_head(ydvs[2 * n], ydvs[2 * n + 1], lo2)
            dks[n] = dks[n] + dk2[0:BLK]
            dks[n + 1] = dks[n + 1] + dk2[BLK:2 * BLK]
            dvs[n] = dvs[n] + dv2[0:BLK]
            dvs[n + 1] = dvs[n + 1] + dv2[BLK:2 * BLK]
        dks[nb] = dks[nb] + ckv_s[:, 0:KV_W]
        dvs[nb] = dvs[nb] + ckv_s[:, KV_W:2 * KV_W]
        ckv_s[:, 0:KV_W] = dks[0]
        ckv_s[:, KV_W:2 * KV_W] = dvs[0]
        put(C_K, jnp.concatenate(dks[1:], axis=0))
        put(C_V, jnp.concatenate(dvs[1:], axis=0))
        dvn = mix_s[...]
        acc["dvng"][...] += _colsum(dvn * vhat)
        acc["dvnb"][...] += _colsum(dvn)
        put(C_VB, _ln_bwd(dvn, vhat, vrstd, w["vng"][...]) * dgv)

        @pl.when(i == nT - 1)
        def _():
            for g in range(SGU_G):
                cols = slice(g * 128, (g + 1) * 128)
                tot = jnp.sum(acc["dbs"][:, cols], axis=1, keepdims=True)
                acc["dbs"][:, cols] = jnp.broadcast_to(tot, (BLK, 128))

    return body


def _resident():
    return pl.BlockSpec(memory_space=pltpu.VMEM)


_BIG_RESIDENT = ("pa", "pb", "wout", "paT", "pbT", "woutT", "dwout", "dpa", "dpb")


def _const_block(shape):
    return pl.BlockSpec(tuple(shape), lambda *_: (0,) * len(shape))


def _operand_spec(name, shape):
    if name == "sinks":
        return pl.BlockSpec(memory_space=pltpu.SMEM)
    return _resident() if name in _BIG_RESIDENT else _const_block(shape)


def _weight_specs(names, wts):
    return [_operand_spec(n, wts[n].shape) for n in names]


def _nbytes(a):
    n = jnp.dtype(a.dtype).itemsize
    for d in a.shape:
        n *= d
    return n


def _layer_params(resident=()):
    scoped = V7X_VMEM_BYTES - V7X_VMEM_RESERVE - sum(_nbytes(a) for a in resident)
    return pltpu.CompilerParams(dimension_semantics=("arbitrary",), vmem_limit_bytes=scoped)


class _Comm:
    def __init__(self, ins, out_shapes, sems, start, finish, aliases=None, turn=None, turn_step=None):
        self.ins = [pltpu.with_memory_space_constraint(a, pltpu.HBM) for a in ins]
        self.out_shapes = [pltpu.HBM(s.shape, s.dtype) for s in out_shapes]
        self.sems = list(sems)
        self.start, self.finish, self.aliases = start, finish, dict(aliases or {})
        self.turn, self.turn_step = turn, turn_step


_ANY = pl.BlockSpec(memory_space=pltpu.HBM)
_DMA = pltpu.SemaphoreType.DMA


def _call(body, *, name, grid, in_specs, out_specs, out_shape, scratch_shapes, compiler_params, operands, comm=None):
    if comm is None:
        res = pl.pallas_call(body, name=name, grid=grid, in_specs=in_specs, out_specs=out_specs, out_shape=out_shape,
                             scratch_shapes=scratch_shapes, compiler_params=compiler_params)(*operands)
        return list(res), []
    n_in, n_out, n_scr = len(in_specs), len(out_specs), len(scratch_shapes)
    ci, co = len(comm.ins), len(comm.out_shapes)
    last = grid[0] - 1

    def hosted(*refs):
        ins, cin = refs[:n_in], refs[n_in:n_in + ci]
        p = n_in + ci
        outs, cout = refs[p:p + n_out], refs[p + n_out:p + n_out + co]
        p += n_out + co
        scr, sems = refs[p:p + n_scr], refs[p + n_scr:]
        pl.when(pl.program_id(0) == 0)(lambda: comm.start(cin, cout, sems))
        body(*ins, *outs, *scr)
        if comm.turn is not None:
            at = last if comm.turn_step is None else comm.turn_step
            pl.when(pl.program_id(0) == at)(lambda: comm.turn(cin, cout, sems))
        pl.when(pl.program_id(0) == last)(lambda: comm.finish(cin, cout, sems))

    res = pl.pallas_call(
        hosted, name=name, grid=grid, in_specs=list(in_specs) + [_ANY] * ci, out_specs=list(out_specs) + [_ANY] * co,
        out_shape=list(out_shape) + comm.out_shapes, scratch_shapes=list(scratch_shapes) + comm.sems,
        input_output_aliases={n_in + a: n_out + b for a, b in comm.aliases.items()},
        compiler_params=compiler_params)(*operands, *comm.ins)
    return list(res[:n_out]), list(res[n_out:])


def _run_comm(comm, name):
    ci, co = len(comm.ins), len(comm.out_shapes)

    def body(*refs):
        cin, cout, sems = refs[:ci], refs[ci:ci + co], refs[ci + co:]
        comm.start(cin, cout, sems)
        if comm.turn is not None:
            comm.turn(cin, cout, sems)
        comm.finish(cin, cout, sems)

    return list(pl.pallas_call(body, name=name, in_specs=[_ANY] * ci, out_specs=[_ANY] * co, out_shape=comm.out_shapes,
                               scratch_shapes=comm.sems, input_output_aliases=comm.aliases)(*comm.ins))


def project(x, winT, b, name, comm=None):
    T = x.shape[0]
    R = ROWS_PROJ

    def body(x_ref, w_ref, b_ref, h_ref):
        xb = x_ref[...].astype(MXU_DTYPE)
        for c0 in range(0, N_COLS, _PROJ_CHUNK):
            c1 = min(c0 + _PROJ_CHUNK, N_COLS)
            h_ref[:, c0:c1] = _mm_nt(xb, w_ref[c0:c1, :]) + b_ref[:, c0:c1]

    (h,), extra = _call(
        body, name=name, grid=(T // R,),
        in_specs=[pl.BlockSpec((R, D_MODEL), lambda t: (t, 0)), _resident(), _const_block(b.shape)],
        out_specs=[pl.BlockSpec((R, N_COLS), lambda t: (t, 0))],
        out_shape=[jax.ShapeDtypeStruct((T, N_COLS), F32)], scratch_shapes=[],
        compiler_params=_layer_params([winT]), operands=(x, winT, b), comm=comm)
    return h, extra


_KV_BLOCK = C_K // (2 * KV_W)


def layer_forward_and_projection(x, h, wts, winT_next, b_next, S, name, comm=None):
    T = x.shape[0]
    R = ROWS
    nT = T // R
    npt = R // BLK
    cur = lambda t: jnp.minimum(t, nT - 1)
    (y, hn), extra = _call(
        _make_fwd_proj_body(R, S, nT), name=name, grid=(nT + 1,),
        in_specs=[pl.BlockSpec((R, D_MODEL), lambda t: (cur(t), 0)), pl.BlockSpec((R, N_COLS), lambda t: (cur(t), 0)),
                  pl.BlockSpec((BLK, 2 * KV_W), lambda t: (jnp.maximum(cur(t) * npt - 1, 0), _KV_BLOCK)),
                  _resident(), _const_block(b_next.shape)] + _weight_specs(_W_NAMES, wts),
        out_specs=[pl.BlockSpec((R, D_MODEL), lambda t: (cur(t), 0)),
                   pl.BlockSpec((R, N_COLS), lambda t: (jnp.maximum(t - 1, 0), 0))],
        out_shape=[jax.ShapeDtypeStruct((T, D_MODEL), F32), jax.ShapeDtypeStruct((T, N_COLS), F32)],
        scratch_shapes=[pltpu.VMEM((R, ATTN_W), F32), pltpu.VMEM((R, SGU_W), F32), pltpu.VMEM((R, D_MODEL), MXU_DTYPE)],
        compiler_params=_layer_params([winT_next] + [wts[n] for n in _W_NAMES if n in _BIG_RESIDENT]),
        operands=(x, h, h, winT_next, b_next, *[wts[n] for n in _W_NAMES]), comm=comm)
    return y, hn, extra


_ACC_SHAPES = dict(dwout=(D_MODEL, D_MODEL), dpa=(ATTN_W, D_MODEL), dpb=(SGU_W, D_MODEL), dbin=(1, N_COLS),
                   dbout=(1, D_MODEL), dlng=(1, D_MODEL), dlnb=(1, D_MODEL), dvng=(1, SGU_W), dvnb=(1, SGU_W),
                   dws=(SGU_G, BLK, BLK), dbs=(BLK, SGU_W), dsink=(8, 128), loss=(8, 128))


def layer_backward(x, h, aux, wts, S, is_last, name, comm=None):
    T = x.shape[0]
    R = ROWS_BWD
    nT = T // R
    npt = R // BLK
    rev = lambda i: (nT - 1 - i, 0)
    halo = lambda i: (jnp.maximum((nT - 1 - i) * npt - 1, 0), _KV_BLOCK)
    names = _W_NAMES + _WT_NAMES
    out_shape = ([jax.ShapeDtypeStruct((T, D_MODEL), F32), jax.ShapeDtypeStruct((T, N_COLS), MXU_DTYPE)]
                 + [jax.ShapeDtypeStruct(_ACC_SHAPES[n], F32) for n in _ACC_NAMES])
    outs, extra = _call(
        _make_bwd_body(R, S, nT, is_last), name=name, grid=(nT,),
        in_specs=[pl.BlockSpec((R, D_MODEL), rev), pl.BlockSpec((R, N_COLS), rev), pl.BlockSpec((BLK, 2 * KV_W), halo),
                  pl.BlockSpec((R, D_MODEL), rev)] + _weight_specs(names, wts),
        out_specs=[pl.BlockSpec((R, D_MODEL), rev), pl.BlockSpec((R, N_COLS), rev)]
        + [_operand_spec(n, _ACC_SHAPES[n]) for n in _ACC_NAMES],
        out_shape=out_shape,
        scratch_shapes=[pltpu.VMEM((R, ATTN_W), F32), pltpu.VMEM((R, SGU_W), F32), pltpu.VMEM((BLK, 2 * KV_W), F32)],
        compiler_params=_layer_params([wts[n] for n in names if n in _BIG_RESIDENT]
                                      + [jax.ShapeDtypeStruct(_ACC_SHAPES[n], F32) for n in _ACC_NAMES if n in _BIG_RESIDENT]),
        operands=(x, h, h, aux, *[wts[n] for n in names]), comm=comm)
    return outs[0], outs[1], dict(zip(_ACC_NAMES, outs[2:])), extra


def input_grads(dres, dh, x, winT, name, comm=None):
    T = x.shape[0]
    R = ROWS

    def body(dres_ref, dh_ref, x_ref, winT_ref, dx_ref, dwin_ref):
        @pl.when(pl.program_id(0) == 0)
        def _():
            dwin_ref[...] = jnp.zeros(dwin_ref.shape, F32)

        dh = dh_ref[...]
        dx_ref[...] = dres_ref[...] + jnp.dot(dh, winT_ref[...], preferred_element_type=F32)
        dwin_ref[...] += _mm_tn(dh, x_ref[...])

    row = lambda t: (t, 0)
    dwin_shape = jax.ShapeDtypeStruct((N_COLS, D_MODEL), F32)
    (dx, dwinT), extra = _call(
        body, name=name, grid=(T // R,),
        in_specs=[pl.BlockSpec((R, D_MODEL), row), pl.BlockSpec((R, N_COLS), row), pl.BlockSpec((R, D_MODEL), row),
                  _resident()],
        out_specs=[pl.BlockSpec((R, D_MODEL), row), _resident()],
        out_shape=[jax.ShapeDtypeStruct((T, D_MODEL), F32), dwin_shape], scratch_shapes=[],
        compiler_params=_layer_params([winT, dwin_shape]), operands=(dres, dh, x, winT), comm=comm)
    return dx, dwinT, extra


def input_norm_forward(x, g, b, name, comm=None):
    T = x.shape[0]
    R = ROWS_PROJ

    def body(x_ref, g_ref, b_ref, y_ref):
        y_ref[...] = _ln_fwd(x_ref[...], g_ref[...], b_ref[...])[0]

    row = lambda t: (t, 0)
    (y,), extra = _call(
        body, name=name, grid=(T // R,),
        in_specs=[pl.BlockSpec((R, D_MODEL), row), _const_block(g.shape), _const_block(b.shape)],
        out_specs=[pl.BlockSpec((R, D_MODEL), row)],
        out_shape=[jax.ShapeDtypeStruct((T, D_MODEL), F32)], scratch_shapes=[],
        compiler_params=_layer_params(), operands=(x, g, b), comm=comm)
    return y, extra


def input_norm_backward(x, dy, g, name, comm=None):
    T = x.shape[0]
    R = ROWS_PROJ

    def body(x_ref, dy_ref, g_ref, dx_ref, dg_ref, db_ref):
        @pl.when(pl.program_id(0) == 0)
        def _():
            dg_ref[...] = jnp.zeros(dg_ref.shape, F32)
            db_ref[...] = jnp.zeros(db_ref.shape, F32)

        dy = dy_ref[...]
        _, xhat, rstd = _ln_fwd(x_ref[...], g_ref[...], g_ref[...])
        dx_ref[...] = _ln_bwd(dy, xhat, rstd, g_ref[...])
        dg_ref[...] += _colsum(dy * xhat)
        db_ref[...] += _colsum(dy)

    row = lambda t: (t, 0)
    vec = jax.ShapeDtypeStruct((1, D_MODEL), F32)
    (dx, dg, db), extra = _call(
        body, name=name, grid=(T // R,),
        in_specs=[pl.BlockSpec((R, D_MODEL), row), pl.BlockSpec((R, D_MODEL), row), _const_block(g.shape)],
        out_specs=[pl.BlockSpec((R, D_MODEL), row), _const_block(vec.shape), _const_block(vec.shape)],
        out_shape=[jax.ShapeDtypeStruct((T, D_MODEL), F32), vec, vec], scratch_shapes=[],
        compiler_params=_layer_params(), operands=(x, dy, g), comm=comm)
    return dx, dg, db, extra


_PIECES = ("winT", "pa", "pb", "wout")
_WHOLE = dict(winT=(N_COLS, D_MODEL), pa=(ATTN_W, D_MODEL), pb=(SGU_W, D_MODEL), wout=(D_MODEL, D_MODEL))
_HALF = dict(winT=(N_COLS // 8, D_MODEL), pa=(ATTN_W // 2, D_MODEL // 4), pb=(SGU_W // 2, D_MODEL // 4),
             wout=(D_MODEL // 8, D_MODEL))
SMALL_ROWS = 1280


def _region(name, ref, k, h):
    hr, hc = _HALF[name]
    if name in ("pa", "pb"):
        return ref.at[pl.ds(h * hr, hr), pl.ds(k * hc, hc)]
    return ref.at[pl.ds(k * 2 * hr + h * hr, hr), :]


def _place():
    x, y, c = lax.axis_index("x"), lax.axis_index("y"), lax.axis_index("c")
    return x, y, c, 2 * x + y


def _as_chip(j, fn):
    for jj in range(N_CHIPS):
        pl.when(j == jj)(functools.partial(fn, jj))


def _remote(src, dst, send_sem, recv_sem, to):
    return pltpu.make_async_remote_copy(src_ref=src, dst_ref=dst, send_sem=send_sem, recv_sem=recv_sem,
                                        device_id=to, device_id_type=MESH)


def _core_of(k, c):
    return (k // 2, k % 2, c)


def _others(jj):
    return [k for k in range(N_CHIPS) if k != jj]


def _start_all(cps):
    for cp in cps:
        cp.start()


def _wait_all(cps):
    for cp in cps:
        cp.wait()


def gather_over_chips(names, shards, on_landed=None):
    n = len(names)

    def own(cin, cout, sems, jj):
        x, y, c, _ = _place()
        cps = []
        for i, name in enumerate(names):
            hr, hc = _HALF[name]
            place = (cout[i].at[:, pl.ds(jj * hc, hc)] if name in ("pa", "pb")
                     else cout[i].at[pl.ds(jj * 2 * hr, 2 * hr), :])
            cps.append(_remote(cin[i], place, sems[2].at[i], sems[3].at[i], (x, y, 1 - c)))
        return cps

    def copies(cin, cout, sems, jj, c):
        out = []
        for i, name in enumerate(names):
            hr = _HALF[name][0]
            for k in _others(jj):
                out.append(_remote(cin[i].at[pl.ds(c * hr, hr)], _region(name, cout[i], jj, c),
                                   sems[0].at[4 * i + k], sems[1].at[4 * i + jj], _core_of(k, c)))
        return out

    def start(cin, cout, sems):
        _, _, c, j = _place()
        _as_chip(j, lambda jj: _start_all(copies(cin, cout, sems, jj, c) + own(cin, cout, sems, jj)))

    def finish(cin, cout, sems):
        _, _, c, j = _place()

        def run(jj):
            for i, name in enumerate(names):
                for k in _others(jj):
                    land = _region(name, cout[i], k, c)
                    _remote(land, land, sems[0].at[4 * i + k], sems[1].at[4 * i + k], _core_of(k, c)).wait_recv()
                    if on_landed is not None:
                        on_landed(land, 4 * i + k, sems)
            for cp in copies(cin, cout, sems, jj, c):
                cp.wait_send()
            _wait_all(own(cin, cout, sems, jj))

        _as_chip(j, run)

    shapes = [jax.ShapeDtypeStruct(_WHOLE[nm], s.dtype) for nm, s in zip(names, shards)]
    return _Comm(shards, shapes, [_DMA((4 * n,)), _DMA((4 * n,)), _DMA((n,)), _DMA((n,))], start, finish)


def gather_weights(names, shards, turn_step=None):
    n = len(names)

    def pass_on(land, slot, sems):
        x, y, c, _ = _place()
        _remote(land, land, sems[4].at[slot], sems[5].at[slot], (x, y, 1 - c)).start()

    chips = gather_over_chips(names, shards, on_landed=pass_on)
    pair_finish = _pair_finish(names)
    return _Comm(shards, chips.out_shapes, chips.sems + [_DMA((4 * n,)), _DMA((4 * n,))], chips.start,
                 lambda cin, cout, sems: pair_finish(cin, cout, sems[4:]), turn=chips.finish, turn_step=turn_step)


def _pair_finish(names):
    def finish(cin, cout, sems):
        x, y, c, j = _place()

        def run(jj):
            for i, name in enumerate(names):
                for k in _others(jj):
                    theirs = _region(name, cout[i], k, 1 - c)
                    _remote(theirs, theirs, sems[0].at[4 * i + k], sems[1].at[4 * i + k], (x, y, 1 - c)).wait_recv()
            for i, name in enumerate(names):
                for k in _others(jj):
                    land = _region(name, cout[i], k, c)
                    _remote(land, land, sems[0].at[4 * i + k], sems[1].at[4 * i + k], (x, y, 1 - c)).wait_send()

        _as_chip(j, run)

    return finish


def pair_send_halves(names, parts, small=None, extra=()):
    n = len(names)
    hs = SMALL_ROWS // 2
    first_extra = n + (small is not None)

    def copies(cin, cout, sems):
        x, y, c, _ = _place()
        sib = (x, y, 1 - c)
        cps = []
        for i, name in enumerate(names):
            for k in range(N_CHIPS):
                cps.append(_remote(_region(name, cin[i], k, 1 - c), cout[i].at[k], sems[0].at[4 * i + k],
                                   sems[1].at[4 * i + k], sib))
        if small is not None:
            cps.append(_remote(cin[n].at[pl.ds((1 - c) * hs, hs)], cout[n], sems[0].at[4 * n], sems[1].at[4 * n], sib))
        for e in range(len(extra)):
            cps.append(_remote(cin[first_extra + e], cout[first_extra + e], sems[0].at[4 * n + 1 + e],
                               sems[1].at[4 * n + 1 + e], sib))
        return cps

    start = lambda cin, cout, sems: _start_all(copies(cin, cout, sems))
    finish = lambda cin, cout, sems: _wait_all(copies(cin, cout, sems))
    shapes = [jax.ShapeDtypeStruct((N_CHIPS,) + _HALF[nm], F32) for nm in names]
    ins = list(parts)
    if small is not None:
        shapes.append(jax.ShapeDtypeStruct((hs, 128), F32))
        ins.append(small)
    shapes += [jax.ShapeDtypeStruct(a.shape, a.dtype) for a in extra]
    ins += list(extra)
    n_sem = 4 * n + 1 + len(extra)
    return _Comm(ins, shapes, [_DMA((n_sem,)), _DMA((n_sem,))], start, finish)


def chip_exchange(sums, small=None):
    n = len(sums)

    def copies(cin, cout, sems, jj, c):
        out = []
        for k in _others(jj):
            for i in range(n):
                out.append(_remote(cin[i].at[k], cout[i].at[jj], sems[0].at[4 * i + k], sems[1].at[4 * i + jj], _core_of(k, c)))
            if small is not None:
                out.append(_remote(cin[n], cout[n].at[jj], sems[0].at[4 * n + k], sems[1].at[4 * n + jj], _core_of(k, c)))
        return out

    def start(cin, cout, sems):
        _, _, c, j = _place()

        def run(jj):
            if small is not None:
                pltpu.make_async_copy(cin[n], cout[n].at[jj], sems[2]).start()
            for cp in copies(cin, cout, sems, jj, c):
                cp.start()

        _as_chip(j, run)

    def finish(cin, cout, sems):
        _, _, c, j = _place()

        def run(jj):
            for k in _others(jj):
                for i in range(n):
                    _remote(cin[i].at[k], cout[i].at[k], sems[0].at[4 * i + k], sems[1].at[4 * i + k], _core_of(k, c)).wait_recv()
                if small is not None:
                    _remote(cin[n], cout[n].at[k], sems[0].at[4 * n + k], sems[1].at[4 * n + k], _core_of(k, c)).wait_recv()
            for cp in copies(cin, cout, sems, jj, c):
                cp.wait_send()
            if small is not None:
                pltpu.make_async_copy(cin[n], cout[n].at[jj], sems[2]).wait()

        _as_chip(j, run)

    shapes = [jax.ShapeDtypeStruct(a.shape, a.dtype) for a in sums]
    ins = list(sums)
    if small is not None:
        shapes.append(jax.ShapeDtypeStruct((N_CHIPS,) + small.shape, small.dtype))
        ins.append(small)
    return _Comm(ins, shapes, [_DMA((4 * n + 4,)), _DMA((4 * n + 4,)), _DMA(())], start, finish)


def pair_send_totals(arrs):
    n = len(arrs)

    def copies(cin, cout, sems):
        x, y, c, _ = _place()
        return [_remote(cin[i], cout[i], sems[0].at[i], sems[1].at[i], (x, y, 1 - c)) for i in range(n)]

    start = lambda cin, cout, sems: _start_all(copies(cin, cout, sems))
    finish = lambda cin, cout, sems: _wait_all(copies(cin, cout, sems))
    return _Comm(arrs, [jax.ShapeDtypeStruct(a.shape, a.dtype) for a in arrs], [_DMA((n,)), _DMA((n,))], start, finish)


def _pin(a):
    return pltpu.with_memory_space_constraint(a, pltpu.HBM)


def _plain_params(n_axes):
    return pltpu.CompilerParams(dimension_semantics=("arbitrary",) * n_axes, vmem_limit_bytes=V7X_VMEM_BYTES // 2)


def add_own_halves(names, parts, landed, core, name):
    n = len(names)

    def body(core_ref, *refs):
        for i in range(n):
            refs[2 * n + i][0] = (refs[i][...] + refs[n + i][0]).astype(MXU_DTYPE)

    in_specs, out_specs = [], []
    for nm in names:
        hr, hc = _HALF[nm]
        if nm in ("pa", "pb"):
            in_specs.append(pl.BlockSpec((hr // 2, hc), lambda k, s, cr: (cr[0] * 2 + s, k)))
        else:
            in_specs.append(pl.BlockSpec((hr // 2, hc), lambda k, s, cr: (k * 4 + cr[0] * 2 + s, 0)))
    for nm in names:
        hr, hc = _HALF[nm]
        in_specs.append(pl.BlockSpec((1, hr // 2, hc), lambda k, s, cr: (k, s, 0)))
        out_specs.append(pl.BlockSpec((1, hr // 2, hc), lambda k, s, cr: (k, s, 0)))
    return pl.pallas_call(
        body, name=name,
        grid_spec=pltpu.PrefetchScalarGridSpec(num_scalar_prefetch=1, grid=(N_CHIPS, 2), in_specs=in_specs, out_specs=out_specs),
        out_shape=[pltpu.HBM((N_CHIPS,) + _HALF[nm], MXU_DTYPE) for nm in names],
        compiler_params=_plain_params(2),
    )(core, *[_pin(a) for a in parts], *[_pin(a) for a in landed])


def add_small_half(small, landed, core):
    hs = SMALL_ROWS // 2

    def body(core_ref, a_ref, b_ref, o_ref):
        o_ref[...] = a_ref[...] + b_ref[...]

    return pl.pallas_call(
        body, name="add_small_half",
        grid_spec=pltpu.PrefetchScalarGridSpec(
            num_scalar_prefetch=1, grid=(1,),
            in_specs=[pl.BlockSpec((hs, 128), lambda s, cr: (cr[0], 0)), pl.BlockSpec((hs, 128), lambda s, cr: (0, 0))],
            out_specs=pl.BlockSpec((hs, 128), lambda s, cr: (0, 0))),
        out_shape=jax.ShapeDtypeStruct((hs, 128), F32),
    )(core, small, landed)


def sum_chips(sums, landed, chips, name, small_landed=None):
    n = len(sums)

    def body(chips_ref, *refs):
        ins, outs = refs[:4 * n + (small_landed is not None)], refs[4 * n + (small_landed is not None):]
        for i in range(n):
            tot = ins[4 * i][0].astype(F32)
            for r in range(1, N_CHIPS):
                tot = tot + ins[4 * i + r][0].astype(F32)
            outs[i][...] = tot
        if small_landed is not None:
            @pl.when(pl.program_id(0) == 0)
            def _():
                sm = ins[4 * n]
                outs[n][...] = ((sm[0] + sm[1]) + sm[2]) + sm[3]

    in_specs, out_specs, operands, shapes = [], [], [], []
    for a, l in zip(sums, landed):
        _, hr, hc = a.shape
        for r in range(N_CHIPS):
            in_specs.append(pl.BlockSpec((1, hr // 2, hc), functools.partial(lambda s, ch, r: (ch[r], s, 0), r=r)))
            operands.append(a if r == 0 else l)
        out_specs.append(pl.BlockSpec((hr // 2, hc), lambda s, ch: (s, 0)))
        shapes.append(jax.ShapeDtypeStruct((hr, hc), F32))
    if small_landed is not None:
        in_specs.append(pl.BlockSpec(small_landed.shape, lambda s, ch: (0, 0, 0)))
        out_specs.append(pl.BlockSpec(small_landed.shape[1:], lambda s, ch: (0, 0)))
        operands.append(small_landed)
        shapes.append(jax.ShapeDtypeStruct(small_landed.shape[1:], F32))
    return pl.pallas_call(
        body, name=name,
        grid_spec=pltpu.PrefetchScalarGridSpec(num_scalar_prefetch=1, grid=(2,), in_specs=in_specs, out_specs=out_specs),
        out_shape=[pltpu.HBM(s.shape, s.dtype) for s in shapes], compiler_params=_plain_params(1),
    )(chips, *[_pin(a) for a in operands])


N_STEPS = 8


def _adam_update(w, g, m, v):
    m = ADAM_B1 * m + (1.0 - ADAM_B1) * g
    v = ADAM_B2 * v + (1.0 - ADAM_B2) * (g * g)
    m_hat = m * (1.0 / (1.0 - ADAM_B1 ** ADAM_STEP))
    v_hat = v * (1.0 / (1.0 - ADAM_B2 ** ADAM_STEP))
    return -ADAM_LR * (m_hat / (jnp.sqrt(v_hat) + ADAM_EPS) + ADAM_WD * w), m, v


def adamw_big(ws, g_halves, ms, vs, core):
    n = len(ws)
    per_layer = N_STEPS // DEPTH
    per_half = per_layer // 2

    def body(core_ref, *refs):
        s = pl.program_id(0)
        first_layer = s < per_layer
        mine = ((s % per_layer) // per_half) == core_ref[0]
        for i in range(n):
            w, m, v = (refs[k * n + i][...] for k in range(3))
            g00, g01, g10, g11 = (refs[3 * n + 4 * i + k][...] for k in range(4))
            g = jnp.where(first_layer, jnp.where(mine, g00, g01), jnp.where(mine, g10, g11))[None]
            delta, m, v = _adam_update(w, g, m, v)
            for k, val in enumerate((g, delta, m, v)):
                refs[7 * n + k * n + i][...] = val

    whole = lambda a: pl.BlockSpec((1, a.shape[1] // per_layer, a.shape[2]),
                                   lambda s, cr: (s // per_layer, s % per_layer, 0))

    def half(a, layer, own):
        def index(s, cr):
            first = layer * per_layer + per_half * jnp.where(own, cr[0], 1 - cr[0])
            return (jnp.clip(s - first, 0, per_half - 1), 0)
        return pl.BlockSpec((a.shape[1] // per_layer, a.shape[2]), index)

    g_specs, g_ops = [], []
    for a, halves in zip(ws, g_halves):
        g_specs += [half(a, 0, True), half(a, 0, False), half(a, 1, True), half(a, 1, False)]
        g_ops += [_pin(a) for a in halves]
    outs = pl.pallas_call(
        body, name="adamw_big",
        grid_spec=pltpu.PrefetchScalarGridSpec(
            num_scalar_prefetch=1, grid=(N_STEPS,), in_specs=[whole(a) for a in ws] * 3 + g_specs,
            out_specs=[whole(a) for a in ws] * 4),
        out_shape=[pltpu.HBM(a.shape, F32) for a in ws] * 4,
        compiler_params=_layer_params(),
    )(core, *[_pin(a) for a in (*ws, *ms, *vs)], *g_ops)
    return outs[:n], outs[n:2 * n], outs[2 * n:3 * n], outs[3 * n:]


def adamw_small(ws, gs, ms, vs):
    n = len(ws)

    def body(*refs):
        for i in range(n):
            outs = _adam_update(*(refs[k * n + i][...] for k in range(4)))
            for k, val in enumerate(outs):
                refs[4 * n + k * n + i][...] = val

    specs = [_const_block(a.shape) for a in ws]
    outs = pl.pallas_call(
        body, name="adamw_small", grid=(1,), in_specs=specs * 4, out_specs=specs * 3,
        out_shape=[jax.ShapeDtypeStruct(a.shape, F32) for a in ws] * 3,
        compiler_params=_plain_params(1),
    )(*ws, *gs, *ms, *vs)
    return outs[:n], outs[n:2 * n], outs[2 * n:]


_SMALL = (("w_s", (DEPTH, SGU_G, BLK, BLK)), ("ln_in_g", (D_MODEL,)), ("ln_in_b", (D_MODEL,)), ("b_in", (DEPTH, N_COLS)),
          ("sinks", (DEPTH, 8)), ("vn_g", (DEPTH, SGU_W)), ("vn_b", (DEPTH, SGU_W)),
          ("b_s", (DEPTH, SGU_G, BLK)), ("b_out", (DEPTH, D_MODEL)), ("ln_g", (DEPTH, D_MODEL)), ("ln_b", (DEPTH, D_MODEL)))
_BIG = ("w_in", "p_a", "p_b", "w_out")


def _size(shape):
    n = 1
    for d in shape:
        n *= d
    return n


def _pack_small(vals, last_row=None):
    rows = []
    for name, shape in _SMALL:
        v = vals[name]
        if _size(shape) % 128:
            v = jnp.pad(v.reshape((1, -1)), ((0, 0), (0, (-_size(shape)) % 128)))
        rows.append(v.reshape((-1, 128)))
    used = sum(r.shape[0] for r in rows)
    tail = jnp.zeros((SMALL_ROWS - used, 128), F32)
    if last_row is not None:
        tail = tail.at[-1, 0].set(last_row)
    return jnp.concatenate([rows[0], jnp.concatenate(rows[1:] + [tail], axis=0)], axis=0)


def _unpack_small(packed):
    out, pos = {}, 0
    for name, shape in _SMALL:
        n = -(-_size(shape) // 128)
        rows = packed[pos:pos + n]
        out[name] = (rows.reshape((-1,))[:_size(shape)] if _size(shape) % 128 else rows).reshape(shape)
        pos += n
    return out


def kernel(x, ln_in_g, ln_in_b, w_in, b_in, sinks, vn_g, vn_b, w_s, b_s, p_a, p_b, w_out, b_out, ln_g, ln_b, loss_target, m_ln_in_g, m_ln_in_b, m_w_in, m_b_in, m_sinks, m_vn_g, m_vn_b, m_w_s, m_b_s, m_p_a, m_p_b, m_w_out, m_b_out, m_ln_g, m_ln_b, v_ln_in_g, v_ln_in_b, v_w_in, v_b_in, v_sinks, v_vn_g, v_vn_b, v_w_s, v_b_s, v_p_a, v_p_b, v_w_out, v_b_out, v_ln_g, v_ln_b):
    weights = dict(ln_in_g=ln_in_g, ln_in_b=ln_in_b, w_in=w_in, b_in=b_in, sinks=sinks, vn_g=vn_g, vn_b=vn_b, w_s=w_s,
                   b_s=b_s, p_a=p_a, p_b=p_b, w_out=w_out, b_out=b_out, ln_g=ln_g, ln_b=ln_b)
    mom1 = dict(ln_in_g=m_ln_in_g, ln_in_b=m_ln_in_b, w_in=m_w_in, b_in=m_b_in, sinks=m_sinks, vn_g=m_vn_g, vn_b=m_vn_b,
                w_s=m_w_s, b_s=m_b_s, p_a=m_p_a, p_b=m_p_b, w_out=m_w_out, b_out=m_b_out, ln_g=m_ln_g, ln_b=m_ln_b)
    mom2 = dict(ln_in_g=v_ln_in_g, ln_in_b=v_ln_in_b, w_in=v_w_in, b_in=v_b_in, sinks=v_sinks, vn_g=v_vn_g, vn_b=v_vn_b,
                w_s=v_w_s, b_s=v_b_s, p_a=v_p_a, p_b=v_p_b, w_out=v_w_out, b_out=v_b_out, ln_g=v_ln_g, ln_b=v_ln_b)
    n_seq, S, _ = x.shape
    T = n_seq * S
    c = lax.axis_index("c")
    j = 2 * lax.axis_index("x") + lax.axis_index("y")
    core = c.astype(jnp.int32).reshape((1,))
    chips = jnp.stack([j] + [r + (r >= j) for r in range(N_CHIPS - 1)]).astype(jnp.int32)
    names = list(_PIECES)
    xt, tt = x.reshape((T, D_MODEL)), loss_target.reshape((T, D_MODEL))

    tview = lambda d: dict(d, w_in=jnp.swapaxes(d["w_in"], 1, 2))
    weights_t, mom1_t, mom2_t = tview(weights), tview(mom1), tview(mom2)

    def own_shards(l):
        return [weights_t[n][l].astype(MXU_DTYPE) for n in _BIG]

    def layer_weights(l, gathered):
        d = dict(zip(_PIECES, gathered))
        d.update(paT=d["pa"].T, pbT=d["pb"].T, woutT=d["wout"].T)
        d.update(bin=b_in[l][None], sinks=sinks[l], vng=vn_g[l][None], vnb=vn_b[l][None], ws=w_s[l],
                 bs=jnp.repeat(b_s[l].T, 128, axis=1), bout=b_out[l][None], lng=ln_g[l][None], lnb=ln_b[l][None])
        return d

    own0, own1 = own_shards(0), own_shards(1)
    x0, g0 = input_norm_forward(xt, ln_in_g[None], ln_in_b[None], "ln_in_fwd", comm=gather_weights(names[:1], own0[:1]))
    n_proj = T // ROWS_PROJ
    h0, g1 = project(x0, g0[0], b_in[0][None], "layer0_proj",
                     comm=gather_weights(names[1:] + names[:1], own0[1:] + own1[:1], turn_step=(13 * n_proj) // 16))
    lw0 = layer_weights(0, g0 + g1[:len(names) - 1])
    x1, h1, g2 = layer_forward_and_projection(x0, h0, lw0, g1[-1], b_in[1][None], S, "layer0_fwd_layer1_proj",
                                              comm=gather_weights(names[1:], own1[1:], turn_step=T // ROWS // 2))
    lw1 = layer_weights(1, g1[-1:] + g2)

    dres1, dh1, acc1, _ = layer_backward(x1, h1, tt, lw1, S, True, "layer1_bwd")
    dx1, dwinT1, _ = input_grads(dres1, dh1, x1, lw1["winT"], "layer1_dx_dwin")
    parts1 = [dwinT1, acc1["dpa"], acc1["dpb"], acc1["dwout"]]
    dres0, dh0, acc0, landed1 = layer_backward(x0, h0, dx1, lw0, S, False, "layer0_bwd", comm=pair_send_halves(names, parts1))
    sums1 = add_own_halves(names, parts1, landed1, core, "add_own_halves1")
    dx0, dwinT0, from_chips1 = input_grads(dres0, dh0, x0, lw0["winT"], "layer0_dx_dwin", comm=chip_exchange(sums1))
    totals1 = sum_chips(sums1, from_chips1, chips, "sum_chips1")
    grad_x, d_ln_in_g, d_ln_in_b, _ = input_norm_backward(xt, dx0, ln_in_g, "ln_in_bwd")

    gl = [acc0, acc1]
    per_layer = lambda key, pick: jnp.stack([pick(gl[l][key]) for l in range(DEPTH)])
    row0 = lambda key: per_layer(key, lambda a: a[0])
    small_part = _pack_small(dict(
        ln_in_g=d_ln_in_g[0], ln_in_b=d_ln_in_b[0], b_in=row0("dbin"), sinks=per_layer("dsink", lambda a: a[:, 0]),
        vn_g=row0("dvng"), vn_b=row0("dvnb"), w_s=per_layer("dws", lambda a: a),
        b_s=per_layer("dbs", lambda a: a[:, ::128].T), b_out=row0("dbout"), ln_g=row0("dlng"), ln_b=row0("dlnb")),
        last_row=acc1["loss"][0, 0])
    parts0 = [dwinT0, acc0["dpa"], acc0["dpb"], acc0["dwout"]]
    first = _run_comm(pair_send_halves(names, parts0, small_part, extra=totals1), "pair_send_halves0")
    landed0, small_landed, sib_totals1 = first[:4], first[4], first[5:]
    sums0 = add_own_halves(names, parts0, landed0, core, "add_own_halves0")
    small_sum = add_small_half(small_part, small_landed, core)
    *from_chips0, small_from_chips = _run_comm(chip_exchange(sums0, small_sum), "chip_exchange0")
    *totals0, small_total = sum_chips(sums0, from_chips0, chips, "sum_chips0", small_landed=small_from_chips)
    *sib_totals0, sib_small = _run_comm(pair_send_totals(totals0 + [small_total]), "pair_send_totals0")

    def both_halves(mine, theirs):
        return jnp.where(c == 0, jnp.concatenate([mine, theirs], axis=0), jnp.concatenate([theirs, mine], axis=0))

    g_halves = [(totals0[pi], sib_totals0[pi], totals1[pi], sib_totals1[pi]) for pi in range(len(_BIG))]
    g_small = both_halves(small_total, sib_small)
    loss = g_small[-1, 0]

    big = adamw_big([weights_t[n] for n in _BIG], g_halves, [mom1_t[n] for n in _BIG], [mom2_t[n] for n in _BIG], core)
    small_names = [n for n, _ in _SMALL]
    as2d = lambda a: a.reshape((1, -1)) if a.ndim == 1 else a
    g_named = _unpack_small(g_small)
    small = adamw_small(*[[as2d(d[n]) for n in small_names] for d in (weights, g_named, mom1, mom2)])

    def named(big_outs, small_outs):
        d = dict(zip(_BIG, big_outs))
        d["w_in"] = jnp.swapaxes(d["w_in"], 1, 2)
        d.update({n: o.reshape(weights[n].shape) for n, o in zip(small_names, small_outs)})
        return d

    order = ("ln_in_g", "ln_in_b", "w_in", "b_in", "sinks", "vn_g", "vn_b", "w_s", "b_s", "p_a", "p_b", "w_out", "b_out",
             "ln_g", "ln_b")
    res = [loss, grad_x.reshape(x.shape)]
    groups = [named(big[0], [g_named[n] for n in small_names])] + [named(big[k + 1], small[k]) for k in range(3)]
    for group in groups:
        res.extend(group[n] for n in order)
    return tuple(res)
```

```python
import functools

import jax
import jax.numpy as jnp
from jax import lax
from jax.experimental import pallas as pl
from jax.experimental.pallas import tpu as pltpu

F32 = jnp.float32
MXU_DTYPE = jnp.bfloat16

D_MODEL = 1024
DEPTH = 2
HEAD_DIM = 64
ATTN_W = 512
KV_W = 128
BLK = 128
SGU_W = 512
SGU_G = 4
N_COLS = 4864
C_Q, C_K, C_V, C_GA, C_UB, C_VB, C_GB, C_RA, C_RB = 0, 512, 640, 768, 1280, 1792, 2304, 2816, 3840
ALPHA = (2.0 * DEPTH) ** 0.25
LN_EPS = 1e-5
SCALE = HEAD_DIM ** -0.5
NEG = float(jnp.finfo(jnp.float32).min)
GELU_C = 0.7978845608028654
GELU_A = 0.044715

ADAM_LR, ADAM_B1, ADAM_B2, ADAM_EPS, ADAM_WD, ADAM_STEP = 0.001, 0.9, 0.999, 1e-08, 0.01, 10

V7X_VMEM_BYTES = 64 * 1024 * 1024
V7X_VMEM_RESERVE = 3 * 1024 * 1024
ROWS = 256
ROWS_BWD = 256
ROWS_PROJ = 512

MESH = pl.DeviceIdType.MESH
N_CHIPS = 4


def _mm(a, b):
    return jnp.dot(a.astype(MXU_DTYPE), b.astype(MXU_DTYPE), preferred_element_type=F32)


def _mm_nt(a, b):
    return lax.dot_general(a.astype(MXU_DTYPE), b.astype(MXU_DTYPE), (((1,), (1,)), ((), ())),
                           preferred_element_type=F32)


def _mm_tn(a, b):
    return lax.dot_general(a.astype(MXU_DTYPE), b.astype(MXU_DTYPE), (((0,), (0,)), ((), ())),
                           preferred_element_type=F32)


def _sigmoid(x):
    return 0.5 * jnp.tanh(0.5 * x) + 0.5


def _gelu_parts(x):
    x2 = x * x
    u = 0.5 * jnp.tanh(x * (GELU_C + (GELU_C * GELU_A) * x2)) + 0.5
    dg = u * (1.0 + x * (1.0 - u) * (2.0 * GELU_C + (6.0 * GELU_C * GELU_A) * x2))
    return x * u, dg


def _ln_fwd(z, g, b):
    mu = jnp.mean(z, axis=-1, keepdims=True)
    zc = z - mu
    var = jnp.mean(zc * zc, axis=-1, keepdims=True)
    rstd = lax.rsqrt(var + LN_EPS)
    zhat = zc * rstd
    return zhat * g + b, zhat, rstd


def _ln_bwd(dy, zhat, rstd, g):
    dzh = dy * g
    m1 = jnp.mean(dzh, axis=-1, keepdims=True)
    m2 = jnp.mean(dzh * zhat, axis=-1, keepdims=True)
    return rstd * (dzh - m1 - zhat * m2)


def _colsum(v):
    return jnp.sum(v, axis=0, keepdims=True)


def _lane_lo(rows):
    return lax.broadcasted_iota(jnp.int32, (rows, 128), 1) < HEAD_DIM


def _dup_head(x2, hk, lo):
    xr = pltpu.roll(x2, HEAD_DIM, 1)
    return jnp.where(lo, x2, xr) if hk == 0 else jnp.where(lo, xr, x2)


def _fold_head(y0, y1, lo):
    f0 = y0 + pltpu.roll(y0, HEAD_DIM, 1)
    f1 = y1 + pltpu.roll(y1, HEAD_DIM, 1)
    return jnp.where(lo, f0, f1)


def _stack_heads(t0, t1, lo):
    z = jnp.zeros_like(t0)
    return jnp.concatenate([jnp.where(lo, t0, z), jnp.where(lo, z, t0),
                            jnp.where(lo, t1, z), jnp.where(lo, z, t1)], axis=0)


def _unstack_heads(o, lo):
    return (jnp.where(lo, o[0:128], o[128:256]), jnp.where(lo, o[256:384], o[384:512]))


def _band_bias(start):
    row = lax.broadcasted_iota(jnp.int32, (4 * BLK, 2 * BLK), 0) & (BLK - 1)
    kpos = lax.broadcasted_iota(jnp.int32, (4 * BLK, 2 * BLK), 1)
    valid = (kpos > row) & (kpos <= row + BLK)
    if start is not None:
        valid = valid & (jnp.logical_not(start) | (kpos >= BLK))
    return jnp.where(valid, 0.0, NEG)


def _attn_probs(s, sink4, bias):
    s = s * SCALE + bias
    r1 = lax.broadcasted_iota(jnp.int32, (4 * BLK, 1), 0)
    sk = jnp.where(r1 < BLK, sink4[0], jnp.where(r1 < 2 * BLK, sink4[1], jnp.where(r1 < 3 * BLK, sink4[2], sink4[3])))
    m = jnp.maximum(jnp.max(s, axis=-1, keepdims=True), sk)
    p = jnp.exp(s - m)
    es = jnp.exp(sk - m)
    inv = 1.0 / (jnp.sum(p, axis=-1, keepdims=True) + es)
    return p * inv, es * inv


_PROJ_CHUNK = 512


def _kv_blocks(kvh, h_s, nb):
    ks = [kvh[:, 0:KV_W]] + [h_s[n * BLK:(n + 1) * BLK, C_K:C_K + KV_W] for n in range(nb)]
    vs = [kvh[:, KV_W:2 * KV_W]] + [h_s[n * BLK:(n + 1) * BLK, C_V:C_V + KV_W] for n in range(nb)]
    return ks, vs


def _attn_operands(h_s, ks, vs, n, hk, lo, lo2):
    rows = slice(n * BLK, (n + 1) * BLK)
    kd = _dup_head(jnp.concatenate([ks[n], ks[n + 1]], axis=0), hk, lo2)
    vd = _dup_head(jnp.concatenate([vs[n], vs[n + 1]], axis=0), hk, lo2)
    c0 = C_Q + 2 * hk * 128
    qs = _stack_heads(h_s[rows, c0:c0 + 128], h_s[rows, c0 + 128:c0 + 256], lo)
    return qs, kd, vd


def _sgu_weights(w):
    tri = (lax.broadcasted_iota(jnp.int32, (BLK, BLK), 0) >= lax.broadcasted_iota(jnp.int32, (BLK, BLK), 1))
    return tri, [jnp.where(tri, w["ws"][g], 0.0) for g in range(SGU_G)]


def _layer_forward(x, kvh, start, w, h_s, attn_s, mix_s, keep, tick=lambda: None):
    R = x.shape[0]
    nb = R // BLK
    lo = _lane_lo(BLK)
    lo2 = _lane_lo(2 * BLK)
    ks, vs = _kv_blocks(kvh, h_s, nb)
    sinks = [w["sinks"][j] for j in range(8)]
    bias_first = _band_bias(start)
    bias_rest = _band_bias(None) if nb > 1 else None
    pairs = [(n, hk) for n in range(nb) for hk in range(2)]
    ops = [_attn_operands(h_s, ks, vs, n, hk, lo, lo2) for n, hk in pairs]
    scores = [_mm_nt(qs, kd) for qs, kd, _ in ops]
    probs_l = [_attn_probs(s, sinks[4 * hk:4 * hk + 4], bias_first if n == 0 else bias_rest)
               for s, (n, hk) in zip(scores, pairs)]
    outs = [_mm(p, vd) for (p, _), (_, _, vd) in zip(probs_l, ops)]
    for o, (n, hk) in zip(outs, pairs):
        rows = slice(n * BLK, (n + 1) * BLK)
        t0, t1 = _unstack_heads(o, lo)
        attn_s[rows, 2 * hk * 128:(2 * hk + 1) * 128] = t0
        attn_s[rows, (2 * hk + 1) * 128:(2 * hk + 2) * 128] = t1
        tick()
    ga = h_s[:, C_GA:C_GA + ATTN_W]
    ya = attn_s[...] * (ga * _sigmoid(ga))
    tick()
    gu, _ = _gelu_parts(h_s[:, C_UB:C_UB + SGU_W])
    tick()
    gv, _ = _gelu_parts(h_s[:, C_VB:C_VB + SGU_W])
    vn, _, _ = _ln_fwd(gv, w["vng"][...], w["vnb"][...])
    tick()
    _, wms = _sgu_weights(w)
    tiles = [(slice(n * BLK, (n + 1) * BLK), g, slice(g * 128, (g + 1) * 128)) for n in range(nb) for g in range(SGU_G)]
    mixes = [_mm(wms[g], vn[rows, cols]) for rows, g, cols in tiles]
    for m, (rows, g, cols) in zip(mixes, tiles):
        mix_s[rows, cols] = m + w["bs"][:, cols]
    gb = h_s[:, C_GB:C_GB + SGU_W]
    yb = gu * mix_s[...] * (gb * _sigmoid(gb))
    tick()
    a = _mm(ya, w["pa"][...])
    b = _mm(yb, w["pb"][...])
    tick()
    merged = _sigmoid(h_s[:, C_RA:C_RA + D_MODEL]) * a + _sigmoid(h_s[:, C_RB:C_RB + D_MODEL]) * b
    tick()
    out = _mm(merged, w["wout"][...]) + w["bout"][...]
    y, zhat, rstd = _ln_fwd(ALPHA * x + out, w["lng"][...], w["lnb"][...])
    if not keep:
        return y, None
    return y, dict(ks=ks, vs=vs, probs=probs_l, ya=ya, yb=yb, a=a, b=b, merged=merged, zhat=zhat, rstd=rstd)


def _dsilu(g, sg):
    return sg * (1.0 + g * (1.0 - sg))


_W_NAMES = ("sinks", "vng", "vnb", "ws", "bs", "pa", "pb", "wout", "bout", "lng", "lnb")
_WT_NAMES = ("paT", "pbT", "woutT")
_ACC_NAMES = ("dwout", "dpa", "dpb", "dbin", "dbout", "dlng", "dlnb", "dvng", "dvnb", "dws", "dbs", "dsink", "loss")


def _make_fwd_proj_body(R, S, nT):
    def body(x_ref, h_ref, kvh_ref, wn_ref, bn_ref, *rest):
        w = dict(zip(_W_NAMES, rest[:len(_W_NAMES)]))
        y_ref, hn_ref, attn_s, mix_s, yprev_s = rest[len(_W_NAMES):]

        @pl.when(pl.program_id(0) == 0)
        def _():
            yprev_s[...] = jnp.zeros(yprev_s.shape, yprev_s.dtype)

        t = jnp.minimum(pl.program_id(0), nT - 1)
        start = (t % (S // R)) == 0
        chunks = [(c0, min(c0 + _PROJ_CHUNK, N_COLS)) for c0 in range(0, N_COLS, _PROJ_CHUNK)]

        def tick():
            if chunks:
                c0, c1 = chunks.pop(0)
                hn_ref[:, c0:c1] = _mm_nt(yprev_s[...], wn_ref[c0:c1, :]) + bn_ref[:, c0:c1]

        y, _ = _layer_forward(x_ref[...], kvh_ref[...], start, w, h_ref, attn_s, mix_s, keep=False, tick=tick)
        while chunks:
            tick()
        y_ref[...] = y
        yprev_s[...] = y.astype(yprev_s.dtype)
    return body


def _make_bwd_body(R, S, nT, is_last):
    nb = R // BLK
    names = _W_NAMES + _WT_NAMES
    nw = len(names)

    def body(x_ref, h_s, kvh_ref, aux_ref, *rest):
        w = dict(zip(names, rest[:nw]))
        dres_ref, dh_ref = rest[nw:nw + 2]
        acc = dict(zip(_ACC_NAMES, rest[nw + 2:nw + 2 + len(_ACC_NAMES)]))
        attn_s, mix_s, ckv_s = rest[nw + 2 + len(_ACC_NAMES):]
        i = pl.program_id(0)
        start = ((nT - 1 - i) % (S // R)) == 0
        lo = _lane_lo(BLK)
        lo2 = _lane_lo(2 * BLK)

        @pl.when(i == 0)
        def _():
            for name in _ACC_NAMES:
                acc[name][...] = jnp.zeros(acc[name].shape, F32)
            ckv_s[...] = jnp.zeros(ckv_s.shape, F32)

        x = x_ref[...]
        y, sv = _layer_forward(x, kvh_ref[...], start, w, h_s, attn_s, mix_s, keep=True)
        if is_last:
            diff = y - aux_ref[...]
            part = 0.5 * jnp.sum(jnp.mean(diff * diff, axis=-1, keepdims=True), axis=0, keepdims=True)
            acc["loss"][...] += jnp.broadcast_to(part, acc["loss"].shape)
            dy = diff * (1.0 / D_MODEL)
        else:
            dy = aux_ref[...]

        def put(c0, val, rows=slice(None)):
            width = val.shape[1]
            dh_ref[rows, c0:c0 + width] = val.astype(dh_ref.dtype)
            acc["dbin"][:, c0:c0 + width] += _colsum(val)

        acc["dlng"][...] += _colsum(dy * sv["zhat"])
        acc["dlnb"][...] += _colsum(dy)
        dz = _ln_bwd(dy, sv["zhat"], sv["rstd"], w["lng"][...])
        dres_ref[...] = ALPHA * dz
        acc["dbout"][...] += _colsum(dz)
        acc["dwout"][...] += _mm_tn(sv["merged"], dz)
        dmerged = _mm(dz, w["woutT"][...])
        sa = _sigmoid(h_s[:, C_RA:C_RA + D_MODEL])
        da = dmerged * sa
        put(C_RA, da * sv["a"] * (1.0 - sa))
        sb = _sigmoid(h_s[:, C_RB:C_RB + D_MODEL])
        db = dmerged * sb
        put(C_RB, db * sv["b"] * (1.0 - sb))
        acc["dpa"][...] += _mm_tn(sv["ya"], da)
        acc["dpb"][...] += _mm_tn(sv["yb"], db)
        dya = _mm(da, w["paT"][...])
        dyb = _mm(db, w["pbT"][...])
        ga = h_s[:, C_GA:C_GA + ATTN_W]
        sga = _sigmoid(ga)
        put(C_GA, dya * attn_s[...] * _dsilu(ga, sga))
        attn_s[...] = dya * (ga * sga)
        ks, vs = sv["ks"], sv["vs"]
        pairs = [(n, hk) for n in range(nb) for hk in range(2)]
        ops = [_attn_operands(h_s, ks, vs, n, hk, lo, lo2) for n, hk in pairs]
        doss = [_stack_heads(attn_s[n * BLK:(n + 1) * BLK, 2 * hk * 128:(2 * hk + 1) * 128],
                             attn_s[n * BLK:(n + 1) * BLK, (2 * hk + 1) * 128:(2 * hk + 2) * 128], lo) for n, hk in pairs]
        dps = [_mm_nt(dos, vd) for dos, (_, _, vd) in zip(doss, ops)]
        gb = h_s[:, C_GB:C_GB + SGU_W]
        sgb = _sigmoid(gb)
        gu, dgu = _gelu_parts(h_s[:, C_UB:C_UB + SGU_W])
        mixed = mix_s[...]
        put(C_GB, dyb * (gu * mixed) * _dsilu(gb, sgb))
        dsgu = dyb * (gb * sgb)
        put(C_UB, dsgu * mixed * dgu)
        dmixed = dsgu * gu
        dsss = []
        for dp, (probs, ps), (n, hk) in zip(dps, sv["probs"], pairs):
            delta = jnp.sum(probs * dp, axis=-1, keepdims=True)
            dsk = -(ps * delta)
            for g in range(4):
                j = 4 * hk + g
                acc["dsink"][j:j + 1, :] += jnp.broadcast_to(_colsum(dsk[g * BLK:(g + 1) * BLK]), (1, 128))
            dsss.append(probs * (dp - delta) * SCALE)
        dqs = [_mm(dss, kd) for dss, (_, kd, _) in zip(dsss, ops)]
        ydks = [_mm_tn(dss, qs) for dss, (qs, _, _) in zip(dsss, ops)]
        ydvs = [_mm_tn(probs, dos) for (probs, _), dos in zip(sv["probs"], doss)]
        gv, dgv = _gelu_parts(h_s[:, C_VB:C_VB + SGU_W])
        vn, vhat, vrstd = _ln_fwd(gv, w["vng"][...], w["vnb"][...])
        tri, wms = _sgu_weights(w)
        tiles = [(slice(n * BLK, (n + 1) * BLK), g, slice(g * 128, (g + 1) * 128)) for n in range(nb) for g in range(SGU_G)]
        dw_parts = [_mm_nt(dmixed[rows, cols], vn[rows, cols]) for rows, g, cols in tiles]
        dvn_parts = [_mm_tn(wms[g], dmixed[rows, cols]) for rows, g, cols in tiles]
        for g in range(SGU_G):
            tot = dw_parts[g]
            for n in range(1, nb):
                tot = tot + dw_parts[n * SGU_G + g]
            acc["dws"][g] += jnp.where(tri, tot, 0.0)
        for part, (rows, g, cols) in zip(dvn_parts, tiles):
            mix_s[rows, cols] = part
        dbs_part = dmixed[0:BLK]
        for n in range(1, nb):
            dbs_part = dbs_part + dmixed[n * BLK:(n + 1) * BLK]
        acc["dbs"][...] += dbs_part
        for dq, (n, hk) in zip(dqs, pairs):
            q0, q1 = _unstack_heads(dq, lo)
            put(C_Q + 2 * hk * 128, q0, slice(n * BLK, (n + 1) * BLK))
            put(C_Q + (2 * hk + 1) * 128, q1, slice(n * BLK, (n + 1) * BLK))
        dks = [jnp.zeros((BLK, KV_W), F32) for _ in range(nb + 1)]
        dvs = [jnp.zeros((BLK, KV_W), F32) for _ in range(nb + 1)]
        for n in range(nb):
            dk2 = _fold_head(ydks[2 * n], ydks[2 * n + 1], lo2)
            dv2 = _fold_head(ydvs[2 * n], ydvs[2 * n + 1], lo2)
            dks[n] = dks[n] + dk2[0:BLK]
            dks[n + 1] = dks[n + 1] + dk2[BLK:2 * BLK]
            dvs[n] = dvs[n] + dv2[0:BLK]
            dvs[n + 1] = dvs[n + 1] + dv2[BLK:2 * BLK]
        dks[nb] = dks[nb] + ckv_s[:, 0:KV_W]
        dvs[nb] = dvs[nb] + ckv_s[:, KV_W:2 * KV_W]
        ckv_s[:, 0:KV_W] = dks[0]
        ckv_s[:, KV_W:2 * KV_W] = dvs[0]
        put(C_K, jnp.concatenate(dks[1:], axis=0))
        put(C_V, jnp.concatenate(dvs[1:], axis=0))
        dvn = mix_s[...]
        acc["dvng"][...] += _colsum(dvn * vhat)
        acc["dvnb"][...] += _colsum(dvn)
        put(C_VB, _ln_bwd(dvn, vhat, vrstd, w["vng"][...]) * dgv)

        @pl.when(i == nT - 1)
        def _():
            for g in range(SGU_G):
                cols = slice(g * 128, (g + 1) * 128)
                tot = jnp.sum(acc["dbs"][:, cols], axis=1, keepdims=True)
                acc["dbs"][:, cols] = jnp.broadcast_to(tot, (BLK, 128))

    return body


def _resident():
    return pl.BlockSpec(memory_space=pltpu.VMEM)


_BIG_RESIDENT = ("pa", "pb", "wout", "paT", "pbT", "woutT", "dwout", "dpa", "dpb")


def _const_block(shape):
    return pl.BlockSpec(tuple(shape), lambda *_: (0,) * len(shape))


def _operand_spec(name, shape):
    if name == "sinks":
        return pl.BlockSpec(memory_space=pltpu.SMEM)
    return _resident() if name in _BIG_RESIDENT else _const_block(shape)


def _weight_specs(names, wts):
    return [_operand_spec(n, wts[n].shape) for n in names]


def _nbytes(a):
    n = jnp.dtype(a.dtype).itemsize
    for d in a.shape:
        n *= d
    return n


def _layer_params(resident=()):
    scoped = V7X_VMEM_BYTES - V7X_VMEM_RESERVE - sum(_nbytes(a) for a in resident)
    return pltpu.CompilerParams(dimension_semantics=("arbitrary",), vmem_limit_bytes=scoped)


class _Comm:
    def __init__(self, ins, out_shapes, sems, start, finish, aliases=None, turn=None, turn_step=None):
        self.ins = [pltpu.with_memory_space_constraint(a, pltpu.HBM) for a in ins]
        self.out_shapes = [pltpu.HBM(s.shape, s.dtype) for s in out_shapes]
        self.sems = list(sems)
        self.start, self.finish, self.aliases = start, finish, dict(aliases or {})
        self.turn, self.turn_step = turn, turn_step


_ANY = pl.BlockSpec(memory_space=pltpu.HBM)
_DMA = pltpu.SemaphoreType.DMA


def _call(body, *, name, grid, in_specs, out_specs, out_shape, scratch_shapes, compiler_params, operands, comm=None):
    if comm is None:
        res = pl.pallas_call(body, name=name, grid=grid, in_specs=in_specs, out_specs=out_specs, out_shape=out_shape,
                             scratch_shapes=scratch_shapes, compiler_params=compiler_params)(*operands)
        return list(res), []
    n_in, n_out, n_scr = len(in_specs), len(out_specs), len(scratch_shapes)
    ci, co = len(comm.ins), len(comm.out_shapes)
    last = grid[0] - 1

    def hosted(*refs):
        ins, cin = refs[:n_in], refs[n_in:n_in + ci]
        p = n_in + ci
        outs, cout = refs[p:p + n_out], refs[p + n_out:p + n_out + co]
        p += n_out + co
        scr, sems = refs[p:p + n_scr], refs[p + n_scr:]
        pl.when(pl.program_id(0) == 0)(lambda: comm.start(cin, cout, sems))
        body(*ins, *outs, *scr)
        if comm.turn is not None:
            at = last if comm.turn_step is None else comm.turn_step
            pl.when(pl.program_id(0) == at)(lambda: comm.turn(cin, cout, sems))
        pl.when(pl.program_id(0) == last)(lambda: comm.finish(cin, cout, sems))

    res = pl.pallas_call(
        hosted, name=name, grid=grid, in_specs=list(in_specs) + [_ANY] * ci, out_specs=list(out_specs) + [_ANY] * co,
        out_shape=list(out_shape) + comm.out_shapes, scratch_shapes=list(scratch_shapes) + comm.sems,
        input_output_aliases={n_in + a: n_out + b for a, b in comm.aliases.items()},
        compiler_params=compiler_params)(*operands, *comm.ins)
    return list(res[:n_out]), list(res[n_out:])


def _both(a, b):
    ni, no, ns = len(a.ins), len(a.out_shapes), len(a.sems)

    def each(f, g):
        def run(cin, cout, sems):
            f(cin[:ni], cout[:no], sems[:ns])
            g(cin[ni:], cout[no:], sems[ns:])
        return run

    both = _Comm([], [], a.sems + b.sems, each(a.start, b.start), each(a.finish, b.finish))
    both.ins, both.out_shapes = a.ins + b.ins, a.out_shapes + b.out_shapes
    return both


def _run_comm(comm, name):
    ci, co = len(comm.ins), len(comm.out_shapes)

    def body(*refs):
        cin, cout, sems = refs[:ci], refs[ci:ci + co], refs[ci + co:]
        comm.start(cin, cout, sems)
        if comm.turn is not None:
            comm.turn(cin, cout, sems)
        comm.finish(cin, cout, sems)

    return list(pl.pallas_call(body, name=name, in_specs=[_ANY] * ci, out_specs=[_ANY] * co, out_shape=comm.out_shapes,
                               scratch_shapes=comm.sems, input_output_aliases=comm.aliases)(*comm.ins))


def project(x, winT, b, name, comm=None):
    T = x.shape[0]
    R = ROWS_PROJ

    def body(x_ref, w_ref, b_ref, h_ref):
        xb = x_ref[...].astype(MXU_DTYPE)
        for c0 in range(0, N_COLS, _PROJ_CHUNK):
            c1 = min(c0 + _PROJ_CHUNK, N_COLS)
            h_ref[:, c0:c1] = _mm_nt(xb, w_ref[c0:c1, :]) + b_ref[:, c0:c1]

    (h,), extra = _call(
        body, name=name, grid=(T // R,),
        in_specs=[pl.BlockSpec((R, D_MODEL), lambda t: (t, 0)), _resident(), _const_block(b.shape)],
        out_specs=[pl.BlockSpec((R, N_COLS), lambda t: (t, 0))],
        out_shape=[jax.ShapeDtypeStruct((T, N_COLS), F32)], scratch_shapes=[],
        compiler_params=_layer_params([winT]), operands=(x, winT, b), comm=comm)
    return h, extra


_KV_BLOCK = C_K // (2 * KV_W)


def layer_forward_and_projection(x, h, wts, winT_next, b_next, S, name, comm=None):
    T = x.shape[0]
    R = ROWS
    nT = T // R
    npt = R // BLK
    cur = lambda t: jnp.minimum(t, nT - 1)
    (y, hn), extra = _call(
        _make_fwd_proj_body(R, S, nT), name=name, grid=(nT + 1,),
        in_specs=[pl.BlockSpec((R, D_MODEL), lambda t: (cur(t), 0)), pl.BlockSpec((R, N_COLS), lambda t: (cur(t), 0)),
                  pl.BlockSpec((BLK, 2 * KV_W), lambda t: (jnp.maximum(cur(t) * npt - 1, 0), _KV_BLOCK)),
                  _resident(), _const_block(b_next.shape)] + _weight_specs(_W_NAMES, wts),
        out_specs=[pl.BlockSpec((R, D_MODEL), lambda t: (cur(t), 0)),
                   pl.BlockSpec((R, N_COLS), lambda t: (jnp.maximum(t - 1, 0), 0))],
        out_shape=[jax.ShapeDtypeStruct((T, D_MODEL), F32), jax.ShapeDtypeStruct((T, N_COLS), F32)],
        scratch_shapes=[pltpu.VMEM((R, ATTN_W), F32), pltpu.VMEM((R, SGU_W), F32), pltpu.VMEM((R, D_MODEL), MXU_DTYPE)],
        compiler_params=_layer_params([winT_next] + [wts[n] for n in _W_NAMES if n in _BIG_RESIDENT]),
        operands=(x, h, h, winT_next, b_next, *[wts[n] for n in _W_NAMES]), comm=comm)
    return y, hn, extra


_ACC_SHAPES = dict(dwout=(D_MODEL, D_MODEL), dpa=(ATTN_W, D_MODEL), dpb=(SGU_W, D_MODEL), dbin=(1, N_COLS),
                   dbout=(1, D_MODEL), dlng=(1, D_MODEL), dlnb=(1, D_MODEL), dvng=(1, SGU_W), dvnb=(1, SGU_W),
                   dws=(SGU_G, BLK, BLK), dbs=(BLK, SGU_W), dsink=(8, 128), loss=(8, 128))


def layer_backward(x, h, aux, wts, S, is_last, name, comm=None):
    T = x.shape[0]
    R = ROWS_BWD
    nT = T // R
    npt = R // BLK
    rev = lambda i: (nT - 1 - i, 0)
    halo = lambda i: (jnp.maximum((nT - 1 - i) * npt - 1, 0), _KV_BLOCK)
    names = _W_NAMES + _WT_NAMES
    out_shape = ([jax.ShapeDtypeStruct((T, D_MODEL), F32), jax.ShapeDtypeStruct((T, N_COLS), MXU_DTYPE)]
                 + [jax.ShapeDtypeStruct(_ACC_SHAPES[n], F32) for n in _ACC_NAMES])
    outs, extra = _call(
        _make_bwd_body(R, S, nT, is_last), name=name, grid=(nT,),
        in_specs=[pl.BlockSpec((R, D_MODEL), rev), pl.BlockSpec((R, N_COLS), rev), pl.BlockSpec((BLK, 2 * KV_W), halo),
                  pl.BlockSpec((R, D_MODEL), rev)] + _weight_specs(names, wts),
        out_specs=[pl.BlockSpec((R, D_MODEL), rev), pl.BlockSpec((R, N_COLS), rev)]
        + [_operand_spec(n, _ACC_SHAPES[n]) for n in _ACC_NAMES],
        out_shape=out_shape,
        scratch_shapes=[pltpu.VMEM((R, ATTN_W), F32), pltpu.VMEM((R, SGU_W), F32), pltpu.VMEM((BLK, 2 * KV_W), F32)],
        compiler_params=_layer_params([wts[n] for n in names if n in _BIG_RESIDENT]
                                      + [jax.ShapeDtypeStruct(_ACC_SHAPES[n], F32) for n in _ACC_NAMES if n in _BIG_RESIDENT]),
        operands=(x, h, h, aux, *[wts[n] for n in names]), comm=comm)
    return outs[0], outs[1], dict(zip(_ACC_NAMES, outs[2:])), extra


def input_grads(dres, dh, x, winT, name, comm=None):
    T = x.shape[0]
    R = ROWS

    def body(dres_ref, dh_ref, x_ref, winT_ref, dx_ref, dwin_ref):
        @pl.when(pl.program_id(0) == 0)
        def _():
            dwin_ref[...] = jnp.zeros(dwin_ref.shape, F32)

        dh = dh_ref[...]
        dx_ref[...] = dres_ref[...] + jnp.dot(dh, winT_ref[...], preferred_element_type=F32)
        dwin_ref[...] += _mm_tn(dh, x_ref[...])

    row = lambda t: (t, 0)
    dwin_shape = jax.ShapeDtypeStruct((N_COLS, D_MODEL), F32)
    (dx, dwinT), extra = _call(
        body, name=name, grid=(T // R,),
        in_specs=[pl.BlockSpec((R, D_MODEL), row), pl.BlockSpec((R, N_COLS), row), pl.BlockSpec((R, D_MODEL), row),
                  _resident()],
        out_specs=[pl.BlockSpec((R, D_MODEL), row), _resident()],
        out_shape=[jax.ShapeDtypeStruct((T, D_MODEL), F32), dwin_shape], scratch_shapes=[],
        compiler_params=_layer_params([winT, dwin_shape]), operands=(dres, dh, x, winT), comm=comm)
    return dx, dwinT, extra


def input_norm_forward(x, g, b, name, comm=None):
    T = x.shape[0]
    R = ROWS_PROJ

    def body(x_ref, g_ref, b_ref, y_ref):
        y_ref[...] = _ln_fwd(x_ref[...], g_ref[...], b_ref[...])[0]

    row = lambda t: (t, 0)
    (y,), extra = _call(
        body, name=name, grid=(T // R,),
        in_specs=[pl.BlockSpec((R, D_MODEL), row), _const_block(g.shape), _const_block(b.shape)],
        out_specs=[pl.BlockSpec((R, D_MODEL), row)],
        out_shape=[jax.ShapeDtypeStruct((T, D_MODEL), F32)], scratch_shapes=[],
        compiler_params=_layer_params(), operands=(x, g, b), comm=comm)
    return y, extra


def input_norm_backward(x, dy, g, name, comm=None):
    T = x.shape[0]
    R = ROWS_PROJ

    def body(x_ref, dy_ref, g_ref, dx_ref, dg_ref, db_ref):
        @pl.when(pl.program_id(0) == 0)
        def _():
            dg_ref[...] = jnp.zeros(dg_ref.shape, F32)
            db_ref[...] = jnp.zeros(db_ref.shape, F32)

        dy = dy_ref[...]
        _, xhat, rstd = _ln_fwd(x_ref[...], g_ref[...], g_ref[...])
        dx_ref[...] = _ln_bwd(dy, xhat, rstd, g_ref[...])
        dg_ref[...] += _colsum(dy * xhat)
        db_ref[...] += _colsum(dy)

    row = lambda t: (t, 0)
    vec = jax.ShapeDtypeStruct((1, D_MODEL), F32)
    (dx, dg, db), extra = _call(
        body, name=name, grid=(T // R,),
        in_specs=[pl.BlockSpec((R, D_MODEL), row), pl.BlockSpec((R, D_MODEL), row), _const_block(g.shape)],
        out_specs=[pl.BlockSpec((R, D_MODEL), row), _const_block(vec.shape), _const_block(vec.shape)],
        out_shape=[jax.ShapeDtypeStruct((T, D_MODEL), F32), vec, vec], scratch_shapes=[],
        compiler_params=_layer_params(), operands=(x, dy, g), comm=comm)
    return dx, dg, db, extra


_PIECES = ("winT", "pa", "pb", "wout")
_WHOLE = dict(winT=(N_COLS, D_MODEL), pa=(ATTN_W, D_MODEL), pb=(SGU_W, D_MODEL), wout=(D_MODEL, D_MODEL))
_HALF = dict(winT=(N_COLS // 8, D_MODEL), pa=(ATTN_W // 2, D_MODEL // 4), pb=(SGU_W // 2, D_MODEL // 4),
             wout=(D_MODEL // 8, D_MODEL))
SMALL_ROWS = 1280


def _region(name, ref, k, h):
    hr, hc = _HALF[name]
    if name in ("pa", "pb"):
        return ref.at[pl.ds(h * hr, hr), pl.ds(k * hc, hc)]
    return ref.at[pl.ds(k * 2 * hr + h * hr, hr), :]


def _place():
    x, y, c = lax.axis_index("x"), lax.axis_index("y"), lax.axis_index("c")
    return x, y, c, 2 * x + y


def _as_chip(j, fn):
    for jj in range(N_CHIPS):
        pl.when(j == jj)(functools.partial(fn, jj))


def _remote(src, dst, send_sem, recv_sem, to):
    return pltpu.make_async_remote_copy(src_ref=src, dst_ref=dst, send_sem=send_sem, recv_sem=recv_sem,
                                        device_id=to, device_id_type=MESH)


def _core_of(k, c):
    return (k // 2, k % 2, c)


def _others(jj):
    return [k for k in range(N_CHIPS) if k != jj]


def _start_all(cps):
    for cp in cps:
        cp.start()


def _wait_all(cps):
    for cp in cps:
        cp.wait()


def gather_over_chips(names, shards, on_landed=None):
    n = len(names)

    def own(cin, cout, sems, jj):
        x, y, c, _ = _place()
        cps = []
        for i, name in enumerate(names):
            hr, hc = _HALF[name]
            place = (cout[i].at[:, pl.ds(jj * hc, hc)] if name in ("pa", "pb")
                     else cout[i].at[pl.ds(jj * 2 * hr, 2 * hr), :])
            cps.append(_remote(cin[i], place, sems[2].at[i], sems[3].at[i], (x, y, 1 - c)))
        return cps

    def copies(cin, cout, sems, jj, c):
        out = []
        for i, name in enumerate(names):
            hr = _HALF[name][0]
            for k in _others(jj):
                out.append(_remote(cin[i].at[pl.ds(c * hr, hr)], _region(name, cout[i], jj, c),
                                   sems[0].at[4 * i + k], sems[1].at[4 * i + jj], _core_of(k, c)))
        return out

    def start(cin, cout, sems):
        _, _, c, j = _place()
        _as_chip(j, lambda jj: _start_all(copies(cin, cout, sems, jj, c) + own(cin, cout, sems, jj)))

    def finish(cin, cout, sems):
        _, _, c, j = _place()

        def run(jj):
            for i, name in enumerate(names):
                for k in _others(jj):
                    land = _region(name, cout[i], k, c)
                    _remote(land, land, sems[0].at[4 * i + k], sems[1].at[4 * i + k], _core_of(k, c)).wait_recv()
                    if on_landed is not None:
                        on_landed(land, 4 * i + k, sems)
            for cp in copies(cin, cout, sems, jj, c):
                cp.wait_send()
            _wait_all(own(cin, cout, sems, jj))

        _as_chip(j, run)

    shapes = [jax.ShapeDtypeStruct(_WHOLE[nm], s.dtype) for nm, s in zip(names, shards)]
    return _Comm(shards, shapes, [_DMA((4 * n,)), _DMA((4 * n,)), _DMA((n,)), _DMA((n,))], start, finish)


def gather_weights(names, shards, turn_step=None):
    n = len(names)

    def pass_on(land, slot, sems):
        x, y, c, _ = _place()
        _remote(land, land, sems[4].at[slot], sems[5].at[slot], (x, y, 1 - c)).start()

    chips = gather_over_chips(names, shards, on_landed=pass_on)
    pair_finish = _pair_finish(names)
    return _Comm(shards, chips.out_shapes, chips.sems + [_DMA((4 * n,)), _DMA((4 * n,))], chips.start,
                 lambda cin, cout, sems: pair_finish(cin, cout, sems[4:]), turn=chips.finish, turn_step=turn_step)


def _pair_finish(names):
    def finish(cin, cout, sems):
        x, y, c, j = _place()

        def run(jj):
            for i, name in enumerate(names):
                for k in _others(jj):
                    theirs = _region(name, cout[i], k, 1 - c)
                    _remote(theirs, theirs, sems[0].at[4 * i + k], sems[1].at[4 * i + k], (x, y, 1 - c)).wait_recv()
            for i, name in enumerate(names):
                for k in _others(jj):
                    land = _region(name, cout[i], k, c)
                    _remote(land, land, sems[0].at[4 * i + k], sems[1].at[4 * i + k], (x, y, 1 - c)).wait_send()

        _as_chip(j, run)

    return finish


def pair_send_halves(names, parts, small=None, extra=()):
    n = len(names)
    hs = SMALL_ROWS // 2
    first_extra = n + (small is not None)

    def copies(cin, cout, sems):
        x, y, c, _ = _place()
        sib = (x, y, 1 - c)
        cps = []
        for i, name in enumerate(names):
            for k in range(N_CHIPS):
                cps.append(_remote(_region(name, cin[i], k, 1 - c), cout[i].at[k], sems[0].at[4 * i + k],
                                   sems[1].at[4 * i + k], sib))
        if small is not None:
            cps.append(_remote(cin[n].at[pl.ds((1 - c) * hs, hs)], cout[n], sems[0].at[4 * n], sems[1].at[4 * n], sib))
        for e in range(len(extra)):
            cps.append(_remote(cin[first_extra + e], cout[first_extra + e], sems[0].at[4 * n + 1 + e],
                               sems[1].at[4 * n + 1 + e], sib))
        return cps

    start = lambda cin, cout, sems: _start_all(copies(cin, cout, sems))
    finish = lambda cin, cout, sems: _wait_all(copies(cin, cout, sems))
    shapes = [jax.ShapeDtypeStruct((N_CHIPS,) + _HALF[nm], F32) for nm in names]
    ins = list(parts)
    if small is not None:
        shapes.append(jax.ShapeDtypeStruct((hs, 128), F32))
        ins.append(small)
    shapes += [jax.ShapeDtypeStruct(a.shape, a.dtype) for a in extra]
    ins += list(extra)
    n_sem = 4 * n + 1 + len(extra)
    return _Comm(ins, shapes, [_DMA((n_sem,)), _DMA((n_sem,))], start, finish)


def chip_exchange(sums, small=None):
    n = len(sums)

    def copies(cin, cout, sems, jj, c):
        out = []
        for k in _others(jj):
            for i in range(n):
                out.append(_remote(cin[i].at[k], cout[i].at[jj], sems[0].at[4 * i + k], sems[1].at[4 * i + jj], _core_of(k, c)))
            if small is not None:
                out.append(_remote(cin[n], cout[n].at[jj], sems[0].at[4 * n + k], sems[1].at[4 * n + jj], _core_of(k, c)))
        return out

    def start(cin, cout, sems):
        _, _, c, j = _place()

        def run(jj):
            if small is not None:
                pltpu.make_async_copy(cin[n], cout[n].at[jj], sems[2]).start()
            for cp in copies(cin, cout, sems, jj, c):
                cp.start()

        _as_chip(j, run)

    def finish(cin, cout, sems):
        _, _, c, j = _place()

        def run(jj):
            for k in _others(jj):
                for i in range(n):
                    _remote(cin[i].at[k], cout[i].at[k], sems[0].at[4 * i + k], sems[1].at[4 * i + k], _core_of(k, c)).wait_recv()
                if small is not None:
                    _remote(cin[n], cout[n].at[k], sems[0].at[4 * n + k], sems[1].at[4 * n + k], _core_of(k, c)).wait_recv()
            for cp in copies(cin, cout, sems, jj, c):
                cp.wait_send()
            if small is not None:
                pltpu.make_async_copy(cin[n], cout[n].at[jj], sems[2]).wait()

        _as_chip(j, run)

    shapes = [jax.ShapeDtypeStruct(a.shape, a.dtype) for a in sums]
    ins = list(sums)
    if small is not None:
        shapes.append(jax.ShapeDtypeStruct((N_CHIPS,) + small.shape, small.dtype))
        ins.append(small)
    return _Comm(ins, shapes, [_DMA((4 * n + 4,)), _DMA((4 * n + 4,)), _DMA(())], start, finish)


def pair_send_totals(arrs):
    n = len(arrs)

    def copies(cin, cout, sems):
        x, y, c, _ = _place()
        return [_remote(cin[i], cout[i], sems[0].at[i], sems[1].at[i], (x, y, 1 - c)) for i in range(n)]

    start = lambda cin, cout, sems: _start_all(copies(cin, cout, sems))
    finish = lambda cin, cout, sems: _wait_all(copies(cin, cout, sems))
    return _Comm(arrs, [jax.ShapeDtypeStruct(a.shape, a.dtype) for a in arrs], [_DMA((n,)), _DMA((n,))], start, finish)


def _pin(a):
    return pltpu.with_memory_space_constraint(a, pltpu.HBM)


def _plain_params(n_axes):
    return pltpu.CompilerParams(dimension_semantics=("arbitrary",) * n_axes, vmem_limit_bytes=V7X_VMEM_BYTES // 2)


def add_own_halves(names, parts, landed, core, name):
    n = len(names)

    def body(core_ref, *refs):
        for i in range(n):
            refs[2 * n + i][0] = (refs[i][...] + refs[n + i][0]).astype(MXU_DTYPE)

    in_specs, out_specs = [], []
    for nm in names:
        hr, hc = _HALF[nm]
        if nm in ("pa", "pb"):
            in_specs.append(pl.BlockSpec((hr // 2, hc), lambda k, s, cr: (cr[0] * 2 + s, k)))
        else:
            in_specs.append(pl.BlockSpec((hr // 2, hc), lambda k, s, cr: (k * 4 + cr[0] * 2 + s, 0)))
    for nm in names:
        hr, hc = _HALF[nm]
        in_specs.append(pl.BlockSpec((1, hr // 2, hc), lambda k, s, cr: (k, s, 0)))
        out_specs.append(pl.BlockSpec((1, hr // 2, hc), lambda k, s, cr: (k, s, 0)))
    return pl.pallas_call(
        body, name=name,
        grid_spec=pltpu.PrefetchScalarGridSpec(num_scalar_prefetch=1, grid=(N_CHIPS, 2), in_specs=in_specs, out_specs=out_specs),
        out_shape=[pltpu.HBM((N_CHIPS,) + _HALF[nm], MXU_DTYPE) for nm in names],
        compiler_params=_plain_params(2),
    )(core, *[_pin(a) for a in parts], *[_pin(a) for a in landed])


def add_small_half(small, landed, core):
    hs = SMALL_ROWS // 2

    def body(core_ref, a_ref, b_ref, o_ref):
        o_ref[...] = a_ref[...] + b_ref[...]

    return pl.pallas_call(
        body, name="add_small_half",
        grid_spec=pltpu.PrefetchScalarGridSpec(
            num_scalar_prefetch=1, grid=(1,),
            in_specs=[pl.BlockSpec((hs, 128), lambda s, cr: (cr[0], 0)), pl.BlockSpec((hs, 128), lambda s, cr: (0, 0))],
            out_specs=pl.BlockSpec((hs, 128), lambda s, cr: (0, 0))),
        out_shape=jax.ShapeDtypeStruct((hs, 128), F32),
    )(core, small, landed)


def sum_chips(sums, landed, chips, name, small_landed=None):
    n = len(sums)

    def body(chips_ref, *refs):
        ins, outs = refs[:4 * n + (small_landed is not None)], refs[4 * n + (small_landed is not None):]
        for i in range(n):
            tot = ins[4 * i][0].astype(F32)
            for r in range(1, N_CHIPS):
                tot = tot + ins[4 * i + r][0].astype(F32)
            outs[i][...] = tot
        if small_landed is not None:
            @pl.when(pl.program_id(0) == 0)
            def _():
                sm = ins[4 * n]
                outs[n][...] = ((sm[0] + sm[1]) + sm[2]) + sm[3]

    in_specs, out_specs, operands, shapes = [], [], [], []
    for a, l in zip(sums, landed):
        _, hr, hc = a.shape
        for r in range(N_CHIPS):
            in_specs.append(pl.BlockSpec((1, hr // 2, hc), functools.partial(lambda s, ch, r: (ch[r], s, 0), r=r)))
            operands.append(a if r == 0 else l)
        out_specs.append(pl.BlockSpec((hr // 2, hc), lambda s, ch: (s, 0)))
        shapes.append(jax.ShapeDtypeStruct((hr, hc), F32))
    if small_landed is not None:
        in_specs.append(pl.BlockSpec(small_landed.shape, lambda s, ch: (0, 0, 0)))
        out_specs.append(pl.BlockSpec(small_landed.shape[1:], lambda s, ch: (0, 0)))
        operands.append(small_landed)
        shapes.append(jax.ShapeDtypeStruct(small_landed.shape[1:], F32))
    return pl.pallas_call(
        body, name=name,
        grid_spec=pltpu.PrefetchScalarGridSpec(num_scalar_prefetch=1, grid=(2,), in_specs=in_specs, out_specs=out_specs),
        out_shape=[pltpu.HBM(s.shape, s.dtype) for s in shapes], compiler_params=_plain_params(1),
    )(chips, *[_pin(a) for a in operands])


N_STEPS = 8


def _adam_update(w, g, m, v):
    m = ADAM_B1 * m + (1.0 - ADAM_B1) * g
    v = ADAM_B2 * v + (1.0 - ADAM_B2) * (g * g)
    m_hat = m * (1.0 / (1.0 - ADAM_B1 ** ADAM_STEP))
    v_hat = v * (1.0 / (1.0 - ADAM_B2 ** ADAM_STEP))
    return -ADAM_LR * (m_hat / (jnp.sqrt(v_hat) + ADAM_EPS) + ADAM_WD * w), m, v


def adamw_big(ws, g_halves, ms, vs, core):
    n = len(ws)
    per_layer = N_STEPS // DEPTH
    per_half = per_layer // 2

    def body(core_ref, *refs):
        s = pl.program_id(0)
        first_layer = s < per_layer
        mine = ((s % per_layer) // per_half) == core_ref[0]
        for i in range(n):
            w, m, v = (refs[k * n + i][...] for k in range(3))
            g00, g01, g10, g11 = (refs[3 * n + 4 * i + k][...] for k in range(4))
            g = jnp.where(first_layer, jnp.where(mine, g00, g01), jnp.where(mine, g10, g11))[None]
            delta, m, v = _adam_update(w, g, m, v)
            for k, val in enumerate((g, delta, m, v)):
                refs[7 * n + k * n + i][...] = val

    whole = lambda a: pl.BlockSpec((1, a.shape[1] // per_layer, a.shape[2]),
                                   lambda s, cr: (s // per_layer, s % per_layer, 0))

    def half(a, layer, own):
        def index(s, cr):
            first = layer * per_layer + per_half * jnp.where(own, cr[0], 1 - cr[0])
            return (jnp.clip(s - first, 0, per_half - 1), 0)
        return pl.BlockSpec((a.shape[1] // per_layer, a.shape[2]), index)

    g_specs, g_ops = [], []
    for a, halves in zip(ws, g_halves):
        g_specs += [half(a, 0, True), half(a, 0, False), half(a, 1, True), half(a, 1, False)]
        g_ops += [_pin(a) for a in halves]
    outs = pl.pallas_call(
        body, name="adamw_big",
        grid_spec=pltpu.PrefetchScalarGridSpec(
            num_scalar_prefetch=1, grid=(N_STEPS,), in_specs=[whole(a) for a in ws] * 3 + g_specs,
            out_specs=[whole(a) for a in ws] * 4),
        out_shape=[pltpu.HBM(a.shape, F32) for a in ws] * 4,
        compiler_params=_layer_params(),
    )(core, *[_pin(a) for a in (*ws, *ms, *vs)], *g_ops)
    return outs[:n], outs[n:2 * n], outs[2 * n:3 * n], outs[3 * n:]


def adamw_small(ws, gs, ms, vs):
    n = len(ws)

    def body(*refs):
        for i in range(n):
            outs = _adam_update(*(refs[k * n + i][...] for k in range(4)))
            for k, val in enumerate(outs):
                refs[4 * n + k * n + i][...] = val

    specs = [_const_block(a.shape) for a in ws]
    outs = pl.pallas_call(
        body, name="adamw_small", grid=(1,), in_specs=specs * 4, out_specs=specs * 3,
        out_shape=[jax.ShapeDtypeStruct(a.shape, F32) for a in ws] * 3,
        compiler_params=_plain_params(1),
    )(*ws, *gs, *ms, *vs)
    return outs[:n], outs[n:2 * n], outs[2 * n:]


_SMALL = (("w_s", (DEPTH, SGU_G, BLK, BLK)), ("ln_in_g", (D_MODEL,)), ("ln_in_b", (D_MODEL,)), ("b_in", (DEPTH, N_COLS)),
          ("sinks", (DEPTH, 8)), ("vn_g", (DEPTH, SGU_W)), ("vn_b", (DEPTH, SGU_W)),
          ("b_s", (DEPTH, SGU_G, BLK)), ("b_out", (DEPTH, D_MODEL)), ("ln_g", (DEPTH, D_MODEL)), ("ln_b", (DEPTH, D_MODEL)))
_BIG = ("w_in", "p_a", "p_b", "w_out")


def _size(shape):
    n = 1
    for d in shape:
        n *= d
    return n


def _pack_small(vals, last_row=None):
    rows = []
    for name, shape in _SMALL:
        v = vals[name]
        if _size(shape) % 128:
            v = jnp.pad(v.reshape((1, -1)), ((0, 0), (0, (-_size(shape)) % 128)))
        rows.append(v.reshape((-1, 128)))
    used = sum(r.shape[0] for r in rows)
    tail = jnp.zeros((SMALL_ROWS - used, 128), F32)
    if last_row is not None:
        tail = tail.at[-1, 0].set(last_row)
    return jnp.concatenate([rows[0], jnp.concatenate(rows[1:] + [tail], axis=0)], axis=0)


def _unpack_small(packed):
    out, pos = {}, 0
    for name, shape in _SMALL:
        n = -(-_size(shape) // 128)
        rows = packed[pos:pos + n]
        out[name] = (rows.reshape((-1,))[:_size(shape)] if _size(shape) % 128 else rows).reshape(shape)
        pos += n
    return out


def kernel(x, ln_in_g, ln_in_b, w_in, b_in, sinks, vn_g, vn_b, w_s, b_s, p_a, p_b, w_out, b_out, ln_g, ln_b, loss_target, m_ln_in_g, m_ln_in_b, m_w_in, m_b_in, m_sinks, m_vn_g, m_vn_b, m_w_s, m_b_s, m_p_a, m_p_b, m_w_out, m_b_out, m_ln_g, m_ln_b, v_ln_in_g, v_ln_in_b, v_w_in, v_b_in, v_sinks, v_vn_g, v_vn_b, v_w_s, v_b_s, v_p_a, v_p_b, v_w_out, v_b_out, v_ln_g, v_ln_b):
    weights = dict(ln_in_g=ln_in_g, ln_in_b=ln_in_b, w_in=w_in, b_in=b_in, sinks=sinks, vn_g=vn_g, vn_b=vn_b, w_s=w_s,
                   b_s=b_s, p_a=p_a, p_b=p_b, w_out=w_out, b_out=b_out, ln_g=ln_g, ln_b=ln_b)
    mom1 = dict(ln_in_g=m_ln_in_g, ln_in_b=m_ln_in_b, w_in=m_w_in, b_in=m_b_in, sinks=m_sinks, vn_g=m_vn_g, vn_b=m_vn_b,
                w_s=m_w_s, b_s=m_b_s, p_a=m_p_a, p_b=m_p_b, w_out=m_w_out, b_out=m_b_out, ln_g=m_ln_g, ln_b=m_ln_b)
    mom2 = dict(ln_in_g=v_ln_in_g, ln_in_b=v_ln_in_b, w_in=v_w_in, b_in=v_b_in, sinks=v_sinks, vn_g=v_vn_g, vn_b=v_vn_b,
                w_s=v_w_s, b_s=v_b_s, p_a=v_p_a, p_b=v_p_b, w_out=v_w_out, b_out=v_b_out, ln_g=v_ln_g, ln_b=v_ln_b)
    n_seq, S, _ = x.shape
    T = n_seq * S
    c = lax.axis_index("c")
    j = 2 * lax.axis_index("x") + lax.axis_index("y")
    core = c.astype(jnp.int32).reshape((1,))
    chips = jnp.stack([j] + [r + (r >= j) for r in range(N_CHIPS - 1)]).astype(jnp.int32)
    names = list(_PIECES)
    xt, tt = x.reshape((T, D_MODEL)), loss_target.reshape((T, D_MODEL))

    tview = lambda d: dict(d, w_in=jnp.swapaxes(d["w_in"], 1, 2))
    weights_t, mom1_t, mom2_t = tview(weights), tview(mom1), tview(mom2)

    def own_shards(l):
        return [weights_t[n][l].astype(MXU_DTYPE) for n in _BIG]

    def layer_weights(l, gathered):
        d = dict(zip(_PIECES, gathered))
        d.update(paT=d["pa"].T, pbT=d["pb"].T, woutT=d["wout"].T)
        d.update(bin=b_in[l][None], sinks=sinks[l], vng=vn_g[l][None], vnb=vn_b[l][None], ws=w_s[l],
                 bs=jnp.repeat(b_s[l].T, 128, axis=1), bout=b_out[l][None], lng=ln_g[l][None], lnb=ln_b[l][None])
        return d

    own0, own1 = own_shards(0), own_shards(1)
    x0, g0 = input_norm_forward(xt, ln_in_g[None], ln_in_b[None], "ln_in_fwd", comm=gather_weights(names[:1], own0[:1]))
    n_proj = T // ROWS_PROJ
    h0, g1 = project(x0, g0[0], b_in[0][None], "layer0_proj",
                     comm=gather_weights(names[1:] + names[:1], own0[1:] + own1[:1], turn_step=(13 * n_proj) // 16))
    lw0 = layer_weights(0, g0 + g1[:len(names) - 1])
    x1, h1, g2 = layer_forward_and_projection(x0, h0, lw0, g1[-1], b_in[1][None], S, "layer0_fwd_layer1_proj",
                                              comm=gather_weights(names[1:], own1[1:], turn_step=T // ROWS // 2))
    lw1 = layer_weights(1, g1[-1:] + g2)

    dres1, dh1, acc1, _ = layer_backward(x1, h1, tt, lw1, S, True, "layer1_bwd")
    dx1, dwinT1, _ = input_grads(dres1, dh1, x1, lw1["winT"], "layer1_dx_dwin")
    parts1 = [dwinT1, acc1["dpa"], acc1["dpb"], acc1["dwout"]]
    dres0, dh0, acc0, landed1 = layer_backward(x0, h0, dx1, lw0, S, False, "layer0_bwd", comm=pair_send_halves(names, parts1))
    sums1 = add_own_halves(names, parts1, landed1, core, "add_own_halves1")
    parts0 = [None, acc0["dpa"], acc0["dpb"], acc0["dwout"]]
    dx0, parts0[0], riding = input_grads(dres0, dh0, x0, lw0["winT"], "layer0_dx_dwin",
                                         comm=_both(chip_exchange(sums1), pair_send_halves(names[1:], parts0[1:])))
    from_chips1, landed0_rest = riding[:len(names)], riding[len(names):]
    totals1 = sum_chips(sums1, from_chips1, chips, "sum_chips1")
    grad_x, d_ln_in_g, d_ln_in_b, _ = input_norm_backward(xt, dx0, ln_in_g, "ln_in_bwd")

    gl = [acc0, acc1]
    per_layer = lambda key, pick: jnp.stack([pick(gl[l][key]) for l in range(DEPTH)])
    row0 = lambda key: per_layer(key, lambda a: a[0])
    small_part = _pack_small(dict(
        ln_in_g=d_ln_in_g[0], ln_in_b=d_ln_in_b[0], b_in=row0("dbin"), sinks=per_layer("dsink", lambda a: a[:, 0]),
        vn_g=row0("dvng"), vn_b=row0("dvnb"), w_s=per_layer("dws", lambda a: a),
        b_s=per_layer("dbs", lambda a: a[:, ::128].T), b_out=row0("dbout"), ln_g=row0("dlng"), ln_b=row0("dlnb")),
        last_row=acc1["loss"][0, 0])
    first = _run_comm(pair_send_halves(names[:1], parts0[:1], small_part, extra=totals1), "pair_send_halves0")
    landed0, small_landed, sib_totals1 = first[:1] + landed0_rest, first[1], first[2:]
    sums0 = add_own_halves(names, parts0, landed0, core, "add_own_halves0")
    small_sum = add_small_half(small_part, small_landed, core)
    *from_chips0, small_from_chips = _run_comm(chip_exchange(sums0, small_sum), "chip_exchange0")
    *totals0, small_total = sum_chips(sums0, from_chips0, chips, "sum_chips0", small_landed=small_from_chips)
    *sib_totals0, sib_small = _run_comm(pair_send_totals(totals0 + [small_total]), "pair_send_totals0")

    def both_halves(mine, theirs):
        return jnp.where(c == 0, jnp.concatenate([mine, theirs], axis=0), jnp.concatenate([theirs, mine], axis=0))

    g_halves = [(totals0[pi], sib_totals0[pi], totals1[pi], sib_totals1[pi]) for pi in range(len(_BIG))]
    g_small = both_halves(small_total, sib_small)
    loss = g_small[-1, 0]

    big = adamw_big([weights_t[n] for n in _BIG], g_halves, [mom1_t[n] for n in _BIG], [mom2_t[n] for n in _BIG], core)
    small_names = [n for n, _ in _SMALL]
    as2d = lambda a: a.reshape((1, -1)) if a.ndim == 1 else a
    g_named = _unpack_small(g_small)
    small = adamw_small(*[[as2d(d[n]) for n in small_names] for d in (weights, g_named, mom1, mom2)])

    def named(big_outs, small_outs):
        d = dict(zip(_BIG, big_outs))
        d["w_in"] = jnp.swapaxes(d["w_in"], 1, 2)
        d.update({n: o.reshape(weights[n].shape) for n, o in zip(small_names, small_outs)})
        return d

    order = ("ln_in_g", "ln_in_b", "w_in", "b_in", "sinks", "vn_g", "vn_b", "w_s", "b_s", "p_a", "p_b", "w_out", "b_out",
             "ln_g", "ln_b")
    res = [loss, grad_x.reshape(x.shape)]
    groups = [named(big[0], [g_named[n] for n in small_names])] + [named(big[k + 1], small[k]) for k in range(3)]
    for group in groups:
        res.extend(group[n] for n in order)
    return tuple(res)
```

```python
import functools

import jax
import jax.numpy as jnp
from jax import lax
from jax.experimental import pallas as pl
from jax.experimental.pallas import tpu as pltpu

F32 = jnp.float32
MXU_DTYPE = jnp.bfloat16

D_MODEL = 1024
DEPTH = 2
HEAD_DIM = 64
ATTN_W = 512
KV_W = 128
BLK = 128
SGU_W = 512
SGU_G = 4
N_COLS = 4864
C_Q, C_K, C_V, C_GA, C_UB, C_VB, C_GB, C_RA, C_RB = 0, 512, 640, 768, 1280, 1792, 2304, 2816, 3840
ALPHA = (2.0 * DEPTH) ** 0.25
LN_EPS = 1e-5
SCALE = HEAD_DIM ** -0.5
NEG = float(jnp.finfo(jnp.float32).min)
GELU_C = 0.7978845608028654
GELU_A = 0.044715

ADAM_LR, ADAM_B1, ADAM_B2, ADAM_EPS, ADAM_WD, ADAM_STEP = 0.001, 0.9, 0.999, 1e-08, 0.01, 10

V7X_VMEM_BYTES = 64 * 1024 * 1024
V7X_VMEM_RESERVE = 3 * 1024 * 1024
ROWS = 256
ROWS_BWD = 256
ROWS_PROJ = 512

MESH = pl.DeviceIdType.MESH
N_CHIPS = 4


def _mm(a, b):
    return jnp.dot(a.astype(MXU_DTYPE), b.astype(MXU_DTYPE), preferred_element_type=F32)


def _mm_nt(a, b):
    return lax.dot_general(a.astype(MXU_DTYPE), b.astype(MXU_DTYPE), (((1,), (1,)), ((), ())),
                           preferred_element_type=F32)


def _mm_tn(a, b):
    return lax.dot_general(a.astype(MXU_DTYPE), b.astype(MXU_DTYPE), (((0,), (0,)), ((), ())),
                           preferred_element_type=F32)


def _sigmoid(x):
    return 0.5 * jnp.tanh(0.5 * x) + 0.5


def _gelu_parts(x):
    x2 = x * x
    u = 0.5 * jnp.tanh(x * (GELU_C + (GELU_C * GELU_A) * x2)) + 0.5
    dg = u * (1.0 + x * (1.0 - u) * (2.0 * GELU_C + (6.0 * GELU_C * GELU_A) * x2))
    return x * u, dg


def _ln_fwd(z, g, b):
    mu = jnp.mean(z, axis=-1, keepdims=True)
    zc = z - mu
    var = jnp.mean(zc * zc, axis=-1, keepdims=True)
    rstd = lax.rsqrt(var + LN_EPS)
    zhat = zc * rstd
    return zhat * g + b, zhat, rstd


def _ln_bwd(dy, zhat, rstd, g):
    dzh = dy * g
    m1 = jnp.mean(dzh, axis=-1, keepdims=True)
    m2 = jnp.mean(dzh * zhat, axis=-1, keepdims=True)
    return rstd * (dzh - m1 - zhat * m2)


def _colsum(v):
    return jnp.sum(v, axis=0, keepdims=True)


def _lane_lo(rows):
    return lax.broadcasted_iota(jnp.int32, (rows, 128), 1) < HEAD_DIM


def _dup_head(x2, hk, lo):
    xr = pltpu.roll(x2, HEAD_DIM, 1)
    return jnp.where(lo, x2, xr) if hk == 0 else jnp.where(lo, xr, x2)


def _fold_head(y0, y1, lo):
    f0 = y0 + pltpu.roll(y0, HEAD_DIM, 1)
    f1 = y1 + pltpu.roll(y1, HEAD_DIM, 1)
    return jnp.where(lo, f0, f1)


def _stack_heads(t0, t1, lo):
    z = jnp.zeros_like(t0)
    return jnp.concatenate([jnp.where(lo, t0, z), jnp.where(lo, z, t0),
                            jnp.where(lo, t1, z), jnp.where(lo, z, t1)], axis=0)


def _unstack_heads(o, lo):
    return (jnp.where(lo, o[0:128], o[128:256]), jnp.where(lo, o[256:384], o[384:512]))


def _band_bias(start):
    row = lax.broadcasted_iota(jnp.int32, (4 * BLK, 2 * BLK), 0) & (BLK - 1)
    kpos = lax.broadcasted_iota(jnp.int32, (4 * BLK, 2 * BLK), 1)
    valid = (kpos > row) & (kpos <= row + BLK)
    if start is not None:
        valid = valid & (jnp.logical_not(start) | (kpos >= BLK))
    return jnp.where(valid, 0.0, NEG)


def _attn_probs(s, sink4, bias):
    s = s * SCALE + bias
    r1 = lax.broadcasted_iota(jnp.int32, (4 * BLK, 1), 0)
    sk = jnp.where(r1 < BLK, sink4[0], jnp.where(r1 < 2 * BLK, sink4[1], jnp.where(r1 < 3 * BLK, sink4[2], sink4[3])))
    m = jnp.maximum(jnp.max(s, axis=-1, keepdims=True), sk)
    p = jnp.exp(s - m)
    es = jnp.exp(sk - m)
    inv = 1.0 / (jnp.sum(p, axis=-1, keepdims=True) + es)
    return p * inv, es * inv


_PROJ_CHUNK = 512


def _kv_blocks(kvh, h_s, nb):
    ks = [kvh[:, 0:KV_W]] + [h_s[n * BLK:(n + 1) * BLK, C_K:C_K + KV_W] for n in range(nb)]
    vs = [kvh[:, KV_W:2 * KV_W]] + [h_s[n * BLK:(n + 1) * BLK, C_V:C_V + KV_W] for n in range(nb)]
    return ks, vs


def _attn_operands(h_s, ks, vs, n, hk, lo, lo2):
    rows = slice(n * BLK, (n + 1) * BLK)
    kd = _dup_head(jnp.concatenate([ks[n], ks[n + 1]], axis=0), hk, lo2)
    vd = _dup_head(jnp.concatenate([vs[n], vs[n + 1]], axis=0), hk, lo2)
    c0 = C_Q + 2 * hk * 128
    qs = _stack_heads(h_s[rows, c0:c0 + 128], h_s[rows, c0 + 128:c0 + 256], lo)
    return qs, kd, vd


def _sgu_weights(w):
    tri = (lax.broadcasted_iota(jnp.int32, (BLK, BLK), 0) >= lax.broadcasted_iota(jnp.int32, (BLK, BLK), 1))
    return tri, [jnp.where(tri, w["ws"][g], 0.0) for g in range(SGU_G)]


def _layer_forward(x, kvh, start, w, h_s, attn_s, mix_s, keep, tick=lambda: None):
    R = x.shape[0]
    nb = R // BLK
    lo = _lane_lo(BLK)
    lo2 = _lane_lo(2 * BLK)
    ks, vs = _kv_blocks(kvh, h_s, nb)
    sinks = [w["sinks"][j] for j in range(8)]
    bias_first = _band_bias(start)
    bias_rest = _band_bias(None) if nb > 1 else None
    pairs = [(n, hk) for n in range(nb) for hk in range(2)]
    ops = [_attn_operands(h_s, ks, vs, n, hk, lo, lo2) for n, hk in pairs]
    scores = [_mm_nt(qs, kd) for qs, kd, _ in ops]
    probs_l = [_attn_probs(s, sinks[4 * hk:4 * hk + 4], bias_first if n == 0 else bias_rest)
               for s, (n, hk) in zip(scores, pairs)]
    outs = [_mm(p, vd) for (p, _), (_, _, vd) in zip(probs_l, ops)]
    for o, (n, hk) in zip(outs, pairs):
        rows = slice(n * BLK, (n + 1) * BLK)
        t0, t1 = _unstack_heads(o, lo)
        attn_s[rows, 2 * hk * 128:(2 * hk + 1) * 128] = t0
        attn_s[rows, (2 * hk + 1) * 128:(2 * hk + 2) * 128] = t1
        tick()
    ga = h_s[:, C_GA:C_GA + ATTN_W]
    ya = attn_s[...] * (ga * _sigmoid(ga))
    tick()
    gu, _ = _gelu_parts(h_s[:, C_UB:C_UB + SGU_W])
    tick()
    gv, _ = _gelu_parts(h_s[:, C_VB:C_VB + SGU_W])
    vn, _, _ = _ln_fwd(gv, w["vng"][...], w["vnb"][...])
    tick()
    _, wms = _sgu_weights(w)
    tiles = [(slice(n * BLK, (n + 1) * BLK), g, slice(g * 128, (g + 1) * 128)) for n in range(nb) for g in range(SGU_G)]
    mixes = [_mm(wms[g], vn[rows, cols]) for rows, g, cols in tiles]
    for m, (rows, g, cols) in zip(mixes, tiles):
        mix_s[rows, cols] = m + w["bs"][:, cols]
    gb = h_s[:, C_GB:C_GB + SGU_W]
    yb = gu * mix_s[...] * (gb * _sigmoid(gb))
    tick()
    a = _mm(ya, w["pa"][...])
    b = _mm(yb, w["pb"][...])
    tick()
    merged = _sigmoid(h_s[:, C_RA:C_RA + D_MODEL]) * a + _sigmoid(h_s[:, C_RB:C_RB + D_MODEL]) * b
    tick()
    out = _mm(merged, w["wout"][...]) + w["bout"][...]
    y, zhat, rstd = _ln_fwd(ALPHA * x + out, w["lng"][...], w["lnb"][...])
    if not keep:
        return y, None
    return y, dict(ks=ks, vs=vs, probs=probs_l, ya=ya, yb=yb, a=a, b=b, merged=merged, zhat=zhat, rstd=rstd)


def _dsilu(g, sg):
    return sg * (1.0 + g * (1.0 - sg))


_W_NAMES = ("sinks", "vng", "vnb", "ws", "bs", "pa", "pb", "wout", "bout", "lng", "lnb")
_WT_NAMES = ("paT", "pbT", "woutT")
_ACC_NAMES = ("dwout", "dpa", "dpb", "dbin", "dbout", "dlng", "dlnb", "dvng", "dvnb", "dws", "dbs", "dsink", "loss")


def _make_fwd_proj_body(R, S, nT):
    def body(x_ref, h_ref, kvh_ref, wn_ref, bn_ref, *rest):
        w = dict(zip(_W_NAMES, rest[:len(_W_NAMES)]))
        y_ref, hn_ref, attn_s, mix_s, yprev_s = rest[len(_W_NAMES):]

        @pl.when(pl.program_id(0) == 0)
        def _():
            yprev_s[...] = jnp.zeros(yprev_s.shape, yprev_s.dtype)

        t = jnp.minimum(pl.program_id(0), nT - 1)
        start = (t % (S // R)) == 0
        chunks = [(c0, min(c0 + _PROJ_CHUNK, N_COLS)) for c0 in range(0, N_COLS, _PROJ_CHUNK)]

        def tick():
            if chunks:
                c0, c1 = chunks.pop(0)
                hn_ref[:, c0:c1] = _mm_nt(yprev_s[...], wn_ref[c0:c1, :]) + bn_ref[:, c0:c1]

        y, _ = _layer_forward(x_ref[...], kvh_ref[...], start, w, h_ref, attn_s, mix_s, keep=False, tick=tick)
        while chunks:
            tick()
        y_ref[...] = y
        yprev_s[...] = y.astype(yprev_s.dtype)
    return body


def _make_bwd_body(R, S, nT, is_last):
    nb = R // BLK
    names = _W_NAMES + _WT_NAMES
    nw = len(names)

    def body(x_ref, h_s, kvh_ref, aux_ref, *rest):
        w = dict(zip(names, rest[:nw]))
        dres_ref, dh_ref = rest[nw:nw + 2]
        acc = dict(zip(_ACC_NAMES, rest[nw + 2:nw + 2 + len(_ACC_NAMES)]))
        attn_s, mix_s, ckv_s = rest[nw + 2 + len(_ACC_NAMES):]
        i = pl.program_id(0)
        start = ((nT - 1 - i) % (S // R)) == 0
        lo = _lane_lo(BLK)
        lo2 = _lane_lo(2 * BLK)

        @pl.when(i == 0)
        def _():
            for name in _ACC_NAMES:
                acc[name][...] = jnp.zeros(acc[name].shape, F32)
            ckv_s[...] = jnp.zeros(ckv_s.shape, F32)

        x = x_ref[...]
        y, sv = _layer_forward(x, kvh_ref[...], start, w, h_s, attn_s, mix_s, keep=True)
        if is_last:
            diff = y - aux_ref[...]
            part = 0.5 * jnp.sum(jnp.mean(diff * diff, axis=-1, keepdims=True), axis=0, keepdims=True)
            acc["loss"][...] += jnp.broadcast_to(part, acc["loss"].shape)
            dy = diff * (1.0 / D_MODEL)
        else:
            dy = aux_ref[...]

        def put(c0, val, rows=slice(None)):
            width = val.shape[1]
            dh_ref[rows, c0:c0 + width] = val.astype(dh_ref.dtype)
            acc["dbin"][:, c0:c0 + width] += _colsum(val)

        acc["dlng"][...] += _colsum(dy * sv["zhat"])
        acc["dlnb"][...] += _colsum(dy)
        dz = _ln_bwd(dy, sv["zhat"], sv["rstd"], w["lng"][...])
        dres_ref[...] = ALPHA * dz
        acc["dbout"][...] += _colsum(dz)
        acc["dwout"][...] += _mm_tn(sv["merged"], dz)
        dmerged = _mm(dz, w["woutT"][...])
        sa = _sigmoid(h_s[:, C_RA:C_RA + D_MODEL])
        da = dmerged * sa
        put(C_RA, da * sv["a"] * (1.0 - sa))
        sb = _sigmoid(h_s[:, C_RB:C_RB + D_MODEL])
        db = dmerged * sb
        put(C_RB, db * sv["b"] * (1.0 - sb))
        acc["dpa"][...] += _mm_tn(sv["ya"], da)
        acc["dpb"][...] += _mm_tn(sv["yb"], db)
        dya = _mm(da, w["paT"][...])
        dyb = _mm(db, w["pbT"][...])
        ga = h_s[:, C_GA:C_GA + ATTN_W]
        sga = _sigmoid(ga)
        put(C_GA, dya * attn_s[...] * _dsilu(ga, sga))
        attn_s[...] = dya * (ga * sga)
        ks, vs = sv["ks"], sv["vs"]
        pairs = [(n, hk) for n in range(nb) for hk in range(2)]
        ops = [_attn_operands(h_s, ks, vs, n, hk, lo, lo2) for n, hk in pairs]
        doss = [_stack_heads(attn_s[n * BLK:(n + 1) * BLK, 2 * hk * 128:(2 * hk + 1) * 128],
                             attn_s[n * BLK:(n + 1) * BLK, (2 * hk + 1) * 128:(2 * hk + 2) * 128], lo) for n, hk in pairs]
        dps = [_mm_nt(dos, vd) for dos, (_, _, vd) in zip(doss, ops)]
        gb = h_s[:, C_GB:C_GB + SGU_W]
        sgb = _sigmoid(gb)
        gu, dgu = _gelu_parts(h_s[:, C_UB:C_UB + SGU_W])
        mixed = mix_s[...]
        put(C_GB, dyb * (gu * mixed) * _dsilu(gb, sgb))
        dsgu = dyb * (gb * sgb)
        put(C_UB, dsgu * mixed * dgu)
        dmixed = dsgu * gu
        dsss = []
        for dp, (probs, ps), (n, hk) in zip(dps, sv["probs"], pairs):
            delta = jnp.sum(probs * dp, axis=-1, keepdims=True)
            dsk = -(ps * delta)
            for g in range(4):
                j = 4 * hk + g
                acc["dsink"][j:j + 1, :] += jnp.broadcast_to(_colsum(dsk[g * BLK:(g + 1) * BLK]), (1, 128))
            dsss.append(probs * (dp - delta) * SCALE)
        dqs = [_mm(dss, kd) for dss, (_, kd, _) in zip(dsss, ops)]
        ydks = [_mm_tn(dss, qs) for dss, (qs, _, _) in zip(dsss, ops)]
        ydvs = [_mm_tn(probs, dos) for (probs, _), dos in zip(sv["probs"], doss)]
        gv, dgv = _gelu_parts(h_s[:, C_VB:C_VB + SGU_W])
        vn, vhat, vrstd = _ln_fwd(gv, w["vng"][...], w["vnb"][...])
        tri, wms = _sgu_weights(w)
        tiles = [(slice(n * BLK, (n + 1) * BLK), g, slice(g * 128, (g + 1) * 128)) for n in range(nb) for g in range(SGU_G)]
        dw_parts = [_mm_nt(dmixed[rows, cols], vn[rows, cols]) for rows, g, cols in tiles]
        dvn_parts = [_mm_tn(wms[g], dmixed[rows, cols]) for rows, g, cols in tiles]
        for g in range(SGU_G):
            tot = dw_parts[g]
            for n in range(1, nb):
                tot = tot + dw_parts[n * SGU_G + g]
            acc["dws"][g] += jnp.where(tri, tot, 0.0)
        for part, (rows, g, cols) in zip(dvn_parts, tiles):
            mix_s[rows, cols] = part
        dbs_part = dmixed[0:BLK]
        for n in range(1, nb):
            dbs_part = dbs_part + dmixed[n * BLK:(n + 1) * BLK]
        acc["dbs"][...] += dbs_part
        for dq, (n, hk) in zip(dqs, pairs):
            q0, q1 = _unstack_heads(dq, lo)
            put(C_Q + 2 * hk * 128, q0, slice(n * BLK, (n + 1) * BLK))
            put(C_Q + (2 * hk + 1) * 128, q1, slice(n * BLK, (n + 1) * BLK))
        dks = [jnp.zeros((BLK, KV_W), F32) for _ in range(nb + 1)]
        dvs = [jnp.zeros((BLK, KV_W), F32) for _ in range(nb + 1)]
        for n in range(nb):
            dk2 = _fold_head(ydks[2 * n], ydks[2 * n + 1], lo2)
            dv2 = _fold_head(ydvs[2 * n], ydvs[2 * n + 1], lo2)
            dks[n] = dks[n] + dk2[0:BLK]
            dks[n + 1] = dks[n + 1] + dk2[BLK:2 * BLK]
            dvs[n] = dvs[n] + dv2[0:BLK]
            dvs[n + 1] = dvs[n + 1] + dv2[BLK:2 * BLK]
        dks[nb] = dks[nb] + ckv_s[:, 0:KV_W]
        dvs[nb] = dvs[nb] + ckv_s[:, KV_W:2 * KV_W]
        ckv_s[:, 0:KV_W] = dks[0]
        ckv_s[:, KV_W:2 * KV_W] = dvs[0]
        put(C_K, jnp.concatenate(dks[1:], axis=0))
        put(C_V, jnp.concatenate(dvs[1:], axis=0))
        dvn = mix_s[...]
        acc["dvng"][...] += _colsum(dvn * vhat)
        acc["dvnb"][...] += _colsum(dvn)
        put(C_VB, _ln_bwd(dvn, vhat, vrstd, w["vng"][...]) * dgv)

        @pl.when(i == nT - 1)
        def _():
            for g in range(SGU_G):
                cols = slice(g * 128, (g + 1) * 128)
                tot = jnp.sum(acc["dbs"][:, cols], axis=1, keepdims=True)
                acc["dbs"][:, cols] = jnp.broadcast_to(tot, (BLK, 128))

    return body


def _resident():
    return pl.BlockSpec(memory_space=pltpu.VMEM)


_BIG_RESIDENT = ("pa", "pb", "wout", "paT", "pbT", "woutT", "dwout", "dpa", "dpb")


def _const_block(shape):
    return pl.BlockSpec(tuple(shape), lambda *_: (0,) * len(shape))


def _operand_spec(name, shape):
    if name == "sinks":
        return pl.BlockSpec(memory_space=pltpu.SMEM)
    return _resident() if name in _BIG_RESIDENT else _const_block(shape)


def _weight_specs(names, wts):
    return [_operand_spec(n, wts[n].shape) for n in names]


def _nbytes(a):
    n = jnp.dtype(a.dtype).itemsize
    for d in a.shape:
        n *= d
    return n


def _layer_params(resident=()):
    scoped = V7X_VMEM_BYTES - V7X_VMEM_RESERVE - sum(_nbytes(a) for a in resident)
    return pltpu.CompilerParams(dimension_semantics=("arbitrary",), vmem_limit_bytes=scoped)


class _Comm:
    def __init__(self, ins, out_shapes, sems, start, finish, aliases=None, turn=None, turn_step=None):
        self.ins = [pltpu.with_memory_space_constraint(a, pltpu.HBM) for a in ins]
        self.out_shapes = [pltpu.HBM(s.shape, s.dtype) for s in out_shapes]
        self.sems = list(sems)
        self.start, self.finish, self.aliases = start, finish, dict(aliases or {})
        self.turn, self.turn_step = turn, turn_step


_ANY = pl.BlockSpec(memory_space=pltpu.HBM)
_DMA = pltpu.SemaphoreType.DMA


def _call(body, *, name, grid, in_specs, out_specs, out_shape, scratch_shapes, compiler_params, operands, comm=None):
    if comm is None:
        res = pl.pallas_call(body, name=name, grid=grid, in_specs=in_specs, out_specs=out_specs, out_shape=out_shape,
                             scratch_shapes=scratch_shapes, compiler_params=compiler_params)(*operands)
        return list(res), []
    n_in, n_out, n_scr = len(in_specs), len(out_specs), len(scratch_shapes)
    ci, co = len(comm.ins), len(comm.out_shapes)
    last = grid[0] - 1

    def hosted(*refs):
        ins, cin = refs[:n_in], refs[n_in:n_in + ci]
        p = n_in + ci
        outs, cout = refs[p:p + n_out], refs[p + n_out:p + n_out + co]
        p += n_out + co
        scr, sems = refs[p:p + n_scr], refs[p + n_scr:]
        pl.when(pl.program_id(0) == 0)(lambda: comm.start(cin, cout, sems))
        body(*ins, *outs, *scr)
        if comm.turn is not None:
            at = last if comm.turn_step is None else comm.turn_step
            pl.when(pl.program_id(0) == at)(lambda: comm.turn(cin, cout, sems))
        pl.when(pl.program_id(0) == last)(lambda: comm.finish(cin, cout, sems))

    res = pl.pallas_call(
        hosted, name=name, grid=grid, in_specs=list(in_specs) + [_ANY] * ci, out_specs=list(out_specs) + [_ANY] * co,
        out_shape=list(out_shape) + comm.out_shapes, scratch_shapes=list(scratch_shapes) + comm.sems,
        input_output_aliases={n_in + a: n_out + b for a, b in comm.aliases.items()},
        compiler_params=compiler_params)(*operands, *comm.ins)
    return list(res[:n_out]), list(res[n_out:])


def _both(a, b):
    ni, no, ns = len(a.ins), len(a.out_shapes), len(a.sems)

    def each(f, g):
        def run(cin, cout, sems):
            f(cin[:ni], cout[:no], sems[:ns])
            g(cin[ni:], cout[no:], sems[ns:])
        return run

    both = _Comm([], [], a.sems + b.sems, each(a.start, b.start), each(a.finish, b.finish))
    both.ins, both.out_shapes = a.ins + b.ins, a.out_shapes + b.out_shapes
    return both


def _run_comm(comm, name):
    ci, co = len(comm.ins), len(comm.out_shapes)

    def body(*refs):
        cin, cout, sems = refs[:ci], refs[ci:ci + co], refs[ci + co:]
        comm.start(cin, cout, sems)
        if comm.turn is not None:
            comm.turn(cin, cout, sems)
        comm.finish(cin, cout, sems)

    return list(pl.pallas_call(body, name=name, in_specs=[_ANY] * ci, out_specs=[_ANY] * co, out_shape=comm.out_shapes,
                               scratch_shapes=comm.sems, input_output_aliases=comm.aliases)(*comm.ins))


def project(x, winT, b, name, comm=None):
    T = x.shape[0]
    R = ROWS_PROJ

    def body(x_ref, w_ref, b_ref, h_ref):
        xb = x_ref[...].astype(MXU_DTYPE)
        for c0 in range(0, N_COLS, _PROJ_CHUNK):
            c1 = min(c0 + _PROJ_CHUNK, N_COLS)
            h_ref[:, c0:c1] = _mm_nt(xb, w_ref[c0:c1, :]) + b_ref[:, c0:c1]

    (h,), extra = _call(
        body, name=name, grid=(T // R,),
        in_specs=[pl.BlockSpec((R, D_MODEL), lambda t: (t, 0)), _resident(), _const_block(b.shape)],
        out_specs=[pl.BlockSpec((R, N_COLS), lambda t: (t, 0))],
        out_shape=[jax.ShapeDtypeStruct((T, N_COLS), F32)], scratch_shapes=[],
        compiler_params=_layer_params([winT]), operands=(x, winT, b), comm=comm)
    return h, extra


_KV_BLOCK = C_K // (2 * KV_W)


def layer_forward_and_projection(x, h, wts, winT_next, b_next, S, name, comm=None):
    T = x.shape[0]
    R = ROWS
    nT = T // R
    npt = R // BLK
    cur = lambda t: jnp.minimum(t, nT - 1)
    (y, hn), extra = _call(
        _make_fwd_proj_body(R, S, nT), name=name, grid=(nT + 1,),
        in_specs=[pl.BlockSpec((R, D_MODEL), lambda t: (cur(t), 0)), pl.BlockSpec((R, N_COLS), lambda t: (cur(t), 0)),
                  pl.BlockSpec((BLK, 2 * KV_W), lambda t: (jnp.maximum(cur(t) * npt - 1, 0), _KV_BLOCK)),
                  _resident(), _const_block(b_next.shape)] + _weight_specs(_W_NAMES, wts),
        out_specs=[pl.BlockSpec((R, D_MODEL), lambda t: (cur(t), 0)),
                   pl.BlockSpec((R, N_COLS), lambda t: (jnp.maximum(t - 1, 0), 0))],
        out_shape=[jax.ShapeDtypeStruct((T, D_MODEL), F32), jax.ShapeDtypeStruct((T, N_COLS), F32)],
        scratch_shapes=[pltpu.VMEM((R, ATTN_W), F32), pltpu.VMEM((R, SGU_W), F32), pltpu.VMEM((R, D_MODEL), MXU_DTYPE)],
        compiler_params=_layer_params([winT_next] + [wts[n] for n in _W_NAMES if n in _BIG_RESIDENT]),
        operands=(x, h, h, winT_next, b_next, *[wts[n] for n in _W_NAMES]), comm=comm)
    return y, hn, extra


_ACC_SHAPES = dict(dwout=(D_MODEL, D_MODEL), dpa=(ATTN_W, D_MODEL), dpb=(SGU_W, D_MODEL), dbin=(1, N_COLS),
                   dbout=(1, D_MODEL), dlng=(1, D_MODEL), dlnb=(1, D_MODEL), dvng=(1, SGU_W), dvnb=(1, SGU_W),
                   dws=(SGU_G, BLK, BLK), dbs=(BLK, SGU_W), dsink=(8, 128), loss=(8, 128))


def layer_backward(x, h, aux, wts, S, is_last, name, comm=None):
    T = x.shape[0]
    R = ROWS_BWD
    nT = T // R
    npt = R // BLK
    rev = lambda i: (nT - 1 - i, 0)
    halo = lambda i: (jnp.maximum((nT - 1 - i) * npt - 1, 0), _KV_BLOCK)
    names = _W_NAMES + _WT_NAMES
    out_shape = ([jax.ShapeDtypeStruct((T, D_MODEL), F32), jax.ShapeDtypeStruct((T, N_COLS), MXU_DTYPE)]
                 + [jax.ShapeDtypeStruct(_ACC_SHAPES[n], F32) for n in _ACC_NAMES])
    outs, extra = _call(
        _make_bwd_body(R, S, nT, is_last), name=name, grid=(nT,),
        in_specs=[pl.BlockSpec((R, D_MODEL), rev), pl.BlockSpec((R, N_COLS), rev), pl.BlockSpec((BLK, 2 * KV_W), halo),
                  pl.BlockSpec((R, D_MODEL), rev)] + _weight_specs(names, wts),
        out_specs=[pl.BlockSpec((R, D_MODEL), rev), pl.BlockSpec((R, N_COLS), rev)]
        + [_operand_spec(n, _ACC_SHAPES[n]) for n in _ACC_NAMES],
        out_shape=out_shape,
        scratch_shapes=[pltpu.VMEM((R, ATTN_W), F32), pltpu.VMEM((R, SGU_W), F32), pltpu.VMEM((BLK, 2 * KV_W), F32)],
        compiler_params=_layer_params([wts[n] for n in names if n in _BIG_RESIDENT]
                                      + [jax.ShapeDtypeStruct(_ACC_SHAPES[n], F32) for n in _ACC_NAMES if n in _BIG_RESIDENT]),
        operands=(x, h, h, aux, *[wts[n] for n in names]), comm=comm)
    return outs[0], outs[1], dict(zip(_ACC_NAMES, outs[2:])), extra


def input_grads(dres, dh, x, winT, name, comm=None):
    T = x.shape[0]
    R = ROWS

    def body(dres_ref, dh_ref, x_ref, winT_ref, dx_ref, dwin_ref):
        @pl.when(pl.program_id(0) == 0)
        def _():
            dwin_ref[...] = jnp.zeros(dwin_ref.shape, F32)

        dh = dh_ref[...]
        dx_ref[...] = dres_ref[...] + jnp.dot(dh, winT_ref[...], preferred_element_type=F32)
        dwin_ref[...] += _mm_tn(dh, x_ref[...])

    row = lambda t: (t, 0)
    dwin_shape = jax.ShapeDtypeStruct((N_COLS, D_MODEL), F32)
    (dx, dwinT), extra = _call(
        body, name=name, grid=(T // R,),
        in_specs=[pl.BlockSpec((R, D_MODEL), row), pl.BlockSpec((R, N_COLS), row), pl.BlockSpec((R, D_MODEL), row),
                  _resident()],
        out_specs=[pl.BlockSpec((R, D_MODEL), row), _resident()],
        out_shape=[jax.ShapeDtypeStruct((T, D_MODEL), F32), dwin_shape], scratch_shapes=[],
        compiler_params=_layer_params([winT, dwin_shape]), operands=(dres, dh, x, winT), comm=comm)
    return dx, dwinT, extra


def input_norm_forward(x, g, b, name, comm=None):
    T = x.shape[0]
    R = ROWS_PROJ

    def body(x_ref, g_ref, b_ref, y_ref):
        y_ref[...] = _ln_fwd(x_ref[...], g_ref[...], b_ref[...])[0]

    row = lambda t: (t, 0)
    (y,), extra = _call(
        body, name=name, grid=(T // R,),
        in_specs=[pl.BlockSpec((R, D_MODEL), row), _const_block(g.shape), _const_block(b.shape)],
        out_specs=[pl.BlockSpec((R, D_MODEL), row)],
        out_shape=[jax.ShapeDtypeStruct((T, D_MODEL), F32)], scratch_shapes=[],
        compiler_params=_layer_params(), operands=(x, g, b), comm=comm)
    return y, extra


def input_norm_backward(x, dy, g, name, comm=None):
    T = x.shape[0]
    R = ROWS_PROJ

    def body(x_ref, dy_ref, g_ref, dx_ref, dg_ref, db_ref):
        @pl.when(pl.program_id(0) == 0)
        def _():
            dg_ref[...] = jnp.zeros(dg_ref.shape, F32)
            db_ref[...] = jnp.zeros(db_ref.shape, F32)

        dy = dy_ref[...]
        _, xhat, rstd = _ln_fwd(x_ref[...], g_ref[...], g_ref[...])
        dx_ref[...] = _ln_bwd(dy, xhat, rstd, g_ref[...])
        dg_ref[...] += _colsum(dy * xhat)
        db_ref[...] += _colsum(dy)

    row = lambda t: (t, 0)
    vec = jax.ShapeDtypeStruct((1, D_MODEL), F32)
    (dx, dg, db), extra = _call(
        body, name=name, grid=(T // R,),
        in_specs=[pl.BlockSpec((R, D_MODEL), row), pl.BlockSpec((R, D_MODEL), row), _const_block(g.shape)],
        out_specs=[pl.BlockSpec((R, D_MODEL), row), _const_block(vec.shape), _const_block(vec.shape)],
        out_shape=[jax.ShapeDtypeStruct((T, D_MODEL), F32), vec, vec], scratch_shapes=[],
        compiler_params=_layer_params(), operands=(x, dy, g), comm=comm)
    return dx, dg, db, extra


_PIECES = ("winT", "pa", "pb", "wout")
_WHOLE = dict(winT=(N_COLS, D_MODEL), pa=(ATTN_W, D_MODEL), pb=(SGU_W, D_MODEL), wout=(D_MODEL, D_MODEL))
_HALF = dict(winT=(N_COLS // 8, D_MODEL), pa=(ATTN_W // 2, D_MODEL // 4), pb=(SGU_W // 2, D_MODEL // 4),
             wout=(D_MODEL // 8, D_MODEL))
SMALL_ROWS = 1280


def _region(name, ref, k, h):
    hr, hc = _HALF[name]
    if name in ("pa", "pb"):
        return ref.at[pl.ds(h * hr, hr), pl.ds(k * hc, hc)]
    return ref.at[pl.ds(k * 2 * hr + h * hr, hr), :]


def _place():
    x, y, c = lax.axis_index("x"), lax.axis_index("y"), lax.axis_index("c")
    return x, y, c, 2 * x + y


def _as_chip(j, fn):
    for jj in range(N_CHIPS):
        pl.when(j == jj)(functools.partial(fn, jj))


def _remote(src, dst, send_sem, recv_sem, to):
    return pltpu.make_async_remote_copy(src_ref=src, dst_ref=dst, send_sem=send_sem, recv_sem=recv_sem,
                                        device_id=to, device_id_type=MESH)


def _core_of(k, c):
    return (k // 2, k % 2, c)


def _others(jj):
    return [k for k in range(N_CHIPS) if k != jj]


def _start_all(cps):
    for cp in cps:
        cp.start()


def _wait_all(cps):
    for cp in cps:
        cp.wait()


def gather_over_chips(names, shards, on_landed=None):
    n = len(names)

    def own(cin, cout, sems, jj):
        x, y, c, _ = _place()
        cps = []
        for i, name in enumerate(names):
            hr, hc = _HALF[name]
            place = (cout[i].at[:, pl.ds(jj * hc, hc)] if name in ("pa", "pb")
                     else cout[i].at[pl.ds(jj * 2 * hr, 2 * hr), :])
            cps.append(_remote(cin[i], place, sems[2].at[i], sems[3].at[i], (x, y, 1 - c)))
        return cps

    def copies(cin, cout, sems, jj, c):
        out = []
        for i, name in enumerate(names):
            hr = _HALF[name][0]
            for k in _others(jj):
                out.append(_remote(cin[i].at[pl.ds(c * hr, hr)], _region(name, cout[i], jj, c),
                                   sems[0].at[4 * i + k], sems[1].at[4 * i + jj], _core_of(k, c)))
        return out

    def start(cin, cout, sems):
        _, _, c, j = _place()
        _as_chip(j, lambda jj: _start_all(copies(cin, cout, sems, jj, c) + own(cin, cout, sems, jj)))

    def finish(cin, cout, sems):
        _, _, c, j = _place()

        def run(jj):
            for i, name in enumerate(names):
                for k in _others(jj):
                    land = _region(name, cout[i], k, c)
                    _remote(land, land, sems[0].at[4 * i + k], sems[1].at[4 * i + k], _core_of(k, c)).wait_recv()
                    if on_landed is not None:
                        on_landed(land, 4 * i + k, sems)
            for cp in copies(cin, cout, sems, jj, c):
                cp.wait_send()
            _wait_all(own(cin, cout, sems, jj))

        _as_chip(j, run)

    shapes = [jax.ShapeDtypeStruct(_WHOLE[nm], s.dtype) for nm, s in zip(names, shards)]
    return _Comm(shards, shapes, [_DMA((4 * n,)), _DMA((4 * n,)), _DMA((n,)), _DMA((n,))], start, finish)


def gather_weights(names, shards, turn_step=None):
    n = len(names)

    def pass_on(land, slot, sems):
        x, y, c, _ = _place()
        _remote(land, land, sems[4].at[slot], sems[5].at[slot], (x, y, 1 - c)).start()

    chips = gather_over_chips(names, shards, on_landed=pass_on)
    pair_finish = _pair_finish(names)
    return _Comm(shards, chips.out_shapes, chips.sems + [_DMA((4 * n,)), _DMA((4 * n,))], chips.start,
                 lambda cin, cout, sems: pair_finish(cin, cout, sems[4:]), turn=chips.finish, turn_step=turn_step)


def _pair_finish(names):
    def finish(cin, cout, sems):
        x, y, c, j = _place()

        def run(jj):
            for i, name in enumerate(names):
                for k in _others(jj):
                    theirs = _region(name, cout[i], k, 1 - c)
                    _remote(theirs, theirs, sems[0].at[4 * i + k], sems[1].at[4 * i + k], (x, y, 1 - c)).wait_recv()
            for i, name in enumerate(names):
                for k in _others(jj):
                    land = _region(name, cout[i], k, c)
                    _remote(land, land, sems[0].at[4 * i + k], sems[1].at[4 * i + k], (x, y, 1 - c)).wait_send()

        _as_chip(j, run)

    return finish


def pair_send_halves(names, parts, small=None, extra=()):
    n = len(names)
    hs = SMALL_ROWS // 2
    first_extra = n + (small is not None)

    def copies(cin, cout, sems):
        x, y, c, _ = _place()
        sib = (x, y, 1 - c)
        cps = []
        for i, name in enumerate(names):
            for k in range(N_CHIPS):
                cps.append(_remote(_region(name, cin[i], k, 1 - c), cout[i].at[k], sems[0].at[4 * i + k],
                                   sems[1].at[4 * i + k], sib))
        if small is not None:
            cps.append(_remote(cin[n].at[pl.ds((1 - c) * hs, hs)], cout[n], sems[0].at[4 * n], sems[1].at[4 * n], sib))
        for e in range(len(extra)):
            cps.append(_remote(cin[first_extra + e], cout[first_extra + e], sems[0].at[4 * n + 1 + e],
                               sems[1].at[4 * n + 1 + e], sib))
        return cps

    start = lambda cin, cout, sems: _start_all(copies(cin, cout, sems))
    finish = lambda cin, cout, sems: _wait_all(copies(cin, cout, sems))
    shapes = [jax.ShapeDtypeStruct((N_CHIPS,) + _HALF[nm], F32) for nm in names]
    ins = list(parts)
    if small is not None:
        shapes.append(jax.ShapeDtypeStruct((hs, 128), F32))
        ins.append(small)
    shapes += [jax.ShapeDtypeStruct(a.shape, a.dtype) for a in extra]
    ins += list(extra)
    n_sem = 4 * n + 1 + len(extra)
    return _Comm(ins, shapes, [_DMA((n_sem,)), _DMA((n_sem,))], start, finish)


def chip_exchange(sums, small=None):
    n = len(sums)

    def copies(cin, cout, sems, jj, c):
        out = []
        for k in _others(jj):
            for i in range(n):
                out.append(_remote(cin[i].at[k], cout[i].at[jj], sems[0].at[4 * i + k], sems[1].at[4 * i + jj], _core_of(k, c)))
            if small is not None:
                out.append(_remote(cin[n], cout[n].at[jj], sems[0].at[4 * n + k], sems[1].at[4 * n + jj], _core_of(k, c)))
        return out

    def start(cin, cout, sems):
        _, _, c, j = _place()

        def run(jj):
            if small is not None:
                pltpu.make_async_copy(cin[n], cout[n].at[jj], sems[2]).start()
            for cp in copies(cin, cout, sems, jj, c):
                cp.start()

        _as_chip(j, run)

    def finish(cin, cout, sems):
        _, _, c, j = _place()

        def run(jj):
            for k in _others(jj):
                for i in range(n):
                    _remote(cin[i].at[k], cout[i].at[k], sems[0].at[4 * i + k], sems[1].at[4 * i + k], _core_of(k, c)).wait_recv()
                if small is not None:
                    _remote(cin[n], cout[n].at[k], sems[0].at[4 * n + k], sems[1].at[4 * n + k], _core_of(k, c)).wait_recv()
            for cp in copies(cin, cout, sems, jj, c):
                cp.wait_send()
            if small is not None:
                pltpu.make_async_copy(cin[n], cout[n].at[jj], sems[2]).wait()

        _as_chip(j, run)

    shapes = [jax.ShapeDtypeStruct(a.shape, a.dtype) for a in sums]
    ins = list(sums)
    if small is not None:
        shapes.append(jax.ShapeDtypeStruct((N_CHIPS,) + small.shape, small.dtype))
        ins.append(small)
    return _Comm(ins, shapes, [_DMA((4 * n + 4,)), _DMA((4 * n + 4,)), _DMA(())], start, finish)


def pair_send_totals(arrs):
    n = len(arrs)

    def copies(cin, cout, sems):
        x, y, c, _ = _place()
        return [_remote(cin[i], cout[i], sems[0].at[i], sems[1].at[i], (x, y, 1 - c)) for i in range(n)]

    start = lambda cin, cout, sems: _start_all(copies(cin, cout, sems))
    finish = lambda cin, cout, sems: _wait_all(copies(cin, cout, sems))
    return _Comm(arrs, [jax.ShapeDtypeStruct(a.shape, a.dtype) for a in arrs], [_DMA((n,)), _DMA((n,))], start, finish)


def _pin(a):
    return pltpu.with_memory_space_constraint(a, pltpu.HBM)


def _plain_params(n_axes):
    return pltpu.CompilerParams(dimension_semantics=("arbitrary",) * n_axes, vmem_limit_bytes=V7X_VMEM_BYTES // 2)


def add_own_halves(names, parts, landed, core, name):
    n = len(names)

    def body(core_ref, *refs):
        for i in range(n):
            refs[2 * n + i][0] = (refs[i][...] + refs[n + i][0]).astype(MXU_DTYPE)

    in_specs, out_specs = [], []
    for nm in names:
        hr, hc = _HALF[nm]
        if nm in ("pa", "pb"):
            in_specs.append(pl.BlockSpec((hr // 2, hc), lambda k, s, cr: (cr[0] * 2 + s, k)))
        else:
            in_specs.append(pl.BlockSpec((hr // 2, hc), lambda k, s, cr: (k * 4 + cr[0] * 2 + s, 0)))
    for nm in names:
        hr, hc = _HALF[nm]
        in_specs.append(pl.BlockSpec((1, hr // 2, hc), lambda k, s, cr: (k, s, 0)))
        out_specs.append(pl.BlockSpec((1, hr // 2, hc), lambda k, s, cr: (k, s, 0)))
    return pl.pallas_call(
        body, name=name,
        grid_spec=pltpu.PrefetchScalarGridSpec(num_scalar_prefetch=1, grid=(N_CHIPS, 2), in_specs=in_specs, out_specs=out_specs),
        out_shape=[pltpu.HBM((N_CHIPS,) + _HALF[nm], MXU_DTYPE) for nm in names],
        compiler_params=_plain_params(2),
    )(core, *[_pin(a) for a in parts], *[_pin(a) for a in landed])


def add_small_half(small, landed, core):
    hs = SMALL_ROWS // 2

    def body(core_ref, a_ref, b_ref, o_ref):
        o_ref[...] = a_ref[...] + b_ref[...]

    return pl.pallas_call(
        body, name="add_small_half",
        grid_spec=pltpu.PrefetchScalarGridSpec(
            num_scalar_prefetch=1, grid=(1,),
            in_specs=[pl.BlockSpec((hs, 128), lambda s, cr: (cr[0], 0)), pl.BlockSpec((hs, 128), lambda s, cr: (0, 0))],
            out_specs=pl.BlockSpec((hs, 128), lambda s, cr: (0, 0))),
        out_shape=jax.ShapeDtypeStruct((hs, 128), F32),
    )(core, small, landed)


def sum_chips(sums, landed, chips, name, small_landed=None):
    n = len(sums)

    def body(chips_ref, *refs):
        ins, outs = refs[:4 * n + (small_landed is not None)], refs[4 * n + (small_landed is not None):]
        for i in range(n):
            tot = ins[4 * i][0].astype(F32)
            for r in range(1, N_CHIPS):
                tot = tot + ins[4 * i + r][0].astype(F32)
            outs[i][...] = tot
        if small_landed is not None:
            @pl.when(pl.program_id(0) == 0)
            def _():
                sm = ins[4 * n]
                outs[n][...] = ((sm[0] + sm[1]) + sm[2]) + sm[3]

    in_specs, out_specs, operands, shapes = [], [], [], []
    for a, l in zip(sums, landed):
        _, hr, hc = a.shape
        for r in range(N_CHIPS):
            in_specs.append(pl.BlockSpec((1, hr // 2, hc), functools.partial(lambda s, ch, r: (ch[r], s, 0), r=r)))
            operands.append(a if r == 0 else l)
        out_specs.append(pl.BlockSpec((hr // 2, hc), lambda s, ch: (s, 0)))
        shapes.append(jax.ShapeDtypeStruct((hr, hc), F32))
    if small_landed is not None:
        in_specs.append(pl.BlockSpec(small_landed.shape, lambda s, ch: (0, 0, 0)))
        out_specs.append(pl.BlockSpec(small_landed.shape[1:], lambda s, ch: (0, 0)))
        operands.append(small_landed)
        shapes.append(jax.ShapeDtypeStruct(small_landed.shape[1:], F32))
    return pl.pallas_call(
        body, name=name,
        grid_spec=pltpu.PrefetchScalarGridSpec(num_scalar_prefetch=1, grid=(2,), in_specs=in_specs, out_specs=out_specs),
        out_shape=[pltpu.HBM(s.shape, s.dtype) for s in shapes], compiler_params=_plain_params(1),
    )(chips, *[_pin(a) for a in operands])


N_STEPS = 16


def _adam_update(w, g, m, v):
    m = ADAM_B1 * m + (1.0 - ADAM_B1) * g
    v = ADAM_B2 * v + (1.0 - ADAM_B2) * (g * g)
    m_hat = m * (1.0 / (1.0 - ADAM_B1 ** ADAM_STEP))
    v_hat = v * (1.0 / (1.0 - ADAM_B2 ** ADAM_STEP))
    return -ADAM_LR * (m_hat / (jnp.sqrt(v_hat) + ADAM_EPS) + ADAM_WD * w), m, v


def adamw_big(ws, g_halves, ms, vs, core):
    n = len(ws)
    per_layer = N_STEPS // DEPTH
    per_half = per_layer // 2

    def body(core_ref, *refs):
        s = pl.program_id(0)
        first_layer = s < per_layer
        mine = ((s % per_layer) // per_half) == core_ref[0]
        for i in range(n):
            w, m, v = (refs[k * n + i][...] for k in range(3))
            g00, g01, g10, g11 = (refs[3 * n + 4 * i + k][...] for k in range(4))
            g = jnp.where(first_layer, jnp.where(mine, g00, g01), jnp.where(mine, g10, g11))[None]
            delta, m, v = _adam_update(w, g, m, v)
            for k, val in enumerate((g, delta, m, v)):
                refs[7 * n + k * n + i][...] = val

    whole = lambda a: pl.BlockSpec((1, a.shape[1] // per_layer, a.shape[2]),
                                   lambda s, cr: (s // per_layer, s % per_layer, 0))

    def half(a, layer, own):
        def index(s, cr):
            first = layer * per_layer + per_half * jnp.where(own, cr[0], 1 - cr[0])
            return (jnp.clip(s - first, 0, per_half - 1), 0)
        return pl.BlockSpec((a.shape[1] // per_layer, a.shape[2]), index)

    g_specs, g_ops = [], []
    for a, halves in zip(ws, g_halves):
        g_specs += [half(a, 0, True), half(a, 0, False), half(a, 1, True), half(a, 1, False)]
        g_ops += [_pin(a) for a in halves]
    outs = pl.pallas_call(
        body, name="adamw_big",
        grid_spec=pltpu.PrefetchScalarGridSpec(
            num_scalar_prefetch=1, grid=(N_STEPS,), in_specs=[whole(a) for a in ws] * 3 + g_specs,
            out_specs=[whole(a) for a in ws] * 4),
        out_shape=[pltpu.HBM(a.shape, F32) for a in ws] * 4,
        compiler_params=_layer_params(),
    )(core, *[_pin(a) for a in (*ws, *ms, *vs)], *g_ops)
    return outs[:n], outs[n:2 * n], outs[2 * n:3 * n], outs[3 * n:]


def adamw_small(ws, gs, ms, vs):
    n = len(ws)

    def body(*refs):
        for i in range(n):
            outs = _adam_update(*(refs[k * n + i][...] for k in range(4)))
            for k, val in enumerate(outs):
                refs[4 * n + k * n + i][...] = val

    specs = [_const_block(a.shape) for a in ws]
    outs = pl.pallas_call(
        body, name="adamw_small", grid=(1,), in_specs=specs * 4, out_specs=specs * 3,
        out_shape=[jax.ShapeDtypeStruct(a.shape, F32) for a in ws] * 3,
        compiler_params=_plain_params(1),
    )(*ws, *gs, *ms, *vs)
    return outs[:n], outs[n:2 * n], outs[2 * n:]


_SMALL = (("w_s", (DEPTH, SGU_G, BLK, BLK)), ("ln_in_g", (D_MODEL,)), ("ln_in_b", (D_MODEL,)), ("b_in", (DEPTH, N_COLS)),
          ("sinks", (DEPTH, 8)), ("vn_g", (DEPTH, SGU_W)), ("vn_b", (DEPTH, SGU_W)),
          ("b_s", (DEPTH, SGU_G, BLK)), ("b_out", (DEPTH, D_MODEL)), ("ln_g", (DEPTH, D_MODEL)), ("ln_b", (DEPTH, D_MODEL)))
_BIG = ("w_in", "p_a", "p_b", "w_out")


def _size(shape):
    n = 1
    for d in shape:
        n *= d
    return n


def _pack_small(vals, last_row=None):
    rows = []
    for name, shape in _SMALL:
        v = vals[name]
        if _size(shape) % 128:
            v = jnp.pad(v.reshape((1, -1)), ((0, 0), (0, (-_size(shape)) % 128)))
        rows.append(v.reshape((-1, 128)))
    used = sum(r.shape[0] for r in rows)
    tail = jnp.zeros((SMALL_ROWS - used, 128), F32)
    if last_row is not None:
        tail = tail.at[-1, 0].set(last_row)
    return jnp.concatenate([rows[0], jnp.concatenate(rows[1:] + [tail], axis=0)], axis=0)


def _unpack_small(packed):
    out, pos = {}, 0
    for name, shape in _SMALL:
        n = -(-_size(shape) // 128)
        rows = packed[pos:pos + n]
        out[name] = (rows.reshape((-1,))[:_size(shape)] if _size(shape) % 128 else rows).reshape(shape)
        pos += n
    return out


def kernel(x, ln_in_g, ln_in_b, w_in, b_in, sinks, vn_g, vn_b, w_s, b_s, p_a, p_b, w_out, b_out, ln_g, ln_b, loss_target, m_ln_in_g, m_ln_in_b, m_w_in, m_b_in, m_sinks, m_vn_g, m_vn_b, m_w_s, m_b_s, m_p_a, m_p_b, m_w_out, m_b_out, m_ln_g, m_ln_b, v_ln_in_g, v_ln_in_b, v_w_in, v_b_in, v_sinks, v_vn_g, v_vn_b, v_w_s, v_b_s, v_p_a, v_p_b, v_w_out, v_b_out, v_ln_g, v_ln_b):
    weights = dict(ln_in_g=ln_in_g, ln_in_b=ln_in_b, w_in=w_in, b_in=b_in, sinks=sinks, vn_g=vn_g, vn_b=vn_b, w_s=w_s,
                   b_s=b_s, p_a=p_a, p_b=p_b, w_out=w_out, b_out=b_out, ln_g=ln_g, ln_b=ln_b)
    mom1 = dict(ln_in_g=m_ln_in_g, ln_in_b=m_ln_in_b, w_in=m_w_in, b_in=m_b_in, sinks=m_sinks, vn_g=m_vn_g, vn_b=m_vn_b,
                w_s=m_w_s, b_s=m_b_s, p_a=m_p_a, p_b=m_p_b, w_out=m_w_out, b_out=m_b_out, ln_g=m_ln_g, ln_b=m_ln_b)
    mom2 = dict(ln_in_g=v_ln_in_g, ln_in_b=v_ln_in_b, w_in=v_w_in, b_in=v_b_in, sinks=v_sinks, vn_g=v_vn_g, vn_b=v_vn_b,
                w_s=v_w_s, b_s=v_b_s, p_a=v_p_a, p_b=v_p_b, w_out=v_w_out, b_out=v_b_out, ln_g=v_ln_g, ln_b=v_ln_b)
    n_seq, S, _ = x.shape
    T = n_seq * S
    c = lax.axis_index("c")
    j = 2 * lax.axis_index("x") + lax.axis_index("y")
    core = c.astype(jnp.int32).reshape((1,))
    chips = jnp.stack([j] + [r + (r >= j) for r in range(N_CHIPS - 1)]).astype(jnp.int32)
    names = list(_PIECES)
    xt, tt = x.reshape((T, D_MODEL)), loss_target.reshape((T, D_MODEL))

    tview = lambda d: dict(d, w_in=jnp.swapaxes(d["w_in"], 1, 2))
    weights_t, mom1_t, mom2_t = tview(weights), tview(mom1), tview(mom2)

    def own_shards(l):
        return [weights_t[n][l].astype(MXU_DTYPE) for n in _BIG]

    def layer_weights(l, gathered):
        d = dict(zip(_PIECES, gathered))
        d.update(paT=d["pa"].T, pbT=d["pb"].T, woutT=d["wout"].T)
        d.update(bin=b_in[l][None], sinks=sinks[l], vng=vn_g[l][None], vnb=vn_b[l][None], ws=w_s[l],
                 bs=jnp.repeat(b_s[l].T, 128, axis=1), bout=b_out[l][None], lng=ln_g[l][None], lnb=ln_b[l][None])
        return d

    own0, own1 = own_shards(0), own_shards(1)
    x0, g0 = input_norm_forward(xt, ln_in_g[None], ln_in_b[None], "ln_in_fwd", comm=gather_weights(names[:1], own0[:1]))
    n_proj = T // ROWS_PROJ
    h0, g1 = project(x0, g0[0], b_in[0][None], "layer0_proj",
                     comm=gather_weights(names[1:] + names[:1], own0[1:] + own1[:1], turn_step=(13 * n_proj) // 16))
    lw0 = layer_weights(0, g0 + g1[:len(names) - 1])
    x1, h1, g2 = layer_forward_and_projection(x0, h0, lw0, g1[-1], b_in[1][None], S, "layer0_fwd_layer1_proj",
                                              comm=gather_weights(names[1:], own1[1:], turn_step=T // ROWS // 2))
    lw1 = layer_weights(1, g1[-1:] + g2)

    dres1, dh1, acc1, _ = layer_backward(x1, h1, tt, lw1, S, True, "layer1_bwd")
    dx1, dwinT1, _ = input_grads(dres1, dh1, x1, lw1["winT"], "layer1_dx_dwin")
    parts1 = [dwinT1, acc1["dpa"], acc1["dpb"], acc1["dwout"]]
    dres0, dh0, acc0, landed1 = layer_backward(x0, h0, dx1, lw0, S, False, "layer0_bwd", comm=pair_send_halves(names, parts1))
    sums1 = add_own_halves(names, parts1, landed1, core, "add_own_halves1")
    parts0 = [None, acc0["dpa"], acc0["dpb"], acc0["dwout"]]
    dx0, parts0[0], riding = input_grads(dres0, dh0, x0, lw0["winT"], "layer0_dx_dwin",
                                         comm=_both(chip_exchange(sums1), pair_send_halves(names[1:], parts0[1:])))
    from_chips1, landed0_rest = riding[:len(names)], riding[len(names):]
    totals1 = sum_chips(sums1, from_chips1, chips, "sum_chips1")
    grad_x, d_ln_in_g, d_ln_in_b, _ = input_norm_backward(xt, dx0, ln_in_g, "ln_in_bwd")

    gl = [acc0, acc1]
    per_layer = lambda key, pick: jnp.stack([pick(gl[l][key]) for l in range(DEPTH)])
    row0 = lambda key: per_layer(key, lambda a: a[0])
    small_part = _pack_small(dict(
        ln_in_g=d_ln_in_g[0], ln_in_b=d_ln_in_b[0], b_in=row0("dbin"), sinks=per_layer("dsink", lambda a: a[:, 0]),
        vn_g=row0("dvng"), vn_b=row0("dvnb"), w_s=per_layer("dws", lambda a: a),
        b_s=per_layer("dbs", lambda a: a[:, ::128].T), b_out=row0("dbout"), ln_g=row0("dlng"), ln_b=row0("dlnb")),
        last_row=acc1["loss"][0, 0])
    first = _run_comm(pair_send_halves(names[:1], parts0[:1], small_part, extra=totals1), "pair_send_halves0")
    landed0, small_landed, sib_totals1 = first[:1] + landed0_rest, first[1], first[2:]
    sums0 = add_own_halves(names, parts0, landed0, core, "add_own_halves0")
    small_sum = add_small_half(small_part, small_landed, core)
    *from_chips0, small_from_chips = _run_comm(chip_exchange(sums0, small_sum), "chip_exchange0")
    *totals0, small_total = sum_chips(sums0, from_chips0, chips, "sum_chips0", small_landed=small_from_chips)
    *sib_totals0, sib_small = _run_comm(pair_send_totals(totals0 + [small_total]), "pair_send_totals0")

    def both_halves(mine, theirs):
        return jnp.where(c == 0, jnp.concatenate([mine, theirs], axis=0), jnp.concatenate([theirs, mine], axis=0))

    g_halves = [(totals0[pi], sib_totals0[pi], totals1[pi], sib_totals1[pi]) for pi in range(len(_BIG))]
    g_small = both_halves(small_total, sib_small)
    loss = g_small[-1, 0]

    big = adamw_big([weights_t[n] for n in _BIG], g_halves, [mom1_t[n] for n in _BIG], [mom2_t[n] for n in _BIG], core)
    small_names = [n for n, _ in _SMALL]
    as2d = lambda a: a.reshape((1, -1)) if a.ndim == 1 else a
    g_named = _unpack_small(g_small)
    small = adamw_small(*[[as2d(d[n]) for n in small_names] for d in (weights, g_named, mom1, mom2)])

    def named(big_outs, small_outs):
        d = dict(zip(_BIG, big_outs))
        d["w_in"] = jnp.swapaxes(d["w_in"], 1, 2)
        d.update({n: o.reshape(weights[n].shape) for n, o in zip(small_names, small_outs)})
        return d

    order = ("ln_in_g", "ln_in_b", "w_in", "b_in", "sinks", "vn_g", "vn_b", "w_s", "b_s", "p_a", "p_b", "w_out", "b_out",
             "ln_g", "ln_b")
    res = [loss, grad_x.reshape(x.shape)]
    groups = [named(big[0], [g_named[n] for n in small_names])] + [named(big[k + 1], small[k]) for k in range(3)]
    for group in groups:
        res.extend(group[n] for n in order)
    return tuple(res)
```

```python
import functools

import jax
import jax.numpy as jnp
from jax import lax
from jax.experimental import pallas as pl
from jax.experimental.pallas import tpu as pltpu

F32 = jnp.float32
MXU_DTYPE = jnp.bfloat16

D_MODEL = 1024
DEPTH = 2
HEAD_DIM = 64
ATTN_W = 512
KV_W = 128
BLK = 128
SGU_W = 512
SGU_G = 4
N_COLS = 4864
C_Q, C_K, C_V, C_GA, C_UB, C_VB, C_GB, C_RA, C_RB = 0, 512, 640, 768, 1280, 1792, 2304, 2816, 3840
ALPHA = (2.0 * DEPTH) ** 0.25
LN_EPS = 1e-5
SCALE = HEAD_DIM ** -0.5
NEG = float(jnp.finfo(jnp.float32).min)
GELU_C = 0.7978845608028654
GELU_A = 0.044715

ADAM_LR, ADAM_B1, ADAM_B2, ADAM_EPS, ADAM_WD, ADAM_STEP = 0.001, 0.9, 0.999, 1e-08, 0.01, 10

V7X_VMEM_BYTES = 64 * 1024 * 1024
V7X_VMEM_RESERVE = 3 * 1024 * 1024
ROWS = 256
ROWS_BWD = 256
ROWS_PROJ = 512

MESH = pl.DeviceIdType.MESH
N_CHIPS = 4


def _mm(a, b):
    return jnp.dot(a.astype(MXU_DTYPE), b.astype(MXU_DTYPE), preferred_element_type=F32)


def _mm_nt(a, b):
    return lax.dot_general(a.astype(MXU_DTYPE), b.astype(MXU_DTYPE), (((1,), (1,)), ((), ())),
                           preferred_element_type=F32)


def _mm_tn(a, b):
    return lax.dot_general(a.astype(MXU_DTYPE), b.astype(MXU_DTYPE), (((0,), (0,)), ((), ())),
                           preferred_element_type=F32)


def _sigmoid(x):
    return 0.5 * jnp.tanh(0.5 * x) + 0.5


def _gelu_parts(x):
    x2 = x * x
    u = 0.5 * jnp.tanh(x * (GELU_C + (GELU_C * GELU_A) * x2)) + 0.5
    dg = u * (1.0 + x * (1.0 - u) * (2.0 * GELU_C + (6.0 * GELU_C * GELU_A) * x2))
    return x * u, dg


def _ln_fwd(z, g, b):
    mu = jnp.mean(z, axis=-1, keepdims=True)
    zc = z - mu
    var = jnp.mean(zc * zc, axis=-1, keepdims=True)
    rstd = lax.rsqrt(var + LN_EPS)
    zhat = zc * rstd
    return zhat * g + b, zhat, rstd


def _ln_bwd(dy, zhat, rstd, g):
    dzh = dy * g
    m1 = jnp.mean(dzh, axis=-1, keepdims=True)
    m2 = jnp.mean(dzh * zhat, axis=-1, keepdims=True)
    return rstd * (dzh - m1 - zhat * m2)


def _colsum(v):
    return jnp.sum(v, axis=0, keepdims=True)


def _lane_lo(rows):
    return lax.broadcasted_iota(jnp.int32, (rows, 128), 1) < HEAD_DIM


def _dup_head(x2, hk, lo):
    xr = pltpu.roll(x2, HEAD_DIM, 1)
    return jnp.where(lo, x2, xr) if hk == 0 else jnp.where(lo, xr, x2)


def _fold_head(y0, y1, lo):
    f0 = y0 + pltpu.roll(y0, HEAD_DIM, 1)
    f1 = y1 + pltpu.roll(y1, HEAD_DIM, 1)
    return jnp.where(lo, f0, f1)


def _stack_heads(t0, t1, lo):
    z = jnp.zeros_like(t0)
    return jnp.concatenate([jnp.where(lo, t0, z), jnp.where(lo, z, t0),
                            jnp.where(lo, t1, z), jnp.where(lo, z, t1)], axis=0)


def _unstack_heads(o, lo):
    return (jnp.where(lo, o[0:128], o[128:256]), jnp.where(lo, o[256:384], o[384:512]))


def _band_bias(start):
    row = lax.broadcasted_iota(jnp.int32, (4 * BLK, 2 * BLK), 0) & (BLK - 1)
    kpos = lax.broadcasted_iota(jnp.int32, (4 * BLK, 2 * BLK), 1)
    valid = (kpos > row) & (kpos <= row + BLK)
    if start is not None:
        valid = valid & (jnp.logical_not(start) | (kpos >= BLK))
    return jnp.where(valid, 0.0, NEG)


def _attn_probs(s, sink4, bias):
    s = s * SCALE + bias
    r1 = lax.broadcasted_iota(jnp.int32, (4 * BLK, 1), 0)
    sk = jnp.where(r1 < BLK, sink4[0], jnp.where(r1 < 2 * BLK, sink4[1], jnp.where(r1 < 3 * BLK, sink4[2], sink4[3])))
    m = jnp.maximum(jnp.max(s, axis=-1, keepdims=True), sk)
    p = jnp.exp(s - m)
    es = jnp.exp(sk - m)
    inv = 1.0 / (jnp.sum(p, axis=-1, keepdims=True) + es)
    return p * inv, es * inv


_PROJ_CHUNK = 512


def _kv_blocks(kvh, h_s, nb):
    ks = [kvh[:, 0:KV_W]] + [h_s[n * BLK:(n + 1) * BLK, C_K:C_K + KV_W] for n in range(nb)]
    vs = [kvh[:, KV_W:2 * KV_W]] + [h_s[n * BLK:(n + 1) * BLK, C_V:C_V + KV_W] for n in range(nb)]
    return ks, vs


def _attn_operands(h_s, ks, vs, n, hk, lo, lo2):
    rows = slice(n * BLK, (n + 1) * BLK)
    kd = _dup_head(jnp.concatenate([ks[n], ks[n + 1]], axis=0), hk, lo2)
    vd = _dup_head(jnp.concatenate([vs[n], vs[n + 1]], axis=0), hk, lo2)
    c0 = C_Q + 2 * hk * 128
    qs = _stack_heads(h_s[rows, c0:c0 + 128], h_s[rows, c0 + 128:c0 + 256], lo)
    return qs, kd, vd


def _sgu_weights(w):
    tri = (lax.broadcasted_iota(jnp.int32, (BLK, BLK), 0) >= lax.broadcasted_iota(jnp.int32, (BLK, BLK), 1))
    return tri, [jnp.where(tri, w["ws"][g], 0.0) for g in range(SGU_G)]


def _layer_forward(x, kvh, start, w, h_s, attn_s, mix_s, keep, tick=lambda: None):
    R = x.shape[0]
    nb = R // BLK
    lo = _lane_lo(BLK)
    lo2 = _lane_lo(2 * BLK)
    ks, vs = _kv_blocks(kvh, h_s, nb)
    sinks = [w["sinks"][j] for j in range(8)]
    bias_first = _band_bias(start)
    bias_rest = _band_bias(None) if nb > 1 else None
    pairs = [(n, hk) for n in range(nb) for hk in range(2)]
    ops = [_attn_operands(h_s, ks, vs, n, hk, lo, lo2) for n, hk in pairs]
    scores = [_mm_nt(qs, kd) for qs, kd, _ in ops]
    probs_l = [_attn_probs(s, sinks[4 * hk:4 * hk + 4], bias_first if n == 0 else bias_rest)
               for s, (n, hk) in zip(scores, pairs)]
    outs = [_mm(p, vd) for (p, _), (_, _, vd) in zip(probs_l, ops)]
    for o, (n, hk) in zip(outs, pairs):
        rows = slice(n * BLK, (n + 1) * BLK)
        t0, t1 = _unstack_heads(o, lo)
        attn_s[rows, 2 * hk * 128:(2 * hk + 1) * 128] = t0
        attn_s[rows, (2 * hk + 1) * 128:(2 * hk + 2) * 128] = t1
        tick()
    ga = h_s[:, C_GA:C_GA + ATTN_W]
    ya = attn_s[...] * (ga * _sigmoid(ga))
    tick()
    gu, _ = _gelu_parts(h_s[:, C_UB:C_UB + SGU_W])
    tick()
    gv, _ = _gelu_parts(h_s[:, C_VB:C_VB + SGU_W])
    vn, _, _ = _ln_fwd(gv, w["vng"][...], w["vnb"][...])
    tick()
    _, wms = _sgu_weights(w)
    tiles = [(slice(n * BLK, (n + 1) * BLK), g, slice(g * 128, (g + 1) * 128)) for n in range(nb) for g in range(SGU_G)]
    mixes = [_mm(wms[g], vn[rows, cols]) for rows, g, cols in tiles]
    for m, (rows, g, cols) in zip(mixes, tiles):
        mix_s[rows, cols] = m + w["bs"][:, cols]
    gb = h_s[:, C_GB:C_GB + SGU_W]
    yb = gu * mix_s[...] * (gb * _sigmoid(gb))
    tick()
    a = _mm(ya, w["pa"][...])
    b = _mm(yb, w["pb"][...])
    tick()
    merged = _sigmoid(h_s[:, C_RA:C_RA + D_MODEL]) * a + _sigmoid(h_s[:, C_RB:C_RB + D_MODEL]) * b
    tick()
    out = _mm(merged, w["wout"][...]) + w["bout"][...]
    y, zhat, rstd = _ln_fwd(ALPHA * x + out, w["lng"][...], w["lnb"][...])
    if not keep:
        return y, None
    return y, dict(ks=ks, vs=vs, probs=probs_l, ya=ya, yb=yb, a=a, b=b, merged=merged, zhat=zhat, rstd=rstd)


def _dsilu(g, sg):
    return sg * (1.0 + g * (1.0 - sg))


_W_NAMES = ("sinks", "vng", "vnb", "ws", "bs", "pa", "pb", "wout", "bout", "lng", "lnb")
_WT_NAMES = ("paT", "pbT", "woutT")
_ACC_NAMES = ("dwout", "dpa", "dpb", "dbin", "dbout", "dlng", "dlnb", "dvng", "dvnb", "dws", "dbs", "dsink", "loss")


def _make_fwd_proj_body(R, S, nT):
    def body(x_ref, h_ref, kvh_ref, wn_ref, bn_ref, *rest):
        w = dict(zip(_W_NAMES, rest[:len(_W_NAMES)]))
        y_ref, hn_ref, attn_s, mix_s, yprev_s = rest[len(_W_NAMES):]

        @pl.when(pl.program_id(0) == 0)
        def _():
            yprev_s[...] = jnp.zeros(yprev_s.shape, yprev_s.dtype)

        t = jnp.minimum(pl.program_id(0), nT - 1)
        start = (t % (S // R)) == 0
        chunks = [(c0, min(c0 + _PROJ_CHUNK, N_COLS)) for c0 in range(0, N_COLS, _PROJ_CHUNK)]

        def tick():
            if chunks:
                c0, c1 = chunks.pop(0)
                hn_ref[:, c0:c1] = _mm_nt(yprev_s[...], wn_ref[c0:c1, :]) + bn_ref[:, c0:c1]

        y, _ = _layer_forward(x_ref[...], kvh_ref[...], start, w, h_ref, attn_s, mix_s, keep=False, tick=tick)
        while chunks:
            tick()
        y_ref[...] = y
        yprev_s[...] = y.astype(yprev_s.dtype)
    return body


def _make_bwd_body(R, S, nT, is_last):
    nb = R // BLK
    names = _W_NAMES + _WT_NAMES
    nw = len(names)

    def body(x_ref, h_s, kvh_ref, aux_ref, *rest):
        w = dict(zip(names, rest[:nw]))
        dres_ref, dh_ref = rest[nw:nw + 2]
        acc = dict(zip(_ACC_NAMES, rest[nw + 2:nw + 2 + len(_ACC_NAMES)]))
        attn_s, mix_s, ckv_s = rest[nw + 2 + len(_ACC_NAMES):]
        i = pl.program_id(0)
        start = ((nT - 1 - i) % (S // R)) == 0
        lo = _lane_lo(BLK)
        lo2 = _lane_lo(2 * BLK)

        @pl.when(i == 0)
        def _():
            for name in _ACC_NAMES:
                acc[name][...] = jnp.zeros(acc[name].shape, F32)
            ckv_s[...] = jnp.zeros(ckv_s.shape, F32)

        x = x_ref[...]
        y, sv = _layer_forward(x, kvh_ref[...], start, w, h_s, attn_s, mix_s, keep=True)
        if is_last:
            diff = y - aux_ref[...]
            part = 0.5 * jnp.sum(jnp.mean(diff * diff, axis=-1, keepdims=True), axis=0, keepdims=True)
            acc["loss"][...] += jnp.broadcast_to(part, acc["loss"].shape)
            dy = diff * (1.0 / D_MODEL)
        else:
            dy = aux_ref[...]

        def put(c0, val, rows=slice(None)):
            width = val.shape[1]
            dh_ref[rows, c0:c0 + width] = val.astype(dh_ref.dtype)
            acc["dbin"][:, c0:c0 + width] += _colsum(val)

        acc["dlng"][...] += _colsum(dy * sv["zhat"])
        acc["dlnb"][...] += _colsum(dy)
        dz = _ln_bwd(dy, sv["zhat"], sv["rstd"], w["lng"][...])
        dres_ref[...] = ALPHA * dz
        acc["dbout"][...] += _colsum(dz)
        acc["dwout"][...] += _mm_tn(sv["merged"], dz)
        dmerged = _mm(dz, w["woutT"][...])
        sa = _sigmoid(h_s[:, C_RA:C_RA + D_MODEL])
        da = dmerged * sa
        put(C_RA, da * sv["a"] * (1.0 - sa))
        sb = _sigmoid(h_s[:, C_RB:C_RB + D_MODEL])
        db = dmerged * sb
        put(C_RB, db * sv["b"] * (1.0 - sb))
        acc["dpa"][...] += _mm_tn(sv["ya"], da)
        acc["dpb"][...] += _mm_tn(sv["yb"], db)
        dya = _mm(da, w["paT"][...])
        dyb = _mm(db, w["pbT"][...])
        ga = h_s[:, C_GA:C_GA + ATTN_W]
        sga = _sigmoid(ga)
        put(C_GA, dya * attn_s[...] * _dsilu(ga, sga))
        attn_s[...] = dya * (ga * sga)
        ks, vs = sv["ks"], sv["vs"]
        pairs = [(n, hk) for n in range(nb) for hk in range(2)]
        ops = [_attn_operands(h_s, ks, vs, n, hk, lo, lo2) for n, hk in pairs]
        doss = [_stack_heads(attn_s[n * BLK:(n + 1) * BLK, 2 * hk * 128:(2 * hk + 1) * 128],
                             attn_s[n * BLK:(n + 1) * BLK, (2 * hk + 1) * 128:(2 * hk + 2) * 128], lo) for n, hk in pairs]
        dps = [_mm_nt(dos, vd) for dos, (_, _, vd) in zip(doss, ops)]
        gb = h_s[:, C_GB:C_GB + SGU_W]
        sgb = _sigmoid(gb)
        gu, dgu = _gelu_parts(h_s[:, C_UB:C_UB + SGU_W])
        mixed = mix_s[...]
        put(C_GB, dyb * (gu * mixed) * _dsilu(gb, sgb))
        dsgu = dyb * (gb * sgb)
        put(C_UB, dsgu * mixed * dgu)
        dmixed = dsgu * gu
        dsss = []
        for dp, (probs, ps), (n, hk) in zip(dps, sv["probs"], pairs):
            delta = jnp.sum(probs * dp, axis=-1, keepdims=True)
            dsk = -(ps * delta)
            for g in range(4):
                j = 4 * hk + g
                acc["dsink"][j:j + 1, :] += jnp.broadcast_to(_colsum(dsk[g * BLK:(g + 1) * BLK]), (1, 128))
            dsss.append(probs * (dp - delta) * SCALE)
        dqs = [_mm(dss, kd) for dss, (_, kd, _) in zip(dsss, ops)]
        ydks = [_mm_tn(dss, qs) for dss, (qs, _, _) in zip(dsss, ops)]
        ydvs = [_mm_tn(probs, dos) for (probs, _), dos in zip(sv["probs"], doss)]
        gv, dgv = _gelu_parts(h_s[:, C_VB:C_VB + SGU_W])
        vn, vhat, vrstd = _ln_fwd(gv, w["vng"][...], w["vnb"][...])
        tri, wms = _sgu_weights(w)
        tiles = [(slice(n * BLK, (n + 1) * BLK), g, slice(g * 128, (g + 1) * 128)) for n in range(nb) for g in range(SGU_G)]
        dw_parts = [_mm_nt(dmixed[rows, cols], vn[rows, cols]) for rows, g, cols in tiles]
        dvn_parts = [_mm_tn(wms[g], dmixed[rows, cols]) for rows, g, cols in tiles]
        for g in range(SGU_G):
            tot = dw_parts[g]
            for n in range(1, nb):
                tot = tot + dw_parts[n * SGU_G + g]
            acc["dws"][g] += jnp.where(tri, tot, 0.0)
        for part, (rows, g, cols) in zip(dvn_parts, tiles):
            mix_s[rows, cols] = part
        dbs_part = dmixed[0:BLK]
        for n in range(1, nb):
            dbs_part = dbs_part + dmixed[n * BLK:(n + 1) * BLK]
        acc["dbs"][...] += dbs_part
        for dq, (n, hk) in zip(dqs, pairs):
            q0, q1 = _unstack_heads(dq, lo)
            put(C_Q + 2 * hk * 128, q0, slice(n * BLK, (n + 1) * BLK))
            put(C_Q + (2 * hk + 1) * 128, q1, slice(n * BLK, (n + 1) * BLK))
        dks = [jnp.zeros((BLK, KV_W), F32) for _ in range(nb + 1)]
        dvs = [jnp.zeros((BLK, KV_W), F32) for _ in range(nb + 1)]
        for n in range(nb):
            dk2 = _fold_head(ydks[2 * n], ydks[2 * n + 1], lo2)
            dv2 = _fold_head(ydvs[2 * n], ydvs[2 * n + 1], lo2)
            dks[n] = dks[n] + dk2[0:BLK]
            dks[n + 1] = dks[n + 1] + dk2[BLK:2 * BLK]
            dvs[n] = dvs[n] + dv2[0:BLK]
            dvs[n + 1] = dvs[n + 1] + dv2[BLK:2 * BLK]
        dks[nb] = dks[nb] + ckv_s[:, 0:KV_W]
        dvs[nb] = dvs[nb] + ckv_s[:, KV_W:2 * KV_W]
        ckv_s[:, 0:KV_W] = dks[0]
        ckv_s[:, KV_W:2 * KV_W] = dvs[0]
        put(C_K, jnp.concatenate(dks[1:], axis=0))
        put(C_V, jnp.concatenate(dvs[1:], axis=0))
        dvn = mix_s[...]
        acc["dvng"][...] += _colsum(dvn * vhat)
        acc["dvnb"][...] += _colsum(dvn)
        put(C_VB, _ln_bwd(dvn, vhat, vrstd, w["vng"][...]) * dgv)

        @pl.when(i == nT - 1)
        def _():
            for g in range(SGU_G):
                cols = slice(g * 128, (g + 1) * 128)
                tot = jnp.sum(acc["dbs"][:, cols], axis=1, keepdims=True)
                acc["dbs"][:, cols] = jnp.broadcast_to(tot, (BLK, 128))

    return body


def _resident():
    return pl.BlockSpec(memory_space=pltpu.VMEM)


_BIG_RESIDENT = ("pa", "pb", "wout", "paT", "pbT", "woutT", "dwout", "dpa", "dpb")


def _const_block(shape):
    return pl.BlockSpec(tuple(shape), lambda *_: (0,) * len(shape))


def _operand_spec(name, shape):
    if name == "sinks":
        return pl.BlockSpec(memory_space=pltpu.SMEM)
    return _resident() if name in _BIG_RESIDENT else _const_block(shape)


def _weight_specs(names, wts):
    return [_operand_spec(n, wts[n].shape) for n in names]


def _nbytes(a):
    n = jnp.dtype(a.dtype).itemsize
    for d in a.shape:
        n *= d
    return n


def _layer_params(resident=()):
    scoped = V7X_VMEM_BYTES - V7X_VMEM_RESERVE - sum(_nbytes(a) for a in resident)
    return pltpu.CompilerParams(dimension_semantics=("arbitrary",), vmem_limit_bytes=scoped)


class _Comm:
    def __init__(self, ins, out_shapes, sems, start, finish, aliases=None, turn=None, turn_step=None):
        self.ins = [pltpu.with_memory_space_constraint(a, pltpu.HBM) for a in ins]
        self.out_shapes = [pltpu.HBM(s.shape, s.dtype) for s in out_shapes]
        self.sems = list(sems)
        self.start, self.finish, self.aliases = start, finish, dict(aliases or {})
        self.turn, self.turn_step = turn, turn_step


_ANY = pl.BlockSpec(memory_space=pltpu.HBM)
_DMA = pltpu.SemaphoreType.DMA


def _call(body, *, name, grid, in_specs, out_specs, out_shape, scratch_shapes, compiler_params, operands, comm=None):
    if comm is None:
        res = pl.pallas_call(body, name=name, grid=grid, in_specs=in_specs, out_specs=out_specs, out_shape=out_shape,
                             scratch_shapes=scratch_shapes, compiler_params=compiler_params)(*operands)
        return list(res), []
    n_in, n_out, n_scr = len(in_specs), len(out_specs), len(scratch_shapes)
    ci, co = len(comm.ins), len(comm.out_shapes)
    last = grid[0] - 1

    def hosted(*refs):
        ins, cin = refs[:n_in], refs[n_in:n_in + ci]
        p = n_in + ci
        outs, cout = refs[p:p + n_out], refs[p + n_out:p + n_out + co]
        p += n_out + co
        scr, sems = refs[p:p + n_scr], refs[p + n_scr:]
        pl.when(pl.program_id(0) == 0)(lambda: comm.start(cin, cout, sems))
        body(*ins, *outs, *scr)
        if comm.turn is not None:
            at = last if comm.turn_step is None else comm.turn_step
            pl.when(pl.program_id(0) == at)(lambda: comm.turn(cin, cout, sems))
        pl.when(pl.program_id(0) == last)(lambda: comm.finish(cin, cout, sems))

    res = pl.pallas_call(
        hosted, name=name, grid=grid, in_specs=list(in_specs) + [_ANY] * ci, out_specs=list(out_specs) + [_ANY] * co,
        out_shape=list(out_shape) + comm.out_shapes, scratch_shapes=list(scratch_shapes) + comm.sems,
        input_output_aliases={n_in + a: n_out + b for a, b in comm.aliases.items()},
        compiler_params=compiler_params)(*operands, *comm.ins)
    return list(res[:n_out]), list(res[n_out:])


def _both(a, b):
    ni, no, ns = len(a.ins), len(a.out_shapes), len(a.sems)

    def each(f, g):
        def run(cin, cout, sems):
            f(cin[:ni], cout[:no], sems[:ns])
            g(cin[ni:], cout[no:], sems[ns:])
        return run

    both = _Comm([], [], a.sems + b.sems, each(a.start, b.start), each(a.finish, b.finish))
    both.ins, both.out_shapes = a.ins + b.ins, a.out_shapes + b.out_shapes
    return both


def _run_comm(comm, name):
    ci, co = len(comm.ins), len(comm.out_shapes)

    def body(*refs):
        cin, cout, sems = refs[:ci], refs[ci:ci + co], refs[ci + co:]
        comm.start(cin, cout, sems)
        if comm.turn is not None:
            comm.turn(cin, cout, sems)
        comm.finish(cin, cout, sems)

    return list(pl.pallas_call(body, name=name, in_specs=[_ANY] * ci, out_specs=[_ANY] * co, out_shape=comm.out_shapes,
                               scratch_shapes=comm.sems, input_output_aliases=comm.aliases)(*comm.ins))


def project(x, winT, b, name, comm=None):
    T = x.shape[0]
    R = ROWS_PROJ

    def body(x_ref, w_ref, b_ref, h_ref):
        xb = x_ref[...].astype(MXU_DTYPE)
        for c0 in range(0, N_COLS, _PROJ_CHUNK):
            c1 = min(c0 + _PROJ_CHUNK, N_COLS)
            h_ref[:, c0:c1] = _mm_nt(xb, w_ref[c0:c1, :]) + b_ref[:, c0:c1]

    (h,), extra = _call(
        body, name=name, grid=(T // R,),
        in_specs=[pl.BlockSpec((R, D_MODEL), lambda t: (t, 0)), _resident(), _const_block(b.shape)],
        out_specs=[pl.BlockSpec((R, N_COLS), lambda t: (t, 0))],
        out_shape=[jax.ShapeDtypeStruct((T, N_COLS), F32)], scratch_shapes=[],
        compiler_params=_layer_params([winT]), operands=(x, winT, b), comm=comm)
    return h, extra


_KV_BLOCK = C_K // (2 * KV_W)


def layer_forward_and_projection(x, h, wts, winT_next, b_next, S, name, comm=None):
    T = x.shape[0]
    R = ROWS
    nT = T // R
    npt = R // BLK
    cur = lambda t: jnp.minimum(t, nT - 1)
    (y, hn), extra = _call(
        _make_fwd_proj_body(R, S, nT), name=name, grid=(nT + 1,),
        in_specs=[pl.BlockSpec((R, D_MODEL), lambda t: (cur(t), 0)), pl.BlockSpec((R, N_COLS), lambda t: (cur(t), 0)),
                  pl.BlockSpec((BLK, 2 * KV_W), lambda t: (jnp.maximum(cur(t) * npt - 1, 0), _KV_BLOCK)),
                  _resident(), _const_block(b_next.shape)] + _weight_specs(_W_NAMES, wts),
        out_specs=[pl.BlockSpec((R, D_MODEL), lambda t: (cur(t), 0)),
                   pl.BlockSpec((R, N_COLS), lambda t: (jnp.maximum(t - 1, 0), 0))],
        out_shape=[jax.ShapeDtypeStruct((T, D_MODEL), F32), jax.ShapeDtypeStruct((T, N_COLS), F32)],
        scratch_shapes=[pltpu.VMEM((R, ATTN_W), F32), pltpu.VMEM((R, SGU_W), F32), pltpu.VMEM((R, D_MODEL), MXU_DTYPE)],
        compiler_params=_layer_params([winT_next] + [wts[n] for n in _W_NAMES if n in _BIG_RESIDENT]),
        operands=(x, h, h, winT_next, b_next, *[wts[n] for n in _W_NAMES]), comm=comm)
    return y, hn, extra


_ACC_SHAPES = dict(dwout=(D_MODEL, D_MODEL), dpa=(ATTN_W, D_MODEL), dpb=(SGU_W, D_MODEL), dbin=(1, N_COLS),
                   dbout=(1, D_MODEL), dlng=(1, D_MODEL), dlnb=(1, D_MODEL), dvng=(1, SGU_W), dvnb=(1, SGU_W),
                   dws=(SGU_G, BLK, BLK), dbs=(BLK, SGU_W), dsink=(8, 128), loss=(8, 128))


def layer_backward(x, h, aux, wts, S, is_last, name, comm=None):
    T = x.shape[0]
    R = ROWS_BWD
    nT = T // R
    npt = R // BLK
    rev = lambda i: (nT - 1 - i, 0)
    halo = lambda i: (jnp.maximum((nT - 1 - i) * npt - 1, 0), _KV_BLOCK)
    names = _W_NAMES + _WT_NAMES
    out_shape = ([jax.ShapeDtypeStruct((T, D_MODEL), F32), jax.ShapeDtypeStruct((T, N_COLS), MXU_DTYPE)]
                 + [jax.ShapeDtypeStruct(_ACC_SHAPES[n], F32) for n in _ACC_NAMES])
    outs, extra = _call(
        _make_bwd_body(R, S, nT, is_last), name=name, grid=(nT,),
        in_specs=[pl.BlockSpec((R, D_MODEL), rev), pl.BlockSpec((R, N_COLS), rev), pl.BlockSpec((BLK, 2 * KV_W), halo),
                  pl.BlockSpec((R, D_MODEL), rev)] + _weight_specs(names, wts),
        out_specs=[pl.BlockSpec((R, D_MODEL), rev), pl.BlockSpec((R, N_COLS), rev)]
        + [_operand_spec(n, _ACC_SHAPES[n]) for n in _ACC_NAMES],
        out_shape=out_shape,
        scratch_shapes=[pltpu.VMEM((R, ATTN_W), F32), pltpu.VMEM((R, SGU_W), F32), pltpu.VMEM((BLK, 2 * KV_W), F32)],
        compiler_params=_layer_params([wts[n] for n in names if n in _BIG_RESIDENT]
                                      + [jax.ShapeDtypeStruct(_ACC_SHAPES[n], F32) for n in _ACC_NAMES if n in _BIG_RESIDENT]),
        operands=(x, h, h, aux, *[wts[n] for n in names]), comm=comm)
    return outs[0], outs[1], dict(zip(_ACC_NAMES, outs[2:])), extra


def input_grads(dres, dh, x, winT, name, comm=None):
    T = x.shape[0]
    R = ROWS

    def body(dres_ref, dh_ref, x_ref, winT_ref, dx_ref, dwin_ref):
        @pl.when(pl.program_id(0) == 0)
        def _():
            dwin_ref[...] = jnp.zeros(dwin_ref.shape, F32)

        dh = dh_ref[...]
        dx_ref[...] = dres_ref[...] + jnp.dot(dh, winT_ref[...], preferred_element_type=F32)
        dwin_ref[...] += _mm_tn(dh, x_ref[...])

    row = lambda t: (t, 0)
    dwin_shape = jax.ShapeDtypeStruct((N_COLS, D_MODEL), F32)
    (dx, dwinT), extra = _call(
        body, name=name, grid=(T // R,),
        in_specs=[pl.BlockSpec((R, D_MODEL), row), pl.BlockSpec((R, N_COLS), row), pl.BlockSpec((R, D_MODEL), row),
                  _resident()],
        out_specs=[pl.BlockSpec((R, D_MODEL), row), _resident()],
        out_shape=[jax.ShapeDtypeStruct((T, D_MODEL), F32), dwin_shape], scratch_shapes=[],
        compiler_params=_layer_params([winT, dwin_shape]), operands=(dres, dh, x, winT), comm=comm)
    return dx, dwinT, extra


def input_norm_forward(x, g, b, name, comm=None):
    T = x.shape[0]
    R = ROWS_PROJ

    def body(x_ref, g_ref, b_ref, y_ref):
        y_ref[...] = _ln_fwd(x_ref[...], g_ref[...], b_ref[...])[0]

    row = lambda t: (t, 0)
    (y,), extra = _call(
        body, name=name, grid=(T // R,),
        in_specs=[pl.BlockSpec((R, D_MODEL), row), _const_block(g.shape), _const_block(b.shape)],
        out_specs=[pl.BlockSpec((R, D_MODEL), row)],
        out_shape=[jax.ShapeDtypeStruct((T, D_MODEL), F32)], scratch_shapes=[],
        compiler_params=_layer_params(), operands=(x, g, b), comm=comm)
    return y, extra


def input_norm_backward(x, dy, g, name, comm=None):
    T = x.shape[0]
    R = ROWS_PROJ

    def body(x_ref, dy_ref, g_ref, dx_ref, dg_ref, db_ref):
        @pl.when(pl.program_id(0) == 0)
        def _():
            dg_ref[...] = jnp.zeros(dg_ref.shape, F32)
            db_ref[...] = jnp.zeros(db_ref.shape, F32)

        dy = dy_ref[...]
        _, xhat, rstd = _ln_fwd(x_ref[...], g_ref[...], g_ref[...])
        dx_ref[...] = _ln_bwd(dy, xhat, rstd, g_ref[...])
        dg_ref[...] += _colsum(dy * xhat)
        db_ref[...] += _colsum(dy)

    row = lambda t: (t, 0)
    vec = jax.ShapeDtypeStruct((1, D_MODEL), F32)
    (dx, dg, db), extra = _call(
        body, name=name, grid=(T // R,),
        in_specs=[pl.BlockSpec((R, D_MODEL), row), pl.BlockSpec((R, D_MODEL), row), _const_block(g.shape)],
        out_specs=[pl.BlockSpec((R, D_MODEL), row), _const_block(vec.shape), _const_block(vec.shape)],
        out_shape=[jax.ShapeDtypeStruct((T, D_MODEL), F32), vec, vec], scratch_shapes=[],
        compiler_params=_layer_params(), operands=(x, dy, g), comm=comm)
    return dx, dg, db, extra


_PIECES = ("winT", "pa", "pb", "wout")
_WHOLE = dict(winT=(N_COLS, D_MODEL), pa=(ATTN_W, D_MODEL), pb=(SGU_W, D_MODEL), wout=(D_MODEL, D_MODEL))
_HALF = dict(winT=(N_COLS // 8, D_MODEL), pa=(ATTN_W // 2, D_MODEL // 4), pb=(SGU_W // 2, D_MODEL // 4),
             wout=(D_MODEL // 8, D_MODEL))
SMALL_ROWS = 1280


def _region(name, ref, k, h):
    hr, hc = _HALF[name]
    if name in ("pa", "pb"):
        return ref.at[pl.ds(h * hr, hr), pl.ds(k * hc, hc)]
    return ref.at[pl.ds(k * 2 * hr + h * hr, hr), :]


def _place():
    x, y, c = lax.axis_index("x"), lax.axis_index("y"), lax.axis_index("c")
    return x, y, c, 2 * x + y


def _as_chip(j, fn):
    for jj in range(N_CHIPS):
        pl.when(j == jj)(functools.partial(fn, jj))


def _remote(src, dst, send_sem, recv_sem, to):
    return pltpu.make_async_remote_copy(src_ref=src, dst_ref=dst, send_sem=send_sem, recv_sem=recv_sem,
                                        device_id=to, device_id_type=MESH)


def _core_of(k, c):
    return (k // 2, k % 2, c)


def _others(jj):
    return [k for k in range(N_CHIPS) if k != jj]


def _start_all(cps):
    for cp in cps:
        cp.start()


def _wait_all(cps):
    for cp in cps:
        cp.wait()


def gather_over_chips(names, shards, on_landed=None):
    n = len(names)

    def own(cin, cout, sems, jj):
        x, y, c, _ = _place()
        cps = []
        for i, name in enumerate(names):
            hr, hc = _HALF[name]
            place = (cout[i].at[:, pl.ds(jj * hc, hc)] if name in ("pa", "pb")
                     else cout[i].at[pl.ds(jj * 2 * hr, 2 * hr), :])
            cps.append(_remote(cin[i], place, sems[2].at[i], sems[3].at[i], (x, y, 1 - c)))
        return cps

    def copies(cin, cout, sems, jj, c):
        out = []
        for i, name in enumerate(names):
            hr = _HALF[name][0]
            for k in _others(jj):
                out.append(_remote(cin[i].at[pl.ds(c * hr, hr)], _region(name, cout[i], jj, c),
                                   sems[0].at[4 * i + k], sems[1].at[4 * i + jj], _core_of(k, c)))
        return out

    def start(cin, cout, sems):
        _, _, c, j = _place()
        _as_chip(j, lambda jj: _start_all(copies(cin, cout, sems, jj, c) + own(cin, cout, sems, jj)))

    def finish(cin, cout, sems):
        _, _, c, j = _place()

        def run(jj):
            for i, name in enumerate(names):
                for k in _others(jj):
                    land = _region(name, cout[i], k, c)
                    _remote(land, land, sems[0].at[4 * i + k], sems[1].at[4 * i + k], _core_of(k, c)).wait_recv()
                    if on_landed is not None:
                        on_landed(land, 4 * i + k, sems)
            for cp in copies(cin, cout, sems, jj, c):
                cp.wait_send()
            _wait_all(own(cin, cout, sems, jj))

        _as_chip(j, run)

    shapes = [jax.ShapeDtypeStruct(_WHOLE[nm], s.dtype) for nm, s in zip(names, shards)]
    return _Comm(shards, shapes, [_DMA((4 * n,)), _DMA((4 * n,)), _DMA((n,)), _DMA((n,))], start, finish)


def gather_weights(names, shards, turn_step=None):
    n = len(names)

    def pass_on(land, slot, sems):
        x, y, c, _ = _place()
        _remote(land, land, sems[4].at[slot], sems[5].at[slot], (x, y, 1 - c)).start()

    chips = gather_over_chips(names, shards, on_landed=pass_on)
    pair_finish = _pair_finish(names)
    return _Comm(shards, chips.out_shapes, chips.sems + [_DMA((4 * n,)), _DMA((4 * n,))], chips.start,
                 lambda cin, cout, sems: pair_finish(cin, cout, sems[4:]), turn=chips.finish, turn_step=turn_step)


def _pair_finish(names):
    def finish(cin, cout, sems):
        x, y, c, j = _place()

        def run(jj):
            for i, name in enumerate(names):
                for k in _others(jj):
                    theirs = _region(name, cout[i], k, 1 - c)
                    _remote(theirs, theirs, sems[0].at[4 * i + k], sems[1].at[4 * i + k], (x, y, 1 - c)).wait_recv()
            for i, name in enumerate(names):
                for k in _others(jj):
                    land = _region(name, cout[i], k, c)
                    _remote(land, land, sems[0].at[4 * i + k], sems[1].at[4 * i + k], (x, y, 1 - c)).wait_send()

        _as_chip(j, run)

    return finish


def pair_send_halves(names, parts, small=None, extra=()):
    n = len(names)
    hs = SMALL_ROWS // 2
    first_extra = n + (small is not None)

    def copies(cin, cout, sems):
        x, y, c, _ = _place()
        sib = (x, y, 1 - c)
        cps = []
        for i, name in enumerate(names):
            for k in range(N_CHIPS):
                cps.append(_remote(_region(name, cin[i], k, 1 - c), cout[i].at[k], sems[0].at[4 * i + k],
                                   sems[1].at[4 * i + k], sib))
        if small is not None:
            cps.append(_remote(cin[n].at[pl.ds((1 - c) * hs, hs)], cout[n], sems[0].at[4 * n], sems[1].at[4 * n], sib))
        for e in range(len(extra)):
            cps.append(_remote(cin[first_extra + e], cout[first_extra + e], sems[0].at[4 * n + 1 + e],
                               sems[1].at[4 * n + 1 + e], sib))
        return cps

    start = lambda cin, cout, sems: _start_all(copies(cin, cout, sems))
    finish = lambda cin, cout, sems: _wait_all(copies(cin, cout, sems))
    shapes = [jax.ShapeDtypeStruct((N_CHIPS,) + _HALF[nm], F32) for nm in names]
    ins = list(parts)
    if small is not None:
        shapes.append(jax.ShapeDtypeStruct((hs, 128), F32))
        ins.append(small)
    shapes += [jax.ShapeDtypeStruct(a.shape, a.dtype) for a in extra]
    ins += list(extra)
    n_sem = 4 * n + 1 + len(extra)
    return _Comm(ins, shapes, [_DMA((n_sem,)), _DMA((n_sem,))], start, finish)


def chip_exchange(sums, small=None):
    n = len(sums)

    def copies(cin, cout, sems, jj, c):
        out = []
        for k in _others(jj):
            for i in range(n):
                out.append(_remote(cin[i].at[k], cout[i].at[jj], sems[0].at[4 * i + k], sems[1].at[4 * i + jj], _core_of(k, c)))
            if small is not None:
                out.append(_remote(cin[n], cout[n].at[jj], sems[0].at[4 * n + k], sems[1].at[4 * n + jj], _core_of(k, c)))
        return out

    def start(cin, cout, sems):
        _, _, c, j = _place()

        def run(jj):
            if small is not None:
                pltpu.make_async_copy(cin[n], cout[n].at[jj], sems[2]).start()
            for cp in copies(cin, cout, sems, jj, c):
                cp.start()

        _as_chip(j, run)

    def finish(cin, cout, sems):
        _, _, c, j = _place()

        def run(jj):
            for k in _others(jj):
                for i in range(n):
                    _remote(cin[i].at[k], cout[i].at[k], sems[0].at[4 * i + k], sems[1].at[4 * i + k], _core_of(k, c)).wait_recv()
                if small is not None:
                    _remote(cin[n], cout[n].at[k], sems[0].at[4 * n + k], sems[1].at[4 * n + k], _core_of(k, c)).wait_recv()
            for cp in copies(cin, cout, sems, jj, c):
                cp.wait_send()
            if small is not None:
                pltpu.make_async_copy(cin[n], cout[n].at[jj], sems[2]).wait()

        _as_chip(j, run)

    shapes = [jax.ShapeDtypeStruct(a.shape, a.dtype) for a in sums]
    ins = list(sums)
    if small is not None:
        shapes.append(jax.ShapeDtypeStruct((N_CHIPS,) + small.shape, small.dtype))
        ins.append(small)
    return _Comm(ins, shapes, [_DMA((4 * n + 4,)), _DMA((4 * n + 4,)), _DMA(())], start, finish)


def pair_send_totals(arrs):
    n = len(arrs)

    def copies(cin, cout, sems):
        x, y, c, _ = _place()
        return [_remote(cin[i], cout[i], sems[0].at[i], sems[1].at[i], (x, y, 1 - c)) for i in range(n)]

    start = lambda cin, cout, sems: _start_all(copies(cin, cout, sems))
    finish = lambda cin, cout, sems: _wait_all(copies(cin, cout, sems))
    return _Comm(arrs, [jax.ShapeDtypeStruct(a.shape, a.dtype) for a in arrs], [_DMA((n,)), _DMA((n,))], start, finish)


def _pin(a):
    return pltpu.with_memory_space_constraint(a, pltpu.HBM)


def _plain_params(n_axes):
    return pltpu.CompilerParams(dimension_semantics=("arbitrary",) * n_axes, vmem_limit_bytes=V7X_VMEM_BYTES // 2)


def add_own_halves(names, parts, landed, core, name):
    n = len(names)

    def body(core_ref, *refs):
        for i in range(n):
            refs[2 * n + i][0] = (refs[i][...] + refs[n + i][0]).astype(MXU_DTYPE)

    in_specs, out_specs = [], []
    for nm in names:
        hr, hc = _HALF[nm]
        if nm in ("pa", "pb"):
            in_specs.append(pl.BlockSpec((hr // 2, hc), lambda k, s, cr: (cr[0] * 2 + s, k)))
        else:
            in_specs.append(pl.BlockSpec((hr // 2, hc), lambda k, s, cr: (k * 4 + cr[0] * 2 + s, 0)))
    for nm in names:
        hr, hc = _HALF[nm]
        in_specs.append(pl.BlockSpec((1, hr // 2, hc), lambda k, s, cr: (k, s, 0)))
        out_specs.append(pl.BlockSpec((1, hr // 2, hc), lambda k, s, cr: (k, s, 0)))
    return pl.pallas_call(
        body, name=name,
        grid_spec=pltpu.PrefetchScalarGridSpec(num_scalar_prefetch=1, grid=(N_CHIPS, 2), in_specs=in_specs, out_specs=out_specs),
        out_shape=[pltpu.HBM((N_CHIPS,) + _HALF[nm], MXU_DTYPE) for nm in names],
        compiler_params=_plain_params(2),
    )(core, *[_pin(a) for a in parts], *[_pin(a) for a in landed])


def add_small_half(small, landed, core):
    hs = SMALL_ROWS // 2

    def body(core_ref, a_ref, b_ref, o_ref):
        o_ref[...] = a_ref[...] + b_ref[...]

    return pl.pallas_call(
        body, name="add_small_half",
        grid_spec=pltpu.PrefetchScalarGridSpec(
            num_scalar_prefetch=1, grid=(1,),
            in_specs=[pl.BlockSpec((hs, 128), lambda s, cr: (cr[0], 0)), pl.BlockSpec((hs, 128), lambda s, cr: (0, 0))],
            out_specs=pl.BlockSpec((hs, 128), lambda s, cr: (0, 0))),
        out_shape=jax.ShapeDtypeStruct((hs, 128), F32),
    )(core, small, landed)


def sum_chips(sums, landed, chips, name, small_landed=None):
    n = len(sums)

    def body(chips_ref, *refs):
        ins, outs = refs[:4 * n + (small_landed is not None)], refs[4 * n + (small_landed is not None):]
        for i in range(n):
            tot = ins[4 * i][0].astype(F32)
            for r in range(1, N_CHIPS):
                tot = tot + ins[4 * i + r][0].astype(F32)
            outs[i][...] = tot
        if small_landed is not None:
            @pl.when(pl.program_id(0) == 0)
            def _():
                sm = ins[4 * n]
                outs[n][...] = ((sm[0] + sm[1]) + sm[2]) + sm[3]

    in_specs, out_specs, operands, shapes = [], [], [], []
    for a, l in zip(sums, landed):
        _, hr, hc = a.shape
        for r in range(N_CHIPS):
            in_specs.append(pl.BlockSpec((1, hr // 2, hc), functools.partial(lambda s, ch, r: (ch[r], s, 0), r=r)))
            operands.append(a if r == 0 else l)
        out_specs.append(pl.BlockSpec((hr // 2, hc), lambda s, ch: (s, 0)))
        shapes.append(jax.ShapeDtypeStruct((hr, hc), F32))
    if small_landed is not None:
        in_specs.append(pl.BlockSpec(small_landed.shape, lambda s, ch: (0, 0, 0)))
        out_specs.append(pl.BlockSpec(small_landed.shape[1:], lambda s, ch: (0, 0)))
        operands.append(small_landed)
        shapes.append(jax.ShapeDtypeStruct(small_landed.shape[1:], F32))
    return pl.pallas_call(
        body, name=name,
        grid_spec=pltpu.PrefetchScalarGridSpec(num_scalar_prefetch=1, grid=(2,), in_specs=in_specs, out_specs=out_specs),
        out_shape=[pltpu.HBM(s.shape, s.dtype) for s in shapes], compiler_params=_plain_params(1),
    )(chips, *[_pin(a) for a in operands])


N_STEPS = 8


def _adam_update(w, g, m, v):
    m = ADAM_B1 * m + (1.0 - ADAM_B1) * g
    v = ADAM_B2 * v + (1.0 - ADAM_B2) * (g * g)
    m_hat = m * (1.0 / (1.0 - ADAM_B1 ** ADAM_STEP))
    v_hat = v * (1.0 / (1.0 - ADAM_B2 ** ADAM_STEP))
    return -ADAM_LR * (m_hat / (jnp.sqrt(v_hat) + ADAM_EPS) + ADAM_WD * w), m, v


def adamw_big(ws, g_halves, ms, vs, core):
    n = len(ws)
    per_layer = N_STEPS // DEPTH
    per_half = per_layer // 2

    def body(core_ref, *refs):
        s = pl.program_id(0)
        first_layer = s < per_layer
        mine = ((s % per_layer) // per_half) == core_ref[0]
        for i in range(n):
            w, m, v = (refs[k * n + i][...] for k in range(3))
            g00, g01, g10, g11 = (refs[3 * n + 4 * i + k][...] for k in range(4))
            g = jnp.where(first_layer, jnp.where(mine, g00, g01), jnp.where(mine, g10, g11))[None]
            delta, m, v = _adam_update(w, g, m, v)
            for k, val in enumerate((g, delta, m, v)):
                refs[7 * n + k * n + i][...] = val

    whole = lambda a: pl.BlockSpec((1, a.shape[1] // per_layer, a.shape[2]),
                                   lambda s, cr: (s // per_layer, s % per_layer, 0))

    def half(a, layer, own):
        def index(s, cr):
            first = layer * per_layer + per_half * jnp.where(own, cr[0], 1 - cr[0])
            return (jnp.clip(s - first, 0, per_half - 1), 0)
        return pl.BlockSpec((a.shape[1] // per_layer, a.shape[2]), index)

    g_specs, g_ops = [], []
    for a, halves in zip(ws, g_halves):
        g_specs += [half(a, 0, True), half(a, 0, False), half(a, 1, True), half(a, 1, False)]
        g_ops += [_pin(a) for a in halves]
    outs = pl.pallas_call(
        body, name="adamw_big",
        grid_spec=pltpu.PrefetchScalarGridSpec(
            num_scalar_prefetch=1, grid=(N_STEPS,), in_specs=[whole(a) for a in ws] * 3 + g_specs,
            out_specs=[whole(a) for a in ws] * 4),
        out_shape=[pltpu.HBM(a.shape, F32) for a in ws] * 4,
        compiler_params=_layer_params(),
    )(core, *[_pin(a) for a in (*ws, *ms, *vs)], *g_ops)
    return outs[:n], outs[n:2 * n], outs[2 * n:3 * n], outs[3 * n:]


def adamw_small(ws, gs, ms, vs):
    n = len(ws)

    def body(*refs):
        for i in range(n):
            outs = _adam_update(*(refs[k * n + i][...] for k in range(4)))
            for k, val in enumerate(outs):
                refs[4 * n + k * n + i][...] = val

    specs = [_const_block(a.shape) for a in ws]
    outs = pl.pallas_call(
        body, name="adamw_small", grid=(1,), in_specs=specs * 4, out_specs=specs * 3,
        out_shape=[jax.ShapeDtypeStruct(a.shape, F32) for a in ws] * 3,
        compiler_params=_plain_params(1),
    )(*ws, *gs, *ms, *vs)
    return outs[:n], outs[n:2 * n], outs[2 * n:]


_SMALL = (("w_s", (DEPTH, SGU_G, BLK, BLK)), ("ln_in_g", (D_MODEL,)), ("ln_in_b", (D_MODEL,)), ("b_in", (DEPTH, N_COLS)),
          ("sinks", (DEPTH, 8)), ("vn_g", (DEPTH, SGU_W)), ("vn_b", (DEPTH, SGU_W)),
          ("b_s", (DEPTH, SGU_G, BLK)), ("b_out", (DEPTH, D_MODEL)), ("ln_g", (DEPTH, D_MODEL)), ("ln_b", (DEPTH, D_MODEL)))
_BIG = ("w_in", "p_a", "p_b", "w_out")


def _size(shape):
    n = 1
    for d in shape:
        n *= d
    return n


def _pack_small(vals, last_row=None):
    rows = []
    for name, shape in _SMALL:
        v = vals[name]
        if _size(shape) % 128:
            v = jnp.pad(v.reshape((1, -1)), ((0, 0), (0, (-_size(shape)) % 128)))
        rows.append(v.reshape((-1, 128)))
    used = sum(r.shape[0] for r in rows)
    tail = jnp.zeros((SMALL_ROWS - used, 128), F32)
    if last_row is not None:
        tail = tail.at[-1, 0].set(last_row)
    return jnp.concatenate([rows[0], jnp.concatenate(rows[1:] + [tail], axis=0)], axis=0)


def _unpack_small(packed):
    out, pos = {}, 0
    for name, shape in _SMALL:
        n = -(-_size(shape) // 128)
        rows = packed[pos:pos + n]
        out[name] = (rows.reshape((-1,))[:_size(shape)] if _size(shape) % 128 else rows).reshape(shape)
        pos += n
    return out


def kernel(x, ln_in_g, ln_in_b, w_in, b_in, sinks, vn_g, vn_b, w_s, b_s, p_a, p_b, w_out, b_out, ln_g, ln_b, loss_target, m_ln_in_g, m_ln_in_b, m_w_in, m_b_in, m_sinks, m_vn_g, m_vn_b, m_w_s, m_b_s, m_p_a, m_p_b, m_w_out, m_b_out, m_ln_g, m_ln_b, v_ln_in_g, v_ln_in_b, v_w_in, v_b_in, v_sinks, v_vn_g, v_vn_b, v_w_s, v_b_s, v_p_a, v_p_b, v_w_out, v_b_out, v_ln_g, v_ln_b):
    weights = dict(ln_in_g=ln_in_g, ln_in_b=ln_in_b, w_in=w_in, b_in=b_in, sinks=sinks, vn_g=vn_g, vn_b=vn_b, w_s=w_s,
                   b_s=b_s, p_a=p_a, p_b=p_b, w_out=w_out, b_out=b_out, ln_g=ln_g, ln_b=ln_b)
    mom1 = dict(ln_in_g=m_ln_in_g, ln_in_b=m_ln_in_b, w_in=m_w_in, b_in=m_b_in, sinks=m_sinks, vn_g=m_vn_g, vn_b=m_vn_b,
                w_s=m_w_s, b_s=m_b_s, p_a=m_p_a, p_b=m_p_b, w_out=m_w_out, b_out=m_b_out, ln_g=m_ln_g, ln_b=m_ln_b)
    mom2 = dict(ln_in_g=v_ln_in_g, ln_in_b=v_ln_in_b, w_in=v_w_in, b_in=v_b_in, sinks=v_sinks, vn_g=v_vn_g, vn_b=v_vn_b,
                w_s=v_w_s, b_s=v_b_s, p_a=v_p_a, p_b=v_p_b, w_out=v_w_out, b_out=v_b_out, ln_g=v_ln_g, ln_b=v_ln_b)
    n_seq, S, _ = x.shape
    T = n_seq * S
    c = lax.axis_index("c")
    j = 2 * lax.axis_index("x") + lax.axis_index("y")
    core = c.astype(jnp.int32).reshape((1,))
    chips = jnp.stack([j] + [r + (r >= j) for r in range(N_CHIPS - 1)]).astype(jnp.int32)
    names = list(_PIECES)
    xt, tt = x.reshape((T, D_MODEL)), loss_target.reshape((T, D_MODEL))

    tview = lambda d: dict(d, w_in=jnp.swapaxes(d["w_in"], 1, 2))
    weights_t, mom1_t, mom2_t = tview(weights), tview(mom1), tview(mom2)

    def own_shards(l):
        return [weights_t[n][l].astype(MXU_DTYPE) for n in _BIG]

    def layer_weights(l, gathered):
        d = dict(zip(_PIECES, gathered))
        d.update(paT=d["pa"].T, pbT=d["pb"].T, woutT=d["wout"].T)
        d.update(bin=b_in[l][None], sinks=sinks[l], vng=vn_g[l][None], vnb=vn_b[l][None], ws=w_s[l],
                 bs=jnp.repeat(b_s[l].T, 128, axis=1), bout=b_out[l][None], lng=ln_g[l][None], lnb=ln_b[l][None])
        return d

    own0, own1 = own_shards(0), own_shards(1)
    x0, g0 = input_norm_forward(xt, ln_in_g[None], ln_in_b[None], "ln_in_fwd", comm=gather_weights(names[:1], own0[:1]))
    n_proj = T // ROWS_PROJ
    h0, g1 = project(x0, g0[0], b_in[0][None], "layer0_proj",
                     comm=gather_weights(names[1:] + names[:1], own0[1:] + own1[:1], turn_step=(13 * n_proj) // 16))
    lw0 = layer_weights(0, g0 + g1[:len(names) - 1])
    x1, h1, g2 = layer_forward_and_projection(x0, h0, lw0, g1[-1], b_in[1][None], S, "layer0_fwd_layer1_proj",
                                              comm=gather_weights(names[1:], own1[1:], turn_step=T // ROWS // 2))
    lw1 = layer_weights(1, g1[-1:] + g2)

    dres1, dh1, acc1, _ = layer_backward(x1, h1, tt, lw1, S, True, "layer1_bwd")
    dx1, dwinT1, _ = input_grads(dres1, dh1, x1, lw1["winT"], "layer1_dx_dwin")
    parts1 = [dwinT1, acc1["dpa"], acc1["dpb"], acc1["dwout"]]
    dres0, dh0, acc0, landed1 = layer_backward(x0, h0, dx1, lw0, S, False, "layer0_bwd", comm=pair_send_halves(names, parts1))
    sums1 = add_own_halves(names, parts1, landed1, core, "add_own_halves1")
    parts0 = [None, acc0["dpa"], acc0["dpb"], acc0["dwout"]]
    dx0, parts0[0], riding = input_grads(dres0, dh0, x0, lw0["winT"], "layer0_dx_dwin",
                                         comm=_both(chip_exchange(sums1), pair_send_halves(names[1:], parts0[1:])))
    from_chips1, landed0_rest = riding[:len(names)], riding[len(names):]
    totals1 = sum_chips(sums1, from_chips1, chips, "sum_chips1")
    grad_x, d_ln_in_g, d_ln_in_b, first = input_norm_backward(
        xt, dx0, ln_in_g, "ln_in_bwd", comm=pair_send_halves(names[:1], parts0[:1], extra=totals1))

    gl = [acc0, acc1]
    per_layer = lambda key, pick: jnp.stack([pick(gl[l][key]) for l in range(DEPTH)])
    row0 = lambda key: per_layer(key, lambda a: a[0])
    small_part = _pack_small(dict(
        ln_in_g=d_ln_in_g[0], ln_in_b=d_ln_in_b[0], b_in=row0("dbin"), sinks=per_layer("dsink", lambda a: a[:, 0]),
        vn_g=row0("dvng"), vn_b=row0("dvnb"), w_s=per_layer("dws", lambda a: a),
        b_s=per_layer("dbs", lambda a: a[:, ::128].T), b_out=row0("dbout"), ln_g=row0("dlng"), ln_b=row0("dlnb")),
        last_row=acc1["loss"][0, 0])
    small_landed, = _run_comm(pair_send_halves([], [], small_part), "pair_send_small")
    landed0, sib_totals1 = first[:1] + landed0_rest, first[1:]
    sums0 = add_own_halves(names, parts0, landed0, core, "add_own_halves0")
    small_sum = add_small_half(small_part, small_landed, core)
    *from_chips0, small_from_chips = _run_comm(chip_exchange(sums0, small_sum), "chip_exchange0")
    *totals0, small_total = sum_chips(sums0, from_chips0, chips, "sum_chips0", small_landed=small_from_chips)
    *sib_totals0, sib_small = _run_comm(pair_send_totals(totals0 + [small_total]), "pair_send_totals0")

    def both_halves(mine, theirs):
        return jnp.where(c == 0, jnp.concatenate([mine, theirs], axis=0), jnp.concatenate([theirs, mine], axis=0))

    g_halves = [(totals0[pi], sib_totals0[pi], totals1[pi], sib_totals1[pi]) for pi in range(len(_BIG))]
    g_small = both_halves(small_total, sib_small)
    loss = g_small[-1, 0]

    big = adamw_big([weights_t[n] for n in _BIG], g_halves, [mom1_t[n] for n in _BIG], [mom2_t[n] for n in _BIG], core)
    small_names = [n for n, _ in _SMALL]
    as2d = lambda a: a.reshape((1, -1)) if a.ndim == 1 else a
    g_named = _unpack_small(g_small)
    small = adamw_small(*[[as2d(d[n]) for n in small_names] for d in (weights, g_named, mom1, mom2)])

    def named(big_outs, small_outs):
        d = dict(zip(_BIG, big_outs))
        d["w_in"] = jnp.swapaxes(d["w_in"], 1, 2)
        d.update({n: o.reshape(weights[n].shape) for n, o in zip(small_names, small_outs)})
        return d

    order = ("ln_in_g", "ln_in_b", "w_in", "b_in", "sinks", "vn_g", "vn_b", "w_s", "b_s", "p_a", "p_b", "w_out", "b_out",
             "ln_g", "ln_b")
    res = [loss, grad_x.reshape(x.shape)]
    groups = [named(big[0], [g_named[n] for n in small_names])] + [named(big[k + 1], small[k]) for k in range(3)]
    for group in groups:
        res.extend(group[n] for n in order)
    return tuple(res)
```

```python
import functools

import jax
import jax.numpy as jnp
from jax import lax
from jax.experimental import pallas as pl
from jax.experimental.pallas import tpu as pltpu

F32 = jnp.float32
MXU_DTYPE = jnp.bfloat16

D_MODEL = 1024
DEPTH = 2
HEAD_DIM = 64
ATTN_W = 512
KV_W = 128
BLK = 128
SGU_W = 512
SGU_G = 4
N_COLS = 4864
C_Q, C_K, C_V, C_GA, C_UB, C_VB, C_GB, C_RA, C_RB = 0, 512, 640, 768, 1280, 1792, 2304, 2816, 3840
ALPHA = (2.0 * DEPTH) ** 0.25
LN_EPS = 1e-5
SCALE = HEAD_DIM ** -0.5
NEG = float(jnp.finfo(jnp.float32).min)
GELU_C = 0.7978845608028654
GELU_A = 0.044715

ADAM_LR, ADAM_B1, ADAM_B2, ADAM_EPS, ADAM_WD, ADAM_STEP = 0.001, 0.9, 0.999, 1e-08, 0.01, 10

V7X_VMEM_BYTES = 64 * 1024 * 1024
V7X_VMEM_RESERVE = 3 * 1024 * 1024
ROWS = 256
ROWS_BWD = 256
ROWS_PROJ = 512

MESH = pl.DeviceIdType.MESH
N_CHIPS = 4


def _mm(a, b):
    return jnp.dot(a.astype(MXU_DTYPE), b.astype(MXU_DTYPE), preferred_element_type=F32)


def _mm_nt(a, b):
    return lax.dot_general(a.astype(MXU_DTYPE), b.astype(MXU_DTYPE), (((1,), (1,)), ((), ())),
                           preferred_element_type=F32)


def _mm_tn(a, b):
    return lax.dot_general(a.astype(MXU_DTYPE), b.astype(MXU_DTYPE), (((0,), (0,)), ((), ())),
                           preferred_element_type=F32)


def _sigmoid(x):
    return 0.5 * jnp.tanh(0.5 * x) + 0.5


def _gelu_parts(x):
    x2 = x * x
    u = 0.5 * jnp.tanh(x * (GELU_C + (GELU_C * GELU_A) * x2)) + 0.5
    dg = u * (1.0 + x * (1.0 - u) * (2.0 * GELU_C + (6.0 * GELU_C * GELU_A) * x2))
    return x * u, dg


def _ln_fwd(z, g, b):
    mu = jnp.mean(z, axis=-1, keepdims=True)
    zc = z - mu
    var = jnp.mean(zc * zc, axis=-1, keepdims=True)
    rstd = lax.rsqrt(var + LN_EPS)
    zhat = zc * rstd
    return zhat * g + b, zhat, rstd


def _ln_bwd(dy, zhat, rstd, g):
    dzh = dy * g
    m1 = jnp.mean(dzh, axis=-1, keepdims=True)
    m2 = jnp.mean(dzh * zhat, axis=-1, keepdims=True)
    return rstd * (dzh - m1 - zhat * m2)


def _colsum(v):
    return jnp.sum(v, axis=0, keepdims=True)


def _lane_lo(rows):
    return lax.broadcasted_iota(jnp.int32, (rows, 128), 1) < HEAD_DIM


def _dup_head(x2, hk, lo):
    xr = pltpu.roll(x2, HEAD_DIM, 1)
    return jnp.where(lo, x2, xr) if hk == 0 else jnp.where(lo, xr, x2)


def _fold_head(y0, y1, lo):
    f0 = y0 + pltpu.roll(y0, HEAD_DIM, 1)
    f1 = y1 + pltpu.roll(y1, HEAD_DIM, 1)
    return jnp.where(lo, f0, f1)


def _stack_heads(t0, t1, lo):
    z = jnp.zeros_like(t0)
    return jnp.concatenate([jnp.where(lo, t0, z), jnp.where(lo, z, t0),
                            jnp.where(lo, t1, z), jnp.where(lo, z, t1)], axis=0)


def _unstack_heads(o, lo):
    return (jnp.where(lo, o[0:128], o[128:256]), jnp.where(lo, o[256:384], o[384:512]))


def _band_bias(start):
    row = lax.broadcasted_iota(jnp.int32, (4 * BLK, 2 * BLK), 0) & (BLK - 1)
    kpos = lax.broadcasted_iota(jnp.int32, (4 * BLK, 2 * BLK), 1)
    valid = (kpos > row) & (kpos <= row + BLK)
    if start is not None:
        valid = valid & (jnp.logical_not(start) | (kpos >= BLK))
    return jnp.where(valid, 0.0, NEG)


def _attn_probs(s, sink4, bias):
    s = s * SCALE + bias
    r1 = lax.broadcasted_iota(jnp.int32, (4 * BLK, 1), 0)
    sk = jnp.where(r1 < BLK, sink4[0], jnp.where(r1 < 2 * BLK, sink4[1], jnp.where(r1 < 3 * BLK, sink4[2], sink4[3])))
    m = jnp.maximum(jnp.max(s, axis=-1, keepdims=True), sk)
    p = jnp.exp(s - m)
    es = jnp.exp(sk - m)
    inv = 1.0 / (jnp.sum(p, axis=-1, keepdims=True) + es)
    return p * inv, es * inv


_PROJ_CHUNK = 512


def _kv_blocks(kvh, h_s, nb):
    ks = [kvh[:, 0:KV_W]] + [h_s[n * BLK:(n + 1) * BLK, C_K:C_K + KV_W] for n in range(nb)]
    vs = [kvh[:, KV_W:2 * KV_W]] + [h_s[n * BLK:(n + 1) * BLK, C_V:C_V + KV_W] for n in range(nb)]
    return ks, vs


def _attn_operands(h_s, ks, vs, n, hk, lo, lo2):
    rows = slice(n * BLK, (n + 1) * BLK)
    kd = _dup_head(jnp.concatenate([ks[n], ks[n + 1]], axis=0), hk, lo2)
    vd = _dup_head(jnp.concatenate([vs[n], vs[n + 1]], axis=0), hk, lo2)
    c0 = C_Q + 2 * hk * 128
    qs = _stack_heads(h_s[rows, c0:c0 + 128], h_s[rows, c0 + 128:c0 + 256], lo)
    return qs, kd, vd


def _sgu_weights(w):
    tri = (lax.broadcasted_iota(jnp.int32, (BLK, BLK), 0) >= lax.broadcasted_iota(jnp.int32, (BLK, BLK), 1))
    return tri, [jnp.where(tri, w["ws"][g], 0.0) for g in range(SGU_G)]


def _layer_forward(x, kvh, start, w, h_s, attn_s, mix_s, keep, tick=lambda: None):
    R = x.shape[0]
    nb = R // BLK
    lo = _lane_lo(BLK)
    lo2 = _lane_lo(2 * BLK)
    ks, vs = _kv_blocks(kvh, h_s, nb)
    sinks = [w["sinks"][j] for j in range(8)]
    bias_first = _band_bias(start)
    bias_rest = _band_bias(None) if nb > 1 else None
    pairs = [(n, hk) for n in range(nb) for hk in range(2)]
    ops = [_attn_operands(h_s, ks, vs, n, hk, lo, lo2) for n, hk in pairs]
    scores = [_mm_nt(qs, kd) for qs, kd, _ in ops]
    probs_l = [_attn_probs(s, sinks[4 * hk:4 * hk + 4], bias_first if n == 0 else bias_rest)
               for s, (n, hk) in zip(scores, pairs)]
    outs = [_mm(p, vd) for (p, _), (_, _, vd) in zip(probs_l, ops)]
    for o, (n, hk) in zip(outs, pairs):
        rows = slice(n * BLK, (n + 1) * BLK)
        t0, t1 = _unstack_heads(o, lo)
        attn_s[rows, 2 * hk * 128:(2 * hk + 1) * 128] = t0
        attn_s[rows, (2 * hk + 1) * 128:(2 * hk + 2) * 128] = t1
        tick()
    ga = h_s[:, C_GA:C_GA + ATTN_W]
    ya = attn_s[...] * (ga * _sigmoid(ga))
    tick()
    gu, _ = _gelu_parts(h_s[:, C_UB:C_UB + SGU_W])
    tick()
    gv, _ = _gelu_parts(h_s[:, C_VB:C_VB + SGU_W])
    vn, _, _ = _ln_fwd(gv, w["vng"][...], w["vnb"][...])
    tick()
    _, wms = _sgu_weights(w)
    tiles = [(slice(n * BLK, (n + 1) * BLK), g, slice(g * 128, (g + 1) * 128)) for n in range(nb) for g in range(SGU_G)]
    mixes = [_mm(wms[g], vn[rows, cols]) for rows, g, cols in tiles]
    for m, (rows, g, cols) in zip(mixes, tiles):
        mix_s[rows, cols] = m + w["bs"][:, cols]
    gb = h_s[:, C_GB:C_GB + SGU_W]
    yb = gu * mix_s[...] * (gb * _sigmoid(gb))
    tick()
    a = _mm(ya, w["pa"][...])
    b = _mm(yb, w["pb"][...])
    tick()
    merged = _sigmoid(h_s[:, C_RA:C_RA + D_MODEL]) * a + _sigmoid(h_s[:, C_RB:C_RB + D_MODEL]) * b
    tick()
    out = _mm(merged, w["wout"][...]) + w["bout"][...]
    y, zhat, rstd = _ln_fwd(ALPHA * x + out, w["lng"][...], w["lnb"][...])
    if not keep:
        return y, None
    return y, dict(ks=ks, vs=vs, probs=probs_l, ya=ya, yb=yb, a=a, b=b, merged=merged, zhat=zhat, rstd=rstd)


def _dsilu(g, sg):
    return sg * (1.0 + g * (1.0 - sg))


_W_NAMES = ("sinks", "vng", "vnb", "ws", "bs", "pa", "pb", "wout", "bout", "lng", "lnb")
_WT_NAMES = ("paT", "pbT", "woutT")
_ACC_NAMES = ("dwout", "dpa", "dpb", "dbin", "dbout", "dlng", "dlnb", "dvng", "dvnb", "dws", "dbs", "dsink", "loss")


def _make_fwd_proj_body(R, S, nT):
    def body(x_ref, h_ref, kvh_ref, wn_ref, bn_ref, *rest):
        w = dict(zip(_W_NAMES, rest[:len(_W_NAMES)]))
        y_ref, hn_ref, attn_s, mix_s, yprev_s = rest[len(_W_NAMES):]

        @pl.when(pl.program_id(0) == 0)
        def _():
            yprev_s[...] = jnp.zeros(yprev_s.shape, yprev_s.dtype)

        t = jnp.minimum(pl.program_id(0), nT - 1)
        start = (t % (S // R)) == 0
        chunks = [(c0, min(c0 + _PROJ_CHUNK, N_COLS)) for c0 in range(0, N_COLS, _PROJ_CHUNK)]

        def tick():
            if chunks:
                c0, c1 = chunks.pop(0)
                hn_ref[:, c0:c1] = _mm_nt(yprev_s[...], wn_ref[c0:c1, :]) + bn_ref[:, c0:c1]

        y, _ = _layer_forward(x_ref[...], kvh_ref[...], start, w, h_ref, attn_s, mix_s, keep=False, tick=tick)
        while chunks:
            tick()
        y_ref[...] = y
        yprev_s[...] = y.astype(yprev_s.dtype)
    return body


def _make_bwd_body(R, S, nT, is_last):
    nb = R // BLK
    names = _W_NAMES + _WT_NAMES
    nw = len(names)

    def body(x_ref, h_s, kvh_ref, aux_ref, *rest):
        w = dict(zip(names, rest[:nw]))
        dres_ref, dh_ref = rest[nw:nw + 2]
        acc = dict(zip(_ACC_NAMES, rest[nw + 2:nw + 2 + len(_ACC_NAMES)]))
        attn_s, mix_s, ckv_s = rest[nw + 2 + len(_ACC_NAMES):]
        i = pl.program_id(0)
        start = ((nT - 1 - i) % (S // R)) == 0
        lo = _lane_lo(BLK)
        lo2 = _lane_lo(2 * BLK)

        @pl.when(i == 0)
        def _():
            for name in _ACC_NAMES:
                acc[name][...] = jnp.zeros(acc[name].shape, F32)
            ckv_s[...] = jnp.zeros(ckv_s.shape, F32)

        x = x_ref[...]
        y, sv = _layer_forward(x, kvh_ref[...], start, w, h_s, attn_s, mix_s, keep=True)
        if is_last:
            diff = y - aux_ref[...]
            part = 0.5 * jnp.sum(jnp.mean(diff * diff, axis=-1, keepdims=True), axis=0, keepdims=True)
            acc["loss"][...] += jnp.broadcast_to(part, acc["loss"].shape)
            dy = diff * (1.0 / D_MODEL)
        else:
            dy = aux_ref[...]

        def put(c0, val, rows=slice(None)):
            width = val.shape[1]
            dh_ref[rows, c0:c0 + width] = val.astype(dh_ref.dtype)
            acc["dbin"][:, c0:c0 + width] += _colsum(val)

        acc["dlng"][...] += _colsum(dy * sv["zhat"])
        acc["dlnb"][...] += _colsum(dy)
        dz = _ln_bwd(dy, sv["zhat"], sv["rstd"], w["lng"][...])
        dres_ref[...] = ALPHA * dz
        acc["dbout"][...] += _colsum(dz)
        acc["dwout"][...] += _mm_tn(sv["merged"], dz)
        dmerged = _mm(dz, w["woutT"][...])
        sa = _sigmoid(h_s[:, C_RA:C_RA + D_MODEL])
        da = dmerged * sa
        put(C_RA, da * sv["a"] * (1.0 - sa))
        sb = _sigmoid(h_s[:, C_RB:C_RB + D_MODEL])
        db = dmerged * sb
        put(C_RB, db * sv["b"] * (1.0 - sb))
        acc["dpa"][...] += _mm_tn(sv["ya"], da)
        acc["dpb"][...] += _mm_tn(sv["yb"], db)
        dya = _mm(da, w["paT"][...])
        dyb = _mm(db, w["pbT"][...])
        ga = h_s[:, C_GA:C_GA + ATTN_W]
        sga = _sigmoid(ga)
        put(C_GA, dya * attn_s[...] * _dsilu(ga, sga))
        attn_s[...] = dya * (ga * sga)
        ks, vs = sv["ks"], sv["vs"]
        pairs = [(n, hk) for n in range(nb) for hk in range(2)]
        ops = [_attn_operands(h_s, ks, vs, n, hk, lo, lo2) for n, hk in pairs]
        doss = [_stack_heads(attn_s[n * BLK:(n + 1) * BLK, 2 * hk * 128:(2 * hk + 1) * 128],
                             attn_s[n * BLK:(n + 1) * BLK, (2 * hk + 1) * 128:(2 * hk + 2) * 128], lo) for n, hk in pairs]
        dps = [_mm_nt(dos, vd) for dos, (_, _, vd) in zip(doss, ops)]
        gb = h_s[:, C_GB:C_GB + SGU_W]
        sgb = _sigmoid(gb)
        gu, dgu = _gelu_parts(h_s[:, C_UB:C_UB + SGU_W])
        mixed = mix_s[...]
        put(C_GB, dyb * (gu * mixed) * _dsilu(gb, sgb))
        dsgu = dyb * (gb * sgb)
        put(C_UB, dsgu * mixed * dgu)
        dmixed = dsgu * gu
        dsss = []
        for dp, (probs, ps), (n, hk) in zip(dps, sv["probs"], pairs):
            delta = jnp.sum(probs * dp, axis=-1, keepdims=True)
            dsk = -(ps * delta)
            for g in range(4):
                j = 4 * hk + g
                acc["dsink"][j:j + 1, :] += jnp.broadcast_to(_colsum(dsk[g * BLK:(g + 1) * BLK]), (1, 128))
            dsss.append(probs * (dp - delta) * SCALE)
        dqs = [_mm(dss, kd) for dss, (_, kd, _) in zip(dsss, ops)]
        ydks = [_mm_tn(dss, qs) for dss, (qs, _, _) in zip(dsss, ops)]
        ydvs = [_mm_tn(probs, dos) for (probs, _), dos in zip(sv["probs"], doss)]
        gv, dgv = _gelu_parts(h_s[:, C_VB:C_VB + SGU_W])
        vn, vhat, vrstd = _ln_fwd(gv, w["vng"][...], w["vnb"][...])
        tri, wms = _sgu_weights(w)
        tiles = [(slice(n * BLK, (n + 1) * BLK), g, slice(g * 128, (g + 1) * 128)) for n in range(nb) for g in range(SGU_G)]
        dw_parts = [_mm_nt(dmixed[rows, cols], vn[rows, cols]) for rows, g, cols in tiles]
        dvn_parts = [_mm_tn(wms[g], dmixed[rows, cols]) for rows, g, cols in tiles]
        for g in range(SGU_G):
            tot = dw_parts[g]
            for n in range(1, nb):
                tot = tot + dw_parts[n * SGU_G + g]
            acc["dws"][g] += jnp.where(tri, tot, 0.0)
        for part, (rows, g, cols) in zip(dvn_parts, tiles):
            mix_s[rows, cols] = part
        dbs_part = dmixed[0:BLK]
        for n in range(1, nb):
            dbs_part = dbs_part + dmixed[n * BLK:(n + 1) * BLK]
        acc["dbs"][...] += dbs_part
        for dq, (n, hk) in zip(dqs, pairs):
            q0, q1 = _unstack_heads(dq, lo)
            put(C_Q + 2 * hk * 128, q0, slice(n * BLK, (n + 1) * BLK))
            put(C_Q + (2 * hk + 1) * 128, q1, slice(n * BLK, (n + 1) * BLK))
        dks = [jnp.zeros((BLK, KV_W), F32) for _ in range(nb + 1)]
        dvs = [jnp.zeros((BLK, KV_W), F32) for _ in range(nb + 1)]
        for n in range(nb):
            dk2 = _fold_head(ydks[2 * n], ydks[2 * n + 1], lo2)
            dv2 = _fold_head(ydvs[2 * n], ydvs[2 * n + 1], lo2)
            dks[n] = dks[n] + dk2[0:BLK]
            dks[n + 1] = dks[n + 1] + dk2[BLK:2 * BLK]
            dvs[n] = dvs[n] + dv2[0:BLK]
            dvs[n + 1] = dvs[n + 1] + dv2[BLK:2 * BLK]
        dks[nb] = dks[nb] + ckv_s[:, 0:KV_W]
        dvs[nb] = dvs[nb] + ckv_s[:, KV_W:2 * KV_W]
        ckv_s[:, 0:KV_W] = dks[0]
        ckv_s[:, KV_W:2 * KV_W] = dvs[0]
        put(C_K, jnp.concatenate(dks[1:], axis=0))
        put(C_V, jnp.concatenate(dvs[1:], axis=0))
        dvn = mix_s[...]
        acc["dvng"][...] += _colsum(dvn * vhat)
        acc["dvnb"][...] += _colsum(dvn)
        put(C_VB, _ln_bwd(dvn, vhat, vrstd, w["vng"][...]) * dgv)

        @pl.when(i == nT - 1)
        def _():
            for g in range(SGU_G):
                cols = slice(g * 128, (g + 1) * 128)
                tot = jnp.sum(acc["dbs"][:, cols], axis=1, keepdims=True)
                acc["dbs"][:, cols] = jnp.broadcast_to(tot, (BLK, 128))

    return body


def _resident():
    return pl.BlockSpec(memory_space=pltpu.VMEM)


_BIG_RESIDENT = ("pa", "pb", "wout", "paT", "pbT", "woutT", "dwout", "dpa", "dpb")


def _const_block(shape):
    return pl.BlockSpec(tuple(shape), lambda *_: (0,) * len(shape))


def _operand_spec(name, shape):
    if name == "sinks":
        return pl.BlockSpec(memory_space=pltpu.SMEM)
    return _resident() if name in _BIG_RESIDENT else _const_block(shape)


def _weight_specs(names, wts):
    return [_operand_spec(n, wts[n].shape) for n in names]


def _nbytes(a):
    n = jnp.dtype(a.dtype).itemsize
    for d in a.shape:
        n *= d
    return n


def _layer_params(resident=()):
    scoped = V7X_VMEM_BYTES - V7X_VMEM_RESERVE - sum(_nbytes(a) for a in resident)
    return pltpu.CompilerParams(dimension_semantics=("arbitrary",), vmem_limit_bytes=scoped)


class _Comm:
    def __init__(self, ins, out_shapes, sems, start, finish, aliases=None, turn=None, turn_step=None):
        self.ins = [pltpu.with_memory_space_constraint(a, pltpu.HBM) for a in ins]
        self.out_shapes = [pltpu.HBM(s.shape, s.dtype) for s in out_shapes]
        self.sems = list(sems)
        self.start, self.finish, self.aliases = start, finish, dict(aliases or {})
        self.turn, self.turn_step = turn, turn_step


_ANY = pl.BlockSpec(memory_space=pltpu.HBM)
_DMA = pltpu.SemaphoreType.DMA


def _call(body, *, name, grid, in_specs, out_specs, out_shape, scratch_shapes, compiler_params, operands, comm=None):
    if comm is None:
        res = pl.pallas_call(body, name=name, grid=grid, in_specs=in_specs, out_specs=out_specs, out_shape=out_shape,
                             scratch_shapes=scratch_shapes, compiler_params=compiler_params)(*operands)
        return list(res), []
    n_in, n_out, n_scr = len(in_specs), len(out_specs), len(scratch_shapes)
    ci, co = len(comm.ins), len(comm.out_shapes)
    last = grid[0] - 1

    def hosted(*refs):
        ins, cin = refs[:n_in], refs[n_in:n_in + ci]
        p = n_in + ci
        outs, cout = refs[p:p + n_out], refs[p + n_out:p + n_out + co]
        p += n_out + co
        scr, sems = refs[p:p + n_scr], refs[p + n_scr:]
        pl.when(pl.program_id(0) == 0)(lambda: comm.start(cin, cout, sems))
        body(*ins, *outs, *scr)
        if comm.turn is not None:
            at = last if comm.turn_step is None else comm.turn_step
            pl.when(pl.program_id(0) == at)(lambda: comm.turn(cin, cout, sems))
        pl.when(pl.program_id(0) == last)(lambda: comm.finish(cin, cout, sems))

    res = pl.pallas_call(
        hosted, name=name, grid=grid, in_specs=list(in_specs) + [_ANY] * ci, out_specs=list(out_specs) + [_ANY] * co,
        out_shape=list(out_shape) + comm.out_shapes, scratch_shapes=list(scratch_shapes) + comm.sems,
        input_output_aliases={n_in + a: n_out + b for a, b in comm.aliases.items()},
        compiler_params=compiler_params)(*operands, *comm.ins)
    return list(res[:n_out]), list(res[n_out:])


def _both(a, b):
    ni, no, ns = len(a.ins), len(a.out_shapes), len(a.sems)

    def each(f, g):
        def run(cin, cout, sems):
            f(cin[:ni], cout[:no], sems[:ns])
            g(cin[ni:], cout[no:], sems[ns:])
        return run

    both = _Comm([], [], a.sems + b.sems, each(a.start, b.start), each(a.finish, b.finish))
    both.ins, both.out_shapes = a.ins + b.ins, a.out_shapes + b.out_shapes
    return both


def _run_comm(comm, name):
    ci, co = len(comm.ins), len(comm.out_shapes)

    def body(*refs):
        cin, cout, sems = refs[:ci], refs[ci:ci + co], refs[ci + co:]
        comm.start(cin, cout, sems)
        if comm.turn is not None:
            comm.turn(cin, cout, sems)
        comm.finish(cin, cout, sems)

    return list(pl.pallas_call(body, name=name, in_specs=[_ANY] * ci, out_specs=[_ANY] * co, out_shape=comm.out_shapes,
                               scratch_shapes=comm.sems, input_output_aliases=comm.aliases)(*comm.ins))


def project(x, winT, b, name, comm=None):
    T = x.shape[0]
    R = ROWS_PROJ

    def body(x_ref, w_ref, b_ref, h_ref):
        xb = x_ref[...].astype(MXU_DTYPE)
        for c0 in range(0, N_COLS, _PROJ_CHUNK):
            c1 = min(c0 + _PROJ_CHUNK, N_COLS)
            h_ref[:, c0:c1] = _mm_nt(xb, w_ref[c0:c1, :]) + b_ref[:, c0:c1]

    (h,), extra = _call(
        body, name=name, grid=(T // R,),
        in_specs=[pl.BlockSpec((R, D_MODEL), lambda t: (t, 0)), _resident(), _const_block(b.shape)],
        out_specs=[pl.BlockSpec((R, N_COLS), lambda t: (t, 0))],
        out_shape=[jax.ShapeDtypeStruct((T, N_COLS), F32)], scratch_shapes=[],
        compiler_params=_layer_params([winT]), operands=(x, winT, b), comm=comm)
    return h, extra


_KV_BLOCK = C_K // (2 * KV_W)


def layer_forward_and_projection(x, h, wts, winT_next, b_next, S, name, comm=None):
    T = x.shape[0]
    R = ROWS
    nT = T // R
    npt = R // BLK
    cur = lambda t: jnp.minimum(t, nT - 1)
    (y, hn), extra = _call(
        _make_fwd_proj_body(R, S, nT), name=name, grid=(nT + 1,),
        in_specs=[pl.BlockSpec((R, D_MODEL), lambda t: (cur(t), 0)), pl.BlockSpec((R, N_COLS), lambda t: (cur(t), 0)),
                  pl.BlockSpec((BLK, 2 * KV_W), lambda t: (jnp.maximum(cur(t) * npt - 1, 0), _KV_BLOCK)),
                  _resident(), _const_block(b_next.shape)] + _weight_specs(_W_NAMES, wts),
        out_specs=[pl.BlockSpec((R, D_MODEL), lambda t: (cur(t), 0)),
                   pl.BlockSpec((R, N_COLS), lambda t: (jnp.maximum(t - 1, 0), 0))],
        out_shape=[jax.ShapeDtypeStruct((T, D_MODEL), F32), jax.ShapeDtypeStruct((T, N_COLS), F32)],
        scratch_shapes=[pltpu.VMEM((R, ATTN_W), F32), pltpu.VMEM((R, SGU_W), F32), pltpu.VMEM((R, D_MODEL), MXU_DTYPE)],
        compiler_params=_layer_params([winT_next] + [wts[n] for n in _W_NAMES if n in _BIG_RESIDENT]),
        operands=(x, h, h, winT_next, b_next, *[wts[n] for n in _W_NAMES]), comm=comm)
    return y, hn, extra


_ACC_SHAPES = dict(dwout=(D_MODEL, D_MODEL), dpa=(ATTN_W, D_MODEL), dpb=(SGU_W, D_MODEL), dbin=(1, N_COLS),
                   dbout=(1, D_MODEL), dlng=(1, D_MODEL), dlnb=(1, D_MODEL), dvng=(1, SGU_W), dvnb=(1, SGU_W),
                   dws=(SGU_G, BLK, BLK), dbs=(BLK, SGU_W), dsink=(8, 128), loss=(8, 128))


def layer_backward(x, h, aux, wts, S, is_last, name, comm=None):
    T = x.shape[0]
    R = ROWS_BWD
    nT = T // R
    npt = R // BLK
    rev = lambda i: (nT - 1 - i, 0)
    halo = lambda i: (jnp.maximum((nT - 1 - i) * npt - 1, 0), _KV_BLOCK)
    names = _W_NAMES + _WT_NAMES
    out_shape = ([jax.ShapeDtypeStruct((T, D_MODEL), F32), jax.ShapeDtypeStruct((T, N_COLS), MXU_DTYPE)]
                 + [jax.ShapeDtypeStruct(_ACC_SHAPES[n], F32) for n in _ACC_NAMES])
    outs, extra = _call(
        _make_bwd_body(R, S, nT, is_last), name=name, grid=(nT,),
        in_specs=[pl.BlockSpec((R, D_MODEL), rev), pl.BlockSpec((R, N_COLS), rev), pl.BlockSpec((BLK, 2 * KV_W), halo),
                  pl.BlockSpec((R, D_MODEL), rev)] + _weight_specs(names, wts),
        out_specs=[pl.BlockSpec((R, D_MODEL), rev), pl.BlockSpec((R, N_COLS), rev)]
        + [_operand_spec(n, _ACC_SHAPES[n]) for n in _ACC_NAMES],
        out_shape=out_shape,
        scratch_shapes=[pltpu.VMEM((R, ATTN_W), F32), pltpu.VMEM((R, SGU_W), F32), pltpu.VMEM((BLK, 2 * KV_W), F32)],
        compiler_params=_layer_params([wts[n] for n in names if n in _BIG_RESIDENT]
                                      + [jax.ShapeDtypeStruct(_ACC_SHAPES[n], F32) for n in _ACC_NAMES if n in _BIG_RESIDENT]),
        operands=(x, h, h, aux, *[wts[n] for n in names]), comm=comm)
    return outs[0], outs[1], dict(zip(_ACC_NAMES, outs[2:])), extra


def input_grads(dres, dh, x, winT, name, comm=None):
    T = x.shape[0]
    R = ROWS

    def body(dres_ref, dh_ref, x_ref, winT_ref, dx_ref, dwin_ref):
        @pl.when(pl.program_id(0) == 0)
        def _():
            dwin_ref[...] = jnp.zeros(dwin_ref.shape, F32)

        dh = dh_ref[...]
        dx_ref[...] = dres_ref[...] + jnp.dot(dh, winT_ref[...], preferred_element_type=F32)
        dwin_ref[...] += _mm_tn(dh, x_ref[...])

    row = lambda t: (t, 0)
    dwin_shape = jax.ShapeDtypeStruct((N_COLS, D_MODEL), F32)
    (dx, dwinT), extra = _call(
        body, name=name, grid=(T // R,),
        in_specs=[pl.BlockSpec((R, D_MODEL), row), pl.BlockSpec((R, N_COLS), row), pl.BlockSpec((R, D_MODEL), row),
                  _resident()],
        out_specs=[pl.BlockSpec((R, D_MODEL), row), _resident()],
        out_shape=[jax.ShapeDtypeStruct((T, D_MODEL), F32), dwin_shape], scratch_shapes=[],
        compiler_params=_layer_params([winT, dwin_shape]), operands=(dres, dh, x, winT), comm=comm)
    return dx, dwinT, extra


def input_norm_forward(x, g, b, name, comm=None):
    T = x.shape[0]
    R = ROWS_PROJ

    def body(x_ref, g_ref, b_ref, y_ref):
        y_ref[...] = _ln_fwd(x_ref[...], g_ref[...], b_ref[...])[0]

    row = lambda t: (t, 0)
    (y,), extra = _call(
        body, name=name, grid=(T // R,),
        in_specs=[pl.BlockSpec((R, D_MODEL), row), _const_block(g.shape), _const_block(b.shape)],
        out_specs=[pl.BlockSpec((R, D_MODEL), row)],
        out_shape=[jax.ShapeDtypeStruct((T, D_MODEL), F32)], scratch_shapes=[],
        compiler_params=_layer_params(), operands=(x, g, b), comm=comm)
    return y, extra


def input_norm_backward(x, dy, g, name, comm=None):
    T = x.shape[0]
    R = ROWS_PROJ

    def body(x_ref, dy_ref, g_ref, dx_ref, dg_ref, db_ref):
        @pl.when(pl.program_id(0) == 0)
        def _():
            dg_ref[...] = jnp.zeros(dg_ref.shape, F32)
            db_ref[...] = jnp.zeros(db_ref.shape, F32)

        dy = dy_ref[...]
        _, xhat, rstd = _ln_fwd(x_ref[...], g_ref[...], g_ref[...])
        dx_ref[...] = _ln_bwd(dy, xhat, rstd, g_ref[...])
        dg_ref[...] += _colsum(dy * xhat)
        db_ref[...] += _colsum(dy)

    row = lambda t: (t, 0)
    vec = jax.ShapeDtypeStruct((1, D_MODEL), F32)
    (dx, dg, db), extra = _call(
        body, name=name, grid=(T // R,),
        in_specs=[pl.BlockSpec((R, D_MODEL), row), pl.BlockSpec((R, D_MODEL), row), _const_block(g.shape)],
        out_specs=[pl.BlockSpec((R, D_MODEL), row), _const_block(vec.shape), _const_block(vec.shape)],
        out_shape=[jax.ShapeDtypeStruct((T, D_MODEL), F32), vec, vec], scratch_shapes=[],
        compiler_params=_layer_params(), operands=(x, dy, g), comm=comm)
    return dx, dg, db, extra


_PIECES = ("winT", "pa", "pb", "wout")
_WHOLE = dict(winT=(N_COLS, D_MODEL), pa=(ATTN_W, D_MODEL), pb=(SGU_W, D_MODEL), wout=(D_MODEL, D_MODEL))
_HALF = dict(winT=(N_COLS // 8, D_MODEL), pa=(ATTN_W // 2, D_MODEL // 4), pb=(SGU_W // 2, D_MODEL // 4),
             wout=(D_MODEL // 8, D_MODEL))
SMALL_ROWS = 1280


def _region(name, ref, k, h):
    hr, hc = _HALF[name]
    if name in ("pa", "pb"):
        return ref.at[pl.ds(h * hr, hr), pl.ds(k * hc, hc)]
    return ref.at[pl.ds(k * 2 * hr + h * hr, hr), :]


def _place():
    x, y, c = lax.axis_index("x"), lax.axis_index("y"), lax.axis_index("c")
    return x, y, c, 2 * x + y


def _as_chip(j, fn):
    for jj in range(N_CHIPS):
        pl.when(j == jj)(functools.partial(fn, jj))


def _remote(src, dst, send_sem, recv_sem, to):
    return pltpu.make_async_remote_copy(src_ref=src, dst_ref=dst, send_sem=send_sem, recv_sem=recv_sem,
                                        device_id=to, device_id_type=MESH)


def _core_of(k, c):
    return (k // 2, k % 2, c)


def _others(jj):
    return [k for k in range(N_CHIPS) if k != jj]


def _start_all(cps):
    for cp in cps:
        cp.start()


def _wait_all(cps):
    for cp in cps:
        cp.wait()


def gather_over_chips(names, shards, on_landed=None):
    n = len(names)

    def own(cin, cout, sems, jj):
        x, y, c, _ = _place()
        cps = []
        for i, name in enumerate(names):
            hr, hc = _HALF[name]
            place = (cout[i].at[:, pl.ds(jj * hc, hc)] if name in ("pa", "pb")
                     else cout[i].at[pl.ds(jj * 2 * hr, 2 * hr), :])
            cps.append(_remote(cin[i], place, sems[2].at[i], sems[3].at[i], (x, y, 1 - c)))
        return cps

    def copies(cin, cout, sems, jj, c):
        out = []
        for i, name in enumerate(names):
            hr = _HALF[name][0]
            for k in _others(jj):
                out.append(_remote(cin[i].at[pl.ds(c * hr, hr)], _region(name, cout[i], jj, c),
                                   sems[0].at[4 * i + k], sems[1].at[4 * i + jj], _core_of(k, c)))
        return out

    def start(cin, cout, sems):
        _, _, c, j = _place()
        _as_chip(j, lambda jj: _start_all(copies(cin, cout, sems, jj, c) + own(cin, cout, sems, jj)))

    def finish(cin, cout, sems):
        _, _, c, j = _place()

        def run(jj):
            for i, name in enumerate(names):
                for k in _others(jj):
                    land = _region(name, cout[i], k, c)
                    _remote(land, land, sems[0].at[4 * i + k], sems[1].at[4 * i + k], _core_of(k, c)).wait_recv()
                    if on_landed is not None:
                        on_landed(land, 4 * i + k, sems)
            for cp in copies(cin, cout, sems, jj, c):
                cp.wait_send()
            _wait_all(own(cin, cout, sems, jj))

        _as_chip(j, run)

    shapes = [jax.ShapeDtypeStruct(_WHOLE[nm], s.dtype) for nm, s in zip(names, shards)]
    return _Comm(shards, shapes, [_DMA((4 * n,)), _DMA((4 * n,)), _DMA((n,)), _DMA((n,))], start, finish)


def gather_weights(names, shards, turn_step=None):
    n = len(names)

    def pass_on(land, slot, sems):
        x, y, c, _ = _place()
        _remote(land, land, sems[4].at[slot], sems[5].at[slot], (x, y, 1 - c)).start()

    chips = gather_over_chips(names, shards, on_landed=pass_on)
    pair_finish = _pair_finish(names)
    return _Comm(shards, chips.out_shapes, chips.sems + [_DMA((4 * n,)), _DMA((4 * n,))], chips.start,
                 lambda cin, cout, sems: pair_finish(cin, cout, sems[4:]), turn=chips.finish, turn_step=turn_step)


def _pair_finish(names):
    def finish(cin, cout, sems):
        x, y, c, j = _place()

        def run(jj):
            for i, name in enumerate(names):
                for k in _others(jj):
                    theirs = _region(name, cout[i], k, 1 - c)
                    _remote(theirs, theirs, sems[0].at[4 * i + k], sems[1].at[4 * i + k], (x, y, 1 - c)).wait_recv()
            for i, name in enumerate(names):
                for k in _others(jj):
                    land = _region(name, cout[i], k, c)
                    _remote(land, land, sems[0].at[4 * i + k], sems[1].at[4 * i + k], (x, y, 1 - c)).wait_send()

        _as_chip(j, run)

    return finish


def pair_send_halves(names, parts, small=None, extra=()):
    n = len(names)
    hs = SMALL_ROWS // 2
    first_extra = n + (small is not None)

    def copies(cin, cout, sems):
        x, y, c, _ = _place()
        sib = (x, y, 1 - c)
        cps = []
        for i, name in enumerate(names):
            for k in range(N_CHIPS):
                cps.append(_remote(_region(name, cin[i], k, 1 - c), cout[i].at[k], sems[0].at[4 * i + k],
                                   sems[1].at[4 * i + k], sib))
        if small is not None:
            cps.append(_remote(cin[n].at[pl.ds((1 - c) * hs, hs)], cout[n], sems[0].at[4 * n], sems[1].at[4 * n], sib))
        for e in range(len(extra)):
            cps.append(_remote(cin[first_extra + e], cout[first_extra + e], sems[0].at[4 * n + 1 + e],
                               sems[1].at[4 * n + 1 + e], sib))
        return cps

    start = lambda cin, cout, sems: _start_all(copies(cin, cout, sems))
    finish = lambda cin, cout, sems: _wait_all(copies(cin, cout, sems))
    shapes = [jax.ShapeDtypeStruct((N_CHIPS,) + _HALF[nm], F32) for nm in names]
    ins = list(parts)
    if small is not None:
        shapes.append(jax.ShapeDtypeStruct((hs, 128), F32))
        ins.append(small)
    shapes += [jax.ShapeDtypeStruct(a.shape, a.dtype) for a in extra]
    ins += list(extra)
    n_sem = 4 * n + 1 + len(extra)
    return _Comm(ins, shapes, [_DMA((n_sem,)), _DMA((n_sem,))], start, finish)


def chip_exchange(sums, small=None):
    n = len(sums)

    def copies(cin, cout, sems, jj, c):
        out = []
        for k in _others(jj):
            for i in range(n):
                out.append(_remote(cin[i].at[k], cout[i].at[jj], sems[0].at[4 * i + k], sems[1].at[4 * i + jj], _core_of(k, c)))
            if small is not None:
                out.append(_remote(cin[n], cout[n].at[jj], sems[0].at[4 * n + k], sems[1].at[4 * n + jj], _core_of(k, c)))
        return out

    def start(cin, cout, sems):
        _, _, c, j = _place()

        def run(jj):
            if small is not None:
                pltpu.make_async_copy(cin[n], cout[n].at[jj], sems[2]).start()
            for cp in copies(cin, cout, sems, jj, c):
                cp.start()

        _as_chip(j, run)

    def finish(cin, cout, sems):
        _, _, c, j = _place()

        def run(jj):
            for k in _others(jj):
                for i in range(n):
                    _remote(cin[i].at[k], cout[i].at[k], sems[0].at[4 * i + k], sems[1].at[4 * i + k], _core_of(k, c)).wait_recv()
                if small is not None:
                    _remote(cin[n], cout[n].at[k], sems[0].at[4 * n + k], sems[1].at[4 * n + k], _core_of(k, c)).wait_recv()
            for cp in copies(cin, cout, sems, jj, c):
                cp.wait_send()
            if small is not None:
                pltpu.make_async_copy(cin[n], cout[n].at[jj], sems[2]).wait()

        _as_chip(j, run)

    shapes = [jax.ShapeDtypeStruct(a.shape, a.dtype) for a in sums]
    ins = list(sums)
    if small is not None:
        shapes.append(jax.ShapeDtypeStruct((N_CHIPS,) + small.shape, small.dtype))
        ins.append(small)
    return _Comm(ins, shapes, [_DMA((4 * n + 4,)), _DMA((4 * n + 4,)), _DMA(())], start, finish)


def pair_send_totals(arrs):
    n = len(arrs)

    def copies(cin, cout, sems):
        x, y, c, _ = _place()
        return [_remote(cin[i], cout[i], sems[0].at[i], sems[1].at[i], (x, y, 1 - c)) for i in range(n)]

    start = lambda cin, cout, sems: _start_all(copies(cin, cout, sems))
    finish = lambda cin, cout, sems: _wait_all(copies(cin, cout, sems))
    return _Comm(arrs, [jax.ShapeDtypeStruct(a.shape, a.dtype) for a in arrs], [_DMA((n,)), _DMA((n,))], start, finish)


def _pin(a):
    return pltpu.with_memory_space_constraint(a, pltpu.HBM)


def _plain_params(n_axes):
    return pltpu.CompilerParams(dimension_semantics=("arbitrary",) * n_axes, vmem_limit_bytes=V7X_VMEM_BYTES // 2)


def add_own_halves(names, parts, landed, core, name):
    n = len(names)

    def body(core_ref, *refs):
        for i in range(n):
            refs[2 * n + i][0] = (refs[i][...] + refs[n + i][0]).astype(MXU_DTYPE)

    in_specs, out_specs = [], []
    for nm in names:
        hr, hc = _HALF[nm]
        if nm in ("pa", "pb"):
            in_specs.append(pl.BlockSpec((hr // 2, hc), lambda k, s, cr: (cr[0] * 2 + s, k)))
        else:
            in_specs.append(pl.BlockSpec((hr // 2, hc), lambda k, s, cr: (k * 4 + cr[0] * 2 + s, 0)))
    for nm in names:
        hr, hc = _HALF[nm]
        in_specs.append(pl.BlockSpec((1, hr // 2, hc), lambda k, s, cr: (k, s, 0)))
        out_specs.append(pl.BlockSpec((1, hr // 2, hc), lambda k, s, cr: (k, s, 0)))
    return pl.pallas_call(
        body, name=name,
        grid_spec=pltpu.PrefetchScalarGridSpec(num_scalar_prefetch=1, grid=(N_CHIPS, 2), in_specs=in_specs, out_specs=out_specs),
        out_shape=[pltpu.HBM((N_CHIPS,) + _HALF[nm], MXU_DTYPE) for nm in names],
        compiler_params=_plain_params(2),
    )(core, *[_pin(a) for a in parts], *[_pin(a) for a in landed])


def add_small_half(small, landed, core):
    hs = SMALL_ROWS // 2

    def body(core_ref, a_ref, b_ref, o_ref):
        o_ref[...] = a_ref[...] + b_ref[...]

    return pl.pallas_call(
        body, name="add_small_half",
        grid_spec=pltpu.PrefetchScalarGridSpec(
            num_scalar_prefetch=1, grid=(1,),
            in_specs=[pl.BlockSpec((hs, 128), lambda s, cr: (cr[0], 0)), pl.BlockSpec((hs, 128), lambda s, cr: (0, 0))],
            out_specs=pl.BlockSpec((hs, 128), lambda s, cr: (0, 0))),
        out_shape=jax.ShapeDtypeStruct((hs, 128), F32),
    )(core, small, landed)


def sum_chips(sums, landed, chips, name, small_landed=None):
    n = len(sums)

    def body(chips_ref, *refs):
        ins, outs = refs[:4 * n + (small_landed is not None)], refs[4 * n + (small_landed is not None):]
        for i in range(n):
            tot = ins[4 * i][0].astype(F32)
            for r in range(1, N_CHIPS):
                tot = tot + ins[4 * i + r][0].astype(F32)
            outs[i][...] = tot
        if small_landed is not None:
            @pl.when(pl.program_id(0) == 0)
            def _():
                sm = ins[4 * n]
                outs[n][...] = ((sm[0] + sm[1]) + sm[2]) + sm[3]

    in_specs, out_specs, operands, shapes = [], [], [], []
    for a, l in zip(sums, landed):
        _, hr, hc = a.shape
        for r in range(N_CHIPS):
            in_specs.append(pl.BlockSpec((1, hr // 2, hc), functools.partial(lambda s, ch, r: (ch[r], s, 0), r=r)))
            operands.append(a if r == 0 else l)
        out_specs.append(pl.BlockSpec((hr // 2, hc), lambda s, ch: (s, 0)))
        shapes.append(jax.ShapeDtypeStruct((hr, hc), F32))
    if small_landed is not None:
        in_specs.append(pl.BlockSpec(small_landed.shape, lambda s, ch: (0, 0, 0)))
        out_specs.append(pl.BlockSpec(small_landed.shape[1:], lambda s, ch: (0, 0)))
        operands.append(small_landed)
        shapes.append(jax.ShapeDtypeStruct(small_landed.shape[1:], F32))
    return pl.pallas_call(
        body, name=name,
        grid_spec=pltpu.PrefetchScalarGridSpec(num_scalar_prefetch=1, grid=(2,), in_specs=in_specs, out_specs=out_specs),
        out_shape=[pltpu.HBM(s.shape, s.dtype) for s in shapes], compiler_params=_plain_params(1),
    )(chips, *[_pin(a) for a in operands])


N_STEPS = 8


def _adam_update(w, g, m, v):
    m = ADAM_B1 * m + (1.0 - ADAM_B1) * g
    v = ADAM_B2 * v + (1.0 - ADAM_B2) * (g * g)
    m_hat = m * (1.0 / (1.0 - ADAM_B1 ** ADAM_STEP))
    v_hat = v * (1.0 / (1.0 - ADAM_B2 ** ADAM_STEP))
    return -ADAM_LR * (m_hat / (jnp.sqrt(v_hat) + ADAM_EPS) + ADAM_WD * w), m, v


def adamw_big(ws, g_halves, ms, vs, core):
    n = len(ws)
    per_layer = N_STEPS // DEPTH
    per_half = per_layer // 2

    def body(core_ref, *refs):
        s = pl.program_id(0)
        first_layer = s < per_layer
        mine = ((s % per_layer) // per_half) == core_ref[0]
        for i in range(n):
            w, m, v = (refs[k * n + i][...] for k in range(3))
            g00, g01, g10, g11 = (refs[3 * n + 4 * i + k][...] for k in range(4))
            g = jnp.where(first_layer, jnp.where(mine, g00, g01), jnp.where(mine, g10, g11))[None]
            delta, m, v = _adam_update(w, g, m, v)
            for k, val in enumerate((g, delta, m, v)):
                refs[7 * n + k * n + i][...] = val

    whole = lambda a: pl.BlockSpec((1, a.shape[1] // per_layer, a.shape[2]),
                                   lambda s, cr: (s // per_layer, s % per_layer, 0))

    def half(a, layer, own):
        def index(s, cr):
            first = layer * per_layer + per_half * jnp.where(own, cr[0], 1 - cr[0])
            return (jnp.clip(s - first, 0, per_half - 1), 0)
        return pl.BlockSpec((a.shape[1] // per_layer, a.shape[2]), index)

    g_specs, g_ops = [], []
    for a, halves in zip(ws, g_halves):
        g_specs += [half(a, 0, True), half(a, 0, False), half(a, 1, True), half(a, 1, False)]
        g_ops += [_pin(a) for a in halves]
    outs = pl.pallas_call(
        body, name="adamw_big",
        grid_spec=pltpu.PrefetchScalarGridSpec(
            num_scalar_prefetch=1, grid=(N_STEPS,), in_specs=[whole(a) for a in ws] * 3 + g_specs,
            out_specs=[whole(a) for a in ws] * 4),
        out_shape=[pltpu.HBM(a.shape, F32) for a in ws] * 4,
        compiler_params=_layer_params(),
    )(core, *[_pin(a) for a in (*ws, *ms, *vs)], *g_ops)
    return outs[:n], outs[n:2 * n], outs[2 * n:3 * n], outs[3 * n:]


def adamw_small(ws, gs, ms, vs):
    n = len(ws)

    def body(*refs):
        for i in range(n):
            outs = _adam_update(*(refs[k * n + i][...] for k in range(4)))
            for k, val in enumerate(outs):
                refs[4 * n + k * n + i][...] = val

    specs = [_const_block(a.shape) for a in ws]
    outs = pl.pallas_call(
        body, name="adamw_small", grid=(1,), in_specs=specs * 4, out_specs=specs * 3,
        out_shape=[jax.ShapeDtypeStruct(a.shape, F32) for a in ws] * 3,
        compiler_params=_plain_params(1),
    )(*ws, *gs, *ms, *vs)
    return outs[:n], outs[n:2 * n], outs[2 * n:]


_SMALL = (("w_s", (DEPTH, SGU_G, BLK, BLK)), ("ln_in_g", (D_MODEL,)), ("ln_in_b", (D_MODEL,)), ("b_in", (DEPTH, N_COLS)),
          ("sinks", (DEPTH, 8)), ("vn_g", (DEPTH, SGU_W)), ("vn_b", (DEPTH, SGU_W)),
          ("b_s", (DEPTH, SGU_G, BLK)), ("b_out", (DEPTH, D_MODEL)), ("ln_g", (DEPTH, D_MODEL)), ("ln_b", (DEPTH, D_MODEL)))
_BIG = ("w_in", "p_a", "p_b", "w_out")


def _size(shape):
    n = 1
    for d in shape:
        n *= d
    return n


def _pack_small(vals, last_row=None):
    rows = []
    for name, shape in _SMALL:
        v = vals[name]
        if _size(shape) % 128:
            v = jnp.pad(v.reshape((1, -1)), ((0, 0), (0, (-_size(shape)) % 128)))
        rows.append(v.reshape((-1, 128)))
    used = sum(r.shape[0] for r in rows)
    tail = jnp.zeros((SMALL_ROWS - used, 128), F32)
    if last_row is not None:
        tail = tail.at[-1, 0].set(last_row)
    return jnp.concatenate([rows[0], jnp.concatenate(rows[1:] + [tail], axis=0)], axis=0)


def _unpack_small(packed):
    out, pos = {}, 0
    for name, shape in _SMALL:
        n = -(-_size(shape) // 128)
        rows = packed[pos:pos + n]
        out[name] = (rows.reshape((-1,))[:_size(shape)] if _size(shape) % 128 else rows).reshape(shape)
        pos += n
    return out


def kernel(x, ln_in_g, ln_in_b, w_in, b_in, sinks, vn_g, vn_b, w_s, b_s, p_a, p_b, w_out, b_out, ln_g, ln_b, loss_target, m_ln_in_g, m_ln_in_b, m_w_in, m_b_in, m_sinks, m_vn_g, m_vn_b, m_w_s, m_b_s, m_p_a, m_p_b, m_w_out, m_b_out, m_ln_g, m_ln_b, v_ln_in_g, v_ln_in_b, v_w_in, v_b_in, v_sinks, v_vn_g, v_vn_b, v_w_s, v_b_s, v_p_a, v_p_b, v_w_out, v_b_out, v_ln_g, v_ln_b):
    weights = dict(ln_in_g=ln_in_g, ln_in_b=ln_in_b, w_in=w_in, b_in=b_in, sinks=sinks, vn_g=vn_g, vn_b=vn_b, w_s=w_s,
                   b_s=b_s, p_a=p_a, p_b=p_b, w_out=w_out, b_out=b_out, ln_g=ln_g, ln_b=ln_b)
    mom1 = dict(ln_in_g=m_ln_in_g, ln_in_b=m_ln_in_b, w_in=m_w_in, b_in=m_b_in, sinks=m_sinks, vn_g=m_vn_g, vn_b=m_vn_b,
                w_s=m_w_s, b_s=m_b_s, p_a=m_p_a, p_b=m_p_b, w_out=m_w_out, b_out=m_b_out, ln_g=m_ln_g, ln_b=m_ln_b)
    mom2 = dict(ln_in_g=v_ln_in_g, ln_in_b=v_ln_in_b, w_in=v_w_in, b_in=v_b_in, sinks=v_sinks, vn_g=v_vn_g, vn_b=v_vn_b,
                w_s=v_w_s, b_s=v_b_s, p_a=v_p_a, p_b=v_p_b, w_out=v_w_out, b_out=v_b_out, ln_g=v_ln_g, ln_b=v_ln_b)
    n_seq, S, _ = x.shape
    T = n_seq * S
    c = lax.axis_index("c")
    j = 2 * lax.axis_index("x") + lax.axis_index("y")
    core = c.astype(jnp.int32).reshape((1,))
    chips = jnp.stack([j] + [r + (r >= j) for r in range(N_CHIPS - 1)]).astype(jnp.int32)
    names = list(_PIECES)
    xt, tt = x.reshape((T, D_MODEL)), loss_target.reshape((T, D_MODEL))

    tview = lambda d: dict(d, w_in=jnp.swapaxes(d["w_in"], 1, 2))
    weights_t, mom1_t, mom2_t = tview(weights), tview(mom1), tview(mom2)

    def own_shards(l):
        return [weights_t[n][l].astype(MXU_DTYPE) for n in _BIG]

    def layer_weights(l, gathered):
        d = dict(zip(_PIECES, gathered))
        d.update(paT=d["pa"].T, pbT=d["pb"].T, woutT=d["wout"].T)
        d.update(bin=b_in[l][None], sinks=sinks[l], vng=vn_g[l][None], vnb=vn_b[l][None], ws=w_s[l],
                 bs=jnp.repeat(b_s[l].T, 128, axis=1), bout=b_out[l][None], lng=ln_g[l][None], lnb=ln_b[l][None])
        return d

    own0, own1 = own_shards(0), own_shards(1)
    x0, g0 = input_norm_forward(xt, ln_in_g[None], ln_in_b[None], "ln_in_fwd", comm=gather_weights(names[:1], own0[:1]))
    n_proj = T // ROWS_PROJ
    h0, g1 = project(x0, g0[0], b_in[0][None], "layer0_proj",
                     comm=gather_weights(names[1:] + names[:1], own0[1:] + own1[:1], turn_step=(13 * n_proj) // 16))
    lw0 = layer_weights(0, g0 + g1[:len(names) - 1])
    x1, h1, g2 = layer_forward_and_projection(x0, h0, lw0, g1[-1], b_in[1][None], S, "layer0_fwd_layer1_proj",
                                              comm=gather_weights(names[1:], own1[1:], turn_step=T // ROWS // 2))
    lw1 = layer_weights(1, g1[-1:] + g2)

    dres1, dh1, acc1, _ = layer_backward(x1, h1, tt, lw1, S, True, "layer1_bwd")
    dx1, dwinT1, _ = input_grads(dres1, dh1, x1, lw1["winT"], "layer1_dx_dwin")
    parts1 = [dwinT1, acc1["dpa"], acc1["dpb"], acc1["dwout"]]
    dres0, dh0, acc0, landed1 = layer_backward(x0, h0, dx1, lw0, S, False, "layer0_bwd", comm=pair_send_halves(names, parts1))
    sums1 = add_own_halves(names, parts1, landed1, core, "add_own_halves1")
    parts0 = [None, acc0["dpa"], acc0["dpb"], acc0["dwout"]]
    dx0, parts0[0], riding = input_grads(dres0, dh0, x0, lw0["winT"], "layer0_dx_dwin",
                                         comm=_both(chip_exchange(sums1), pair_send_halves(names[1:], parts0[1:])))
    from_chips1, landed0_rest = riding[:len(names)], riding[len(names):]
    totals1 = sum_chips(sums1, from_chips1, chips, "sum_chips1")
    first = _run_comm(pair_send_halves(names[:1], parts0[:1], extra=totals1), "pair_send_halves0")
    landed0, sib_totals1 = first[:1] + landed0_rest, first[1:]
    sums0 = add_own_halves(names, parts0, landed0, core, "add_own_halves0")
    grad_x, d_ln_in_g, d_ln_in_b, from_chips0 = input_norm_backward(xt, dx0, ln_in_g, "ln_in_bwd", comm=chip_exchange(sums0))

    gl = [acc0, acc1]
    per_layer = lambda key, pick: jnp.stack([pick(gl[l][key]) for l in range(DEPTH)])
    row0 = lambda key: per_layer(key, lambda a: a[0])
    small_part = _pack_small(dict(
        ln_in_g=d_ln_in_g[0], ln_in_b=d_ln_in_b[0], b_in=row0("dbin"), sinks=per_layer("dsink", lambda a: a[:, 0]),
        vn_g=row0("dvng"), vn_b=row0("dvnb"), w_s=per_layer("dws", lambda a: a),
        b_s=per_layer("dbs", lambda a: a[:, ::128].T), b_out=row0("dbout"), ln_g=row0("dlng"), ln_b=row0("dlnb")),
        last_row=acc1["loss"][0, 0])
    small_landed, = _run_comm(pair_send_halves([], [], small_part), "pair_send_small")
    small_sum = add_small_half(small_part, small_landed, core)
    small_from_chips, = _run_comm(chip_exchange([], small_sum), "chip_exchange_small")
    *totals0, small_total = sum_chips(sums0, from_chips0, chips, "sum_chips0", small_landed=small_from_chips)
    *sib_totals0, sib_small = _run_comm(pair_send_totals(totals0 + [small_total]), "pair_send_totals0")

    def both_halves(mine, theirs):
        return jnp.where(c == 0, jnp.concatenate([mine, theirs], axis=0), jnp.concatenate([theirs, mine], axis=0))

    g_halves = [(totals0[pi], sib_totals0[pi], totals1[pi], sib_totals1[pi]) for pi in range(len(_BIG))]
    g_small = both_halves(small_total, sib_small)
    loss = g_small[-1, 0]

    big = adamw_big([weights_t[n] for n in _BIG], g_halves, [mom1_t[n] for n in _BIG], [mom2_t[n] for n in _BIG], core)
    small_names = [n for n, _ in _SMALL]
    as2d = lambda a: a.reshape((1, -1)) if a.ndim == 1 else a
    g_named = _unpack_small(g_small)
    small = adamw_small(*[[as2d(d[n]) for n in small_names] for d in (weights, g_named, mom1, mom2)])

    def named(big_outs, small_outs):
        d = dict(zip(_BIG, big_outs))
        d["w_in"] = jnp.swapaxes(d["w_in"], 1, 2)
        d.update({n: o.reshape(weights[n].shape) for n, o in zip(small_names, small_outs)})
        return d

    order = ("ln_in_g", "ln_in_b", "w_in", "b_in", "sinks", "vn_g", "vn_b", "w_s", "b_s", "p_a", "p_b", "w_out", "b_out",
             "ln_g", "ln_b")
    res = [loss, grad_x.reshape(x.shape)]
    groups = [named(big[0], [g_named[n] for n in small_names])] + [named(big[k + 1], small[k]) for k in range(3)]
    for group in groups:
        res.extend(group[n] for n in order)
    return tuple(res)
```
